```python
import math, functools
import jax, jax.numpy as jnp
from jax import lax
import numpy as np

D_MODEL = 1024
BATCH = 4
SEQ = 4096
DEPTH = 1
DEC_BATCH = 32
DEC_SEQ = 8
PAST_LEN = 16384
PAGE_SIZE = 128

MIX_WIDTH = D_MODEL
HEAD_DIM = 64
NSA_HEADS = 8
NSA_KV_HEADS = 2
NSA_GROUP = NSA_HEADS // NSA_KV_HEADS
NSA_WIDTH = NSA_HEADS * HEAD_DIM
KV_WIDTH = NSA_KV_HEADS * HEAD_DIM
CMP_BLOCK = 32
CMP_HIDDEN = 2 * HEAD_DIM
SEL_BLOCK = 64
TOP_N = 16
WINDOW = 512
Q_BLOCK = 128
N_BRANCH = 3
ATTN_SCALE = HEAD_DIM ** -0.5
MLSTM_HEADS = 4
MLSTM_WIDTH = MIX_WIDTH - NSA_WIDTH
MLSTM_DH = MLSTM_WIDTH // MLSTM_HEADS
MLSTM_CONV = 4
MLSTM_CHUNK = 64
D_FF = ((8 * D_MODEL // 3 + 127) // 128) * 128
FFN_CONV = 3
EPS = 1e-6
NEG_INF = -1e30
SEL_PRIORITY = 1e4
IN_SPLITS = (NSA_WIDTH, 2 * KV_WIDTH, 2 * KV_WIDTH, 2 * KV_WIDTH, NSA_HEADS * N_BRANCH,
             MLSTM_WIDTH, MLSTM_WIDTH, MLSTM_WIDTH, MLSTM_HEADS, MLSTM_HEADS)
N_IN = sum(IN_SPLITS)

kernel_name = 'hymba_mlstm_nsa_convffn_step'


def rmsnorm(x, g):
    xf = x.astype(jnp.float32)
    y = xf * lax.rsqrt(jnp.mean(xf * xf, axis=-1, keepdims=True) + EPS)
    return (y * g.astype(jnp.float32)).astype(x.dtype)


def causal_dwconv(x, buf, w):
    width, s = w.shape[0], x.shape[1]
    xx = jnp.concatenate([buf.astype(x.dtype), x], axis=1)
    y = xx[:, :s] * w[0]
    for j in range(1, width):
        y = y + xx[:, j:j + s] * w[j]
    return y, xx[:, s:]


def softmax_masked(scores, mask):
    return jax.nn.softmax(jnp.where(mask, scores.astype(jnp.float32), NEG_INF), axis=-1)


def mlstm_chunk_scan(q, k, v, ig, fg, C0, n0, m0):
    b, s, h, d = q.shape
    L = math.gcd(s, MLSTM_CHUNK)
    nc = s // L
    f32 = jnp.float32

    def chunks(a):
        a = a.astype(f32).reshape((b, nc, L) + a.shape[2:])
        return jnp.moveaxis(a, 1, 0)

    logf = jax.nn.log_sigmoid(fg.astype(f32))
    xs = (chunks(q), chunks(k.astype(f32) * d ** -0.5), chunks(v), chunks(ig), chunks(logf))
    tril = jnp.tril(jnp.ones((L, L), bool))[None, :, :, None]

    def step(carry, inp):
        C, n, m = carry
        qc, kc, vc, ic, lf = inp
        cum = jnp.cumsum(lf, axis=1)
        dmat = cum[:, :, None] - cum[:, None, :] + ic[:, None]
        dmat = jnp.where(tril, dmat, -jnp.inf)
        inter = cum + m[:, None]
        m_t = jnp.maximum(inter, jnp.max(dmat, axis=2))
        w = jnp.exp(dmat - m_t[:, :, None])
        sc = jnp.exp(inter - m_t)
        qk = jnp.einsum('bthd,bshd->btsh', qc, kc) * w
        num = jnp.einsum('btsh,bshe->bthe', qk, vc) + sc[..., None] * jnp.einsum('bhed,bthd->bthe', C, qc)
        den = jnp.sum(qk, axis=2) + sc * jnp.einsum('bhd,bthd->bth', n, qc)
        hc = num / jnp.maximum(jnp.abs(den), jnp.exp(-m_t))[..., None]
        m_new = m_t[:, -1]
        wl = jnp.exp(cum[:, -1:] - cum + ic - m_new[:, None])
        sl = jnp.exp(cum[:, -1] + m - m_new)
        C_new = sl[..., None, None] * C + jnp.einsum('bsh,bshe,bshd->bhed', wl, vc, kc)
        n_new = sl[..., None] * n + jnp.einsum('bsh,bshd->bhd', wl, kc)
        return (C_new, n_new, m_new), hc

    (C, n, m), hs = lax.scan(step, (C0.astype(f32), n0.astype(f32), m0.astype(f32)), xs)
    hs = jnp.moveaxis(hs, 0, 1).reshape(b, s, h, d)
    return hs, C, n, m


def mlstm_mixer(u, v, o, ig, fg, conv_buf, C0, n0, m0,
                w_conv, b_conv, w_q, w_k, b_ig, b_fg, g_head, skip):
    b, s, _ = u.shape
    uc, conv_new = causal_dwconv(u, conv_buf, w_conv)
    uc = jax.nn.silu(uc + b_conv)
    uh = uc.reshape(b, s, MLSTM_HEADS, MLSTM_DH)
    q = jnp.einsum('bshd,hde->bshe', uh, w_q)
    k = jnp.einsum('bshd,hde->bshe', uh, w_k)
    vh = v.reshape(b, s, MLSTM_HEADS, MLSTM_DH)
    hs, C, n, m = mlstm_chunk_scan(q, k, vh, ig + b_ig, fg + b_fg, C0, n0, m0)
    hn = rmsnorm(hs, g_head).reshape(b, s, MLSTM_WIDTH).astype(u.dtype)
    out = (hn + skip * uc) * jax.nn.sigmoid(o)
    return out, conv_new, C, n, m


def compress_blocks(rows, pe, w1, w2):
    b, l, h, d = rows.shape
    nb = l // CMP_BLOCK
    blk = rows[:, :nb * CMP_BLOCK].reshape(b, nb, CMP_BLOCK, h, d) + pe[:, None, :]
    blk = jnp.moveaxis(blk, 3, 2).reshape(b, nb, h, CMP_BLOCK * d)
    return jax.nn.silu(blk @ w1) @ w2


def cmp_branch(q, kc, vc, q_pos):
    nb = kc.shape[1]
    blk_end = (jnp.arange(nb) + 1) * CMP_BLOCK - 1
    avail = blk_end[None, :] <= q_pos[:, None]
    sc = jnp.einsum('btkgd,bnkd->btkgn', q, kc) * ATTN_SCALE
    p = softmax_masked(sc, avail[None, :, None, None, :])
    p = p * jnp.any(avail, axis=-1)[None, :, None, None, None]
    o = jnp.einsum('btkgn,bnkd->btkgd', p.astype(vc.dtype), vc)
    return o, jnp.sum(p, axis=3)


def select_blocks(imp, q_pos, n_sel):
    r = SEL_BLOCK // CMP_BLOCK
    imp = jnp.pad(imp, ((0, 0), (0, 0), (0, 0), (0, n_sel * r - imp.shape[-1])))
    score = imp.reshape(imp.shape[:3] + (n_sel, r)).sum(-1)
    j = jnp.arange(n_sel)[None, None, None, :]
    cur = (q_pos // SEL_BLOCK)[None, :, None, None]
    valid = j <= cur
    forced = (j == 0) | (j == cur) | (j == cur - 1)
    pri = jnp.where(valid, jnp.where(forced, SEL_PRIORITY, score), -SEL_PRIORITY)
    return lax.top_k(pri, min(TOP_N, n_sel))[1]


def sel_branch(q, kg, vg, idx, q_pos):
    b, t, h, kk, sb, d = kg.shape
    kpos = idx[..., None] * SEL_BLOCK + jnp.arange(SEL_BLOCK)
    mask = (kpos <= q_pos[None, :, None, None, None]).reshape(b, t, h, 1, kk * sb)
    sc = jnp.einsum('btkgd,btknsd->btkgns', q, kg) * ATTN_SCALE
    p = softmax_masked(sc.reshape(b, t, h, NSA_GROUP, kk * sb), mask)
    return jnp.einsum('btkgm,btkmd->btkgd', p.astype(vg.dtype), vg.reshape(b, t, h, kk * sb, d))


def win_branch(q, kw, vw, q_pos, k_pos):
    diff = q_pos[:, None] - k_pos[None, :]
    mask = (diff >= 0) & (diff < WINDOW) & (k_pos[None, :] >= 0)
    sc = jnp.einsum('btkgd,bnkd->btkgn', q, kw) * ATTN_SCALE
    p = softmax_masked(sc, mask[None, :, None, None, :])
    return jnp.einsum('btkgn,bnkd->btkgd', p.astype(vw.dtype), vw)


def nsa_prompt(q, kv_cmp, kv_sel, kv_win, pe, w1, w2):
    b, s = q.shape[:2]
    pos = jnp.arange(s)
    kc = compress_blocks(kv_cmp[:, :, 0], pe[0], w1[0], w2[0])
    vc = compress_blocks(kv_cmp[:, :, 1], pe[1], w1[1], w2[1])
    o_cmp, imp = cmp_branch(q, kc, vc, pos)
    n_sel = -(-s // SEL_BLOCK)
    idx = select_blocks(imp, pos, n_sel)
    sel_blk = jnp.pad(kv_sel, ((0, 0), (0, n_sel * SEL_BLOCK - s), (0, 0), (0, 0), (0, 0)))
    sel_blk = sel_blk.reshape(b, n_sel, SEL_BLOCK, 2, NSA_KV_HEADS, HEAD_DIM)
    win_pad = jnp.pad(kv_win, ((0, 0), (WINDOW, 0), (0, 0), (0, 0), (0, 0)))
    bi = jnp.arange(b)[:, None, None, None]
    hi = jnp.arange(NSA_KV_HEADS)[None, None, :, None]

    def q_block(start):
        qb = lax.dynamic_slice_in_dim(q, start, Q_BLOCK, axis=1)
        ib = lax.dynamic_slice_in_dim(idx, start, Q_BLOCK, axis=1)
        pb = start + jnp.arange(Q_BLOCK)
        g = sel_blk[bi, ib, :, :, hi]
        o_s = sel_branch(qb, g[..., 0, :], g[..., 1, :], ib, pb)
        wb = lax.dynamic_slice_in_dim(win_pad, start, WINDOW + Q_BLOCK, axis=1)
        kp = start - WINDOW + jnp.arange(WINDOW + Q_BLOCK)
        o_w = win_branch(qb, wb[:, :, 0], wb[:, :, 1], pb, kp)
        return o_s, o_w

    o_sel, o_win = lax.map(q_block, jnp.arange(0, s, Q_BLOCK))
    o_sel = jnp.moveaxis(o_sel, 0, 1).reshape(q.shape)
    o_win = jnp.moveaxis(o_win, 0, 1).reshape(q.shape)
    return o_cmp, o_sel, o_win, kv_cmp, kv_sel, kv_win[:, s - min(WINDOW, s):]


def nsa_sample(q, kv_cmp, kv_sel, kv_win, pool_cmp, pool_sel, win_buf, page_table, pe, w1, w2):
    b, s = q.shape[:2]
    past = page_table.shape[1] * PAGE_SIZE
    pos = past + jnp.arange(s)
    past_cmp = pool_cmp[page_table].reshape(b, past, 2, NSA_KV_HEADS, HEAD_DIM)
    kc = jnp.concatenate([compress_blocks(past_cmp[:, :, 0], pe[0], w1[0], w2[0]),
                          compress_blocks(kv_cmp[:, :, 0], pe[0], w1[0], w2[0])], axis=1)
    vc = jnp.concatenate([compress_blocks(past_cmp[:, :, 1], pe[1], w1[1], w2[1]),
                          compress_blocks(kv_cmp[:, :, 1], pe[1], w1[1], w2[1])], axis=1)
    o_cmp, imp = cmp_branch(q, kc, vc, pos)
    n_sel = -(-(past + s) // SEL_BLOCK)
    n_past_blk = past // SEL_BLOCK
    n_new_blk = n_sel - n_past_blk
    idx = select_blocks(imp, pos, n_sel)
    bi = jnp.arange(b)[:, None, None, None]
    hi = jnp.arange(NSA_KV_HEADS)[None, None, :, None]
    r2 = PAGE_SIZE // SEL_BLOCK
    pool_blk = pool_sel.reshape((-1, SEL_BLOCK) + pool_sel.shape[2:])
    jp = jnp.minimum(idx, n_past_blk - 1)
    phys = page_table[bi, jp // r2] * r2 + jp % r2
    g_past = pool_blk[phys, :, :, hi]
    new_blk = jnp.pad(kv_sel, ((0, 0), (0, n_new_blk * SEL_BLOCK - s), (0, 0), (0, 0), (0, 0)))
    new_blk = new_blk.reshape(b, n_new_blk, SEL_BLOCK, 2, NSA_KV_HEADS, HEAD_DIM)
    jn = jnp.clip(idx - n_past_blk, 0, n_new_blk - 1)
    g_new = new_blk[bi, jn, :, :, hi]
    g = jnp.where((idx < n_past_blk)[..., None, None, None], g_past.astype(g_new.dtype), g_new)
    o_sel = sel_branch(q, g[..., 0, :], g[..., 1, :], idx, pos)
    nb = win_buf.shape[1]
    w_rows = jnp.concatenate([win_buf.astype(kv_win.dtype), kv_win], axis=1)
    kp = past - nb + jnp.arange(nb + s)
    o_win = win_branch(q, w_rows[:, :, 0], w_rows[:, :, 1], pos, kp)
    return o_cmp, o_sel, o_win, kv_cmp, kv_sel, w_rows[:, s:]


def decoder_layer(x, nsa_fn, m_conv_buf, m_C, m_n, m_m, f_buf,
                  g_mix, w_in, w_out, w_mconv, b_mconv, w_mq, w_mk, b_ig, b_fg,
                  g_mhead, m_skip, g_nsa, g_ffn, w_up, w_fconv, w_down):
    b, s, _ = x.shape
    xn = rmsnorm(x, g_mix)
    q, kvc, kvs, kvw, gt, mu, mv, mo, mi, mf = jnp.split(
        xn @ w_in, np.cumsum(IN_SPLITS)[:-1].tolist(), axis=-1)
    kv_shape = (b, s, 2, NSA_KV_HEADS, HEAD_DIM)
    qh = q.reshape(b, s, NSA_KV_HEADS, NSA_GROUP, HEAD_DIM)
    o_cmp, o_sel, o_win, new_cmp, new_sel, new_win = nsa_fn(
        qh, kvc.reshape(kv_shape), kvs.reshape(kv_shape), kvw.reshape(kv_shape))
    gates = jax.nn.sigmoid(gt).reshape(b, s, NSA_KV_HEADS, NSA_GROUP, N_BRANCH, 1)
    o_nsa = gates[..., 0, :] * o_cmp + gates[..., 1, :] * o_sel + gates[..., 2, :] * o_win
    o_nsa = rmsnorm(o_nsa.reshape(b, s, NSA_WIDTH), g_nsa).astype(x.dtype)
    o_m, m_conv_new, C, n, m = mlstm_mixer(mu, mv, mo, mi, mf, m_conv_buf, m_C, m_n, m_m,
                                           w_mconv, b_mconv, w_mq, w_mk, b_ig, b_fg, g_mhead, m_skip)
    h = x + jnp.concatenate([o_m, o_nsa], axis=-1) @ w_out
    up, f_new = causal_dwconv(rmsnorm(h, g_ffn) @ w_up, f_buf, w_fconv)
    a, g = jnp.split(up, 2, axis=-1)
    y = h + (jax.nn.silu(g) * a) @ w_down
    return y, new_cmp, new_sel, new_win, m_conv_new, C, n, m, f_new


def setup_inputs(seed: int = 0) -> dict:
    key = jax.random.key(seed)
    keys = iter(jax.random.split(key, 40))
    f32 = jnp.float32

    def nrm(shape, scale):
        return jax.random.normal(next(keys), shape, f32) * scale

    def gain(shape):
        return 1.0 + nrm(shape, 0.05)

    n_pages = PAST_LEN // PAGE_SIZE
    used = DEC_BATCH * n_pages
    n_pool = used + max(1, used // 4)
    win_buf = min(WINDOW, PAST_LEN)
    page_table = jax.random.permutation(next(keys), n_pool)[:used].reshape(DEC_BATCH, n_pages).astype(jnp.int32)
    kv_row = (2, NSA_KV_HEADS, HEAD_DIM)
    return {
        'x_prompt': nrm((BATCH, SEQ, D_MODEL), 1.0),
        'x_sample': nrm((DEC_BATCH, DEC_SEQ, D_MODEL), 1.0),
        'cache_cmp': nrm((DEPTH, n_pool, PAGE_SIZE) + kv_row, 1.0),
        'cache_sel': nrm((DEPTH, n_pool, PAGE_SIZE) + kv_row, 1.0),
        'state_win': nrm((DEPTH, DEC_BATCH, win_buf) + kv_row, 1.0),
        'state_mlstm_C': nrm((DEPTH, DEC_BATCH, MLSTM_HEADS, MLSTM_DH, MLSTM_DH), 0.1),
        'state_mlstm_n': nrm((DEPTH, DEC_BATCH, MLSTM_HEADS, MLSTM_DH), 0.3),
        'state_mlstm_m': nrm((DEPTH, DEC_BATCH, MLSTM_HEADS), 0.5),
        'state_mlstm_conv': nrm((DEPTH, DEC_BATCH, MLSTM_CONV - 1, MLSTM_WIDTH), 1.0),
        'state_ffn_conv': nrm((DEPTH, DEC_BATCH, FFN_CONV - 1, 2 * D_FF), 1.0),
        'page_table': page_table,
        'g_mix': gain((DEPTH, D_MODEL)),
        'w_in': nrm((DEPTH, D_MODEL, N_IN), D_MODEL ** -0.5),
        'w_out': nrm((DEPTH, MIX_WIDTH, D_MODEL), MIX_WIDTH ** -0.5),
        'w_mconv': nrm((DEPTH, MLSTM_CONV, MLSTM_WIDTH), MLSTM_CONV ** -0.5),
        'b_mconv': nrm((DEPTH, MLSTM_WIDTH), 0.02),
        'w_mq': nrm((DEPTH, MLSTM_HEADS, MLSTM_DH, MLSTM_DH), MLSTM_DH ** -0.5),
        'w_mk': nrm((DEPTH, MLSTM_HEADS, MLSTM_DH, MLSTM_DH), MLSTM_DH ** -0.5),
        'b_ig': nrm((DEPTH, MLSTM_HEADS), 0.1),
        'b_fg': jnp.linspace(3.0, 6.0, MLSTM_HEADS, dtype=f32) + nrm((DEPTH, MLSTM_HEADS), 0.1),
        'g_mhead': gain((DEPTH, MLSTM_HEADS, MLSTM_DH)),
        'm_skip': gain((DEPTH, MLSTM_WIDTH)),
        'pe_cmp': nrm((DEPTH, 2, CMP_BLOCK, HEAD_DIM), 0.1),
        'w_cmp1': nrm((DEPTH, 2, CMP_BLOCK * HEAD_DIM, CMP_HIDDEN), (CMP_BLOCK * HEAD_DIM) ** -0.5),
        'w_cmp2': nrm((DEPTH, 2, CMP_HIDDEN, HEAD_DIM), CMP_HIDDEN ** -0.5),
        'g_nsa': gain((DEPTH, NSA_WIDTH)),
        'g_ffn': gain((DEPTH, D_MODEL)),
        'w_up': nrm((DEPTH, D_MODEL, 2 * D_FF), D_MODEL ** -0.5),
        'w_fconv': nrm((DEPTH, FFN_CONV, 2 * D_FF), FFN_CONV ** -0.5),
        'w_down': nrm((DEPTH, D_FF, D_MODEL), D_FF ** -0.5),
        'g_final': gain((D_MODEL,)),
    }


def reference(x_prompt, x_sample, cache_cmp, cache_sel, state_win, state_mlstm_C, state_mlstm_n,
              state_mlstm_m, state_mlstm_conv, state_ffn_conv, page_table,
              g_mix, w_in, w_out, w_mconv, b_mconv, w_mq, w_mk, b_ig, b_fg, g_mhead, m_skip,
              pe_cmp, w_cmp1, w_cmp2, g_nsa, g_ffn, w_up, w_fconv, w_down, g_final):
    yp, ys = x_prompt, x_sample
    bp, dt = x_prompt.shape[0], x_prompt.dtype
    per_p, per_s = [], []
    for l in range(DEPTH):
        wl = (g_mix[l], w_in[l], w_out[l], w_mconv[l], b_mconv[l], w_mq[l], w_mk[l], b_ig[l], b_fg[l],
              g_mhead[l], m_skip[l], g_nsa[l], g_ffn[l], w_up[l], w_fconv[l], w_down[l])
        nsa_p = functools.partial(nsa_prompt, pe=pe_cmp[l], w1=w_cmp1[l], w2=w_cmp2[l])
        nsa_s = functools.partial(nsa_sample, pool_cmp=cache_cmp[l], pool_sel=cache_sel[l],
                                  win_buf=state_win[l], page_table=page_table,
                                  pe=pe_cmp[l], w1=w_cmp1[l], w2=w_cmp2[l])
        yp, *st_p = decoder_layer(
            yp, nsa_p,
            jnp.zeros((bp, MLSTM_CONV - 1, MLSTM_WIDTH), dt),
            jnp.zeros((bp, MLSTM_HEADS, MLSTM_DH, MLSTM_DH), jnp.float32),
            jnp.zeros((bp, MLSTM_HEADS, MLSTM_DH), jnp.float32),
            jnp.zeros((bp, MLSTM_HEADS), jnp.float32),
            jnp.zeros((bp, FFN_CONV - 1, 2 * D_FF), dt), *wl)
        ys, *st_s = decoder_layer(
            ys, nsa_s, state_mlstm_conv[l], state_mlstm_C[l], state_mlstm_n[l], state_mlstm_m[l],
            state_ffn_conv[l], *wl)
        per_p.append(st_p)
        per_s.append(st_s)
    y_prompt = rmsnorm(yp, g_final)
    y_sample = rmsnorm(ys, g_final)
    cmp_p, sel_p, win_p, mconv_p, C_p, n_p, m_p, fconv_p = [jnp.stack(a) for a in zip(*per_p)]
    cmp_s, sel_s, win_s, mconv_s, C_s, n_s, m_s, fconv_s = [jnp.stack(a) for a in zip(*per_s)]
    return (y_prompt, y_sample, cmp_p, cmp_s, sel_p, sel_s, win_p, win_s,
            C_p, C_s, n_p, n_s, m_p, m_s, mconv_p, mconv_s, fconv_p, fconv_s)
```

```python
import functools

import numpy as np
import jax
import jax.numpy as jnp
from jax import lax
from jax.experimental import pallas as pl
from jax.experimental.pallas import tpu as pltpu

F32 = jnp.float32
BF16 = jnp.bfloat16

D_MODEL = 1024
PAGE_SIZE = 128
HEAD_DIM = 64
NSA_HEADS = 8
NSA_KV_HEADS = 2
NSA_GROUP = NSA_HEADS // NSA_KV_HEADS
NSA_WIDTH = NSA_HEADS * HEAD_DIM
KV_WIDTH = NSA_KV_HEADS * HEAD_DIM
CMP_BLOCK = 32
CMP_HIDDEN = 2 * HEAD_DIM
SEL_BLOCK = 64
TOP_N = 16
WINDOW = 512
N_BRANCH = 3
ATTN_SCALE = HEAD_DIM ** -0.5
MLSTM_HEADS = 4
MLSTM_WIDTH = D_MODEL - NSA_WIDTH
MLSTM_DH = MLSTM_WIDTH // MLSTM_HEADS
MLSTM_CONV = 4
D_FF = ((8 * D_MODEL // 3 + 127) // 128) * 128
FFN_CONV = 3
EPS = 1e-6
NEG_INF = -1e30
SEL_PRIORITY = 1e4

LANES = 128
SUBLANES = 8
VMEM_LIMIT = 48 * 1024 * 1024

GATE_COL_NSA = 0
GATE_COL_I = NSA_HEADS * N_BRANCH
GATE_COL_F = GATE_COL_I + MLSTM_HEADS

MLSTM_CHUNK = 128


def _cparams(sem):
    return pltpu.CompilerParams(dimension_semantics=sem, vmem_limit_bytes=VMEM_LIMIT)


def _dot(a, b):
    return jnp.dot(a, b, preferred_element_type=F32)


def _dot_nt(a, b):
    return lax.dot_general(a, b, (((1,), (1,)), ((), ())), preferred_element_type=F32)


def _sigmoid(x):
    return 1.0 / (1.0 + jnp.exp(-x))


def _silu(x):
    return x * _sigmoid(x)


def _rms(x, g):
    return x * lax.rsqrt(jnp.mean(x * x, axis=-1, keepdims=True) + EPS) * g


IN_OUT_WIDTHS = (NSA_WIDTH, 2 * KV_WIDTH, 2 * KV_WIDTH, 2 * KV_WIDTH,
                 MLSTM_WIDTH, MLSTM_WIDTH, MLSTM_WIDTH, LANES)


def _inproj_body(x_ref, g_ref, w_ref, *out_refs):
    xb = _rms(x_ref[...], g_ref[...]).astype(BF16)
    off = 0
    for ref in out_refs:
        n = ref.shape[-1]
        ref[...] = _dot(xb, w_ref[:, off:off + n])
        off += n


def _pack_w_in(w_in):
    splits = np.cumsum([NSA_WIDTH, 2 * KV_WIDTH, 2 * KV_WIDTH, 2 * KV_WIDTH, NSA_HEADS * N_BRANCH,
                        MLSTM_WIDTH, MLSTM_WIDTH, MLSTM_WIDTH, MLSTM_HEADS]).tolist()
    q, kvc, kvs, kvw, gt, mu, mv, mo, mi, mf = jnp.split(w_in, splits, axis=1)
    gates = jnp.concatenate([gt, mi, mf], axis=1)
    gates = jnp.pad(gates, ((0, 0), (0, LANES - gates.shape[1])))
    return jnp.concatenate([q, kvc, kvs, kvw, mu, mv, mo, gates], axis=1).astype(BF16)


def _in_proj(x2d, g_mix, w_packed, tm):
    t = x2d.shape[0]
    n_all = w_packed.shape[1]
    return pl.pallas_call(
        _inproj_body,
        grid=(t // tm,),
        in_specs=[pl.BlockSpec((tm, D_MODEL), lambda i: (i, 0)),
                  pl.BlockSpec((1, D_MODEL), lambda i: (0, 0)),
                  pl.BlockSpec((D_MODEL, n_all), lambda i: (0, 0))],
        out_specs=[pl.BlockSpec((tm, n), lambda i: (i, 0)) for n in IN_OUT_WIDTHS],
        out_shape=[jax.ShapeDtypeStruct((t, n), F32) for n in IN_OUT_WIDTHS],
        compiler_params=_cparams(("arbitrary",)),
        name="in_proj",
    )(x2d, g_mix.reshape(1, D_MODEL), w_packed)


def _mlstm_body(mu_ref, mv_ref, mo_ref, g_ref, cb_ref, c0_ref, n0_ref, m0_ref,
                wc_ref, bc_ref, wq_ref, wk_ref, gb_ref, gh_ref, sk_ref,
                o_ref, cn_ref, c_ref, n_ref, m_ref,
                xx_ref, vpad_ref, gpad_ref, *, valid):
    L = MLSTM_CHUNK
    DH = MLSTM_DH
    halo = SUBLANES
    c = pl.program_id(1)

    @pl.when(c == 0)
    def _():
        xx_ref[0:halo, :] = jnp.zeros((halo, MLSTM_WIDTH), F32)
        xx_ref[halo - (MLSTM_CONV - 1):halo, :] = cb_ref[0]
        c_ref[...] = c0_ref[...]
        n_ref[...] = n0_ref[...]
        m_ref[...] = m0_ref[...]

    if valid < L:
        xx_ref[halo:, :] = jnp.zeros((L, MLSTM_WIDTH), F32)
        vpad_ref[...] = jnp.zeros((L, MLSTM_WIDTH), F32)
        gpad_ref[...] = jnp.zeros((L, LANES), F32)
    xx_ref[halo:halo + valid, :] = mu_ref[...]
    vpad_ref[0:valid, :] = mv_ref[...]
    gpad_ref[0:valid, :] = g_ref[...]

    conv = xx_ref[halo - 3:halo - 3 + L, :] * wc_ref[0:1, :]
    for j in range(1, MLSTM_CONV):
        conv = conv + xx_ref[halo - 3 + j:halo - 3 + j + L, :] * wc_ref[j:j + 1, :]
    uc = _silu(conv + bc_ref[...])

    tail = xx_ref[valid + halo - 3:valid + halo, :]
    xx_ref[halo - 3:halo, :] = tail
    cn_ref[0] = tail

    gb = gpad_ref[...] + gb_ref[...]
    gbt = gb.T
    row = lax.broadcasted_iota(jnp.int32, (L, L), 0)
    col = lax.broadcasted_iota(jnp.int32, (L, L), 1)
    tril = row >= col
    triu = row <= col
    tok_col = lax.broadcasted_iota(jnp.int32, (L, 1), 0)
    tok_row = lax.broadcasted_iota(jnp.int32, (1, L), 1)

    def log_sigmoid(x):
        return jnp.minimum(x, 0.0) - jnp.log(1.0 + jnp.exp(-jnp.abs(x)))

    for h in range(MLSTM_HEADS):
        sl_h = slice(h * DH, (h + 1) * DH)
        u_h = uc[:, sl_h]
        ub = u_h.astype(BF16)
        q = _dot(ub, wq_ref[h])
        k = _dot(ub, wk_ref[h]) * (DH ** -0.5)
        v = vpad_ref[:, sl_h]
        qb, kb = q.astype(BF16), k.astype(BF16)

        ic_col = gb[:, GATE_COL_I + h:GATE_COL_I + h + 1]
        ic_row = gbt[GATE_COL_I + h:GATE_COL_I + h + 1, :]
        lf_col = log_sigmoid(gb[:, GATE_COL_F + h:GATE_COL_F + h + 1])
        lf_row = log_sigmoid(gbt[GATE_COL_F + h:GATE_COL_F + h + 1, :])
        if valid < L:
            ic_col = jnp.where(tok_col < valid, ic_col, NEG_INF)
            ic_row = jnp.where(tok_row < valid, ic_row, NEG_INF)
            lf_col = jnp.where(tok_col < valid, lf_col, 0.0)
            lf_row = jnp.where(tok_row < valid, lf_row, 0.0)

        cum_col = jnp.sum(jnp.where(tril, lf_row, 0.0), axis=1, keepdims=True)
        cum_row = jnp.sum(jnp.where(triu, lf_col, 0.0), axis=0, keepdims=True)
        m0 = m_ref[0, 0:1, h:h + 1]
        dmat = jnp.where(tril, cum_col - cum_row + ic_row, NEG_INF)
        inter = cum_col + m0
        m_t = jnp.maximum(inter, jnp.max(dmat, axis=1, keepdims=True))
        w = jnp.exp(dmat - m_t)
        sc = jnp.exp(inter - m_t)
        s = _dot_nt(qb, kb) * w
        c_old = c_ref[0, h]
        n_old = n_ref[0, h:h + 1, :]
        num = _dot(s.astype(BF16), v.astype(BF16)) + sc * _dot_nt(qb, c_old.astype(BF16))
        den = jnp.sum(s, axis=1, keepdims=True) + sc * jnp.sum(q * n_old, axis=1, keepdims=True)
        hc = num / jnp.maximum(jnp.abs(den), jnp.exp(-m_t))

        m_new = m_t[L - 1:L, :]
        cum_last = cum_col[L - 1:L, :]
        wl = jnp.exp(cum_last - cum_col + ic_col - m_new)
        sl = jnp.exp(cum_last + m0 - m_new)
        vw_t = (v * wl).T.astype(BF16)
        c_ref[0, h] = sl * c_old + _dot(vw_t, kb)
        n_ref[0, h:h + 1, :] = sl * n_old + jnp.sum(wl * k, axis=0, keepdims=True)
        m_ref[0, 0:1, h:h + 1] = m_new

        hn = _rms(hc, gh_ref[:, sl_h])
        out = (hn[0:valid, :] + sk_ref[:, sl_h] * u_h[0:valid, :]) * _sigmoid(mo_ref[:, sl_h])
        o_ref[:, sl_h] = out


def _mlstm(mu, mv, mo, gates, conv_buf, c0, n0, m0, w_mconv, b_mconv, w_mq, w_mk, b_ig, b_fg,
           g_mhead, m_skip, *, batch, seq):
    L = MLSTM_CHUNK
    valid = min(seq, L)
    assert seq % valid == 0 and (valid == L or seq == valid)
    nc = seq // valid
    gate_bias = jnp.zeros((1, LANES), F32)
    gate_bias = gate_bias.at[0, GATE_COL_I:GATE_COL_I + MLSTM_HEADS].set(b_ig)
    gate_bias = gate_bias.at[0, GATE_COL_F:GATE_COL_F + MLSTM_HEADS].set(b_fg)
    tok = lambda b, c: (b * nc + c, 0)
    const2 = lambda b, c: (0, 0)
    const3 = lambda b, c: (0, 0, 0)
    per_b3 = lambda b, c: (b, 0, 0)
    per_b4 = lambda b, c: (b, 0, 0, 0)
    H, DH, W = MLSTM_HEADS, MLSTM_DH, MLSTM_WIDTH
    return pl.pallas_call(
        functools.partial(_mlstm_body, valid=valid),
        grid=(batch, nc),
        in_specs=[pl.BlockSpec((valid, W), tok), pl.BlockSpec((valid, W), tok),
                  pl.BlockSpec((valid, W), tok), pl.BlockSpec((valid, LANES), tok),
                  pl.BlockSpec((1, MLSTM_CONV - 1, W), per_b3),
                  pl.BlockSpec((1, H, DH, DH), per_b4),
                  pl.BlockSpec((1, H, DH), per_b3),
                  pl.BlockSpec((1, 1, H), per_b3),
                  pl.BlockSpec((MLSTM_CONV, W), const2), pl.BlockSpec((1, W), const2),
                  pl.BlockSpec((H, DH, DH), const3), pl.BlockSpec((H, DH, DH), const3),
                  pl.BlockSpec((1, LANES), const2), pl.BlockSpec((1, W), const2),
                  pl.BlockSpec((1, W), const2)],
        out_specs=[pl.BlockSpec((valid, W), tok),
                   pl.BlockSpec((1, MLSTM_CONV - 1, W), per_b3),
                   pl.BlockSpec((1, H, DH, DH), per_b4),
                   pl.BlockSpec((1, H, DH), per_b3),
                   pl.BlockSpec((1, 1, H), per_b3)],
        out_shape=[jax.ShapeDtypeStruct((batch * seq, W), F32),
                   jax.ShapeDtypeStruct((batch, MLSTM_CONV - 1, W), F32),
                   jax.ShapeDtypeStruct((batch, H, DH, DH), F32),
                   jax.ShapeDtypeStruct((batch, H, DH), F32),
                   jax.ShapeDtypeStruct((batch, 1, H), F32)],
        scratch_shapes=[pltpu.VMEM((SUBLANES + L, W), F32), pltpu.VMEM((L, W), F32),
                        pltpu.VMEM((L, LANES), F32)],
        compiler_params=_cparams(("arbitrary", "arbitrary")),
        name="mlstm",
    )(mu, mv, mo, gates, conv_buf, c0, n0, m0.reshape(batch, 1, H),
      w_mconv, b_mconv.reshape(1, W), w_mq.astype(BF16), w_mk.astype(BF16), gate_bias,
      g_mhead.reshape(1, W), m_skip.reshape(1, W))


def _compress_rows(x_ref, pe_ref, w1_ref, w2_ref, n_pairs):
    pair_rows = 2 * 2 * CMP_BLOCK
    outs = []
    for kv in range(2):
        acc = jnp.zeros((2 * n_pairs, NSA_KV_HEADS * CMP_HIDDEN), F32)
        for r in range(CMP_BLOCK):
            ev = x_ref[pl.ds(2 * r + kv, n_pairs, stride=pair_rows), :]
            od = x_ref[pl.ds(2 * CMP_BLOCK + 2 * r + kv, n_pairs, stride=pair_rows), :]
            xr = jnp.concatenate([ev, od], axis=0) + pe_ref[kv, r:r + 1, :]
            acc = acc + _dot(xr.astype(BF16), w1_ref[kv, r])
        outs.append(_dot(_silu(acc).astype(BF16), w2_ref[kv]))
    return jnp.concatenate(outs, axis=1)


def _compress_body(x_ref, pe_ref, w1_ref, w2_ref, oe_ref, oo_ref, *, n_pairs):
    out = _compress_rows(x_ref, pe_ref, w1_ref, w2_ref, n_pairs)
    oe_ref[0] = out[0:n_pairs, :]
    oo_ref[0] = out[n_pairs:, :]


def _compress_paged_body(pt_ref, *refs, n_pages):
    page_refs = refs[:n_pages]
    pe_ref, w1_ref, w2_ref, oe_ref, oo_ref, buf_ref = refs[n_pages:]
    for j in range(n_pages):
        buf_ref[j * 2 * PAGE_SIZE:(j + 1) * 2 * PAGE_SIZE, :] = page_refs[j][0]
    n_pairs = n_pages * PAGE_SIZE // (2 * CMP_BLOCK)
    out = _compress_rows(buf_ref, pe_ref, w1_ref, w2_ref, n_pairs)
    oe_ref[0] = out[0:n_pairs, :]
    oo_ref[0] = out[n_pairs:, :]


def _pack_compress_weights(pe, w1, w2):
    eye_h = jnp.eye(NSA_KV_HEADS, dtype=F32)
    pe_r = jnp.broadcast_to(pe[:, :, None, :], (2, CMP_BLOCK, NSA_KV_HEADS, HEAD_DIM))
    pe_r = pe_r.reshape(2, CMP_BLOCK, KV_WIDTH)
    w1r = w1.reshape(2, CMP_BLOCK, HEAD_DIM, CMP_HIDDEN)
    w1_big = jnp.einsum('krdc,hH->krhdHc', w1r, eye_h)
    w1_big = w1_big.reshape(2, CMP_BLOCK, KV_WIDTH, NSA_KV_HEADS * CMP_HIDDEN).astype(BF16)
    w2_big = jnp.einsum('kcd,hH->khcHd', w2, eye_h)
    w2_big = w2_big.reshape(2, NSA_KV_HEADS * CMP_HIDDEN, KV_WIDTH).astype(BF16)
    return pe_r, w1_big, w2_big


def _compress_prompt(kvc2d, cw, *, batch, seq):
    pe_r, w1_big, w2_big = cw
    n_pairs = seq // (2 * CMP_BLOCK)
    const3 = lambda b: (0, 0, 0)
    out_sd = jax.ShapeDtypeStruct((batch, n_pairs, 2 * KV_WIDTH), F32)
    return pl.pallas_call(
        functools.partial(_compress_body, n_pairs=n_pairs),
        grid=(batch,),
        in_specs=[pl.BlockSpec((2 * seq, KV_WIDTH), lambda b: (b, 0)),
                  pl.BlockSpec(pe_r.shape, const3),
                  pl.BlockSpec(w1_big.shape, lambda b: (0, 0, 0, 0)),
                  pl.BlockSpec(w2_big.shape, const3)],
        out_specs=[pl.BlockSpec((1, n_pairs, 2 * KV_WIDTH), lambda b: (b, 0, 0))] * 2,
        out_shape=[out_sd, out_sd],
        compiler_params=_cparams(("arbitrary",)),
        name="compress_prompt",
    )(kvc2d.reshape(2 * batch * seq, KV_WIDTH), pe_r, w1_big, w2_big)


COMPRESS_PAGES_PER_STEP = 32


def _compress_paged(pool, page_table, cw):
    pe_r, w1_big, w2_big = cw
    batch, n_pages = page_table.shape
    pps = COMPRESS_PAGES_PER_STEP
    assert n_pages % pps == 0
    n_steps = n_pages // pps
    n_pairs = pps * PAGE_SIZE // (2 * CMP_BLOCK)
    const3 = lambda b, c, pt: (0, 0, 0)

    def page_spec(j):
        return pl.BlockSpec((1, 2 * PAGE_SIZE, KV_WIDTH),
                            lambda b, c, pt: (pt[(b * n_steps + c) * pps + j], 0, 0))

    out_sd = jax.ShapeDtypeStruct((batch, n_steps * n_pairs, 2 * KV_WIDTH), F32)
    return pl.pallas_call(
        functools.partial(_compress_paged_body, n_pages=pps),
        grid_spec=pltpu.PrefetchScalarGridSpec(
            num_scalar_prefetch=1,
            grid=(batch, n_steps),
            in_specs=[page_spec(j) for j in range(pps)] + [
                pl.BlockSpec(pe_r.shape, const3),
                pl.BlockSpec(w1_big.shape, lambda b, c, pt: (0, 0, 0, 0)),
                pl.BlockSpec(w2_big.shape, const3)],
            out_specs=[pl.BlockSpec((1, n_pairs, 2 * KV_WIDTH), lambda b, c, pt: (b, c, 0))] * 2,
            scratch_shapes=[pltpu.VMEM((pps * 2 * PAGE_SIZE, KV_WIDTH), F32)]),
        out_shape=[out_sd, out_sd],
        compiler_params=_cparams(("arbitrary", "arbitrary")),
        name="compress_paged",
    )(page_table.reshape(-1), *([pool] * pps), pe_r, w1_big, w2_big)


def _cmp_select_body(q_ref, ke_ref, ko_ref, o_ref, sb_ref, *, tq, pos0, n_sel):
    ns = ke_ref.shape[1]
    nsw = sb_ref.shape[-1]
    i = pl.program_id(1)
    pos = pos0 + i * tq + lax.broadcasted_iota(jnp.int32, (tq, 1), 0)
    pos4 = jnp.concatenate([pos] * NSA_GROUP, axis=0)
    pair = lax.broadcasted_iota(jnp.int32, (1, ns), 1)
    avail_e = (2 * pair + 1) * CMP_BLOCK - 1 <= pos4
    avail_o = (2 * pair + 2) * CMP_BLOCK - 1 <= pos4
    any_avail = (CMP_BLOCK - 1 <= pos4).astype(F32)
    q = q_ref[...] * ATTN_SCALE
    lane = lax.broadcasted_iota(jnp.int32, (1, nsw), 1)
    lane_f = lane.astype(F32)
    cur = pos // SEL_BLOCK
    forced = (lane == 0) | (lane == cur) | (lane == cur - 1)
    for kh in range(NSA_KV_HEADS):
        qs = jnp.concatenate([q[:, (kh * NSA_GROUP + g) * HEAD_DIM:(kh * NSA_GROUP + g + 1) * HEAD_DIM]
                              for g in range(NSA_GROUP)], axis=0).astype(BF16)
        ks, vs = slice(kh * HEAD_DIM, (kh + 1) * HEAD_DIM), slice(KV_WIDTH + kh * HEAD_DIM,
                                                                   KV_WIDTH + (kh + 1) * HEAD_DIM)
        se = jnp.where(avail_e, _dot_nt(qs, ke_ref[0, :, ks].astype(BF16)), NEG_INF)
        so = jnp.where(avail_o, _dot_nt(qs, ko_ref[0, :, ks].astype(BF16)), NEG_INF)
        mx = jnp.maximum(jnp.max(se, axis=1, keepdims=True), jnp.max(so, axis=1, keepdims=True))
        pe, po = jnp.exp(se - mx), jnp.exp(so - mx)
        inv = any_avail / (jnp.sum(pe, axis=1, keepdims=True) + jnp.sum(po, axis=1, keepdims=True))
        pe, po = pe * inv, po * inv
        oh = (_dot(pe.astype(BF16), ke_ref[0, :, vs].astype(BF16))
              + _dot(po.astype(BF16), ko_ref[0, :, vs].astype(BF16)))
        for g in range(NSA_GROUP):
            hd = kh * NSA_GROUP + g
            o_ref[:, hd * HEAD_DIM:(hd + 1) * HEAD_DIM] = oh[g * tq:(g + 1) * tq, :]
        psum = pe + po
        score = psum[0:tq, :]
        for g in range(1, NSA_GROUP):
            score = score + psum[g * tq:(g + 1) * tq, :]
        if nsw > ns:
            score = jnp.concatenate([score, jnp.zeros((tq, nsw - ns), F32)], axis=1)
        pri = jnp.where(lane <= cur, jnp.where(forced, SEL_PRIORITY, score), -SEL_PRIORITY)
        pri = jnp.where(lane < n_sel, pri, -jnp.inf)
        bias = jnp.full((tq, nsw), NEG_INF, F32)
        for _ in range(min(TOP_N, n_sel)):
            top = jnp.max(pri, axis=1, keepdims=True)
            first = jnp.min(jnp.where(pri == top, lane_f, float(nsw)), axis=1, keepdims=True)
            hit = lane_f == first
            bias = jnp.where(hit, 0.0, bias)
            pri = jnp.where(hit, -jnp.inf, pri)
        sb_ref[0, kh] = bias


def _cmp_select(q2d, kce, kco, *, batch, seq, tq, pos0, n_sel):
    ns = kce.shape[1]
    nsw = ns if n_sel <= ns else ns + LANES
    nq = seq // tq
    return pl.pallas_call(
        functools.partial(_cmp_select_body, tq=tq, pos0=pos0, n_sel=n_sel),
        grid=(batch, nq),
        in_specs=[pl.BlockSpec((tq, NSA_WIDTH), lambda b, i: (b * nq + i, 0)),
                  pl.BlockSpec((1, ns, 2 * KV_WIDTH), lambda b, i: (b, 0, 0)),
                  pl.BlockSpec((1, ns, 2 * KV_WIDTH), lambda b, i: (b, 0, 0))],
        out_specs=[pl.BlockSpec((tq, NSA_WIDTH), lambda b, i: (b * nq + i, 0)),
                   pl.BlockSpec((1, NSA_KV_HEADS, tq, nsw), lambda b, i: (b, 0, i, 0))],
        out_shape=[jax.ShapeDtypeStruct((batch * seq, NSA_WIDTH), F32),
                   jax.ShapeDtypeStruct((batch, NSA_KV_HEADS, seq, nsw), F32)],
        compiler_params=_cparams(("arbitrary", "arbitrary")),
        name="cmp_select",
    )(q2d, kce, kco)


def _softmax_update(sc, v_bf16, m_ref, l_ref, acc_ref, idx):
    m_old = m_ref[idx]
    m_new = jnp.maximum(m_old, jnp.max(sc, axis=1, keepdims=True))
    alpha = jnp.exp(m_old - m_new)
    pr = jnp.exp(sc - m_new)
    l_ref[idx] = alpha * l_ref[idx] + jnp.sum(pr, axis=1, keepdims=True)
    acc_ref[idx] = alpha * acc_ref[idx] + _dot(pr.astype(BF16), v_bf16)
    m_ref[idx] = m_new


def _attn_pairs(seq, tq, tk, window):
    rows = []
    for i in range(seq // tq):
        t_lo, t_hi = i * tq, i * tq + tq - 1
        k_lo = 0 if window is None else max(0, t_lo - window + 1)
        js = list(range(k_lo // tk, t_hi // tk + 1))
        for n, j in enumerate(js):
            rows.append((i, j, int(n == 0), int(n == len(js) - 1)))
    return np.asarray(rows, np.int32)


def _attn_body(tab_ref, q_ref, kv_ref, *rest, tq, tk, window, use_bias):
    if use_bias:
        sb_ref, o_ref, qa_ref, m_ref, l_ref, acc_ref = rest
    else:
        o_ref, qa_ref, m_ref, l_ref, acc_ref = rest
    p = pl.program_id(1)
    i, j = tab_ref[4 * p], tab_ref[4 * p + 1]
    first, last = tab_ref[4 * p + 2], tab_ref[4 * p + 3]
    G = NSA_GROUP

    @pl.when(first == 1)
    def _():
        q = q_ref[0] * ATTN_SCALE
        for kh in range(NSA_KV_HEADS):
            for g in range(G):
                hd = kh * G + g
                piece = q[:, hd * HEAD_DIM:(hd + 1) * HEAD_DIM]
                if use_bias:
                    piece = jnp.concatenate([piece, sb_ref[0, kh]], axis=1)
                qa_ref[kh, g * tq:(g + 1) * tq, :] = piece.astype(BF16)
        m_ref[...] = jnp.full(m_ref.shape, NEG_INF, F32)
        l_ref[...] = jnp.zeros(l_ref.shape, F32)
        acc_ref[...] = jnp.zeros(acc_ref.shape, F32)

    qpos = i * tq + (lax.broadcasted_iota(jnp.int32, (G * tq, 1), 0) & (tq - 1))
    kpos = j * tk + lax.broadcasted_iota(jnp.int32, (1, tk), 1)
    valid = kpos <= qpos
    if window is not None:
        valid = valid & (kpos > qpos - window)
    for kh in range(NSA_KV_HEADS):
        kt = kv_ref[0, :, kh * HEAD_DIM:(kh + 1) * HEAD_DIM]
        if use_bias:
            nsw = sb_ref.shape[-1]
            blk = (j * tk + lax.broadcasted_iota(jnp.int32, (tk, 1), 0)) // SEL_BLOCK
            onehot = (blk == lax.broadcasted_iota(jnp.int32, (1, nsw), 1)).astype(F32)
            kt = jnp.concatenate([kt, onehot], axis=1)
        sc = jnp.where(valid, _dot_nt(qa_ref[kh], kt.astype(BF16)), NEG_INF)
        vt = kv_ref[0, :, KV_WIDTH + kh * HEAD_DIM:KV_WIDTH + (kh + 1) * HEAD_DIM].astype(BF16)
        _softmax_update(sc, vt, m_ref, l_ref, acc_ref, kh)

    @pl.when(last == 1)
    def _():
        for kh in range(NSA_KV_HEADS):
            o = acc_ref[kh] / l_ref[kh]
            for g in range(G):
                hd = kh * G + g
                o_ref[0, :, hd * HEAD_DIM:(hd + 1) * HEAD_DIM] = o[g * tq:(g + 1) * tq, :]


def _attn_prompt(q3d, kv3d, selb, *, tq, tk, window):
    batch, seq, _ = q3d.shape
    assert tq & (tq - 1) == 0
    use_bias = selb is not None
    tab = _attn_pairs(seq, tq, tk, window)
    depth = HEAD_DIM + (selb.shape[-1] if use_bias else 0)
    rows = NSA_GROUP * tq
    in_specs = [pl.BlockSpec((1, tq, NSA_WIDTH), lambda b, p, t: (b, t[4 * p], 0)),
                pl.BlockSpec((1, tk, 2 * KV_WIDTH), lambda b, p, t: (b, t[4 * p + 1], 0))]
    args = [q3d, kv3d]
    if use_bias:
        in_specs.append(pl.BlockSpec((1, NSA_KV_HEADS, tq, selb.shape[-1]),
                                     lambda b, p, t: (b, 0, t[4 * p], 0)))
        args.append(selb)
    return pl.pallas_call(
        functools.partial(_attn_body, tq=tq, tk=tk, window=window, use_bias=use_bias),
        grid_spec=pltpu.PrefetchScalarGridSpec(
            num_scalar_prefetch=1,
            grid=(batch, tab.shape[0]),
            in_specs=in_specs,
            out_specs=pl.BlockSpec((1, tq, NSA_WIDTH), lambda b, p, t: (b, t[4 * p], 0)),
            scratch_shapes=[pltpu.VMEM((NSA_KV_HEADS, rows, depth), BF16),
                            pltpu.VMEM((NSA_KV_HEADS, rows, 1), F32),
                            pltpu.VMEM((NSA_KV_HEADS, rows, 1), F32),
                            pltpu.VMEM((NSA_KV_HEADS, rows, HEAD_DIM), F32)]),
        out_shape=jax.ShapeDtypeStruct((batch, seq, NSA_WIDTH), F32),
        compiler_params=_cparams(("arbitrary", "arbitrary")),
        name="attn_sel" if use_bias else "attn_win",
    )(jnp.asarray(tab.reshape(-1)), *args)


ATTN_PAGES_PER_STEP = 16


def _attn_paged_body(pt_ref, qa_ref, bq_ref, bn_ref, kn_ref, *rest, n_pages, n_new):
    page_refs = rest[:n_pages]
    o_ref, m_ref, l_ref, acc_ref = rest[n_pages:]
    c = pl.program_id(1)
    rows = qa_ref.shape[1]

    @pl.when(c == 0)
    def _():
        m_ref[...] = jnp.full(m_ref.shape, NEG_INF, F32)
        l_ref[...] = jnp.zeros(l_ref.shape, F32)
        acc_ref[...] = jnp.zeros(acc_ref.shape, F32)

    keys = n_pages * PAGE_SIZE
    k_all = jnp.concatenate([r[0, :, 0:KV_WIDTH] for r in page_refs], axis=0)
    v_all = jnp.concatenate([r[0, :, KV_WIDTH:] for r in page_refs], axis=0).astype(BF16)
    blk = lax.broadcasted_iota(jnp.int32, (keys, 1), 0) // SEL_BLOCK
    onehot = (blk == lax.broadcasted_iota(jnp.int32, (1, LANES), 1)).astype(F32)
    rhs = jnp.concatenate([k_all, onehot], axis=1).astype(BF16)
    qa = qa_ref[0]
    lhs = jnp.concatenate([qa, bq_ref[0, 0]], axis=1).astype(BF16)
    _softmax_update(_dot_nt(lhs, rhs), v_all, m_ref, l_ref, acc_ref, 0)

    @pl.when(c == pl.num_programs(1) - 1)
    def _():
        kn = kn_ref[0]
        sc = _dot_nt(qa.astype(BF16), kn[:, 0:KV_WIDTH].astype(BF16)) + bn_ref[0]
        tq = lax.broadcasted_iota(jnp.int32, (rows, 1), 0) % n_new
        kk = lax.broadcasted_iota(jnp.int32, (1, kn.shape[0]), 1)
        sc = jnp.where((kk <= tq) & (kk < n_new), sc, NEG_INF)
        _softmax_update(sc, kn[:, KV_WIDTH:].astype(BF16), m_ref, l_ref, acc_ref, 0)
        o_ref[0] = acc_ref[0] / l_ref[0]


def _attn_paged(qa, bias_q, bias_new, k_new, pool, page_table, *, n_new):
    batch, n_pages = page_table.shape
    pps = ATTN_PAGES_PER_STEP
    assert n_pages % pps == 0 and pps * PAGE_SIZE // SEL_BLOCK <= LANES
    n_steps = n_pages // pps
    rows = qa.shape[1]

    def page_spec(j):
        return pl.BlockSpec((1, PAGE_SIZE, 2 * KV_WIDTH),
                            lambda b, c, pt: (pt[(b * n_steps + c) * pps + j], 0, 0))

    per_b = lambda b, c, pt: (b, 0, 0)
    return pl.pallas_call(
        functools.partial(_attn_paged_body, n_pages=pps, n_new=n_new),
        grid_spec=pltpu.PrefetchScalarGridSpec(
            num_scalar_prefetch=1,
            grid=(batch, n_steps),
            in_specs=[pl.BlockSpec((1, rows, LANES), per_b),
                      pl.BlockSpec((1, 1, rows, LANES), lambda b, c, pt: (b, c, 0, 0)),
                      pl.BlockSpec((1, rows, LANES), per_b),
                      pl.BlockSpec((1,) + k_new.shape[1:], per_b)]
            + [page_spec(j) for j in range(pps)],
            out_specs=pl.BlockSpec((1, rows, LANES), per_b),
            scratch_shapes=[pltpu.VMEM((1, rows, 1), F32), pltpu.VMEM((1, rows, 1), F32),
                            pltpu.VMEM((1, rows, LANES), F32)]),
        out_shape=jax.ShapeDtypeStruct((batch, rows, LANES), F32),
        compiler_params=_cparams(("arbitrary", "arbitrary")),
        name="attn_sel_paged",
    )(page_table.reshape(-1), qa, bias_q, bias_new, k_new, *([pool] * pps))


def _attn_window_body(qa_ref, w_ref, o_ref, *, n_buf, n_new, past):
    kv = w_ref[0]
    rows, n_keys = qa_ref.shape[1], kv.shape[0]
    sc = _dot_nt(qa_ref[0].astype(BF16), kv[:, 0:KV_WIDTH].astype(BF16))
    n = lax.broadcasted_iota(jnp.int32, (1, n_keys), 1)
    qpos = past + lax.broadcasted_iota(jnp.int32, (rows, 1), 0) % n_new
    kpos = past - n_buf + n
    diff = qpos - kpos
    valid = (diff >= 0) & (diff < WINDOW) & (kpos >= 0) & (n < n_buf + n_new)
    sc = jnp.where(valid, sc, NEG_INF)
    pr = jnp.exp(sc - jnp.max(sc, axis=1, keepdims=True))
    o = _dot(pr.astype(BF16), kv[:, KV_WIDTH:].astype(BF16))
    o_ref[0] = o / jnp.sum(pr, axis=1, keepdims=True)


def _attn_window_small(qa, w_rows, *, n_buf, n_new, past):
    batch, rows, _ = qa.shape
    per_b = lambda b: (b, 0, 0)
    return pl.pallas_call(
        functools.partial(_attn_window_body, n_buf=n_buf, n_new=n_new, past=past),
        grid=(batch,),
        in_specs=[pl.BlockSpec((1, rows, LANES), per_b),
                  pl.BlockSpec((1,) + w_rows.shape[1:], per_b)],
        out_specs=pl.BlockSpec((1, rows, LANES), per_b),
        out_shape=jax.ShapeDtypeStruct((batch, rows, LANES), F32),
        compiler_params=_cparams(("arbitrary",)),
        name="attn_win_small",
    )(qa, w_rows)


FFN_TF = 256


def _ffn_body(x_ref, om_ref, oc_ref, os_ref, ow_ref, gt_ref, ge_ref, wo_ref, gn_ref, gf_ref, gl_ref,
              wua_ref, wug_ref, wca_ref, wcg_ref, wd_ref, fba_ref, fbg_ref,
              y_ref, fna_ref, fng_ref,
              h_ref, hn_ref, acc_ref, xa_ref, xg_ref, ca_ref, cg_ref, *, tm, stride, halo):
    s = pl.program_id(1)
    f = pl.program_id(2)

    @pl.when(f == 0)
    def _():
        sig = _sigmoid(gt_ref[...])
        hi = sig.astype(BF16)
        lo = (sig - hi.astype(F32)).astype(BF16)
        comb = None
        for br, ob_ref in enumerate((oc_ref, os_ref, ow_ref)):
            gate = _dot(hi, ge_ref[br]) + _dot(lo, ge_ref[br])
            term = gate * ob_ref[...]
            comb = term if comb is None else comb + term
        onsa = _rms(comb, gn_ref[...])
        h = (x_ref[...] + _dot(om_ref[...].astype(BF16), wo_ref[0:MLSTM_WIDTH, :])
             + _dot(onsa.astype(BF16), wo_ref[MLSTM_WIDTH:, :]))
        h_ref[...] = h
        hn_ref[...] = _rms(h, gf_ref[...]).astype(BF16)
        acc_ref[...] = jnp.zeros(acc_ref.shape, F32)

    hn = hn_ref[...]
    base = halo - (FFN_CONV - 1) * stride
    convs = []
    for w_ref, xx_ref, fb_ref, c_ref, wc_ref, fn_ref in (
            (wua_ref, xa_ref, fba_ref, ca_ref, wca_ref, fna_ref),
            (wug_ref, xg_ref, fbg_ref, cg_ref, wcg_ref, fng_ref)):
        @pl.when(s == 0)
        def _():
            xx_ref[base:halo, :] = fb_ref[0]

        @pl.when(s > 0)
        def _():
            xx_ref[0:halo, :] = c_ref[f]

        xx_ref[halo:halo + tm, :] = _dot(hn, w_ref[...])
        conv = xx_ref[base:base + tm, :] * wc_ref[0:1, :]
        for j in range(1, FFN_CONV):
            conv = conv + xx_ref[base + j * stride:base + j * stride + tm, :] * wc_ref[j:j + 1, :]
        convs.append(conv)
        c_ref[f] = xx_ref[tm:tm + halo, :]
        fn_ref[0, 0] = xx_ref[tm + base:tm + halo, :]
    act = _silu(convs[1]) * convs[0]
    acc_ref[...] += _dot(act.astype(BF16), wd_ref[...])

    @pl.when(f == pl.num_programs(2) - 1)
    def _():
        y_ref[...] = _rms(h_ref[...] + acc_ref[...], gl_ref[...])


def _gate_expand():
    ge = np.zeros((N_BRANCH, LANES, NSA_WIDTH), np.float32)
    for hd in range(NSA_HEADS):
        for br in range(N_BRANCH):
            ge[br, GATE_COL_NSA + hd * N_BRANCH + br, hd * HEAD_DIM:(hd + 1) * HEAD_DIM] = 1.0
    return jnp.asarray(ge, BF16)


def _ffn(x2d, om, oc, osel, ow, gt, fbuf, w_out, g_nsa, g_ffn, g_final, w_up, w_fconv, w_down,
         *, nb, tm, stride):
    rows = x2d.shape[0]
    ns = rows // (nb * tm)
    tf = FFN_TF
    nf = D_FF // tf
    halo = -(-(FFN_CONV - 1) * stride // SUBLANES) * SUBLANES
    assert tm >= halo and D_FF % tf == 0
    tok = lambda b, s, f: (b * ns + s, 0)
    const2 = lambda b, s, f: (0, 0)
    nfb = (FFN_CONV - 1) * stride
    w_up_b = w_up.astype(BF16)
    fn_sd = jax.ShapeDtypeStruct((nb, ns, nfb, D_FF), F32)
    y, fna, fng = pl.pallas_call(
        functools.partial(_ffn_body, tm=tm, stride=stride, halo=halo),
        grid=(nb, ns, nf),
        in_specs=[pl.BlockSpec((tm, D_MODEL), tok)] + [pl.BlockSpec((tm, NSA_WIDTH), tok)] * 4
        + [pl.BlockSpec((tm, LANES), tok),
           pl.BlockSpec((N_BRANCH, LANES, NSA_WIDTH), lambda b, s, f: (0, 0, 0)),
           pl.BlockSpec((D_MODEL, D_MODEL), const2),
           pl.BlockSpec((1, NSA_WIDTH), const2), pl.BlockSpec((1, D_MODEL), const2),
           pl.BlockSpec((1, D_MODEL), const2),
           pl.BlockSpec((D_MODEL, tf), lambda b, s, f: (0, f)),
           pl.BlockSpec((D_MODEL, tf), lambda b, s, f: (0, f + nf)),
           pl.BlockSpec((FFN_CONV, tf), lambda b, s, f: (0, f)),
           pl.BlockSpec((FFN_CONV, tf), lambda b, s, f: (0, f + nf)),
           pl.BlockSpec((tf, D_MODEL), lambda b, s, f: (f, 0)),
           pl.BlockSpec((1, nfb, tf), lambda b, s, f: (b, 0, f)),
           pl.BlockSpec((1, nfb, tf), lambda b, s, f: (b, 0, f + nf))],
        out_specs=[pl.BlockSpec((tm, D_MODEL), tok),
                   pl.BlockSpec((1, 1, nfb, tf), lambda b, s, f: (b, s, 0, f)),
                   pl.BlockSpec((1, 1, nfb, tf), lambda b, s, f: (b, s, 0, f))],
        out_shape=[jax.ShapeDtypeStruct((rows, D_MODEL), F32), fn_sd, fn_sd],
        scratch_shapes=[pltpu.VMEM((tm, D_MODEL), F32), pltpu.VMEM((tm, D_MODEL), BF16),
                        pltpu.VMEM((tm, D_MODEL), F32),
                        pltpu.VMEM((halo + tm, tf), F32), pltpu.VMEM((halo + tm, tf), F32),
                        pltpu.VMEM((nf, halo, tf), F32), pltpu.VMEM((nf, halo, tf), F32)],
        compiler_params=_cparams(("arbitrary", "arbitrary", "arbitrary")),
        name="outproj_ffn",
    )(x2d, om, oc, osel, ow, gt, _gate_expand(), w_out.astype(BF16), g_nsa.reshape(1, -1),
      g_ffn.reshape(1, -1), g_final.reshape(1, -1), w_up_b, w_up_b, w_fconv, w_fconv,
      w_down.astype(BF16), fbuf, fbuf)
    return y, fna[:, ns - 1], fng[:, ns - 1]


PROMPT_TM = 512
PROMPT_TQ_CMP = 256
PROMPT_TQ = 128
PROMPT_TK = 256


def _kv_rows(kv2d, batch, seq):
    return kv2d.reshape(batch, seq, 2, NSA_KV_HEADS, HEAD_DIM)


def _prompt_layer(x, wts):
    batch, seq, _ = x.shape
    x2d = x.reshape(batch * seq, D_MODEL)
    q, kvc, kvs, kvw, mu, mv, mo, gt = _in_proj(x2d, wts["g_mix"], wts["w_in_packed"], PROMPT_TM)
    H, DH, W = MLSTM_HEADS, MLSTM_DH, MLSTM_WIDTH
    o_m, mconv, c_new, n_new, m_new = _mlstm(
        mu, mv, mo, gt, jnp.zeros((batch, MLSTM_CONV - 1, W), F32), jnp.zeros((batch, H, DH, DH), F32),
        jnp.zeros((batch, H, DH), F32), jnp.zeros((batch, H), F32),
        wts["w_mconv"], wts["b_mconv"], wts["w_mq"], wts["w_mk"], wts["b_ig"], wts["b_fg"],
        wts["g_mhead"], wts["m_skip"], batch=batch, seq=seq)
    kce, kco = _compress_prompt(kvc, wts["cw"], batch=batch, seq=seq)
    n_sel = -(-seq // SEL_BLOCK)
    o_cmp, selb = _cmp_select(q, kce, kco, batch=batch, seq=seq, tq=min(PROMPT_TQ_CMP, seq), pos0=0,
                              n_sel=n_sel)
    q3d = q.reshape(batch, seq, NSA_WIDTH)
    o_sel = _attn_prompt(q3d, kvs.reshape(batch, seq, 2 * KV_WIDTH), selb,
                         tq=PROMPT_TQ, tk=PROMPT_TK, window=None)
    o_win = _attn_prompt(q3d, kvw.reshape(batch, seq, 2 * KV_WIDTH), None,
                         tq=PROMPT_TQ, tk=PROMPT_TK, window=WINDOW)
    fbuf = jnp.zeros((batch, FFN_CONV - 1, 2 * D_FF), F32)
    y, fna, fng = _ffn(x2d, o_m, o_cmp, o_sel.reshape(-1, NSA_WIDTH), o_win.reshape(-1, NSA_WIDTH), gt,
                       fbuf, wts["w_out"], wts["g_nsa"], wts["g_ffn"], wts["g_final"], wts["w_up"],
                       wts["w_fconv"], wts["w_down"], nb=batch, tm=min(PROMPT_TM, seq), stride=1)
    n_win = min(WINDOW, seq)
    return (y.reshape(batch, seq, D_MODEL), _kv_rows(kvc, batch, seq), _kv_rows(kvs, batch, seq),
            _kv_rows(kvw, batch, seq)[:, seq - n_win:], mconv, c_new, n_new, m_new.reshape(batch, H),
            jnp.concatenate([fna, fng], axis=-1))


def _decode_rows(q2d, batch, seq):
    q5 = (q2d * ATTN_SCALE).reshape(batch, seq, NSA_KV_HEADS, NSA_GROUP, HEAD_DIM).transpose(0, 2, 3, 1, 4)
    eye = jnp.eye(NSA_KV_HEADS, dtype=F32)
    qa = jnp.einsum('bkgtd,kK->bkgtKd', q5, eye)
    return qa.reshape(batch, NSA_KV_HEADS * NSA_GROUP * seq, KV_WIDTH)


def _decode_rows_out(o, batch, seq):
    o6 = o.reshape(batch, NSA_KV_HEADS, NSA_GROUP, seq, NSA_KV_HEADS, HEAD_DIM)
    o5 = jnp.stack([o6[:, kh, :, :, kh, :] for kh in range(NSA_KV_HEADS)], axis=1)
    return o5.transpose(0, 3, 1, 2, 4).reshape(batch * seq, NSA_WIDTH)


def _sample_layer(x, pool_cmp, pool_sel, win_buf, m_conv, m_c, m_n, m_m, f_buf, page_table, wts):
    batch, seq, _ = x.shape
    n_pages = page_table.shape[1]
    past = n_pages * PAGE_SIZE
    assert past % SEL_BLOCK == 0 and seq <= SEL_BLOCK and seq < CMP_BLOCK
    x2d = x.reshape(batch * seq, D_MODEL)
    q, kvc, kvs, kvw, mu, mv, mo, gt = _in_proj(x2d, wts["g_mix"], wts["w_in_packed"], batch * seq)
    H = MLSTM_HEADS
    o_m, mconv, c_new, n_new, m_new = _mlstm(
        mu, mv, mo, gt, m_conv, m_c, m_n, m_m,
        wts["w_mconv"], wts["b_mconv"], wts["w_mq"], wts["w_mk"], wts["b_ig"], wts["b_fg"],
        wts["g_mhead"], wts["m_skip"], batch=batch, seq=seq)
    pool_cmp3 = pool_cmp.reshape(pool_cmp.shape[0], 2 * PAGE_SIZE, KV_WIDTH)
    pool_sel3 = pool_sel.reshape(pool_sel.shape[0], PAGE_SIZE, 2 * KV_WIDTH)
    kce, kco = _compress_paged(pool_cmp3, page_table, wts["cw"])
    n_past_blk = past // SEL_BLOCK
    n_sel = -(-(past + seq) // SEL_BLOCK)
    o_cmp, selb = _cmp_select(q, kce, kco, batch=batch, seq=seq, tq=seq, pos0=past, n_sel=n_sel)
    qa = _decode_rows(q, batch, seq)
    rows = qa.shape[1]
    blk_per_step = ATTN_PAGES_PER_STEP * PAGE_SIZE // SEL_BLOCK
    n_steps = n_pages // ATTN_PAGES_PER_STEP
    sb_rows = jnp.broadcast_to(selb[:, :, None], (batch, NSA_KV_HEADS, NSA_GROUP, seq, selb.shape[-1]))
    sb_rows = sb_rows.reshape(batch, rows, selb.shape[-1])
    bias_q = sb_rows[:, :, :n_past_blk].reshape(batch, rows, n_steps, blk_per_step).transpose(0, 2, 1, 3)
    bias_q = jnp.pad(bias_q, ((0, 0), (0, 0), (0, 0), (0, LANES - blk_per_step)))
    bias_new = jnp.broadcast_to(sb_rows[:, :, n_past_blk:n_past_blk + 1], (batch, rows, LANES))
    kvs3 = kvs.reshape(batch, seq, 2 * KV_WIDTH)
    k_new = jnp.pad(kvs3, ((0, 0), (0, LANES - seq), (0, 0)))
    o_sel = _attn_paged(qa, bias_q, bias_new, k_new, pool_sel3, page_table, n_new=seq)
    n_buf = win_buf.shape[1]
    w_rows = jnp.concatenate([win_buf.reshape(batch, n_buf, 2 * KV_WIDTH),
                              kvw.reshape(batch, seq, 2 * KV_WIDTH)], axis=1)
    n_keys = -(-(n_buf + seq) // LANES) * LANES
    w_pad = jnp.pad(w_rows, ((0, 0), (0, n_keys - n_buf - seq), (0, 0)))
    o_win = _attn_window_small(qa, w_pad, n_buf=n_buf, n_new=seq, past=past)
    tmaj = lambda a: a.reshape(batch, seq, -1).transpose(1, 0, 2).reshape(batch * seq, -1)
    fb_t = f_buf.transpose(1, 0, 2).reshape(1, (FFN_CONV - 1) * batch, 2 * D_FF)
    y, fna, fng = _ffn(tmaj(x2d), tmaj(o_m), tmaj(o_cmp), tmaj(_decode_rows_out(o_sel, batch, seq)),
                       tmaj(_decode_rows_out(o_win, batch, seq)), tmaj(gt), fb_t,
                       wts["w_out"], wts["g_nsa"], wts["g_ffn"], wts["g_final"], wts["w_up"],
                       wts["w_fconv"], wts["w_down"], nb=1, tm=batch * seq, stride=batch)
    y = y.reshape(seq, batch, D_MODEL).transpose(1, 0, 2)
    f_new = jnp.concatenate([fna, fng], axis=-1).reshape(FFN_CONV - 1, batch, 2 * D_FF).transpose(1, 0, 2)
    return (y, _kv_rows(kvc, batch, seq), _kv_rows(kvs, batch, seq),
            w_rows[:, seq:].reshape(batch, n_buf, 2, NSA_KV_HEADS, HEAD_DIM), mconv, c_new, n_new,
            m_new.reshape(batch, H), f_new)


def kernel(x_prompt, x_sample, cache_cmp, cache_sel, state_win, state_mlstm_C, state_mlstm_n,
           state_mlstm_m, state_mlstm_conv, state_ffn_conv, page_table,
           g_mix, w_in, w_out, w_mconv, b_mconv, w_mq, w_mk, b_ig, b_fg, g_mhead, m_skip,
           pe_cmp, w_cmp1, w_cmp2, g_nsa, g_ffn, w_up, w_fconv, w_down, g_final):
    assert w_in.shape[0] == 1, "one layer: the final norm is fused into the layer's FFN kernel"
    l = 0
    wts = dict(g_mix=g_mix[l], w_in_packed=_pack_w_in(w_in[l]), w_out=w_out[l], w_mconv=w_mconv[l],
               b_mconv=b_mconv[l], w_mq=w_mq[l], w_mk=w_mk[l], b_ig=b_ig[l], b_fg=b_fg[l],
               g_mhead=g_mhead[l], m_skip=m_skip[l],
               cw=_pack_compress_weights(pe_cmp[l], w_cmp1[l], w_cmp2[l]),
               g_nsa=g_nsa[l], g_ffn=g_ffn[l], g_final=g_final, w_up=w_up[l], w_fconv=w_fconv[l],
               w_down=w_down[l])
    p = _prompt_layer(x_prompt, wts)
    s = _sample_layer(x_sample, cache_cmp[l], cache_sel[l], state_win[l], state_mlstm_conv[l],
                      state_mlstm_C[l], state_mlstm_n[l], state_mlstm_m[l], state_ffn_conv[l],
                      page_table, wts)
    yp, cmp_p, sel_p, win_p, mconv_p, c_p, n_p, m_p, fconv_p = p
    ys, cmp_s, sel_s, win_s, mconv_s, c_s, n_s, m_s, fconv_s = s
    st = lambda a: a[None]
    return (yp, ys, st(cmp_p), st(cmp_s), st(sel_p), st(sel_s), st(win_p), st(win_s),
            st(c_p), st(c_s), st(n_p), st(n_s), st(m_p), st(m_s), st(mconv_p), st(mconv_s),
            st(fconv_p), st(fconv_s))
```

```python
import functools

import numpy as np
import jax
import jax.numpy as jnp
from jax import lax
from jax.experimental import pallas as pl
from jax.experimental.pallas import tpu as pltpu

F32 = jnp.float32
BF16 = jnp.bfloat16

D_MODEL = 1024
PAGE_SIZE = 128
HEAD_DIM = 64
NSA_HEADS = 8
NSA_KV_HEADS = 2
NSA_GROUP = NSA_HEADS // NSA_KV_HEADS
NSA_WIDTH = NSA_HEADS * HEAD_DIM
KV_WIDTH = NSA_KV_HEADS * HEAD_DIM
CMP_BLOCK = 32
CMP_HIDDEN = 2 * HEAD_DIM
SEL_BLOCK = 64
TOP_N = 16
WINDOW = 512
N_BRANCH = 3
ATTN_SCALE = HEAD_DIM ** -0.5
MLSTM_HEADS = 4
MLSTM_WIDTH = D_MODEL - NSA_WIDTH
MLSTM_DH = MLSTM_WIDTH // MLSTM_HEADS
MLSTM_CONV = 4
D_FF = ((8 * D_MODEL // 3 + 127) // 128) * 128
FFN_CONV = 3
EPS = 1e-6
NEG_INF = -1e30
SEL_PRIORITY = 1e4

LANES = 128
SUBLANES = 8
VMEM_LIMIT = 48 * 1024 * 1024

GATE_COL_NSA = 0
GATE_COL_I = NSA_HEADS * N_BRANCH
GATE_COL_F = GATE_COL_I + MLSTM_HEADS

MLSTM_CHUNK = 128


def _cparams(sem):
    return pltpu.CompilerParams(dimension_semantics=sem, vmem_limit_bytes=VMEM_LIMIT)


def _dot(a, b):
    return jnp.dot(a, b, preferred_element_type=F32)


def _dot_nt(a, b):
    return lax.dot_general(a, b, (((1,), (1,)), ((), ())), preferred_element_type=F32)


def _sigmoid(x):
    return 1.0 / (1.0 + jnp.exp(-x))


def _silu(x):
    return x * _sigmoid(x)


def _rms(x, g):
    return x * lax.rsqrt(jnp.mean(x * x, axis=-1, keepdims=True) + EPS) * g


IN_ROW_WIDTHS = (NSA_WIDTH, KV_WIDTH, KV_WIDTH, MLSTM_WIDTH, MLSTM_WIDTH, MLSTM_WIDTH, LANES)
N_KV_BRANCH = 3


def _inproj_body(x_ref, g_ref, w_ref, wt_ref, *out_refs):
    xb = _rms(x_ref[...], g_ref[...]).astype(BF16)
    off = 0
    for ref in out_refs[:len(IN_ROW_WIDTHS)]:
        n = ref.shape[-1]
        ref[...] = _dot(xb, w_ref[:, off:off + n])
        off += n
    for n, ref in enumerate(out_refs[len(IN_ROW_WIDTHS):]):
        ref[0] = _dot_nt(wt_ref[n * 2 * KV_WIDTH:(n + 1) * 2 * KV_WIDTH, :], xb)


def _pack_w_in(w_in):
    splits = np.cumsum([NSA_WIDTH, 2 * KV_WIDTH, 2 * KV_WIDTH, 2 * KV_WIDTH, NSA_HEADS * N_BRANCH,
                        MLSTM_WIDTH, MLSTM_WIDTH, MLSTM_WIDTH, MLSTM_HEADS]).tolist()
    q, kvc, kvs, kvw, gt, mu, mv, mo, mi, mf = jnp.split(w_in, splits, axis=1)
    gates = jnp.concatenate([gt, mi, mf], axis=1)
    gates = jnp.pad(gates, ((0, 0), (0, LANES - gates.shape[1])))
    w_rows = jnp.concatenate([q, kvc, mu, mv, mo, gates], axis=1).astype(BF16)
    w_kv_t = jnp.concatenate([kvc, kvs, kvw], axis=1).T.astype(BF16)
    return w_rows, w_kv_t


def _in_proj(x2d, g_mix, w_packed, *, batch, seq, tm):
    w_rows, w_kv_t = w_packed
    t = x2d.shape[0]
    ns = seq // tm
    kv_sd = jax.ShapeDtypeStruct((batch, 2 * KV_WIDTH, seq), F32)
    return pl.pallas_call(
        _inproj_body,
        grid=(t // tm,),
        in_specs=[pl.BlockSpec((tm, D_MODEL), lambda i: (i, 0)),
                  pl.BlockSpec((1, D_MODEL), lambda i: (0, 0)),
                  pl.BlockSpec(w_rows.shape, lambda i: (0, 0)),
                  pl.BlockSpec(w_kv_t.shape, lambda i: (0, 0))],
        out_specs=[pl.BlockSpec((tm, n), lambda i: (i, 0)) for n in IN_ROW_WIDTHS]
        + [pl.BlockSpec((1, 2 * KV_WIDTH, tm), lambda i: (i // ns, 0, i % ns))] * N_KV_BRANCH,
        out_shape=[jax.ShapeDtypeStruct((t, n), F32) for n in IN_ROW_WIDTHS] + [kv_sd] * N_KV_BRANCH,
        compiler_params=_cparams(("arbitrary",)),
        name="in_proj",
    )(x2d, g_mix.reshape(1, D_MODEL), w_rows, w_kv_t)


def _mlstm_body(mu_ref, mv_ref, mo_ref, g_ref, cb_ref, c0_ref, n0_ref, m0_ref,
                wc_ref, bc_ref, wq_ref, wk_ref, gb_ref, gh_ref, sk_ref,
                o_ref, cn_ref, c_ref, n_ref, m_ref,
                xx_ref, vpad_ref, gpad_ref, *, valid):
    L = MLSTM_CHUNK
    DH = MLSTM_DH
    halo = SUBLANES
    c = pl.program_id(1)

    @pl.when(c == 0)
    def _():
        xx_ref[0:halo, :] = jnp.zeros((halo, MLSTM_WIDTH), F32)
        xx_ref[halo - (MLSTM_CONV - 1):halo, :] = cb_ref[0]
        c_ref[...] = c0_ref[...]
        n_ref[...] = n0_ref[...]
        m_ref[...] = m0_ref[...]

    if valid < L:
        xx_ref[halo:, :] = jnp.zeros((L, MLSTM_WIDTH), F32)
        vpad_ref[...] = jnp.zeros((L, MLSTM_WIDTH), F32)
        gpad_ref[...] = jnp.zeros((L, LANES), F32)
    xx_ref[halo:halo + valid, :] = mu_ref[...]
    vpad_ref[0:valid, :] = mv_ref[...]
    gpad_ref[0:valid, :] = g_ref[...]

    conv = xx_ref[halo - 3:halo - 3 + L, :] * wc_ref[0:1, :]
    for j in range(1, MLSTM_CONV):
        conv = conv + xx_ref[halo - 3 + j:halo - 3 + j + L, :] * wc_ref[j:j + 1, :]
    uc = _silu(conv + bc_ref[...])

    tail = xx_ref[valid + halo - 3:valid + halo, :]
    xx_ref[halo - 3:halo, :] = tail
    cn_ref[0] = tail

    gb = gpad_ref[...] + gb_ref[...]
    gbt = gb.T
    row = lax.broadcasted_iota(jnp.int32, (L, L), 0)
    col = lax.broadcasted_iota(jnp.int32, (L, L), 1)
    tril = row >= col
    triu = row <= col
    tok_col = lax.broadcasted_iota(jnp.int32, (L, 1), 0)
    tok_row = lax.broadcasted_iota(jnp.int32, (1, L), 1)

    def log_sigmoid(x):
        return jnp.minimum(x, 0.0) - jnp.log(1.0 + jnp.exp(-jnp.abs(x)))

    for h in range(MLSTM_HEADS):
        sl_h = slice(h * DH, (h + 1) * DH)
        u_h = uc[:, sl_h]
        ub = u_h.astype(BF16)
        q = _dot(ub, wq_ref[h])
        k = _dot(ub, wk_ref[h]) * (DH ** -0.5)
        v = vpad_ref[:, sl_h]
        qb, kb = q.astype(BF16), k.astype(BF16)

        ic_col = gb[:, GATE_COL_I + h:GATE_COL_I + h + 1]
        ic_row = gbt[GATE_COL_I + h:GATE_COL_I + h + 1, :]
        lf_col = log_sigmoid(gb[:, GATE_COL_F + h:GATE_COL_F + h + 1])
        lf_row = log_sigmoid(gbt[GATE_COL_F + h:GATE_COL_F + h + 1, :])
        if valid < L:
            ic_col = jnp.where(tok_col < valid, ic_col, NEG_INF)
            ic_row = jnp.where(tok_row < valid, ic_row, NEG_INF)
            lf_col = jnp.where(tok_col < valid, lf_col, 0.0)
            lf_row = jnp.where(tok_row < valid, lf_row, 0.0)

        cum_col = jnp.sum(jnp.where(tril, lf_row, 0.0), axis=1, keepdims=True)
        cum_row = jnp.sum(jnp.where(triu, lf_col, 0.0), axis=0, keepdims=True)
        m0 = m_ref[0, 0:1, h:h + 1]
        dmat = jnp.where(tril, cum_col - cum_row + ic_row, NEG_INF)
        inter = cum_col + m0
        m_t = jnp.maximum(inter, jnp.max(dmat, axis=1, keepdims=True))
        w = jnp.exp(dmat - m_t)
        sc = jnp.exp(inter - m_t)
        s = _dot_nt(qb, kb) * w
        c_old = c_ref[0, h]
        n_old = n_ref[0, h:h + 1, :]
        num = _dot(s.astype(BF16), v.astype(BF16)) + sc * _dot_nt(qb, c_old.astype(BF16))
        den = jnp.sum(s, axis=1, keepdims=True) + sc * jnp.sum(q * n_old, axis=1, keepdims=True)
        hc = num / jnp.maximum(jnp.abs(den), jnp.exp(-m_t))

        m_new = m_t[L - 1:L, :]
        cum_last = cum_col[L - 1:L, :]
        wl = jnp.exp(cum_last - cum_col + ic_col - m_new)
        sl = jnp.exp(cum_last + m0 - m_new)
        vw_t = (v * wl).T.astype(BF16)
        c_ref[0, h] = sl * c_old + _dot(vw_t, kb)
        n_ref[0, h:h + 1, :] = sl * n_old + jnp.sum(wl * k, axis=0, keepdims=True)
        m_ref[0, 0:1, h:h + 1] = m_new

        hn = _rms(hc, gh_ref[:, sl_h])
        out = (hn[0:valid, :] + sk_ref[:, sl_h] * u_h[0:valid, :]) * _sigmoid(mo_ref[:, sl_h])
        o_ref[:, sl_h] = out


def _mlstm(mu, mv, mo, gates, conv_buf, c0, n0, m0, w_mconv, b_mconv, w_mq, w_mk, b_ig, b_fg,
           g_mhead, m_skip, *, batch, seq):
    L = MLSTM_CHUNK
    valid = min(seq, L)
    assert seq % valid == 0 and (valid == L or seq == valid)
    nc = seq // valid
    gate_bias = jnp.zeros((1, LANES), F32)
    gate_bias = gate_bias.at[0, GATE_COL_I:GATE_COL_I + MLSTM_HEADS].set(b_ig)
    gate_bias = gate_bias.at[0, GATE_COL_F:GATE_COL_F + MLSTM_HEADS].set(b_fg)
    tok = lambda b, c: (b * nc + c, 0)
    const2 = lambda b, c: (0, 0)
    const3 = lambda b, c: (0, 0, 0)
    per_b3 = lambda b, c: (b, 0, 0)
    per_b4 = lambda b, c: (b, 0, 0, 0)
    H, DH, W = MLSTM_HEADS, MLSTM_DH, MLSTM_WIDTH
    return pl.pallas_call(
        functools.partial(_mlstm_body, valid=valid),
        grid=(batch, nc),
        in_specs=[pl.BlockSpec((valid, W), tok), pl.BlockSpec((valid, W), tok),
                  pl.BlockSpec((valid, W), tok), pl.BlockSpec((valid, LANES), tok),
                  pl.BlockSpec((1, MLSTM_CONV - 1, W), per_b3),
                  pl.BlockSpec((1, H, DH, DH), per_b4),
                  pl.BlockSpec((1, H, DH), per_b3),
                  pl.BlockSpec((1, 1, H), per_b3),
                  pl.BlockSpec((MLSTM_CONV, W), const2), pl.BlockSpec((1, W), const2),
                  pl.BlockSpec((H, DH, DH), const3), pl.BlockSpec((H, DH, DH), const3),
                  pl.BlockSpec((1, LANES), const2), pl.BlockSpec((1, W), const2),
                  pl.BlockSpec((1, W), const2)],
        out_specs=[pl.BlockSpec((valid, W), tok),
                   pl.BlockSpec((1, MLSTM_CONV - 1, W), per_b3),
                   pl.BlockSpec((1, H, DH, DH), per_b4),
                   pl.BlockSpec((1, H, DH), per_b3),
                   pl.BlockSpec((1, 1, H), per_b3)],
        out_shape=[jax.ShapeDtypeStruct((batch * seq, W), F32),
                   jax.ShapeDtypeStruct((batch, MLSTM_CONV - 1, W), F32),
                   jax.ShapeDtypeStruct((batch, H, DH, DH), F32),
                   jax.ShapeDtypeStruct((batch, H, DH), F32),
                   jax.ShapeDtypeStruct((batch, 1, H), F32)],
        scratch_shapes=[pltpu.VMEM((SUBLANES + L, W), F32), pltpu.VMEM((L, W), F32),
                        pltpu.VMEM((L, LANES), F32)],
        compiler_params=_cparams(("arbitrary", "arbitrary")),
        name="mlstm",
    )(mu, mv, mo, gates, conv_buf, c0, n0, m0.reshape(batch, 1, H),
      w_mconv, b_mconv.reshape(1, W), w_mq.astype(BF16), w_mk.astype(BF16), gate_bias,
      g_mhead.reshape(1, W), m_skip.reshape(1, W))


def _compress_rows(xk_ref, xv_ref, pe_ref, w1_ref, w2_ref, n_pairs):
    pair_rows = 2 * CMP_BLOCK
    outs = []
    for kv, x_ref in enumerate((xk_ref, xv_ref)):
        acc = jnp.zeros((2 * n_pairs, NSA_KV_HEADS * CMP_HIDDEN), F32)
        for r in range(CMP_BLOCK):
            ev = x_ref[pl.ds(r, n_pairs, stride=pair_rows), :]
            od = x_ref[pl.ds(CMP_BLOCK + r, n_pairs, stride=pair_rows), :]
            xr = jnp.concatenate([ev, od], axis=0) + pe_ref[kv, r:r + 1, :]
            acc = acc + _dot(xr.astype(BF16), w1_ref[kv, r])
        outs.append(_dot(_silu(acc).astype(BF16), w2_ref[kv]))
    return jnp.concatenate(outs, axis=1)


def _compress_body(xk_ref, xv_ref, pe_ref, w1_ref, w2_ref, oe_ref, oo_ref, *, n_pairs):
    out = _compress_rows(xk_ref, xv_ref, pe_ref, w1_ref, w2_ref, n_pairs)
    oe_ref[0] = out[0:n_pairs, :]
    oo_ref[0] = out[n_pairs:, :]


def _compress_paged_body(pt_ref, *refs, n_pages):
    page_refs = refs[:n_pages]
    pe_ref, w1_ref, w2_ref, oe_ref, oo_ref, bk_ref, bv_ref = refs[n_pages:]
    for j in range(n_pages):
        rows = slice(j * PAGE_SIZE, (j + 1) * PAGE_SIZE)
        bk_ref[rows, :] = page_refs[j][0, 0:KV_WIDTH, :].T
        bv_ref[rows, :] = page_refs[j][0, KV_WIDTH:, :].T
    n_pairs = n_pages * PAGE_SIZE // (2 * CMP_BLOCK)
    out = _compress_rows(bk_ref, bv_ref, pe_ref, w1_ref, w2_ref, n_pairs)
    oe_ref[0] = out[0:n_pairs, :]
    oo_ref[0] = out[n_pairs:, :]


def _pack_compress_weights(pe, w1, w2):
    eye_h = jnp.eye(NSA_KV_HEADS, dtype=F32)
    pe_r = jnp.broadcast_to(pe[:, :, None, :], (2, CMP_BLOCK, NSA_KV_HEADS, HEAD_DIM))
    pe_r = pe_r.reshape(2, CMP_BLOCK, KV_WIDTH)
    w1r = w1.reshape(2, CMP_BLOCK, HEAD_DIM, CMP_HIDDEN)
    w1_big = jnp.einsum('krdc,hH->krhdHc', w1r, eye_h)
    w1_big = w1_big.reshape(2, CMP_BLOCK, KV_WIDTH, NSA_KV_HEADS * CMP_HIDDEN).astype(BF16)
    w2_big = jnp.einsum('kcd,hH->khcHd', w2, eye_h)
    w2_big = w2_big.reshape(2, NSA_KV_HEADS * CMP_HIDDEN, KV_WIDTH).astype(BF16)
    return pe_r, w1_big, w2_big


def _compress_prompt(k_rows, v_rows, cw, *, batch, seq):
    pe_r, w1_big, w2_big = cw
    n_pairs = seq // (2 * CMP_BLOCK)
    const3 = lambda b: (0, 0, 0)
    out_sd = jax.ShapeDtypeStruct((batch, n_pairs, 2 * KV_WIDTH), F32)
    return pl.pallas_call(
        functools.partial(_compress_body, n_pairs=n_pairs),
        grid=(batch,),
        in_specs=[pl.BlockSpec((seq, KV_WIDTH), lambda b: (b, 0)),
                  pl.BlockSpec((seq, KV_WIDTH), lambda b: (b, 0)),
                  pl.BlockSpec(pe_r.shape, const3),
                  pl.BlockSpec(w1_big.shape, lambda b: (0, 0, 0, 0)),
                  pl.BlockSpec(w2_big.shape, const3)],
        out_specs=[pl.BlockSpec((1, n_pairs, 2 * KV_WIDTH), lambda b: (b, 0, 0))] * 2,
        out_shape=[out_sd, out_sd],
        compiler_params=_cparams(("arbitrary",)),
        name="compress_prompt",
    )(k_rows, v_rows, pe_r, w1_big, w2_big)


COMPRESS_PAGES_PER_STEP = 32


def _compress_paged(pool, page_table, cw):
    pe_r, w1_big, w2_big = cw
    batch, n_pages = page_table.shape
    pps = COMPRESS_PAGES_PER_STEP
    assert n_pages % pps == 0
    n_steps = n_pages // pps
    n_pairs = pps * PAGE_SIZE // (2 * CMP_BLOCK)
    const3 = lambda b, c, pt: (0, 0, 0)

    def page_spec(j):
        return pl.BlockSpec((1, 2 * PAGE_SIZE, KV_WIDTH),
                            lambda b, c, pt: (pt[(b * n_steps + c) * pps + j], 0, 0))

    out_sd = jax.ShapeDtypeStruct((batch, n_steps * n_pairs, 2 * KV_WIDTH), F32)
    return pl.pallas_call(
        functools.partial(_compress_paged_body, n_pages=pps),
        grid_spec=pltpu.PrefetchScalarGridSpec(
            num_scalar_prefetch=1,
            grid=(batch, n_steps),
            in_specs=[page_spec(j) for j in range(pps)] + [
                pl.BlockSpec(pe_r.shape, const3),
                pl.BlockSpec(w1_big.shape, lambda b, c, pt: (0, 0, 0, 0)),
                pl.BlockSpec(w2_big.shape, const3)],
            out_specs=[pl.BlockSpec((1, n_pairs, 2 * KV_WIDTH), lambda b, c, pt: (b, c, 0))] * 2,
            scratch_shapes=[pltpu.VMEM((pps * PAGE_SIZE, KV_WIDTH), F32),
                            pltpu.VMEM((pps * PAGE_SIZE, KV_WIDTH), F32)]),
        out_shape=[out_sd, out_sd],
        compiler_params=_cparams(("arbitrary", "arbitrary")),
        name="compress_paged",
    )(page_table.reshape(-1), *([pool] * pps), pe_r, w1_big, w2_big)


def _cmp_select_body(q_ref, ke_ref, ko_ref, o_ref, sb_ref, *, tq, pos0, n_sel):
    ns = ke_ref.shape[1]
    nsw = sb_ref.shape[-1]
    i = pl.program_id(1)
    pos = pos0 + i * tq + lax.broadcasted_iota(jnp.int32, (tq, 1), 0)
    pos4 = jnp.concatenate([pos] * NSA_GROUP, axis=0)
    pair = lax.broadcasted_iota(jnp.int32, (1, ns), 1)
    avail_e = (2 * pair + 1) * CMP_BLOCK - 1 <= pos4
    avail_o = (2 * pair + 2) * CMP_BLOCK - 1 <= pos4
    any_avail = (CMP_BLOCK - 1 <= pos4).astype(F32)
    q = q_ref[...] * ATTN_SCALE
    lane = lax.broadcasted_iota(jnp.int32, (1, nsw), 1)
    lane_f = lane.astype(F32)
    cur = pos // SEL_BLOCK
    forced = (lane == 0) | (lane == cur) | (lane == cur - 1)
    for kh in range(NSA_KV_HEADS):
        qs = jnp.concatenate([q[:, (kh * NSA_GROUP + g) * HEAD_DIM:(kh * NSA_GROUP + g + 1) * HEAD_DIM]
                              for g in range(NSA_GROUP)], axis=0).astype(BF16)
        ks, vs = slice(kh * HEAD_DIM, (kh + 1) * HEAD_DIM), slice(KV_WIDTH + kh * HEAD_DIM,
                                                                   KV_WIDTH + (kh + 1) * HEAD_DIM)
        se = jnp.where(avail_e, _dot_nt(qs, ke_ref[0, :, ks].astype(BF16)), NEG_INF)
        so = jnp.where(avail_o, _dot_nt(qs, ko_ref[0, :, ks].astype(BF16)), NEG_INF)
        mx = jnp.maximum(jnp.max(se, axis=1, keepdims=True), jnp.max(so, axis=1, keepdims=True))
        pe, po = jnp.exp(se - mx), jnp.exp(so - mx)
        inv = any_avail / (jnp.sum(pe, axis=1, keepdims=True) + jnp.sum(po, axis=1, keepdims=True))
        pe, po = pe * inv, po * inv
        oh = (_dot(pe.astype(BF16), ke_ref[0, :, vs].astype(BF16))
              + _dot(po.astype(BF16), ko_ref[0, :, vs].astype(BF16)))
        for g in range(NSA_GROUP):
            hd = kh * NSA_GROUP + g
            o_ref[:, hd * HEAD_DIM:(hd + 1) * HEAD_DIM] = oh[g * tq:(g + 1) * tq, :]
        psum = pe + po
        score = psum[0:tq, :]
        for g in range(1, NSA_GROUP):
            score = score + psum[g * tq:(g + 1) * tq, :]
        if nsw > ns:
            score = jnp.concatenate([score, jnp.zeros((tq, nsw - ns), F32)], axis=1)
        pri = jnp.where(lane <= cur, jnp.where(forced, SEL_PRIORITY, score), -SEL_PRIORITY)
        pri = jnp.where(lane < n_sel, pri, -jnp.inf)
        bias = jnp.full((tq, nsw), NEG_INF, F32)
        for _ in range(min(TOP_N, n_sel)):
            top = jnp.max(pri, axis=1, keepdims=True)
            first = jnp.min(jnp.where(pri == top, lane_f, float(nsw)), axis=1, keepdims=True)
            hit = lane_f == first
            bias = jnp.where(hit, 0.0, bias)
            pri = jnp.where(hit, -jnp.inf, pri)
        sb_ref[0, kh] = bias


def _cmp_select(q2d, kce, kco, *, batch, seq, tq, pos0, n_sel):
    ns = kce.shape[1]
    nsw = ns if n_sel <= ns else ns + LANES
    nq = seq // tq
    return pl.pallas_call(
        functools.partial(_cmp_select_body, tq=tq, pos0=pos0, n_sel=n_sel),
        grid=(batch, nq),
        in_specs=[pl.BlockSpec((tq, NSA_WIDTH), lambda b, i: (b * nq + i, 0)),
                  pl.BlockSpec((1, ns, 2 * KV_WIDTH), lambda b, i: (b, 0, 0)),
                  pl.BlockSpec((1, ns, 2 * KV_WIDTH), lambda b, i: (b, 0, 0))],
        out_specs=[pl.BlockSpec((tq, NSA_WIDTH), lambda b, i: (b * nq + i, 0)),
                   pl.BlockSpec((1, NSA_KV_HEADS, tq, nsw), lambda b, i: (b, 0, i, 0))],
        out_shape=[jax.ShapeDtypeStruct((batch * seq, NSA_WIDTH), F32),
                   jax.ShapeDtypeStruct((batch, NSA_KV_HEADS, seq, nsw), F32)],
        compiler_params=_cparams(("arbitrary", "arbitrary")),
        name="cmp_select",
    )(q2d, kce, kco)


def _softmax_update(sc, vt_bf16, m_ref, l_ref, acc_ref):
    m_old = m_ref[...]
    m_new = jnp.maximum(m_old, jnp.max(sc, axis=1, keepdims=True))
    alpha = jnp.exp(m_old - m_new)
    pr = jnp.exp(sc - jnp.concatenate([m_new] * (sc.shape[1] // LANES), axis=1))
    l_ref[...] = alpha * l_ref[...] + jnp.sum(pr, axis=1, keepdims=True)
    acc_ref[...] = alpha * acc_ref[...] + _dot_nt(pr.astype(BF16), vt_bf16)
    m_ref[...] = m_new


def _softmax_init(m_ref, l_ref, acc_ref):
    m_ref[...] = jnp.full(m_ref.shape, NEG_INF, F32)
    l_ref[...] = jnp.zeros(l_ref.shape, F32)
    acc_ref[...] = jnp.zeros(acc_ref.shape, F32)


def _block_onehot_t(first_key, n_keys):
    blk = (first_key + lax.broadcasted_iota(jnp.int32, (1, n_keys), 1)) // SEL_BLOCK
    r = lax.broadcasted_iota(jnp.int32, (LANES, 1), 0) & (SEL_BLOCK - 1)
    return (r == blk).astype(F32)


ATTN_TAB_COLS = 5


def _attn_pairs(seq, tq, tk, window):
    rows = []
    for i in range(seq // tq):
        t_lo, t_hi = i * tq, i * tq + tq - 1
        k_lo = 0 if window is None else max(0, t_lo - window + 1)
        js = list(range(k_lo // tk, t_hi // tk + 1))
        for n, j in enumerate(js):
            partial_tile = j * tk + tk - 1 > t_lo or (window is not None and j * tk <= t_hi - window)
            rows.append((i, j, int(n == 0), int(n == len(js) - 1), int(partial_tile)))
    return np.asarray(rows, np.int32)


def _attn_body(tab_ref, q_ref, kv_ref, *rest, tq, tk, window, use_bias):
    if use_bias:
        sb_ref, o_ref, qa_ref, m_ref, l_ref, acc_ref = rest
    else:
        o_ref, qa_ref, m_ref, l_ref, acc_ref = rest
    p = pl.program_id(1)
    i, j, first, last, partial_tile = [tab_ref[ATTN_TAB_COLS * p + n] for n in range(ATTN_TAB_COLS)]
    G = NSA_GROUP
    rows = NSA_HEADS * tq

    @pl.when(first == 1)
    def _():
        qa_ref[...] = jnp.zeros(qa_ref.shape, BF16)
        q = (q_ref[0] * ATTN_SCALE).astype(BF16)
        for hd in range(NSA_HEADS):
            kh = hd // G
            qa_ref[hd * tq:(hd + 1) * tq, kh * HEAD_DIM:(kh + 1) * HEAD_DIM] = (
                q[:, hd * HEAD_DIM:(hd + 1) * HEAD_DIM])
            if use_bias:
                nsw = sb_ref.shape[-1]
                qa_ref[hd * tq:(hd + 1) * tq, LANES + kh * SEL_BLOCK:LANES + kh * SEL_BLOCK + nsw] = (
                    sb_ref[0, kh].astype(BF16))
        _softmax_init(m_ref, l_ref, acc_ref)

    kt = kv_ref[0, 0:KV_WIDTH, :]
    if use_bias:
        kt = jnp.concatenate([kt, _block_onehot_t(j * tk, tk)], axis=0)
    sc = _dot(qa_ref[...], kt.astype(BF16))
    vt = kv_ref[0, KV_WIDTH:, :].astype(BF16)

    @pl.when(partial_tile == 1)
    def _():
        qpos = i * tq + (lax.broadcasted_iota(jnp.int32, (rows, 1), 0) & (tq - 1))
        kpos = j * tk + lax.broadcasted_iota(jnp.int32, (1, tk), 1)
        valid = kpos <= qpos
        if window is not None:
            valid = valid & (kpos > qpos - window)
        _softmax_update(jnp.where(valid, sc, NEG_INF), vt, m_ref, l_ref, acc_ref)

    @pl.when(partial_tile == 0)
    def _():
        _softmax_update(sc, vt, m_ref, l_ref, acc_ref)

    @pl.when(last == 1)
    def _():
        o = acc_ref[...] / l_ref[...]
        for hd in range(NSA_HEADS):
            kh = hd // G
            o_ref[0, :, hd * HEAD_DIM:(hd + 1) * HEAD_DIM] = (
                o[hd * tq:(hd + 1) * tq, kh * HEAD_DIM:(kh + 1) * HEAD_DIM])


def _attn_prompt(q3d, kv_t, selb, *, tq, tk, window):
    batch, seq, _ = q3d.shape
    assert tq & (tq - 1) == 0 and tk % LANES == 0
    use_bias = selb is not None
    assert not use_bias or selb.shape[-1] <= SEL_BLOCK
    tab = _attn_pairs(seq, tq, tk, window)
    depth = 2 * LANES if use_bias else LANES
    rows = NSA_HEADS * tq
    C = ATTN_TAB_COLS
    in_specs = [pl.BlockSpec((1, tq, NSA_WIDTH), lambda b, p, t: (b, t[C * p], 0)),
                pl.BlockSpec((1, 2 * KV_WIDTH, tk), lambda b, p, t: (b, 0, t[C * p + 1]))]
    args = [q3d, kv_t]
    if use_bias:
        in_specs.append(pl.BlockSpec((1, NSA_KV_HEADS, tq, selb.shape[-1]),
                                     lambda b, p, t: (b, 0, t[C * p], 0)))
        args.append(selb)
    return pl.pallas_call(
        functools.partial(_attn_body, tq=tq, tk=tk, window=window, use_bias=use_bias),
        grid_spec=pltpu.PrefetchScalarGridSpec(
            num_scalar_prefetch=1,
            grid=(batch, tab.shape[0]),
            in_specs=in_specs,
            out_specs=pl.BlockSpec((1, tq, NSA_WIDTH), lambda b, p, t: (b, t[C * p], 0)),
            scratch_shapes=[pltpu.VMEM((rows, depth), BF16), pltpu.VMEM((rows, LANES), F32),
                            pltpu.VMEM((rows, LANES), F32), pltpu.VMEM((rows, LANES), F32)]),
        out_shape=jax.ShapeDtypeStruct((batch, seq, NSA_WIDTH), F32),
        compiler_params=_cparams(("arbitrary", "arbitrary")),
        name="attn_sel" if use_bias else "attn_win",
    )(jnp.asarray(tab.reshape(-1)), *args)


ATTN_PAGES_PER_STEP = 16


def _attn_paged_body(pt_ref, qa_ref, bq_ref, bn_ref, kn_ref, *rest, n_pages, n_new):
    page_refs = rest[:n_pages]
    o_ref, m_ref, l_ref, acc_ref = rest[n_pages:]
    c = pl.program_id(1)
    rows = qa_ref.shape[1]

    @pl.when(c == 0)
    def _():
        _softmax_init(m_ref, l_ref, acc_ref)

    keys = n_pages * PAGE_SIZE
    kt = jnp.concatenate([r[0, 0:KV_WIDTH, :] for r in page_refs], axis=1)
    vt = jnp.concatenate([r[0, KV_WIDTH:, :] for r in page_refs], axis=1).astype(BF16)
    rhs = jnp.concatenate([kt, _block_onehot_t(0, keys)], axis=0).astype(BF16)
    qa = qa_ref[0]
    lhs = jnp.concatenate([qa, bq_ref[0, 0]], axis=1).astype(BF16)
    _softmax_update(_dot(lhs, rhs), vt, m_ref, l_ref, acc_ref)

    @pl.when(c == pl.num_programs(1) - 1)
    def _():
        kn = kn_ref[0]
        sc = _dot(qa.astype(BF16), kn[0:KV_WIDTH, :].astype(BF16)) + bn_ref[0]
        tq = lax.broadcasted_iota(jnp.int32, (rows, 1), 0) % n_new
        kk = lax.broadcasted_iota(jnp.int32, (1, kn.shape[1]), 1)
        sc = jnp.where((kk <= tq) & (kk < n_new), sc, NEG_INF)
        _softmax_update(sc, kn[KV_WIDTH:, :].astype(BF16), m_ref, l_ref, acc_ref)
        o_ref[0] = acc_ref[...] / l_ref[...]


def _attn_paged(qa, bias_q, bias_new, kv_new_t, pool, page_table, *, n_new):
    batch, n_pages = page_table.shape
    pps = ATTN_PAGES_PER_STEP
    assert n_pages % pps == 0 and pps * PAGE_SIZE // SEL_BLOCK <= SEL_BLOCK
    n_steps = n_pages // pps
    rows = qa.shape[1]

    def page_spec(j):
        return pl.BlockSpec((1, 2 * KV_WIDTH, PAGE_SIZE),
                            lambda b, c, pt: (pt[(b * n_steps + c) * pps + j], 0, 0))

    per_b = lambda b, c, pt: (b, 0, 0)
    return pl.pallas_call(
        functools.partial(_attn_paged_body, n_pages=pps, n_new=n_new),
        grid_spec=pltpu.PrefetchScalarGridSpec(
            num_scalar_prefetch=1,
            grid=(batch, n_steps),
            in_specs=[pl.BlockSpec((1, rows, LANES), per_b),
                      pl.BlockSpec((1, 1, rows, LANES), lambda b, c, pt: (b, c, 0, 0)),
                      pl.BlockSpec((1, rows, LANES), per_b),
                      pl.BlockSpec((1,) + kv_new_t.shape[1:], per_b)]
            + [page_spec(j) for j in range(pps)],
            out_specs=pl.BlockSpec((1, rows, LANES), per_b),
            scratch_shapes=[pltpu.VMEM((rows, LANES), F32), pltpu.VMEM((rows, LANES), F32),
                            pltpu.VMEM((rows, LANES), F32)]),
        out_shape=jax.ShapeDtypeStruct((batch, rows, LANES), F32),
        compiler_params=_cparams(("arbitrary", "arbitrary")),
        name="attn_sel_paged",
    )(page_table.reshape(-1), qa, bias_q, bias_new, kv_new_t, *([pool] * pps))


def _attn_window_body(qa_ref, wb_ref, kn_ref, o_ref, *, n_new, past):
    qa = qa_ref[0].astype(BF16)
    wb, kn = wb_ref[0], kn_ref[0]
    rows, n_buf = qa.shape[0], wb.shape[1]
    qpos = past + lax.broadcasted_iota(jnp.int32, (rows, 1), 0) % n_new

    def masked(sc, kpos, extra):
        diff = qpos - kpos
        return jnp.where((diff >= 0) & (diff < WINDOW) & (kpos >= 0) & extra, sc, NEG_INF)

    nb = lax.broadcasted_iota(jnp.int32, (1, n_buf), 1)
    nn = lax.broadcasted_iota(jnp.int32, (1, kn.shape[1]), 1)
    sb = masked(_dot(qa, wb[0:KV_WIDTH, :].astype(BF16)), past - n_buf + nb, nb >= 0)
    sn = masked(_dot(qa, kn[0:KV_WIDTH, :].astype(BF16)), past + nn, nn < n_new)
    mx = jnp.maximum(jnp.max(sb, axis=1, keepdims=True), jnp.max(sn, axis=1, keepdims=True))
    pb, pn = jnp.exp(sb - mx), jnp.exp(sn - mx)
    o = (_dot_nt(pb.astype(BF16), wb[KV_WIDTH:, :].astype(BF16))
         + _dot_nt(pn.astype(BF16), kn[KV_WIDTH:, :].astype(BF16)))
    o_ref[0] = o / (jnp.sum(pb, axis=1, keepdims=True) + jnp.sum(pn, axis=1, keepdims=True))


def _attn_window_small(qa, win_t, kv_new_t, *, n_new, past):
    batch, rows, _ = qa.shape
    per_b = lambda b: (b, 0, 0)
    return pl.pallas_call(
        functools.partial(_attn_window_body, n_new=n_new, past=past),
        grid=(batch,),
        in_specs=[pl.BlockSpec((1, rows, LANES), per_b),
                  pl.BlockSpec((1,) + win_t.shape[1:], per_b),
                  pl.BlockSpec((1,) + kv_new_t.shape[1:], per_b)],
        out_specs=pl.BlockSpec((1, rows, LANES), per_b),
        out_shape=jax.ShapeDtypeStruct((batch, rows, LANES), F32),
        compiler_params=_cparams(("arbitrary",)),
        name="attn_win_small",
    )(qa, win_t, kv_new_t)


FFN_TF = 256


def _ffn_body(x_ref, om_ref, oc_ref, os_ref, ow_ref, gt_ref, ge_ref, wo_ref, gn_ref, gf_ref, gl_ref,
              wua_ref, wug_ref, wca_ref, wcg_ref, wd_ref, fba_ref, fbg_ref,
              y_ref, fna_ref, fng_ref,
              h_ref, hn_ref, acc_ref, xa_ref, xg_ref, ca_ref, cg_ref, *, tm, stride, halo):
    s = pl.program_id(1)
    f = pl.program_id(2)

    @pl.when(f == 0)
    def _():
        sig = _sigmoid(gt_ref[...])
        hi = sig.astype(BF16)
        lo = (sig - hi.astype(F32)).astype(BF16)
        comb = None
        for br, ob_ref in enumerate((oc_ref, os_ref, ow_ref)):
            gate = _dot(hi, ge_ref[br]) + _dot(lo, ge_ref[br])
            term = gate * ob_ref[...]
            comb = term if comb is None else comb + term
        onsa = _rms(comb, gn_ref[...])
        h = (x_ref[...] + _dot(om_ref[...].astype(BF16), wo_ref[0:MLSTM_WIDTH, :])
             + _dot(onsa.astype(BF16), wo_ref[MLSTM_WIDTH:, :]))
        h_ref[...] = h
        hn_ref[...] = _rms(h, gf_ref[...]).astype(BF16)
        acc_ref[...] = jnp.zeros(acc_ref.shape, F32)

    hn = hn_ref[...]
    base = halo - (FFN_CONV - 1) * stride
    convs = []
    for w_ref, xx_ref, fb_ref, c_ref, wc_ref, fn_ref in (
            (wua_ref, xa_ref, fba_ref, ca_ref, wca_ref, fna_ref),
            (wug_ref, xg_ref, fbg_ref, cg_ref, wcg_ref, fng_ref)):
        @pl.when(s == 0)
        def _():
            xx_ref[base:halo, :] = fb_ref[0]

        @pl.when(s > 0)
        def _():
            xx_ref[0:halo, :] = c_ref[f]

        xx_ref[halo:halo + tm, :] = _dot(hn, w_ref[...])
        conv = xx_ref[base:base + tm, :] * wc_ref[0:1, :]
        for j in range(1, FFN_CONV):
            conv = conv + xx_ref[base + j * stride:base + j * stride + tm, :] * wc_ref[j:j + 1, :]
        convs.append(conv)
        c_ref[f] = xx_ref[tm:tm + halo, :]
        fn_ref[0, 0] = xx_ref[tm + base:tm + halo, :]
    act = _silu(convs[1]) * convs[0]
    acc_ref[...] += _dot(act.astype(BF16), wd_ref[...])

    @pl.when(f == pl.num_programs(2) - 1)
    def _():
        y_ref[...] = _rms(h_ref[...] + acc_ref[...], gl_ref[...])


def _gate_expand():
    ge = np.zeros((N_BRANCH, LANES, NSA_WIDTH), np.float32)
    for hd in range(NSA_HEADS):
        for br in range(N_BRANCH):
            ge[br, GATE_COL_NSA + hd * N_BRANCH + br, hd * HEAD_DIM:(hd + 1) * HEAD_DIM] = 1.0
    return jnp.asarray(ge, BF16)


def _ffn(x2d, om, oc, osel, ow, gt, fbuf, w_out, g_nsa, g_ffn, g_final, w_up, w_fconv, w_down,
         *, nb, tm, stride):
    rows = x2d.shape[0]
    ns = rows // (nb * tm)
    tf = FFN_TF
    nf = D_FF // tf
    halo = -(-(FFN_CONV - 1) * stride // SUBLANES) * SUBLANES
    assert tm >= halo and D_FF % tf == 0
    tok = lambda b, s, f: (b * ns + s, 0)
    const2 = lambda b, s, f: (0, 0)
    nfb = (FFN_CONV - 1) * stride
    w_up_b = w_up.astype(BF16)
    fn_sd = jax.ShapeDtypeStruct((nb, ns, nfb, D_FF), F32)
    y, fna, fng = pl.pallas_call(
        functools.partial(_ffn_body, tm=tm, stride=stride, halo=halo),
        grid=(nb, ns, nf),
        in_specs=[pl.BlockSpec((tm, D_MODEL), tok)] + [pl.BlockSpec((tm, NSA_WIDTH), tok)] * 4
        + [pl.BlockSpec((tm, LANES), tok),
           pl.BlockSpec((N_BRANCH, LANES, NSA_WIDTH), lambda b, s, f: (0, 0, 0)),
           pl.BlockSpec((D_MODEL, D_MODEL), const2),
           pl.BlockSpec((1, NSA_WIDTH), const2), pl.BlockSpec((1, D_MODEL), const2),
           pl.BlockSpec((1, D_MODEL), const2),
           pl.BlockSpec((D_MODEL, tf), lambda b, s, f: (0, f)),
           pl.BlockSpec((D_MODEL, tf), lambda b, s, f: (0, f + nf)),
           pl.BlockSpec((FFN_CONV, tf), lambda b, s, f: (0, f)),
           pl.BlockSpec((FFN_CONV, tf), lambda b, s, f: (0, f + nf)),
           pl.BlockSpec((tf, D_MODEL), lambda b, s, f: (f, 0)),
           pl.BlockSpec((1, nfb, tf), lambda b, s, f: (b, 0, f)),
           pl.BlockSpec((1, nfb, tf), lambda b, s, f: (b, 0, f + nf))],
        out_specs=[pl.BlockSpec((tm, D_MODEL), tok),
                   pl.BlockSpec((1, 1, nfb, tf), lambda b, s, f: (b, s, 0, f)),
                   pl.BlockSpec((1, 1, nfb, tf), lambda b, s, f: (b, s, 0, f))],
        out_shape=[jax.ShapeDtypeStruct((rows, D_MODEL), F32), fn_sd, fn_sd],
        scratch_shapes=[pltpu.VMEM((tm, D_MODEL), F32), pltpu.VMEM((tm, D_MODEL), BF16),
                        pltpu.VMEM((tm, D_MODEL), F32),
                        pltpu.VMEM((halo + tm, tf), F32), pltpu.VMEM((halo + tm, tf), F32),
                        pltpu.VMEM((nf, halo, tf), F32), pltpu.VMEM((nf, halo, tf), F32)],
        compiler_params=_cparams(("arbitrary", "arbitrary", "arbitrary")),
        name="outproj_ffn",
    )(x2d, om, oc, osel, ow, gt, _gate_expand(), w_out.astype(BF16), g_nsa.reshape(1, -1),
      g_ffn.reshape(1, -1), g_final.reshape(1, -1), w_up_b, w_up_b, w_fconv, w_fconv,
      w_down.astype(BF16), fbuf, fbuf)
    return y, fna[:, ns - 1], fng[:, ns - 1]


PROMPT_TM = 512
PROMPT_TQ_CMP = 256
PROMPT_TQ = 128
PROMPT_TK = 256


def _kv_rows(kv_t):
    batch, _, rows = kv_t.shape
    return kv_t.reshape(batch, 2, NSA_KV_HEADS, HEAD_DIM, rows).transpose(0, 4, 1, 2, 3)


def _kv_feature_major(kv5):
    batch, rows = kv5.shape[:2]
    return kv5.transpose(0, 2, 3, 4, 1).reshape(batch, 2 * KV_WIDTH, rows)


def _prompt_layer(x, wts):
    batch, seq, _ = x.shape
    x2d = x.reshape(batch * seq, D_MODEL)
    q, kc_rows, vc_rows, mu, mv, mo, gt, kvc_t, kvs_t, kvw_t = _in_proj(
        x2d, wts["g_mix"], wts["w_in_packed"], batch=batch, seq=seq, tm=min(PROMPT_TM, seq))
    H, DH, W = MLSTM_HEADS, MLSTM_DH, MLSTM_WIDTH
    o_m, mconv, c_new, n_new, m_new = _mlstm(
        mu, mv, mo, gt, jnp.zeros((batch, MLSTM_CONV - 1, W), F32), jnp.zeros((batch, H, DH, DH), F32),
        jnp.zeros((batch, H, DH), F32), jnp.zeros((batch, H), F32),
        wts["w_mconv"], wts["b_mconv"], wts["w_mq"], wts["w_mk"], wts["b_ig"], wts["b_fg"],
        wts["g_mhead"], wts["m_skip"], batch=batch, seq=seq)
    kce, kco = _compress_prompt(kc_rows, vc_rows, wts["cw"], batch=batch, seq=seq)
    n_sel = -(-seq // SEL_BLOCK)
    o_cmp, selb = _cmp_select(q, kce, kco, batch=batch, seq=seq, tq=min(PROMPT_TQ_CMP, seq), pos0=0,
                              n_sel=n_sel)
    q3d = q.reshape(batch, seq, NSA_WIDTH)
    o_sel = _attn_prompt(q3d, kvs_t, selb, tq=PROMPT_TQ, tk=PROMPT_TK, window=None)
    o_win = _attn_prompt(q3d, kvw_t, None, tq=PROMPT_TQ, tk=PROMPT_TK, window=WINDOW)
    fbuf = jnp.zeros((batch, FFN_CONV - 1, 2 * D_FF), F32)
    y, fna, fng = _ffn(x2d, o_m, o_cmp, o_sel.reshape(-1, NSA_WIDTH), o_win.reshape(-1, NSA_WIDTH), gt,
                       fbuf, wts["w_out"], wts["g_nsa"], wts["g_ffn"], wts["g_final"], wts["w_up"],
                       wts["w_fconv"], wts["w_down"], nb=batch, tm=min(PROMPT_TM, seq), stride=1)
    n_win = min(WINDOW, seq)
    return (y.reshape(batch, seq, D_MODEL), _kv_rows(kvc_t), _kv_rows(kvs_t),
            _kv_rows(kvw_t[:, :, seq - n_win:]), mconv, c_new, n_new, m_new.reshape(batch, H),
            jnp.concatenate([fna, fng], axis=-1))


def _decode_rows(q2d, batch, seq):
    q5 = (q2d * ATTN_SCALE).reshape(batch, seq, NSA_KV_HEADS, NSA_GROUP, HEAD_DIM).transpose(0, 2, 3, 1, 4)
    eye = jnp.eye(NSA_KV_HEADS, dtype=F32)
    qa = jnp.einsum('bkgtd,kK->bkgtKd', q5, eye)
    return qa.reshape(batch, NSA_KV_HEADS * NSA_GROUP * seq, KV_WIDTH)


def _decode_rows_out(o, batch, seq):
    o6 = o.reshape(batch, NSA_KV_HEADS, NSA_GROUP, seq, NSA_KV_HEADS, HEAD_DIM)
    o5 = jnp.stack([o6[:, kh, :, :, kh, :] for kh in range(NSA_KV_HEADS)], axis=1)
    return o5.transpose(0, 3, 1, 2, 4).reshape(batch * seq, NSA_WIDTH)


def _sample_layer(x, pool_cmp, pool_sel, win_buf, m_conv, m_c, m_n, m_m, f_buf, page_table, wts):
    batch, seq, _ = x.shape
    n_pages = page_table.shape[1]
    past = n_pages * PAGE_SIZE
    assert past % SEL_BLOCK == 0 and seq <= SEL_BLOCK and seq < CMP_BLOCK
    x2d = x.reshape(batch * seq, D_MODEL)
    q, _, _, mu, mv, mo, gt, kvc_t, kvs_t, kvw_t = _in_proj(
        x2d, wts["g_mix"], wts["w_in_packed"], batch=1, seq=batch * seq, tm=batch * seq)
    per_batch = lambda a: a.reshape(2 * KV_WIDTH, batch, seq).transpose(1, 0, 2)
    kvc_t, kvs_t, kvw_t = per_batch(kvc_t), per_batch(kvs_t), per_batch(kvw_t)
    pad_keys = lambda a: jnp.pad(a, ((0, 0), (0, 0), (0, LANES - seq)))
    H = MLSTM_HEADS
    o_m, mconv, c_new, n_new, m_new = _mlstm(
        mu, mv, mo, gt, m_conv, m_c, m_n, m_m,
        wts["w_mconv"], wts["b_mconv"], wts["w_mq"], wts["w_mk"], wts["b_ig"], wts["b_fg"],
        wts["g_mhead"], wts["m_skip"], batch=batch, seq=seq)
    pool_cmp3, pool_sel3 = _kv_feature_major(pool_cmp), _kv_feature_major(pool_sel)
    kce, kco = _compress_paged(pool_cmp3, page_table, wts["cw"])
    n_past_blk = past // SEL_BLOCK
    n_sel = -(-(past + seq) // SEL_BLOCK)
    o_cmp, selb = _cmp_select(q, kce, kco, batch=batch, seq=seq, tq=seq, pos0=past, n_sel=n_sel)
    qa = _decode_rows(q, batch, seq)
    rows = qa.shape[1]
    blk_per_step = ATTN_PAGES_PER_STEP * PAGE_SIZE // SEL_BLOCK
    n_steps = n_pages // ATTN_PAGES_PER_STEP
    sb_rows = jnp.broadcast_to(selb[:, :, None], (batch, NSA_KV_HEADS, NSA_GROUP, seq, selb.shape[-1]))
    sb_rows = sb_rows.reshape(batch, rows, selb.shape[-1])
    bias_q = sb_rows[:, :, :n_past_blk].reshape(batch, rows, n_steps, blk_per_step).transpose(0, 2, 1, 3)
    bias_q = jnp.pad(bias_q, ((0, 0), (0, 0), (0, 0), (0, LANES - blk_per_step)))
    bias_new = jnp.broadcast_to(sb_rows[:, :, n_past_blk:n_past_blk + 1], (batch, rows, LANES))
    o_sel = _attn_paged(qa, bias_q, bias_new, pad_keys(kvs_t), pool_sel3, page_table, n_new=seq)
    n_buf = win_buf.shape[1]
    assert past >= n_buf
    win_t = _kv_feature_major(win_buf)
    o_win = _attn_window_small(qa, win_t, pad_keys(kvw_t), n_new=seq, past=past)
    win_new = jnp.concatenate([win_t, kvw_t], axis=2)[:, :, seq:]
    tmaj = lambda a: a.reshape(batch, seq, -1).transpose(1, 0, 2).reshape(batch * seq, -1)
    fb_t = f_buf.transpose(1, 0, 2).reshape(1, (FFN_CONV - 1) * batch, 2 * D_FF)
    y, fna, fng = _ffn(tmaj(x2d), tmaj(o_m), tmaj(o_cmp), tmaj(_decode_rows_out(o_sel, batch, seq)),
                       tmaj(_decode_rows_out(o_win, batch, seq)), tmaj(gt), fb_t,
                       wts["w_out"], wts["g_nsa"], wts["g_ffn"], wts["g_final"], wts["w_up"],
                       wts["w_fconv"], wts["w_down"], nb=1, tm=batch * seq, stride=batch)
    y = y.reshape(seq, batch, D_MODEL).transpose(1, 0, 2)
    f_new = jnp.concatenate([fna, fng], axis=-1).reshape(FFN_CONV - 1, batch, 2 * D_FF).transpose(1, 0, 2)
    return (y, _kv_rows(kvc_t), _kv_rows(kvs_t), _kv_rows(win_new), mconv, c_new, n_new,
            m_new.reshape(batch, H), f_new)


def kernel(x_prompt, x_sample, cache_cmp, cache_sel, state_win, state_mlstm_C, state_mlstm_n,
           state_mlstm_m, state_mlstm_conv, state_ffn_conv, page_table,
           g_mix, w_in, w_out, w_mconv, b_mconv, w_mq, w_mk, b_ig, b_fg, g_mhead, m_skip,
           pe_cmp, w_cmp1, w_cmp2, g_nsa, g_ffn, w_up, w_fconv, w_down, g_final):
    assert w_in.shape[0] == 1, "one layer: the final norm is fused into the layer's FFN kernel"
    l = 0
    wts = dict(g_mix=g_mix[l], w_in_packed=_pack_w_in(w_in[l]), w_out=w_out[l], w_mconv=w_mconv[l],
               b_mconv=b_mconv[l], w_mq=w_mq[l], w_mk=w_mk[l], b_ig=b_ig[l], b_fg=b_fg[l],
               g_mhead=g_mhead[l], m_skip=m_skip[l],
               cw=_pack_compress_weights(pe_cmp[l], w_cmp1[l], w_cmp2[l]),
               g_nsa=g_nsa[l], g_ffn=g_ffn[l], g_final=g_final, w_up=w_up[l], w_fconv=w_fconv[l],
               w_down=w_down[l])
    p = _prompt_layer(x_prompt, wts)
    s = _sample_layer(x_sample, cache_cmp[l], cache_sel[l], state_win[l], state_mlstm_conv[l],
                      state_mlstm_C[l], state_mlstm_n[l], state_mlstm_m[l], state_ffn_conv[l],
                      page_table, wts)
    yp, cmp_p, sel_p, win_p, mconv_p, c_p, n_p, m_p, fconv_p = p
    ys, cmp_s, sel_s, win_s, mconv_s, c_s, n_s, m_s, fconv_s = s
    st = lambda a: a[None]
    return (yp, ys, st(cmp_p), st(cmp_s), st(sel_p), st(sel_s), st(win_p), st(win_s),
            st(c_p), st(c_s), st(n_p), st(n_s), st(m_p), st(m_s), st(mconv_p), st(mconv_s),
            st(fconv_p), st(fconv_s))
```

```python
import functools

import numpy as np
import jax
import jax.numpy as jnp
from jax import lax
from jax.experimental import pallas as pl
from jax.experimental.pallas import tpu as pltpu

F32 = jnp.float32
BF16 = jnp.bfloat16

D_MODEL = 1024
PAGE_SIZE = 128
HEAD_DIM = 64
NSA_HEADS = 8
NSA_KV_HEADS = 2
NSA_GROUP = NSA_HEADS // NSA_KV_HEADS
NSA_WIDTH = NSA_HEADS * HEAD_DIM
KV_WIDTH = NSA_KV_HEADS * HEAD_DIM
CMP_BLOCK = 32
CMP_HIDDEN = 2 * HEAD_DIM
SEL_BLOCK = 64
TOP_N = 16
WINDOW = 512
N_BRANCH = 3
ATTN_SCALE = HEAD_DIM ** -0.5
MLSTM_HEADS = 4
MLSTM_WIDTH = D_MODEL - NSA_WIDTH
MLSTM_DH = MLSTM_WIDTH // MLSTM_HEADS
MLSTM_CONV = 4
D_FF = ((8 * D_MODEL // 3 + 127) // 128) * 128
FFN_CONV = 3
EPS = 1e-6
NEG_INF = -1e30
SEL_PRIORITY = 1e4

LANES = 128
SUBLANES = 8
VMEM_LIMIT = 48 * 1024 * 1024

GATE_COL_NSA = 0
GATE_COL_I = NSA_HEADS * N_BRANCH
GATE_COL_F = GATE_COL_I + MLSTM_HEADS

MLSTM_CHUNK = 128
MLSTM_SEQS_PER_STEP = 4


def _cparams(sem):
    return pltpu.CompilerParams(dimension_semantics=sem, vmem_limit_bytes=VMEM_LIMIT)


def _dot(a, b):
    return jnp.dot(a, b, preferred_element_type=F32)


def _dot_nt(a, b):
    return lax.dot_general(a, b, (((1,), (1,)), ((), ())), preferred_element_type=F32)


def _sigmoid(x):
    return 1.0 / (1.0 + jnp.exp(-x))


def _silu(x):
    return x * _sigmoid(x)


def _rms(x, g):
    return x * lax.rsqrt(jnp.mean(x * x, axis=-1, keepdims=True) + EPS) * g


IN_ROW_WIDTHS = (NSA_WIDTH, KV_WIDTH, KV_WIDTH, MLSTM_WIDTH, MLSTM_WIDTH, MLSTM_WIDTH, LANES)
N_KV_BRANCH = 3


def _inproj_body(x_ref, g_ref, w_ref, wt_ref, *out_refs):
    xb = _rms(x_ref[...], g_ref[...]).astype(BF16)
    off = 0
    for ref in out_refs[:len(IN_ROW_WIDTHS)]:
        n = ref.shape[-1]
        ref[...] = _dot(xb, w_ref[:, off:off + n])
        off += n
    for n, ref in enumerate(out_refs[len(IN_ROW_WIDTHS):]):
        ref[0] = _dot_nt(wt_ref[n * 2 * KV_WIDTH:(n + 1) * 2 * KV_WIDTH, :], xb)


def _pack_w_in(w_in):
    splits = np.cumsum([NSA_WIDTH, 2 * KV_WIDTH, 2 * KV_WIDTH, 2 * KV_WIDTH, NSA_HEADS * N_BRANCH,
                        MLSTM_WIDTH, MLSTM_WIDTH, MLSTM_WIDTH, MLSTM_HEADS]).tolist()
    q, kvc, kvs, kvw, gt, mu, mv, mo, mi, mf = jnp.split(w_in, splits, axis=1)
    gates = jnp.concatenate([gt, mi, mf], axis=1)
    gates = jnp.pad(gates, ((0, 0), (0, LANES - gates.shape[1])))
    w_rows = jnp.concatenate([q, kvc, mu, mv, mo, gates], axis=1).astype(BF16)
    w_kv_t = jnp.concatenate([kvc, kvs, kvw], axis=1).T.astype(BF16)
    return w_rows, w_kv_t


def _in_proj(x2d, g_mix, w_packed, *, batch, seq, tm):
    w_rows, w_kv_t = w_packed
    t = x2d.shape[0]
    ns = seq // tm
    kv_sd = jax.ShapeDtypeStruct((batch, 2 * KV_WIDTH, seq), F32)
    return pl.pallas_call(
        _inproj_body,
        grid=(t // tm,),
        in_specs=[pl.BlockSpec((tm, D_MODEL), lambda i: (i, 0)),
                  pl.BlockSpec((1, D_MODEL), lambda i: (0, 0)),
                  pl.BlockSpec(w_rows.shape, lambda i: (0, 0)),
                  pl.BlockSpec(w_kv_t.shape, lambda i: (0, 0))],
        out_specs=[pl.BlockSpec((tm, n), lambda i: (i, 0)) for n in IN_ROW_WIDTHS]
        + [pl.BlockSpec((1, 2 * KV_WIDTH, tm), lambda i: (i // ns, 0, i % ns))] * N_KV_BRANCH,
        out_shape=[jax.ShapeDtypeStruct((t, n), F32) for n in IN_ROW_WIDTHS] + [kv_sd] * N_KV_BRANCH,
        compiler_params=_cparams(("arbitrary",)),
        name="in_proj",
    )(x2d, g_mix.reshape(1, D_MODEL), w_rows, w_kv_t)


def _mlstm_body(*refs, valid, bb):
    cb_ref, c0_ref, n0_ref, m0_ref = refs[4:8]
    cn_ref, c_ref, n_ref, m_ref, xx_ref = refs[16:21]
    halo = SUBLANES

    @pl.when(pl.program_id(1) == 0)
    def _():
        xx_ref[:, 0:halo, :] = jnp.zeros((bb, halo, MLSTM_WIDTH), F32)
        xx_ref[:, halo - (MLSTM_CONV - 1):halo, :] = cb_ref[...]
        c_ref[...] = c0_ref[...]
        n_ref[...] = n0_ref[...]
        m_ref[...] = m0_ref[...]

    for bi in range(bb):
        _mlstm_sequence(bi, *refs, valid=valid)


def _mlstm_sequence(bi, mu_ref, mv_ref, mo_ref, g_ref, cb_ref, c0_ref, n0_ref, m0_ref,
                    wc_ref, bc_ref, wq_ref, wk_ref, gb_ref, gh_ref, sk_ref,
                    o_ref, cn_ref, c_ref, n_ref, m_ref,
                    xx_ref, vpad_ref, gpad_ref, *, valid):
    L = MLSTM_CHUNK
    DH = MLSTM_DH
    halo = SUBLANES

    if valid < L:
        xx_ref[bi, halo:, :] = jnp.zeros((L, MLSTM_WIDTH), F32)
        vpad_ref[bi] = jnp.zeros((L, MLSTM_WIDTH), F32)
        gpad_ref[bi] = jnp.zeros((L, LANES), F32)
    xx_ref[bi, halo:halo + valid, :] = mu_ref[bi]
    vpad_ref[bi, 0:valid, :] = mv_ref[bi]
    gpad_ref[bi, 0:valid, :] = g_ref[bi]

    conv = xx_ref[bi, halo - 3:halo - 3 + L, :] * wc_ref[0:1, :]
    for j in range(1, MLSTM_CONV):
        conv = conv + xx_ref[bi, halo - 3 + j:halo - 3 + j + L, :] * wc_ref[j:j + 1, :]
    uc = _silu(conv + bc_ref[...])

    tail = xx_ref[bi, valid + halo - 3:valid + halo, :]
    xx_ref[bi, halo - 3:halo, :] = tail
    cn_ref[bi] = tail

    gb = gpad_ref[bi] + gb_ref[...]
    gbt = gb.T
    row = lax.broadcasted_iota(jnp.int32, (L, L), 0)
    col = lax.broadcasted_iota(jnp.int32, (L, L), 1)
    tril = row >= col
    triu = row <= col
    tok_col = lax.broadcasted_iota(jnp.int32, (L, 1), 0)
    tok_row = lax.broadcasted_iota(jnp.int32, (1, L), 1)

    def log_sigmoid(x):
        return jnp.minimum(x, 0.0) - jnp.log(1.0 + jnp.exp(-jnp.abs(x)))

    for h in range(MLSTM_HEADS):
        sl_h = slice(h * DH, (h + 1) * DH)
        u_h = uc[:, sl_h]
        ub = u_h.astype(BF16)
        q = _dot(ub, wq_ref[h])
        k = _dot(ub, wk_ref[h]) * (DH ** -0.5)
        v = vpad_ref[bi, :, sl_h]
        qb, kb = q.astype(BF16), k.astype(BF16)

        ic_col = gb[:, GATE_COL_I + h:GATE_COL_I + h + 1]
        ic_row = gbt[GATE_COL_I + h:GATE_COL_I + h + 1, :]
        lf_col = log_sigmoid(gb[:, GATE_COL_F + h:GATE_COL_F + h + 1])
        lf_row = log_sigmoid(gbt[GATE_COL_F + h:GATE_COL_F + h + 1, :])
        if valid < L:
            ic_col = jnp.where(tok_col < valid, ic_col, NEG_INF)
            ic_row = jnp.where(tok_row < valid, ic_row, NEG_INF)
            lf_col = jnp.where(tok_col < valid, lf_col, 0.0)
            lf_row = jnp.where(tok_row < valid, lf_row, 0.0)

        cum_col = jnp.sum(jnp.where(tril, lf_row, 0.0), axis=1, keepdims=True)
        cum_row = jnp.sum(jnp.where(triu, lf_col, 0.0), axis=0, keepdims=True)
        m0 = m_ref[bi, 0:1, h:h + 1]
        dmat = jnp.where(tril, cum_col - cum_row + ic_row, NEG_INF)
        inter = cum_col + m0
        m_t = jnp.maximum(inter, jnp.max(dmat, axis=1, keepdims=True))
        w = jnp.exp(dmat - m_t)
        sc = jnp.exp(inter - m_t)
        s = _dot_nt(qb, kb) * w
        c_old = c_ref[bi, h]
        n_old = n_ref[bi, h:h + 1, :]
        num = _dot(s.astype(BF16), v.astype(BF16)) + sc * _dot_nt(qb, c_old.astype(BF16))
        den = jnp.sum(s, axis=1, keepdims=True) + sc * jnp.sum(q * n_old, axis=1, keepdims=True)
        hc = num / jnp.maximum(jnp.abs(den), jnp.exp(-m_t))

        m_new = m_t[L - 1:L, :]
        cum_last = cum_col[L - 1:L, :]
        wl = jnp.exp(cum_last - cum_col + ic_col - m_new)
        sl = jnp.exp(cum_last + m0 - m_new)
        vw_t = (v * wl).T.astype(BF16)
        c_ref[bi, h] = sl * c_old + _dot(vw_t, kb)
        n_ref[bi, h:h + 1, :] = sl * n_old + jnp.sum(wl * k, axis=0, keepdims=True)
        m_ref[bi, 0:1, h:h + 1] = m_new

        hn = _rms(hc, gh_ref[:, sl_h])
        out = (hn[0:valid, :] + sk_ref[:, sl_h] * u_h[0:valid, :]) * _sigmoid(mo_ref[bi, :, sl_h])
        o_ref[bi, :, sl_h] = out


def _mlstm(mu, mv, mo, gates, conv_buf, c0, n0, m0, w_mconv, b_mconv, w_mq, w_mk, b_ig, b_fg,
           g_mhead, m_skip, *, batch, seq):
    L = MLSTM_CHUNK
    valid = min(seq, L)
    assert seq % valid == 0 and (valid == L or seq == valid)
    nc = seq // valid
    gate_bias = jnp.zeros((1, LANES), F32)
    gate_bias = gate_bias.at[0, GATE_COL_I:GATE_COL_I + MLSTM_HEADS].set(b_ig)
    gate_bias = gate_bias.at[0, GATE_COL_F:GATE_COL_F + MLSTM_HEADS].set(b_fg)
    bb = MLSTM_SEQS_PER_STEP
    assert batch % bb == 0
    tok = lambda b, c: (b, c, 0)
    const2 = lambda b, c: (0, 0)
    const3 = lambda b, c: (0, 0, 0)
    per_b3 = lambda b, c: (b, 0, 0)
    per_b4 = lambda b, c: (b, 0, 0, 0)
    H, DH, W = MLSTM_HEADS, MLSTM_DH, MLSTM_WIDTH
    rows3 = lambda a: a.reshape(batch, seq, a.shape[-1])
    o_m, conv_new, c_new, n_new, m_new = pl.pallas_call(
        functools.partial(_mlstm_body, valid=valid, bb=bb),
        grid=(batch // bb, nc),
        in_specs=[pl.BlockSpec((bb, valid, W), tok), pl.BlockSpec((bb, valid, W), tok),
                  pl.BlockSpec((bb, valid, W), tok), pl.BlockSpec((bb, valid, LANES), tok),
                  pl.BlockSpec((bb, MLSTM_CONV - 1, W), per_b3),
                  pl.BlockSpec((bb, H, DH, DH), per_b4),
                  pl.BlockSpec((bb, H, DH), per_b3),
                  pl.BlockSpec((bb, 1, H), per_b3),
                  pl.BlockSpec((MLSTM_CONV, W), const2), pl.BlockSpec((1, W), const2),
                  pl.BlockSpec((H, DH, DH), const3), pl.BlockSpec((H, DH, DH), const3),
                  pl.BlockSpec((1, LANES), const2), pl.BlockSpec((1, W), const2),
                  pl.BlockSpec((1, W), const2)],
        out_specs=[pl.BlockSpec((bb, valid, W), tok),
                   pl.BlockSpec((bb, MLSTM_CONV - 1, W), per_b3),
                   pl.BlockSpec((bb, H, DH, DH), per_b4),
                   pl.BlockSpec((bb, H, DH), per_b3),
                   pl.BlockSpec((bb, 1, H), per_b3)],
        out_shape=[jax.ShapeDtypeStruct((batch, seq, W), F32),
                   jax.ShapeDtypeStruct((batch, MLSTM_CONV - 1, W), F32),
                   jax.ShapeDtypeStruct((batch, H, DH, DH), F32),
                   jax.ShapeDtypeStruct((batch, H, DH), F32),
                   jax.ShapeDtypeStruct((batch, 1, H), F32)],
        scratch_shapes=[pltpu.VMEM((bb, SUBLANES + L, W), F32), pltpu.VMEM((bb, L, W), F32),
                        pltpu.VMEM((bb, L, LANES), F32)],
        compiler_params=_cparams(("arbitrary", "arbitrary")),
        name="mlstm",
    )(rows3(mu), rows3(mv), rows3(mo), rows3(gates), conv_buf, c0, n0, m0.reshape(batch, 1, H),
      w_mconv, b_mconv.reshape(1, W), w_mq.astype(BF16), w_mk.astype(BF16), gate_bias,
      g_mhead.reshape(1, W), m_skip.reshape(1, W))
    return o_m.reshape(batch * seq, W), conv_new, c_new, n_new, m_new


def _compress_rows(xk_ref, xv_ref, pe_ref, w1_ref, w2_ref, n_pairs):
    pair_rows = 2 * CMP_BLOCK
    outs = []
    for kv, x_ref in enumerate((xk_ref, xv_ref)):
        acc = jnp.zeros((2 * n_pairs, NSA_KV_HEADS * CMP_HIDDEN), F32)
        for r in range(CMP_BLOCK):
            ev = x_ref[pl.ds(r, n_pairs, stride=pair_rows), :]
            od = x_ref[pl.ds(CMP_BLOCK + r, n_pairs, stride=pair_rows), :]
            xr = jnp.concatenate([ev, od], axis=0) + pe_ref[kv, r:r + 1, :]
            acc = acc + _dot(xr.astype(BF16), w1_ref[kv, r])
        outs.append(_dot(_silu(acc).astype(BF16), w2_ref[kv]))
    return jnp.concatenate(outs, axis=1)


def _compress_body(xk_ref, xv_ref, pe_ref, w1_ref, w2_ref, oe_ref, oo_ref, *, n_pairs):
    out = _compress_rows(xk_ref, xv_ref, pe_ref, w1_ref, w2_ref, n_pairs)
    oe_ref[0] = out[0:n_pairs, :]
    oo_ref[0] = out[n_pairs:, :]


BLOCKS_PER_PAGE = PAGE_SIZE // CMP_BLOCK


def _compress_paged_body(pt_ref, *refs, n_pages):
    page_refs = refs[:n_pages]
    pet_ref, perm_ref, w1_ref, w2_ref, o_ref, buf_ref = refs[n_pages:]
    grp = 2 * BLOCKS_PER_PAGE
    for jp in range(n_pages // 2):
        xt = jnp.concatenate([page_refs[2 * jp][0], page_refs[2 * jp + 1][0]], axis=1)
        xb = (xt + pet_ref[...]).astype(BF16)
        xp = _dot_nt(perm_ref[...], xb)
        for r in range(CMP_BLOCK):
            buf_ref[r, grp * jp:grp * (jp + 1), :] = xp[grp * r:grp * (r + 1), :]
    outs = []
    for kv in range(2):
        feats = slice(kv * KV_WIDTH, (kv + 1) * KV_WIDTH)
        acc = _dot(buf_ref[0, :, feats].astype(BF16), w1_ref[kv, 0])
        for r in range(1, CMP_BLOCK):
            acc = acc + _dot(buf_ref[r, :, feats].astype(BF16), w1_ref[kv, r])
        outs.append(_dot(_silu(acc).astype(BF16), w2_ref[kv]))
    o_ref[0] = jnp.concatenate(outs, axis=1)


def _page_pair_constants(pe):
    pe_t = jnp.broadcast_to(pe.transpose(0, 2, 1)[:, None, :, None, :],
                            (2, NSA_KV_HEADS, HEAD_DIM, 2 * BLOCKS_PER_PAGE, CMP_BLOCK))
    pe_t = pe_t.reshape(2 * KV_WIDTH, 2 * PAGE_SIZE)
    grp = 2 * BLOCKS_PER_PAGE
    perm = np.zeros((2 * PAGE_SIZE, 2 * PAGE_SIZE), np.float32)
    for r in range(CMP_BLOCK):
        for b in range(grp):
            perm[r * grp + b, b * CMP_BLOCK + r] = 1.0
    return pe_t, jnp.asarray(perm, BF16)


def _pack_compress_weights(pe, w1, w2):
    eye_h = jnp.eye(NSA_KV_HEADS, dtype=F32)
    pe_r = jnp.broadcast_to(pe[:, :, None, :], (2, CMP_BLOCK, NSA_KV_HEADS, HEAD_DIM))
    pe_r = pe_r.reshape(2, CMP_BLOCK, KV_WIDTH)
    w1r = w1.reshape(2, CMP_BLOCK, HEAD_DIM, CMP_HIDDEN)
    w1_big = jnp.einsum('krdc,hH->krhdHc', w1r, eye_h)
    w1_big = w1_big.reshape(2, CMP_BLOCK, KV_WIDTH, NSA_KV_HEADS * CMP_HIDDEN).astype(BF16)
    w2_big = jnp.einsum('kcd,hH->khcHd', w2, eye_h)
    w2_big = w2_big.reshape(2, NSA_KV_HEADS * CMP_HIDDEN, KV_WIDTH).astype(BF16)
    return pe_r, w1_big, w2_big


def _compress_prompt(k_rows, v_rows, cw, *, batch, seq):
    pe_r, w1_big, w2_big = cw
    n_pairs = seq // (2 * CMP_BLOCK)
    const3 = lambda b: (0, 0, 0)
    out_sd = jax.ShapeDtypeStruct((batch, n_pairs, 2 * KV_WIDTH), F32)
    return pl.pallas_call(
        functools.partial(_compress_body, n_pairs=n_pairs),
        grid=(batch,),
        in_specs=[pl.BlockSpec((seq, KV_WIDTH), lambda b: (b, 0)),
                  pl.BlockSpec((seq, KV_WIDTH), lambda b: (b, 0)),
                  pl.BlockSpec(pe_r.shape, const3),
                  pl.BlockSpec(w1_big.shape, lambda b: (0, 0, 0, 0)),
                  pl.BlockSpec(w2_big.shape, const3)],
        out_specs=[pl.BlockSpec((1, n_pairs, 2 * KV_WIDTH), lambda b: (b, 0, 0))] * 2,
        out_shape=[out_sd, out_sd],
        compiler_params=_cparams(("arbitrary",)),
        name="compress_prompt",
    )(k_rows, v_rows, pe_r, w1_big, w2_big)


COMPRESS_PAGES_PER_STEP = 32


def _compress_paged(pool, page_table, cw, cw_pages):
    _, w1_big, w2_big = cw
    pe_t, perm = cw_pages
    batch, n_pages = page_table.shape
    pps = COMPRESS_PAGES_PER_STEP
    assert n_pages % pps == 0 and pps % 2 == 0
    n_steps = n_pages // pps
    n_blk = pps * BLOCKS_PER_PAGE
    const3 = lambda b, c, pt: (0, 0, 0)

    def page_spec(j):
        return pl.BlockSpec((1, 2 * KV_WIDTH, PAGE_SIZE),
                            lambda b, c, pt: (pt[(b * n_steps + c) * pps + j], 0, 0))

    return pl.pallas_call(
        functools.partial(_compress_paged_body, n_pages=pps),
        grid_spec=pltpu.PrefetchScalarGridSpec(
            num_scalar_prefetch=1,
            grid=(batch, n_steps),
            in_specs=[page_spec(j) for j in range(pps)] + [
                pl.BlockSpec(pe_t.shape, lambda b, c, pt: (0, 0)),
                pl.BlockSpec(perm.shape, lambda b, c, pt: (0, 0)),
                pl.BlockSpec(w1_big.shape, lambda b, c, pt: (0, 0, 0, 0)),
                pl.BlockSpec(w2_big.shape, const3)],
            out_specs=pl.BlockSpec((1, n_blk, 2 * KV_WIDTH), lambda b, c, pt: (b, c, 0)),
            scratch_shapes=[pltpu.VMEM((CMP_BLOCK, n_blk, 2 * KV_WIDTH), F32)]),
        out_shape=jax.ShapeDtypeStruct((batch, n_steps * n_blk, 2 * KV_WIDTH), F32),
        compiler_params=_cparams(("arbitrary", "arbitrary")),
        name="compress_paged",
    )(page_table.reshape(-1), *([pool] * pps), pe_t, perm, w1_big, w2_big)


def _cmp_attn_body(q_ref, ke_ref, ko_ref, o_ref, st_ref, *, tq, pos0):
    ns = ke_ref.shape[1]
    i = pl.program_id(1)
    rows = NSA_GROUP * tq
    tok0 = pos0 + i * tq
    pos_c = tok0 + lax.broadcasted_iota(jnp.int32, (rows, 1), 0) % tq
    pos_r = tok0 + lax.broadcasted_iota(jnp.int32, (1, rows), 1) % tq
    pair_r = lax.broadcasted_iota(jnp.int32, (1, ns), 1)
    pair_c = lax.broadcasted_iota(jnp.int32, (ns, 1), 0)
    end_e = lambda pair: (2 * pair + 1) * CMP_BLOCK - 1
    end_o = lambda pair: (2 * pair + 2) * CMP_BLOCK - 1
    any_c = (CMP_BLOCK - 1 <= pos_c).astype(F32)
    any_r = (CMP_BLOCK - 1 <= pos_r).astype(F32)
    q = q_ref[...] * ATTN_SCALE
    for kh in range(NSA_KV_HEADS):
        qs = jnp.concatenate([q[:, (kh * NSA_GROUP + g) * HEAD_DIM:(kh * NSA_GROUP + g + 1) * HEAD_DIM]
                              for g in range(NSA_GROUP)], axis=0).astype(BF16)
        ks, vs = slice(kh * HEAD_DIM, (kh + 1) * HEAD_DIM), slice(KV_WIDTH + kh * HEAD_DIM,
                                                                   KV_WIDTH + (kh + 1) * HEAD_DIM)
        ke, ko = ke_ref[0, :, ks].astype(BF16), ko_ref[0, :, ks].astype(BF16)
        se = jnp.where(end_e(pair_r) <= pos_c, _dot_nt(qs, ke), NEG_INF)
        so = jnp.where(end_o(pair_r) <= pos_c, _dot_nt(qs, ko), NEG_INF)
        mx = jnp.maximum(jnp.max(se, axis=1, keepdims=True), jnp.max(so, axis=1, keepdims=True))
        pe, po = jnp.exp(se - mx), jnp.exp(so - mx)
        inv = any_c / (jnp.sum(pe, axis=1, keepdims=True) + jnp.sum(po, axis=1, keepdims=True))
        oh = (_dot((pe * inv).astype(BF16), ke_ref[0, :, vs].astype(BF16))
              + _dot((po * inv).astype(BF16), ko_ref[0, :, vs].astype(BF16)))
        for g in range(NSA_GROUP):
            hd = kh * NSA_GROUP + g
            o_ref[:, hd * HEAD_DIM:(hd + 1) * HEAD_DIM] = oh[g * tq:(g + 1) * tq, :]
        te = jnp.where(end_e(pair_c) <= pos_r, _dot_nt(ke, qs), NEG_INF)
        to = jnp.where(end_o(pair_c) <= pos_r, _dot_nt(ko, qs), NEG_INF)
        mt = jnp.maximum(jnp.max(te, axis=0, keepdims=True), jnp.max(to, axis=0, keepdims=True))
        pte, pto = jnp.exp(te - mt), jnp.exp(to - mt)
        invt = any_r / (jnp.sum(pte, axis=0, keepdims=True) + jnp.sum(pto, axis=0, keepdims=True))
        ps = (pte + pto) * invt
        score = ps[:, 0:tq]
        for g in range(1, NSA_GROUP):
            score = score + ps[:, g * tq:(g + 1) * tq]
        st_ref[0, kh] = score


def _cmp_attn(q2d, kce, kco, *, batch, seq, tq, pos0):
    ns = kce.shape[1]
    nq = seq // tq
    return pl.pallas_call(
        functools.partial(_cmp_attn_body, tq=tq, pos0=pos0),
        grid=(batch, nq),
        in_specs=[pl.BlockSpec((tq, NSA_WIDTH), lambda b, i: (b * nq + i, 0)),
                  pl.BlockSpec((1, ns, 2 * KV_WIDTH), lambda b, i: (b, 0, 0)),
                  pl.BlockSpec((1, ns, 2 * KV_WIDTH), lambda b, i: (b, 0, 0))],
        out_specs=[pl.BlockSpec((tq, NSA_WIDTH), lambda b, i: (b * nq + i, 0)),
                   pl.BlockSpec((1, NSA_KV_HEADS, ns, tq), lambda b, i: (b, 0, 0, i))],
        out_shape=[jax.ShapeDtypeStruct((batch * seq, NSA_WIDTH), F32),
                   jax.ShapeDtypeStruct((batch, NSA_KV_HEADS, ns, seq), F32)],
        compiler_params=_cparams(("arbitrary", "arbitrary")),
        name="cmp_attn",
    )(q2d, kce, kco)


def _topk_body(pos_ref, st_ref, b_ref, *, n_sel):
    score = st_ref[0]
    ns, tt = score.shape
    nsw = b_ref.shape[1]
    if nsw > ns:
        score = jnp.concatenate([score, jnp.zeros((nsw - ns, tt), F32)], axis=0)
    blk = lax.broadcasted_iota(jnp.int32, (nsw, 1), 0)
    blk_f = blk.astype(F32)
    cur = pos_ref[...] // SEL_BLOCK
    forced = (blk == 0) | (blk == cur) | (blk == cur - 1)
    pri = jnp.where(blk <= cur, jnp.where(forced, SEL_PRIORITY, score), -SEL_PRIORITY)
    pri = jnp.where(blk < n_sel, pri, -jnp.inf)
    bias = jnp.full((nsw, tt), NEG_INF, F32)
    for _ in range(min(TOP_N, n_sel)):
        top = jnp.max(pri, axis=0, keepdims=True)
        first = jnp.min(jnp.where(pri == top, blk_f, float(nsw)), axis=0, keepdims=True)
        hit = blk_f == first
        bias = jnp.where(hit, 0.0, bias)
        pri = jnp.where(hit, -jnp.inf, pri)
    b_ref[0] = bias


def _topk_blocks(scores_t, pos, *, n_sel, nsw, tt):
    groups, ns, tokens = scores_t.shape
    assert nsw >= max(ns, n_sel) and tokens % tt == 0
    return pl.pallas_call(
        functools.partial(_topk_body, n_sel=n_sel),
        grid=(groups, tokens // tt),
        in_specs=[pl.BlockSpec((1, tt), lambda g, i: (0, i)),
                  pl.BlockSpec((1, ns, tt), lambda g, i: (g, 0, i))],
        out_specs=pl.BlockSpec((1, nsw, tt), lambda g, i: (g, 0, i)),
        out_shape=jax.ShapeDtypeStruct((groups, nsw, tokens), F32),
        compiler_params=_cparams(("arbitrary", "arbitrary")),
        name="topk_blocks",
    )(pos, scores_t)


def _softmax_update(sc, vt_bf16, m_ref, l_ref, acc_ref):
    m_old = m_ref[...]
    m_new = jnp.maximum(m_old, jnp.max(sc, axis=1, keepdims=True))
    alpha = jnp.exp(m_old - m_new)
    pr = jnp.exp(sc - jnp.concatenate([m_new] * (sc.shape[1] // LANES), axis=1))
    l_ref[...] = alpha * l_ref[...] + jnp.sum(pr, axis=1, keepdims=True)
    acc_ref[...] = alpha * acc_ref[...] + _dot_nt(pr.astype(BF16), vt_bf16)
    m_ref[...] = m_new


def _softmax_init(m_ref, l_ref, acc_ref):
    m_ref[...] = jnp.full(m_ref.shape, NEG_INF, F32)
    l_ref[...] = jnp.zeros(l_ref.shape, F32)
    acc_ref[...] = jnp.zeros(acc_ref.shape, F32)


def _block_onehot_t(first_key, n_keys):
    blk = (first_key + lax.broadcasted_iota(jnp.int32, (1, n_keys), 1)) // SEL_BLOCK
    r = lax.broadcasted_iota(jnp.int32, (LANES, 1), 0) & (SEL_BLOCK - 1)
    return (r == blk).astype(F32)


ATTN_TAB_COLS = 5


def _attn_pairs(seq, tq, tk, window):
    rows = []
    for i in range(seq // tq):
        t_lo, t_hi = i * tq, i * tq + tq - 1
        k_lo = 0 if window is None else max(0, t_lo - window + 1)
        js = list(range(k_lo // tk, t_hi // tk + 1))
        for n, j in enumerate(js):
            partial_tile = j * tk + tk - 1 > t_lo or (window is not None and j * tk <= t_hi - window)
            rows.append((i, j, int(n == 0), int(n == len(js) - 1), int(partial_tile)))
    return np.asarray(rows, np.int32)


def _attn_body(tab_ref, q_ref, kv_ref, *rest, tq, tk, window, use_bias):
    if use_bias:
        sb_ref, o_ref, qa_ref, m_ref, l_ref, acc_ref = rest
    else:
        o_ref, qa_ref, m_ref, l_ref, acc_ref = rest
    p = pl.program_id(1)
    i, j, first, last, partial_tile = [tab_ref[ATTN_TAB_COLS * p + n] for n in range(ATTN_TAB_COLS)]
    G = NSA_GROUP
    rows = NSA_HEADS * tq

    lane_head = lax.broadcasted_iota(jnp.int32, (1, LANES), 1) // HEAD_DIM

    @pl.when(first == 1)
    def _():
        q = q_ref[0] * ATTN_SCALE
        if use_bias:
            bias_t = sb_ref[0].reshape(NSA_KV_HEADS * SEL_BLOCK, tq).T
        for hd in range(NSA_HEADS):
            kh = hd // G
            pair = q[:, (hd // 2) * LANES:(hd // 2 + 1) * LANES]
            if hd % 2 != kh:
                pair = pltpu.roll(pair, HEAD_DIM, axis=1)
            qa_ref[hd * tq:(hd + 1) * tq, 0:LANES] = jnp.where(lane_head == kh, pair, 0.0).astype(BF16)
            if use_bias:
                qa_ref[hd * tq:(hd + 1) * tq, LANES:] = (
                    jnp.where(lane_head == kh, bias_t, 0.0).astype(BF16))
        _softmax_init(m_ref, l_ref, acc_ref)

    kt = kv_ref[0, 0:KV_WIDTH, :]
    if use_bias:
        kt = jnp.concatenate([kt, _block_onehot_t(j * tk, tk)], axis=0)
    sc = _dot(qa_ref[...], kt.astype(BF16))
    vt = kv_ref[0, KV_WIDTH:, :].astype(BF16)

    @pl.when(partial_tile == 1)
    def _():
        qpos = i * tq + (lax.broadcasted_iota(jnp.int32, (rows, 1), 0) & (tq - 1))
        kpos = j * tk + lax.broadcasted_iota(jnp.int32, (1, tk), 1)
        valid = kpos <= qpos
        if window is not None:
            valid = valid & (kpos > qpos - window)
        _softmax_update(jnp.where(valid, sc, NEG_INF), vt, m_ref, l_ref, acc_ref)

    @pl.when(partial_tile == 0)
    def _():
        _softmax_update(sc, vt, m_ref, l_ref, acc_ref)

    @pl.when(last == 1)
    def _():
        o = acc_ref[...] / l_ref[...]

        def head_at(hd, slot):
            oh = o[hd * tq:(hd + 1) * tq, :]
            return oh if hd // G == slot else pltpu.roll(oh, HEAD_DIM, axis=1)

        for m in range(NSA_HEADS // 2):
            o_ref[0, :, m * LANES:(m + 1) * LANES] = jnp.where(
                lane_head == 0, head_at(2 * m, 0), head_at(2 * m + 1, 1))


def _attn_prompt(q3d, kv_t, selb, *, tq, tk, window):
    batch, seq, _ = q3d.shape
    assert tq & (tq - 1) == 0 and tk % LANES == 0 and tq % LANES == 0
    use_bias = selb is not None
    assert not use_bias or selb.shape[2] == SEL_BLOCK
    tab = _attn_pairs(seq, tq, tk, window)
    depth = 2 * LANES if use_bias else LANES
    rows = NSA_HEADS * tq
    C = ATTN_TAB_COLS
    in_specs = [pl.BlockSpec((1, tq, NSA_WIDTH), lambda b, p, t: (b, t[C * p], 0)),
                pl.BlockSpec((1, 2 * KV_WIDTH, tk), lambda b, p, t: (b, 0, t[C * p + 1]))]
    args = [q3d, kv_t]
    if use_bias:
        in_specs.append(pl.BlockSpec((1, NSA_KV_HEADS, SEL_BLOCK, tq),
                                     lambda b, p, t: (b, 0, 0, t[C * p])))
        args.append(selb)
    return pl.pallas_call(
        functools.partial(_attn_body, tq=tq, tk=tk, window=window, use_bias=use_bias),
        grid_spec=pltpu.PrefetchScalarGridSpec(
            num_scalar_prefetch=1,
            grid=(batch, tab.shape[0]),
            in_specs=in_specs,
            out_specs=pl.BlockSpec((1, tq, NSA_WIDTH), lambda b, p, t: (b, t[C * p], 0)),
            scratch_shapes=[pltpu.VMEM((rows, depth), BF16), pltpu.VMEM((rows, LANES), F32),
                            pltpu.VMEM((rows, LANES), F32), pltpu.VMEM((rows, LANES), F32)]),
        out_shape=jax.ShapeDtypeStruct((batch, seq, NSA_WIDTH), F32),
        compiler_params=_cparams(("arbitrary", "arbitrary")),
        name="attn_sel" if use_bias else "attn_win",
    )(jnp.asarray(tab.reshape(-1)), *args)


ATTN_PAGES_PER_STEP = 16


def _attn_paged_body(pt_ref, qa_ref, bq_ref, bn_ref, kn_ref, *rest, n_pages, n_new):
    page_refs = rest[:n_pages]
    o_ref, m_ref, l_ref, acc_ref = rest[n_pages:]
    c = pl.program_id(1)
    rows = qa_ref.shape[1]

    @pl.when(c == 0)
    def _():
        _softmax_init(m_ref, l_ref, acc_ref)

    keys = n_pages * PAGE_SIZE
    kt = jnp.concatenate([r[0, 0:KV_WIDTH, :] for r in page_refs], axis=1)
    vt = jnp.concatenate([r[0, KV_WIDTH:, :] for r in page_refs], axis=1).astype(BF16)
    rhs = jnp.concatenate([kt, _block_onehot_t(0, keys)], axis=0).astype(BF16)
    qa = qa_ref[0]
    lhs = jnp.concatenate([qa, bq_ref[0, 0]], axis=1).astype(BF16)
    _softmax_update(_dot(lhs, rhs), vt, m_ref, l_ref, acc_ref)

    @pl.when(c == pl.num_programs(1) - 1)
    def _():
        kn = kn_ref[0]
        sc = _dot(qa.astype(BF16), kn[0:KV_WIDTH, :].astype(BF16)) + bn_ref[0]
        tq = lax.broadcasted_iota(jnp.int32, (rows, 1), 0) % n_new
        kk = lax.broadcasted_iota(jnp.int32, (1, kn.shape[1]), 1)
        sc = jnp.where((kk <= tq) & (kk < n_new), sc, NEG_INF)
        _softmax_update(sc, kn[KV_WIDTH:, :].astype(BF16), m_ref, l_ref, acc_ref)
        o_ref[0] = acc_ref[...] / l_ref[...]


def _attn_paged(qa, bias_q, bias_new, kv_new_t, pool, page_table, *, n_new):
    batch, n_pages = page_table.shape
    pps = ATTN_PAGES_PER_STEP
    assert n_pages % pps == 0 and pps * PAGE_SIZE // SEL_BLOCK <= SEL_BLOCK
    n_steps = n_pages // pps
    rows = qa.shape[1]

    def page_spec(j):
        return pl.BlockSpec((1, 2 * KV_WIDTH, PAGE_SIZE),
                            lambda b, c, pt: (pt[(b * n_steps + c) * pps + j], 0, 0))

    per_b = lambda b, c, pt: (b, 0, 0)
    return pl.pallas_call(
        functools.partial(_attn_paged_body, n_pages=pps, n_new=n_new),
        grid_spec=pltpu.PrefetchScalarGridSpec(
            num_scalar_prefetch=1,
            grid=(batch, n_steps),
            in_specs=[pl.BlockSpec((1, rows, LANES), per_b),
                      pl.BlockSpec((1, 1, rows, LANES), lambda b, c, pt: (b, c, 0, 0)),
                      pl.BlockSpec((1, rows, LANES), per_b),
                      pl.BlockSpec((1,) + kv_new_t.shape[1:], per_b)]
            + [page_spec(j) for j in range(pps)],
            out_specs=pl.BlockSpec((1, rows, LANES), per_b),
            scratch_shapes=[pltpu.VMEM((rows, LANES), F32), pltpu.VMEM((rows, LANES), F32),
                            pltpu.VMEM((rows, LANES), F32)]),
        out_shape=jax.ShapeDtypeStruct((batch, rows, LANES), F32),
        compiler_params=_cparams(("arbitrary", "arbitrary")),
        name="attn_sel_paged",
    )(page_table.reshape(-1), qa, bias_q, bias_new, kv_new_t, *([pool] * pps))


def _attn_window_body(qa_ref, wb_ref, kn_ref, o_ref, *, n_new, past):
    qa = qa_ref[0].astype(BF16)
    wb, kn = wb_ref[0], kn_ref[0]
    rows, n_buf = qa.shape[0], wb.shape[1]
    qpos = past + lax.broadcasted_iota(jnp.int32, (rows, 1), 0) % n_new

    def masked(sc, kpos, extra):
        diff = qpos - kpos
        return jnp.where((diff >= 0) & (diff < WINDOW) & (kpos >= 0) & extra, sc, NEG_INF)

    nb = lax.broadcasted_iota(jnp.int32, (1, n_buf), 1)
    nn = lax.broadcasted_iota(jnp.int32, (1, kn.shape[1]), 1)
    sb = masked(_dot(qa, wb[0:KV_WIDTH, :].astype(BF16)), past - n_buf + nb, nb >= 0)
    sn = masked(_dot(qa, kn[0:KV_WIDTH, :].astype(BF16)), past + nn, nn < n_new)
    mx = jnp.maximum(jnp.max(sb, axis=1, keepdims=True), jnp.max(sn, axis=1, keepdims=True))
    pb, pn = jnp.exp(sb - mx), jnp.exp(sn - mx)
    o = (_dot_nt(pb.astype(BF16), wb[KV_WIDTH:, :].astype(BF16))
         + _dot_nt(pn.astype(BF16), kn[KV_WIDTH:, :].astype(BF16)))
    o_ref[0] = o / (jnp.sum(pb, axis=1, keepdims=True) + jnp.sum(pn, axis=1, keepdims=True))


def _attn_window_small(qa, win_t, kv_new_t, *, n_new, past):
    batch, rows, _ = qa.shape
    per_b = lambda b: (b, 0, 0)
    return pl.pallas_call(
        functools.partial(_attn_window_body, n_new=n_new, past=past),
        grid=(batch,),
        in_specs=[pl.BlockSpec((1, rows, LANES), per_b),
                  pl.BlockSpec((1,) + win_t.shape[1:], per_b),
                  pl.BlockSpec((1,) + kv_new_t.shape[1:], per_b)],
        out_specs=pl.BlockSpec((1, rows, LANES), per_b),
        out_shape=jax.ShapeDtypeStruct((batch, rows, LANES), F32),
        compiler_params=_cparams(("arbitrary",)),
        name="attn_win_small",
    )(qa, win_t, kv_new_t)


FFN_TM = 256
MXU_DEPTH = 256
FFN_CHUNKS = ((0, 6 * MXU_DEPTH), (6 * MXU_DEPTH, D_FF))


def _ffn_body(x_ref, om_ref, oc_ref, os_ref, ow_ref, gt_ref, ge_ref, gn_ref, gf_ref, gl_ref, wc_ref,
              fb_ref, wo_hbm, wu_hbm, wd_hbm, y_ref, fn_ref, xx_ref, wo_ref, wu_ref, wd_ref, sem_ref,
              *, tm, stride, halo):
    s = pl.program_id(1)

    @pl.when((pl.program_id(0) == 0) & (s == 0))
    def _():
        copies = [pltpu.make_async_copy(src, dst, sem_ref.at[n])
                  for n, (src, dst) in enumerate(((wo_hbm, wo_ref), (wu_hbm, wu_ref), (wd_hbm, wd_ref)))]
        for cp in copies:
            cp.start()
        for cp in copies:
            cp.wait()

    sig = _sigmoid(gt_ref[...])
    hi = sig.astype(BF16)
    lo = (sig - hi.astype(F32)).astype(BF16)
    comb = None
    for br, ob_ref in enumerate((oc_ref, os_ref, ow_ref)):
        gate = _dot(hi, ge_ref[br]) + _dot(lo, ge_ref[br])
        term = gate * ob_ref[...]
        comb = term if comb is None else comb + term
    onsa = _rms(comb, gn_ref[...])
    h = (x_ref[...] + _dot(om_ref[...].astype(BF16), wo_ref[0:MLSTM_WIDTH, :])
         + _dot(onsa.astype(BF16), wo_ref[MLSTM_WIDTH:, :]))
    hn = _rms(h, gf_ref[...]).astype(BF16)

    base = halo - (FFN_CONV - 1) * stride

    @pl.when(s == 0)
    def _():
        xx_ref[base:halo, :] = fb_ref[0]

    y_ref[...] = h
    for lo_col, hi_col in FFN_CHUNKS:
        convs = []
        for half in range(2):
            cols = slice(half * D_FF + lo_col, half * D_FF + hi_col)
            xx_ref[halo:halo + tm, cols] = _dot(hn, wu_ref[:, cols])
            conv = xx_ref[base:base + tm, cols] * wc_ref[0:1, cols]
            for j in range(1, FFN_CONV):
                conv = conv + xx_ref[base + j * stride:base + j * stride + tm, cols] * wc_ref[j:j + 1, cols]
            convs.append(conv)
        act = _silu(convs[1]) * convs[0]
        y_ref[...] += _dot(act.astype(BF16), wd_ref[lo_col:hi_col, :])
    fn_ref[0, 0] = xx_ref[tm + base:tm + halo, :]
    xx_ref[0:halo, :] = xx_ref[tm:tm + halo, :]
    y_ref[...] = _rms(y_ref[...], gl_ref[...])


def _gate_expand():
    ge = np.zeros((N_BRANCH, LANES, NSA_WIDTH), np.float32)
    for hd in range(NSA_HEADS):
        for br in range(N_BRANCH):
            ge[br, GATE_COL_NSA + hd * N_BRANCH + br, hd * HEAD_DIM:(hd + 1) * HEAD_DIM] = 1.0
    return jnp.asarray(ge, BF16)


def _ffn(x2d, om, oc, osel, ow, gt, fbuf, w_out, g_nsa, g_ffn, g_final, w_up, w_fconv, w_down,
         *, nb, tm, stride):
    rows = x2d.shape[0]
    ns = rows // (nb * tm)
    halo = -(-(FFN_CONV - 1) * stride // SUBLANES) * SUBLANES
    assert tm >= halo and all((hi - lo) % MXU_DEPTH == 0 for lo, hi in FFN_CHUNKS)
    tok = lambda b, s: (b * ns + s, 0)
    nfb = (FFN_CONV - 1) * stride

    def const(shape):
        return pl.BlockSpec(shape, lambda b, s: (0,) * len(shape))

    hbm = pl.BlockSpec(memory_space=pl.ANY)
    y, fn = pl.pallas_call(
        functools.partial(_ffn_body, tm=tm, stride=stride, halo=halo),
        grid=(nb, ns),
        in_specs=[pl.BlockSpec((tm, D_MODEL), tok)] + [pl.BlockSpec((tm, NSA_WIDTH), tok)] * 4
        + [pl.BlockSpec((tm, LANES), tok),
           const((N_BRANCH, LANES, NSA_WIDTH)), const((1, NSA_WIDTH)), const((1, D_MODEL)),
           const((1, D_MODEL)), const((FFN_CONV, 2 * D_FF)),
           pl.BlockSpec((1, nfb, 2 * D_FF), lambda b, s: (b, 0, 0)), hbm, hbm, hbm],
        out_specs=[pl.BlockSpec((tm, D_MODEL), tok),
                   pl.BlockSpec((1, 1, nfb, 2 * D_FF), lambda b, s: (b, s, 0, 0))],
        out_shape=[jax.ShapeDtypeStruct((rows, D_MODEL), F32),
                   jax.ShapeDtypeStruct((nb, ns, nfb, 2 * D_FF), F32)],
        scratch_shapes=[pltpu.VMEM((halo + tm, 2 * D_FF), F32),
                        pltpu.VMEM((D_MODEL, D_MODEL), BF16), pltpu.VMEM((D_MODEL, 2 * D_FF), BF16),
                        pltpu.VMEM((D_FF, D_MODEL), BF16), pltpu.SemaphoreType.DMA((3,))],
        compiler_params=_cparams(("arbitrary", "arbitrary")),
        name="outproj_ffn",
    )(x2d, om, oc, osel, ow, gt, _gate_expand(), g_nsa.reshape(1, -1), g_ffn.reshape(1, -1),
      g_final.reshape(1, -1), w_fconv, fbuf, w_out.astype(BF16), w_up.astype(BF16),
      w_down.astype(BF16))
    return y, fn[:, ns - 1]


PROMPT_TM = 512
PROMPT_TQ_CMP = 256
PROMPT_TQ = 128
PROMPT_TK_SEL = 512
PROMPT_TK_WIN = 256


def _kv_rows(kv_t):
    batch, _, rows = kv_t.shape
    return kv_t.reshape(batch, 2, NSA_KV_HEADS, HEAD_DIM, rows).transpose(0, 4, 1, 2, 3)


def _kv_feature_major(kv5):
    batch, rows = kv5.shape[:2]
    return kv5.transpose(0, 2, 3, 4, 1).reshape(batch, 2 * KV_WIDTH, rows)


def _prompt_layer(x, wts):
    batch, seq, _ = x.shape
    x2d = x.reshape(batch * seq, D_MODEL)
    q, kc_rows, vc_rows, mu, mv, mo, gt, kvc_t, kvs_t, kvw_t = _in_proj(
        x2d, wts["g_mix"], wts["w_in_packed"], batch=batch, seq=seq, tm=min(PROMPT_TM, seq))
    H, DH, W = MLSTM_HEADS, MLSTM_DH, MLSTM_WIDTH
    o_m, mconv, c_new, n_new, m_new = _mlstm(
        mu, mv, mo, gt, jnp.zeros((batch, MLSTM_CONV - 1, W), F32), jnp.zeros((batch, H, DH, DH), F32),
        jnp.zeros((batch, H, DH), F32), jnp.zeros((batch, H), F32),
        wts["w_mconv"], wts["b_mconv"], wts["w_mq"], wts["w_mk"], wts["b_ig"], wts["b_fg"],
        wts["g_mhead"], wts["m_skip"], batch=batch, seq=seq)
    kce, kco = _compress_prompt(kc_rows, vc_rows, wts["cw"], batch=batch, seq=seq)
    n_sel = -(-seq // SEL_BLOCK)
    assert n_sel <= SEL_BLOCK
    o_cmp, scores_t = _cmp_attn(q, kce, kco, batch=batch, seq=seq, tq=min(PROMPT_TQ_CMP, seq), pos0=0)
    selb = _topk_blocks(scores_t.reshape(batch * NSA_KV_HEADS, -1, seq),
                        jnp.arange(seq, dtype=jnp.int32).reshape(1, seq),
                        n_sel=n_sel, nsw=SEL_BLOCK, tt=min(PROMPT_TM, seq))
    selb = selb.reshape(batch, NSA_KV_HEADS, SEL_BLOCK, seq)
    q3d = q.reshape(batch, seq, NSA_WIDTH)
    o_sel = _attn_prompt(q3d, kvs_t, selb, tq=PROMPT_TQ, tk=min(PROMPT_TK_SEL, seq), window=None)
    o_win = _attn_prompt(q3d, kvw_t, None, tq=PROMPT_TQ, tk=PROMPT_TK_WIN, window=WINDOW)
    fbuf = jnp.zeros((batch, FFN_CONV - 1, 2 * D_FF), F32)
    y, f_new = _ffn(x2d, o_m, o_cmp, o_sel.reshape(-1, NSA_WIDTH), o_win.reshape(-1, NSA_WIDTH), gt,
                    fbuf, wts["w_out"], wts["g_nsa"], wts["g_ffn"], wts["g_final"], wts["w_up"],
                    wts["w_fconv"], wts["w_down"], nb=batch, tm=min(FFN_TM, seq), stride=1)
    n_win = min(WINDOW, seq)
    return (y.reshape(batch, seq, D_MODEL), _kv_rows(kvc_t), _kv_rows(kvs_t),
            _kv_rows(kvw_t[:, :, seq - n_win:]), mconv, c_new, n_new, m_new.reshape(batch, H), f_new)


def _decode_rows(q2d, batch, seq):
    q5 = (q2d * ATTN_SCALE).reshape(batch, seq, NSA_KV_HEADS, NSA_GROUP, HEAD_DIM).transpose(0, 2, 3, 1, 4)
    eye = jnp.eye(NSA_KV_HEADS, dtype=F32)
    qa = jnp.einsum('bkgtd,kK->bkgtKd', q5, eye)
    return qa.reshape(batch, NSA_KV_HEADS * NSA_GROUP * seq, KV_WIDTH)


def _decode_rows_out(o, batch, seq):
    o6 = o.reshape(batch, NSA_KV_HEADS, NSA_GROUP, seq, NSA_KV_HEADS, HEAD_DIM)
    o5 = jnp.stack([o6[:, kh, :, :, kh, :] for kh in range(NSA_KV_HEADS)], axis=1)
    return o5.transpose(0, 3, 1, 2, 4).reshape(batch * seq, NSA_WIDTH)


def _sample_layer(x, pool_cmp, pool_sel, win_buf, m_conv, m_c, m_n, m_m, f_buf, page_table, wts):
    batch, seq, _ = x.shape
    n_pages = page_table.shape[1]
    past = n_pages * PAGE_SIZE
    assert past % SEL_BLOCK == 0 and seq <= SEL_BLOCK and seq < CMP_BLOCK
    x2d = x.reshape(batch * seq, D_MODEL)
    q, _, _, mu, mv, mo, gt, kvc_t, kvs_t, kvw_t = _in_proj(
        x2d, wts["g_mix"], wts["w_in_packed"], batch=1, seq=batch * seq, tm=batch * seq)
    per_batch = lambda a: a.reshape(2 * KV_WIDTH, batch, seq).transpose(1, 0, 2)
    kvc_t, kvs_t, kvw_t = per_batch(kvc_t), per_batch(kvs_t), per_batch(kvw_t)
    pad_keys = lambda a: jnp.pad(a, ((0, 0), (0, 0), (0, LANES - seq)))
    H = MLSTM_HEADS
    o_m, mconv, c_new, n_new, m_new = _mlstm(
        mu, mv, mo, gt, m_conv, m_c, m_n, m_m,
        wts["w_mconv"], wts["b_mconv"], wts["w_mq"], wts["w_mk"], wts["b_ig"], wts["b_fg"],
        wts["g_mhead"], wts["m_skip"], batch=batch, seq=seq)
    pool_cmp3, pool_sel3 = _kv_feature_major(pool_cmp), _kv_feature_major(pool_sel)
    kc_all = _compress_paged(pool_cmp3, page_table, wts["cw"], wts["cw_pages"])
    kce, kco = kc_all[:, 0::2], kc_all[:, 1::2]
    n_past_blk = past // SEL_BLOCK
    n_sel = -(-(past + seq) // SEL_BLOCK)
    o_cmp, scores_t = _cmp_attn(q, kce, kco, batch=batch, seq=seq, tq=seq, pos0=past)
    ns = scores_t.shape[2]
    nsw = ns + LANES
    scores_all = scores_t.transpose(1, 2, 0, 3).reshape(NSA_KV_HEADS, ns, batch * seq)
    pos_all = (past + jnp.arange(batch * seq, dtype=jnp.int32) % seq).reshape(1, batch * seq)
    selb = _topk_blocks(scores_all, pos_all, n_sel=n_sel, nsw=nsw, tt=batch * seq)
    selb = selb.reshape(NSA_KV_HEADS, nsw, batch, seq).transpose(2, 0, 3, 1)
    qa = _decode_rows(q, batch, seq)
    rows = qa.shape[1]
    blk_per_step = ATTN_PAGES_PER_STEP * PAGE_SIZE // SEL_BLOCK
    n_steps = n_pages // ATTN_PAGES_PER_STEP
    sb_rows = jnp.broadcast_to(selb[:, :, None], (batch, NSA_KV_HEADS, NSA_GROUP, seq, selb.shape[-1]))
    sb_rows = sb_rows.reshape(batch, rows, selb.shape[-1])
    bias_q = sb_rows[:, :, :n_past_blk].reshape(batch, rows, n_steps, blk_per_step).transpose(0, 2, 1, 3)
    bias_q = jnp.pad(bias_q, ((0, 0), (0, 0), (0, 0), (0, LANES - blk_per_step)))
    bias_new = jnp.broadcast_to(sb_rows[:, :, n_past_blk:n_past_blk + 1], (batch, rows, LANES))
    o_sel = _attn_paged(qa, bias_q, bias_new, pad_keys(kvs_t), pool_sel3, page_table, n_new=seq)
    n_buf = win_buf.shape[1]
    assert past >= n_buf
    win_t = _kv_feature_major(win_buf)
    o_win = _attn_window_small(qa, win_t, pad_keys(kvw_t), n_new=seq, past=past)
    win_new = jnp.concatenate([win_t, kvw_t], axis=2)[:, :, seq:]
    tmaj = lambda a: a.reshape(batch, seq, -1).transpose(1, 0, 2).reshape(batch * seq, -1)
    fb_t = f_buf.transpose(1, 0, 2).reshape(1, (FFN_CONV - 1) * batch, 2 * D_FF)
    y, f_new = _ffn(tmaj(x2d), tmaj(o_m), tmaj(o_cmp), tmaj(_decode_rows_out(o_sel, batch, seq)),
                    tmaj(_decode_rows_out(o_win, batch, seq)), tmaj(gt), fb_t,
                    wts["w_out"], wts["g_nsa"], wts["g_ffn"], wts["g_final"], wts["w_up"],
                    wts["w_fconv"], wts["w_down"], nb=1, tm=batch * seq, stride=batch)
    y = y.reshape(seq, batch, D_MODEL).transpose(1, 0, 2)
    f_new = f_new.reshape(FFN_CONV - 1, batch, 2 * D_FF).transpose(1, 0, 2)
    return (y, _kv_rows(kvc_t), _kv_rows(kvs_t), _kv_rows(win_new), mconv, c_new, n_new,
            m_new.reshape(batch, H), f_new)


def kernel(x_prompt, x_sample, cache_cmp, cache_sel, state_win, state_mlstm_C, state_mlstm_n,
           state_mlstm_m, state_mlstm_conv, state_ffn_conv, page_table,
           g_mix, w_in, w_out, w_mconv, b_mconv, w_mq, w_mk, b_ig, b_fg, g_mhead, m_skip,
           pe_cmp, w_cmp1, w_cmp2, g_nsa, g_ffn, w_up, w_fconv, w_down, g_final):
    assert w_in.shape[0] == 1, "one layer: the final norm is fused into the layer's FFN kernel"
    l = 0
    wts = dict(g_mix=g_mix[l], w_in_packed=_pack_w_in(w_in[l]), w_out=w_out[l], w_mconv=w_mconv[l],
               b_mconv=b_mconv[l], w_mq=w_mq[l], w_mk=w_mk[l], b_ig=b_ig[l], b_fg=b_fg[l],
               g_mhead=g_mhead[l], m_skip=m_skip[l],
               cw=_pack_compress_weights(pe_cmp[l], w_cmp1[l], w_cmp2[l]),
               cw_pages=_page_pair_constants(pe_cmp[l]),
               g_nsa=g_nsa[l], g_ffn=g_ffn[l], g_final=g_final, w_up=w_up[l], w_fconv=w_fconv[l],
               w_down=w_down[l])
    p = _prompt_layer(x_prompt, wts)
    s = _sample_layer(x_sample, cache_cmp[l], cache_sel[l], state_win[l], state_mlstm_conv[l],
                      state_mlstm_C[l], state_mlstm_n[l], state_mlstm_m[l], state_ffn_conv[l],
                      page_table, wts)
    yp, cmp_p, sel_p, win_p, mconv_p, c_p, n_p, m_p, fconv_p = p
    ys, cmp_s, sel_s, win_s, mconv_s, c_s, n_s, m_s, fconv_s = s
    st = lambda a: a[None]
    return (yp, ys, st(cmp_p), st(cmp_s), st(sel_p), st(sel_s), st(win_p), st(win_s),
            st(c_p), st(c_s), st(n_p), st(n_s), st(m_p), st(m_s), st(mconv_p), st(mconv_s),
            st(fconv_p), st(fconv_s))
```

```python
import functools

import numpy as np
import jax
import jax.numpy as jnp
from jax import lax
from jax.experimental import pallas as pl
from jax.experimental.pallas import tpu as pltpu

F32 = jnp.float32
BF16 = jnp.bfloat16

D_MODEL = 1024
PAGE_SIZE = 128
HEAD_DIM = 64
NSA_HEADS = 8
NSA_KV_HEADS = 2
NSA_GROUP = NSA_HEADS // NSA_KV_HEADS
NSA_WIDTH = NSA_HEADS * HEAD_DIM
KV_WIDTH = NSA_KV_HEADS * HEAD_DIM
CMP_BLOCK = 32
CMP_HIDDEN = 2 * HEAD_DIM
SEL_BLOCK = 64
TOP_N = 16
WINDOW = 512
N_BRANCH = 3
ATTN_SCALE = HEAD_DIM ** -0.5
MLSTM_HEADS = 4
MLSTM_WIDTH = D_MODEL - NSA_WIDTH
MLSTM_DH = MLSTM_WIDTH // MLSTM_HEADS
MLSTM_CONV = 4
D_FF = ((8 * D_MODEL // 3 + 127) // 128) * 128
FFN_CONV = 3
EPS = 1e-6
NEG_INF = -1e30
SEL_PRIORITY = 1e4
LOG2_E = 1.4426950408889634

LANES = 128
SUBLANES = 8
VMEM_LIMIT = 48 * 1024 * 1024

GATE_COL_NSA = 0
GATE_COL_I = NSA_HEADS * N_BRANCH
GATE_COL_F = GATE_COL_I + MLSTM_HEADS

MLSTM_CHUNK = 128
MLSTM_SEQS_PER_STEP = 4


def _cparams(sem):
    return pltpu.CompilerParams(dimension_semantics=sem, vmem_limit_bytes=VMEM_LIMIT)


def _dot(a, b):
    return jnp.dot(a, b, preferred_element_type=F32)


def _dot_nt(a, b):
    return lax.dot_general(a, b, (((1,), (1,)), ((), ())), preferred_element_type=F32)


def _sigmoid(x):
    return 1.0 / (1.0 + jnp.exp(-x))


def _silu(x):
    return x * _sigmoid(x)


def _rms(x, g):
    return x * lax.rsqrt(jnp.mean(x * x, axis=-1, keepdims=True) + EPS) * g


IN_ROW_WIDTHS = (NSA_WIDTH, KV_WIDTH, KV_WIDTH, MLSTM_WIDTH, MLSTM_WIDTH, MLSTM_WIDTH, LANES)
N_KV_BRANCH = 3


def _inproj_body(x_ref, g_ref, w_ref, wt_ref, *out_refs):
    xb = _rms(x_ref[...], g_ref[...]).astype(BF16)
    off = 0
    for ref in out_refs[:len(IN_ROW_WIDTHS)]:
        n = ref.shape[-1]
        ref[...] = _dot(xb, w_ref[:, off:off + n])
        off += n
    for n, ref in enumerate(out_refs[len(IN_ROW_WIDTHS):]):
        ref[0] = _dot_nt(wt_ref[n * 2 * KV_WIDTH:(n + 1) * 2 * KV_WIDTH, :], xb)


def _pack_w_in(w_in):
    splits = np.cumsum([NSA_WIDTH, 2 * KV_WIDTH, 2 * KV_WIDTH, 2 * KV_WIDTH, NSA_HEADS * N_BRANCH,
                        MLSTM_WIDTH, MLSTM_WIDTH, MLSTM_WIDTH, MLSTM_HEADS]).tolist()
    q, kvc, kvs, kvw, gt, mu, mv, mo, mi, mf = jnp.split(w_in, splits, axis=1)
    gates = jnp.concatenate([gt, mi, mf], axis=1)
    gates = jnp.pad(gates, ((0, 0), (0, LANES - gates.shape[1])))
    w_rows = jnp.concatenate([q, kvc, mu, mv, mo, gates], axis=1).astype(BF16)
    w_kv_t = jnp.concatenate([kvc, kvs, kvw], axis=1).T.astype(BF16)
    return w_rows, w_kv_t


def _in_proj(x2d, g_mix, w_packed, *, batch, seq, tm):
    w_rows, w_kv_t = w_packed
    t = x2d.shape[0]
    ns = seq // tm
    kv_sd = jax.ShapeDtypeStruct((batch, 2 * KV_WIDTH, seq), F32)
    return pl.pallas_call(
        _inproj_body,
        grid=(t // tm,),
        in_specs=[pl.BlockSpec((tm, D_MODEL), lambda i: (i, 0)),
                  pl.BlockSpec((1, D_MODEL), lambda i: (0, 0)),
                  pl.BlockSpec(w_rows.shape, lambda i: (0, 0)),
                  pl.BlockSpec(w_kv_t.shape, lambda i: (0, 0))],
        out_specs=[pl.BlockSpec((tm, n), lambda i: (i, 0)) for n in IN_ROW_WIDTHS]
        + [pl.BlockSpec((1, 2 * KV_WIDTH, tm), lambda i: (i // ns, 0, i % ns))] * N_KV_BRANCH,
        out_shape=[jax.ShapeDtypeStruct((t, n), F32) for n in IN_ROW_WIDTHS] + [kv_sd] * N_KV_BRANCH,
        compiler_params=_cparams(("arbitrary",)),
        name="in_proj",
    )(x2d, g_mix.reshape(1, D_MODEL), w_rows, w_kv_t)


def _mlstm_body(*refs, valid, bb):
    cb_ref, c0_ref, n0_ref, m0_ref = refs[4:8]
    cn_ref, c_ref, n_ref, m_ref, xx_ref = refs[16:21]
    halo = SUBLANES

    @pl.when(pl.program_id(1) == 0)
    def _():
        xx_ref[:, 0:halo, :] = jnp.zeros((bb, halo, MLSTM_WIDTH), F32)
        xx_ref[:, halo - (MLSTM_CONV - 1):halo, :] = cb_ref[...]
        c_ref[...] = c0_ref[...]
        n_ref[...] = n0_ref[...]
        m_ref[...] = m0_ref[...]

    for bi in range(bb):
        _mlstm_sequence(bi, *refs, valid=valid)


def _mlstm_sequence(bi, mu_ref, mv_ref, mo_ref, g_ref, cb_ref, c0_ref, n0_ref, m0_ref,
                    wc_ref, bc_ref, wq_ref, wk_ref, gb_ref, gh_ref, sk_ref,
                    o_ref, cn_ref, c_ref, n_ref, m_ref,
                    xx_ref, vpad_ref, gpad_ref, *, valid):
    L = MLSTM_CHUNK
    DH = MLSTM_DH
    halo = SUBLANES

    if valid < L:
        xx_ref[bi, halo:, :] = jnp.zeros((L, MLSTM_WIDTH), F32)
        vpad_ref[bi] = jnp.zeros((L, MLSTM_WIDTH), F32)
        gpad_ref[bi] = jnp.zeros((L, LANES), F32)
    xx_ref[bi, halo:halo + valid, :] = mu_ref[bi]
    vpad_ref[bi, 0:valid, :] = mv_ref[bi]
    gpad_ref[bi, 0:valid, :] = g_ref[bi]

    conv = xx_ref[bi, halo - 3:halo - 3 + L, :] * wc_ref[0:1, :]
    for j in range(1, MLSTM_CONV):
        conv = conv + xx_ref[bi, halo - 3 + j:halo - 3 + j + L, :] * wc_ref[j:j + 1, :]
    uc = _silu(conv + bc_ref[...])

    tail = xx_ref[bi, valid + halo - 3:valid + halo, :]
    xx_ref[bi, halo - 3:halo, :] = tail
    cn_ref[bi] = tail

    gb = gpad_ref[bi] + gb_ref[...]
    gbt = gb.T
    row = lax.broadcasted_iota(jnp.int32, (L, L), 0)
    col = lax.broadcasted_iota(jnp.int32, (L, L), 1)
    tril = row >= col
    triu = row <= col
    tok_col = lax.broadcasted_iota(jnp.int32, (L, 1), 0)
    tok_row = lax.broadcasted_iota(jnp.int32, (1, L), 1)

    def log_sigmoid(x):
        return jnp.minimum(x, 0.0) - jnp.log(1.0 + jnp.exp(-jnp.abs(x)))

    for h in range(MLSTM_HEADS):
        sl_h = slice(h * DH, (h + 1) * DH)
        u_h = uc[:, sl_h]
        ub = u_h.astype(BF16)
        q = _dot(ub, wq_ref[h])
        k = _dot(ub, wk_ref[h]) * (DH ** -0.5)
        v = vpad_ref[bi, :, sl_h]
        qb, kb = q.astype(BF16), k.astype(BF16)

        ic_col = gb[:, GATE_COL_I + h:GATE_COL_I + h + 1]
        ic_row = gbt[GATE_COL_I + h:GATE_COL_I + h + 1, :]
        lf_col = log_sigmoid(gb[:, GATE_COL_F + h:GATE_COL_F + h + 1])
        lf_row = log_sigmoid(gbt[GATE_COL_F + h:GATE_COL_F + h + 1, :])
        if valid < L:
            ic_col = jnp.where(tok_col < valid, ic_col, NEG_INF)
            ic_row = jnp.where(tok_row < valid, ic_row, NEG_INF)
            lf_col = jnp.where(tok_col < valid, lf_col, 0.0)
            lf_row = jnp.where(tok_row < valid, lf_row, 0.0)

        cum_col = jnp.sum(jnp.where(tril, lf_row, 0.0), axis=1, keepdims=True)
        cum_row = jnp.sum(jnp.where(triu, lf_col, 0.0), axis=0, keepdims=True)
        m0 = m_ref[bi, 0:1, h:h + 1]
        dmat = jnp.where(tril, cum_col - cum_row + ic_row, NEG_INF)
        inter = cum_col + m0
        m_t = jnp.maximum(inter, jnp.max(dmat, axis=1, keepdims=True))
        w = jnp.exp(dmat - m_t)
        sc = jnp.exp(inter - m_t)
        s = _dot_nt(qb, kb) * w
        c_old = c_ref[bi, h]
        n_old = n_ref[bi, h:h + 1, :]
        num = _dot(s.astype(BF16), v.astype(BF16)) + sc * _dot_nt(qb, c_old.astype(BF16))
        den = jnp.sum(s, axis=1, keepdims=True) + sc * jnp.sum(q * n_old, axis=1, keepdims=True)
        hc = num / jnp.maximum(jnp.abs(den), jnp.exp(-m_t))

        m_new = m_t[L - 1:L, :]
        cum_last = cum_col[L - 1:L, :]
        wl = jnp.exp(cum_last - cum_col + ic_col - m_new)
        sl = jnp.exp(cum_last + m0 - m_new)
        vw_t = (v * wl).T.astype(BF16)
        c_ref[bi, h] = sl * c_old + _dot(vw_t, kb)
        n_ref[bi, h:h + 1, :] = sl * n_old + jnp.sum(wl * k, axis=0, keepdims=True)
        m_ref[bi, 0:1, h:h + 1] = m_new

        hn = _rms(hc, gh_ref[:, sl_h])
        out = (hn[0:valid, :] + sk_ref[:, sl_h] * u_h[0:valid, :]) * _sigmoid(mo_ref[bi, :, sl_h])
        o_ref[bi, :, sl_h] = out


def _mlstm(mu, mv, mo, gates, conv_buf, c0, n0, m0, w_mconv, b_mconv, w_mq, w_mk, b_ig, b_fg,
           g_mhead, m_skip, *, batch, seq):
    L = MLSTM_CHUNK
    valid = min(seq, L)
    assert seq % valid == 0 and (valid == L or seq == valid)
    nc = seq // valid
    gate_bias = jnp.zeros((1, LANES), F32)
    gate_bias = gate_bias.at[0, GATE_COL_I:GATE_COL_I + MLSTM_HEADS].set(b_ig)
    gate_bias = gate_bias.at[0, GATE_COL_F:GATE_COL_F + MLSTM_HEADS].set(b_fg)
    bb = MLSTM_SEQS_PER_STEP
    assert batch % bb == 0
    tok = lambda b, c: (b, c, 0)
    const2 = lambda b, c: (0, 0)
    const3 = lambda b, c: (0, 0, 0)
    per_b3 = lambda b, c: (b, 0, 0)
    per_b4 = lambda b, c: (b, 0, 0, 0)
    H, DH, W = MLSTM_HEADS, MLSTM_DH, MLSTM_WIDTH
    rows3 = lambda a: a.reshape(batch, seq, a.shape[-1])
    o_m, conv_new, c_new, n_new, m_new = pl.pallas_call(
        functools.partial(_mlstm_body, valid=valid, bb=bb),
        grid=(batch // bb, nc),
        in_specs=[pl.BlockSpec((bb, valid, W), tok), pl.BlockSpec((bb, valid, W), tok),
                  pl.BlockSpec((bb, valid, W), tok), pl.BlockSpec((bb, valid, LANES), tok),
                  pl.BlockSpec((bb, MLSTM_CONV - 1, W), per_b3),
                  pl.BlockSpec((bb, H, DH, DH), per_b4),
                  pl.BlockSpec((bb, H, DH), per_b3),
                  pl.BlockSpec((bb, 1, H), per_b3),
                  pl.BlockSpec((MLSTM_CONV, W), const2), pl.BlockSpec((1, W), const2),
                  pl.BlockSpec((H, DH, DH), const3), pl.BlockSpec((H, DH, DH), const3),
                  pl.BlockSpec((1, LANES), const2), pl.BlockSpec((1, W), const2),
                  pl.BlockSpec((1, W), const2)],
        out_specs=[pl.BlockSpec((bb, valid, W), tok),
                   pl.BlockSpec((bb, MLSTM_CONV - 1, W), per_b3),
                   pl.BlockSpec((bb, H, DH, DH), per_b4),
                   pl.BlockSpec((bb, H, DH), per_b3),
                   pl.BlockSpec((bb, 1, H), per_b3)],
        out_shape=[jax.ShapeDtypeStruct((batch, seq, W), F32),
                   jax.ShapeDtypeStruct((batch, MLSTM_CONV - 1, W), F32),
                   jax.ShapeDtypeStruct((batch, H, DH, DH), F32),
                   jax.ShapeDtypeStruct((batch, H, DH), F32),
                   jax.ShapeDtypeStruct((batch, 1, H), F32)],
        scratch_shapes=[pltpu.VMEM((bb, SUBLANES + L, W), F32), pltpu.VMEM((bb, L, W), F32),
                        pltpu.VMEM((bb, L, LANES), F32)],
        compiler_params=_cparams(("arbitrary", "arbitrary")),
        name="mlstm",
    )(rows3(mu), rows3(mv), rows3(mo), rows3(gates), conv_buf, c0, n0, m0.reshape(batch, 1, H),
      w_mconv, b_mconv.reshape(1, W), w_mq.astype(BF16), w_mk.astype(BF16), gate_bias,
      g_mhead.reshape(1, W), m_skip.reshape(1, W))
    return o_m.reshape(batch * seq, W), conv_new, c_new, n_new, m_new


def _compress_rows(xk_ref, xv_ref, pe_ref, w1_ref, w2_ref, n_pairs):
    pair_rows = 2 * CMP_BLOCK
    outs = []
    for kv, x_ref in enumerate((xk_ref, xv_ref)):
        acc = jnp.zeros((2 * n_pairs, NSA_KV_HEADS * CMP_HIDDEN), F32)
        for r in range(CMP_BLOCK):
            ev = x_ref[pl.ds(r, n_pairs, stride=pair_rows), :]
            od = x_ref[pl.ds(CMP_BLOCK + r, n_pairs, stride=pair_rows), :]
            xr = jnp.concatenate([ev, od], axis=0) + pe_ref[kv, r:r + 1, :]
            acc = acc + _dot(xr.astype(BF16), w1_ref[kv, r])
        outs.append(_dot(_silu(acc).astype(BF16), w2_ref[kv]))
    return jnp.concatenate(outs, axis=1)


def _compress_body(xk_ref, xv_ref, pe_ref, w1_ref, w2_ref, oe_ref, oo_ref, *, n_pairs):
    out = _compress_rows(xk_ref, xv_ref, pe_ref, w1_ref, w2_ref, n_pairs)
    oe_ref[0] = out[0:n_pairs, :]
    oo_ref[0] = out[n_pairs:, :]


BLOCKS_PER_PAGE = PAGE_SIZE // CMP_BLOCK


def _compress_paged_body(pt_ref, *refs, n_pages):
    page_refs = refs[:n_pages]
    pet_ref, perm_ref, w1_ref, w2_ref, oe_ref, oo_ref, buf_ref, os_ref = refs[n_pages:]
    grp = 2 * BLOCKS_PER_PAGE
    for jp in range(n_pages // 2):
        xt = jnp.concatenate([page_refs[2 * jp][0], page_refs[2 * jp + 1][0]], axis=1)
        xb = (xt + pet_ref[...]).astype(BF16)
        xp = _dot_nt(perm_ref[...], xb)
        for r in range(CMP_BLOCK):
            buf_ref[r, grp * jp:grp * (jp + 1), :] = xp[grp * r:grp * (r + 1), :]
    for kv in range(2):
        feats = slice(kv * KV_WIDTH, (kv + 1) * KV_WIDTH)
        acc = _dot(buf_ref[0, :, feats].astype(BF16), w1_ref[kv, 0])
        for r in range(1, CMP_BLOCK):
            acc = acc + _dot(buf_ref[r, :, feats].astype(BF16), w1_ref[kv, r])
        os_ref[kv] = _dot(_silu(acc).astype(BF16), w2_ref[kv])
    half = os_ref.shape[1] // 2
    for parity, ref in enumerate((oe_ref, oo_ref)):
        ref[0] = jnp.concatenate([os_ref[kv, pl.ds(parity, half, stride=2), :] for kv in range(2)],
                                 axis=1)


def _page_pair_constants(pe):
    pe_t = jnp.broadcast_to(pe.transpose(0, 2, 1)[:, None, :, None, :],
                            (2, NSA_KV_HEADS, HEAD_DIM, 2 * BLOCKS_PER_PAGE, CMP_BLOCK))
    pe_t = pe_t.reshape(2 * KV_WIDTH, 2 * PAGE_SIZE)
    grp = 2 * BLOCKS_PER_PAGE
    perm = np.zeros((2 * PAGE_SIZE, 2 * PAGE_SIZE), np.float32)
    for r in range(CMP_BLOCK):
        for b in range(grp):
            perm[r * grp + b, b * CMP_BLOCK + r] = 1.0
    return pe_t, jnp.asarray(perm, BF16)


def _pack_compress_weights(pe, w1, w2):
    eye_h = jnp.eye(NSA_KV_HEADS, dtype=F32)
    pe_r = jnp.broadcast_to(pe[:, :, None, :], (2, CMP_BLOCK, NSA_KV_HEADS, HEAD_DIM))
    pe_r = pe_r.reshape(2, CMP_BLOCK, KV_WIDTH)
    w1r = w1.reshape(2, CMP_BLOCK, HEAD_DIM, CMP_HIDDEN)
    w1_big = jnp.einsum('krdc,hH->krhdHc', w1r, eye_h)
    w1_big = w1_big.reshape(2, CMP_BLOCK, KV_WIDTH, NSA_KV_HEADS * CMP_HIDDEN).astype(BF16)
    w2_big = jnp.einsum('kcd,hH->khcHd', w2, eye_h)
    w2_big = w2_big.reshape(2, NSA_KV_HEADS * CMP_HIDDEN, KV_WIDTH).astype(BF16)
    return pe_r, w1_big, w2_big


def _compress_prompt(k_rows, v_rows, cw, *, batch, seq):
    pe_r, w1_big, w2_big = cw
    n_pairs = seq // (2 * CMP_BLOCK)
    const3 = lambda b: (0, 0, 0)
    out_sd = jax.ShapeDtypeStruct((batch, n_pairs, 2 * KV_WIDTH), F32)
    return pl.pallas_call(
        functools.partial(_compress_body, n_pairs=n_pairs),
        grid=(batch,),
        in_specs=[pl.BlockSpec((seq, KV_WIDTH), lambda b: (b, 0)),
                  pl.BlockSpec((seq, KV_WIDTH), lambda b: (b, 0)),
                  pl.BlockSpec(pe_r.shape, const3),
                  pl.BlockSpec(w1_big.shape, lambda b: (0, 0, 0, 0)),
                  pl.BlockSpec(w2_big.shape, const3)],
        out_specs=[pl.BlockSpec((1, n_pairs, 2 * KV_WIDTH), lambda b: (b, 0, 0))] * 2,
        out_shape=[out_sd, out_sd],
        compiler_params=_cparams(("arbitrary",)),
        name="compress_prompt",
    )(k_rows, v_rows, pe_r, w1_big, w2_big)


COMPRESS_PAGES_PER_STEP = 32


def _compress_paged(pool, page_table, cw, cw_pages):
    _, w1_big, w2_big = cw
    pe_t, perm = cw_pages
    batch, n_pages = page_table.shape
    pps = COMPRESS_PAGES_PER_STEP
    assert n_pages % pps == 0 and pps % 2 == 0
    n_steps = n_pages // pps
    n_blk = pps * BLOCKS_PER_PAGE
    const3 = lambda b, c, pt: (0, 0, 0)

    def page_spec(j):
        return pl.BlockSpec((1, 2 * KV_WIDTH, PAGE_SIZE),
                            lambda b, c, pt: (pt[(b * n_steps + c) * pps + j], 0, 0))

    return pl.pallas_call(
        functools.partial(_compress_paged_body, n_pages=pps),
        grid_spec=pltpu.PrefetchScalarGridSpec(
            num_scalar_prefetch=1,
            grid=(batch, n_steps),
            in_specs=[page_spec(j) for j in range(pps)] + [
                pl.BlockSpec(pe_t.shape, lambda b, c, pt: (0, 0)),
                pl.BlockSpec(perm.shape, lambda b, c, pt: (0, 0)),
                pl.BlockSpec(w1_big.shape, lambda b, c, pt: (0, 0, 0, 0)),
                pl.BlockSpec(w2_big.shape, const3)],
            out_specs=[pl.BlockSpec((1, n_blk // 2, 2 * KV_WIDTH), lambda b, c, pt: (b, c, 0))] * 2,
            scratch_shapes=[pltpu.VMEM((CMP_BLOCK, n_blk, 2 * KV_WIDTH), F32),
                            pltpu.VMEM((2, n_blk, KV_WIDTH), F32)]),
        out_shape=[jax.ShapeDtypeStruct((batch, n_steps * n_blk // 2, 2 * KV_WIDTH), F32)] * 2,
        compiler_params=_cparams(("arbitrary", "arbitrary")),
        name="compress_paged",
    )(page_table.reshape(-1), *([pool] * pps), pe_t, perm, w1_big, w2_big)


def _cmp_attn_body(q_ref, ke_ref, ko_ref, o_ref, st_ref, *, tq, pos0):
    ns = ke_ref.shape[1]
    i = pl.program_id(1)
    rows = NSA_GROUP * tq
    tok0 = pos0 + i * tq
    pos_c = tok0 + lax.broadcasted_iota(jnp.int32, (rows, 1), 0) % tq
    pos_r = tok0 + lax.broadcasted_iota(jnp.int32, (1, rows), 1) % tq
    pair_r = lax.broadcasted_iota(jnp.int32, (1, ns), 1)
    pair_c = lax.broadcasted_iota(jnp.int32, (ns, 1), 0)
    end_e = lambda pair: (2 * pair + 1) * CMP_BLOCK - 1
    end_o = lambda pair: (2 * pair + 2) * CMP_BLOCK - 1
    any_c = (CMP_BLOCK - 1 <= pos_c).astype(F32)
    any_r = (CMP_BLOCK - 1 <= pos_r).astype(F32)
    q = q_ref[...] * ATTN_SCALE
    for kh in range(NSA_KV_HEADS):
        qs = jnp.concatenate([q[:, (kh * NSA_GROUP + g) * HEAD_DIM:(kh * NSA_GROUP + g + 1) * HEAD_DIM]
                              for g in range(NSA_GROUP)], axis=0).astype(BF16)
        ks, vs = slice(kh * HEAD_DIM, (kh + 1) * HEAD_DIM), slice(KV_WIDTH + kh * HEAD_DIM,
                                                                   KV_WIDTH + (kh + 1) * HEAD_DIM)
        ke, ko = ke_ref[0, :, ks].astype(BF16), ko_ref[0, :, ks].astype(BF16)
        se = jnp.where(end_e(pair_r) <= pos_c, _dot_nt(qs, ke), NEG_INF)
        so = jnp.where(end_o(pair_r) <= pos_c, _dot_nt(qs, ko), NEG_INF)
        mx = jnp.maximum(jnp.max(se, axis=1, keepdims=True), jnp.max(so, axis=1, keepdims=True))
        pe, po = jnp.exp(se - mx), jnp.exp(so - mx)
        inv = any_c / (jnp.sum(pe, axis=1, keepdims=True) + jnp.sum(po, axis=1, keepdims=True))
        oh = (_dot((pe * inv).astype(BF16), ke_ref[0, :, vs].astype(BF16))
              + _dot((po * inv).astype(BF16), ko_ref[0, :, vs].astype(BF16)))
        for g in range(NSA_GROUP):
            hd = kh * NSA_GROUP + g
            o_ref[:, hd * HEAD_DIM:(hd + 1) * HEAD_DIM] = oh[g * tq:(g + 1) * tq, :]
        te = jnp.where(end_e(pair_c) <= pos_r, _dot_nt(ke, qs), NEG_INF)
        to = jnp.where(end_o(pair_c) <= pos_r, _dot_nt(ko, qs), NEG_INF)
        mt = jnp.maximum(jnp.max(te, axis=0, keepdims=True), jnp.max(to, axis=0, keepdims=True))
        pte, pto = jnp.exp(te - mt), jnp.exp(to - mt)
        invt = any_r / (jnp.sum(pte, axis=0, keepdims=True) + jnp.sum(pto, axis=0, keepdims=True))
        ps = (pte + pto) * invt
        score = ps[:, 0:tq]
        for g in range(1, NSA_GROUP):
            score = score + ps[:, g * tq:(g + 1) * tq]
        st_ref[0, kh] = score


def _cmp_attn(q2d, kce, kco, *, batch, seq, tq, pos0):
    ns = kce.shape[1]
    nq = seq // tq
    return pl.pallas_call(
        functools.partial(_cmp_attn_body, tq=tq, pos0=pos0),
        grid=(batch, nq),
        in_specs=[pl.BlockSpec((tq, NSA_WIDTH), lambda b, i: (b * nq + i, 0)),
                  pl.BlockSpec((1, ns, 2 * KV_WIDTH), lambda b, i: (b, 0, 0)),
                  pl.BlockSpec((1, ns, 2 * KV_WIDTH), lambda b, i: (b, 0, 0))],
        out_specs=[pl.BlockSpec((tq, NSA_WIDTH), lambda b, i: (b * nq + i, 0)),
                   pl.BlockSpec((1, NSA_KV_HEADS, ns, tq), lambda b, i: (b, 0, 0, i))],
        out_shape=[jax.ShapeDtypeStruct((batch * seq, NSA_WIDTH), F32),
                   jax.ShapeDtypeStruct((batch, NSA_KV_HEADS, ns, seq), F32)],
        compiler_params=_cparams(("arbitrary", "arbitrary")),
        name="cmp_attn",
    )(q2d, kce, kco)


def _topk_body(pos_ref, st_ref, b_ref, *, n_sel):
    score = st_ref[0]
    ns, tt = score.shape
    nsw = b_ref.shape[1]
    if nsw > ns:
        score = jnp.concatenate([score, jnp.zeros((nsw - ns, tt), F32)], axis=0)
    blk = lax.broadcasted_iota(jnp.int32, (nsw, 1), 0)
    blk_f = blk.astype(F32)
    cur = pos_ref[...] // SEL_BLOCK
    forced = (blk == 0) | (blk == cur) | (blk == cur - 1)
    pri = jnp.where(blk <= cur, jnp.where(forced, SEL_PRIORITY, score), -SEL_PRIORITY)
    pri = jnp.where(blk < n_sel, pri, -jnp.inf)
    bias = jnp.full((nsw, tt), NEG_INF, F32)
    for _ in range(min(TOP_N, n_sel)):
        top = jnp.max(pri, axis=0, keepdims=True)
        first = jnp.min(jnp.where(pri == top, blk_f, float(nsw)), axis=0, keepdims=True)
        hit = blk_f == first
        bias = jnp.where(hit, 0.0, bias)
        pri = jnp.where(hit, -jnp.inf, pri)
    b_ref[0] = bias


def _topk_blocks(scores_t, pos, *, n_sel, nsw, tt):
    groups, ns, tokens = scores_t.shape
    assert nsw >= max(ns, n_sel) and tokens % tt == 0
    return pl.pallas_call(
        functools.partial(_topk_body, n_sel=n_sel),
        grid=(groups, tokens // tt),
        in_specs=[pl.BlockSpec((1, tt), lambda g, i: (0, i)),
                  pl.BlockSpec((1, ns, tt), lambda g, i: (g, 0, i))],
        out_specs=pl.BlockSpec((1, nsw, tt), lambda g, i: (g, 0, i)),
        out_shape=jax.ShapeDtypeStruct((groups, nsw, tokens), F32),
        compiler_params=_cparams(("arbitrary", "arbitrary")),
        name="topk_blocks",
    )(pos, scores_t)


def _softmax_update(sc, vt_bf16, m_ref, l_ref, acc_ref):
    m_old = m_ref[...]
    m_new = jnp.maximum(m_old, jnp.max(sc, axis=1, keepdims=True))
    alpha = jnp.exp(m_old - m_new)
    pr = jnp.exp(sc - jnp.concatenate([m_new] * (sc.shape[1] // LANES), axis=1))
    l_ref[...] = alpha * l_ref[...] + jnp.sum(pr, axis=1, keepdims=True)
    acc_ref[...] = alpha * acc_ref[...] + _dot_nt(pr.astype(BF16), vt_bf16)
    m_ref[...] = m_new


def _softmax_init(m_ref, l_ref, acc_ref):
    m_ref[...] = jnp.full(m_ref.shape, NEG_INF, F32)
    l_ref[...] = jnp.zeros(l_ref.shape, F32)
    acc_ref[...] = jnp.zeros(acc_ref.shape, F32)


def _block_onehot_t(first_key, n_keys):
    blk = (first_key + lax.broadcasted_iota(jnp.int32, (1, n_keys), 1)) // SEL_BLOCK
    r = lax.broadcasted_iota(jnp.int32, (LANES, 1), 0) & (SEL_BLOCK - 1)
    return (r == blk).astype(F32)


ATTN_TAB_COLS = 5


def _attn_pairs(seq, tq, tk, window):
    rows = []
    for i in range(seq // tq):
        t_lo, t_hi = i * tq, i * tq + tq - 1
        k_lo = 0 if window is None else max(0, t_lo - window + 1)
        js = list(range(k_lo // tk, t_hi // tk + 1))
        for n, j in enumerate(js):
            partial_tile = j * tk + tk - 1 > t_lo or (window is not None and j * tk <= t_hi - window)
            rows.append((i, j, int(n == 0), int(n == len(js) - 1), int(partial_tile)))
    return np.asarray(rows, np.int32)


def _attn_body(tab_ref, q_ref, kv_ref, *rest, tq, tk, window, use_bias):
    if use_bias:
        sb_ref, o_ref, qa_ref, m_ref, l_ref, acc_ref = rest
    else:
        o_ref, qa_ref, m_ref, l_ref, acc_ref = rest
    p = pl.program_id(1)
    i, j, first, last, partial_tile = [tab_ref[ATTN_TAB_COLS * p + n] for n in range(ATTN_TAB_COLS)]
    G = NSA_GROUP
    cols = NSA_HEADS * tq
    zeros64 = jnp.zeros((HEAD_DIM, tq), F32)

    def kv_head_rows(x, kh):
        return jnp.concatenate([x, zeros64] if kh == 0 else [zeros64, x], axis=0)

    @pl.when(first == 1)
    def _():
        q = q_ref[0] * (ATTN_SCALE * LOG2_E)
        for m in range(NSA_HEADS // 2):
            q_t = q[:, m * LANES:(m + 1) * LANES].T
            for hd in (2 * m, 2 * m + 1):
                kh = hd // G
                piece = kv_head_rows(q_t[(hd % 2) * HEAD_DIM:(hd % 2 + 1) * HEAD_DIM, :], kh)
                if use_bias:
                    piece = jnp.concatenate([piece, kv_head_rows(sb_ref[0, kh], kh)], axis=0)
                qa_ref[:, hd * tq:(hd + 1) * tq] = piece.astype(BF16)
        m_ref[...] = jnp.full(m_ref.shape, NEG_INF, F32)
        l_ref[...] = jnp.zeros(l_ref.shape, F32)
        acc_ref[...] = jnp.zeros(acc_ref.shape, F32)

    k_rows = kv_ref[0, 0:KV_WIDTH, :].T
    if use_bias:
        blk = (j * tk + lax.broadcasted_iota(jnp.int32, (tk, 1), 0)) // SEL_BLOCK
        lane_blk = lax.broadcasted_iota(jnp.int32, (1, LANES), 1) & (SEL_BLOCK - 1)
        k_rows = jnp.concatenate([k_rows, (lane_blk == blk).astype(F32)], axis=1)
    sc = _dot(k_rows.astype(BF16), qa_ref[...])
    vt = kv_ref[0, KV_WIDTH:, :].astype(BF16)

    def update(sc):
        m_old = m_ref[...]
        m_new = jnp.maximum(m_old, jnp.max(sc, axis=0, keepdims=True))
        alpha = jnp.exp2(m_old - m_new)
        pr = jnp.exp2(sc - m_new)
        l_ref[...] = alpha * l_ref[...] + jnp.sum(pr, axis=0, keepdims=True)
        acc_ref[...] = alpha * acc_ref[...] + _dot(vt, pr.astype(BF16))
        m_ref[...] = m_new

    @pl.when(partial_tile == 1)
    def _():
        qpos = i * tq + (lax.broadcasted_iota(jnp.int32, (1, cols), 1) & (tq - 1))
        kpos = j * tk + lax.broadcasted_iota(jnp.int32, (tk, 1), 0)
        valid = kpos <= qpos
        if window is not None:
            valid = valid & (kpos > qpos - window)
        update(jnp.where(valid, sc, NEG_INF))

    @pl.when(partial_tile == 0)
    def _():
        update(sc)

    @pl.when(last == 1)
    def _():
        o_t = acc_ref[...] / l_ref[...]
        for m in range(NSA_HEADS // 2):
            pair = jnp.concatenate(
                [o_t[(hd // G) * HEAD_DIM:(hd // G + 1) * HEAD_DIM, hd * tq:(hd + 1) * tq]
                 for hd in (2 * m, 2 * m + 1)], axis=0)
            o_ref[0, :, m * LANES:(m + 1) * LANES] = pair.T


def _attn_prompt(q3d, kv_t, selb, *, tq, tk, window):
    batch, seq, _ = q3d.shape
    assert tq & (tq - 1) == 0 and tk % LANES == 0 and tq % LANES == 0
    use_bias = selb is not None
    assert not use_bias or selb.shape[2] == SEL_BLOCK
    tab = _attn_pairs(seq, tq, tk, window)
    depth = 2 * LANES if use_bias else LANES
    cols = NSA_HEADS * tq
    C = ATTN_TAB_COLS
    in_specs = [pl.BlockSpec((1, tq, NSA_WIDTH), lambda b, p, t: (b, t[C * p], 0)),
                pl.BlockSpec((1, 2 * KV_WIDTH, tk), lambda b, p, t: (b, 0, t[C * p + 1]))]
    args = [q3d, kv_t]
    if use_bias:
        in_specs.append(pl.BlockSpec((1, NSA_KV_HEADS, SEL_BLOCK, tq),
                                     lambda b, p, t: (b, 0, 0, t[C * p])))
        args.append(selb)
    return pl.pallas_call(
        functools.partial(_attn_body, tq=tq, tk=tk, window=window, use_bias=use_bias),
        grid_spec=pltpu.PrefetchScalarGridSpec(
            num_scalar_prefetch=1,
            grid=(batch, tab.shape[0]),
            in_specs=in_specs,
            out_specs=pl.BlockSpec((1, tq, NSA_WIDTH), lambda b, p, t: (b, t[C * p], 0)),
            scratch_shapes=[pltpu.VMEM((depth, cols), BF16), pltpu.VMEM((1, cols), F32),
                            pltpu.VMEM((1, cols), F32), pltpu.VMEM((KV_WIDTH, cols), F32)]),
        out_shape=jax.ShapeDtypeStruct((batch, seq, NSA_WIDTH), F32),
        compiler_params=_cparams(("arbitrary", "arbitrary")),
        name="attn_sel" if use_bias else "attn_win",
    )(jnp.asarray(tab.reshape(-1)), *args)


ATTN_PAGES_PER_STEP = 16


def _attn_paged_body(pt_ref, qa_ref, bq_ref, bn_ref, kn_ref, *rest, n_pages, n_new):
    page_refs = rest[:n_pages]
    o_ref, m_ref, l_ref, acc_ref = rest[n_pages:]
    c = pl.program_id(1)
    rows = qa_ref.shape[1]

    @pl.when(c == 0)
    def _():
        _softmax_init(m_ref, l_ref, acc_ref)

    keys = n_pages * PAGE_SIZE
    kt = jnp.concatenate([r[0, 0:KV_WIDTH, :] for r in page_refs], axis=1)
    vt = jnp.concatenate([r[0, KV_WIDTH:, :] for r in page_refs], axis=1).astype(BF16)
    rhs = jnp.concatenate([kt, _block_onehot_t(0, keys)], axis=0).astype(BF16)
    qa = qa_ref[0]
    lhs = jnp.concatenate([qa, bq_ref[0, 0]], axis=1).astype(BF16)
    _softmax_update(_dot(lhs, rhs), vt, m_ref, l_ref, acc_ref)

    @pl.when(c == pl.num_programs(1) - 1)
    def _():
        kn = kn_ref[0]
        sc = _dot(qa.astype(BF16), kn[0:KV_WIDTH, :].astype(BF16)) + bn_ref[0]
        tq = lax.broadcasted_iota(jnp.int32, (rows, 1), 0) % n_new
        kk = lax.broadcasted_iota(jnp.int32, (1, kn.shape[1]), 1)
        sc = jnp.where((kk <= tq) & (kk < n_new), sc, NEG_INF)
        _softmax_update(sc, kn[KV_WIDTH:, :].astype(BF16), m_ref, l_ref, acc_ref)
        o_ref[0] = acc_ref[...] / l_ref[...]


def _attn_paged(qa, bias_q, bias_new, kv_new_t, pool, page_table, *, n_new):
    batch, n_pages = page_table.shape
    pps = ATTN_PAGES_PER_STEP
    assert n_pages % pps == 0 and pps * PAGE_SIZE // SEL_BLOCK <= SEL_BLOCK
    n_steps = n_pages // pps
    rows = qa.shape[1]

    def page_spec(j):
        return pl.BlockSpec((1, 2 * KV_WIDTH, PAGE_SIZE),
                            lambda b, c, pt: (pt[(b * n_steps + c) * pps + j], 0, 0))

    per_b = lambda b, c, pt: (b, 0, 0)
    return pl.pallas_call(
        functools.partial(_attn_paged_body, n_pages=pps, n_new=n_new),
        grid_spec=pltpu.PrefetchScalarGridSpec(
            num_scalar_prefetch=1,
            grid=(batch, n_steps),
            in_specs=[pl.BlockSpec((1, rows, LANES), per_b),
                      pl.BlockSpec((1, 1, rows, LANES), lambda b, c, pt: (b, c, 0, 0)),
                      pl.BlockSpec((1, rows, LANES), per_b),
                      pl.BlockSpec((1,) + kv_new_t.shape[1:], per_b)]
            + [page_spec(j) for j in range(pps)],
            out_specs=pl.BlockSpec((1, rows, LANES), per_b),
            scratch_shapes=[pltpu.VMEM((rows, LANES), F32), pltpu.VMEM((rows, LANES), F32),
                            pltpu.VMEM((rows, LANES), F32)]),
        out_shape=jax.ShapeDtypeStruct((batch, rows, LANES), F32),
        compiler_params=_cparams(("arbitrary", "arbitrary")),
        name="attn_sel_paged",
    )(page_table.reshape(-1), qa, bias_q, bias_new, kv_new_t, *([pool] * pps))


def _attn_window_body(qa_ref, wb_ref, kn_ref, o_ref, *, n_new, past):
    qa = qa_ref[0].astype(BF16)
    wb, kn = wb_ref[0], kn_ref[0]
    rows, n_buf = qa.shape[0], wb.shape[1]
    qpos = past + lax.broadcasted_iota(jnp.int32, (rows, 1), 0) % n_new

    def masked(sc, kpos, extra):
        diff = qpos - kpos
        return jnp.where((diff >= 0) & (diff < WINDOW) & (kpos >= 0) & extra, sc, NEG_INF)

    nb = lax.broadcasted_iota(jnp.int32, (1, n_buf), 1)
    nn = lax.broadcasted_iota(jnp.int32, (1, kn.shape[1]), 1)
    sb = masked(_dot(qa, wb[0:KV_WIDTH, :].astype(BF16)), past - n_buf + nb, nb >= 0)
    sn = masked(_dot(qa, kn[0:KV_WIDTH, :].astype(BF16)), past + nn, nn < n_new)
    mx = jnp.maximum(jnp.max(sb, axis=1, keepdims=True), jnp.max(sn, axis=1, keepdims=True))
    pb, pn = jnp.exp(sb - mx), jnp.exp(sn - mx)
    o = (_dot_nt(pb.astype(BF16), wb[KV_WIDTH:, :].astype(BF16))
         + _dot_nt(pn.astype(BF16), kn[KV_WIDTH:, :].astype(BF16)))
    o_ref[0] = o / (jnp.sum(pb, axis=1, keepdims=True) + jnp.sum(pn, axis=1, keepdims=True))


def _attn_window_small(qa, win_t, kv_new_t, *, n_new, past):
    batch, rows, _ = qa.shape
    per_b = lambda b: (b, 0, 0)
    return pl.pallas_call(
        functools.partial(_attn_window_body, n_new=n_new, past=past),
        grid=(batch,),
        in_specs=[pl.BlockSpec((1, rows, LANES), per_b),
                  pl.BlockSpec((1,) + win_t.shape[1:], per_b),
                  pl.BlockSpec((1,) + kv_new_t.shape[1:], per_b)],
        out_specs=pl.BlockSpec((1, rows, LANES), per_b),
        out_shape=jax.ShapeDtypeStruct((batch, rows, LANES), F32),
        compiler_params=_cparams(("arbitrary",)),
        name="attn_win_small",
    )(qa, win_t, kv_new_t)


FFN_TM = 256
MXU_DEPTH = 256
FFN_CHUNKS = ((0, 6 * MXU_DEPTH), (6 * MXU_DEPTH, D_FF))


def _ffn_body(x_ref, om_ref, oc_ref, os_ref, ow_ref, gt_ref, ge_ref, gn_ref, gf_ref, gl_ref, wc_ref,
              fb_ref, wo_hbm, wu_hbm, wd_hbm, y_ref, fn_ref, xx_ref, wo_ref, wu_ref, wd_ref, sem_ref,
              *, tm, stride, halo):
    s = pl.program_id(1)

    @pl.when((pl.program_id(0) == 0) & (s == 0))
    def _():
        copies = [pltpu.make_async_copy(src, dst, sem_ref.at[n])
                  for n, (src, dst) in enumerate(((wo_hbm, wo_ref), (wu_hbm, wu_ref), (wd_hbm, wd_ref)))]
        for cp in copies:
            cp.start()
        for cp in copies:
            cp.wait()

    sig = _sigmoid(gt_ref[...])
    hi = sig.astype(BF16)
    lo = (sig - hi.astype(F32)).astype(BF16)
    comb = None
    for br, ob_ref in enumerate((oc_ref, os_ref, ow_ref)):
        gate = _dot(hi, ge_ref[br]) + _dot(lo, ge_ref[br])
        term = gate * ob_ref[...]
        comb = term if comb is None else comb + term
    onsa = _rms(comb, gn_ref[...])
    h = (x_ref[...] + _dot(om_ref[...].astype(BF16), wo_ref[0:MLSTM_WIDTH, :])
         + _dot(onsa.astype(BF16), wo_ref[MLSTM_WIDTH:, :]))
    hn = _rms(h, gf_ref[...]).astype(BF16)

    base = halo - (FFN_CONV - 1) * stride

    @pl.when(s == 0)
    def _():
        xx_ref[base:halo, :] = fb_ref[0]

    y_ref[...] = h
    for lo_col, hi_col in FFN_CHUNKS:
        convs = []
        for half in range(2):
            cols = slice(half * D_FF + lo_col, half * D_FF + hi_col)
            xx_ref[halo:halo + tm, cols] = _dot(hn, wu_ref[:, cols])
            conv = xx_ref[base:base + tm, cols] * wc_ref[0:1, cols]
            for j in range(1, FFN_CONV):
                conv = conv + xx_ref[base + j * stride:base + j * stride + tm, cols] * wc_ref[j:j + 1, cols]
            convs.append(conv)
        act = _silu(convs[1]) * convs[0]
        y_ref[...] += _dot(act.astype(BF16), wd_ref[lo_col:hi_col, :])
    fn_ref[0, 0] = xx_ref[tm + base:tm + halo, :]
    xx_ref[0:halo, :] = xx_ref[tm:tm + halo, :]
    y_ref[...] = _rms(y_ref[...], gl_ref[...])


def _gate_expand():
    ge = np.zeros((N_BRANCH, LANES, NSA_WIDTH), np.float32)
    for hd in range(NSA_HEADS):
        for br in range(N_BRANCH):
            ge[br, GATE_COL_NSA + hd * N_BRANCH + br, hd * HEAD_DIM:(hd + 1) * HEAD_DIM] = 1.0
    return jnp.asarray(ge, BF16)


def _ffn(x2d, om, oc, osel, ow, gt, fbuf, w_out, g_nsa, g_ffn, g_final, w_up, w_fconv, w_down,
         *, nb, tm, stride):
    rows = x2d.shape[0]
    ns = rows // (nb * tm)
    halo = -(-(FFN_CONV - 1) * stride // SUBLANES) * SUBLANES
    assert tm >= halo and all((hi - lo) % MXU_DEPTH == 0 for lo, hi in FFN_CHUNKS)
    tok = lambda b, s: (b * ns + s, 0)
    nfb = (FFN_CONV - 1) * stride

    def const(shape):
        return pl.BlockSpec(shape, lambda b, s: (0,) * len(shape))

    hbm = pl.BlockSpec(memory_space=pl.ANY)
    y, fn = pl.pallas_call(
        functools.partial(_ffn_body, tm=tm, stride=stride, halo=halo),
        grid=(nb, ns),
        in_specs=[pl.BlockSpec((tm, D_MODEL), tok)] + [pl.BlockSpec((tm, NSA_WIDTH), tok)] * 4
        + [pl.BlockSpec((tm, LANES), tok),
           const((N_BRANCH, LANES, NSA_WIDTH)), const((1, NSA_WIDTH)), const((1, D_MODEL)),
           const((1, D_MODEL)), const((FFN_CONV, 2 * D_FF)),
           pl.BlockSpec((1, nfb, 2 * D_FF), lambda b, s: (b, 0, 0)), hbm, hbm, hbm],
        out_specs=[pl.BlockSpec((tm, D_MODEL), tok),
                   pl.BlockSpec((1, 1, nfb, 2 * D_FF), lambda b, s: (b, s, 0, 0))],
        out_shape=[jax.ShapeDtypeStruct((rows, D_MODEL), F32),
                   jax.ShapeDtypeStruct((nb, ns, nfb, 2 * D_FF), F32)],
        scratch_shapes=[pltpu.VMEM((halo + tm, 2 * D_FF), F32),
                        pltpu.VMEM((D_MODEL, D_MODEL), BF16), pltpu.VMEM((D_MODEL, 2 * D_FF), BF16),
                        pltpu.VMEM((D_FF, D_MODEL), BF16), pltpu.SemaphoreType.DMA((3,))],
        compiler_params=_cparams(("arbitrary", "arbitrary")),
        name="outproj_ffn",
    )(x2d, om, oc, osel, ow, gt, _gate_expand(), g_nsa.reshape(1, -1), g_ffn.reshape(1, -1),
      g_final.reshape(1, -1), w_fconv, fbuf, w_out.astype(BF16), w_up.astype(BF16),
      w_down.astype(BF16))
    return y, fn[:, ns - 1]


PROMPT_TM = 512
PROMPT_TQ_CMP = 256
PROMPT_TQ = 128
PROMPT_TK_SEL = 512
PROMPT_TK_WIN = 256


def _kv_rows(kv_t):
    batch, _, rows = kv_t.shape
    return kv_t.reshape(batch, 2, NSA_KV_HEADS, HEAD_DIM, rows).transpose(0, 4, 1, 2, 3)


def _kv_feature_major(kv5):
    batch, rows = kv5.shape[:2]
    return kv5.transpose(0, 2, 3, 4, 1).reshape(batch, 2 * KV_WIDTH, rows)


def _prompt_layer(x, wts):
    batch, seq, _ = x.shape
    x2d = x.reshape(batch * seq, D_MODEL)
    q, kc_rows, vc_rows, mu, mv, mo, gt, kvc_t, kvs_t, kvw_t = _in_proj(
        x2d, wts["g_mix"], wts["w_in_packed"], batch=batch, seq=seq, tm=min(PROMPT_TM, seq))
    H, DH, W = MLSTM_HEADS, MLSTM_DH, MLSTM_WIDTH
    o_m, mconv, c_new, n_new, m_new = _mlstm(
        mu, mv, mo, gt, jnp.zeros((batch, MLSTM_CONV - 1, W), F32), jnp.zeros((batch, H, DH, DH), F32),
        jnp.zeros((batch, H, DH), F32), jnp.zeros((batch, H), F32),
        wts["w_mconv"], wts["b_mconv"], wts["w_mq"], wts["w_mk"], wts["b_ig"], wts["b_fg"],
        wts["g_mhead"], wts["m_skip"], batch=batch, seq=seq)
    kce, kco = _compress_prompt(kc_rows, vc_rows, wts["cw"], batch=batch, seq=seq)
    n_sel = -(-seq // SEL_BLOCK)
    assert n_sel <= SEL_BLOCK
    o_cmp, scores_t = _cmp_attn(q, kce, kco, batch=batch, seq=seq, tq=min(PROMPT_TQ_CMP, seq), pos0=0)
    selb = _topk_blocks(scores_t.reshape(batch * NSA_KV_HEADS, -1, seq),
                        jnp.arange(seq, dtype=jnp.int32).reshape(1, seq),
                        n_sel=n_sel, nsw=SEL_BLOCK, tt=min(PROMPT_TM, seq))
    selb = selb.reshape(batch, NSA_KV_HEADS, SEL_BLOCK, seq)
    q3d = q.reshape(batch, seq, NSA_WIDTH)
    o_sel = _attn_prompt(q3d, kvs_t, selb, tq=PROMPT_TQ, tk=min(PROMPT_TK_SEL, seq), window=None)
    o_win = _attn_prompt(q3d, kvw_t, None, tq=PROMPT_TQ, tk=PROMPT_TK_WIN, window=WINDOW)
    fbuf = jnp.zeros((batch, FFN_CONV - 1, 2 * D_FF), F32)
    y, f_new = _ffn(x2d, o_m, o_cmp, o_sel.reshape(-1, NSA_WIDTH), o_win.reshape(-1, NSA_WIDTH), gt,
                    fbuf, wts["w_out"], wts["g_nsa"], wts["g_ffn"], wts["g_final"], wts["w_up"],
                    wts["w_fconv"], wts["w_down"], nb=batch, tm=min(FFN_TM, seq), stride=1)
    n_win = min(WINDOW, seq)
    return (y.reshape(batch, seq, D_MODEL), _kv_rows(kvc_t), _kv_rows(kvs_t),
            _kv_rows(kvw_t[:, :, seq - n_win:]), mconv, c_new, n_new, m_new.reshape(batch, H), f_new)


def _decode_rows(q2d, batch, seq):
    q5 = (q2d * ATTN_SCALE).reshape(batch, seq, NSA_KV_HEADS, NSA_GROUP, HEAD_DIM).transpose(0, 2, 3, 1, 4)
    eye = jnp.eye(NSA_KV_HEADS, dtype=F32)
    qa = jnp.einsum('bkgtd,kK->bkgtKd', q5, eye)
    return qa.reshape(batch, NSA_KV_HEADS * NSA_GROUP * seq, KV_WIDTH)


def _decode_rows_out(o, batch, seq):
    o6 = o.reshape(batch, NSA_KV_HEADS, NSA_GROUP, seq, NSA_KV_HEADS, HEAD_DIM)
    o5 = jnp.stack([o6[:, kh, :, :, kh, :] for kh in range(NSA_KV_HEADS)], axis=1)
    return o5.transpose(0, 3, 1, 2, 4).reshape(batch * seq, NSA_WIDTH)


def _sample_layer(x, pool_cmp, pool_sel, win_buf, m_conv, m_c, m_n, m_m, f_buf, page_table, wts):
    batch, seq, _ = x.shape
    n_pages = page_table.shape[1]
    past = n_pages * PAGE_SIZE
    assert past % SEL_BLOCK == 0 and seq <= SEL_BLOCK and seq < CMP_BLOCK
    x2d = x.reshape(batch * seq, D_MODEL)
    q, _, _, mu, mv, mo, gt, kvc_t, kvs_t, kvw_t = _in_proj(
        x2d, wts["g_mix"], wts["w_in_packed"], batch=1, seq=batch * seq, tm=batch * seq)
    per_batch = lambda a: a.reshape(2 * KV_WIDTH, batch, seq).transpose(1, 0, 2)
    kvc_t, kvs_t, kvw_t = per_batch(kvc_t), per_batch(kvs_t), per_batch(kvw_t)
    pad_keys = lambda a: jnp.pad(a, ((0, 0), (0, 0), (0, LANES - seq)))
    H = MLSTM_HEADS
    o_m, mconv, c_new, n_new, m_new = _mlstm(
        mu, mv, mo, gt, m_conv, m_c, m_n, m_m,
        wts["w_mconv"], wts["b_mconv"], wts["w_mq"], wts["w_mk"], wts["b_ig"], wts["b_fg"],
        wts["g_mhead"], wts["m_skip"], batch=batch, seq=seq)
    pool_cmp3, pool_sel3 = _kv_feature_major(pool_cmp), _kv_feature_major(pool_sel)
    kce, kco = _compress_paged(pool_cmp3, page_table, wts["cw"], wts["cw_pages"])
    n_past_blk = past // SEL_BLOCK
    n_sel = -(-(past + seq) // SEL_BLOCK)
    o_cmp, scores_t = _cmp_attn(q, kce, kco, batch=batch, seq=seq, tq=seq, pos0=past)
    ns = scores_t.shape[2]
    nsw = ns + LANES
    scores_all = scores_t.transpose(1, 2, 0, 3).reshape(NSA_KV_HEADS, ns, batch * seq)
    pos_all = (past + jnp.arange(batch * seq, dtype=jnp.int32) % seq).reshape(1, batch * seq)
    selb = _topk_blocks(scores_all, pos_all, n_sel=n_sel, nsw=nsw, tt=batch * seq)
    selb = selb.reshape(NSA_KV_HEADS, nsw, batch, seq).transpose(2, 0, 3, 1)
    qa = _decode_rows(q, batch, seq)
    rows = qa.shape[1]
    blk_per_step = ATTN_PAGES_PER_STEP * PAGE_SIZE // SEL_BLOCK
    n_steps = n_pages // ATTN_PAGES_PER_STEP
    sb_rows = jnp.broadcast_to(selb[:, :, None], (batch, NSA_KV_HEADS, NSA_GROUP, seq, selb.shape[-1]))
    sb_rows = sb_rows.reshape(batch, rows, selb.shape[-1])
    bias_q = sb_rows[:, :, :n_past_blk].reshape(batch, rows, n_steps, blk_per_step).transpose(0, 2, 1, 3)
    bias_q = jnp.pad(bias_q, ((0, 0), (0, 0), (0, 0), (0, LANES - blk_per_step)))
    bias_new = jnp.broadcast_to(sb_rows[:, :, n_past_blk:n_past_blk + 1], (batch, rows, LANES))
    o_sel = _attn_paged(qa, bias_q, bias_new, pad_keys(kvs_t), pool_sel3, page_table, n_new=seq)
    n_buf = win_buf.shape[1]
    assert past >= n_buf
    win_t = _kv_feature_major(win_buf)
    o_win = _attn_window_small(qa, win_t, pad_keys(kvw_t), n_new=seq, past=past)
    win_new = jnp.concatenate([win_t, kvw_t], axis=2)[:, :, seq:]
    tmaj = lambda a: a.reshape(batch, seq, -1).transpose(1, 0, 2).reshape(batch * seq, -1)
    fb_t = f_buf.transpose(1, 0, 2).reshape(1, (FFN_CONV - 1) * batch, 2 * D_FF)
    y, f_new = _ffn(tmaj(x2d), tmaj(o_m), tmaj(o_cmp), tmaj(_decode_rows_out(o_sel, batch, seq)),
                    tmaj(_decode_rows_out(o_win, batch, seq)), tmaj(gt), fb_t,
                    wts["w_out"], wts["g_nsa"], wts["g_ffn"], wts["g_final"], wts["w_up"],
                    wts["w_fconv"], wts["w_down"], nb=1, tm=batch * seq, stride=batch)
    y = y.reshape(seq, batch, D_MODEL).transpose(1, 0, 2)
    f_new = f_new.reshape(FFN_CONV - 1, batch, 2 * D_FF).transpose(1, 0, 2)
    return (y, _kv_rows(kvc_t), _kv_rows(kvs_t), _kv_rows(win_new), mconv, c_new, n_new,
            m_new.reshape(batch, H), f_new)


def kernel(x_prompt, x_sample, cache_cmp, cache_sel, state_win, state_mlstm_C, state_mlstm_n,
           state_mlstm_m, state_mlstm_conv, state_ffn_conv, page_table,
           g_mix, w_in, w_out, w_mconv, b_mconv, w_mq, w_mk, b_ig, b_fg, g_mhead, m_skip,
           pe_cmp, w_cmp1, w_cmp2, g_nsa, g_ffn, w_up, w_fconv, w_down, g_final):
    assert w_in.shape[0] == 1, "one layer: the final norm is fused into the layer's FFN kernel"
    l = 0
    wts = dict(g_mix=g_mix[l], w_in_packed=_pack_w_in(w_in[l]), w_out=w_out[l], w_mconv=w_mconv[l],
               b_mconv=b_mconv[l], w_mq=w_mq[l], w_mk=w_mk[l], b_ig=b_ig[l], b_fg=b_fg[l],
               g_mhead=g_mhead[l], m_skip=m_skip[l],
               cw=_pack_compress_weights(pe_cmp[l], w_cmp1[l], w_cmp2[l]),
               cw_pages=_page_pair_constants(pe_cmp[l]),
               g_nsa=g_nsa[l], g_ffn=g_ffn[l], g_final=g_final, w_up=w_up[l], w_fconv=w_fconv[l],
               w_down=w_down[l])
    p = _prompt_layer(x_prompt, wts)
    s = _sample_layer(x_sample, cache_cmp[l], cache_sel[l], state_win[l], state_mlstm_conv[l],
                      state_mlstm_C[l], state_mlstm_n[l], state_mlstm_m[l], state_ffn_conv[l],
                      page_table, wts)
    yp, cmp_p, sel_p, win_p, mconv_p, c_p, n_p, m_p, fconv_p = p
    ys, cmp_s, sel_s, win_s, mconv_s, c_s, n_s, m_s, fconv_s = s
    st = lambda a: a[None]
    return (yp, ys, st(cmp_p), st(cmp_s), st(sel_p), st(sel_s), st(win_p), st(win_s),
            st(c_p), st(c_s), st(n_p), st(n_s), st(m_p), st(m_s), st(mconv_p), st(mconv_s),
            st(fconv_p), st(fconv_s))
```

```python
import functools

import numpy as np
import jax
import jax.numpy as jnp
from jax import lax
from jax.experimental import pallas as pl
from jax.experimental.pallas import tpu as pltpu

F32 = jnp.float32
BF16 = jnp.bfloat16

D_MODEL = 1024
PAGE_SIZE = 128
HEAD_DIM = 64
NSA_HEADS = 8
NSA_KV_HEADS = 2
NSA_GROUP = NSA_HEADS // NSA_KV_HEADS
NSA_WIDTH = NSA_HEADS * HEAD_DIM
KV_WIDTH = NSA_KV_HEADS * HEAD_DIM
CMP_BLOCK = 32
CMP_HIDDEN = 2 * HEAD_DIM
SEL_BLOCK = 64
TOP_N = 16
WINDOW = 512
N_BRANCH = 3
ATTN_SCALE = HEAD_DIM ** -0.5
MLSTM_HEADS = 4
MLSTM_WIDTH = D_MODEL - NSA_WIDTH
MLSTM_DH = MLSTM_WIDTH // MLSTM_HEADS
MLSTM_CONV = 4
D_FF = ((8 * D_MODEL // 3 + 127) // 128) * 128
FFN_CONV = 3
EPS = 1e-6
NEG_INF = -1e30
SEL_PRIORITY = 1e4
LOG2_E = 1.4426950408889634

LANES = 128
SUBLANES = 8
VMEM_LIMIT = 48 * 1024 * 1024

GATE_COL_NSA = 0
GATE_COL_I = NSA_HEADS * N_BRANCH
GATE_COL_F = GATE_COL_I + MLSTM_HEADS

MLSTM_CHUNK = 128
MLSTM_SEQS_PER_STEP = 4


def _cparams(sem):
    return pltpu.CompilerParams(dimension_semantics=sem, vmem_limit_bytes=VMEM_LIMIT)


def _dot(a, b):
    return jnp.dot(a, b, preferred_element_type=F32)


def _dot_nt(a, b):
    return lax.dot_general(a, b, (((1,), (1,)), ((), ())), preferred_element_type=F32)


def _sigmoid(x):
    return 1.0 / (1.0 + jnp.exp(-x))


def _silu(x):
    return x * _sigmoid(x)


def _rms(x, g):
    return x * lax.rsqrt(jnp.mean(x * x, axis=-1, keepdims=True) + EPS) * g


IN_ROW_WIDTHS = (NSA_WIDTH, KV_WIDTH, KV_WIDTH, MLSTM_WIDTH, MLSTM_WIDTH, MLSTM_WIDTH, LANES)
N_KV_BRANCH = 3


def _inproj_body(x_ref, g_ref, w_ref, wt_ref, *out_refs):
    xb = _rms(x_ref[...], g_ref[...]).astype(BF16)
    off = 0
    for ref in out_refs[:len(IN_ROW_WIDTHS)]:
        n = ref.shape[-1]
        ref[...] = _dot(xb, w_ref[:, off:off + n])
        off += n
    for n, ref in enumerate(out_refs[len(IN_ROW_WIDTHS):]):
        ref[0] = _dot_nt(wt_ref[n * 2 * KV_WIDTH:(n + 1) * 2 * KV_WIDTH, :], xb)


def _pack_w_in(w_in):
    splits = np.cumsum([NSA_WIDTH, 2 * KV_WIDTH, 2 * KV_WIDTH, 2 * KV_WIDTH, NSA_HEADS * N_BRANCH,
                        MLSTM_WIDTH, MLSTM_WIDTH, MLSTM_WIDTH, MLSTM_HEADS]).tolist()
    q, kvc, kvs, kvw, gt, mu, mv, mo, mi, mf = jnp.split(w_in, splits, axis=1)
    gates = jnp.concatenate([gt, mi, mf], axis=1)
    gates = jnp.pad(gates, ((0, 0), (0, LANES - gates.shape[1])))
    w_rows = jnp.concatenate([q, kvc, mu, mv, mo, gates], axis=1).astype(BF16)
    w_kv_t = jnp.concatenate([kvc, kvs, kvw], axis=1).T.astype(BF16)
    return w_rows, w_kv_t


def _in_proj(x2d, g_mix, w_packed, *, batch, seq, tm):
    w_rows, w_kv_t = w_packed
    t = x2d.shape[0]
    ns = seq // tm
    kv_sd = jax.ShapeDtypeStruct((batch, 2 * KV_WIDTH, seq), F32)
    return pl.pallas_call(
        _inproj_body,
        grid=(t // tm,),
        in_specs=[pl.BlockSpec((tm, D_MODEL), lambda i: (i, 0)),
                  pl.BlockSpec((1, D_MODEL), lambda i: (0, 0)),
                  pl.BlockSpec(w_rows.shape, lambda i: (0, 0)),
                  pl.BlockSpec(w_kv_t.shape, lambda i: (0, 0))],
        out_specs=[pl.BlockSpec((tm, n), lambda i: (i, 0)) for n in IN_ROW_WIDTHS]
        + [pl.BlockSpec((1, 2 * KV_WIDTH, tm), lambda i: (i // ns, 0, i % ns))] * N_KV_BRANCH,
        out_shape=[jax.ShapeDtypeStruct((t, n), F32) for n in IN_ROW_WIDTHS] + [kv_sd] * N_KV_BRANCH,
        compiler_params=_cparams(("arbitrary",)),
        name="in_proj",
    )(x2d, g_mix.reshape(1, D_MODEL), w_rows, w_kv_t)


def _mlstm_body(*refs, valid, bb):
    cb_ref, c0_ref, n0_ref, m0_ref = refs[4:8]
    cn_ref, c_ref, n_ref, m_ref, xx_ref = refs[16:21]
    halo = SUBLANES

    @pl.when(pl.program_id(1) == 0)
    def _():
        xx_ref[:, 0:halo, :] = jnp.zeros((bb, halo, MLSTM_WIDTH), F32)
        xx_ref[:, halo - (MLSTM_CONV - 1):halo, :] = cb_ref[...]
        c_ref[...] = c0_ref[...]
        n_ref[...] = n0_ref[...]
        m_ref[...] = m0_ref[...]

    _mlstm_chunk(*refs, valid=valid, bb=bb)


def _mlstm_chunk(mu_ref, mv_ref, mo_ref, g_ref, cb_ref, c0_ref, n0_ref, m0_ref,
                 wc_ref, bc_ref, wq_ref, wk_ref, gb_ref, gh_ref, sk_ref,
                 o_ref, cn_ref, c_ref, n_ref, m_ref,
                 xx_ref, vpad_ref, gpad_ref, *, valid, bb):
    L = MLSTM_CHUNK
    DH = MLSTM_DH
    halo = SUBLANES
    units = [(bi, h) for bi in range(bb) for h in range(MLSTM_HEADS)]
    head_lanes = lambda h: slice(h * DH, (h + 1) * DH)
    row = lax.broadcasted_iota(jnp.int32, (L, L), 0)
    col = lax.broadcasted_iota(jnp.int32, (L, L), 1)
    tril = row >= col
    triu = row <= col
    tok_col = lax.broadcasted_iota(jnp.int32, (L, 1), 0)
    tok_row = lax.broadcasted_iota(jnp.int32, (1, L), 1)

    def log_sigmoid(x):
        return jnp.minimum(x, 0.0) - jnp.log(1.0 + jnp.exp(-jnp.abs(x)))

    uc, gb, gbt = {}, {}, {}
    for bi in range(bb):
        if valid < L:
            xx_ref[bi, halo:, :] = jnp.zeros((L, MLSTM_WIDTH), F32)
            vpad_ref[bi] = jnp.zeros((L, MLSTM_WIDTH), F32)
            gpad_ref[bi] = jnp.zeros((L, LANES), F32)
        xx_ref[bi, halo:halo + valid, :] = mu_ref[bi]
        vpad_ref[bi, 0:valid, :] = mv_ref[bi]
        gpad_ref[bi, 0:valid, :] = g_ref[bi]
        conv = xx_ref[bi, halo - 3:halo - 3 + L, :] * wc_ref[0:1, :]
        for j in range(1, MLSTM_CONV):
            conv = conv + xx_ref[bi, halo - 3 + j:halo - 3 + j + L, :] * wc_ref[j:j + 1, :]
        uc[bi] = _silu(conv + bc_ref[...])
        tail = xx_ref[bi, valid + halo - 3:valid + halo, :]
        xx_ref[bi, halo - 3:halo, :] = tail
        cn_ref[bi] = tail
        gb[bi] = gpad_ref[bi] + gb_ref[...]
        gbt[bi] = gb[bi].T

    q, k, qb, kb = {}, {}, {}, {}
    for u in units:
        bi, h = u
        ub = uc[bi][:, head_lanes(h)].astype(BF16)
        q[u] = _dot(ub, wq_ref[h])
        k[u] = _dot(ub, wk_ref[h]) * (DH ** -0.5)
        qb[u], kb[u] = q[u].astype(BF16), k[u].astype(BF16)

    ic_col, ic_row, cum_col, cum_row = {}, {}, {}, {}
    for u in units:
        bi, h = u
        ic_c = gb[bi][:, GATE_COL_I + h:GATE_COL_I + h + 1]
        ic_r = gbt[bi][GATE_COL_I + h:GATE_COL_I + h + 1, :]
        lf_c = log_sigmoid(gb[bi][:, GATE_COL_F + h:GATE_COL_F + h + 1])
        lf_r = log_sigmoid(gbt[bi][GATE_COL_F + h:GATE_COL_F + h + 1, :])
        if valid < L:
            ic_c = jnp.where(tok_col < valid, ic_c, NEG_INF)
            ic_r = jnp.where(tok_row < valid, ic_r, NEG_INF)
            lf_c = jnp.where(tok_col < valid, lf_c, 0.0)
            lf_r = jnp.where(tok_row < valid, lf_r, 0.0)
        ic_col[u], ic_row[u] = ic_c, ic_r
        cum_col[u] = jnp.sum(jnp.where(tril, lf_r, 0.0), axis=1, keepdims=True)
        cum_row[u] = jnp.sum(jnp.where(triu, lf_c, 0.0), axis=0, keepdims=True)

    m_t, w, sc = {}, {}, {}
    for u in units:
        bi, h = u
        m0 = m_ref[bi, 0:1, h:h + 1]
        dmat = jnp.where(tril, cum_col[u] - cum_row[u] + ic_row[u], NEG_INF)
        inter = cum_col[u] + m0
        m_t[u] = jnp.maximum(inter, jnp.max(dmat, axis=1, keepdims=True))
        w[u] = jnp.exp(dmat - m_t[u])
        sc[u] = jnp.exp(inter - m_t[u])

    hc = {}
    for u in units:
        bi, h = u
        s = _dot_nt(qb[u], kb[u]) * w[u]
        v = vpad_ref[bi, :, head_lanes(h)]
        c_old = c_ref[bi, h]
        n_old = n_ref[bi, h:h + 1, :]
        num = _dot(s.astype(BF16), v.astype(BF16)) + sc[u] * _dot_nt(qb[u], c_old.astype(BF16))
        den = (jnp.sum(s, axis=1, keepdims=True)
               + sc[u] * jnp.sum(q[u] * n_old, axis=1, keepdims=True))
        hc[u] = num / jnp.maximum(jnp.abs(den), jnp.exp(-m_t[u]))

    for u in units:
        bi, h = u
        m0 = m_ref[bi, 0:1, h:h + 1]
        m_new = m_t[u][L - 1:L, :]
        cum_last = cum_col[u][L - 1:L, :]
        wl = jnp.exp(cum_last - cum_col[u] + ic_col[u] - m_new)
        sl = jnp.exp(cum_last + m0 - m_new)
        v = vpad_ref[bi, :, head_lanes(h)]
        vw_t = (v * wl).T.astype(BF16)
        c_ref[bi, h] = sl * c_ref[bi, h] + _dot(vw_t, kb[u])
        n_ref[bi, h:h + 1, :] = sl * n_ref[bi, h:h + 1, :] + jnp.sum(wl * k[u], axis=0, keepdims=True)
        m_ref[bi, 0:1, h:h + 1] = m_new

    for u in units:
        bi, h = u
        hn = _rms(hc[u], gh_ref[:, head_lanes(h)])
        u_h = uc[bi][:, head_lanes(h)]
        out = ((hn[0:valid, :] + sk_ref[:, head_lanes(h)] * u_h[0:valid, :])
               * _sigmoid(mo_ref[bi, :, head_lanes(h)]))
        o_ref[bi, :, head_lanes(h)] = out


def _mlstm(mu, mv, mo, gates, conv_buf, c0, n0, m0, w_mconv, b_mconv, w_mq, w_mk, b_ig, b_fg,
           g_mhead, m_skip, *, batch, seq):
    L = MLSTM_CHUNK
    valid = min(seq, L)
    assert seq % valid == 0 and (valid == L or seq == valid)
    nc = seq // valid
    gate_bias = jnp.zeros((1, LANES), F32)
    gate_bias = gate_bias.at[0, GATE_COL_I:GATE_COL_I + MLSTM_HEADS].set(b_ig)
    gate_bias = gate_bias.at[0, GATE_COL_F:GATE_COL_F + MLSTM_HEADS].set(b_fg)
    bb = MLSTM_SEQS_PER_STEP
    assert batch % bb == 0
    tok = lambda b, c: (b, c, 0)
    const2 = lambda b, c: (0, 0)
    const3 = lambda b, c: (0, 0, 0)
    per_b3 = lambda b, c: (b, 0, 0)
    per_b4 = lambda b, c: (b, 0, 0, 0)
    H, DH, W = MLSTM_HEADS, MLSTM_DH, MLSTM_WIDTH
    rows3 = lambda a: a.reshape(batch, seq, a.shape[-1])
    o_m, conv_new, c_new, n_new, m_new = pl.pallas_call(
        functools.partial(_mlstm_body, valid=valid, bb=bb),
        grid=(batch // bb, nc),
        in_specs=[pl.BlockSpec((bb, valid, W), tok), pl.BlockSpec((bb, valid, W), tok),
                  pl.BlockSpec((bb, valid, W), tok), pl.BlockSpec((bb, valid, LANES), tok),
                  pl.BlockSpec((bb, MLSTM_CONV - 1, W), per_b3),
                  pl.BlockSpec((bb, H, DH, DH), per_b4),
                  pl.BlockSpec((bb, H, DH), per_b3),
                  pl.BlockSpec((bb, 1, H), per_b3),
                  pl.BlockSpec((MLSTM_CONV, W), const2), pl.BlockSpec((1, W), const2),
                  pl.BlockSpec((H, DH, DH), const3), pl.BlockSpec((H, DH, DH), const3),
                  pl.BlockSpec((1, LANES), const2), pl.BlockSpec((1, W), const2),
                  pl.BlockSpec((1, W), const2)],
        out_specs=[pl.BlockSpec((bb, valid, W), tok),
                   pl.BlockSpec((bb, MLSTM_CONV - 1, W), per_b3),
                   pl.BlockSpec((bb, H, DH, DH), per_b4),
                   pl.BlockSpec((bb, H, DH), per_b3),
                   pl.BlockSpec((bb, 1, H), per_b3)],
        out_shape=[jax.ShapeDtypeStruct((batch, seq, W), F32),
                   jax.ShapeDtypeStruct((batch, MLSTM_CONV - 1, W), F32),
                   jax.ShapeDtypeStruct((batch, H, DH, DH), F32),
                   jax.ShapeDtypeStruct((batch, H, DH), F32),
                   jax.ShapeDtypeStruct((batch, 1, H), F32)],
        scratch_shapes=[pltpu.VMEM((bb, SUBLANES + L, W), F32), pltpu.VMEM((bb, L, W), F32),
                        pltpu.VMEM((bb, L, LANES), F32)],
        compiler_params=_cparams(("arbitrary", "arbitrary")),
        name="mlstm",
    )(rows3(mu), rows3(mv), rows3(mo), rows3(gates), conv_buf, c0, n0, m0.reshape(batch, 1, H),
      w_mconv, b_mconv.reshape(1, W), w_mq.astype(BF16), w_mk.astype(BF16), gate_bias,
      g_mhead.reshape(1, W), m_skip.reshape(1, W))
    return o_m.reshape(batch * seq, W), conv_new, c_new, n_new, m_new


def _compress_rows(xk_ref, xv_ref, pe_ref, w1_ref, w2_ref, n_pairs):
    pair_rows = 2 * CMP_BLOCK
    outs = []
    for kv, x_ref in enumerate((xk_ref, xv_ref)):
        acc = jnp.zeros((2 * n_pairs, NSA_KV_HEADS * CMP_HIDDEN), F32)
        for r in range(CMP_BLOCK):
            ev = x_ref[pl.ds(r, n_pairs, stride=pair_rows), :]
            od = x_ref[pl.ds(CMP_BLOCK + r, n_pairs, stride=pair_rows), :]
            xr = jnp.concatenate([ev, od], axis=0) + pe_ref[kv, r:r + 1, :]
            acc = acc + _dot(xr.astype(BF16), w1_ref[kv, r])
        outs.append(_dot(_silu(acc).astype(BF16), w2_ref[kv]))
    return jnp.concatenate(outs, axis=1)


def _compress_body(xk_ref, xv_ref, pe_ref, w1_ref, w2_ref, oe_ref, oo_ref, *, n_pairs):
    out = _compress_rows(xk_ref, xv_ref, pe_ref, w1_ref, w2_ref, n_pairs)
    oe_ref[0] = out[0:n_pairs, :]
    oo_ref[0] = out[n_pairs:, :]


BLOCKS_PER_PAGE = PAGE_SIZE // CMP_BLOCK


def _compress_paged_body(pt_ref, *refs, n_pages):
    page_refs = refs[:n_pages]
    pet_ref, perm_ref, w1_ref, w2_ref, oe_ref, oo_ref, buf_ref, os_ref = refs[n_pages:]
    grp = 2 * BLOCKS_PER_PAGE
    for jp in range(n_pages // 2):
        xt = jnp.concatenate([page_refs[2 * jp][0], page_refs[2 * jp + 1][0]], axis=1)
        xb = (xt + pet_ref[...]).astype(BF16)
        xp = _dot_nt(perm_ref[...], xb)
        for r in range(CMP_BLOCK):
            for kv in range(2):
                lane0 = (2 * kv + r % 2) * KV_WIDTH
                buf_ref[r // 2, grp * jp:grp * (jp + 1), lane0:lane0 + KV_WIDTH] = (
                    xp[grp * r:grp * (r + 1), kv * KV_WIDTH:(kv + 1) * KV_WIDTH])
    for kv in range(2):
        lanes = slice(2 * kv * KV_WIDTH, 2 * (kv + 1) * KV_WIDTH)
        acc = _dot(buf_ref[0, :, lanes].astype(BF16), w1_ref[kv, 0])
        for r2 in range(1, CMP_BLOCK // 2):
            acc = acc + _dot(buf_ref[r2, :, lanes].astype(BF16), w1_ref[kv, r2])
        os_ref[kv] = _dot(_silu(acc).astype(BF16), w2_ref[kv])
    half = os_ref.shape[1] // 2
    for parity, ref in enumerate((oe_ref, oo_ref)):
        ref[0] = jnp.concatenate([os_ref[kv, pl.ds(parity, half, stride=2), :] for kv in range(2)],
                                 axis=1)


def _page_pair_constants(pe):
    pe_t = jnp.broadcast_to(pe.transpose(0, 2, 1)[:, None, :, None, :],
                            (2, NSA_KV_HEADS, HEAD_DIM, 2 * BLOCKS_PER_PAGE, CMP_BLOCK))
    pe_t = pe_t.reshape(2 * KV_WIDTH, 2 * PAGE_SIZE)
    grp = 2 * BLOCKS_PER_PAGE
    perm = np.zeros((2 * PAGE_SIZE, 2 * PAGE_SIZE), np.float32)
    for r in range(CMP_BLOCK):
        for b in range(grp):
            perm[r * grp + b, b * CMP_BLOCK + r] = 1.0
    return pe_t, jnp.asarray(perm, BF16)


def _pack_compress_weights(pe, w1, w2):
    eye_h = jnp.eye(NSA_KV_HEADS, dtype=F32)
    pe_r = jnp.broadcast_to(pe[:, :, None, :], (2, CMP_BLOCK, NSA_KV_HEADS, HEAD_DIM))
    pe_r = pe_r.reshape(2, CMP_BLOCK, KV_WIDTH)
    w1r = w1.reshape(2, CMP_BLOCK, HEAD_DIM, CMP_HIDDEN)
    w1_big = jnp.einsum('krdc,hH->krhdHc', w1r, eye_h)
    w1_big = w1_big.reshape(2, CMP_BLOCK, KV_WIDTH, NSA_KV_HEADS * CMP_HIDDEN).astype(BF16)
    w2_big = jnp.einsum('kcd,hH->khcHd', w2, eye_h)
    w2_big = w2_big.reshape(2, NSA_KV_HEADS * CMP_HIDDEN, KV_WIDTH).astype(BF16)
    return pe_r, w1_big, w2_big


def _compress_prompt(k_rows, v_rows, cw, *, batch, seq):
    pe_r, w1_big, w2_big = cw
    n_pairs = seq // (2 * CMP_BLOCK)
    const3 = lambda b: (0, 0, 0)
    out_sd = jax.ShapeDtypeStruct((batch, n_pairs, 2 * KV_WIDTH), F32)
    return pl.pallas_call(
        functools.partial(_compress_body, n_pairs=n_pairs),
        grid=(batch,),
        in_specs=[pl.BlockSpec((seq, KV_WIDTH), lambda b: (b, 0)),
                  pl.BlockSpec((seq, KV_WIDTH), lambda b: (b, 0)),
                  pl.BlockSpec(pe_r.shape, const3),
                  pl.BlockSpec(w1_big.shape, lambda b: (0, 0, 0, 0)),
                  pl.BlockSpec(w2_big.shape, const3)],
        out_specs=[pl.BlockSpec((1, n_pairs, 2 * KV_WIDTH), lambda b: (b, 0, 0))] * 2,
        out_shape=[out_sd, out_sd],
        compiler_params=_cparams(("arbitrary",)),
        name="compress_prompt",
    )(k_rows, v_rows, pe_r, w1_big, w2_big)


COMPRESS_PAGES_PER_STEP = 32


def _compress_paged(pool, page_table, cw, cw_pages):
    _, w1_big, w2_big = cw
    w1_big = w1_big.reshape(2, CMP_BLOCK // 2, 2 * KV_WIDTH, NSA_KV_HEADS * CMP_HIDDEN)
    pe_t, perm = cw_pages
    batch, n_pages = page_table.shape
    pps = COMPRESS_PAGES_PER_STEP
    assert n_pages % pps == 0 and pps % 2 == 0
    n_steps = n_pages // pps
    n_blk = pps * BLOCKS_PER_PAGE
    const3 = lambda b, c, pt: (0, 0, 0)

    def page_spec(j):
        return pl.BlockSpec((1, 2 * KV_WIDTH, PAGE_SIZE),
                            lambda b, c, pt: (pt[(b * n_steps + c) * pps + j], 0, 0))

    return pl.pallas_call(
        functools.partial(_compress_paged_body, n_pages=pps),
        grid_spec=pltpu.PrefetchScalarGridSpec(
            num_scalar_prefetch=1,
            grid=(batch, n_steps),
            in_specs=[page_spec(j) for j in range(pps)] + [
                pl.BlockSpec(pe_t.shape, lambda b, c, pt: (0, 0)),
                pl.BlockSpec(perm.shape, lambda b, c, pt: (0, 0)),
                pl.BlockSpec(w1_big.shape, lambda b, c, pt: (0, 0, 0, 0)),
                pl.BlockSpec(w2_big.shape, const3)],
            out_specs=[pl.BlockSpec((1, n_blk // 2, 2 * KV_WIDTH), lambda b, c, pt: (b, c, 0))] * 2,
            scratch_shapes=[pltpu.VMEM((CMP_BLOCK // 2, n_blk, 4 * KV_WIDTH), F32),
                            pltpu.VMEM((2, n_blk, KV_WIDTH), F32)]),
        out_shape=[jax.ShapeDtypeStruct((batch, n_steps * n_blk // 2, 2 * KV_WIDTH), F32)] * 2,
        compiler_params=_cparams(("arbitrary", "arbitrary")),
        name="compress_paged",
    )(page_table.reshape(-1), *([pool] * pps), pe_t, perm, w1_big, w2_big)


def _cmp_attn_body(q_ref, ke_ref, ko_ref, o_ref, st_ref, *, tq, pos0):
    ns = ke_ref.shape[1]
    i = pl.program_id(1)
    rows = NSA_GROUP * tq
    tok0 = pos0 + i * tq
    pos_c = tok0 + lax.broadcasted_iota(jnp.int32, (rows, 1), 0) % tq
    pos_r = tok0 + lax.broadcasted_iota(jnp.int32, (1, rows), 1) % tq
    pair_r = lax.broadcasted_iota(jnp.int32, (1, ns), 1)
    pair_c = lax.broadcasted_iota(jnp.int32, (ns, 1), 0)
    end_e = lambda pair: (2 * pair + 1) * CMP_BLOCK - 1
    end_o = lambda pair: (2 * pair + 2) * CMP_BLOCK - 1
    any_c = (CMP_BLOCK - 1 <= pos_c).astype(F32)
    any_r = (CMP_BLOCK - 1 <= pos_r).astype(F32)
    q = q_ref[...] * ATTN_SCALE
    for kh in range(NSA_KV_HEADS):
        qs = jnp.concatenate([q[:, (kh * NSA_GROUP + g) * HEAD_DIM:(kh * NSA_GROUP + g + 1) * HEAD_DIM]
                              for g in range(NSA_GROUP)], axis=0).astype(BF16)
        ks, vs = slice(kh * HEAD_DIM, (kh + 1) * HEAD_DIM), slice(KV_WIDTH + kh * HEAD_DIM,
                                                                   KV_WIDTH + (kh + 1) * HEAD_DIM)
        ke, ko = ke_ref[0, :, ks].astype(BF16), ko_ref[0, :, ks].astype(BF16)
        se = jnp.where(end_e(pair_r) <= pos_c, _dot_nt(qs, ke), NEG_INF)
        so = jnp.where(end_o(pair_r) <= pos_c, _dot_nt(qs, ko), NEG_INF)
        mx = jnp.maximum(jnp.max(se, axis=1, keepdims=True), jnp.max(so, axis=1, keepdims=True))
        pe, po = jnp.exp(se - mx), jnp.exp(so - mx)
        inv = any_c / (jnp.sum(pe, axis=1, keepdims=True) + jnp.sum(po, axis=1, keepdims=True))
        oh = (_dot((pe * inv).astype(BF16), ke_ref[0, :, vs].astype(BF16))
              + _dot((po * inv).astype(BF16), ko_ref[0, :, vs].astype(BF16)))
        for g in range(NSA_GROUP):
            hd = kh * NSA_GROUP + g
            o_ref[:, hd * HEAD_DIM:(hd + 1) * HEAD_DIM] = oh[g * tq:(g + 1) * tq, :]
        te = jnp.where(end_e(pair_c) <= pos_r, _dot_nt(ke, qs), NEG_INF)
        to = jnp.where(end_o(pair_c) <= pos_r, _dot_nt(ko, qs), NEG_INF)
        mt = jnp.maximum(jnp.max(te, axis=0, keepdims=True), jnp.max(to, axis=0, keepdims=True))
        pte, pto = jnp.exp(te - mt), jnp.exp(to - mt)
        invt = any_r / (jnp.sum(pte, axis=0, keepdims=True) + jnp.sum(pto, axis=0, keepdims=True))
        ps = (pte + pto) * invt
        score = ps[:, 0:tq]
        for g in range(1, NSA_GROUP):
            score = score + ps[:, g * tq:(g + 1) * tq]
        st_ref[0, kh] = score


def _cmp_attn(q2d, kce, kco, *, batch, seq, tq, pos0):
    ns = kce.shape[1]
    nq = seq // tq
    return pl.pallas_call(
        functools.partial(_cmp_attn_body, tq=tq, pos0=pos0),
        grid=(batch, nq),
        in_specs=[pl.BlockSpec((tq, NSA_WIDTH), lambda b, i: (b * nq + i, 0)),
                  pl.BlockSpec((1, ns, 2 * KV_WIDTH), lambda b, i: (b, 0, 0)),
                  pl.BlockSpec((1, ns, 2 * KV_WIDTH), lambda b, i: (b, 0, 0))],
        out_specs=[pl.BlockSpec((tq, NSA_WIDTH), lambda b, i: (b * nq + i, 0)),
                   pl.BlockSpec((1, NSA_KV_HEADS, ns, tq), lambda b, i: (b, 0, 0, i))],
        out_shape=[jax.ShapeDtypeStruct((batch * seq, NSA_WIDTH), F32),
                   jax.ShapeDtypeStruct((batch, NSA_KV_HEADS, ns, seq), F32)],
        compiler_params=_cparams(("arbitrary", "arbitrary")),
        name="cmp_attn",
    )(q2d, kce, kco)


def _topk_body(pos_ref, st_ref, b_ref, *, n_sel):
    score = st_ref[0]
    ns, tt = score.shape
    nsw = b_ref.shape[1]
    if nsw > ns:
        score = jnp.concatenate([score, jnp.zeros((nsw - ns, tt), F32)], axis=0)
    blk = lax.broadcasted_iota(jnp.int32, (nsw, 1), 0)
    blk_f = blk.astype(F32)
    cur = pos_ref[...] // SEL_BLOCK
    forced = (blk == 0) | (blk == cur) | (blk == cur - 1)
    pri = jnp.where(blk <= cur, jnp.where(forced, SEL_PRIORITY, score), -SEL_PRIORITY)
    pri = jnp.where(blk < n_sel, pri, -jnp.inf)
    bias = jnp.full((nsw, tt), NEG_INF, F32)
    for _ in range(min(TOP_N, n_sel)):
        top = jnp.max(pri, axis=0, keepdims=True)
        first = jnp.min(jnp.where(pri == top, blk_f, float(nsw)), axis=0, keepdims=True)
        hit = blk_f == first
        bias = jnp.where(hit, 0.0, bias)
        pri = jnp.where(hit, -jnp.inf, pri)
    b_ref[0] = bias


def _topk_blocks(scores_t, pos, *, n_sel, nsw, tt):
    groups, ns, tokens = scores_t.shape
    assert nsw >= max(ns, n_sel) and tokens % tt == 0
    return pl.pallas_call(
        functools.partial(_topk_body, n_sel=n_sel),
        grid=(groups, tokens // tt),
        in_specs=[pl.BlockSpec((1, tt), lambda g, i: (0, i)),
                  pl.BlockSpec((1, ns, tt), lambda g, i: (g, 0, i))],
        out_specs=pl.BlockSpec((1, nsw, tt), lambda g, i: (g, 0, i)),
        out_shape=jax.ShapeDtypeStruct((groups, nsw, tokens), F32),
        compiler_params=_cparams(("arbitrary", "arbitrary")),
        name="topk_blocks",
    )(pos, scores_t)


def _softmax_update(sc, vt_bf16, m_ref, l_ref, acc_ref):
    m_old = m_ref[...]
    m_new = jnp.maximum(m_old, jnp.max(sc, axis=1, keepdims=True))
    alpha = jnp.exp(m_old - m_new)
    pr = jnp.exp(sc - jnp.concatenate([m_new] * (sc.shape[1] // LANES), axis=1))
    l_ref[...] = alpha * l_ref[...] + jnp.sum(pr, axis=1, keepdims=True)
    acc_ref[...] = alpha * acc_ref[...] + _dot_nt(pr.astype(BF16), vt_bf16)
    m_ref[...] = m_new


def _softmax_init(m_ref, l_ref, acc_ref):
    m_ref[...] = jnp.full(m_ref.shape, NEG_INF, F32)
    l_ref[...] = jnp.zeros(l_ref.shape, F32)
    acc_ref[...] = jnp.zeros(acc_ref.shape, F32)


def _block_onehot_t(first_key, n_keys):
    blk = (first_key + lax.broadcasted_iota(jnp.int32, (1, n_keys), 1)) // SEL_BLOCK
    r = lax.broadcasted_iota(jnp.int32, (LANES, 1), 0) & (SEL_BLOCK - 1)
    return (r == blk).astype(F32)


ATTN_TAB_COLS = 5


def _attn_pairs(seq, tq, tk, window):
    rows = []
    for i in range(seq // tq):
        t_lo, t_hi = i * tq, i * tq + tq - 1
        k_lo = 0 if window is None else max(0, t_lo - window + 1)
        js = list(range(k_lo // tk, t_hi // tk + 1))
        for n, j in enumerate(js):
            partial_tile = j * tk + tk - 1 > t_lo or (window is not None and j * tk <= t_hi - window)
            rows.append((i, j, int(n == 0), int(n == len(js) - 1), int(partial_tile)))
    return np.asarray(rows, np.int32)


def _attn_body(tab_ref, q_ref, kv_ref, *rest, tq, tk, window, use_bias):
    if use_bias:
        sb_ref, o_ref, qa_ref, m_ref, l_ref, acc_ref = rest
    else:
        o_ref, qa_ref, m_ref, l_ref, acc_ref = rest
    p = pl.program_id(1)
    i, j, first, last, partial_tile = [tab_ref[ATTN_TAB_COLS * p + n] for n in range(ATTN_TAB_COLS)]
    G = NSA_GROUP
    cols = NSA_HEADS * tq
    zeros64 = jnp.zeros((HEAD_DIM, tq), F32)

    def kv_head_rows(x, kh):
        return jnp.concatenate([x, zeros64] if kh == 0 else [zeros64, x], axis=0)

    @pl.when(first == 1)
    def _():
        q = q_ref[0] * (ATTN_SCALE * LOG2_E)
        for m in range(NSA_HEADS // 2):
            q_t = q[:, m * LANES:(m + 1) * LANES].T
            for hd in (2 * m, 2 * m + 1):
                kh = hd // G
                piece = kv_head_rows(q_t[(hd % 2) * HEAD_DIM:(hd % 2 + 1) * HEAD_DIM, :], kh)
                if use_bias:
                    piece = jnp.concatenate([piece, kv_head_rows(sb_ref[0, kh], kh)], axis=0)
                qa_ref[:, hd * tq:(hd + 1) * tq] = piece.astype(BF16)
        m_ref[...] = jnp.full(m_ref.shape, NEG_INF, F32)
        l_ref[...] = jnp.zeros(l_ref.shape, F32)
        acc_ref[...] = jnp.zeros(acc_ref.shape, F32)

    k_rows = kv_ref[0, 0:KV_WIDTH, :].T
    if use_bias:
        blk = (j * tk + lax.broadcasted_iota(jnp.int32, (tk, 1), 0)) // SEL_BLOCK
        lane_blk = lax.broadcasted_iota(jnp.int32, (1, LANES), 1) & (SEL_BLOCK - 1)
        k_rows = jnp.concatenate([k_rows, (lane_blk == blk).astype(F32)], axis=1)
    sc = _dot(k_rows.astype(BF16), qa_ref[...])
    vt = kv_ref[0, KV_WIDTH:, :].astype(BF16)

    def update(sc):
        m_old = m_ref[...]
        m_new = jnp.maximum(m_old, jnp.max(sc, axis=0, keepdims=True))
        alpha = jnp.exp2(m_old - m_new)
        pr = jnp.exp2(sc - m_new)
        l_ref[...] = alpha * l_ref[...] + jnp.sum(pr, axis=0, keepdims=True)
        acc_ref[...] = alpha * acc_ref[...] + _dot(vt, pr.astype(BF16))
        m_ref[...] = m_new

    @pl.when(partial_tile == 1)
    def _():
        qpos = i * tq + (lax.broadcasted_iota(jnp.int32, (1, cols), 1) & (tq - 1))
        kpos = j * tk + lax.broadcasted_iota(jnp.int32, (tk, 1), 0)
        valid = kpos <= qpos
        if window is not None:
            valid = valid & (kpos > qpos - window)
        update(jnp.where(valid, sc, NEG_INF))

    @pl.when(partial_tile == 0)
    def _():
        update(sc)

    @pl.when(last == 1)
    def _():
        o_t = acc_ref[...] / l_ref[...]
        for m in range(NSA_HEADS // 2):
            pair = jnp.concatenate(
                [o_t[(hd // G) * HEAD_DIM:(hd // G + 1) * HEAD_DIM, hd * tq:(hd + 1) * tq]
                 for hd in (2 * m, 2 * m + 1)], axis=0)
            o_ref[0, :, m * LANES:(m + 1) * LANES] = pair.T


def _attn_prompt(q3d, kv_t, selb, *, tq, tk, window):
    batch, seq, _ = q3d.shape
    assert tq & (tq - 1) == 0 and tk % LANES == 0 and tq % LANES == 0
    use_bias = selb is not None
    assert not use_bias or selb.shape[2] == SEL_BLOCK
    tab = _attn_pairs(seq, tq, tk, window)
    depth = 2 * LANES if use_bias else LANES
    cols = NSA_HEADS * tq
    C = ATTN_TAB_COLS
    in_specs = [pl.BlockSpec((1, tq, NSA_WIDTH), lambda b, p, t: (b, t[C * p], 0)),
                pl.BlockSpec((1, 2 * KV_WIDTH, tk), lambda b, p, t: (b, 0, t[C * p + 1]))]
    args = [q3d, kv_t]
    if use_bias:
        in_specs.append(pl.BlockSpec((1, NSA_KV_HEADS, SEL_BLOCK, tq),
                                     lambda b, p, t: (b, 0, 0, t[C * p])))
        args.append(selb)
    return pl.pallas_call(
        functools.partial(_attn_body, tq=tq, tk=tk, window=window, use_bias=use_bias),
        grid_spec=pltpu.PrefetchScalarGridSpec(
            num_scalar_prefetch=1,
            grid=(batch, tab.shape[0]),
            in_specs=in_specs,
            out_specs=pl.BlockSpec((1, tq, NSA_WIDTH), lambda b, p, t: (b, t[C * p], 0)),
            scratch_shapes=[pltpu.VMEM((depth, cols), BF16), pltpu.VMEM((1, cols), F32),
                            pltpu.VMEM((1, cols), F32), pltpu.VMEM((KV_WIDTH, cols), F32)]),
        out_shape=jax.ShapeDtypeStruct((batch, seq, NSA_WIDTH), F32),
        compiler_params=_cparams(("arbitrary", "arbitrary")),
        name="attn_sel" if use_bias else "attn_win",
    )(jnp.asarray(tab.reshape(-1)), *args)


ATTN_PAGES_PER_STEP = 32


def _attn_paged_body(pt_ref, qa_ref, bq_ref, bn_ref, kn_ref, *rest, n_pages, n_new):
    page_refs = rest[:n_pages]
    o_ref, m_ref, l_ref, acc_ref = rest[n_pages:]
    c = pl.program_id(1)
    rows = qa_ref.shape[1]

    @pl.when(c == 0)
    def _():
        _softmax_init(m_ref, l_ref, acc_ref)

    keys = n_pages * PAGE_SIZE
    kt = jnp.concatenate([r[0, 0:KV_WIDTH, :] for r in page_refs], axis=1)
    vt = jnp.concatenate([r[0, KV_WIDTH:, :] for r in page_refs], axis=1).astype(BF16)
    rhs = jnp.concatenate([kt, _block_onehot_t(0, keys)], axis=0).astype(BF16)
    qa = qa_ref[0]
    lhs = jnp.concatenate([qa, bq_ref[0, 0]], axis=1).astype(BF16)
    _softmax_update(_dot(lhs, rhs), vt, m_ref, l_ref, acc_ref)

    @pl.when(c == pl.num_programs(1) - 1)
    def _():
        kn = kn_ref[0]
        sc = _dot(qa.astype(BF16), kn[0:KV_WIDTH, :].astype(BF16)) + bn_ref[0]
        tq = lax.broadcasted_iota(jnp.int32, (rows, 1), 0) % n_new
        kk = lax.broadcasted_iota(jnp.int32, (1, kn.shape[1]), 1)
        sc = jnp.where((kk <= tq) & (kk < n_new), sc, NEG_INF)
        _softmax_update(sc, kn[KV_WIDTH:, :].astype(BF16), m_ref, l_ref, acc_ref)
        o_ref[0] = acc_ref[...] / l_ref[...]


def _attn_paged(qa, bias_q, bias_new, kv_new_t, pool, page_table, *, n_new):
    batch, n_pages = page_table.shape
    pps = ATTN_PAGES_PER_STEP
    assert n_pages % pps == 0 and pps * PAGE_SIZE // SEL_BLOCK <= SEL_BLOCK
    n_steps = n_pages // pps
    rows = qa.shape[1]

    def page_spec(j):
        return pl.BlockSpec((1, 2 * KV_WIDTH, PAGE_SIZE),
                            lambda b, c, pt: (pt[(b * n_steps + c) * pps + j], 0, 0))

    per_b = lambda b, c, pt: (b, 0, 0)
    return pl.pallas_call(
        functools.partial(_attn_paged_body, n_pages=pps, n_new=n_new),
        grid_spec=pltpu.PrefetchScalarGridSpec(
            num_scalar_prefetch=1,
            grid=(batch, n_steps),
            in_specs=[pl.BlockSpec((1, rows, LANES), per_b),
                      pl.BlockSpec((1, 1, rows, LANES), lambda b, c, pt: (b, c, 0, 0)),
                      pl.BlockSpec((1, rows, LANES), per_b),
                      pl.BlockSpec((1,) + kv_new_t.shape[1:], per_b)]
            + [page_spec(j) for j in range(pps)],
            out_specs=pl.BlockSpec((1, rows, LANES), per_b),
            scratch_shapes=[pltpu.VMEM((rows, LANES), F32), pltpu.VMEM((rows, LANES), F32),
                            pltpu.VMEM((rows, LANES), F32)]),
        out_shape=jax.ShapeDtypeStruct((batch, rows, LANES), F32),
        compiler_params=_cparams(("arbitrary", "arbitrary")),
        name="attn_sel_paged",
    )(page_table.reshape(-1), qa, bias_q, bias_new, kv_new_t, *([pool] * pps))


def _attn_window_body(qa_ref, wb_ref, kn_ref, o_ref, *, n_new, past):
    qa = qa_ref[0].astype(BF16)
    wb, kn = wb_ref[0], kn_ref[0]
    rows, n_buf = qa.shape[0], wb.shape[1]
    qpos = past + lax.broadcasted_iota(jnp.int32, (rows, 1), 0) % n_new

    def masked(sc, kpos, extra):
        diff = qpos - kpos
        return jnp.where((diff >= 0) & (diff < WINDOW) & (kpos >= 0) & extra, sc, NEG_INF)

    nb = lax.broadcasted_iota(jnp.int32, (1, n_buf), 1)
    nn = lax.broadcasted_iota(jnp.int32, (1, kn.shape[1]), 1)
    sb = masked(_dot(qa, wb[0:KV_WIDTH, :].astype(BF16)), past - n_buf + nb, nb >= 0)
    sn = masked(_dot(qa, kn[0:KV_WIDTH, :].astype(BF16)), past + nn, nn < n_new)
    mx = jnp.maximum(jnp.max(sb, axis=1, keepdims=True), jnp.max(sn, axis=1, keepdims=True))
    pb, pn = jnp.exp(sb - mx), jnp.exp(sn - mx)
    o = (_dot_nt(pb.astype(BF16), wb[KV_WIDTH:, :].astype(BF16))
         + _dot_nt(pn.astype(BF16), kn[KV_WIDTH:, :].astype(BF16)))
    o_ref[0] = o / (jnp.sum(pb, axis=1, keepdims=True) + jnp.sum(pn, axis=1, keepdims=True))


def _attn_window_small(qa, win_t, kv_new_t, *, n_new, past):
    batch, rows, _ = qa.shape
    per_b = lambda b: (b, 0, 0)
    return pl.pallas_call(
        functools.partial(_attn_window_body, n_new=n_new, past=past),
        grid=(batch,),
        in_specs=[pl.BlockSpec((1, rows, LANES), per_b),
                  pl.BlockSpec((1,) + win_t.shape[1:], per_b),
                  pl.BlockSpec((1,) + kv_new_t.shape[1:], per_b)],
        out_specs=pl.BlockSpec((1, rows, LANES), per_b),
        out_shape=jax.ShapeDtypeStruct((batch, rows, LANES), F32),
        compiler_params=_cparams(("arbitrary",)),
        name="attn_win_small",
    )(qa, win_t, kv_new_t)


FFN_TM = 256
MXU_DEPTH = 256
FFN_CHUNKS = ((0, 6 * MXU_DEPTH), (6 * MXU_DEPTH, D_FF))


def _ffn_body(x_ref, om_ref, oc_ref, os_ref, ow_ref, gt_ref, ge_ref, gn_ref, gf_ref, gl_ref, wc_ref,
              fb_ref, wo_hbm, wu_hbm, wd_hbm, y_ref, fn_ref, xx_ref, wo_ref, wu_ref, wd_ref, sem_ref,
              *, tm, stride, halo):
    s = pl.program_id(1)

    @pl.when((pl.program_id(0) == 0) & (s == 0))
    def _():
        copies = [pltpu.make_async_copy(src, dst, sem_ref.at[n])
                  for n, (src, dst) in enumerate(((wo_hbm, wo_ref), (wu_hbm, wu_ref), (wd_hbm, wd_ref)))]
        for cp in copies:
            cp.start()
        for cp in copies:
            cp.wait()

    sig = _sigmoid(gt_ref[...])
    hi = sig.astype(BF16)
    lo = (sig - hi.astype(F32)).astype(BF16)
    comb = None
    for br, ob_ref in enumerate((oc_ref, os_ref, ow_ref)):
        gate = _dot(hi, ge_ref[br]) + _dot(lo, ge_ref[br])
        term = gate * ob_ref[...]
        comb = term if comb is None else comb + term
    onsa = _rms(comb, gn_ref[...])
    h = (x_ref[...] + _dot(om_ref[...].astype(BF16), wo_ref[0:MLSTM_WIDTH, :])
         + _dot(onsa.astype(BF16), wo_ref[MLSTM_WIDTH:, :]))
    hn = _rms(h, gf_ref[...]).astype(BF16)

    base = halo - (FFN_CONV - 1) * stride

    @pl.when(s == 0)
    def _():
        xx_ref[base:halo, :] = fb_ref[0]

    y_ref[...] = h
    for lo_col, hi_col in FFN_CHUNKS:
        convs = []
        for half in range(2):
            cols = slice(half * D_FF + lo_col, half * D_FF + hi_col)
            xx_ref[halo:halo + tm, cols] = _dot(hn, wu_ref[:, cols])
            conv = xx_ref[base:base + tm, cols] * wc_ref[0:1, cols]
            for j in range(1, FFN_CONV):
                conv = conv + xx_ref[base + j * stride:base + j * stride + tm, cols] * wc_ref[j:j + 1, cols]
            convs.append(conv)
        act = _silu(convs[1]) * convs[0]
        y_ref[...] += _dot(act.astype(BF16), wd_ref[lo_col:hi_col, :])
    fn_ref[0, 0] = xx_ref[tm + base:tm + halo, :]
    xx_ref[0:halo, :] = xx_ref[tm:tm + halo, :]
    y_ref[...] = _rms(y_ref[...], gl_ref[...])


def _gate_expand():
    ge = np.zeros((N_BRANCH, LANES, NSA_WIDTH), np.float32)
    for hd in range(NSA_HEADS):
        for br in range(N_BRANCH):
            ge[br, GATE_COL_NSA + hd * N_BRANCH + br, hd * HEAD_DIM:(hd + 1) * HEAD_DIM] = 1.0
    return jnp.asarray(ge, BF16)


def _ffn(x2d, om, oc, osel, ow, gt, fbuf, w_out, g_nsa, g_ffn, g_final, w_up, w_fconv, w_down,
         *, nb, tm, stride):
    rows = x2d.shape[0]
    ns = rows // (nb * tm)
    halo = -(-(FFN_CONV - 1) * stride // SUBLANES) * SUBLANES
    assert tm >= halo and all((hi - lo) % MXU_DEPTH == 0 for lo, hi in FFN_CHUNKS)
    tok = lambda b, s: (b * ns + s, 0)
    nfb = (FFN_CONV - 1) * stride

    def const(shape):
        return pl.BlockSpec(shape, lambda b, s: (0,) * len(shape))

    hbm = pl.BlockSpec(memory_space=pl.ANY)
    y, fn = pl.pallas_call(
        functools.partial(_ffn_body, tm=tm, stride=stride, halo=halo),
        grid=(nb, ns),
        in_specs=[pl.BlockSpec((tm, D_MODEL), tok)] + [pl.BlockSpec((tm, NSA_WIDTH), tok)] * 4
        + [pl.BlockSpec((tm, LANES), tok),
           const((N_BRANCH, LANES, NSA_WIDTH)), const((1, NSA_WIDTH)), const((1, D_MODEL)),
           const((1, D_MODEL)), const((FFN_CONV, 2 * D_FF)),
           pl.BlockSpec((1, nfb, 2 * D_FF), lambda b, s: (b, 0, 0)), hbm, hbm, hbm],
        out_specs=[pl.BlockSpec((tm, D_MODEL), tok),
                   pl.BlockSpec((1, 1, nfb, 2 * D_FF), lambda b, s: (b, s, 0, 0))],
        out_shape=[jax.ShapeDtypeStruct((rows, D_MODEL), F32),
                   jax.ShapeDtypeStruct((nb, ns, nfb, 2 * D_FF), F32)],
        scratch_shapes=[pltpu.VMEM((halo + tm, 2 * D_FF), F32),
                        pltpu.VMEM((D_MODEL, D_MODEL), BF16), pltpu.VMEM((D_MODEL, 2 * D_FF), BF16),
                        pltpu.VMEM((D_FF, D_MODEL), BF16), pltpu.SemaphoreType.DMA((3,))],
        compiler_params=_cparams(("arbitrary", "arbitrary")),
        name="outproj_ffn",
    )(x2d, om, oc, osel, ow, gt, _gate_expand(), g_nsa.reshape(1, -1), g_ffn.reshape(1, -1),
      g_final.reshape(1, -1), w_fconv, fbuf, w_out.astype(BF16), w_up.astype(BF16),
      w_down.astype(BF16))
    return y, fn[:, ns - 1]


PROMPT_TM = 512
PROMPT_TQ_CMP = 256
PROMPT_TQ_SEL = 128
PROMPT_TQ_WIN = 256
PROMPT_TK_SEL = 512
PROMPT_TK_WIN = 256


def _kv_rows(kv_t):
    batch, _, rows = kv_t.shape
    return kv_t.reshape(batch, 2, NSA_KV_HEADS, HEAD_DIM, rows).transpose(0, 4, 1, 2, 3)


def _kv_feature_major(kv5):
    batch, rows = kv5.shape[:2]
    return kv5.transpose(0, 2, 3, 4, 1).reshape(batch, 2 * KV_WIDTH, rows)


def _prompt_layer(x, wts):
    batch, seq, _ = x.shape
    x2d = x.reshape(batch * seq, D_MODEL)
    q, kc_rows, vc_rows, mu, mv, mo, gt, kvc_t, kvs_t, kvw_t = _in_proj(
        x2d, wts["g_mix"], wts["w_in_packed"], batch=batch, seq=seq, tm=min(PROMPT_TM, seq))
    H, DH, W = MLSTM_HEADS, MLSTM_DH, MLSTM_WIDTH
    o_m, mconv, c_new, n_new, m_new = _mlstm(
        mu, mv, mo, gt, jnp.zeros((batch, MLSTM_CONV - 1, W), F32), jnp.zeros((batch, H, DH, DH), F32),
        jnp.zeros((batch, H, DH), F32), jnp.zeros((batch, H), F32),
        wts["w_mconv"], wts["b_mconv"], wts["w_mq"], wts["w_mk"], wts["b_ig"], wts["b_fg"],
        wts["g_mhead"], wts["m_skip"], batch=batch, seq=seq)
    kce, kco = _compress_prompt(kc_rows, vc_rows, wts["cw"], batch=batch, seq=seq)
    n_sel = -(-seq // SEL_BLOCK)
    assert n_sel <= SEL_BLOCK
    o_cmp, scores_t = _cmp_attn(q, kce, kco, batch=batch, seq=seq, tq=min(PROMPT_TQ_CMP, seq), pos0=0)
    selb = _topk_blocks(scores_t.reshape(batch * NSA_KV_HEADS, -1, seq),
                        jnp.arange(seq, dtype=jnp.int32).reshape(1, seq),
                        n_sel=n_sel, nsw=SEL_BLOCK, tt=min(PROMPT_TM, seq))
    selb = selb.reshape(batch, NSA_KV_HEADS, SEL_BLOCK, seq)
    q3d = q.reshape(batch, seq, NSA_WIDTH)
    o_sel = _attn_prompt(q3d, kvs_t, selb, tq=PROMPT_TQ_SEL, tk=min(PROMPT_TK_SEL, seq), window=None)
    o_win = _attn_prompt(q3d, kvw_t, None, tq=PROMPT_TQ_WIN, tk=PROMPT_TK_WIN, window=WINDOW)
    fbuf = jnp.zeros((batch, FFN_CONV - 1, 2 * D_FF), F32)
    y, f_new = _ffn(x2d, o_m, o_cmp, o_sel.reshape(-1, NSA_WIDTH), o_win.reshape(-1, NSA_WIDTH), gt,
                    fbuf, wts["w_out"], wts["g_nsa"], wts["g_ffn"], wts["g_final"], wts["w_up"],
                    wts["w_fconv"], wts["w_down"], nb=batch, tm=min(FFN_TM, seq), stride=1)
    n_win = min(WINDOW, seq)
    return (y.reshape(batch, seq, D_MODEL), _kv_rows(kvc_t), _kv_rows(kvs_t),
            _kv_rows(kvw_t[:, :, seq - n_win:]), mconv, c_new, n_new, m_new.reshape(batch, H), f_new)


def _decode_rows(q2d, batch, seq):
    q5 = (q2d * ATTN_SCALE).reshape(batch, seq, NSA_KV_HEADS, NSA_GROUP, HEAD_DIM).transpose(0, 2, 3, 1, 4)
    eye = jnp.eye(NSA_KV_HEADS, dtype=F32)
    qa = jnp.einsum('bkgtd,kK->bkgtKd', q5, eye)
    return qa.reshape(batch, NSA_KV_HEADS * NSA_GROUP * seq, KV_WIDTH)


def _decode_rows_out(o, batch, seq):
    o6 = o.reshape(batch, NSA_KV_HEADS, NSA_GROUP, seq, NSA_KV_HEADS, HEAD_DIM)
    o5 = jnp.stack([o6[:, kh, :, :, kh, :] for kh in range(NSA_KV_HEADS)], axis=1)
    return o5.transpose(0, 3, 1, 2, 4).reshape(batch * seq, NSA_WIDTH)


def _sample_layer(x, pool_cmp, pool_sel, win_buf, m_conv, m_c, m_n, m_m, f_buf, page_table, wts):
    batch, seq, _ = x.shape
    n_pages = page_table.shape[1]
    past = n_pages * PAGE_SIZE
    assert past % SEL_BLOCK == 0 and seq <= SEL_BLOCK and seq < CMP_BLOCK
    x2d = x.reshape(batch * seq, D_MODEL)
    q, _, _, mu, mv, mo, gt, kvc_t, kvs_t, kvw_t = _in_proj(
        x2d, wts["g_mix"], wts["w_in_packed"], batch=1, seq=batch * seq, tm=batch * seq)
    per_batch = lambda a: a.reshape(2 * KV_WIDTH, batch, seq).transpose(1, 0, 2)
    kvc_t, kvs_t, kvw_t = per_batch(kvc_t), per_batch(kvs_t), per_batch(kvw_t)
    pad_keys = lambda a: jnp.pad(a, ((0, 0), (0, 0), (0, LANES - seq)))
    H = MLSTM_HEADS
    o_m, mconv, c_new, n_new, m_new = _mlstm(
        mu, mv, mo, gt, m_conv, m_c, m_n, m_m,
        wts["w_mconv"], wts["b_mconv"], wts["w_mq"], wts["w_mk"], wts["b_ig"], wts["b_fg"],
        wts["g_mhead"], wts["m_skip"], batch=batch, seq=seq)
    pool_cmp3, pool_sel3 = _kv_feature_major(pool_cmp), _kv_feature_major(pool_sel)
    kce, kco = _compress_paged(pool_cmp3, page_table, wts["cw"], wts["cw_pages"])
    n_past_blk = past // SEL_BLOCK
    n_sel = -(-(past + seq) // SEL_BLOCK)
    o_cmp, scores_t = _cmp_attn(q, kce, kco, batch=batch, seq=seq, tq=seq, pos0=past)
    ns = scores_t.shape[2]
    nsw = ns + LANES
    scores_all = scores_t.transpose(1, 2, 0, 3).reshape(NSA_KV_HEADS, ns, batch * seq)
    pos_all = (past + jnp.arange(batch * seq, dtype=jnp.int32) % seq).reshape(1, batch * seq)
    selb = _topk_blocks(scores_all, pos_all, n_sel=n_sel, nsw=nsw, tt=batch * seq)
    selb = selb.reshape(NSA_KV_HEADS, nsw, batch, seq).transpose(2, 0, 3, 1)
    qa = _decode_rows(q, batch, seq)
    rows = qa.shape[1]
    blk_per_step = ATTN_PAGES_PER_STEP * PAGE_SIZE // SEL_BLOCK
    n_steps = n_pages // ATTN_PAGES_PER_STEP
    sb_rows = jnp.broadcast_to(selb[:, :, None], (batch, NSA_KV_HEADS, NSA_GROUP, seq, selb.shape[-1]))
    sb_rows = sb_rows.reshape(batch, rows, selb.shape[-1])
    bias_q = sb_rows[:, :, :n_past_blk].reshape(batch, rows, n_steps, blk_per_step).transpose(0, 2, 1, 3)
    bias_q = jnp.pad(bias_q, ((0, 0), (0, 0), (0, 0), (0, LANES - blk_per_step)))
    bias_new = jnp.broadcast_to(sb_rows[:, :, n_past_blk:n_past_blk + 1], (batch, rows, LANES))
    o_sel = _attn_paged(qa, bias_q, bias_new, pad_keys(kvs_t), pool_sel3, page_table, n_new=seq)
    n_buf = win_buf.shape[1]
    assert past >= n_buf
    win_t = _kv_feature_major(win_buf)
    o_win = _attn_window_small(qa, win_t, pad_keys(kvw_t), n_new=seq, past=past)
    win_new = jnp.concatenate([win_t, kvw_t], axis=2)[:, :, seq:]
    tmaj = lambda a: a.reshape(batch, seq, -1).transpose(1, 0, 2).reshape(batch * seq, -1)
    fb_t = f_buf.transpose(1, 0, 2).reshape(1, (FFN_CONV - 1) * batch, 2 * D_FF)
    y, f_new = _ffn(tmaj(x2d), tmaj(o_m), tmaj(o_cmp), tmaj(_decode_rows_out(o_sel, batch, seq)),
                    tmaj(_decode_rows_out(o_win, batch, seq)), tmaj(gt), fb_t,
                    wts["w_out"], wts["g_nsa"], wts["g_ffn"], wts["g_final"], wts["w_up"],
                    wts["w_fconv"], wts["w_down"], nb=1, tm=batch * seq, stride=batch)
    y = y.reshape(seq, batch, D_MODEL).transpose(1, 0, 2)
    f_new = f_new.reshape(FFN_CONV - 1, batch, 2 * D_FF).transpose(1, 0, 2)
    return (y, _kv_rows(kvc_t), _kv_rows(kvs_t), _kv_rows(win_new), mconv, c_new, n_new,
            m_new.reshape(batch, H), f_new)


def kernel(x_prompt, x_sample, cache_cmp, cache_sel, state_win, state_mlstm_C, state_mlstm_n,
           state_mlstm_m, state_mlstm_conv, state_ffn_conv, page_table,
           g_mix, w_in, w_out, w_mconv, b_mconv, w_mq, w_mk, b_ig, b_fg, g_mhead, m_skip,
           pe_cmp, w_cmp1, w_cmp2, g_nsa, g_ffn, w_up, w_fconv, w_down, g_final):
    assert w_in.shape[0] == 1, "one layer: the final norm is fused into the layer's FFN kernel"
    l = 0
    wts = dict(g_mix=g_mix[l], w_in_packed=_pack_w_in(w_in[l]), w_out=w_out[l], w_mconv=w_mconv[l],
               b_mconv=b_mconv[l], w_mq=w_mq[l], w_mk=w_mk[l], b_ig=b_ig[l], b_fg=b_fg[l],
               g_mhead=g_mhead[l], m_skip=m_skip[l],
               cw=_pack_compress_weights(pe_cmp[l], w_cmp1[l], w_cmp2[l]),
               cw_pages=_page_pair_constants(pe_cmp[l]),
               g_nsa=g_nsa[l], g_ffn=g_ffn[l], g_final=g_final, w_up=w_up[l], w_fconv=w_fconv[l],
               w_down=w_down[l])
    p = _prompt_layer(x_prompt, wts)
    s = _sample_layer(x_sample, cache_cmp[l], cache_sel[l], state_win[l], state_mlstm_conv[l],
                      state_mlstm_C[l], state_mlstm_n[l], state_mlstm_m[l], state_ffn_conv[l],
                      page_table, wts)
    yp, cmp_p, sel_p, win_p, mconv_p, c_p, n_p, m_p, fconv_p = p
    ys, cmp_s, sel_s, win_s, mconv_s, c_s, n_s, m_s, fconv_s = s
    st = lambda a: a[None]
    return (yp, ys, st(cmp_p), st(cmp_s), st(sel_p), st(sel_s), st(win_p), st(win_s),
            st(c_p), st(c_s), st(n_p), st(n_s), st(m_p), st(m_s), st(mconv_p), st(mconv_s),
            st(fconv_p), st(fconv_s))
```

```python
import functools

import numpy as np
import jax
import jax.numpy as jnp
from jax import lax
from jax.experimental import pallas as pl
from jax.experimental.pallas import tpu as pltpu

F32 = jnp.float32
BF16 = jnp.bfloat16

D_MODEL = 1024
PAGE_SIZE = 128
HEAD_DIM = 64
NSA_HEADS = 8
NSA_KV_HEADS = 2
NSA_GROUP = NSA_HEADS // NSA_KV_HEADS
NSA_WIDTH = NSA_HEADS * HEAD_DIM
KV_WIDTH = NSA_KV_HEADS * HEAD_DIM
CMP_BLOCK = 32
CMP_HIDDEN = 2 * HEAD_DIM
SEL_BLOCK = 64
TOP_N = 16
WINDOW = 512
N_BRANCH = 3
ATTN_SCALE = HEAD_DIM ** -0.5
MLSTM_HEADS = 4
MLSTM_WIDTH = D_MODEL - NSA_WIDTH
MLSTM_DH = MLSTM_WIDTH // MLSTM_HEADS
MLSTM_CONV = 4
D_FF = ((8 * D_MODEL // 3 + 127) // 128) * 128
FFN_CONV = 3
EPS = 1e-6
NEG_INF = -1e30
SEL_PRIORITY = 1e4
LOG2_E = 1.4426950408889634

LANES = 128
SUBLANES = 8
VMEM_LIMIT = 48 * 1024 * 1024

GATE_COL_NSA = 0
GATE_COL_I = NSA_HEADS * N_BRANCH
GATE_COL_F = GATE_COL_I + MLSTM_HEADS

MLSTM_CHUNK = 128
MLSTM_SEQS_PER_STEP = 4


def _cparams(sem):
    return pltpu.CompilerParams(dimension_semantics=sem, vmem_limit_bytes=VMEM_LIMIT)


def _dot(a, b):
    return jnp.dot(a, b, preferred_element_type=F32)


def _dot_nt(a, b):
    return lax.dot_general(a, b, (((1,), (1,)), ((), ())), preferred_element_type=F32)


def _sigmoid(x):
    return 1.0 / (1.0 + jnp.exp(-x))


def _silu(x):
    return x * _sigmoid(x)


def _rms(x, g):
    return x * lax.rsqrt(jnp.mean(x * x, axis=-1, keepdims=True) + EPS) * g


IN_ROW_WIDTHS = (NSA_WIDTH, KV_WIDTH, KV_WIDTH, MLSTM_WIDTH, MLSTM_WIDTH, MLSTM_WIDTH, LANES,
                 KV_WIDTH, KV_WIDTH)
IN_ROW_DTYPES = (F32,) * 7 + (BF16,) * 2
N_KV_BRANCH = 3


def _inproj_body(x_ref, g_ref, w_ref, wt_ref, *out_refs):
    xb = _rms(x_ref[...], g_ref[...]).astype(BF16)
    off = 0
    n_rows = len(IN_ROW_WIDTHS)
    for ref in out_refs[:n_rows]:
        n = ref.shape[-1]
        ref[...] = _dot(xb, w_ref[:, off:off + n]).astype(ref.dtype)
        off += n
    kv_refs = out_refs[n_rows:n_rows + N_KV_BRANCH]
    vt_refs = out_refs[n_rows + N_KV_BRANCH:]
    for n, ref in enumerate(kv_refs):
        kv_t = _dot_nt(wt_ref[n * 2 * KV_WIDTH:(n + 1) * 2 * KV_WIDTH, :], xb)
        ref[0] = kv_t
        if n > 0:
            vt_refs[n - 1][0] = kv_t[KV_WIDTH:, :].astype(BF16)


def _pack_w_in(w_in):
    splits = np.cumsum([NSA_WIDTH, 2 * KV_WIDTH, 2 * KV_WIDTH, 2 * KV_WIDTH, NSA_HEADS * N_BRANCH,
                        MLSTM_WIDTH, MLSTM_WIDTH, MLSTM_WIDTH, MLSTM_HEADS]).tolist()
    q, kvc, kvs, kvw, gt, mu, mv, mo, mi, mf = jnp.split(w_in, splits, axis=1)
    gates = jnp.concatenate([gt, mi, mf], axis=1)
    gates = jnp.pad(gates, ((0, 0), (0, LANES - gates.shape[1])))
    w_rows = jnp.concatenate([q, kvc, mu, mv, mo, gates, kvs[:, :KV_WIDTH], kvw[:, :KV_WIDTH]],
                             axis=1).astype(BF16)
    w_kv_t = jnp.concatenate([kvc, kvs, kvw], axis=1).T.astype(BF16)
    return w_rows, w_kv_t


def _in_proj(x2d, g_mix, w_packed, *, batch, seq, tm):
    w_rows, w_kv_t = w_packed
    t = x2d.shape[0]
    ns = seq // tm
    kv_sd = jax.ShapeDtypeStruct((batch, 2 * KV_WIDTH, seq), F32)
    vt_sd = jax.ShapeDtypeStruct((batch, KV_WIDTH, seq), BF16)
    feat_major = lambda rows: pl.BlockSpec((1, rows, tm), lambda i: (i // ns, 0, i % ns))
    return pl.pallas_call(
        _inproj_body,
        grid=(t // tm,),
        in_specs=[pl.BlockSpec((tm, D_MODEL), lambda i: (i, 0)),
                  pl.BlockSpec((1, D_MODEL), lambda i: (0, 0)),
                  pl.BlockSpec(w_rows.shape, lambda i: (0, 0)),
                  pl.BlockSpec(w_kv_t.shape, lambda i: (0, 0))],
        out_specs=[pl.BlockSpec((tm, n), lambda i: (i, 0)) for n in IN_ROW_WIDTHS]
        + [feat_major(2 * KV_WIDTH)] * N_KV_BRANCH + [feat_major(KV_WIDTH)] * (N_KV_BRANCH - 1),
        out_shape=[jax.ShapeDtypeStruct((t, n), dt) for n, dt in zip(IN_ROW_WIDTHS, IN_ROW_DTYPES)]
        + [kv_sd] * N_KV_BRANCH + [vt_sd] * (N_KV_BRANCH - 1),
        compiler_params=_cparams(("arbitrary",)),
        name="in_proj",
    )(x2d, g_mix.reshape(1, D_MODEL), w_rows, w_kv_t)


def _mlstm_body(*refs, valid, bb):
    cb_ref, c0_ref, n0_ref, m0_ref = refs[4:8]
    cn_ref, c_ref, n_ref, m_ref, xx_ref = refs[16:21]
    halo = SUBLANES

    @pl.when(pl.program_id(1) == 0)
    def _():
        xx_ref[:, 0:halo, :] = jnp.zeros((bb, halo, MLSTM_WIDTH), F32)
        xx_ref[:, halo - (MLSTM_CONV - 1):halo, :] = cb_ref[...]
        c_ref[...] = c0_ref[...]
        n_ref[...] = n0_ref[...]
        m_ref[...] = m0_ref[...]

    _mlstm_chunk(*refs, valid=valid, bb=bb)


def _mlstm_chunk(mu_ref, mv_ref, mo_ref, g_ref, cb_ref, c0_ref, n0_ref, m0_ref,
                 wc_ref, bc_ref, wq_ref, wk_ref, gb_ref, gh_ref, sk_ref,
                 o_ref, cn_ref, c_ref, n_ref, m_ref,
                 xx_ref, vpad_ref, gpad_ref, *, valid, bb):
    L = MLSTM_CHUNK
    DH = MLSTM_DH
    halo = SUBLANES
    units = [(bi, h) for bi in range(bb) for h in range(MLSTM_HEADS)]
    head_lanes = lambda h: slice(h * DH, (h + 1) * DH)
    row = lax.broadcasted_iota(jnp.int32, (L, L), 0)
    col = lax.broadcasted_iota(jnp.int32, (L, L), 1)
    tril = row >= col
    triu = row <= col
    tok_col = lax.broadcasted_iota(jnp.int32, (L, 1), 0)
    tok_row = lax.broadcasted_iota(jnp.int32, (1, L), 1)

    def log_sigmoid(x):
        return jnp.minimum(x, 0.0) - jnp.log(1.0 + jnp.exp(-jnp.abs(x)))

    uc, gb, gbt = {}, {}, {}
    for bi in range(bb):
        if valid < L:
            xx_ref[bi, halo:, :] = jnp.zeros((L, MLSTM_WIDTH), F32)
            vpad_ref[bi] = jnp.zeros((L, MLSTM_WIDTH), F32)
            gpad_ref[bi] = jnp.zeros((L, LANES), F32)
        xx_ref[bi, halo:halo + valid, :] = mu_ref[bi]
        vpad_ref[bi, 0:valid, :] = mv_ref[bi]
        gpad_ref[bi, 0:valid, :] = g_ref[bi]
        conv = xx_ref[bi, halo - 3:halo - 3 + L, :] * wc_ref[0:1, :]
        for j in range(1, MLSTM_CONV):
            conv = conv + xx_ref[bi, halo - 3 + j:halo - 3 + j + L, :] * wc_ref[j:j + 1, :]
        uc[bi] = _silu(conv + bc_ref[...])
        tail = xx_ref[bi, valid + halo - 3:valid + halo, :]
        xx_ref[bi, halo - 3:halo, :] = tail
        cn_ref[bi] = tail
        gb[bi] = gpad_ref[bi] + gb_ref[...]
        gbt[bi] = gb[bi].T

    q, k, qb, kb = {}, {}, {}, {}
    for u in units:
        bi, h = u
        ub = uc[bi][:, head_lanes(h)].astype(BF16)
        q[u] = _dot(ub, wq_ref[h])
        k[u] = _dot(ub, wk_ref[h]) * (DH ** -0.5)
        qb[u], kb[u] = q[u].astype(BF16), k[u].astype(BF16)

    ic_col, ic_row, cum_col, cum_row = {}, {}, {}, {}
    for u in units:
        bi, h = u
        ic_c = gb[bi][:, GATE_COL_I + h:GATE_COL_I + h + 1]
        ic_r = gbt[bi][GATE_COL_I + h:GATE_COL_I + h + 1, :]
        lf_c = log_sigmoid(gb[bi][:, GATE_COL_F + h:GATE_COL_F + h + 1])
        lf_r = log_sigmoid(gbt[bi][GATE_COL_F + h:GATE_COL_F + h + 1, :])
        if valid < L:
            ic_c = jnp.where(tok_col < valid, ic_c, NEG_INF)
            ic_r = jnp.where(tok_row < valid, ic_r, NEG_INF)
            lf_c = jnp.where(tok_col < valid, lf_c, 0.0)
            lf_r = jnp.where(tok_row < valid, lf_r, 0.0)
        ic_col[u], ic_row[u] = ic_c, ic_r
        cum_col[u] = jnp.sum(jnp.where(tril, lf_r, 0.0), axis=1, keepdims=True)
        cum_row[u] = jnp.sum(jnp.where(triu, lf_c, 0.0), axis=0, keepdims=True)

    m_t, w, sc = {}, {}, {}
    for u in units:
        bi, h = u
        m0 = m_ref[bi, 0:1, h:h + 1]
        dmat = jnp.where(tril, cum_col[u] - cum_row[u] + ic_row[u], NEG_INF)
        inter = cum_col[u] + m0
        m_t[u] = jnp.maximum(inter, jnp.max(dmat, axis=1, keepdims=True))
        w[u] = jnp.exp(dmat - m_t[u])
        sc[u] = jnp.exp(inter - m_t[u])

    hc = {}
    for u in units:
        bi, h = u
        s = _dot_nt(qb[u], kb[u]) * w[u]
        v = vpad_ref[bi, :, head_lanes(h)]
        c_old = c_ref[bi, h]
        n_old = n_ref[bi, h:h + 1, :]
        num = _dot(s.astype(BF16), v.astype(BF16)) + sc[u] * _dot_nt(qb[u], c_old.astype(BF16))
        den = (jnp.sum(s, axis=1, keepdims=True)
               + sc[u] * jnp.sum(q[u] * n_old, axis=1, keepdims=True))
        hc[u] = num / jnp.maximum(jnp.abs(den), jnp.exp(-m_t[u]))

    for u in units:
        bi, h = u
        m0 = m_ref[bi, 0:1, h:h + 1]
        m_new = m_t[u][L - 1:L, :]
        cum_last = cum_col[u][L - 1:L, :]
        wl = jnp.exp(cum_last - cum_col[u] + ic_col[u] - m_new)
        sl = jnp.exp(cum_last + m0 - m_new)
        v = vpad_ref[bi, :, head_lanes(h)]
        vw_t = (v * wl).T.astype(BF16)
        c_ref[bi, h] = sl * c_ref[bi, h] + _dot(vw_t, kb[u])
        n_ref[bi, h:h + 1, :] = sl * n_ref[bi, h:h + 1, :] + jnp.sum(wl * k[u], axis=0, keepdims=True)
        m_ref[bi, 0:1, h:h + 1] = m_new

    for u in units:
        bi, h = u
        hn = _rms(hc[u], gh_ref[:, head_lanes(h)])
        u_h = uc[bi][:, head_lanes(h)]
        out = ((hn[0:valid, :] + sk_ref[:, head_lanes(h)] * u_h[0:valid, :])
               * _sigmoid(mo_ref[bi, :, head_lanes(h)]))
        o_ref[bi, :, head_lanes(h)] = out


def _mlstm(mu, mv, mo, gates, conv_buf, c0, n0, m0, w_mconv, b_mconv, w_mq, w_mk, b_ig, b_fg,
           g_mhead, m_skip, *, batch, seq):
    L = MLSTM_CHUNK
    valid = min(seq, L)
    assert seq % valid == 0 and (valid == L or seq == valid)
    nc = seq // valid
    gate_bias = jnp.zeros((1, LANES), F32)
    gate_bias = gate_bias.at[0, GATE_COL_I:GATE_COL_I + MLSTM_HEADS].set(b_ig)
    gate_bias = gate_bias.at[0, GATE_COL_F:GATE_COL_F + MLSTM_HEADS].set(b_fg)
    bb = MLSTM_SEQS_PER_STEP
    assert batch % bb == 0
    tok = lambda b, c: (b, c, 0)
    const2 = lambda b, c: (0, 0)
    const3 = lambda b, c: (0, 0, 0)
    per_b3 = lambda b, c: (b, 0, 0)
    per_b4 = lambda b, c: (b, 0, 0, 0)
    H, DH, W = MLSTM_HEADS, MLSTM_DH, MLSTM_WIDTH
    rows3 = lambda a: a.reshape(batch, seq, a.shape[-1])
    o_m, conv_new, c_new, n_new, m_new = pl.pallas_call(
        functools.partial(_mlstm_body, valid=valid, bb=bb),
        grid=(batch // bb, nc),
        in_specs=[pl.BlockSpec((bb, valid, W), tok), pl.BlockSpec((bb, valid, W), tok),
                  pl.BlockSpec((bb, valid, W), tok), pl.BlockSpec((bb, valid, LANES), tok),
                  pl.BlockSpec((bb, MLSTM_CONV - 1, W), per_b3),
                  pl.BlockSpec((bb, H, DH, DH), per_b4),
                  pl.BlockSpec((bb, H, DH), per_b3),
                  pl.BlockSpec((bb, 1, H), per_b3),
                  pl.BlockSpec((MLSTM_CONV, W), const2), pl.BlockSpec((1, W), const2),
                  pl.BlockSpec((H, DH, DH), const3), pl.BlockSpec((H, DH, DH), const3),
                  pl.BlockSpec((1, LANES), const2), pl.BlockSpec((1, W), const2),
                  pl.BlockSpec((1, W), const2)],
        out_specs=[pl.BlockSpec((bb, valid, W), tok),
                   pl.BlockSpec((bb, MLSTM_CONV - 1, W), per_b3),
                   pl.BlockSpec((bb, H, DH, DH), per_b4),
                   pl.BlockSpec((bb, H, DH), per_b3),
                   pl.BlockSpec((bb, 1, H), per_b3)],
        out_shape=[jax.ShapeDtypeStruct((batch, seq, W), F32),
                   jax.ShapeDtypeStruct((batch, MLSTM_CONV - 1, W), F32),
                   jax.ShapeDtypeStruct((batch, H, DH, DH), F32),
                   jax.ShapeDtypeStruct((batch, H, DH), F32),
                   jax.ShapeDtypeStruct((batch, 1, H), F32)],
        scratch_shapes=[pltpu.VMEM((bb, SUBLANES + L, W), F32), pltpu.VMEM((bb, L, W), F32),
                        pltpu.VMEM((bb, L, LANES), F32)],
        compiler_params=_cparams(("arbitrary", "arbitrary")),
        name="mlstm",
    )(rows3(mu), rows3(mv), rows3(mo), rows3(gates), conv_buf, c0, n0, m0.reshape(batch, 1, H),
      w_mconv, b_mconv.reshape(1, W), w_mq.astype(BF16), w_mk.astype(BF16), gate_bias,
      g_mhead.reshape(1, W), m_skip.reshape(1, W))
    return o_m.reshape(batch * seq, W), conv_new, c_new, n_new, m_new


def _compress_rows(xk_ref, xv_ref, pe_ref, w1_ref, w2_ref, n_pairs):
    pair_rows = 2 * CMP_BLOCK
    outs = []
    for kv, x_ref in enumerate((xk_ref, xv_ref)):
        acc = jnp.zeros((2 * n_pairs, NSA_KV_HEADS * CMP_HIDDEN), F32)
        for r in range(CMP_BLOCK):
            ev = x_ref[pl.ds(r, n_pairs, stride=pair_rows), :]
            od = x_ref[pl.ds(CMP_BLOCK + r, n_pairs, stride=pair_rows), :]
            xr = jnp.concatenate([ev, od], axis=0) + pe_ref[kv, r:r + 1, :]
            acc = acc + _dot(xr.astype(BF16), w1_ref[kv, r])
        outs.append(_dot(_silu(acc).astype(BF16), w2_ref[kv]))
    return jnp.concatenate(outs, axis=1)


def _compress_body(xk_ref, xv_ref, pe_ref, w1_ref, w2_ref, oe_ref, oo_ref, *, n_pairs):
    out = _compress_rows(xk_ref, xv_ref, pe_ref, w1_ref, w2_ref, n_pairs)
    oe_ref[0] = out[0:n_pairs, :]
    oo_ref[0] = out[n_pairs:, :]


BLOCKS_PER_PAGE = PAGE_SIZE // CMP_BLOCK


def _compress_paged_body(pt_ref, *refs, n_pages):
    page_refs = refs[:n_pages]
    pet_ref, perm_ref, w1_ref, w2_ref, oe_ref, oo_ref, buf_ref, os_ref = refs[n_pages:]
    grp = 2 * BLOCKS_PER_PAGE
    for jp in range(n_pages // 2):
        xt = jnp.concatenate([page_refs[2 * jp][0], page_refs[2 * jp + 1][0]], axis=1)
        xb = (xt + pet_ref[...]).astype(BF16)
        xp = _dot_nt(perm_ref[...], xb)
        for r in range(CMP_BLOCK):
            for kv in range(2):
                lane0 = (2 * kv + r % 2) * KV_WIDTH
                buf_ref[r // 2, grp * jp:grp * (jp + 1), lane0:lane0 + KV_WIDTH] = (
                    xp[grp * r:grp * (r + 1), kv * KV_WIDTH:(kv + 1) * KV_WIDTH])
    for kv in range(2):
        lanes = slice(2 * kv * KV_WIDTH, 2 * (kv + 1) * KV_WIDTH)
        acc = _dot(buf_ref[0, :, lanes].astype(BF16), w1_ref[kv, 0])
        for r2 in range(1, CMP_BLOCK // 2):
            acc = acc + _dot(buf_ref[r2, :, lanes].astype(BF16), w1_ref[kv, r2])
        os_ref[kv] = _dot(_silu(acc).astype(BF16), w2_ref[kv])
    half = os_ref.shape[1] // 2
    for parity, ref in enumerate((oe_ref, oo_ref)):
        ref[0] = jnp.concatenate([os_ref[kv, pl.ds(parity, half, stride=2), :] for kv in range(2)],
                                 axis=1)


def _page_pair_constants(pe):
    pe_t = jnp.broadcast_to(pe.transpose(0, 2, 1)[:, None, :, None, :],
                            (2, NSA_KV_HEADS, HEAD_DIM, 2 * BLOCKS_PER_PAGE, CMP_BLOCK))
    pe_t = pe_t.reshape(2 * KV_WIDTH, 2 * PAGE_SIZE)
    grp = 2 * BLOCKS_PER_PAGE
    perm = np.zeros((2 * PAGE_SIZE, 2 * PAGE_SIZE), np.float32)
    for r in range(CMP_BLOCK):
        for b in range(grp):
            perm[r * grp + b, b * CMP_BLOCK + r] = 1.0
    return pe_t, jnp.asarray(perm, BF16)


def _pack_compress_weights(pe, w1, w2):
    eye_h = jnp.eye(NSA_KV_HEADS, dtype=F32)
    pe_r = jnp.broadcast_to(pe[:, :, None, :], (2, CMP_BLOCK, NSA_KV_HEADS, HEAD_DIM))
    pe_r = pe_r.reshape(2, CMP_BLOCK, KV_WIDTH)
    w1r = w1.reshape(2, CMP_BLOCK, HEAD_DIM, CMP_HIDDEN)
    w1_big = jnp.einsum('krdc,hH->krhdHc', w1r, eye_h)
    w1_big = w1_big.reshape(2, CMP_BLOCK, KV_WIDTH, NSA_KV_HEADS * CMP_HIDDEN).astype(BF16)
    w2_big = jnp.einsum('kcd,hH->khcHd', w2, eye_h)
    w2_big = w2_big.reshape(2, NSA_KV_HEADS * CMP_HIDDEN, KV_WIDTH).astype(BF16)
    return pe_r, w1_big, w2_big


def _compress_prompt(k_rows, v_rows, cw, *, batch, seq):
    pe_r, w1_big, w2_big = cw
    n_pairs = seq // (2 * CMP_BLOCK)
    const3 = lambda b: (0, 0, 0)
    out_sd = jax.ShapeDtypeStruct((batch, n_pairs, 2 * KV_WIDTH), F32)
    return pl.pallas_call(
        functools.partial(_compress_body, n_pairs=n_pairs),
        grid=(batch,),
        in_specs=[pl.BlockSpec((seq, KV_WIDTH), lambda b: (b, 0)),
                  pl.BlockSpec((seq, KV_WIDTH), lambda b: (b, 0)),
                  pl.BlockSpec(pe_r.shape, const3),
                  pl.BlockSpec(w1_big.shape, lambda b: (0, 0, 0, 0)),
                  pl.BlockSpec(w2_big.shape, const3)],
        out_specs=[pl.BlockSpec((1, n_pairs, 2 * KV_WIDTH), lambda b: (b, 0, 0))] * 2,
        out_shape=[out_sd, out_sd],
        compiler_params=_cparams(("arbitrary",)),
        name="compress_prompt",
    )(k_rows, v_rows, pe_r, w1_big, w2_big)


COMPRESS_PAGES_PER_STEP = 64


def _compress_paged(pool, page_table, cw, cw_pages):
    _, w1_big, w2_big = cw
    w1_big = w1_big.reshape(2, CMP_BLOCK // 2, 2 * KV_WIDTH, NSA_KV_HEADS * CMP_HIDDEN)
    pe_t, perm = cw_pages
    batch, n_pages = page_table.shape
    pps = COMPRESS_PAGES_PER_STEP
    assert n_pages % pps == 0 and pps % 2 == 0
    n_steps = n_pages // pps
    n_blk = pps * BLOCKS_PER_PAGE
    const3 = lambda b, c, pt: (0, 0, 0)

    def page_spec(j):
        return pl.BlockSpec((1, 2 * KV_WIDTH, PAGE_SIZE),
                            lambda b, c, pt: (pt[(b * n_steps + c) * pps + j], 0, 0))

    return pl.pallas_call(
        functools.partial(_compress_paged_body, n_pages=pps),
        grid_spec=pltpu.PrefetchScalarGridSpec(
            num_scalar_prefetch=1,
            grid=(batch, n_steps),
            in_specs=[page_spec(j) for j in range(pps)] + [
                pl.BlockSpec(pe_t.shape, lambda b, c, pt: (0, 0)),
                pl.BlockSpec(perm.shape, lambda b, c, pt: (0, 0)),
                pl.BlockSpec(w1_big.shape, lambda b, c, pt: (0, 0, 0, 0)),
                pl.BlockSpec(w2_big.shape, const3)],
            out_specs=[pl.BlockSpec((1, n_blk // 2, 2 * KV_WIDTH), lambda b, c, pt: (b, c, 0))] * 2,
            scratch_shapes=[pltpu.VMEM((CMP_BLOCK // 2, n_blk, 4 * KV_WIDTH), F32),
                            pltpu.VMEM((2, n_blk, KV_WIDTH), F32)]),
        out_shape=[jax.ShapeDtypeStruct((batch, n_steps * n_blk // 2, 2 * KV_WIDTH), F32)] * 2,
        compiler_params=_cparams(("arbitrary", "arbitrary")),
        name="compress_paged",
    )(page_table.reshape(-1), *([pool] * pps), pe_t, perm, w1_big, w2_big)


def _cmp_attn_body(q_ref, ke_ref, ko_ref, o_ref, st_ref, *, tq, pos0):
    ns = ke_ref.shape[1]
    i = pl.program_id(1)
    rows = NSA_GROUP * tq
    tok0 = pos0 + i * tq
    pos_c = tok0 + lax.broadcasted_iota(jnp.int32, (rows, 1), 0) % tq
    pos_r = tok0 + lax.broadcasted_iota(jnp.int32, (1, rows), 1) % tq
    pair_r = lax.broadcasted_iota(jnp.int32, (1, ns), 1)
    pair_c = lax.broadcasted_iota(jnp.int32, (ns, 1), 0)
    end_e = lambda pair: (2 * pair + 1) * CMP_BLOCK - 1
    end_o = lambda pair: (2 * pair + 2) * CMP_BLOCK - 1
    any_c = (CMP_BLOCK - 1 <= pos_c).astype(F32)
    any_r = (CMP_BLOCK - 1 <= pos_r).astype(F32)
    q = q_ref[...] * ATTN_SCALE
    for kh in range(NSA_KV_HEADS):
        qs = jnp.concatenate([q[:, (kh * NSA_GROUP + g) * HEAD_DIM:(kh * NSA_GROUP + g + 1) * HEAD_DIM]
                              for g in range(NSA_GROUP)], axis=0).astype(BF16)
        ks, vs = slice(kh * HEAD_DIM, (kh + 1) * HEAD_DIM), slice(KV_WIDTH + kh * HEAD_DIM,
                                                                   KV_WIDTH + (kh + 1) * HEAD_DIM)
        ke, ko = ke_ref[0, :, ks].astype(BF16), ko_ref[0, :, ks].astype(BF16)
        se = jnp.where(end_e(pair_r) <= pos_c, _dot_nt(qs, ke), NEG_INF)
        so = jnp.where(end_o(pair_r) <= pos_c, _dot_nt(qs, ko), NEG_INF)
        mx = jnp.maximum(jnp.max(se, axis=1, keepdims=True), jnp.max(so, axis=1, keepdims=True))
        pe, po = jnp.exp(se - mx), jnp.exp(so - mx)
        inv = any_c / (jnp.sum(pe, axis=1, keepdims=True) + jnp.sum(po, axis=1, keepdims=True))
        oh = (_dot((pe * inv).astype(BF16), ke_ref[0, :, vs].astype(BF16))
              + _dot((po * inv).astype(BF16), ko_ref[0, :, vs].astype(BF16)))
        for g in range(NSA_GROUP):
            hd = kh * NSA_GROUP + g
            o_ref[:, hd * HEAD_DIM:(hd + 1) * HEAD_DIM] = oh[g * tq:(g + 1) * tq, :]
        te = jnp.where(end_e(pair_c) <= pos_r, _dot_nt(ke, qs), NEG_INF)
        to = jnp.where(end_o(pair_c) <= pos_r, _dot_nt(ko, qs), NEG_INF)
        mt = jnp.maximum(jnp.max(te, axis=0, keepdims=True), jnp.max(to, axis=0, keepdims=True))
        pte, pto = jnp.exp(te - mt), jnp.exp(to - mt)
        invt = any_r / (jnp.sum(pte, axis=0, keepdims=True) + jnp.sum(pto, axis=0, keepdims=True))
        ps = (pte + pto) * invt
        score = ps[:, 0:tq]
        for g in range(1, NSA_GROUP):
            score = score + ps[:, g * tq:(g + 1) * tq]
        st_ref[0, kh] = score


def _cmp_attn(q2d, kce, kco, *, batch, seq, tq, pos0):
    ns = kce.shape[1]
    nq = seq // tq
    return pl.pallas_call(
        functools.partial(_cmp_attn_body, tq=tq, pos0=pos0),
        grid=(batch, nq),
        in_specs=[pl.BlockSpec((tq, NSA_WIDTH), lambda b, i: (b * nq + i, 0)),
                  pl.BlockSpec((1, ns, 2 * KV_WIDTH), lambda b, i: (b, 0, 0)),
                  pl.BlockSpec((1, ns, 2 * KV_WIDTH), lambda b, i: (b, 0, 0))],
        out_specs=[pl.BlockSpec((tq, NSA_WIDTH), lambda b, i: (b * nq + i, 0)),
                   pl.BlockSpec((1, NSA_KV_HEADS, ns, tq), lambda b, i: (b, 0, 0, i))],
        out_shape=[jax.ShapeDtypeStruct((batch * seq, NSA_WIDTH), F32),
                   jax.ShapeDtypeStruct((batch, NSA_KV_HEADS, ns, seq), F32)],
        compiler_params=_cparams(("arbitrary", "arbitrary")),
        name="cmp_attn",
    )(q2d, kce, kco)


def _topk_body(pos_ref, st_ref, b_ref, *, n_sel):
    score = st_ref[0]
    ns, tt = score.shape
    nsw = b_ref.shape[1]
    if nsw > ns:
        score = jnp.concatenate([score, jnp.zeros((nsw - ns, tt), F32)], axis=0)
    blk = lax.broadcasted_iota(jnp.int32, (nsw, 1), 0)
    blk_f = blk.astype(F32)
    cur = pos_ref[...] // SEL_BLOCK
    forced = (blk == 0) | (blk == cur) | (blk == cur - 1)
    pri = jnp.where(blk <= cur, jnp.where(forced, SEL_PRIORITY, score), -SEL_PRIORITY)
    pri = jnp.where(blk < n_sel, pri, -jnp.inf)
    bias = jnp.full((nsw, tt), NEG_INF, F32)
    for _ in range(min(TOP_N, n_sel)):
        top = jnp.max(pri, axis=0, keepdims=True)
        first = jnp.min(jnp.where(pri == top, blk_f, float(nsw)), axis=0, keepdims=True)
        hit = blk_f == first
        bias = jnp.where(hit, 0.0, bias)
        pri = jnp.where(hit, -jnp.inf, pri)
    b_ref[0] = bias


def _topk_blocks(scores_t, pos, *, n_sel, nsw, tt):
    groups, ns, tokens = scores_t.shape
    assert nsw >= max(ns, n_sel) and tokens % tt == 0
    return pl.pallas_call(
        functools.partial(_topk_body, n_sel=n_sel),
        grid=(groups, tokens // tt),
        in_specs=[pl.BlockSpec((1, tt), lambda g, i: (0, i)),
                  pl.BlockSpec((1, ns, tt), lambda g, i: (g, 0, i))],
        out_specs=pl.BlockSpec((1, nsw, tt), lambda g, i: (g, 0, i)),
        out_shape=jax.ShapeDtypeStruct((groups, nsw, tokens), F32),
        compiler_params=_cparams(("arbitrary", "arbitrary")),
        name="topk_blocks",
    )(pos, scores_t)


def _softmax_update(sc, vt_bf16, m_ref, l_ref, acc_ref):
    m_old = m_ref[...]
    m_new = jnp.maximum(m_old, jnp.max(sc, axis=1, keepdims=True))
    alpha = jnp.exp(m_old - m_new)
    pr = jnp.exp(sc - jnp.concatenate([m_new] * (sc.shape[1] // LANES), axis=1))
    l_ref[...] = alpha * l_ref[...] + jnp.sum(pr, axis=1, keepdims=True)
    acc_ref[...] = alpha * acc_ref[...] + _dot_nt(pr.astype(BF16), vt_bf16)
    m_ref[...] = m_new


def _softmax_init(m_ref, l_ref, acc_ref):
    m_ref[...] = jnp.full(m_ref.shape, NEG_INF, F32)
    l_ref[...] = jnp.zeros(l_ref.shape, F32)
    acc_ref[...] = jnp.zeros(acc_ref.shape, F32)


def _block_onehot_t(first_key, n_keys):
    blk = (first_key + lax.broadcasted_iota(jnp.int32, (1, n_keys), 1)) // SEL_BLOCK
    r = lax.broadcasted_iota(jnp.int32, (LANES, 1), 0) & (SEL_BLOCK - 1)
    return (r == blk).astype(F32)


ATTN_TAB_COLS = 5
ATTN_COL_BLOCK = 256


def _attn_pairs(seq, tq, tk, window):
    rows = []
    for i in range(seq // tq):
        t_lo, t_hi = i * tq, i * tq + tq - 1
        k_lo = 0 if window is None else max(0, t_lo - window + 1)
        js = list(range(k_lo // tk, t_hi // tk + 1))
        for n, j in enumerate(js):
            partial_tile = j * tk + tk - 1 > t_lo or (window is not None and j * tk <= t_hi - window)
            rows.append((i, j, int(n == 0), int(n == len(js) - 1), int(partial_tile)))
    return np.asarray(rows, np.int32)


def _attn_body(tab_ref, q_ref, k_ref, vt_ref, *rest, tq, tk, window, use_bias):
    if use_bias:
        oh_ref, sb_ref, o_ref, qa_ref, m_ref, l_ref, acc_ref = rest
    else:
        o_ref, qa_ref, m_ref, l_ref, acc_ref = rest
    p = pl.program_id(1)
    i, j, first, last, partial_tile = [tab_ref[ATTN_TAB_COLS * p + n] for n in range(ATTN_TAB_COLS)]
    G = NSA_GROUP
    cols = NSA_HEADS * tq
    zeros64 = jnp.zeros((HEAD_DIM, tq), F32)

    def kv_head_rows(x, kh):
        return jnp.concatenate([x, zeros64] if kh == 0 else [zeros64, x], axis=0)

    @pl.when(first == 1)
    def _():
        q = q_ref[0] * (ATTN_SCALE * LOG2_E)
        for m in range(NSA_HEADS // 2):
            q_t = q[:, m * LANES:(m + 1) * LANES].T
            for hd in (2 * m, 2 * m + 1):
                kh = hd // G
                piece = kv_head_rows(q_t[(hd % 2) * HEAD_DIM:(hd % 2 + 1) * HEAD_DIM, :], kh)
                if use_bias:
                    piece = jnp.concatenate([piece, kv_head_rows(sb_ref[0, kh], kh)], axis=0)
                qa_ref[:, hd * tq:(hd + 1) * tq] = piece.astype(BF16)
        m_ref[...] = jnp.full(m_ref.shape, NEG_INF, F32)
        l_ref[...] = jnp.zeros(l_ref.shape, F32)
        acc_ref[...] = jnp.zeros(acc_ref.shape, F32)

    k_aug = k_ref[...]
    if use_bias:
        k_aug = jnp.concatenate([k_aug, oh_ref[...]], axis=1)
    sc = _dot(k_aug, qa_ref[...])
    vt = vt_ref[0]

    def update(sc):
        m_old = m_ref[...]
        m_new = jnp.maximum(m_old, jnp.max(sc, axis=0, keepdims=True))
        alpha = jnp.exp2(m_old - m_new)
        pr = jnp.exp2(sc - m_new)
        l_ref[...] = alpha * l_ref[...] + jnp.sum(pr, axis=0, keepdims=True)
        acc_ref[...] = alpha * acc_ref[...] + _dot(vt, pr.astype(BF16))
        m_ref[...] = m_new

    @pl.when(partial_tile == 1)
    def _():
        qpos = i * tq + (lax.broadcasted_iota(jnp.int32, (1, cols), 1) & (tq - 1))
        kpos = j * tk + lax.broadcasted_iota(jnp.int32, (tk, 1), 0)
        valid = kpos <= qpos
        if window is not None:
            valid = valid & (kpos > qpos - window)
        update(jnp.where(valid, sc, NEG_INF))

    @pl.when(partial_tile == 0)
    def _():
        update(sc)

    @pl.when(last == 1)
    def _():
        o_t = acc_ref[...] / l_ref[...]
        for m in range(NSA_HEADS // 2):
            pair = jnp.concatenate(
                [o_t[(hd // G) * HEAD_DIM:(hd // G + 1) * HEAD_DIM, hd * tq:(hd + 1) * tq]
                 for hd in (2 * m, 2 * m + 1)], axis=0)
            o_ref[0, :, m * LANES:(m + 1) * LANES] = pair.T


def _block_onehot(seq):
    blk = np.arange(seq)[:, None] // SEL_BLOCK
    return jnp.asarray((np.arange(LANES)[None, :] % SEL_BLOCK) == blk, BF16)


def _attn_prompt(q3d, k_rows, v_t, selb, *, tq, tk, window):
    batch, seq, _ = q3d.shape
    assert tq & (tq - 1) == 0 and tk % LANES == 0 and tq % LANES == 0
    use_bias = selb is not None
    assert not use_bias or selb.shape[2] == SEL_BLOCK
    tab = _attn_pairs(seq, tq, tk, window)
    depth = 2 * LANES if use_bias else LANES
    cols = NSA_HEADS * tq
    C = ATTN_TAB_COLS
    nk = seq // tk
    in_specs = [pl.BlockSpec((1, tq, NSA_WIDTH), lambda b, p, t: (b, t[C * p], 0)),
                pl.BlockSpec((tk, KV_WIDTH), lambda b, p, t: (b * nk + t[C * p + 1], 0)),
                pl.BlockSpec((1, KV_WIDTH, tk), lambda b, p, t: (b, 0, t[C * p + 1]))]
    args = [q3d, k_rows, v_t]
    if use_bias:
        in_specs.append(pl.BlockSpec((tk, LANES), lambda b, p, t: (t[C * p + 1], 0)))
        in_specs.append(pl.BlockSpec((1, NSA_KV_HEADS, SEL_BLOCK, tq),
                                     lambda b, p, t: (b, 0, 0, t[C * p])))
        args += [_block_onehot(seq), selb]
    return pl.pallas_call(
        functools.partial(_attn_body, tq=tq, tk=tk, window=window, use_bias=use_bias),
        grid_spec=pltpu.PrefetchScalarGridSpec(
            num_scalar_prefetch=1,
            grid=(batch, tab.shape[0]),
            in_specs=in_specs,
            out_specs=pl.BlockSpec((1, tq, NSA_WIDTH), lambda b, p, t: (b, t[C * p], 0)),
            scratch_shapes=[pltpu.VMEM((depth, cols), BF16), pltpu.VMEM((1, cols), F32),
                            pltpu.VMEM((1, cols), F32), pltpu.VMEM((KV_WIDTH, cols), F32)]),
        out_shape=jax.ShapeDtypeStruct((batch, seq, NSA_WIDTH), F32),
        compiler_params=_cparams(("arbitrary", "arbitrary")),
        name="attn_sel" if use_bias else "attn_win",
    )(jnp.asarray(tab.reshape(-1)), *args)


ATTN_PAGES_PER_STEP = 32
ATTN_PAGED_SPLIT = 2


def _attn_paged_body(pt_ref, qa_ref, bq_ref, bn_ref, kn_ref, *rest, n_pages, n_new):
    page_refs = rest[:n_pages]
    o_ref, m_ref, l_ref, acc_ref = rest[n_pages:]
    c = pl.program_id(1)
    rows = qa_ref.shape[1]

    @pl.when(c == 0)
    def _():
        _softmax_init(m_ref, l_ref, acc_ref)

    n_split = m_ref.shape[0]
    per = n_pages // n_split
    keys = per * PAGE_SIZE
    qa = qa_ref[0]
    lhs = jnp.concatenate([qa, bq_ref[0, 0]], axis=1).astype(BF16)
    scs, vts = [], []
    for s in range(n_split):
        refs_s = page_refs[s * per:(s + 1) * per]
        kt = jnp.concatenate([r[0, 0:KV_WIDTH, :] for r in refs_s], axis=1)
        rhs = jnp.concatenate([kt, _block_onehot_t(s * keys, keys)], axis=0).astype(BF16)
        scs.append(_dot(lhs, rhs))
        vts.append(jnp.concatenate([r[0, KV_WIDTH:, :] for r in refs_s], axis=1).astype(BF16))
    for s in range(n_split):
        _softmax_update(scs[s], vts[s], m_ref.at[s], l_ref.at[s], acc_ref.at[s])

    @pl.when(c == pl.num_programs(1) - 1)
    def _():
        kn = kn_ref[0]
        sc = _dot(qa.astype(BF16), kn[0:KV_WIDTH, :].astype(BF16)) + bn_ref[0]
        tq = lax.broadcasted_iota(jnp.int32, (rows, 1), 0) % n_new
        kk = lax.broadcasted_iota(jnp.int32, (1, kn.shape[1]), 1)
        sc = jnp.where((kk <= tq) & (kk < n_new), sc, NEG_INF)
        _softmax_update(sc, kn[KV_WIDTH:, :].astype(BF16), m_ref.at[0], l_ref.at[0], acc_ref.at[0])
        m_all = m_ref[0]
        for s in range(1, n_split):
            m_all = jnp.maximum(m_all, m_ref[s])
        l_all = jnp.zeros(m_all.shape, F32)
        acc_all = jnp.zeros(m_all.shape, F32)
        for s in range(n_split):
            scale = jnp.exp(m_ref[s] - m_all)
            l_all = l_all + scale * l_ref[s]
            acc_all = acc_all + scale * acc_ref[s]
        o_ref[0] = acc_all / l_all


def _attn_paged(qa, bias_q, bias_new, kv_new_t, pool, page_table, *, n_new):
    batch, n_pages = page_table.shape
    pps = ATTN_PAGES_PER_STEP
    assert n_pages % pps == 0 and pps * PAGE_SIZE // SEL_BLOCK <= SEL_BLOCK
    n_steps = n_pages // pps
    rows = qa.shape[1]

    def page_spec(j):
        return pl.BlockSpec((1, 2 * KV_WIDTH, PAGE_SIZE),
                            lambda b, c, pt: (pt[(b * n_steps + c) * pps + j], 0, 0))

    per_b = lambda b, c, pt: (b, 0, 0)
    return pl.pallas_call(
        functools.partial(_attn_paged_body, n_pages=pps, n_new=n_new),
        grid_spec=pltpu.PrefetchScalarGridSpec(
            num_scalar_prefetch=1,
            grid=(batch, n_steps),
            in_specs=[pl.BlockSpec((1, rows, LANES), per_b),
                      pl.BlockSpec((1, 1, rows, LANES), lambda b, c, pt: (b, c, 0, 0)),
                      pl.BlockSpec((1, rows, LANES), per_b),
                      pl.BlockSpec((1,) + kv_new_t.shape[1:], per_b)]
            + [page_spec(j) for j in range(pps)],
            out_specs=pl.BlockSpec((1, rows, LANES), per_b),
            scratch_shapes=[pltpu.VMEM((ATTN_PAGED_SPLIT, rows, LANES), F32)] * 3),
        out_shape=jax.ShapeDtypeStruct((batch, rows, LANES), F32),
        compiler_params=_cparams(("arbitrary", "arbitrary")),
        name="attn_sel_paged",
    )(page_table.reshape(-1), qa, bias_q, bias_new, kv_new_t, *([pool] * pps))


def _attn_window_body(qa_ref, wb_ref, kn_ref, o_ref, *, n_new, past):
    qa = qa_ref[0].astype(BF16)
    wb, kn = wb_ref[0], kn_ref[0]
    rows, n_buf = qa.shape[0], wb.shape[1]
    qpos = past + lax.broadcasted_iota(jnp.int32, (rows, 1), 0) % n_new

    def masked(sc, kpos, extra):
        diff = qpos - kpos
        return jnp.where((diff >= 0) & (diff < WINDOW) & (kpos >= 0) & extra, sc, NEG_INF)

    nb = lax.broadcasted_iota(jnp.int32, (1, n_buf), 1)
    nn = lax.broadcasted_iota(jnp.int32, (1, kn.shape[1]), 1)
    sb = masked(_dot(qa, wb[0:KV_WIDTH, :].astype(BF16)), past - n_buf + nb, nb >= 0)
    sn = masked(_dot(qa, kn[0:KV_WIDTH, :].astype(BF16)), past + nn, nn < n_new)
    mx = jnp.maximum(jnp.max(sb, axis=1, keepdims=True), jnp.max(sn, axis=1, keepdims=True))
    pb, pn = jnp.exp(sb - mx), jnp.exp(sn - mx)
    o = (_dot_nt(pb.astype(BF16), wb[KV_WIDTH:, :].astype(BF16))
         + _dot_nt(pn.astype(BF16), kn[KV_WIDTH:, :].astype(BF16)))
    o_ref[0] = o / (jnp.sum(pb, axis=1, keepdims=True) + jnp.sum(pn, axis=1, keepdims=True))


def _attn_window_small(qa, win_t, kv_new_t, *, n_new, past):
    batch, rows, _ = qa.shape
    per_b = lambda b: (b, 0, 0)
    return pl.pallas_call(
        functools.partial(_attn_window_body, n_new=n_new, past=past),
        grid=(batch,),
        in_specs=[pl.BlockSpec((1, rows, LANES), per_b),
                  pl.BlockSpec((1,) + win_t.shape[1:], per_b),
                  pl.BlockSpec((1,) + kv_new_t.shape[1:], per_b)],
        out_specs=pl.BlockSpec((1, rows, LANES), per_b),
        out_shape=jax.ShapeDtypeStruct((batch, rows, LANES), F32),
        compiler_params=_cparams(("arbitrary",)),
        name="attn_win_small",
    )(qa, win_t, kv_new_t)


FFN_TM = 256
MXU_DEPTH = 256
FFN_CHUNKS = ((0, 6 * MXU_DEPTH), (6 * MXU_DEPTH, D_FF))


def _ffn_body(x_ref, om_ref, oc_ref, os_ref, ow_ref, gt_ref, ge_ref, gn_ref, gf_ref, gl_ref, wc_ref,
              fb_ref, wo_hbm, wu_hbm, wd_hbm, y_ref, fn_ref, xx_ref, wo_ref, wu_ref, wd_ref, sem_ref,
              *, tm, stride, halo):
    s = pl.program_id(1)

    @pl.when((pl.program_id(0) == 0) & (s == 0))
    def _():
        copies = [pltpu.make_async_copy(src, dst, sem_ref.at[n])
                  for n, (src, dst) in enumerate(((wo_hbm, wo_ref), (wu_hbm, wu_ref), (wd_hbm, wd_ref)))]
        for cp in copies:
            cp.start()
        for cp in copies:
            cp.wait()

    sig = _sigmoid(gt_ref[...])
    hi = sig.astype(BF16)
    lo = (sig - hi.astype(F32)).astype(BF16)
    comb = None
    for br, ob_ref in enumerate((oc_ref, os_ref, ow_ref)):
        gate = _dot(hi, ge_ref[br]) + _dot(lo, ge_ref[br])
        term = gate * ob_ref[...]
        comb = term if comb is None else comb + term
    onsa = _rms(comb, gn_ref[...])
    h = (x_ref[...] + _dot(om_ref[...].astype(BF16), wo_ref[0:MLSTM_WIDTH, :])
         + _dot(onsa.astype(BF16), wo_ref[MLSTM_WIDTH:, :]))
    hn = _rms(h, gf_ref[...]).astype(BF16)

    base = halo - (FFN_CONV - 1) * stride

    @pl.when(s == 0)
    def _():
        xx_ref[base:halo, :] = fb_ref[0]

    y_ref[...] = h
    for lo_col, hi_col in FFN_CHUNKS:
        convs = []
        for half in range(2):
            cols = slice(half * D_FF + lo_col, half * D_FF + hi_col)
            xx_ref[halo:halo + tm, cols] = _dot(hn, wu_ref[:, cols])
            conv = xx_ref[base:base + tm, cols] * wc_ref[0:1, cols]
            for j in range(1, FFN_CONV):
                conv = conv + xx_ref[base + j * stride:base + j * stride + tm, cols] * wc_ref[j:j + 1, cols]
            convs.append(conv)
        act = _silu(convs[1]) * convs[0]
        y_ref[...] += _dot(act.astype(BF16), wd_ref[lo_col:hi_col, :])
    fn_ref[0, 0] = xx_ref[tm + base:tm + halo, :]
    xx_ref[0:halo, :] = xx_ref[tm:tm + halo, :]
    y_ref[...] = _rms(y_ref[...], gl_ref[...])


def _gate_expand():
    ge = np.zeros((N_BRANCH, LANES, NSA_WIDTH), np.float32)
    for hd in range(NSA_HEADS):
        for br in range(N_BRANCH):
            ge[br, GATE_COL_NSA + hd * N_BRANCH + br, hd * HEAD_DIM:(hd + 1) * HEAD_DIM] = 1.0
    return jnp.asarray(ge, BF16)


def _ffn(x2d, om, oc, osel, ow, gt, fbuf, w_out, g_nsa, g_ffn, g_final, w_up, w_fconv, w_down,
         *, nb, tm, stride):
    rows = x2d.shape[0]
    ns = rows // (nb * tm)
    halo = -(-(FFN_CONV - 1) * stride // SUBLANES) * SUBLANES
    assert tm >= halo and all((hi - lo) % MXU_DEPTH == 0 for lo, hi in FFN_CHUNKS)
    tok = lambda b, s: (b * ns + s, 0)
    nfb = (FFN_CONV - 1) * stride

    def const(shape):
        return pl.BlockSpec(shape, lambda b, s: (0,) * len(shape))

    hbm = pl.BlockSpec(memory_space=pl.ANY)
    y, fn = pl.pallas_call(
        functools.partial(_ffn_body, tm=tm, stride=stride, halo=halo),
        grid=(nb, ns),
        in_specs=[pl.BlockSpec((tm, D_MODEL), tok)] + [pl.BlockSpec((tm, NSA_WIDTH), tok)] * 4
        + [pl.BlockSpec((tm, LANES), tok),
           const((N_BRANCH, LANES, NSA_WIDTH)), const((1, NSA_WIDTH)), const((1, D_MODEL)),
           const((1, D_MODEL)), const((FFN_CONV, 2 * D_FF)),
           pl.BlockSpec((1, nfb, 2 * D_FF), lambda b, s: (b, 0, 0)), hbm, hbm, hbm],
        out_specs=[pl.BlockSpec((tm, D_MODEL), tok),
                   pl.BlockSpec((1, 1, nfb, 2 * D_FF), lambda b, s: (b, s, 0, 0))],
        out_shape=[jax.ShapeDtypeStruct((rows, D_MODEL), F32),
                   jax.ShapeDtypeStruct((nb, ns, nfb, 2 * D_FF), F32)],
        scratch_shapes=[pltpu.VMEM((halo + tm, 2 * D_FF), F32),
                        pltpu.VMEM((D_MODEL, D_MODEL), BF16), pltpu.VMEM((D_MODEL, 2 * D_FF), BF16),
                        pltpu.VMEM((D_FF, D_MODEL), BF16), pltpu.SemaphoreType.DMA((3,))],
        compiler_params=_cparams(("arbitrary", "arbitrary")),
        name="outproj_ffn",
    )(x2d, om, oc, osel, ow, gt, _gate_expand(), g_nsa.reshape(1, -1), g_ffn.reshape(1, -1),
      g_final.reshape(1, -1), w_fconv, fbuf, w_out.astype(BF16), w_up.astype(BF16),
      w_down.astype(BF16))
    return y, fn[:, ns - 1]


PROMPT_TM = 512
PROMPT_TQ_CMP = 512
PROMPT_TT_TOPK = 1024
PROMPT_TQ_SEL = 256
PROMPT_TQ_WIN = 256
PROMPT_TK_SEL = 512
PROMPT_TK_WIN = 256


def _kv_rows(kv_t):
    batch, _, rows = kv_t.shape
    return kv_t.reshape(batch, 2, NSA_KV_HEADS, HEAD_DIM, rows).transpose(0, 4, 1, 2, 3)


def _kv_feature_major(kv5):
    batch, rows = kv5.shape[:2]
    return kv5.transpose(0, 2, 3, 4, 1).reshape(batch, 2 * KV_WIDTH, rows)


def _prompt_layer(x, wts):
    batch, seq, _ = x.shape
    x2d = x.reshape(batch * seq, D_MODEL)
    q, kc_rows, vc_rows, mu, mv, mo, gt, ks_rows, kw_rows, kvc_t, kvs_t, kvw_t, vs_t, vw_t = _in_proj(
        x2d, wts["g_mix"], wts["w_in_packed"], batch=batch, seq=seq, tm=min(PROMPT_TM, seq))
    H, DH, W = MLSTM_HEADS, MLSTM_DH, MLSTM_WIDTH
    o_m, mconv, c_new, n_new, m_new = _mlstm(
        mu, mv, mo, gt, jnp.zeros((batch, MLSTM_CONV - 1, W), F32), jnp.zeros((batch, H, DH, DH), F32),
        jnp.zeros((batch, H, DH), F32), jnp.zeros((batch, H), F32),
        wts["w_mconv"], wts["b_mconv"], wts["w_mq"], wts["w_mk"], wts["b_ig"], wts["b_fg"],
        wts["g_mhead"], wts["m_skip"], batch=batch, seq=seq)
    kce, kco = _compress_prompt(kc_rows, vc_rows, wts["cw"], batch=batch, seq=seq)
    n_sel = -(-seq // SEL_BLOCK)
    assert n_sel <= SEL_BLOCK
    o_cmp, scores_t = _cmp_attn(q, kce, kco, batch=batch, seq=seq, tq=min(PROMPT_TQ_CMP, seq), pos0=0)
    selb = _topk_blocks(scores_t.reshape(batch * NSA_KV_HEADS, -1, seq),
                        jnp.arange(seq, dtype=jnp.int32).reshape(1, seq),
                        n_sel=n_sel, nsw=SEL_BLOCK, tt=min(PROMPT_TT_TOPK, seq))
    selb = selb.reshape(batch, NSA_KV_HEADS, SEL_BLOCK, seq)
    q3d = q.reshape(batch, seq, NSA_WIDTH)
    o_sel = _attn_prompt(q3d, ks_rows, vs_t, selb, tq=PROMPT_TQ_SEL, tk=min(PROMPT_TK_SEL, seq),
                         window=None)
    o_win = _attn_prompt(q3d, kw_rows, vw_t, None, tq=PROMPT_TQ_WIN, tk=PROMPT_TK_WIN, window=WINDOW)
    fbuf = jnp.zeros((batch, FFN_CONV - 1, 2 * D_FF), F32)
    y, f_new = _ffn(x2d, o_m, o_cmp, o_sel.reshape(-1, NSA_WIDTH), o_win.reshape(-1, NSA_WIDTH), gt,
                    fbuf, wts["w_out"], wts["g_nsa"], wts["g_ffn"], wts["g_final"], wts["w_up"],
                    wts["w_fconv"], wts["w_down"], nb=batch, tm=min(FFN_TM, seq), stride=1)
    n_win = min(WINDOW, seq)
    return (y.reshape(batch, seq, D_MODEL), _kv_rows(kvc_t), _kv_rows(kvs_t),
            _kv_rows(kvw_t[:, :, seq - n_win:]), mconv, c_new, n_new, m_new.reshape(batch, H), f_new)


def _decode_rows(q2d, batch, seq):
    q5 = (q2d * ATTN_SCALE).reshape(batch, seq, NSA_KV_HEADS, NSA_GROUP, HEAD_DIM).transpose(0, 2, 3, 1, 4)
    eye = jnp.eye(NSA_KV_HEADS, dtype=F32)
    qa = jnp.einsum('bkgtd,kK->bkgtKd', q5, eye)
    return qa.reshape(batch, NSA_KV_HEADS * NSA_GROUP * seq, KV_WIDTH)


def _decode_rows_out(o, batch, seq):
    o6 = o.reshape(batch, NSA_KV_HEADS, NSA_GROUP, seq, NSA_KV_HEADS, HEAD_DIM)
    o5 = jnp.stack([o6[:, kh, :, :, kh, :] for kh in range(NSA_KV_HEADS)], axis=1)
    return o5.transpose(0, 3, 1, 2, 4).reshape(batch * seq, NSA_WIDTH)


def _sample_layer(x, pool_cmp, pool_sel, win_buf, m_conv, m_c, m_n, m_m, f_buf, page_table, wts):
    batch, seq, _ = x.shape
    n_pages = page_table.shape[1]
    past = n_pages * PAGE_SIZE
    assert past % SEL_BLOCK == 0 and seq <= SEL_BLOCK and seq < CMP_BLOCK
    x2d = x.reshape(batch * seq, D_MODEL)
    q, _, _, mu, mv, mo, gt, _, _, kvc_t, kvs_t, kvw_t, _, _ = _in_proj(
        x2d, wts["g_mix"], wts["w_in_packed"], batch=1, seq=batch * seq, tm=batch * seq)
    per_batch = lambda a: a.reshape(2 * KV_WIDTH, batch, seq).transpose(1, 0, 2)
    kvc_t, kvs_t, kvw_t = per_batch(kvc_t), per_batch(kvs_t), per_batch(kvw_t)
    pad_keys = lambda a: jnp.pad(a, ((0, 0), (0, 0), (0, LANES - seq)))
    H = MLSTM_HEADS
    o_m, mconv, c_new, n_new, m_new = _mlstm(
        mu, mv, mo, gt, m_conv, m_c, m_n, m_m,
        wts["w_mconv"], wts["b_mconv"], wts["w_mq"], wts["w_mk"], wts["b_ig"], wts["b_fg"],
        wts["g_mhead"], wts["m_skip"], batch=batch, seq=seq)
    pool_cmp3, pool_sel3 = _kv_feature_major(pool_cmp), _kv_feature_major(pool_sel)
    kce, kco = _compress_paged(pool_cmp3, page_table, wts["cw"], wts["cw_pages"])
    n_past_blk = past // SEL_BLOCK
    n_sel = -(-(past + seq) // SEL_BLOCK)
    o_cmp, scores_t = _cmp_attn(q, kce, kco, batch=batch, seq=seq, tq=seq, pos0=past)
    ns = scores_t.shape[2]
    nsw = ns + LANES
    scores_all = scores_t.transpose(1, 2, 0, 3).reshape(NSA_KV_HEADS, ns, batch * seq)
    pos_all = (past + jnp.arange(batch * seq, dtype=jnp.int32) % seq).reshape(1, batch * seq)
    selb = _topk_blocks(scores_all, pos_all, n_sel=n_sel, nsw=nsw, tt=batch * seq)
    selb = selb.reshape(NSA_KV_HEADS, nsw, batch, seq).transpose(2, 0, 3, 1)
    qa = _decode_rows(q, batch, seq)
    rows = qa.shape[1]
    blk_per_step = ATTN_PAGES_PER_STEP * PAGE_SIZE // SEL_BLOCK
    n_steps = n_pages // ATTN_PAGES_PER_STEP
    sb_rows = jnp.broadcast_to(selb[:, :, None], (batch, NSA_KV_HEADS, NSA_GROUP, seq, selb.shape[-1]))
    sb_rows = sb_rows.reshape(batch, rows, selb.shape[-1])
    bias_q = sb_rows[:, :, :n_past_blk].reshape(batch, rows, n_steps, blk_per_step).transpose(0, 2, 1, 3)
    bias_q = jnp.pad(bias_q, ((0, 0), (0, 0), (0, 0), (0, LANES - blk_per_step)))
    bias_new = jnp.broadcast_to(sb_rows[:, :, n_past_blk:n_past_blk + 1], (batch, rows, LANES))
    o_sel = _attn_paged(qa, bias_q, bias_new, pad_keys(kvs_t), pool_sel3, page_table, n_new=seq)
    n_buf = win_buf.shape[1]
    assert past >= n_buf
    win_t = _kv_feature_major(win_buf)
    o_win = _attn_window_small(qa, win_t, pad_keys(kvw_t), n_new=seq, past=past)
    win_new = jnp.concatenate([win_t, kvw_t], axis=2)[:, :, seq:]
    tmaj = lambda a: a.reshape(batch, seq, -1).transpose(1, 0, 2).reshape(batch * seq, -1)
    fb_t = f_buf.transpose(1, 0, 2).reshape(1, (FFN_CONV - 1) * batch, 2 * D_FF)
    y, f_new = _ffn(tmaj(x2d), tmaj(o_m), tmaj(o_cmp), tmaj(_decode_rows_out(o_sel, batch, seq)),
                    tmaj(_decode_rows_out(o_win, batch, seq)), tmaj(gt), fb_t,
                    wts["w_out"], wts["g_nsa"], wts["g_ffn"], wts["g_final"], wts["w_up"],
                    wts["w_fconv"], wts["w_down"], nb=1, tm=batch * seq, stride=batch)
    y = y.reshape(seq, batch, D_MODEL).transpose(1, 0, 2)
    f_new = f_new.reshape(FFN_CONV - 1, batch, 2 * D_FF).transpose(1, 0, 2)
    return (y, _kv_rows(kvc_t), _kv_rows(kvs_t), _kv_rows(win_new), mconv, c_new, n_new,
            m_new.reshape(batch, H), f_new)


def kernel(x_prompt, x_sample, cache_cmp, cache_sel, state_win, state_mlstm_C, state_mlstm_n,
           state_mlstm_m, state_mlstm_conv, state_ffn_conv, page_table,
           g_mix, w_in, w_out, w_mconv, b_mconv, w_mq, w_mk, b_ig, b_fg, g_mhead, m_skip,
           pe_cmp, w_cmp1, w_cmp2, g_nsa, g_ffn, w_up, w_fconv, w_down, g_final):
    assert w_in.shape[0] == 1, "one layer: the final norm is fused into the layer's FFN kernel"
    l = 0
    wts = dict(g_mix=g_mix[l], w_in_packed=_pack_w_in(w_in[l]), w_out=w_out[l], w_mconv=w_mconv[l],
               b_mconv=b_mconv[l], w_mq=w_mq[l], w_mk=w_mk[l], b_ig=b_ig[l], b_fg=b_fg[l],
               g_mhead=g_mhead[l], m_skip=m_skip[l],
               cw=_pack_compress_weights(pe_cmp[l], w_cmp1[l], w_cmp2[l]),
               cw_pages=_page_pair_constants(pe_cmp[l]),
               g_nsa=g_nsa[l], g_ffn=g_ffn[l], g_final=g_final, w_up=w_up[l], w_fconv=w_fconv[l],
               w_down=w_down[l])
    p = _prompt_layer(x_prompt, wts)
    s = _sample_layer(x_sample, cache_cmp[l], cache_sel[l], state_win[l], state_mlstm_conv[l],
                      state_mlstm_C[l], state_mlstm_n[l], state_mlstm_m[l], state_ffn_conv[l],
                      page_table, wts)
    yp, cmp_p, sel_p, win_p, mconv_p, c_p, n_p, m_p, fconv_p = p
    ys, cmp_s, sel_s, win_s, mconv_s, c_s, n_s, m_s, fconv_s = s
    st = lambda a: a[None]
    return (yp, ys, st(cmp_p), st(cmp_s), st(sel_p), st(sel_s), st(win_p), st(win_s),
            st(c_p), st(c_s), st(n_p), st(n_s), st(m_p), st(m_s), st(mconv_p), st(mconv_s),
            st(fconv_p), st(fconv_s))
```

```python
import functools

import numpy as np
import jax
import jax.numpy as jnp
from jax import lax
from jax.experimental import pallas as pl
from jax.experimental.pallas import tpu as pltpu

F32 = jnp.float32
BF16 = jnp.bfloat16

D_MODEL = 1024
PAGE_SIZE = 128
HEAD_DIM = 64
NSA_HEADS = 8
NSA_KV_HEADS = 2
NSA_GROUP = NSA_HEADS // NSA_KV_HEADS
NSA_WIDTH = NSA_HEADS * HEAD_DIM
KV_WIDTH = NSA_KV_HEADS * HEAD_DIM
CMP_BLOCK = 32
CMP_HIDDEN = 2 * HEAD_DIM
SEL_BLOCK = 64
TOP_N = 16
WINDOW = 512
N_BRANCH = 3
ATTN_SCALE = HEAD_DIM ** -0.5
MLSTM_HEADS = 4
MLSTM_WIDTH = D_MODEL - NSA_WIDTH
MLSTM_DH = MLSTM_WIDTH // MLSTM_HEADS
MLSTM_CONV = 4
D_FF = ((8 * D_MODEL // 3 + 127) // 128) * 128
FFN_CONV = 3
EPS = 1e-6
NEG_INF = -1e30
SEL_PRIORITY = 1e4
LOG2_E = 1.4426950408889634

LANES = 128
SUBLANES = 8
VMEM_LIMIT = 48 * 1024 * 1024

GATE_COL_NSA = 0
GATE_COL_I = NSA_HEADS * N_BRANCH
GATE_COL_F = GATE_COL_I + MLSTM_HEADS

MLSTM_CHUNK = 128
MLSTM_SEQS_PER_STEP = 4


def _cparams(sem):
    return pltpu.CompilerParams(dimension_semantics=sem, vmem_limit_bytes=VMEM_LIMIT)


def _dot(a, b):
    return jnp.dot(a, b, preferred_element_type=F32)


def _dot_nt(a, b):
    return lax.dot_general(a, b, (((1,), (1,)), ((), ())), preferred_element_type=F32)


def _sigmoid(x):
    return 1.0 / (1.0 + jnp.exp(-x))


def _silu(x):
    return x * _sigmoid(x)


def _rms(x, g):
    return x * lax.rsqrt(jnp.mean(x * x, axis=-1, keepdims=True) + EPS) * g


IN_ROW_WIDTHS = (NSA_WIDTH, KV_WIDTH, KV_WIDTH, MLSTM_WIDTH, MLSTM_WIDTH, MLSTM_WIDTH, LANES,
                 KV_WIDTH, KV_WIDTH)
IN_ROW_DTYPES = (F32,) * 7 + (BF16,) * 2
N_KV_BRANCH = 3


def _inproj_body(x_ref, g_ref, w_ref, wt_ref, *out_refs):
    xb = _rms(x_ref[...], g_ref[...]).astype(BF16)
    off = 0
    n_rows = len(IN_ROW_WIDTHS)
    for ref in out_refs[:n_rows]:
        n = ref.shape[-1]
        ref[...] = _dot(xb, w_ref[:, off:off + n]).astype(ref.dtype)
        off += n
    kv_refs = out_refs[n_rows:n_rows + N_KV_BRANCH]
    vt_refs = out_refs[n_rows + N_KV_BRANCH:]
    for n, ref in enumerate(kv_refs):
        kv_t = _dot_nt(wt_ref[n * 2 * KV_WIDTH:(n + 1) * 2 * KV_WIDTH, :], xb)
        ref[0] = kv_t
        if n > 0:
            vt_refs[n - 1][0] = kv_t[KV_WIDTH:, :].astype(BF16)


def _pack_w_in(w_in):
    splits = np.cumsum([NSA_WIDTH, 2 * KV_WIDTH, 2 * KV_WIDTH, 2 * KV_WIDTH, NSA_HEADS * N_BRANCH,
                        MLSTM_WIDTH, MLSTM_WIDTH, MLSTM_WIDTH, MLSTM_HEADS]).tolist()
    q, kvc, kvs, kvw, gt, mu, mv, mo, mi, mf = jnp.split(w_in, splits, axis=1)
    gates = jnp.concatenate([gt, mi, mf], axis=1)
    gates = jnp.pad(gates, ((0, 0), (0, LANES - gates.shape[1])))
    w_rows = jnp.concatenate([q, kvc, mu, mv, mo, gates, kvs[:, :KV_WIDTH], kvw[:, :KV_WIDTH]],
                             axis=1).astype(BF16)
    w_kv_t = jnp.concatenate([kvc, kvs, kvw], axis=1).T.astype(BF16)
    return w_rows, w_kv_t


def _in_proj(x2d, g_mix, w_packed, *, batch, seq, tm):
    w_rows, w_kv_t = w_packed
    t = x2d.shape[0]
    ns = seq // tm
    kv_sd = jax.ShapeDtypeStruct((batch, 2 * KV_WIDTH, seq), F32)
    vt_sd = jax.ShapeDtypeStruct((batch, KV_WIDTH, seq), BF16)
    feat_major = lambda rows: pl.BlockSpec((1, rows, tm), lambda i: (i // ns, 0, i % ns))
    return pl.pallas_call(
        _inproj_body,
        grid=(t // tm,),
        in_specs=[pl.BlockSpec((tm, D_MODEL), lambda i: (i, 0)),
                  pl.BlockSpec((1, D_MODEL), lambda i: (0, 0)),
                  pl.BlockSpec(w_rows.shape, lambda i: (0, 0)),
                  pl.BlockSpec(w_kv_t.shape, lambda i: (0, 0))],
        out_specs=[pl.BlockSpec((tm, n), lambda i: (i, 0)) for n in IN_ROW_WIDTHS]
        + [feat_major(2 * KV_WIDTH)] * N_KV_BRANCH + [feat_major(KV_WIDTH)] * (N_KV_BRANCH - 1),
        out_shape=[jax.ShapeDtypeStruct((t, n), dt) for n, dt in zip(IN_ROW_WIDTHS, IN_ROW_DTYPES)]
        + [kv_sd] * N_KV_BRANCH + [vt_sd] * (N_KV_BRANCH - 1),
        compiler_params=_cparams(("arbitrary",)),
        name="in_proj",
    )(x2d, g_mix.reshape(1, D_MODEL), w_rows, w_kv_t)


def _mlstm_body(*refs, valid, bb):
    cb_ref, c0_ref, n0_ref, m0_ref = refs[4:8]
    cn_ref, c_ref, n_ref, m_ref, xx_ref = refs[16:21]
    halo = SUBLANES

    @pl.when(pl.program_id(1) == 0)
    def _():
        xx_ref[:, 0:halo, :] = jnp.zeros((bb, halo, MLSTM_WIDTH), F32)
        xx_ref[:, halo - (MLSTM_CONV - 1):halo, :] = cb_ref[...]
        c_ref[...] = c0_ref[...]
        n_ref[...] = n0_ref[...]
        m_ref[...] = m0_ref[...]

    _mlstm_chunk(*refs, valid=valid, bb=bb)


def _mlstm_chunk(mu_ref, mv_ref, mo_ref, g_ref, cb_ref, c0_ref, n0_ref, m0_ref,
                 wc_ref, bc_ref, wq_ref, wk_ref, gb_ref, gh_ref, sk_ref,
                 o_ref, cn_ref, c_ref, n_ref, m_ref,
                 xx_ref, vpad_ref, gpad_ref, *, valid, bb):
    L = MLSTM_CHUNK
    DH = MLSTM_DH
    halo = SUBLANES
    units = [(bi, h) for bi in range(bb) for h in range(MLSTM_HEADS)]
    head_lanes = lambda h: slice(h * DH, (h + 1) * DH)
    row = lax.broadcasted_iota(jnp.int32, (L, L), 0)
    col = lax.broadcasted_iota(jnp.int32, (L, L), 1)
    tril = row >= col
    triu = row <= col
    tok_col = lax.broadcasted_iota(jnp.int32, (L, 1), 0)
    tok_row = lax.broadcasted_iota(jnp.int32, (1, L), 1)

    def log_sigmoid(x):
        return jnp.minimum(x, 0.0) - jnp.log(1.0 + jnp.exp(-jnp.abs(x)))

    uc, gb, gbt = {}, {}, {}
    for bi in range(bb):
        if valid < L:
            xx_ref[bi, halo:, :] = jnp.zeros((L, MLSTM_WIDTH), F32)
            vpad_ref[bi] = jnp.zeros((L, MLSTM_WIDTH), F32)
            gpad_ref[bi] = jnp.zeros((L, LANES), F32)
        xx_ref[bi, halo:halo + valid, :] = mu_ref[bi]
        vpad_ref[bi, 0:valid, :] = mv_ref[bi]
        gpad_ref[bi, 0:valid, :] = g_ref[bi]
        conv = xx_ref[bi, halo - 3:halo - 3 + L, :] * wc_ref[0:1, :]
        for j in range(1, MLSTM_CONV):
            conv = conv + xx_ref[bi, halo - 3 + j:halo - 3 + j + L, :] * wc_ref[j:j + 1, :]
        uc[bi] = _silu(conv + bc_ref[...])
        tail = xx_ref[bi, valid + halo - 3:valid + halo, :]
        xx_ref[bi, halo - 3:halo, :] = tail
        cn_ref[bi] = tail
        gb[bi] = gpad_ref[bi] + gb_ref[...]
        gbt[bi] = gb[bi].T

    q, k, qb, kb = {}, {}, {}, {}
    for u in units:
        bi, h = u
        ub = uc[bi][:, head_lanes(h)].astype(BF16)
        q[u] = _dot(ub, wq_ref[h])
        k[u] = _dot(ub, wk_ref[h]) * (DH ** -0.5)
        qb[u], kb[u] = q[u].astype(BF16), k[u].astype(BF16)

    ic_col, ic_row, cum_col, cum_row = {}, {}, {}, {}
    for u in units:
        bi, h = u
        ic_c = gb[bi][:, GATE_COL_I + h:GATE_COL_I + h + 1]
        ic_r = gbt[bi][GATE_COL_I + h:GATE_COL_I + h + 1, :]
        lf_c = log_sigmoid(gb[bi][:, GATE_COL_F + h:GATE_COL_F + h + 1])
        lf_r = log_sigmoid(gbt[bi][GATE_COL_F + h:GATE_COL_F + h + 1, :])
        if valid < L:
            ic_c = jnp.where(tok_col < valid, ic_c, NEG_INF)
            ic_r = jnp.where(tok_row < valid, ic_r, NEG_INF)
            lf_c = jnp.where(tok_col < valid, lf_c, 0.0)
            lf_r = jnp.where(tok_row < valid, lf_r, 0.0)
        ic_col[u], ic_row[u] = ic_c, ic_r
        cum_col[u] = jnp.sum(jnp.where(tril, lf_r, 0.0), axis=1, keepdims=True)
        cum_row[u] = jnp.sum(jnp.where(triu, lf_c, 0.0), axis=0, keepdims=True)

    m_t, w, sc = {}, {}, {}
    for u in units:
        bi, h = u
        m0 = m_ref[bi, 0:1, h:h + 1]
        dmat = jnp.where(tril, cum_col[u] - cum_row[u] + ic_row[u], NEG_INF)
        inter = cum_col[u] + m0
        m_t[u] = jnp.maximum(inter, jnp.max(dmat, axis=1, keepdims=True))
        w[u] = jnp.exp(dmat - m_t[u])
        sc[u] = jnp.exp(inter - m_t[u])

    hc = {}
    for u in units:
        bi, h = u
        s = _dot_nt(qb[u], kb[u]) * w[u]
        v = vpad_ref[bi, :, head_lanes(h)]
        c_old = c_ref[bi, h]
        n_old = n_ref[bi, h:h + 1, :]
        num = _dot(s.astype(BF16), v.astype(BF16)) + sc[u] * _dot_nt(qb[u], c_old.astype(BF16))
        den = (jnp.sum(s, axis=1, keepdims=True)
               + sc[u] * jnp.sum(q[u] * n_old, axis=1, keepdims=True))
        hc[u] = num / jnp.maximum(jnp.abs(den), jnp.exp(-m_t[u]))

    for u in units:
        bi, h = u
        m0 = m_ref[bi, 0:1, h:h + 1]
        m_new = m_t[u][L - 1:L, :]
        cum_last = cum_col[u][L - 1:L, :]
        wl = jnp.exp(cum_last - cum_col[u] + ic_col[u] - m_new)
        sl = jnp.exp(cum_last + m0 - m_new)
        v = vpad_ref[bi, :, head_lanes(h)]
        vw_t = (v * wl).T.astype(BF16)
        c_ref[bi, h] = sl * c_ref[bi, h] + _dot(vw_t, kb[u])
        n_ref[bi, h:h + 1, :] = sl * n_ref[bi, h:h + 1, :] + jnp.sum(wl * k[u], axis=0, keepdims=True)
        m_ref[bi, 0:1, h:h + 1] = m_new

    for u in units:
        bi, h = u
        hn = _rms(hc[u], gh_ref[:, head_lanes(h)])
        u_h = uc[bi][:, head_lanes(h)]
        out = ((hn[0:valid, :] + sk_ref[:, head_lanes(h)] * u_h[0:valid, :])
               * _sigmoid(mo_ref[bi, :, head_lanes(h)]))
        o_ref[bi, :, head_lanes(h)] = out


def _mlstm(mu, mv, mo, gates, conv_buf, c0, n0, m0, w_mconv, b_mconv, w_mq, w_mk, b_ig, b_fg,
           g_mhead, m_skip, *, batch, seq):
    L = MLSTM_CHUNK
    valid = min(seq, L)
    assert seq % valid == 0 and (valid == L or seq == valid)
    nc = seq // valid
    gate_bias = jnp.zeros((1, LANES), F32)
    gate_bias = gate_bias.at[0, GATE_COL_I:GATE_COL_I + MLSTM_HEADS].set(b_ig)
    gate_bias = gate_bias.at[0, GATE_COL_F:GATE_COL_F + MLSTM_HEADS].set(b_fg)
    bb = MLSTM_SEQS_PER_STEP
    assert batch % bb == 0
    tok = lambda b, c: (b, c, 0)
    const2 = lambda b, c: (0, 0)
    const3 = lambda b, c: (0, 0, 0)
    per_b3 = lambda b, c: (b, 0, 0)
    per_b4 = lambda b, c: (b, 0, 0, 0)
    H, DH, W = MLSTM_HEADS, MLSTM_DH, MLSTM_WIDTH
    rows3 = lambda a: a.reshape(batch, seq, a.shape[-1])
    o_m, conv_new, c_new, n_new, m_new = pl.pallas_call(
        functools.partial(_mlstm_body, valid=valid, bb=bb),
        grid=(batch // bb, nc),
        in_specs=[pl.BlockSpec((bb, valid, W), tok), pl.BlockSpec((bb, valid, W), tok),
                  pl.BlockSpec((bb, valid, W), tok), pl.BlockSpec((bb, valid, LANES), tok),
                  pl.BlockSpec((bb, MLSTM_CONV - 1, W), per_b3),
                  pl.BlockSpec((bb, H, DH, DH), per_b4),
                  pl.BlockSpec((bb, H, DH), per_b3),
                  pl.BlockSpec((bb, 1, H), per_b3),
                  pl.BlockSpec((MLSTM_CONV, W), const2), pl.BlockSpec((1, W), const2),
                  pl.BlockSpec((H, DH, DH), const3), pl.BlockSpec((H, DH, DH), const3),
                  pl.BlockSpec((1, LANES), const2), pl.BlockSpec((1, W), const2),
                  pl.BlockSpec((1, W), const2)],
        out_specs=[pl.BlockSpec((bb, valid, W), tok),
                   pl.BlockSpec((bb, MLSTM_CONV - 1, W), per_b3),
                   pl.BlockSpec((bb, H, DH, DH), per_b4),
                   pl.BlockSpec((bb, H, DH), per_b3),
                   pl.BlockSpec((bb, 1, H), per_b3)],
        out_shape=[jax.ShapeDtypeStruct((batch, seq, W), F32),
                   jax.ShapeDtypeStruct((batch, MLSTM_CONV - 1, W), F32),
                   jax.ShapeDtypeStruct((batch, H, DH, DH), F32),
                   jax.ShapeDtypeStruct((batch, H, DH), F32),
                   jax.ShapeDtypeStruct((batch, 1, H), F32)],
        scratch_shapes=[pltpu.VMEM((bb, SUBLANES + L, W), F32), pltpu.VMEM((bb, L, W), F32),
                        pltpu.VMEM((bb, L, LANES), F32)],
        compiler_params=_cparams(("arbitrary", "arbitrary")),
        name="mlstm",
    )(rows3(mu), rows3(mv), rows3(mo), rows3(gates), conv_buf, c0, n0, m0.reshape(batch, 1, H),
      w_mconv, b_mconv.reshape(1, W), w_mq.astype(BF16), w_mk.astype(BF16), gate_bias,
      g_mhead.reshape(1, W), m_skip.reshape(1, W))
    return o_m.reshape(batch * seq, W), conv_new, c_new, n_new, m_new


def _compress_rows(xk_ref, xv_ref, pe_ref, w1_ref, w2_ref, n_pairs):
    pair_rows = 2 * CMP_BLOCK
    outs = []
    for kv, x_ref in enumerate((xk_ref, xv_ref)):
        acc = jnp.zeros((2 * n_pairs, NSA_KV_HEADS * CMP_HIDDEN), F32)
        for r in range(CMP_BLOCK):
            ev = x_ref[pl.ds(r, n_pairs, stride=pair_rows), :]
            od = x_ref[pl.ds(CMP_BLOCK + r, n_pairs, stride=pair_rows), :]
            xr = jnp.concatenate([ev, od], axis=0) + pe_ref[kv, r:r + 1, :]
            acc = acc + _dot(xr.astype(BF16), w1_ref[kv, r])
        outs.append(_dot(_silu(acc).astype(BF16), w2_ref[kv]))
    return jnp.concatenate(outs, axis=1)


def _compress_body(xk_ref, xv_ref, pe_ref, w1_ref, w2_ref, oe_ref, oo_ref, *, n_pairs):
    out = _compress_rows(xk_ref, xv_ref, pe_ref, w1_ref, w2_ref, n_pairs)
    oe_ref[0] = out[0:n_pairs, :]
    oo_ref[0] = out[n_pairs:, :]


BLOCKS_PER_PAGE = PAGE_SIZE // CMP_BLOCK


def _compress_paged_body(pt_ref, *refs, n_pages):
    page_refs = refs[:n_pages]
    pet_ref, perm_ref, w1_ref, w2_ref, oe_ref, oo_ref, buf_ref, os_ref = refs[n_pages:]
    grp = 2 * BLOCKS_PER_PAGE
    for jp in range(n_pages // 2):
        xt = jnp.concatenate([page_refs[2 * jp][0], page_refs[2 * jp + 1][0]], axis=1)
        xb = (xt + pet_ref[...]).astype(BF16)
        xp = _dot_nt(perm_ref[...], xb)
        for r in range(CMP_BLOCK):
            for kv in range(2):
                lane0 = (2 * kv + r % 2) * KV_WIDTH
                buf_ref[r // 2, grp * jp:grp * (jp + 1), lane0:lane0 + KV_WIDTH] = (
                    xp[grp * r:grp * (r + 1), kv * KV_WIDTH:(kv + 1) * KV_WIDTH])
    for kv in range(2):
        lanes = slice(2 * kv * KV_WIDTH, 2 * (kv + 1) * KV_WIDTH)
        acc = _dot(buf_ref[0, :, lanes].astype(BF16), w1_ref[kv, 0])
        for r2 in range(1, CMP_BLOCK // 2):
            acc = acc + _dot(buf_ref[r2, :, lanes].astype(BF16), w1_ref[kv, r2])
        os_ref[kv] = _dot(_silu(acc).astype(BF16), w2_ref[kv])
    half = os_ref.shape[1] // 2
    for parity, ref in enumerate((oe_ref, oo_ref)):
        ref[0] = jnp.concatenate([os_ref[kv, pl.ds(parity, half, stride=2), :] for kv in range(2)],
                                 axis=1)


def _page_pair_constants(pe):
    pe_t = jnp.broadcast_to(pe.transpose(0, 2, 1)[:, None, :, None, :],
                            (2, NSA_KV_HEADS, HEAD_DIM, 2 * BLOCKS_PER_PAGE, CMP_BLOCK))
    pe_t = pe_t.reshape(2 * KV_WIDTH, 2 * PAGE_SIZE)
    grp = 2 * BLOCKS_PER_PAGE
    perm = np.zeros((2 * PAGE_SIZE, 2 * PAGE_SIZE), np.float32)
    for r in range(CMP_BLOCK):
        for b in range(grp):
            perm[r * grp + b, b * CMP_BLOCK + r] = 1.0
    return pe_t, jnp.asarray(perm, BF16)


def _pack_compress_weights(pe, w1, w2):
    eye_h = jnp.eye(NSA_KV_HEADS, dtype=F32)
    pe_r = jnp.broadcast_to(pe[:, :, None, :], (2, CMP_BLOCK, NSA_KV_HEADS, HEAD_DIM))
    pe_r = pe_r.reshape(2, CMP_BLOCK, KV_WIDTH)
    w1r = w1.reshape(2, CMP_BLOCK, HEAD_DIM, CMP_HIDDEN)
    w1_big = jnp.einsum('krdc,hH->krhdHc', w1r, eye_h)
    w1_big = w1_big.reshape(2, CMP_BLOCK, KV_WIDTH, NSA_KV_HEADS * CMP_HIDDEN).astype(BF16)
    w2_big = jnp.einsum('kcd,hH->khcHd', w2, eye_h)
    w2_big = w2_big.reshape(2, NSA_KV_HEADS * CMP_HIDDEN, KV_WIDTH).astype(BF16)
    return pe_r, w1_big, w2_big


def _compress_prompt(k_rows, v_rows, cw, *, batch, seq):
    pe_r, w1_big, w2_big = cw
    n_pairs = seq // (2 * CMP_BLOCK)
    const3 = lambda b: (0, 0, 0)
    out_sd = jax.ShapeDtypeStruct((batch, n_pairs, 2 * KV_WIDTH), F32)
    return pl.pallas_call(
        functools.partial(_compress_body, n_pairs=n_pairs),
        grid=(batch,),
        in_specs=[pl.BlockSpec((seq, KV_WIDTH), lambda b: (b, 0)),
                  pl.BlockSpec((seq, KV_WIDTH), lambda b: (b, 0)),
                  pl.BlockSpec(pe_r.shape, const3),
                  pl.BlockSpec(w1_big.shape, lambda b: (0, 0, 0, 0)),
                  pl.BlockSpec(w2_big.shape, const3)],
        out_specs=[pl.BlockSpec((1, n_pairs, 2 * KV_WIDTH), lambda b: (b, 0, 0))] * 2,
        out_shape=[out_sd, out_sd],
        compiler_params=_cparams(("arbitrary",)),
        name="compress_prompt",
    )(k_rows, v_rows, pe_r, w1_big, w2_big)


COMPRESS_PAGES_PER_STEP = 64


def _compress_paged(pool, page_table, cw, cw_pages):
    _, w1_big, w2_big = cw
    w1_big = w1_big.reshape(2, CMP_BLOCK // 2, 2 * KV_WIDTH, NSA_KV_HEADS * CMP_HIDDEN)
    pe_t, perm = cw_pages
    batch, n_pages = page_table.shape
    pps = COMPRESS_PAGES_PER_STEP
    assert n_pages % pps == 0 and pps % 2 == 0
    n_steps = n_pages // pps
    n_blk = pps * BLOCKS_PER_PAGE
    const3 = lambda b, c, pt: (0, 0, 0)

    def page_spec(j):
        return pl.BlockSpec((1, 2 * KV_WIDTH, PAGE_SIZE),
                            lambda b, c, pt: (pt[(b * n_steps + c) * pps + j], 0, 0))

    return pl.pallas_call(
        functools.partial(_compress_paged_body, n_pages=pps),
        grid_spec=pltpu.PrefetchScalarGridSpec(
            num_scalar_prefetch=1,
            grid=(batch, n_steps),
            in_specs=[page_spec(j) for j in range(pps)] + [
                pl.BlockSpec(pe_t.shape, lambda b, c, pt: (0, 0)),
                pl.BlockSpec(perm.shape, lambda b, c, pt: (0, 0)),
                pl.BlockSpec(w1_big.shape, lambda b, c, pt: (0, 0, 0, 0)),
                pl.BlockSpec(w2_big.shape, const3)],
            out_specs=[pl.BlockSpec((1, n_blk // 2, 2 * KV_WIDTH), lambda b, c, pt: (b, c, 0))] * 2,
            scratch_shapes=[pltpu.VMEM((CMP_BLOCK // 2, n_blk, 4 * KV_WIDTH), F32),
                            pltpu.VMEM((2, n_blk, KV_WIDTH), F32)]),
        out_shape=[jax.ShapeDtypeStruct((batch, n_steps * n_blk // 2, 2 * KV_WIDTH), F32)] * 2,
        compiler_params=_cparams(("arbitrary", "arbitrary")),
        name="compress_paged",
    )(page_table.reshape(-1), *([pool] * pps), pe_t, perm, w1_big, w2_big)


def _cmp_attn_body(q_ref, ke_ref, ko_ref, o_ref, st_ref, *, tq, pos0):
    ns = ke_ref.shape[1]
    i = pl.program_id(1)
    rows = NSA_GROUP * tq
    tok0 = pos0 + i * tq
    pos_c = tok0 + lax.broadcasted_iota(jnp.int32, (rows, 1), 0) % tq
    pos_r = tok0 + lax.broadcasted_iota(jnp.int32, (1, rows), 1) % tq
    pair_r = lax.broadcasted_iota(jnp.int32, (1, ns), 1)
    pair_c = lax.broadcasted_iota(jnp.int32, (ns, 1), 0)
    end_e = lambda pair: (2 * pair + 1) * CMP_BLOCK - 1
    end_o = lambda pair: (2 * pair + 2) * CMP_BLOCK - 1
    any_c = (CMP_BLOCK - 1 <= pos_c).astype(F32)
    any_r = (CMP_BLOCK - 1 <= pos_r).astype(F32)
    q = q_ref[...] * ATTN_SCALE
    for kh in range(NSA_KV_HEADS):
        qs = jnp.concatenate([q[:, (kh * NSA_GROUP + g) * HEAD_DIM:(kh * NSA_GROUP + g + 1) * HEAD_DIM]
                              for g in range(NSA_GROUP)], axis=0).astype(BF16)
        ks, vs = slice(kh * HEAD_DIM, (kh + 1) * HEAD_DIM), slice(KV_WIDTH + kh * HEAD_DIM,
                                                                   KV_WIDTH + (kh + 1) * HEAD_DIM)
        ke, ko = ke_ref[0, :, ks].astype(BF16), ko_ref[0, :, ks].astype(BF16)
        se = jnp.where(end_e(pair_r) <= pos_c, _dot_nt(qs, ke), NEG_INF)
        so = jnp.where(end_o(pair_r) <= pos_c, _dot_nt(qs, ko), NEG_INF)
        mx = jnp.maximum(jnp.max(se, axis=1, keepdims=True), jnp.max(so, axis=1, keepdims=True))
        pe, po = jnp.exp(se - mx), jnp.exp(so - mx)
        inv = any_c / (jnp.sum(pe, axis=1, keepdims=True) + jnp.sum(po, axis=1, keepdims=True))
        oh = (_dot((pe * inv).astype(BF16), ke_ref[0, :, vs].astype(BF16))
              + _dot((po * inv).astype(BF16), ko_ref[0, :, vs].astype(BF16)))
        for g in range(NSA_GROUP):
            hd = kh * NSA_GROUP + g
            o_ref[:, hd * HEAD_DIM:(hd + 1) * HEAD_DIM] = oh[g * tq:(g + 1) * tq, :]
        te = jnp.where(end_e(pair_c) <= pos_r, _dot_nt(ke, qs), NEG_INF)
        to = jnp.where(end_o(pair_c) <= pos_r, _dot_nt(ko, qs), NEG_INF)
        mt = jnp.maximum(jnp.max(te, axis=0, keepdims=True), jnp.max(to, axis=0, keepdims=True))
        pte, pto = jnp.exp(te - mt), jnp.exp(to - mt)
        invt = any_r / (jnp.sum(pte, axis=0, keepdims=True) + jnp.sum(pto, axis=0, keepdims=True))
        ps = (pte + pto) * invt
        score = ps[:, 0:tq]
        for g in range(1, NSA_GROUP):
            score = score + ps[:, g * tq:(g + 1) * tq]
        st_ref[0, kh] = score


def _cmp_attn(q2d, kce, kco, *, batch, seq, tq, pos0):
    ns = kce.shape[1]
    nq = seq // tq
    return pl.pallas_call(
        functools.partial(_cmp_attn_body, tq=tq, pos0=pos0),
        grid=(batch, nq),
        in_specs=[pl.BlockSpec((tq, NSA_WIDTH), lambda b, i: (b * nq + i, 0)),
                  pl.BlockSpec((1, ns, 2 * KV_WIDTH), lambda b, i: (b, 0, 0)),
                  pl.BlockSpec((1, ns, 2 * KV_WIDTH), lambda b, i: (b, 0, 0))],
        out_specs=[pl.BlockSpec((tq, NSA_WIDTH), lambda b, i: (b * nq + i, 0)),
                   pl.BlockSpec((1, NSA_KV_HEADS, ns, tq), lambda b, i: (b, 0, 0, i))],
        out_shape=[jax.ShapeDtypeStruct((batch * seq, NSA_WIDTH), F32),
                   jax.ShapeDtypeStruct((batch, NSA_KV_HEADS, ns, seq), F32)],
        compiler_params=_cparams(("arbitrary", "arbitrary")),
        name="cmp_attn",
    )(q2d, kce, kco)


def _topk_body(pos_ref, st_ref, b_ref, *, n_sel):
    score = st_ref[0]
    ns, tt = score.shape
    nsw = b_ref.shape[1]
    if nsw > ns:
        score = jnp.concatenate([score, jnp.zeros((nsw - ns, tt), F32)], axis=0)
    blk = lax.broadcasted_iota(jnp.int32, (nsw, 1), 0)
    blk_f = blk.astype(F32)
    cur = pos_ref[...] // SEL_BLOCK
    forced = (blk == 0) | (blk == cur) | (blk == cur - 1)
    pri = jnp.where(blk <= cur, jnp.where(forced, SEL_PRIORITY, score), -SEL_PRIORITY)
    pri = jnp.where(blk < n_sel, pri, -jnp.inf)
    bias = jnp.full((nsw, tt), NEG_INF, F32)
    for _ in range(min(TOP_N, n_sel)):
        top = jnp.max(pri, axis=0, keepdims=True)
        first = jnp.min(jnp.where(pri == top, blk_f, float(nsw)), axis=0, keepdims=True)
        hit = blk_f == first
        bias = jnp.where(hit, 0.0, bias)
        pri = jnp.where(hit, -jnp.inf, pri)
    b_ref[0] = bias


def _topk_blocks(scores_t, pos, *, n_sel, nsw, tt):
    groups, ns, tokens = scores_t.shape
    assert nsw >= max(ns, n_sel) and tokens % tt == 0
    return pl.pallas_call(
        functools.partial(_topk_body, n_sel=n_sel),
        grid=(groups, tokens // tt),
        in_specs=[pl.BlockSpec((1, tt), lambda g, i: (0, i)),
                  pl.BlockSpec((1, ns, tt), lambda g, i: (g, 0, i))],
        out_specs=pl.BlockSpec((1, nsw, tt), lambda g, i: (g, 0, i)),
        out_shape=jax.ShapeDtypeStruct((groups, nsw, tokens), F32),
        compiler_params=_cparams(("arbitrary", "arbitrary")),
        name="topk_blocks",
    )(pos, scores_t)


def _softmax_update(sc, vt_bf16, m_ref, l_ref, acc_ref):
    m_old = m_ref[...]
    m_new = jnp.maximum(m_old, jnp.max(sc, axis=1, keepdims=True))
    alpha = jnp.exp(m_old - m_new)
    pr = jnp.exp(sc - jnp.concatenate([m_new] * (sc.shape[1] // LANES), axis=1))
    l_ref[...] = alpha * l_ref[...] + jnp.sum(pr, axis=1, keepdims=True)
    acc_ref[...] = alpha * acc_ref[...] + _dot_nt(pr.astype(BF16), vt_bf16)
    m_ref[...] = m_new


def _softmax_init(m_ref, l_ref, acc_ref):
    m_ref[...] = jnp.full(m_ref.shape, NEG_INF, F32)
    l_ref[...] = jnp.zeros(l_ref.shape, F32)
    acc_ref[...] = jnp.zeros(acc_ref.shape, F32)


def _block_onehot_t(first_key, n_keys):
    blk = (first_key + lax.broadcasted_iota(jnp.int32, (1, n_keys), 1)) // SEL_BLOCK
    r = lax.broadcasted_iota(jnp.int32, (LANES, 1), 0) & (SEL_BLOCK - 1)
    return (r == blk).astype(F32)


ATTN_TAB_COLS = 5
ATTN_COL_BLOCK = 256


def _attn_pairs(seq, tq, tk, window):
    rows = []
    for i in range(seq // tq):
        t_lo, t_hi = i * tq, i * tq + tq - 1
        k_lo = 0 if window is None else max(0, t_lo - window + 1)
        js = list(range(k_lo // tk, t_hi // tk + 1))
        for n, j in enumerate(js):
            partial_tile = j * tk + tk - 1 > t_lo or (window is not None and j * tk <= t_hi - window)
            rows.append((i, j, int(n == 0), int(n == len(js) - 1), int(partial_tile)))
    return np.asarray(rows, np.int32)


def _attn_body(tab_ref, q_ref, k_ref, vt_ref, *rest, tq, tk, window, use_bias):
    if use_bias:
        oh_ref, sb_ref, o_ref, qa_ref, m_ref, l_ref, acc_ref = rest
    else:
        o_ref, qa_ref, m_ref, l_ref, acc_ref = rest
    p = pl.program_id(1)
    i, j, first, last, partial_tile = [tab_ref[ATTN_TAB_COLS * p + n] for n in range(ATTN_TAB_COLS)]
    G = NSA_GROUP
    cols = NSA_HEADS * tq
    zeros64 = jnp.zeros((HEAD_DIM, tq), F32)

    def kv_head_rows(x, kh):
        return jnp.concatenate([x, zeros64] if kh == 0 else [zeros64, x], axis=0)

    @pl.when(first == 1)
    def _():
        q = q_ref[0] * (ATTN_SCALE * LOG2_E)
        for m in range(NSA_HEADS // 2):
            q_t = q[:, m * LANES:(m + 1) * LANES].T
            for hd in (2 * m, 2 * m + 1):
                kh = hd // G
                piece = kv_head_rows(q_t[(hd % 2) * HEAD_DIM:(hd % 2 + 1) * HEAD_DIM, :], kh)
                if use_bias:
                    piece = jnp.concatenate([piece, kv_head_rows(sb_ref[0, kh], kh)], axis=0)
                qa_ref[:, hd * tq:(hd + 1) * tq] = piece.astype(BF16)
        m_ref[...] = jnp.full(m_ref.shape, NEG_INF, F32)
        l_ref[...] = jnp.zeros(l_ref.shape, F32)
        acc_ref[...] = jnp.zeros(acc_ref.shape, F32)

    k_aug = k_ref[...]
    if use_bias:
        k_aug = jnp.concatenate([k_aug, oh_ref[...]], axis=1)
    sc = _dot(k_aug, qa_ref[...])
    vt = vt_ref[0]

    def update(sc):
        m_old = m_ref[...]
        m_new = jnp.maximum(m_old, jnp.max(sc, axis=0, keepdims=True))
        alpha = jnp.exp2(m_old - m_new)
        pr = jnp.exp2(sc - m_new)
        l_ref[...] = alpha * l_ref[...] + jnp.sum(pr, axis=0, keepdims=True)
        acc_ref[...] = alpha * acc_ref[...] + _dot(vt, pr.astype(BF16))
        m_ref[...] = m_new

    @pl.when(partial_tile == 1)
    def _():
        qpos = i * tq + (lax.broadcasted_iota(jnp.int32, (1, cols), 1) & (tq - 1))
        kpos = j * tk + lax.broadcasted_iota(jnp.int32, (tk, 1), 0)
        valid = kpos <= qpos
        if window is not None:
            valid = valid & (kpos > qpos - window)
        update(jnp.where(valid, sc, NEG_INF))

    @pl.when(partial_tile == 0)
    def _():
        update(sc)

    @pl.when(last == 1)
    def _():
        o_t = acc_ref[...] / l_ref[...]
        for m in range(NSA_HEADS // 2):
            pair = jnp.concatenate(
                [o_t[(hd // G) * HEAD_DIM:(hd // G + 1) * HEAD_DIM, hd * tq:(hd + 1) * tq]
                 for hd in (2 * m, 2 * m + 1)], axis=0)
            o_ref[0, :, m * LANES:(m + 1) * LANES] = pair.T


def _block_onehot(seq):
    blk = np.arange(seq)[:, None] // SEL_BLOCK
    return jnp.asarray((np.arange(LANES)[None, :] % SEL_BLOCK) == blk, BF16)


def _attn_prompt(q3d, k_rows, v_t, selb, *, tq, tk, window):
    batch, seq, _ = q3d.shape
    assert tq & (tq - 1) == 0 and tk % LANES == 0 and tq % LANES == 0
    use_bias = selb is not None
    assert not use_bias or selb.shape[2] == SEL_BLOCK
    tab = _attn_pairs(seq, tq, tk, window)
    depth = 2 * LANES if use_bias else LANES
    cols = NSA_HEADS * tq
    C = ATTN_TAB_COLS
    nk = seq // tk
    in_specs = [pl.BlockSpec((1, tq, NSA_WIDTH), lambda b, p, t: (b, t[C * p], 0)),
                pl.BlockSpec((tk, KV_WIDTH), lambda b, p, t: (b * nk + t[C * p + 1], 0)),
                pl.BlockSpec((1, KV_WIDTH, tk), lambda b, p, t: (b, 0, t[C * p + 1]))]
    args = [q3d, k_rows, v_t]
    if use_bias:
        in_specs.append(pl.BlockSpec((tk, LANES), lambda b, p, t: (t[C * p + 1], 0)))
        in_specs.append(pl.BlockSpec((1, NSA_KV_HEADS, SEL_BLOCK, tq),
                                     lambda b, p, t: (b, 0, 0, t[C * p])))
        args += [_block_onehot(seq), selb]
    return pl.pallas_call(
        functools.partial(_attn_body, tq=tq, tk=tk, window=window, use_bias=use_bias),
        grid_spec=pltpu.PrefetchScalarGridSpec(
            num_scalar_prefetch=1,
            grid=(batch, tab.shape[0]),
            in_specs=in_specs,
            out_specs=pl.BlockSpec((1, tq, NSA_WIDTH), lambda b, p, t: (b, t[C * p], 0)),
            scratch_shapes=[pltpu.VMEM((depth, cols), BF16), pltpu.VMEM((1, cols), F32),
                            pltpu.VMEM((1, cols), F32), pltpu.VMEM((KV_WIDTH, cols), F32)]),
        out_shape=jax.ShapeDtypeStruct((batch, seq, NSA_WIDTH), F32),
        compiler_params=_cparams(("arbitrary", "arbitrary")),
        name="attn_sel" if use_bias else "attn_win",
    )(jnp.asarray(tab.reshape(-1)), *args)


ATTN_PAGES_PER_STEP = 32
ATTN_PAGED_SPLIT = 2


def _attn_paged_body(pt_ref, qa_ref, bq_ref, bn_ref, kn_ref, *rest, n_pages, n_new):
    page_refs = rest[:n_pages]
    o_ref, m_ref, l_ref, acc_ref = rest[n_pages:]
    c = pl.program_id(1)
    rows = qa_ref.shape[1]

    @pl.when(c == 0)
    def _():
        _softmax_init(m_ref, l_ref, acc_ref)

    n_split = m_ref.shape[0]
    per = n_pages // n_split
    keys = per * PAGE_SIZE
    qa = qa_ref[0]
    lhs = jnp.concatenate([qa, bq_ref[0, 0]], axis=1).astype(BF16)
    scs, vts = [], []
    for s in range(n_split):
        refs_s = page_refs[s * per:(s + 1) * per]
        kt = jnp.concatenate([r[0, 0:KV_WIDTH, :] for r in refs_s], axis=1)
        rhs = jnp.concatenate([kt, _block_onehot_t(s * keys, keys)], axis=0).astype(BF16)
        scs.append(_dot(lhs, rhs))
        vts.append(jnp.concatenate([r[0, KV_WIDTH:, :] for r in refs_s], axis=1).astype(BF16))
    for s in range(n_split):
        _softmax_update(scs[s], vts[s], m_ref.at[s], l_ref.at[s], acc_ref.at[s])

    @pl.when(c == pl.num_programs(1) - 1)
    def _():
        kn = kn_ref[0]
        sc = _dot(qa.astype(BF16), kn[0:KV_WIDTH, :].astype(BF16)) + bn_ref[0]
        tq = lax.broadcasted_iota(jnp.int32, (rows, 1), 0) % n_new
        kk = lax.broadcasted_iota(jnp.int32, (1, kn.shape[1]), 1)
        sc = jnp.where((kk <= tq) & (kk < n_new), sc, NEG_INF)
        _softmax_update(sc, kn[KV_WIDTH:, :].astype(BF16), m_ref.at[0], l_ref.at[0], acc_ref.at[0])
        m_all = m_ref[0]
        for s in range(1, n_split):
            m_all = jnp.maximum(m_all, m_ref[s])
        l_all = jnp.zeros(m_all.shape, F32)
        acc_all = jnp.zeros(m_all.shape, F32)
        for s in range(n_split):
            scale = jnp.exp(m_ref[s] - m_all)
            l_all = l_all + scale * l_ref[s]
            acc_all = acc_all + scale * acc_ref[s]
        o_ref[0] = acc_all / l_all


def _attn_paged(qa, bias_q, bias_new, kv_new_t, pool, page_table, *, n_new):
    batch, n_pages = page_table.shape
    pps = ATTN_PAGES_PER_STEP
    assert n_pages % pps == 0 and pps * PAGE_SIZE // SEL_BLOCK <= SEL_BLOCK
    n_steps = n_pages // pps
    rows = qa.shape[1]

    def page_spec(j):
        return pl.BlockSpec((1, 2 * KV_WIDTH, PAGE_SIZE),
                            lambda b, c, pt: (pt[(b * n_steps + c) * pps + j], 0, 0))

    per_b = lambda b, c, pt: (b, 0, 0)
    return pl.pallas_call(
        functools.partial(_attn_paged_body, n_pages=pps, n_new=n_new),
        grid_spec=pltpu.PrefetchScalarGridSpec(
            num_scalar_prefetch=1,
            grid=(batch, n_steps),
            in_specs=[pl.BlockSpec((1, rows, LANES), per_b),
                      pl.BlockSpec((1, 1, rows, LANES), lambda b, c, pt: (b, c, 0, 0)),
                      pl.BlockSpec((1, rows, LANES), per_b),
                      pl.BlockSpec((1,) + kv_new_t.shape[1:], per_b)]
            + [page_spec(j) for j in range(pps)],
            out_specs=pl.BlockSpec((1, rows, LANES), per_b),
            scratch_shapes=[pltpu.VMEM((ATTN_PAGED_SPLIT, rows, LANES), F32)] * 3),
        out_shape=jax.ShapeDtypeStruct((batch, rows, LANES), F32),
        compiler_params=_cparams(("arbitrary", "arbitrary")),
        name="attn_sel_paged",
    )(page_table.reshape(-1), qa, bias_q, bias_new, kv_new_t, *([pool] * pps))


def _attn_window_body(qa_ref, wb_ref, kn_ref, o_ref, *, n_new, past):
    qa = qa_ref[0].astype(BF16)
    wb, kn = wb_ref[0], kn_ref[0]
    rows, n_buf = qa.shape[0], wb.shape[1]
    qpos = past + lax.broadcasted_iota(jnp.int32, (rows, 1), 0) % n_new

    def masked(sc, kpos, extra):
        diff = qpos - kpos
        return jnp.where((diff >= 0) & (diff < WINDOW) & (kpos >= 0) & extra, sc, NEG_INF)

    nb = lax.broadcasted_iota(jnp.int32, (1, n_buf), 1)
    nn = lax.broadcasted_iota(jnp.int32, (1, kn.shape[1]), 1)
    sb = masked(_dot(qa, wb[0:KV_WIDTH, :].astype(BF16)), past - n_buf + nb, nb >= 0)
    sn = masked(_dot(qa, kn[0:KV_WIDTH, :].astype(BF16)), past + nn, nn < n_new)
    mx = jnp.maximum(jnp.max(sb, axis=1, keepdims=True), jnp.max(sn, axis=1, keepdims=True))
    pb, pn = jnp.exp(sb - mx), jnp.exp(sn - mx)
    o = (_dot_nt(pb.astype(BF16), wb[KV_WIDTH:, :].astype(BF16))
         + _dot_nt(pn.astype(BF16), kn[KV_WIDTH:, :].astype(BF16)))
    o_ref[0] = o / (jnp.sum(pb, axis=1, keepdims=True) + jnp.sum(pn, axis=1, keepdims=True))


def _attn_window_small(qa, win_t, kv_new_t, *, n_new, past):
    batch, rows, _ = qa.shape
    per_b = lambda b: (b, 0, 0)
    return pl.pallas_call(
        functools.partial(_attn_window_body, n_new=n_new, past=past),
        grid=(batch,),
        in_specs=[pl.BlockSpec((1, rows, LANES), per_b),
                  pl.BlockSpec((1,) + win_t.shape[1:], per_b),
                  pl.BlockSpec((1,) + kv_new_t.shape[1:], per_b)],
        out_specs=pl.BlockSpec((1, rows, LANES), per_b),
        out_shape=jax.ShapeDtypeStruct((batch, rows, LANES), F32),
        compiler_params=_cparams(("arbitrary",)),
        name="attn_win_small",
    )(qa, win_t, kv_new_t)


FFN_TM = 512
FFN_VMEM_LIMIT = 58 * 1024 * 1024
MXU_DEPTH = 256
FFN_CHUNKS = ((0, 6 * MXU_DEPTH), (6 * MXU_DEPTH, D_FF))


def _ffn_body(x_ref, om_ref, oc_ref, os_ref, ow_ref, gt_ref, ge_ref, gn_ref, gf_ref, gl_ref, wc_ref,
              fb_ref, wo_hbm, wu_hbm, wd_hbm, y_ref, fn_ref, xx_ref, wo_ref, wu_ref, wd_ref, sem_ref,
              *, tm, stride, halo):
    s = pl.program_id(1)

    @pl.when((pl.program_id(0) == 0) & (s == 0))
    def _():
        copies = [pltpu.make_async_copy(src, dst, sem_ref.at[n])
                  for n, (src, dst) in enumerate(((wo_hbm, wo_ref), (wu_hbm, wu_ref), (wd_hbm, wd_ref)))]
        for cp in copies:
            cp.start()
        for cp in copies:
            cp.wait()

    sig = _sigmoid(gt_ref[...])
    hi = sig.astype(BF16)
    lo = (sig - hi.astype(F32)).astype(BF16)
    comb = None
    for br, ob_ref in enumerate((oc_ref, os_ref, ow_ref)):
        gate = _dot(hi, ge_ref[br]) + _dot(lo, ge_ref[br])
        term = gate * ob_ref[...]
        comb = term if comb is None else comb + term
    onsa = _rms(comb, gn_ref[...])
    h = (x_ref[...] + _dot(om_ref[...].astype(BF16), wo_ref[0:MLSTM_WIDTH, :])
         + _dot(onsa.astype(BF16), wo_ref[MLSTM_WIDTH:, :]))
    hn = _rms(h, gf_ref[...]).astype(BF16)

    base = halo - (FFN_CONV - 1) * stride

    @pl.when(s == 0)
    def _():
        xx_ref[base:halo, :] = fb_ref[0]

    y_ref[...] = h
    for lo_col, hi_col in FFN_CHUNKS:
        convs = []
        for half in range(2):
            cols = slice(half * D_FF + lo_col, half * D_FF + hi_col)
            xx_ref[halo:halo + tm, cols] = _dot(hn, wu_ref[:, cols])
            conv = xx_ref[base:base + tm, cols] * wc_ref[0:1, cols]
            for j in range(1, FFN_CONV):
                conv = conv + xx_ref[base + j * stride:base + j * stride + tm, cols] * wc_ref[j:j + 1, cols]
            convs.append(conv)
        act = _silu(convs[1]) * convs[0]
        y_ref[...] += _dot(act.astype(BF16), wd_ref[lo_col:hi_col, :])
    fn_ref[0, 0] = xx_ref[tm + base:tm + halo, :]
    xx_ref[0:halo, :] = xx_ref[tm:tm + halo, :]
    y_ref[...] = _rms(y_ref[...], gl_ref[...])


def _gate_expand():
    ge = np.zeros((N_BRANCH, LANES, NSA_WIDTH), np.float32)
    for hd in range(NSA_HEADS):
        for br in range(N_BRANCH):
            ge[br, GATE_COL_NSA + hd * N_BRANCH + br, hd * HEAD_DIM:(hd + 1) * HEAD_DIM] = 1.0
    return jnp.asarray(ge, BF16)


def _ffn(x2d, om, oc, osel, ow, gt, fbuf, w_out, g_nsa, g_ffn, g_final, w_up, w_fconv, w_down,
         *, nb, tm, stride):
    rows = x2d.shape[0]
    ns = rows // (nb * tm)
    halo = -(-(FFN_CONV - 1) * stride // SUBLANES) * SUBLANES
    assert tm >= halo and all((hi - lo) % MXU_DEPTH == 0 for lo, hi in FFN_CHUNKS)
    tok = lambda b, s: (b * ns + s, 0)
    nfb = (FFN_CONV - 1) * stride

    def const(shape):
        return pl.BlockSpec(shape, lambda b, s: (0,) * len(shape))

    hbm = pl.BlockSpec(memory_space=pl.ANY)
    y, fn = pl.pallas_call(
        functools.partial(_ffn_body, tm=tm, stride=stride, halo=halo),
        grid=(nb, ns),
        in_specs=[pl.BlockSpec((tm, D_MODEL), tok)] + [pl.BlockSpec((tm, NSA_WIDTH), tok)] * 4
        + [pl.BlockSpec((tm, LANES), tok),
           const((N_BRANCH, LANES, NSA_WIDTH)), const((1, NSA_WIDTH)), const((1, D_MODEL)),
           const((1, D_MODEL)), const((FFN_CONV, 2 * D_FF)),
           pl.BlockSpec((1, nfb, 2 * D_FF), lambda b, s: (b, 0, 0)), hbm, hbm, hbm],
        out_specs=[pl.BlockSpec((tm, D_MODEL), tok),
                   pl.BlockSpec((1, 1, nfb, 2 * D_FF), lambda b, s: (b, s, 0, 0))],
        out_shape=[jax.ShapeDtypeStruct((rows, D_MODEL), F32),
                   jax.ShapeDtypeStruct((nb, ns, nfb, 2 * D_FF), F32)],
        scratch_shapes=[pltpu.VMEM((halo + tm, 2 * D_FF), F32),
                        pltpu.VMEM((D_MODEL, D_MODEL), BF16), pltpu.VMEM((D_MODEL, 2 * D_FF), BF16),
                        pltpu.VMEM((D_FF, D_MODEL), BF16), pltpu.SemaphoreType.DMA((3,))],
        compiler_params=pltpu.CompilerParams(dimension_semantics=("arbitrary", "arbitrary"),
                                             vmem_limit_bytes=FFN_VMEM_LIMIT),
        name="outproj_ffn",
    )(x2d, om, oc, osel, ow, gt, _gate_expand(), g_nsa.reshape(1, -1), g_ffn.reshape(1, -1),
      g_final.reshape(1, -1), w_fconv, fbuf, w_out.astype(BF16), w_up.astype(BF16),
      w_down.astype(BF16))
    return y, fn[:, ns - 1]


PROMPT_TM = 512
PROMPT_TQ_CMP = 512
PROMPT_TT_TOPK = 1024
PROMPT_TQ_SEL = 512
PROMPT_TQ_WIN = 256
PROMPT_TK_SEL = 512
PROMPT_TK_WIN = 256


def _kv_rows(kv_t):
    batch, _, rows = kv_t.shape
    return kv_t.reshape(batch, 2, NSA_KV_HEADS, HEAD_DIM, rows).transpose(0, 4, 1, 2, 3)


def _kv_feature_major(kv5):
    batch, rows = kv5.shape[:2]
    return kv5.transpose(0, 2, 3, 4, 1).reshape(batch, 2 * KV_WIDTH, rows)


def _prompt_layer(x, wts):
    batch, seq, _ = x.shape
    x2d = x.reshape(batch * seq, D_MODEL)
    q, kc_rows, vc_rows, mu, mv, mo, gt, ks_rows, kw_rows, kvc_t, kvs_t, kvw_t, vs_t, vw_t = _in_proj(
        x2d, wts["g_mix"], wts["w_in_packed"], batch=batch, seq=seq, tm=min(PROMPT_TM, seq))
    H, DH, W = MLSTM_HEADS, MLSTM_DH, MLSTM_WIDTH
    o_m, mconv, c_new, n_new, m_new = _mlstm(
        mu, mv, mo, gt, jnp.zeros((batch, MLSTM_CONV - 1, W), F32), jnp.zeros((batch, H, DH, DH), F32),
        jnp.zeros((batch, H, DH), F32), jnp.zeros((batch, H), F32),
        wts["w_mconv"], wts["b_mconv"], wts["w_mq"], wts["w_mk"], wts["b_ig"], wts["b_fg"],
        wts["g_mhead"], wts["m_skip"], batch=batch, seq=seq)
    kce, kco = _compress_prompt(kc_rows, vc_rows, wts["cw"], batch=batch, seq=seq)
    n_sel = -(-seq // SEL_BLOCK)
    assert n_sel <= SEL_BLOCK
    o_cmp, scores_t = _cmp_attn(q, kce, kco, batch=batch, seq=seq, tq=min(PROMPT_TQ_CMP, seq), pos0=0)
    selb = _topk_blocks(scores_t.reshape(batch * NSA_KV_HEADS, -1, seq),
                        jnp.arange(seq, dtype=jnp.int32).reshape(1, seq),
                        n_sel=n_sel, nsw=SEL_BLOCK, tt=min(PROMPT_TT_TOPK, seq))
    selb = selb.reshape(batch, NSA_KV_HEADS, SEL_BLOCK, seq)
    q3d = q.reshape(batch, seq, NSA_WIDTH)
    o_sel = _attn_prompt(q3d, ks_rows, vs_t, selb, tq=PROMPT_TQ_SEL, tk=min(PROMPT_TK_SEL, seq),
                         window=None)
    o_win = _attn_prompt(q3d, kw_rows, vw_t, None, tq=PROMPT_TQ_WIN, tk=PROMPT_TK_WIN, window=WINDOW)
    fbuf = jnp.zeros((batch, FFN_CONV - 1, 2 * D_FF), F32)
    y, f_new = _ffn(x2d, o_m, o_cmp, o_sel.reshape(-1, NSA_WIDTH), o_win.reshape(-1, NSA_WIDTH), gt,
                    fbuf, wts["w_out"], wts["g_nsa"], wts["g_ffn"], wts["g_final"], wts["w_up"],
                    wts["w_fconv"], wts["w_down"], nb=batch, tm=min(FFN_TM, seq), stride=1)
    n_win = min(WINDOW, seq)
    return (y.reshape(batch, seq, D_MODEL), _kv_rows(kvc_t), _kv_rows(kvs_t),
            _kv_rows(kvw_t[:, :, seq - n_win:]), mconv, c_new, n_new, m_new.reshape(batch, H), f_new)


def _decode_rows(q2d, batch, seq):
    q5 = (q2d * ATTN_SCALE).reshape(batch, seq, NSA_KV_HEADS, NSA_GROUP, HEAD_DIM).transpose(0, 2, 3, 1, 4)
    eye = jnp.eye(NSA_KV_HEADS, dtype=F32)
    qa = jnp.einsum('bkgtd,kK->bkgtKd', q5, eye)
    return qa.reshape(batch, NSA_KV_HEADS * NSA_GROUP * seq, KV_WIDTH)


def _decode_rows_out(o, batch, seq):
    o6 = o.reshape(batch, NSA_KV_HEADS, NSA_GROUP, seq, NSA_KV_HEADS, HEAD_DIM)
    o5 = jnp.stack([o6[:, kh, :, :, kh, :] for kh in range(NSA_KV_HEADS)], axis=1)
    return o5.transpose(0, 3, 1, 2, 4).reshape(batch * seq, NSA_WIDTH)


def _sample_layer(x, pool_cmp, pool_sel, win_buf, m_conv, m_c, m_n, m_m, f_buf, page_table, wts):
    batch, seq, _ = x.shape
    n_pages = page_table.shape[1]
    past = n_pages * PAGE_SIZE
    assert past % SEL_BLOCK == 0 and seq <= SEL_BLOCK and seq < CMP_BLOCK
    x2d = x.reshape(batch * seq, D_MODEL)
    q, _, _, mu, mv, mo, gt, _, _, kvc_t, kvs_t, kvw_t, _, _ = _in_proj(
        x2d, wts["g_mix"], wts["w_in_packed"], batch=1, seq=batch * seq, tm=batch * seq)
    per_batch = lambda a: a.reshape(2 * KV_WIDTH, batch, seq).transpose(1, 0, 2)
    kvc_t, kvs_t, kvw_t = per_batch(kvc_t), per_batch(kvs_t), per_batch(kvw_t)
    pad_keys = lambda a: jnp.pad(a, ((0, 0), (0, 0), (0, LANES - seq)))
    H = MLSTM_HEADS
    o_m, mconv, c_new, n_new, m_new = _mlstm(
        mu, mv, mo, gt, m_conv, m_c, m_n, m_m,
        wts["w_mconv"], wts["b_mconv"], wts["w_mq"], wts["w_mk"], wts["b_ig"], wts["b_fg"],
        wts["g_mhead"], wts["m_skip"], batch=batch, seq=seq)
    pool_cmp3, pool_sel3 = _kv_feature_major(pool_cmp), _kv_feature_major(pool_sel)
    kce, kco = _compress_paged(pool_cmp3, page_table, wts["cw"], wts["cw_pages"])
    n_past_blk = past // SEL_BLOCK
    n_sel = -(-(past + seq) // SEL_BLOCK)
    o_cmp, scores_t = _cmp_attn(q, kce, kco, batch=batch, seq=seq, tq=seq, pos0=past)
    ns = scores_t.shape[2]
    nsw = ns + LANES
    scores_all = scores_t.transpose(1, 2, 0, 3).reshape(NSA_KV_HEADS, ns, batch * seq)
    pos_all = (past + jnp.arange(batch * seq, dtype=jnp.int32) % seq).reshape(1, batch * seq)
    selb = _topk_blocks(scores_all, pos_all, n_sel=n_sel, nsw=nsw, tt=batch * seq)
    selb = selb.reshape(NSA_KV_HEADS, nsw, batch, seq).transpose(2, 0, 3, 1)
    qa = _decode_rows(q, batch, seq)
    rows = qa.shape[1]
    blk_per_step = ATTN_PAGES_PER_STEP * PAGE_SIZE // SEL_BLOCK
    n_steps = n_pages // ATTN_PAGES_PER_STEP
    sb_rows = jnp.broadcast_to(selb[:, :, None], (batch, NSA_KV_HEADS, NSA_GROUP, seq, selb.shape[-1]))
    sb_rows = sb_rows.reshape(batch, rows, selb.shape[-1])
    bias_q = sb_rows[:, :, :n_past_blk].reshape(batch, rows, n_steps, blk_per_step).transpose(0, 2, 1, 3)
    bias_q = jnp.pad(bias_q, ((0, 0), (0, 0), (0, 0), (0, LANES - blk_per_step)))
    bias_new = jnp.broadcast_to(sb_rows[:, :, n_past_blk:n_past_blk + 1], (batch, rows, LANES))
    o_sel = _attn_paged(qa, bias_q, bias_new, pad_keys(kvs_t), pool_sel3, page_table, n_new=seq)
    n_buf = win_buf.shape[1]
    assert past >= n_buf
    win_t = _kv_feature_major(win_buf)
    o_win = _attn_window_small(qa, win_t, pad_keys(kvw_t), n_new=seq, past=past)
    win_new = jnp.concatenate([win_t, kvw_t], axis=2)[:, :, seq:]
    tmaj = lambda a: a.reshape(batch, seq, -1).transpose(1, 0, 2).reshape(batch * seq, -1)
    fb_t = f_buf.transpose(1, 0, 2).reshape(1, (FFN_CONV - 1) * batch, 2 * D_FF)
    y, f_new = _ffn(tmaj(x2d), tmaj(o_m), tmaj(o_cmp), tmaj(_decode_rows_out(o_sel, batch, seq)),
                    tmaj(_decode_rows_out(o_win, batch, seq)), tmaj(gt), fb_t,
                    wts["w_out"], wts["g_nsa"], wts["g_ffn"], wts["g_final"], wts["w_up"],
                    wts["w_fconv"], wts["w_down"], nb=1, tm=batch * seq, stride=batch)
    y = y.reshape(seq, batch, D_MODEL).transpose(1, 0, 2)
    f_new = f_new.reshape(FFN_CONV - 1, batch, 2 * D_FF).transpose(1, 0, 2)
    return (y, _kv_rows(kvc_t), _kv_rows(kvs_t), _kv_rows(win_new), mconv, c_new, n_new,
            m_new.reshape(batch, H), f_new)


def kernel(x_prompt, x_sample, cache_cmp, cache_sel, state_win, state_mlstm_C, state_mlstm_n,
           state_mlstm_m, state_mlstm_conv, state_ffn_conv, page_table,
           g_mix, w_in, w_out, w_mconv, b_mconv, w_mq, w_mk, b_ig, b_fg, g_mhead, m_skip,
           pe_cmp, w_cmp1, w_cmp2, g_nsa, g_ffn, w_up, w_fconv, w_down, g_final):
    assert w_in.shape[0] == 1, "one layer: the final norm is fused into the layer's FFN kernel"
    l = 0
    wts = dict(g_mix=g_mix[l], w_in_packed=_pack_w_in(w_in[l]), w_out=w_out[l], w_mconv=w_mconv[l],
               b_mconv=b_mconv[l], w_mq=w_mq[l], w_mk=w_mk[l], b_ig=b_ig[l], b_fg=b_fg[l],
               g_mhead=g_mhead[l], m_skip=m_skip[l],
               cw=_pack_compress_weights(pe_cmp[l], w_cmp1[l], w_cmp2[l]),
               cw_pages=_page_pair_constants(pe_cmp[l]),
               g_nsa=g_nsa[l], g_ffn=g_ffn[l], g_final=g_final, w_up=w_up[l], w_fconv=w_fconv[l],
               w_down=w_down[l])
    p = _prompt_layer(x_prompt, wts)
    s = _sample_layer(x_sample, cache_cmp[l], cache_sel[l], state_win[l], state_mlstm_conv[l],
                      state_mlstm_C[l], state_mlstm_n[l], state_mlstm_m[l], state_ffn_conv[l],
                      page_table, wts)
    yp, cmp_p, sel_p, win_p, mconv_p, c_p, n_p, m_p, fconv_p = p
    ys, cmp_s, sel_s, win_s, mconv_s, c_s, n_s, m_s, fconv_s = s
    st = lambda a: a[None]
    return (yp, ys, st(cmp_p), st(cmp_s), st(sel_p), st(sel_s), st(win_p), st(win_s),
            st(c_p), st(c_s), st(n_p), st(n_s), st(m_p), st(m_s), st(mconv_p), st(mconv_s),
            st(fconv_p), st(fconv_s))
```

```python
import functools

import numpy as np
import jax
import jax.numpy as jnp
from jax import lax
from jax.experimental import pallas as pl
from jax.experimental.pallas import tpu as pltpu

F32 = jnp.float32
BF16 = jnp.bfloat16

D_MODEL = 1024
PAGE_SIZE = 128
HEAD_DIM = 64
NSA_HEADS = 8
NSA_KV_HEADS = 2
NSA_GROUP = NSA_HEADS // NSA_KV_HEADS
NSA_WIDTH = NSA_HEADS * HEAD_DIM
KV_WIDTH = NSA_KV_HEADS * HEAD_DIM
CMP_BLOCK = 32
CMP_HIDDEN = 2 * HEAD_DIM
SEL_BLOCK = 64
TOP_N = 16
WINDOW = 512
N_BRANCH = 3
ATTN_SCALE = HEAD_DIM ** -0.5
MLSTM_HEADS = 4
MLSTM_WIDTH = D_MODEL - NSA_WIDTH
MLSTM_DH = MLSTM_WIDTH // MLSTM_HEADS
MLSTM_CONV = 4
D_FF = ((8 * D_MODEL // 3 + 127) // 128) * 128
FFN_CONV = 3
EPS = 1e-6
NEG_INF = -1e30
SEL_PRIORITY = 1e4
LOG2_E = 1.4426950408889634

LANES = 128
SUBLANES = 8
VMEM_LIMIT = 48 * 1024 * 1024

GATE_COL_NSA = 0
GATE_COL_I = NSA_HEADS * N_BRANCH
GATE_COL_F = GATE_COL_I + MLSTM_HEADS

MLSTM_CHUNK = 128
MLSTM_SEQS_PER_STEP = 4


def _cparams(sem):
    return pltpu.CompilerParams(dimension_semantics=sem, vmem_limit_bytes=VMEM_LIMIT)


def _dot(a, b):
    return jnp.dot(a, b, preferred_element_type=F32)


def _dot_nt(a, b):
    return lax.dot_general(a, b, (((1,), (1,)), ((), ())), preferred_element_type=F32)


def _sigmoid(x):
    return 1.0 / (1.0 + jnp.exp(-x))


def _silu(x):
    return x * _sigmoid(x)


def _rms(x, g):
    return x * lax.rsqrt(jnp.mean(x * x, axis=-1, keepdims=True) + EPS) * g


IN_ROW_WIDTHS = (NSA_WIDTH, KV_WIDTH, KV_WIDTH, MLSTM_WIDTH, MLSTM_WIDTH, MLSTM_WIDTH, LANES,
                 KV_WIDTH, KV_WIDTH)
IN_ROW_DTYPES = (F32,) * 7 + (BF16,) * 2
N_KV_BRANCH = 3


def _inproj_body(x_ref, g_ref, w_ref, wt_ref, *out_refs):
    xb = _rms(x_ref[...], g_ref[...]).astype(BF16)
    off = 0
    n_rows = len(IN_ROW_WIDTHS)
    for ref in out_refs[:n_rows]:
        n = ref.shape[-1]
        ref[...] = _dot(xb, w_ref[:, off:off + n]).astype(ref.dtype)
        off += n
    kv_refs = out_refs[n_rows:n_rows + N_KV_BRANCH]
    vt_refs = out_refs[n_rows + N_KV_BRANCH:]
    for n, ref in enumerate(kv_refs):
        kv_t = _dot_nt(wt_ref[n * 2 * KV_WIDTH:(n + 1) * 2 * KV_WIDTH, :], xb)
        ref[0] = kv_t
        if n > 0:
            vt_refs[n - 1][0] = kv_t[KV_WIDTH:, :].astype(BF16)


def _pack_w_in(w_in):
    splits = np.cumsum([NSA_WIDTH, 2 * KV_WIDTH, 2 * KV_WIDTH, 2 * KV_WIDTH, NSA_HEADS * N_BRANCH,
                        MLSTM_WIDTH, MLSTM_WIDTH, MLSTM_WIDTH, MLSTM_HEADS]).tolist()
    q, kvc, kvs, kvw, gt, mu, mv, mo, mi, mf = jnp.split(w_in, splits, axis=1)
    gates = jnp.concatenate([gt, mi, mf], axis=1)
    gates = jnp.pad(gates, ((0, 0), (0, LANES - gates.shape[1])))
    w_rows = jnp.concatenate([q, kvc, mu, mv, mo, gates, kvs[:, :KV_WIDTH], kvw[:, :KV_WIDTH]],
                             axis=1).astype(BF16)
    w_kv_t = jnp.concatenate([kvc, kvs, kvw], axis=1).T.astype(BF16)
    return w_rows, w_kv_t


def _in_proj(x2d, g_mix, w_packed, *, batch, seq, tm):
    w_rows, w_kv_t = w_packed
    t = x2d.shape[0]
    ns = seq // tm
    kv_sd = jax.ShapeDtypeStruct((batch, 2 * KV_WIDTH, seq), F32)
    vt_sd = jax.ShapeDtypeStruct((batch, KV_WIDTH, seq), BF16)
    feat_major = lambda rows: pl.BlockSpec((1, rows, tm), lambda i: (i // ns, 0, i % ns))
    return pl.pallas_call(
        _inproj_body,
        grid=(t // tm,),
        in_specs=[pl.BlockSpec((tm, D_MODEL), lambda i: (i, 0)),
                  pl.BlockSpec((1, D_MODEL), lambda i: (0, 0)),
                  pl.BlockSpec(w_rows.shape, lambda i: (0, 0)),
                  pl.BlockSpec(w_kv_t.shape, lambda i: (0, 0))],
        out_specs=[pl.BlockSpec((tm, n), lambda i: (i, 0)) for n in IN_ROW_WIDTHS]
        + [feat_major(2 * KV_WIDTH)] * N_KV_BRANCH + [feat_major(KV_WIDTH)] * (N_KV_BRANCH - 1),
        out_shape=[jax.ShapeDtypeStruct((t, n), dt) for n, dt in zip(IN_ROW_WIDTHS, IN_ROW_DTYPES)]
        + [kv_sd] * N_KV_BRANCH + [vt_sd] * (N_KV_BRANCH - 1),
        compiler_params=_cparams(("arbitrary",)),
        name="in_proj",
    )(x2d, g_mix.reshape(1, D_MODEL), w_rows, w_kv_t)


def _mlstm_body(*refs, valid, bb):
    cb_ref, c0_ref, n0_ref, m0_ref = refs[4:8]
    cn_ref, c_ref, n_ref, m_ref, xx_ref = refs[16:21]
    halo = SUBLANES

    @pl.when(pl.program_id(1) == 0)
    def _():
        xx_ref[:, 0:halo, :] = jnp.zeros((bb, halo, MLSTM_WIDTH), F32)
        xx_ref[:, halo - (MLSTM_CONV - 1):halo, :] = cb_ref[...]
        c_ref[...] = c0_ref[...]
        n_ref[...] = n0_ref[...]
        m_ref[...] = m0_ref[...]

    _mlstm_chunk(*refs, valid=valid, bb=bb)


def _mlstm_chunk(mu_ref, mv_ref, mo_ref, g_ref, cb_ref, c0_ref, n0_ref, m0_ref,
                 wc_ref, bc_ref, wq_ref, wk_ref, gb_ref, gh_ref, sk_ref,
                 o_ref, cn_ref, c_ref, n_ref, m_ref,
                 xx_ref, vpad_ref, gpad_ref, *, valid, bb):
    L = MLSTM_CHUNK
    DH = MLSTM_DH
    halo = SUBLANES
    units = [(bi, h) for bi in range(bb) for h in range(MLSTM_HEADS)]
    head_lanes = lambda h: slice(h * DH, (h + 1) * DH)
    row = lax.broadcasted_iota(jnp.int32, (L, L), 0)
    col = lax.broadcasted_iota(jnp.int32, (L, L), 1)
    tril = row >= col
    triu = row <= col
    tok_col = lax.broadcasted_iota(jnp.int32, (L, 1), 0)
    tok_row = lax.broadcasted_iota(jnp.int32, (1, L), 1)

    def log_sigmoid(x):
        return jnp.minimum(x, 0.0) - jnp.log(1.0 + jnp.exp(-jnp.abs(x)))

    uc, gb, gbt = {}, {}, {}
    for bi in range(bb):
        if valid < L:
            xx_ref[bi, halo:, :] = jnp.zeros((L, MLSTM_WIDTH), F32)
            vpad_ref[bi] = jnp.zeros((L, MLSTM_WIDTH), F32)
            gpad_ref[bi] = jnp.zeros((L, LANES), F32)
        xx_ref[bi, halo:halo + valid, :] = mu_ref[bi]
        vpad_ref[bi, 0:valid, :] = mv_ref[bi]
        gpad_ref[bi, 0:valid, :] = g_ref[bi]
        conv = xx_ref[bi, halo - 3:halo - 3 + L, :] * wc_ref[0:1, :]
        for j in range(1, MLSTM_CONV):
            conv = conv + xx_ref[bi, halo - 3 + j:halo - 3 + j + L, :] * wc_ref[j:j + 1, :]
        uc[bi] = _silu(conv + bc_ref[...])
        tail = xx_ref[bi, valid + halo - 3:valid + halo, :]
        xx_ref[bi, halo - 3:halo, :] = tail
        cn_ref[bi] = tail
        gb[bi] = gpad_ref[bi] + gb_ref[...]
        gbt[bi] = gb[bi].T

    q, k, qb, kb = {}, {}, {}, {}
    for u in units:
        bi, h = u
        ub = uc[bi][:, head_lanes(h)].astype(BF16)
        q[u] = _dot(ub, wq_ref[h])
        k[u] = _dot(ub, wk_ref[h]) * (DH ** -0.5)
        qb[u], kb[u] = q[u].astype(BF16), k[u].astype(BF16)

    ic_col, ic_row, cum_col, cum_row = {}, {}, {}, {}
    for u in units:
        bi, h = u
        ic_c = gb[bi][:, GATE_COL_I + h:GATE_COL_I + h + 1]
        ic_r = gbt[bi][GATE_COL_I + h:GATE_COL_I + h + 1, :]
        lf_c = log_sigmoid(gb[bi][:, GATE_COL_F + h:GATE_COL_F + h + 1])
        lf_r = log_sigmoid(gbt[bi][GATE_COL_F + h:GATE_COL_F + h + 1, :])
        if valid < L:
            ic_c = jnp.where(tok_col < valid, ic_c, NEG_INF)
            ic_r = jnp.where(tok_row < valid, ic_r, NEG_INF)
            lf_c = jnp.where(tok_col < valid, lf_c, 0.0)
            lf_r = jnp.where(tok_row < valid, lf_r, 0.0)
        ic_col[u], ic_row[u] = ic_c, ic_r
        cum_col[u] = jnp.sum(jnp.where(tril, lf_r, 0.0), axis=1, keepdims=True)
        cum_row[u] = jnp.sum(jnp.where(triu, lf_c, 0.0), axis=0, keepdims=True)

    m_t, w, sc = {}, {}, {}
    for u in units:
        bi, h = u
        m0 = m_ref[bi, 0:1, h:h + 1]
        dmat = jnp.where(tril, cum_col[u] - cum_row[u] + ic_row[u], NEG_INF)
        inter = cum_col[u] + m0
        m_t[u] = jnp.maximum(inter, jnp.max(dmat, axis=1, keepdims=True))
        w[u] = jnp.exp(dmat - m_t[u])
        sc[u] = jnp.exp(inter - m_t[u])

    hc = {}
    for u in units:
        bi, h = u
        s = _dot_nt(qb[u], kb[u]) * w[u]
        v = vpad_ref[bi, :, head_lanes(h)]
        c_old = c_ref[bi, h]
        n_old = n_ref[bi, h:h + 1, :]
        num = _dot(s.astype(BF16), v.astype(BF16)) + sc[u] * _dot_nt(qb[u], c_old.astype(BF16))
        den = (jnp.sum(s, axis=1, keepdims=True)
               + sc[u] * jnp.sum(q[u] * n_old, axis=1, keepdims=True))
        hc[u] = num / jnp.maximum(jnp.abs(den), jnp.exp(-m_t[u]))

    for u in units:
        bi, h = u
        m0 = m_ref[bi, 0:1, h:h + 1]
        m_new = m_t[u][L - 1:L, :]
        cum_last = cum_col[u][L - 1:L, :]
        wl = jnp.exp(cum_last - cum_col[u] + ic_col[u] - m_new)
        sl = jnp.exp(cum_last + m0 - m_new)
        v = vpad_ref[bi, :, head_lanes(h)]
        vw_t = (v * wl).T.astype(BF16)
        c_ref[bi, h] = sl * c_ref[bi, h] + _dot(vw_t, kb[u])
        n_ref[bi, h:h + 1, :] = sl * n_ref[bi, h:h + 1, :] + jnp.sum(wl * k[u], axis=0, keepdims=True)
        m_ref[bi, 0:1, h:h + 1] = m_new

    for u in units:
        bi, h = u
        hn = _rms(hc[u], gh_ref[:, head_lanes(h)])
        u_h = uc[bi][:, head_lanes(h)]
        out = ((hn[0:valid, :] + sk_ref[:, head_lanes(h)] * u_h[0:valid, :])
               * _sigmoid(mo_ref[bi, :, head_lanes(h)]))
        o_ref[bi, :, head_lanes(h)] = out


def _mlstm(mu, mv, mo, gates, conv_buf, c0, n0, m0, w_mconv, b_mconv, w_mq, w_mk, b_ig, b_fg,
           g_mhead, m_skip, *, batch, seq):
    L = MLSTM_CHUNK
    valid = min(seq, L)
    assert seq % valid == 0 and (valid == L or seq == valid)
    nc = seq // valid
    gate_bias = jnp.zeros((1, LANES), F32)
    gate_bias = gate_bias.at[0, GATE_COL_I:GATE_COL_I + MLSTM_HEADS].set(b_ig)
    gate_bias = gate_bias.at[0, GATE_COL_F:GATE_COL_F + MLSTM_HEADS].set(b_fg)
    bb = MLSTM_SEQS_PER_STEP
    assert batch % bb == 0
    tok = lambda b, c: (b, c, 0)
    const2 = lambda b, c: (0, 0)
    const3 = lambda b, c: (0, 0, 0)
    per_b3 = lambda b, c: (b, 0, 0)
    per_b4 = lambda b, c: (b, 0, 0, 0)
    H, DH, W = MLSTM_HEADS, MLSTM_DH, MLSTM_WIDTH
    rows3 = lambda a: a.reshape(batch, seq, a.shape[-1])
    o_m, conv_new, c_new, n_new, m_new = pl.pallas_call(
        functools.partial(_mlstm_body, valid=valid, bb=bb),
        grid=(batch // bb, nc),
        in_specs=[pl.BlockSpec((bb, valid, W), tok), pl.BlockSpec((bb, valid, W), tok),
                  pl.BlockSpec((bb, valid, W), tok), pl.BlockSpec((bb, valid, LANES), tok),
                  pl.BlockSpec((bb, MLSTM_CONV - 1, W), per_b3),
                  pl.BlockSpec((bb, H, DH, DH), per_b4),
                  pl.BlockSpec((bb, H, DH), per_b3),
                  pl.BlockSpec((bb, 1, H), per_b3),
                  pl.BlockSpec((MLSTM_CONV, W), const2), pl.BlockSpec((1, W), const2),
                  pl.BlockSpec((H, DH, DH), const3), pl.BlockSpec((H, DH, DH), const3),
                  pl.BlockSpec((1, LANES), const2), pl.BlockSpec((1, W), const2),
                  pl.BlockSpec((1, W), const2)],
        out_specs=[pl.BlockSpec((bb, valid, W), tok),
                   pl.BlockSpec((bb, MLSTM_CONV - 1, W), per_b3),
                   pl.BlockSpec((bb, H, DH, DH), per_b4),
                   pl.BlockSpec((bb, H, DH), per_b3),
                   pl.BlockSpec((bb, 1, H), per_b3)],
        out_shape=[jax.ShapeDtypeStruct((batch, seq, W), F32),
                   jax.ShapeDtypeStruct((batch, MLSTM_CONV - 1, W), F32),
                   jax.ShapeDtypeStruct((batch, H, DH, DH), F32),
                   jax.ShapeDtypeStruct((batch, H, DH), F32),
                   jax.ShapeDtypeStruct((batch, 1, H), F32)],
        scratch_shapes=[pltpu.VMEM((bb, SUBLANES + L, W), F32), pltpu.VMEM((bb, L, W), F32),
                        pltpu.VMEM((bb, L, LANES), F32)],
        compiler_params=_cparams(("arbitrary", "arbitrary")),
        name="mlstm",
    )(rows3(mu), rows3(mv), rows3(mo), rows3(gates), conv_buf, c0, n0, m0.reshape(batch, 1, H),
      w_mconv, b_mconv.reshape(1, W), w_mq.astype(BF16), w_mk.astype(BF16), gate_bias,
      g_mhead.reshape(1, W), m_skip.reshape(1, W))
    return o_m.reshape(batch * seq, W), conv_new, c_new, n_new, m_new


def _compress_rows(xk_ref, xv_ref, pe_ref, w1_ref, w2_ref, n_pairs):
    pair_rows = 2 * CMP_BLOCK
    outs = []
    for kv, x_ref in enumerate((xk_ref, xv_ref)):
        acc = jnp.zeros((2 * n_pairs, NSA_KV_HEADS * CMP_HIDDEN), F32)
        for r in range(CMP_BLOCK):
            ev = x_ref[pl.ds(r, n_pairs, stride=pair_rows), :]
            od = x_ref[pl.ds(CMP_BLOCK + r, n_pairs, stride=pair_rows), :]
            xr = jnp.concatenate([ev, od], axis=0) + pe_ref[kv, r:r + 1, :]
            acc = acc + _dot(xr.astype(BF16), w1_ref[kv, r])
        outs.append(_dot(_silu(acc).astype(BF16), w2_ref[kv]))
    return jnp.concatenate(outs, axis=1)


def _compress_body(xk_ref, xv_ref, pe_ref, w1_ref, w2_ref, oe_ref, oo_ref, *, n_pairs):
    out = _compress_rows(xk_ref, xv_ref, pe_ref, w1_ref, w2_ref, n_pairs)
    oe_ref[0] = out[0:n_pairs, :]
    oo_ref[0] = out[n_pairs:, :]


BLOCKS_PER_PAGE = PAGE_SIZE // CMP_BLOCK


def _gather_pages(pt_ref, pool_hbm, pages_ref, sem_ref, n_pages):
    g = pl.program_id(0) * pl.num_programs(1) + pl.program_id(1)
    n_total = pl.num_programs(0) * pl.num_programs(1)

    def copies(step, slot):
        return [pltpu.make_async_copy(pool_hbm.at[pt_ref[step * n_pages + j]], pages_ref.at[slot, j],
                                      sem_ref.at[slot]) for j in range(n_pages)]

    @pl.when(g == 0)
    def _():
        for cp in copies(0, 0):
            cp.start()

    @pl.when(g + 1 < n_total)
    def _():
        for cp in copies(g + 1, (g + 1) % 2):
            cp.start()

    slot = g % 2
    for cp in copies(g, slot):
        cp.wait()
    return slot


def _compress_paged_body(pt_ref, pool_hbm, pet_ref, perm_ref, w1_ref, w2_ref, oe_ref, oo_ref,
                         buf_ref, os_ref, pages_ref, sem_ref, *, n_pages):
    slot = _gather_pages(pt_ref, pool_hbm, pages_ref, sem_ref, n_pages)
    grp = 2 * BLOCKS_PER_PAGE
    for jp in range(n_pages // 2):
        xt = jnp.concatenate([pages_ref[slot, 2 * jp], pages_ref[slot, 2 * jp + 1]], axis=1)
        xb = (xt + pet_ref[...]).astype(BF16)
        xp = _dot_nt(perm_ref[...], xb)
        for r in range(CMP_BLOCK):
            for kv in range(2):
                lane0 = (2 * kv + r % 2) * KV_WIDTH
                buf_ref[r // 2, grp * jp:grp * (jp + 1), lane0:lane0 + KV_WIDTH] = (
                    xp[grp * r:grp * (r + 1), kv * KV_WIDTH:(kv + 1) * KV_WIDTH])
    for kv in range(2):
        lanes = slice(2 * kv * KV_WIDTH, 2 * (kv + 1) * KV_WIDTH)
        acc = _dot(buf_ref[0, :, lanes].astype(BF16), w1_ref[kv, 0])
        for r2 in range(1, CMP_BLOCK // 2):
            acc = acc + _dot(buf_ref[r2, :, lanes].astype(BF16), w1_ref[kv, r2])
        os_ref[kv] = _dot(_silu(acc).astype(BF16), w2_ref[kv])
    half = os_ref.shape[1] // 2
    for parity, ref in enumerate((oe_ref, oo_ref)):
        ref[0] = jnp.concatenate([os_ref[kv, pl.ds(parity, half, stride=2), :] for kv in range(2)],
                                 axis=1)


def _page_pair_constants(pe):
    pe_t = jnp.broadcast_to(pe.transpose(0, 2, 1)[:, None, :, None, :],
                            (2, NSA_KV_HEADS, HEAD_DIM, 2 * BLOCKS_PER_PAGE, CMP_BLOCK))
    pe_t = pe_t.reshape(2 * KV_WIDTH, 2 * PAGE_SIZE)
    grp = 2 * BLOCKS_PER_PAGE
    perm = np.zeros((2 * PAGE_SIZE, 2 * PAGE_SIZE), np.float32)
    for r in range(CMP_BLOCK):
        for b in range(grp):
            perm[r * grp + b, b * CMP_BLOCK + r] = 1.0
    return pe_t, jnp.asarray(perm, BF16)


def _pack_compress_weights(pe, w1, w2):
    eye_h = jnp.eye(NSA_KV_HEADS, dtype=F32)
    pe_r = jnp.broadcast_to(pe[:, :, None, :], (2, CMP_BLOCK, NSA_KV_HEADS, HEAD_DIM))
    pe_r = pe_r.reshape(2, CMP_BLOCK, KV_WIDTH)
    w1r = w1.reshape(2, CMP_BLOCK, HEAD_DIM, CMP_HIDDEN)
    w1_big = jnp.einsum('krdc,hH->krhdHc', w1r, eye_h)
    w1_big = w1_big.reshape(2, CMP_BLOCK, KV_WIDTH, NSA_KV_HEADS * CMP_HIDDEN).astype(BF16)
    w2_big = jnp.einsum('kcd,hH->khcHd', w2, eye_h)
    w2_big = w2_big.reshape(2, NSA_KV_HEADS * CMP_HIDDEN, KV_WIDTH).astype(BF16)
    return pe_r, w1_big, w2_big


def _compress_prompt(k_rows, v_rows, cw, *, batch, seq):
    pe_r, w1_big, w2_big = cw
    n_pairs = seq // (2 * CMP_BLOCK)
    const3 = lambda b: (0, 0, 0)
    out_sd = jax.ShapeDtypeStruct((batch, n_pairs, 2 * KV_WIDTH), F32)
    return pl.pallas_call(
        functools.partial(_compress_body, n_pairs=n_pairs),
        grid=(batch,),
        in_specs=[pl.BlockSpec((seq, KV_WIDTH), lambda b: (b, 0)),
                  pl.BlockSpec((seq, KV_WIDTH), lambda b: (b, 0)),
                  pl.BlockSpec(pe_r.shape, const3),
                  pl.BlockSpec(w1_big.shape, lambda b: (0, 0, 0, 0)),
                  pl.BlockSpec(w2_big.shape, const3)],
        out_specs=[pl.BlockSpec((1, n_pairs, 2 * KV_WIDTH), lambda b: (b, 0, 0))] * 2,
        out_shape=[out_sd, out_sd],
        compiler_params=_cparams(("arbitrary",)),
        name="compress_prompt",
    )(k_rows, v_rows, pe_r, w1_big, w2_big)


COMPRESS_PAGES_PER_STEP = 64


def _compress_paged(pool, page_table, cw, cw_pages):
    _, w1_big, w2_big = cw
    w1_big = w1_big.reshape(2, CMP_BLOCK // 2, 2 * KV_WIDTH, NSA_KV_HEADS * CMP_HIDDEN)
    pe_t, perm = cw_pages
    batch, n_pages = page_table.shape
    pps = COMPRESS_PAGES_PER_STEP
    assert n_pages % pps == 0 and pps % 2 == 0
    n_steps = n_pages // pps
    n_blk = pps * BLOCKS_PER_PAGE
    const3 = lambda b, c, pt: (0, 0, 0)
    return pl.pallas_call(
        functools.partial(_compress_paged_body, n_pages=pps),
        grid_spec=pltpu.PrefetchScalarGridSpec(
            num_scalar_prefetch=1,
            grid=(batch, n_steps),
            in_specs=[pl.BlockSpec(memory_space=pl.ANY),
                      pl.BlockSpec(pe_t.shape, lambda b, c, pt: (0, 0)),
                      pl.BlockSpec(perm.shape, lambda b, c, pt: (0, 0)),
                      pl.BlockSpec(w1_big.shape, lambda b, c, pt: (0, 0, 0, 0)),
                      pl.BlockSpec(w2_big.shape, const3)],
            out_specs=[pl.BlockSpec((1, n_blk // 2, 2 * KV_WIDTH), lambda b, c, pt: (b, c, 0))] * 2,
            scratch_shapes=[pltpu.VMEM((CMP_BLOCK // 2, n_blk, 4 * KV_WIDTH), F32),
                            pltpu.VMEM((2, n_blk, KV_WIDTH), F32),
                            pltpu.VMEM((2, pps, 2 * KV_WIDTH, PAGE_SIZE), F32),
                            pltpu.SemaphoreType.DMA((2,))]),
        out_shape=[jax.ShapeDtypeStruct((batch, n_steps * n_blk // 2, 2 * KV_WIDTH), F32)] * 2,
        compiler_params=_cparams(("arbitrary", "arbitrary")),
        name="compress_paged",
    )(page_table.reshape(-1), pool, pe_t, perm, w1_big, w2_big)


def _cmp_attn_body(q_ref, ke_ref, ko_ref, o_ref, st_ref, *, tq, pos0):
    ns = ke_ref.shape[1]
    i = pl.program_id(1)
    rows = NSA_GROUP * tq
    tok0 = pos0 + i * tq
    pos_c = tok0 + lax.broadcasted_iota(jnp.int32, (rows, 1), 0) % tq
    pos_r = tok0 + lax.broadcasted_iota(jnp.int32, (1, rows), 1) % tq
    pair_r = lax.broadcasted_iota(jnp.int32, (1, ns), 1)
    pair_c = lax.broadcasted_iota(jnp.int32, (ns, 1), 0)
    end_e = lambda pair: (2 * pair + 1) * CMP_BLOCK - 1
    end_o = lambda pair: (2 * pair + 2) * CMP_BLOCK - 1
    any_c = (CMP_BLOCK - 1 <= pos_c).astype(F32)
    any_r = (CMP_BLOCK - 1 <= pos_r).astype(F32)
    q = q_ref[...] * ATTN_SCALE
    for kh in range(NSA_KV_HEADS):
        qs = jnp.concatenate([q[:, (kh * NSA_GROUP + g) * HEAD_DIM:(kh * NSA_GROUP + g + 1) * HEAD_DIM]
                              for g in range(NSA_GROUP)], axis=0).astype(BF16)
        ks, vs = slice(kh * HEAD_DIM, (kh + 1) * HEAD_DIM), slice(KV_WIDTH + kh * HEAD_DIM,
                                                                   KV_WIDTH + (kh + 1) * HEAD_DIM)
        ke, ko = ke_ref[0, :, ks].astype(BF16), ko_ref[0, :, ks].astype(BF16)
        se = jnp.where(end_e(pair_r) <= pos_c, _dot_nt(qs, ke), NEG_INF)
        so = jnp.where(end_o(pair_r) <= pos_c, _dot_nt(qs, ko), NEG_INF)
        mx = jnp.maximum(jnp.max(se, axis=1, keepdims=True), jnp.max(so, axis=1, keepdims=True))
        pe, po = jnp.exp(se - mx), jnp.exp(so - mx)
        inv = any_c / (jnp.sum(pe, axis=1, keepdims=True) + jnp.sum(po, axis=1, keepdims=True))
        oh = (_dot((pe * inv).astype(BF16), ke_ref[0, :, vs].astype(BF16))
              + _dot((po * inv).astype(BF16), ko_ref[0, :, vs].astype(BF16)))
        for g in range(NSA_GROUP):
            hd = kh * NSA_GROUP + g
            o_ref[:, hd * HEAD_DIM:(hd + 1) * HEAD_DIM] = oh[g * tq:(g + 1) * tq, :]
        te = jnp.where(end_e(pair_c) <= pos_r, _dot_nt(ke, qs), NEG_INF)
        to = jnp.where(end_o(pair_c) <= pos_r, _dot_nt(ko, qs), NEG_INF)
        mt = jnp.maximum(jnp.max(te, axis=0, keepdims=True), jnp.max(to, axis=0, keepdims=True))
        pte, pto = jnp.exp(te - mt), jnp.exp(to - mt)
        invt = any_r / (jnp.sum(pte, axis=0, keepdims=True) + jnp.sum(pto, axis=0, keepdims=True))
        ps = (pte + pto) * invt
        score = ps[:, 0:tq]
        for g in range(1, NSA_GROUP):
            score = score + ps[:, g * tq:(g + 1) * tq]
        st_ref[0, kh] = score


def _cmp_attn(q2d, kce, kco, *, batch, seq, tq, pos0):
    ns = kce.shape[1]
    nq = seq // tq
    return pl.pallas_call(
        functools.partial(_cmp_attn_body, tq=tq, pos0=pos0),
        grid=(batch, nq),
        in_specs=[pl.BlockSpec((tq, NSA_WIDTH), lambda b, i: (b * nq + i, 0)),
                  pl.BlockSpec((1, ns, 2 * KV_WIDTH), lambda b, i: (b, 0, 0)),
                  pl.BlockSpec((1, ns, 2 * KV_WIDTH), lambda b, i: (b, 0, 0))],
        out_specs=[pl.BlockSpec((tq, NSA_WIDTH), lambda b, i: (b * nq + i, 0)),
                   pl.BlockSpec((1, NSA_KV_HEADS, ns, tq), lambda b, i: (b, 0, 0, i))],
        out_shape=[jax.ShapeDtypeStruct((batch * seq, NSA_WIDTH), F32),
                   jax.ShapeDtypeStruct((batch, NSA_KV_HEADS, ns, seq), F32)],
        compiler_params=_cparams(("arbitrary", "arbitrary")),
        name="cmp_attn",
    )(q2d, kce, kco)


def _topk_body(pos_ref, st_ref, b_ref, *, n_sel):
    score = st_ref[0]
    ns, tt = score.shape
    nsw = b_ref.shape[1]
    if nsw > ns:
        score = jnp.concatenate([score, jnp.zeros((nsw - ns, tt), F32)], axis=0)
    blk = lax.broadcasted_iota(jnp.int32, (nsw, 1), 0)
    blk_f = blk.astype(F32)
    cur = pos_ref[...] // SEL_BLOCK
    forced = (blk == 0) | (blk == cur) | (blk == cur - 1)
    pri = jnp.where(blk <= cur, jnp.where(forced, SEL_PRIORITY, score), -SEL_PRIORITY)
    pri = jnp.where(blk < n_sel, pri, -jnp.inf)
    bias = jnp.full((nsw, tt), NEG_INF, F32)
    for _ in range(min(TOP_N, n_sel)):
        top = jnp.max(pri, axis=0, keepdims=True)
        first = jnp.min(jnp.where(pri == top, blk_f, float(nsw)), axis=0, keepdims=True)
        hit = blk_f == first
        bias = jnp.where(hit, 0.0, bias)
        pri = jnp.where(hit, -jnp.inf, pri)
    b_ref[0] = bias


def _topk_blocks(scores_t, pos, *, n_sel, nsw, tt):
    groups, ns, tokens = scores_t.shape
    assert nsw >= max(ns, n_sel) and tokens % tt == 0
    return pl.pallas_call(
        functools.partial(_topk_body, n_sel=n_sel),
        grid=(groups, tokens // tt),
        in_specs=[pl.BlockSpec((1, tt), lambda g, i: (0, i)),
                  pl.BlockSpec((1, ns, tt), lambda g, i: (g, 0, i))],
        out_specs=pl.BlockSpec((1, nsw, tt), lambda g, i: (g, 0, i)),
        out_shape=jax.ShapeDtypeStruct((groups, nsw, tokens), F32),
        compiler_params=_cparams(("arbitrary", "arbitrary")),
        name="topk_blocks",
    )(pos, scores_t)


def _softmax_update(sc, vt_bf16, m_ref, l_ref, acc_ref):
    m_old = m_ref[...]
    m_new = jnp.maximum(m_old, jnp.max(sc, axis=1, keepdims=True))
    alpha = jnp.exp(m_old - m_new)
    pr = jnp.exp(sc - jnp.concatenate([m_new] * (sc.shape[1] // LANES), axis=1))
    l_ref[...] = alpha * l_ref[...] + jnp.sum(pr, axis=1, keepdims=True)
    acc_ref[...] = alpha * acc_ref[...] + _dot_nt(pr.astype(BF16), vt_bf16)
    m_ref[...] = m_new


def _softmax_init(m_ref, l_ref, acc_ref):
    m_ref[...] = jnp.full(m_ref.shape, NEG_INF, F32)
    l_ref[...] = jnp.zeros(l_ref.shape, F32)
    acc_ref[...] = jnp.zeros(acc_ref.shape, F32)


def _block_onehot_t(first_key, n_keys):
    blk = (first_key + lax.broadcasted_iota(jnp.int32, (1, n_keys), 1)) // SEL_BLOCK
    r = lax.broadcasted_iota(jnp.int32, (LANES, 1), 0) & (SEL_BLOCK - 1)
    return (r == blk).astype(F32)


ATTN_TAB_COLS = 5
ATTN_COL_BLOCK = 256


def _attn_pairs(seq, tq, tk, window):
    rows = []
    for i in range(seq // tq):
        t_lo, t_hi = i * tq, i * tq + tq - 1
        k_lo = 0 if window is None else max(0, t_lo - window + 1)
        js = list(range(k_lo // tk, t_hi // tk + 1))
        for n, j in enumerate(js):
            partial_tile = j * tk + tk - 1 > t_lo or (window is not None and j * tk <= t_hi - window)
            rows.append((i, j, int(n == 0), int(n == len(js) - 1), int(partial_tile)))
    return np.asarray(rows, np.int32)


def _attn_body(tab_ref, q_ref, k_ref, vt_ref, *rest, tq, tk, window, use_bias):
    if use_bias:
        oh_ref, sb_ref, o_ref, qa_ref, m_ref, l_ref, acc_ref = rest
    else:
        o_ref, qa_ref, m_ref, l_ref, acc_ref = rest
    p = pl.program_id(1)
    i, j, first, last, partial_tile = [tab_ref[ATTN_TAB_COLS * p + n] for n in range(ATTN_TAB_COLS)]
    G = NSA_GROUP
    cols = NSA_HEADS * tq
    zeros64 = jnp.zeros((HEAD_DIM, tq), F32)

    def kv_head_rows(x, kh):
        return jnp.concatenate([x, zeros64] if kh == 0 else [zeros64, x], axis=0)

    @pl.when(first == 1)
    def _():
        q = q_ref[0] * (ATTN_SCALE * LOG2_E)
        for m in range(NSA_HEADS // 2):
            q_t = q[:, m * LANES:(m + 1) * LANES].T
            for hd in (2 * m, 2 * m + 1):
                kh = hd // G
                piece = kv_head_rows(q_t[(hd % 2) * HEAD_DIM:(hd % 2 + 1) * HEAD_DIM, :], kh)
                if use_bias:
                    piece = jnp.concatenate([piece, kv_head_rows(sb_ref[0, kh], kh)], axis=0)
                qa_ref[:, hd * tq:(hd + 1) * tq] = piece.astype(BF16)
        m_ref[...] = jnp.full(m_ref.shape, NEG_INF, F32)
        l_ref[...] = jnp.zeros(l_ref.shape, F32)
        acc_ref[...] = jnp.zeros(acc_ref.shape, F32)

    k_aug = k_ref[...]
    if use_bias:
        k_aug = jnp.concatenate([k_aug, oh_ref[...]], axis=1)
    sc = _dot(k_aug, qa_ref[...])
    vt = vt_ref[0]

    def update(sc):
        m_old = m_ref[...]
        m_new = jnp.maximum(m_old, jnp.max(sc, axis=0, keepdims=True))
        alpha = jnp.exp2(m_old - m_new)
        pr = jnp.exp2(sc - m_new)
        l_ref[...] = alpha * l_ref[...] + jnp.sum(pr, axis=0, keepdims=True)
        acc_ref[...] = alpha * acc_ref[...] + _dot(vt, pr.astype(BF16))
        m_ref[...] = m_new

    @pl.when(partial_tile == 1)
    def _():
        qpos = i * tq + (lax.broadcasted_iota(jnp.int32, (1, cols), 1) & (tq - 1))
        kpos = j * tk + lax.broadcasted_iota(jnp.int32, (tk, 1), 0)
        valid = kpos <= qpos
        if window is not None:
            valid = valid & (kpos > qpos - window)
        update(jnp.where(valid, sc, NEG_INF))

    @pl.when(partial_tile == 0)
    def _():
        update(sc)

    @pl.when(last == 1)
    def _():
        o_t = acc_ref[...] / l_ref[...]
        for m in range(NSA_HEADS // 2):
            pair = jnp.concatenate(
                [o_t[(hd // G) * HEAD_DIM:(hd // G + 1) * HEAD_DIM, hd * tq:(hd + 1) * tq]
                 for hd in (2 * m, 2 * m + 1)], axis=0)
            o_ref[0, :, m * LANES:(m + 1) * LANES] = pair.T


def _block_onehot(seq):
    blk = np.arange(seq)[:, None] // SEL_BLOCK
    return jnp.asarray((np.arange(LANES)[None, :] % SEL_BLOCK) == blk, BF16)


def _attn_prompt(q3d, k_rows, v_t, selb, *, tq, tk, window):
    batch, seq, _ = q3d.shape
    assert tq & (tq - 1) == 0 and tk % LANES == 0 and tq % LANES == 0
    use_bias = selb is not None
    assert not use_bias or selb.shape[2] == SEL_BLOCK
    tab = _attn_pairs(seq, tq, tk, window)
    depth = 2 * LANES if use_bias else LANES
    cols = NSA_HEADS * tq
    C = ATTN_TAB_COLS
    nk = seq // tk
    in_specs = [pl.BlockSpec((1, tq, NSA_WIDTH), lambda b, p, t: (b, t[C * p], 0)),
                pl.BlockSpec((tk, KV_WIDTH), lambda b, p, t: (b * nk + t[C * p + 1], 0)),
                pl.BlockSpec((1, KV_WIDTH, tk), lambda b, p, t: (b, 0, t[C * p + 1]))]
    args = [q3d, k_rows, v_t]
    if use_bias:
        in_specs.append(pl.BlockSpec((tk, LANES), lambda b, p, t: (t[C * p + 1], 0)))
        in_specs.append(pl.BlockSpec((1, NSA_KV_HEADS, SEL_BLOCK, tq),
                                     lambda b, p, t: (b, 0, 0, t[C * p])))
        args += [_block_onehot(seq), selb]
    return pl.pallas_call(
        functools.partial(_attn_body, tq=tq, tk=tk, window=window, use_bias=use_bias),
        grid_spec=pltpu.PrefetchScalarGridSpec(
            num_scalar_prefetch=1,
            grid=(batch, tab.shape[0]),
            in_specs=in_specs,
            out_specs=pl.BlockSpec((1, tq, NSA_WIDTH), lambda b, p, t: (b, t[C * p], 0)),
            scratch_shapes=[pltpu.VMEM((depth, cols), BF16), pltpu.VMEM((1, cols), F32),
                            pltpu.VMEM((1, cols), F32), pltpu.VMEM((KV_WIDTH, cols), F32)]),
        out_shape=jax.ShapeDtypeStruct((batch, seq, NSA_WIDTH), F32),
        compiler_params=_cparams(("arbitrary", "arbitrary")),
        name="attn_sel" if use_bias else "attn_win",
    )(jnp.asarray(tab.reshape(-1)), *args)


ATTN_PAGES_PER_STEP = 32
ATTN_PAGED_SPLIT = 2


def _attn_paged_body(pt_ref, qa_ref, bq_ref, bn_ref, kn_ref, pool_hbm, o_ref, m_ref, l_ref, acc_ref,
                     pages_ref, sem_ref, *, n_pages, n_new):
    slot = _gather_pages(pt_ref, pool_hbm, pages_ref, sem_ref, n_pages)
    page_refs = [pages_ref.at[slot, j] for j in range(n_pages)]
    c = pl.program_id(1)
    rows = qa_ref.shape[1]

    @pl.when(c == 0)
    def _():
        _softmax_init(m_ref, l_ref, acc_ref)

    n_split = m_ref.shape[0]
    per = n_pages // n_split
    keys = per * PAGE_SIZE
    qa = qa_ref[0]
    lhs = jnp.concatenate([qa, bq_ref[0, 0]], axis=1).astype(BF16)
    scs, vts = [], []
    for s in range(n_split):
        refs_s = page_refs[s * per:(s + 1) * per]
        kt = jnp.concatenate([r[0:KV_WIDTH, :] for r in refs_s], axis=1)
        rhs = jnp.concatenate([kt, _block_onehot_t(s * keys, keys)], axis=0).astype(BF16)
        scs.append(_dot(lhs, rhs))
        vts.append(jnp.concatenate([r[KV_WIDTH:, :] for r in refs_s], axis=1).astype(BF16))
    for s in range(n_split):
        _softmax_update(scs[s], vts[s], m_ref.at[s], l_ref.at[s], acc_ref.at[s])

    @pl.when(c == pl.num_programs(1) - 1)
    def _():
        kn = kn_ref[0]
        sc = _dot(qa.astype(BF16), kn[0:KV_WIDTH, :].astype(BF16)) + bn_ref[0]
        tq = lax.broadcasted_iota(jnp.int32, (rows, 1), 0) % n_new
        kk = lax.broadcasted_iota(jnp.int32, (1, kn.shape[1]), 1)
        sc = jnp.where((kk <= tq) & (kk < n_new), sc, NEG_INF)
        _softmax_update(sc, kn[KV_WIDTH:, :].astype(BF16), m_ref.at[0], l_ref.at[0], acc_ref.at[0])
        m_all = m_ref[0]
        for s in range(1, n_split):
            m_all = jnp.maximum(m_all, m_ref[s])
        l_all = jnp.zeros(m_all.shape, F32)
        acc_all = jnp.zeros(m_all.shape, F32)
        for s in range(n_split):
            scale = jnp.exp(m_ref[s] - m_all)
            l_all = l_all + scale * l_ref[s]
            acc_all = acc_all + scale * acc_ref[s]
        o_ref[0] = acc_all / l_all


def _attn_paged(qa, bias_q, bias_new, kv_new_t, pool, page_table, *, n_new):
    batch, n_pages = page_table.shape
    pps = ATTN_PAGES_PER_STEP
    assert n_pages % pps == 0 and pps * PAGE_SIZE // SEL_BLOCK <= SEL_BLOCK
    n_steps = n_pages // pps
    rows = qa.shape[1]

    per_b = lambda b, c, pt: (b, 0, 0)
    return pl.pallas_call(
        functools.partial(_attn_paged_body, n_pages=pps, n_new=n_new),
        grid_spec=pltpu.PrefetchScalarGridSpec(
            num_scalar_prefetch=1,
            grid=(batch, n_steps),
            in_specs=[pl.BlockSpec((1, rows, LANES), per_b),
                      pl.BlockSpec((1, 1, rows, LANES), lambda b, c, pt: (b, c, 0, 0)),
                      pl.BlockSpec((1, rows, LANES), per_b),
                      pl.BlockSpec((1,) + kv_new_t.shape[1:], per_b),
                      pl.BlockSpec(memory_space=pl.ANY)],
            out_specs=pl.BlockSpec((1, rows, LANES), per_b),
            scratch_shapes=[pltpu.VMEM((ATTN_PAGED_SPLIT, rows, LANES), F32)] * 3
            + [pltpu.VMEM((2, pps, 2 * KV_WIDTH, PAGE_SIZE), F32), pltpu.SemaphoreType.DMA((2,))]),
        out_shape=jax.ShapeDtypeStruct((batch, rows, LANES), F32),
        compiler_params=_cparams(("arbitrary", "arbitrary")),
        name="attn_sel_paged",
    )(page_table.reshape(-1), qa, bias_q, bias_new, kv_new_t, pool)


def _attn_window_body(qa_ref, wb_ref, kn_ref, o_ref, *, n_new, past):
    qa = qa_ref[0].astype(BF16)
    wb, kn = wb_ref[0], kn_ref[0]
    rows, n_buf = qa.shape[0], wb.shape[1]
    qpos = past + lax.broadcasted_iota(jnp.int32, (rows, 1), 0) % n_new

    def masked(sc, kpos, extra):
        diff = qpos - kpos
        return jnp.where((diff >= 0) & (diff < WINDOW) & (kpos >= 0) & extra, sc, NEG_INF)

    nb = lax.broadcasted_iota(jnp.int32, (1, n_buf), 1)
    nn = lax.broadcasted_iota(jnp.int32, (1, kn.shape[1]), 1)
    sb = masked(_dot(qa, wb[0:KV_WIDTH, :].astype(BF16)), past - n_buf + nb, nb >= 0)
    sn = masked(_dot(qa, kn[0:KV_WIDTH, :].astype(BF16)), past + nn, nn < n_new)
    mx = jnp.maximum(jnp.max(sb, axis=1, keepdims=True), jnp.max(sn, axis=1, keepdims=True))
    pb, pn = jnp.exp(sb - mx), jnp.exp(sn - mx)
    o = (_dot_nt(pb.astype(BF16), wb[KV_WIDTH:, :].astype(BF16))
         + _dot_nt(pn.astype(BF16), kn[KV_WIDTH:, :].astype(BF16)))
    o_ref[0] = o / (jnp.sum(pb, axis=1, keepdims=True) + jnp.sum(pn, axis=1, keepdims=True))


def _attn_window_small(qa, win_t, kv_new_t, *, n_new, past):
    batch, rows, _ = qa.shape
    per_b = lambda b: (b, 0, 0)
    return pl.pallas_call(
        functools.partial(_attn_window_body, n_new=n_new, past=past),
        grid=(batch,),
        in_specs=[pl.BlockSpec((1, rows, LANES), per_b),
                  pl.BlockSpec((1,) + win_t.shape[1:], per_b),
                  pl.BlockSpec((1,) + kv_new_t.shape[1:], per_b)],
        out_specs=pl.BlockSpec((1, rows, LANES), per_b),
        out_shape=jax.ShapeDtypeStruct((batch, rows, LANES), F32),
        compiler_params=_cparams(("arbitrary",)),
        name="attn_win_small",
    )(qa, win_t, kv_new_t)


FFN_TM = 512
FFN_VMEM_LIMIT = 58 * 1024 * 1024
MXU_DEPTH = 256
FFN_CHUNKS = ((0, 6 * MXU_DEPTH), (6 * MXU_DEPTH, D_FF))


def _ffn_body(x_ref, om_ref, oc_ref, os_ref, ow_ref, gt_ref, ge_ref, gn_ref, gf_ref, gl_ref, wc_ref,
              fb_ref, wo_hbm, wu_hbm, wd_hbm, y_ref, fn_ref, xx_ref, wo_ref, wu_ref, wd_ref, sem_ref,
              *, tm, stride, halo):
    s = pl.program_id(1)

    @pl.when((pl.program_id(0) == 0) & (s == 0))
    def _():
        copies = [pltpu.make_async_copy(src, dst, sem_ref.at[n])
                  for n, (src, dst) in enumerate(((wo_hbm, wo_ref), (wu_hbm, wu_ref), (wd_hbm, wd_ref)))]
        for cp in copies:
            cp.start()
        for cp in copies:
            cp.wait()

    sig = _sigmoid(gt_ref[...])
    hi = sig.astype(BF16)
    lo = (sig - hi.astype(F32)).astype(BF16)
    comb = None
    for br, ob_ref in enumerate((oc_ref, os_ref, ow_ref)):
        gate = _dot(hi, ge_ref[br]) + _dot(lo, ge_ref[br])
        term = gate * ob_ref[...]
        comb = term if comb is None else comb + term
    onsa = _rms(comb, gn_ref[...])
    h = (x_ref[...] + _dot(om_ref[...].astype(BF16), wo_ref[0:MLSTM_WIDTH, :])
         + _dot(onsa.astype(BF16), wo_ref[MLSTM_WIDTH:, :]))
    hn = _rms(h, gf_ref[...]).astype(BF16)

    base = halo - (FFN_CONV - 1) * stride

    @pl.when(s == 0)
    def _():
        xx_ref[base:halo, :] = fb_ref[0]

    y_ref[...] = h
    for lo_col, hi_col in FFN_CHUNKS:
        convs = []
        for half in range(2):
            cols = slice(half * D_FF + lo_col, half * D_FF + hi_col)
            xx_ref[halo:halo + tm, cols] = _dot(hn, wu_ref[:, cols])
            conv = xx_ref[base:base + tm, cols] * wc_ref[0:1, cols]
            for j in range(1, FFN_CONV):
                conv = conv + xx_ref[base + j * stride:base + j * stride + tm, cols] * wc_ref[j:j + 1, cols]
            convs.append(conv)
        act = _silu(convs[1]) * convs[0]
        y_ref[...] += _dot(act.astype(BF16), wd_ref[lo_col:hi_col, :])
    fn_ref[0, 0] = xx_ref[tm + base:tm + halo, :]
    xx_ref[0:halo, :] = xx_ref[tm:tm + halo, :]
    y_ref[...] = _rms(y_ref[...], gl_ref[...])


def _gate_expand():
    ge = np.zeros((N_BRANCH, LANES, NSA_WIDTH), np.float32)
    for hd in range(NSA_HEADS):
        for br in range(N_BRANCH):
            ge[br, GATE_COL_NSA + hd * N_BRANCH + br, hd * HEAD_DIM:(hd + 1) * HEAD_DIM] = 1.0
    return jnp.asarray(ge, BF16)


def _ffn(x2d, om, oc, osel, ow, gt, fbuf, w_out, g_nsa, g_ffn, g_final, w_up, w_fconv, w_down,
         *, nb, tm, stride):
    rows = x2d.shape[0]
    ns = rows // (nb * tm)
    halo = -(-(FFN_CONV - 1) * stride // SUBLANES) * SUBLANES
    assert tm >= halo and all((hi - lo) % MXU_DEPTH == 0 for lo, hi in FFN_CHUNKS)
    tok = lambda b, s: (b * ns + s, 0)
    nfb = (FFN_CONV - 1) * stride

    def const(shape):
        return pl.BlockSpec(shape, lambda b, s: (0,) * len(shape))

    hbm = pl.BlockSpec(memory_space=pl.ANY)
    y, fn = pl.pallas_call(
        functools.partial(_ffn_body, tm=tm, stride=stride, halo=halo),
        grid=(nb, ns),
        in_specs=[pl.BlockSpec((tm, D_MODEL), tok)] + [pl.BlockSpec((tm, NSA_WIDTH), tok)] * 4
        + [pl.BlockSpec((tm, LANES), tok),
           const((N_BRANCH, LANES, NSA_WIDTH)), const((1, NSA_WIDTH)), const((1, D_MODEL)),
           const((1, D_MODEL)), const((FFN_CONV, 2 * D_FF)),
           pl.BlockSpec((1, nfb, 2 * D_FF), lambda b, s: (b, 0, 0)), hbm, hbm, hbm],
        out_specs=[pl.BlockSpec((tm, D_MODEL), tok),
                   pl.BlockSpec((1, 1, nfb, 2 * D_FF), lambda b, s: (b, s, 0, 0))],
        out_shape=[jax.ShapeDtypeStruct((rows, D_MODEL), F32),
                   jax.ShapeDtypeStruct((nb, ns, nfb, 2 * D_FF), F32)],
        scratch_shapes=[pltpu.VMEM((halo + tm, 2 * D_FF), F32),
                        pltpu.VMEM((D_MODEL, D_MODEL), BF16), pltpu.VMEM((D_MODEL, 2 * D_FF), BF16),
                        pltpu.VMEM((D_FF, D_MODEL), BF16), pltpu.SemaphoreType.DMA((3,))],
        compiler_params=pltpu.CompilerParams(dimension_semantics=("arbitrary", "arbitrary"),
                                             vmem_limit_bytes=FFN_VMEM_LIMIT),
        name="outproj_ffn",
    )(x2d, om, oc, osel, ow, gt, _gate_expand(), g_nsa.reshape(1, -1), g_ffn.reshape(1, -1),
      g_final.reshape(1, -1), w_fconv, fbuf, w_out.astype(BF16), w_up.astype(BF16),
      w_down.astype(BF16))
    return y, fn[:, ns - 1]


PROMPT_TM = 512
PROMPT_TQ_CMP = 512
PROMPT_TT_TOPK = 1024
PROMPT_TQ_SEL = 512
PROMPT_TQ_WIN = 256
PROMPT_TK_SEL = 512
PROMPT_TK_WIN = 256


def _kv_rows(kv_t):
    batch, _, rows = kv_t.shape
    return kv_t.reshape(batch, 2, NSA_KV_HEADS, HEAD_DIM, rows).transpose(0, 4, 1, 2, 3)


def _kv_feature_major(kv5):
    batch, rows = kv5.shape[:2]
    return kv5.transpose(0, 2, 3, 4, 1).reshape(batch, 2 * KV_WIDTH, rows)


def _prompt_layer(x, wts):
    batch, seq, _ = x.shape
    x2d = x.reshape(batch * seq, D_MODEL)
    q, kc_rows, vc_rows, mu, mv, mo, gt, ks_rows, kw_rows, kvc_t, kvs_t, kvw_t, vs_t, vw_t = _in_proj(
        x2d, wts["g_mix"], wts["w_in_packed"], batch=batch, seq=seq, tm=min(PROMPT_TM, seq))
    H, DH, W = MLSTM_HEADS, MLSTM_DH, MLSTM_WIDTH
    o_m, mconv, c_new, n_new, m_new = _mlstm(
        mu, mv, mo, gt, jnp.zeros((batch, MLSTM_CONV - 1, W), F32), jnp.zeros((batch, H, DH, DH), F32),
        jnp.zeros((batch, H, DH), F32), jnp.zeros((batch, H), F32),
        wts["w_mconv"], wts["b_mconv"], wts["w_mq"], wts["w_mk"], wts["b_ig"], wts["b_fg"],
        wts["g_mhead"], wts["m_skip"], batch=batch, seq=seq)
    kce, kco = _compress_prompt(kc_rows, vc_rows, wts["cw"], batch=batch, seq=seq)
    n_sel = -(-seq // SEL_BLOCK)
    assert n_sel <= SEL_BLOCK
    o_cmp, scores_t = _cmp_attn(q, kce, kco, batch=batch, seq=seq, tq=min(PROMPT_TQ_CMP, seq), pos0=0)
    selb = _topk_blocks(scores_t.reshape(batch * NSA_KV_HEADS, -1, seq),
                        jnp.arange(seq, dtype=jnp.int32).reshape(1, seq),
                        n_sel=n_sel, nsw=SEL_BLOCK, tt=min(PROMPT_TT_TOPK, seq))
    selb = selb.reshape(batch, NSA_KV_HEADS, SEL_BLOCK, seq)
    q3d = q.reshape(batch, seq, NSA_WIDTH)
    o_sel = _attn_prompt(q3d, ks_rows, vs_t, selb, tq=PROMPT_TQ_SEL, tk=min(PROMPT_TK_SEL, seq),
                         window=None)
    o_win = _attn_prompt(q3d, kw_rows, vw_t, None, tq=PROMPT_TQ_WIN, tk=PROMPT_TK_WIN, window=WINDOW)
    fbuf = jnp.zeros((batch, FFN_CONV - 1, 2 * D_FF), F32)
    y, f_new = _ffn(x2d, o_m, o_cmp, o_sel.reshape(-1, NSA_WIDTH), o_win.reshape(-1, NSA_WIDTH), gt,
                    fbuf, wts["w_out"], wts["g_nsa"], wts["g_ffn"], wts["g_final"], wts["w_up"],
                    wts["w_fconv"], wts["w_down"], nb=batch, tm=min(FFN_TM, seq), stride=1)
    n_win = min(WINDOW, seq)
    return (y.reshape(batch, seq, D_MODEL), _kv_rows(kvc_t), _kv_rows(kvs_t),
            _kv_rows(kvw_t[:, :, seq - n_win:]), mconv, c_new, n_new, m_new.reshape(batch, H), f_new)


def _decode_rows(q2d, batch, seq):
    q5 = (q2d * ATTN_SCALE).reshape(batch, seq, NSA_KV_HEADS, NSA_GROUP, HEAD_DIM).transpose(0, 2, 3, 1, 4)
    eye = jnp.eye(NSA_KV_HEADS, dtype=F32)
    qa = jnp.einsum('bkgtd,kK->bkgtKd', q5, eye)
    return qa.reshape(batch, NSA_KV_HEADS * NSA_GROUP * seq, KV_WIDTH)


def _decode_rows_out(o, batch, seq):
    o6 = o.reshape(batch, NSA_KV_HEADS, NSA_GROUP, seq, NSA_KV_HEADS, HEAD_DIM)
    o5 = jnp.stack([o6[:, kh, :, :, kh, :] for kh in range(NSA_KV_HEADS)], axis=1)
    return o5.transpose(0, 3, 1, 2, 4).reshape(batch * seq, NSA_WIDTH)


def _sample_layer(x, pool_cmp, pool_sel, win_buf, m_conv, m_c, m_n, m_m, f_buf, page_table, wts):
    batch, seq, _ = x.shape
    n_pages = page_table.shape[1]
    past = n_pages * PAGE_SIZE
    assert past % SEL_BLOCK == 0 and seq <= SEL_BLOCK and seq < CMP_BLOCK
    x2d = x.reshape(batch * seq, D_MODEL)
    q, _, _, mu, mv, mo, gt, _, _, kvc_t, kvs_t, kvw_t, _, _ = _in_proj(
        x2d, wts["g_mix"], wts["w_in_packed"], batch=1, seq=batch * seq, tm=batch * seq)
    per_batch = lambda a: a.reshape(2 * KV_WIDTH, batch, seq).transpose(1, 0, 2)
    kvc_t, kvs_t, kvw_t = per_batch(kvc_t), per_batch(kvs_t), per_batch(kvw_t)
    pad_keys = lambda a: jnp.pad(a, ((0, 0), (0, 0), (0, LANES - seq)))
    H = MLSTM_HEADS
    o_m, mconv, c_new, n_new, m_new = _mlstm(
        mu, mv, mo, gt, m_conv, m_c, m_n, m_m,
        wts["w_mconv"], wts["b_mconv"], wts["w_mq"], wts["w_mk"], wts["b_ig"], wts["b_fg"],
        wts["g_mhead"], wts["m_skip"], batch=batch, seq=seq)
    pool_cmp3, pool_sel3 = _kv_feature_major(pool_cmp), _kv_feature_major(pool_sel)
    kce, kco = _compress_paged(pool_cmp3, page_table, wts["cw"], wts["cw_pages"])
    n_past_blk = past // SEL_BLOCK
    n_sel = -(-(past + seq) // SEL_BLOCK)
    o_cmp, scores_t = _cmp_attn(q, kce, kco, batch=batch, seq=seq, tq=seq, pos0=past)
    ns = scores_t.shape[2]
    nsw = ns + LANES
    scores_all = scores_t.transpose(1, 2, 0, 3).reshape(NSA_KV_HEADS, ns, batch * seq)
    pos_all = (past + jnp.arange(batch * seq, dtype=jnp.int32) % seq).reshape(1, batch * seq)
    selb = _topk_blocks(scores_all, pos_all, n_sel=n_sel, nsw=nsw, tt=batch * seq)
    selb = selb.reshape(NSA_KV_HEADS, nsw, batch, seq).transpose(2, 0, 3, 1)
    qa = _decode_rows(q, batch, seq)
    rows = qa.shape[1]
    blk_per_step = ATTN_PAGES_PER_STEP * PAGE_SIZE // SEL_BLOCK
    n_steps = n_pages // ATTN_PAGES_PER_STEP
    sb_rows = jnp.broadcast_to(selb[:, :, None], (batch, NSA_KV_HEADS, NSA_GROUP, seq, selb.shape[-1]))
    sb_rows = sb_rows.reshape(batch, rows, selb.shape[-1])
    bias_q = sb_rows[:, :, :n_past_blk].reshape(batch, rows, n_steps, blk_per_step).transpose(0, 2, 1, 3)
    bias_q = jnp.pad(bias_q, ((0, 0), (0, 0), (0, 0), (0, LANES - blk_per_step)))
    bias_new = jnp.broadcast_to(sb_rows[:, :, n_past_blk:n_past_blk + 1], (batch, rows, LANES))
    o_sel = _attn_paged(qa, bias_q, bias_new, pad_keys(kvs_t), pool_sel3, page_table, n_new=seq)
    n_buf = win_buf.shape[1]
    assert past >= n_buf
    win_t = _kv_feature_major(win_buf)
    o_win = _attn_window_small(qa, win_t, pad_keys(kvw_t), n_new=seq, past=past)
    win_new = jnp.concatenate([win_t, kvw_t], axis=2)[:, :, seq:]
    tmaj = lambda a: a.reshape(batch, seq, -1).transpose(1, 0, 2).reshape(batch * seq, -1)
    fb_t = f_buf.transpose(1, 0, 2).reshape(1, (FFN_CONV - 1) * batch, 2 * D_FF)
    y, f_new = _ffn(tmaj(x2d), tmaj(o_m), tmaj(o_cmp), tmaj(_decode_rows_out(o_sel, batch, seq)),
                    tmaj(_decode_rows_out(o_win, batch, seq)), tmaj(gt), fb_t,
                    wts["w_out"], wts["g_nsa"], wts["g_ffn"], wts["g_final"], wts["w_up"],
                    wts["w_fconv"], wts["w_down"], nb=1, tm=batch * seq, stride=batch)
    y = y.reshape(seq, batch, D_MODEL).transpose(1, 0, 2)
    f_new = f_new.reshape(FFN_CONV - 1, batch, 2 * D_FF).transpose(1, 0, 2)
    return (y, _kv_rows(kvc_t), _kv_rows(kvs_t), _kv_rows(win_new), mconv, c_new, n_new,
            m_new.reshape(batch, H), f_new)


def kernel(x_prompt, x_sample, cache_cmp, cache_sel, state_win, state_mlstm_C, state_mlstm_n,
           state_mlstm_m, state_mlstm_conv, state_ffn_conv, page_table,
           g_mix, w_in, w_out, w_mconv, b_mconv, w_mq, w_mk, b_ig, b_fg, g_mhead, m_skip,
           pe_cmp, w_cmp1, w_cmp2, g_nsa, g_ffn, w_up, w_fconv, w_down, g_final):
    assert w_in.shape[0] == 1, "one layer: the final norm is fused into the layer's FFN kernel"
    l = 0
    wts = dict(g_mix=g_mix[l], w_in_packed=_pack_w_in(w_in[l]), w_out=w_out[l], w_mconv=w_mconv[l],
               b_mconv=b_mconv[l], w_mq=w_mq[l], w_mk=w_mk[l], b_ig=b_ig[l], b_fg=b_fg[l],
               g_mhead=g_mhead[l], m_skip=m_skip[l],
               cw=_pack_compress_weights(pe_cmp[l], w_cmp1[l], w_cmp2[l]),
               cw_pages=_page_pair_constants(pe_cmp[l]),
               g_nsa=g_nsa[l], g_ffn=g_ffn[l], g_final=g_final, w_up=w_up[l], w_fconv=w_fconv[l],
               w_down=w_down[l])
    p = _prompt_layer(x_prompt, wts)
    s = _sample_layer(x_sample, cache_cmp[l], cache_sel[l], state_win[l], state_mlstm_conv[l],
                      state_mlstm_C[l], state_mlstm_n[l], state_mlstm_m[l], state_ffn_conv[l],
                      page_table, wts)
    yp, cmp_p, sel_p, win_p, mconv_p, c_p, n_p, m_p, fconv_p = p
    ys, cmp_s, sel_s, win_s, mconv_s, c_s, n_s, m_s, fconv_s = s
    st = lambda a: a[None]
    return (yp, ys, st(cmp_p), st(cmp_s), st(sel_p), st(sel_s), st(win_p), st(win_s),
            st(c_p), st(c_s), st(n_p), st(n_s), st(m_p), st(m_s), st(mconv_p), st(mconv_s),
            st(fconv_p), st(fconv_s))
```

```python
import functools

import numpy as np
import jax
import jax.numpy as jnp
from jax import lax
from jax.experimental import pallas as pl
from jax.experimental.pallas import tpu as pltpu

F32 = jnp.float32
BF16 = jnp.bfloat16

D_MODEL = 1024
PAGE_SIZE = 128
HEAD_DIM = 64
NSA_HEADS = 8
NSA_KV_HEADS = 2
NSA_GROUP = NSA_HEADS // NSA_KV_HEADS
NSA_WIDTH = NSA_HEADS * HEAD_DIM
KV_WIDTH = NSA_KV_HEADS * HEAD_DIM
CMP_BLOCK = 32
CMP_HIDDEN = 2 * HEAD_DIM
SEL_BLOCK = 64
TOP_N = 16
WINDOW = 512
N_BRANCH = 3
ATTN_SCALE = HEAD_DIM ** -0.5
MLSTM_HEADS = 4
MLSTM_WIDTH = D_MODEL - NSA_WIDTH
MLSTM_DH = MLSTM_WIDTH // MLSTM_HEADS
MLSTM_CONV = 4
D_FF = ((8 * D_MODEL // 3 + 127) // 128) * 128
FFN_CONV = 3
EPS = 1e-6
NEG_INF = -1e30
SEL_PRIORITY = 1e4
LOG2_E = 1.4426950408889634

LANES = 128
SUBLANES = 8
VMEM_LIMIT = 48 * 1024 * 1024

GATE_COL_NSA = 0
GATE_COL_I = NSA_HEADS * N_BRANCH
GATE_COL_F = GATE_COL_I + MLSTM_HEADS

MLSTM_CHUNK = 128
MLSTM_SEQS_PER_STEP = 4


def _cparams(sem):
    return pltpu.CompilerParams(dimension_semantics=sem, vmem_limit_bytes=VMEM_LIMIT)


def _dot(a, b):
    return jnp.dot(a, b, preferred_element_type=F32)


def _dot_nt(a, b):
    return lax.dot_general(a, b, (((1,), (1,)), ((), ())), preferred_element_type=F32)


def _sigmoid(x):
    return 1.0 / (1.0 + jnp.exp(-x))


def _silu(x):
    return x * _sigmoid(x)


def _rms(x, g):
    return x * lax.rsqrt(jnp.mean(x * x, axis=-1, keepdims=True) + EPS) * g


IN_ROW_WIDTHS = (NSA_WIDTH, KV_WIDTH, KV_WIDTH, MLSTM_WIDTH, MLSTM_WIDTH, MLSTM_WIDTH, LANES,
                 KV_WIDTH, KV_WIDTH)
IN_ROW_DTYPES = (F32,) * 7 + (BF16,) * 2
N_KV_BRANCH = 3


def _inproj_body(x_ref, g_ref, w_ref, wt_ref, *out_refs):
    xb = _rms(x_ref[...], g_ref[...]).astype(BF16)
    off = 0
    n_rows = len(IN_ROW_WIDTHS)
    for ref in out_refs[:n_rows]:
        n = ref.shape[-1]
        ref[...] = _dot(xb, w_ref[:, off:off + n]).astype(ref.dtype)
        off += n
    kv_refs = out_refs[n_rows:n_rows + N_KV_BRANCH]
    vt_refs = out_refs[n_rows + N_KV_BRANCH:]
    for n, ref in enumerate(kv_refs):
        kv_t = _dot_nt(wt_ref[n * 2 * KV_WIDTH:(n + 1) * 2 * KV_WIDTH, :], xb)
        ref[0] = kv_t
        if n > 0:
            vt_refs[n - 1][0] = kv_t[KV_WIDTH:, :].astype(BF16)


def _pack_w_in(w_in):
    splits = np.cumsum([NSA_WIDTH, 2 * KV_WIDTH, 2 * KV_WIDTH, 2 * KV_WIDTH, NSA_HEADS * N_BRANCH,
                        MLSTM_WIDTH, MLSTM_WIDTH, MLSTM_WIDTH, MLSTM_HEADS]).tolist()
    q, kvc, kvs, kvw, gt, mu, mv, mo, mi, mf = jnp.split(w_in, splits, axis=1)
    gates = jnp.concatenate([gt, mi, mf], axis=1)
    gates = jnp.pad(gates, ((0, 0), (0, LANES - gates.shape[1])))
    w_rows = jnp.concatenate([q, kvc, mu, mv, mo, gates, kvs[:, :KV_WIDTH], kvw[:, :KV_WIDTH]],
                             axis=1).astype(BF16)
    w_kv_t = jnp.concatenate([kvc, kvs, kvw], axis=1).T.astype(BF16)
    return w_rows, w_kv_t


def _in_proj(x2d, g_mix, w_packed, *, batch, seq, tm):
    w_rows, w_kv_t = w_packed
    t = x2d.shape[0]
    ns = seq // tm
    kv_sd = jax.ShapeDtypeStruct((batch, 2 * KV_WIDTH, seq), F32)
    vt_sd = jax.ShapeDtypeStruct((batch, KV_WIDTH, seq), BF16)
    feat_major = lambda rows: pl.BlockSpec((1, rows, tm), lambda i: (i // ns, 0, i % ns))
    return pl.pallas_call(
        _inproj_body,
        grid=(t // tm,),
        in_specs=[pl.BlockSpec((tm, D_MODEL), lambda i: (i, 0)),
                  pl.BlockSpec((1, D_MODEL), lambda i: (0, 0)),
                  pl.BlockSpec(w_rows.shape, lambda i: (0, 0)),
                  pl.BlockSpec(w_kv_t.shape, lambda i: (0, 0))],
        out_specs=[pl.BlockSpec((tm, n), lambda i: (i, 0)) for n in IN_ROW_WIDTHS]
        + [feat_major(2 * KV_WIDTH)] * N_KV_BRANCH + [feat_major(KV_WIDTH)] * (N_KV_BRANCH - 1),
        out_shape=[jax.ShapeDtypeStruct((t, n), dt) for n, dt in zip(IN_ROW_WIDTHS, IN_ROW_DTYPES)]
        + [kv_sd] * N_KV_BRANCH + [vt_sd] * (N_KV_BRANCH - 1),
        compiler_params=_cparams(("arbitrary",)),
        name="in_proj",
    )(x2d, g_mix.reshape(1, D_MODEL), w_rows, w_kv_t)


def _mlstm_body(*refs, valid, bb):
    cb_ref, c0_ref, n0_ref, m0_ref = refs[4:8]
    cn_ref, c_ref, n_ref, m_ref, xx_ref = refs[16:21]
    halo = SUBLANES

    @pl.when(pl.program_id(1) == 0)
    def _():
        xx_ref[:, 0:halo, :] = jnp.zeros((bb, halo, MLSTM_WIDTH), F32)
        xx_ref[:, halo - (MLSTM_CONV - 1):halo, :] = cb_ref[...]
        c_ref[...] = c0_ref[...]
        n_ref[...] = n0_ref[...]
        m_ref[...] = m0_ref[...]

    _mlstm_chunk(*refs, valid=valid, bb=bb)


def _mlstm_chunk(mu_ref, mv_ref, mo_ref, g_ref, cb_ref, c0_ref, n0_ref, m0_ref,
                 wc_ref, bc_ref, wq_ref, wk_ref, gb_ref, gh_ref, sk_ref,
                 o_ref, cn_ref, c_ref, n_ref, m_ref,
                 xx_ref, vpad_ref, gpad_ref, *, valid, bb):
    L = MLSTM_CHUNK
    DH = MLSTM_DH
    halo = SUBLANES
    units = [(bi, h) for bi in range(bb) for h in range(MLSTM_HEADS)]
    head_lanes = lambda h: slice(h * DH, (h + 1) * DH)
    row = lax.broadcasted_iota(jnp.int32, (L, L), 0)
    col = lax.broadcasted_iota(jnp.int32, (L, L), 1)
    tril = row >= col
    triu = row <= col
    tok_col = lax.broadcasted_iota(jnp.int32, (L, 1), 0)
    tok_row = lax.broadcasted_iota(jnp.int32, (1, L), 1)

    def log_sigmoid(x):
        return jnp.minimum(x, 0.0) - jnp.log(1.0 + jnp.exp(-jnp.abs(x)))

    uc, gb, gbt = {}, {}, {}
    for bi in range(bb):
        if valid < L:
            xx_ref[bi, halo:, :] = jnp.zeros((L, MLSTM_WIDTH), F32)
            vpad_ref[bi] = jnp.zeros((L, MLSTM_WIDTH), F32)
            gpad_ref[bi] = jnp.zeros((L, LANES), F32)
        xx_ref[bi, halo:halo + valid, :] = mu_ref[bi]
        vpad_ref[bi, 0:valid, :] = mv_ref[bi]
        gpad_ref[bi, 0:valid, :] = g_ref[bi]
        conv = xx_ref[bi, halo - 3:halo - 3 + L, :] * wc_ref[0:1, :]
        for j in range(1, MLSTM_CONV):
            conv = conv + xx_ref[bi, halo - 3 + j:halo - 3 + j + L, :] * wc_ref[j:j + 1, :]
        uc[bi] = _silu(conv + bc_ref[...])
        tail = xx_ref[bi, valid + halo - 3:valid + halo, :]
        xx_ref[bi, halo - 3:halo, :] = tail
        cn_ref[bi] = tail
        gb[bi] = gpad_ref[bi] + gb_ref[...]
        gbt[bi] = gb[bi].T

    q, k, qb, kb = {}, {}, {}, {}
    for u in units:
        bi, h = u
        ub = uc[bi][:, head_lanes(h)].astype(BF16)
        q[u] = _dot(ub, wq_ref[h])
        k[u] = _dot(ub, wk_ref[h]) * (DH ** -0.5)
        qb[u], kb[u] = q[u].astype(BF16), k[u].astype(BF16)

    ic_col, ic_row, cum_col, cum_row = {}, {}, {}, {}
    for u in units:
        bi, h = u
        ic_c = gb[bi][:, GATE_COL_I + h:GATE_COL_I + h + 1]
        ic_r = gbt[bi][GATE_COL_I + h:GATE_COL_I + h + 1, :]
        lf_c = log_sigmoid(gb[bi][:, GATE_COL_F + h:GATE_COL_F + h + 1])
        lf_r = log_sigmoid(gbt[bi][GATE_COL_F + h:GATE_COL_F + h + 1, :])
        if valid < L:
            ic_c = jnp.where(tok_col < valid, ic_c, NEG_INF)
            ic_r = jnp.where(tok_row < valid, ic_r, NEG_INF)
            lf_c = jnp.where(tok_col < valid, lf_c, 0.0)
            lf_r = jnp.where(tok_row < valid, lf_r, 0.0)
        ic_col[u], ic_row[u] = ic_c, ic_r
        cum_col[u] = jnp.sum(jnp.where(tril, lf_r, 0.0), axis=1, keepdims=True)
        cum_row[u] = jnp.sum(jnp.where(triu, lf_c, 0.0), axis=0, keepdims=True)

    m_t, w, sc = {}, {}, {}
    for u in units:
        bi, h = u
        m0 = m_ref[bi, 0:1, h:h + 1]
        dmat = jnp.where(tril, cum_col[u] - cum_row[u] + ic_row[u], NEG_INF)
        inter = cum_col[u] + m0
        m_t[u] = jnp.maximum(inter, jnp.max(dmat, axis=1, keepdims=True))
        w[u] = jnp.exp(dmat - m_t[u])
        sc[u] = jnp.exp(inter - m_t[u])

    hc = {}
    for u in units:
        bi, h = u
        s = _dot_nt(qb[u], kb[u]) * w[u]
        v = vpad_ref[bi, :, head_lanes(h)]
        c_old = c_ref[bi, h]
        n_old = n_ref[bi, h:h + 1, :]
        num = _dot(s.astype(BF16), v.astype(BF16)) + sc[u] * _dot_nt(qb[u], c_old.astype(BF16))
        den = (jnp.sum(s, axis=1, keepdims=True)
               + sc[u] * jnp.sum(q[u] * n_old, axis=1, keepdims=True))
        hc[u] = num / jnp.maximum(jnp.abs(den), jnp.exp(-m_t[u]))

    for u in units:
        bi, h = u
        m0 = m_ref[bi, 0:1, h:h + 1]
        m_new = m_t[u][L - 1:L, :]
        cum_last = cum_col[u][L - 1:L, :]
        wl = jnp.exp(cum_last - cum_col[u] + ic_col[u] - m_new)
        sl = jnp.exp(cum_last + m0 - m_new)
        v = vpad_ref[bi, :, head_lanes(h)]
        vw_t = (v * wl).T.astype(BF16)
        c_ref[bi, h] = sl * c_ref[bi, h] + _dot(vw_t, kb[u])
        n_ref[bi, h:h + 1, :] = sl * n_ref[bi, h:h + 1, :] + jnp.sum(wl * k[u], axis=0, keepdims=True)
        m_ref[bi, 0:1, h:h + 1] = m_new

    for u in units:
        bi, h = u
        hn = _rms(hc[u], gh_ref[:, head_lanes(h)])
        u_h = uc[bi][:, head_lanes(h)]
        out = ((hn[0:valid, :] + sk_ref[:, head_lanes(h)] * u_h[0:valid, :])
               * _sigmoid(mo_ref[bi, :, head_lanes(h)]))
        o_ref[bi, :, head_lanes(h)] = out


def _mlstm(mu, mv, mo, gates, conv_buf, c0, n0, m0, w_mconv, b_mconv, w_mq, w_mk, b_ig, b_fg,
           g_mhead, m_skip, *, batch, seq):
    L = MLSTM_CHUNK
    valid = min(seq, L)
    assert seq % valid == 0 and (valid == L or seq == valid)
    nc = seq // valid
    gate_bias = jnp.zeros((1, LANES), F32)
    gate_bias = gate_bias.at[0, GATE_COL_I:GATE_COL_I + MLSTM_HEADS].set(b_ig)
    gate_bias = gate_bias.at[0, GATE_COL_F:GATE_COL_F + MLSTM_HEADS].set(b_fg)
    bb = MLSTM_SEQS_PER_STEP
    assert batch % bb == 0
    tok = lambda b, c: (b, c, 0)
    const2 = lambda b, c: (0, 0)
    const3 = lambda b, c: (0, 0, 0)
    per_b3 = lambda b, c: (b, 0, 0)
    per_b4 = lambda b, c: (b, 0, 0, 0)
    H, DH, W = MLSTM_HEADS, MLSTM_DH, MLSTM_WIDTH
    rows3 = lambda a: a.reshape(batch, seq, a.shape[-1])
    o_m, conv_new, c_new, n_new, m_new = pl.pallas_call(
        functools.partial(_mlstm_body, valid=valid, bb=bb),
        grid=(batch // bb, nc),
        in_specs=[pl.BlockSpec((bb, valid, W), tok), pl.BlockSpec((bb, valid, W), tok),
                  pl.BlockSpec((bb, valid, W), tok), pl.BlockSpec((bb, valid, LANES), tok),
                  pl.BlockSpec((bb, MLSTM_CONV - 1, W), per_b3),
                  pl.BlockSpec((bb, H, DH, DH), per_b4),
                  pl.BlockSpec((bb, H, DH), per_b3),
                  pl.BlockSpec((bb, 1, H), per_b3),
                  pl.BlockSpec((MLSTM_CONV, W), const2), pl.BlockSpec((1, W), const2),
                  pl.BlockSpec((H, DH, DH), const3), pl.BlockSpec((H, DH, DH), const3),
                  pl.BlockSpec((1, LANES), const2), pl.BlockSpec((1, W), const2),
                  pl.BlockSpec((1, W), const2)],
        out_specs=[pl.BlockSpec((bb, valid, W), tok),
                   pl.BlockSpec((bb, MLSTM_CONV - 1, W), per_b3),
                   pl.BlockSpec((bb, H, DH, DH), per_b4),
                   pl.BlockSpec((bb, H, DH), per_b3),
                   pl.BlockSpec((bb, 1, H), per_b3)],
        out_shape=[jax.ShapeDtypeStruct((batch, seq, W), F32),
                   jax.ShapeDtypeStruct((batch, MLSTM_CONV - 1, W), F32),
                   jax.ShapeDtypeStruct((batch, H, DH, DH), F32),
                   jax.ShapeDtypeStruct((batch, H, DH), F32),
                   jax.ShapeDtypeStruct((batch, 1, H), F32)],
        scratch_shapes=[pltpu.VMEM((bb, SUBLANES + L, W), F32), pltpu.VMEM((bb, L, W), F32),
                        pltpu.VMEM((bb, L, LANES), F32)],
        compiler_params=_cparams(("arbitrary", "arbitrary")),
        name="mlstm",
    )(rows3(mu), rows3(mv), rows3(mo), rows3(gates), conv_buf, c0, n0, m0.reshape(batch, 1, H),
      w_mconv, b_mconv.reshape(1, W), w_mq.astype(BF16), w_mk.astype(BF16), gate_bias,
      g_mhead.reshape(1, W), m_skip.reshape(1, W))
    return o_m.reshape(batch * seq, W), conv_new, c_new, n_new, m_new


def _compress_rows(xk_ref, xv_ref, pe_ref, w1_ref, w2_ref, n_pairs):
    pair_rows = 2 * CMP_BLOCK
    outs = []
    for kv, x_ref in enumerate((xk_ref, xv_ref)):
        acc = jnp.zeros((2 * n_pairs, NSA_KV_HEADS * CMP_HIDDEN), F32)
        for r in range(CMP_BLOCK):
            ev = x_ref[pl.ds(r, n_pairs, stride=pair_rows), :]
            od = x_ref[pl.ds(CMP_BLOCK + r, n_pairs, stride=pair_rows), :]
            xr = jnp.concatenate([ev, od], axis=0) + pe_ref[kv, r:r + 1, :]
            acc = acc + _dot(xr.astype(BF16), w1_ref[kv, r])
        outs.append(_dot(_silu(acc).astype(BF16), w2_ref[kv]))
    return jnp.concatenate(outs, axis=1)


def _compress_body(xk_ref, xv_ref, pe_ref, w1_ref, w2_ref, oe_ref, oo_ref, *, n_pairs):
    out = _compress_rows(xk_ref, xv_ref, pe_ref, w1_ref, w2_ref, n_pairs)
    oe_ref[0] = out[0:n_pairs, :]
    oo_ref[0] = out[n_pairs:, :]


BLOCKS_PER_PAGE = PAGE_SIZE // CMP_BLOCK


def _gather_pages(pt_ref, pool_hbm, pages_ref, sem_ref, n_pages):
    g = pl.program_id(0) * pl.num_programs(1) + pl.program_id(1)
    n_total = pl.num_programs(0) * pl.num_programs(1)

    def copies(step, slot):
        return [pltpu.make_async_copy(pool_hbm.at[pt_ref[step * n_pages + j]], pages_ref.at[slot, j],
                                      sem_ref.at[slot]) for j in range(n_pages)]

    @pl.when(g == 0)
    def _():
        for cp in copies(0, 0):
            cp.start()

    @pl.when(g + 1 < n_total)
    def _():
        for cp in copies(g + 1, (g + 1) % 2):
            cp.start()

    slot = g % 2
    for cp in copies(g, slot):
        cp.wait()
    return slot


def _compress_paged_body(pt_ref, pool_hbm, pet_ref, perm_ref, w1_ref, w2_ref, oe_ref, oo_ref,
                         buf_ref, os_ref, pages_ref, sem_ref, *, n_pages):
    slot = _gather_pages(pt_ref, pool_hbm, pages_ref, sem_ref, n_pages)
    grp = 2 * BLOCKS_PER_PAGE
    for jp in range(n_pages // 2):
        xt = jnp.concatenate([pages_ref[slot, 2 * jp], pages_ref[slot, 2 * jp + 1]], axis=1)
        xb = (xt + pet_ref[...]).astype(BF16)
        xp = _dot_nt(perm_ref[...], xb)
        for r in range(CMP_BLOCK):
            for kv in range(2):
                lane0 = (2 * kv + r % 2) * KV_WIDTH
                buf_ref[r // 2, grp * jp:grp * (jp + 1), lane0:lane0 + KV_WIDTH] = (
                    xp[grp * r:grp * (r + 1), kv * KV_WIDTH:(kv + 1) * KV_WIDTH])
    for kv in range(2):
        lanes = slice(2 * kv * KV_WIDTH, 2 * (kv + 1) * KV_WIDTH)
        acc = _dot(buf_ref[0, :, lanes].astype(BF16), w1_ref[kv, 0])
        for r2 in range(1, CMP_BLOCK // 2):
            acc = acc + _dot(buf_ref[r2, :, lanes].astype(BF16), w1_ref[kv, r2])
        os_ref[kv] = _dot(_silu(acc).astype(BF16), w2_ref[kv])
    half = os_ref.shape[1] // 2
    for parity, ref in enumerate((oe_ref, oo_ref)):
        ref[0] = jnp.concatenate([os_ref[kv, pl.ds(parity, half, stride=2), :] for kv in range(2)],
                                 axis=1)


def _page_pair_constants(pe):
    pe_t = jnp.broadcast_to(pe.transpose(0, 2, 1)[:, None, :, None, :],
                            (2, NSA_KV_HEADS, HEAD_DIM, 2 * BLOCKS_PER_PAGE, CMP_BLOCK))
    pe_t = pe_t.reshape(2 * KV_WIDTH, 2 * PAGE_SIZE)
    grp = 2 * BLOCKS_PER_PAGE
    perm = np.zeros((2 * PAGE_SIZE, 2 * PAGE_SIZE), np.float32)
    for r in range(CMP_BLOCK):
        for b in range(grp):
            perm[r * grp + b, b * CMP_BLOCK + r] = 1.0
    return pe_t, jnp.asarray(perm, BF16)


def _pack_compress_weights(pe, w1, w2):
    eye_h = jnp.eye(NSA_KV_HEADS, dtype=F32)
    pe_r = jnp.broadcast_to(pe[:, :, None, :], (2, CMP_BLOCK, NSA_KV_HEADS, HEAD_DIM))
    pe_r = pe_r.reshape(2, CMP_BLOCK, KV_WIDTH)
    w1r = w1.reshape(2, CMP_BLOCK, HEAD_DIM, CMP_HIDDEN)
    w1_big = jnp.einsum('krdc,hH->krhdHc', w1r, eye_h)
    w1_big = w1_big.reshape(2, CMP_BLOCK, KV_WIDTH, NSA_KV_HEADS * CMP_HIDDEN).astype(BF16)
    w2_big = jnp.einsum('kcd,hH->khcHd', w2, eye_h)
    w2_big = w2_big.reshape(2, NSA_KV_HEADS * CMP_HIDDEN, KV_WIDTH).astype(BF16)
    return pe_r, w1_big, w2_big


def _compress_prompt(k_rows, v_rows, cw, *, batch, seq):
    pe_r, w1_big, w2_big = cw
    n_pairs = seq // (2 * CMP_BLOCK)
    const3 = lambda b: (0, 0, 0)
    out_sd = jax.ShapeDtypeStruct((batch, n_pairs, 2 * KV_WIDTH), F32)
    return pl.pallas_call(
        functools.partial(_compress_body, n_pairs=n_pairs),
        grid=(batch,),
        in_specs=[pl.BlockSpec((seq, KV_WIDTH), lambda b: (b, 0)),
                  pl.BlockSpec((seq, KV_WIDTH), lambda b: (b, 0)),
                  pl.BlockSpec(pe_r.shape, const3),
                  pl.BlockSpec(w1_big.shape, lambda b: (0, 0, 0, 0)),
                  pl.BlockSpec(w2_big.shape, const3)],
        out_specs=[pl.BlockSpec((1, n_pairs, 2 * KV_WIDTH), lambda b: (b, 0, 0))] * 2,
        out_shape=[out_sd, out_sd],
        compiler_params=_cparams(("arbitrary",)),
        name="compress_prompt",
    )(k_rows, v_rows, pe_r, w1_big, w2_big)


COMPRESS_PAGES_PER_STEP = 64


def _compress_paged(pool, page_table, cw, cw_pages):
    _, w1_big, w2_big = cw
    w1_big = w1_big.reshape(2, CMP_BLOCK // 2, 2 * KV_WIDTH, NSA_KV_HEADS * CMP_HIDDEN)
    pe_t, perm = cw_pages
    batch, n_pages = page_table.shape
    pps = COMPRESS_PAGES_PER_STEP
    assert n_pages % pps == 0 and pps % 2 == 0
    n_steps = n_pages // pps
    n_blk = pps * BLOCKS_PER_PAGE
    const3 = lambda b, c, pt: (0, 0, 0)
    return pl.pallas_call(
        functools.partial(_compress_paged_body, n_pages=pps),
        grid_spec=pltpu.PrefetchScalarGridSpec(
            num_scalar_prefetch=1,
            grid=(batch, n_steps),
            in_specs=[pl.BlockSpec(memory_space=pl.ANY),
                      pl.BlockSpec(pe_t.shape, lambda b, c, pt: (0, 0)),
                      pl.BlockSpec(perm.shape, lambda b, c, pt: (0, 0)),
                      pl.BlockSpec(w1_big.shape, lambda b, c, pt: (0, 0, 0, 0)),
                      pl.BlockSpec(w2_big.shape, const3)],
            out_specs=[pl.BlockSpec((1, n_blk // 2, 2 * KV_WIDTH), lambda b, c, pt: (b, c, 0))] * 2,
            scratch_shapes=[pltpu.VMEM((CMP_BLOCK // 2, n_blk, 4 * KV_WIDTH), F32),
                            pltpu.VMEM((2, n_blk, KV_WIDTH), F32),
                            pltpu.VMEM((2, pps, 2 * KV_WIDTH, PAGE_SIZE), F32),
                            pltpu.SemaphoreType.DMA((2,))]),
        out_shape=[jax.ShapeDtypeStruct((batch, n_steps * n_blk // 2, 2 * KV_WIDTH), F32)] * 2,
        compiler_params=_cparams(("arbitrary", "arbitrary")),
        name="compress_paged",
    )(page_table.reshape(-1), pool, pe_t, perm, w1_big, w2_big)


def _cmp_attn_body(q_ref, ke_ref, ko_ref, o_ref, st_ref, *, tq, pos0):
    ns = ke_ref.shape[1]
    i = pl.program_id(1)
    rows = NSA_GROUP * tq
    tok0 = pos0 + i * tq
    pos_c = tok0 + lax.broadcasted_iota(jnp.int32, (rows, 1), 0) % tq
    pos_r = tok0 + lax.broadcasted_iota(jnp.int32, (1, rows), 1) % tq
    pair_r = lax.broadcasted_iota(jnp.int32, (1, ns), 1)
    pair_c = lax.broadcasted_iota(jnp.int32, (ns, 1), 0)
    end_e = lambda pair: (2 * pair + 1) * CMP_BLOCK - 1
    end_o = lambda pair: (2 * pair + 2) * CMP_BLOCK - 1
    any_c = (CMP_BLOCK - 1 <= pos_c).astype(F32)
    any_r = (CMP_BLOCK - 1 <= pos_r).astype(F32)
    q = q_ref[...] * ATTN_SCALE
    for kh in range(NSA_KV_HEADS):
        qs = jnp.concatenate([q[:, (kh * NSA_GROUP + g) * HEAD_DIM:(kh * NSA_GROUP + g + 1) * HEAD_DIM]
                              for g in range(NSA_GROUP)], axis=0).astype(BF16)
        ks, vs = slice(kh * HEAD_DIM, (kh + 1) * HEAD_DIM), slice(KV_WIDTH + kh * HEAD_DIM,
                                                                   KV_WIDTH + (kh + 1) * HEAD_DIM)
        ke, ko = ke_ref[0, :, ks].astype(BF16), ko_ref[0, :, ks].astype(BF16)
        se = jnp.where(end_e(pair_r) <= pos_c, _dot_nt(qs, ke), NEG_INF)
        so = jnp.where(end_o(pair_r) <= pos_c, _dot_nt(qs, ko), NEG_INF)
        mx = jnp.maximum(jnp.max(se, axis=1, keepdims=True), jnp.max(so, axis=1, keepdims=True))
        pe, po = jnp.exp(se - mx), jnp.exp(so - mx)
        inv = any_c / (jnp.sum(pe, axis=1, keepdims=True) + jnp.sum(po, axis=1, keepdims=True))
        oh = (_dot((pe * inv).astype(BF16), ke_ref[0, :, vs].astype(BF16))
              + _dot((po * inv).astype(BF16), ko_ref[0, :, vs].astype(BF16)))
        for g in range(NSA_GROUP):
            hd = kh * NSA_GROUP + g
            o_ref[:, hd * HEAD_DIM:(hd + 1) * HEAD_DIM] = oh[g * tq:(g + 1) * tq, :]
        te = jnp.where(end_e(pair_c) <= pos_r, _dot_nt(ke, qs), NEG_INF)
        to = jnp.where(end_o(pair_c) <= pos_r, _dot_nt(ko, qs), NEG_INF)
        mt = jnp.maximum(jnp.max(te, axis=0, keepdims=True), jnp.max(to, axis=0, keepdims=True))
        pte, pto = jnp.exp(te - mt), jnp.exp(to - mt)
        invt = any_r / (jnp.sum(pte, axis=0, keepdims=True) + jnp.sum(pto, axis=0, keepdims=True))
        ps = (pte + pto) * invt
        score = ps[:, 0:tq]
        for g in range(1, NSA_GROUP):
            score = score + ps[:, g * tq:(g + 1) * tq]
        st_ref[0, kh] = score


def _cmp_attn(q2d, kce, kco, *, batch, seq, tq, pos0):
    ns = kce.shape[1]
    nq = seq // tq
    return pl.pallas_call(
        functools.partial(_cmp_attn_body, tq=tq, pos0=pos0),
        grid=(batch, nq),
        in_specs=[pl.BlockSpec((tq, NSA_WIDTH), lambda b, i: (b * nq + i, 0)),
                  pl.BlockSpec((1, ns, 2 * KV_WIDTH), lambda b, i: (b, 0, 0)),
                  pl.BlockSpec((1, ns, 2 * KV_WIDTH), lambda b, i: (b, 0, 0))],
        out_specs=[pl.BlockSpec((tq, NSA_WIDTH), lambda b, i: (b * nq + i, 0)),
                   pl.BlockSpec((1, NSA_KV_HEADS, ns, tq), lambda b, i: (b, 0, 0, i))],
        out_shape=[jax.ShapeDtypeStruct((batch * seq, NSA_WIDTH), F32),
                   jax.ShapeDtypeStruct((batch, NSA_KV_HEADS, ns, seq), F32)],
        compiler_params=_cparams(("arbitrary", "arbitrary")),
        name="cmp_attn",
    )(q2d, kce, kco)


def _topk_body(pos_ref, st_ref, b_ref, *, n_sel):
    score = st_ref[0]
    ns, tt = score.shape
    nsw = b_ref.shape[1]
    if nsw > ns:
        score = jnp.concatenate([score, jnp.zeros((nsw - ns, tt), F32)], axis=0)
    blk = lax.broadcasted_iota(jnp.int32, (nsw, 1), 0)
    blk_f = blk.astype(F32)
    cur = pos_ref[...] // SEL_BLOCK
    forced = (blk == 0) | (blk == cur) | (blk == cur - 1)
    pri = jnp.where(blk <= cur, jnp.where(forced, SEL_PRIORITY, score), -SEL_PRIORITY)
    pri = jnp.where(blk < n_sel, pri, -jnp.inf)
    bias = jnp.full((nsw, tt), NEG_INF, F32)
    for _ in range(min(TOP_N, n_sel)):
        top = jnp.max(pri, axis=0, keepdims=True)
        first = jnp.min(jnp.where(pri == top, blk_f, float(nsw)), axis=0, keepdims=True)
        hit = blk_f == first
        bias = jnp.where(hit, 0.0, bias)
        pri = jnp.where(hit, -jnp.inf, pri)
    b_ref[0] = bias


def _topk_blocks(scores_t, pos, *, n_sel, nsw, tt):
    groups, ns, tokens = scores_t.shape
    assert nsw >= max(ns, n_sel) and tokens % tt == 0
    return pl.pallas_call(
        functools.partial(_topk_body, n_sel=n_sel),
        grid=(groups, tokens // tt),
        in_specs=[pl.BlockSpec((1, tt), lambda g, i: (0, i)),
                  pl.BlockSpec((1, ns, tt), lambda g, i: (g, 0, i))],
        out_specs=pl.BlockSpec((1, nsw, tt), lambda g, i: (g, 0, i)),
        out_shape=jax.ShapeDtypeStruct((groups, nsw, tokens), F32),
        compiler_params=_cparams(("arbitrary", "arbitrary")),
        name="topk_blocks",
    )(pos, scores_t)


def _softmax_update(sc, vt_bf16, m_ref, l_ref, acc_ref):
    m_old = m_ref[...]
    m_new = jnp.maximum(m_old, jnp.max(sc, axis=1, keepdims=True))
    alpha = jnp.exp(m_old - m_new)
    pr = jnp.exp(sc - jnp.concatenate([m_new] * (sc.shape[1] // LANES), axis=1))
    l_ref[...] = alpha * l_ref[...] + jnp.sum(pr, axis=1, keepdims=True)
    acc_ref[...] = alpha * acc_ref[...] + _dot_nt(pr.astype(BF16), vt_bf16)
    m_ref[...] = m_new


def _softmax_init(m_ref, l_ref, acc_ref):
    m_ref[...] = jnp.full(m_ref.shape, NEG_INF, F32)
    l_ref[...] = jnp.zeros(l_ref.shape, F32)
    acc_ref[...] = jnp.zeros(acc_ref.shape, F32)


def _block_onehot_t(first_key, n_keys):
    blk = (first_key + lax.broadcasted_iota(jnp.int32, (1, n_keys), 1)) // SEL_BLOCK
    r = lax.broadcasted_iota(jnp.int32, (LANES, 1), 0) & (SEL_BLOCK - 1)
    return (r == blk).astype(F32)


ATTN_TAB_COLS = 5


def _attn_pairs(seq, tq, tk):
    rows = []
    for i in range(seq // tq):
        t_lo, t_hi = i * tq, i * tq + tq - 1
        js = list(range(0, t_hi // tk + 1))
        for n, j in enumerate(js):
            rows.append((i, j, int(n == 0), int(n == len(js) - 1), int(j * tk + tk - 1 > t_lo)))
    return np.asarray(rows, np.int32)


def _attn_body(tab_ref, q_ref, k_ref, vt_ref, oh_ref, sb_ref, o_ref, qa_ref, m_ref, l_ref, acc_ref,
               *, tq, tk):
    p = pl.program_id(1)
    i, j, first, last, partial_tile = [tab_ref[ATTN_TAB_COLS * p + n] for n in range(ATTN_TAB_COLS)]
    cols = NSA_HEADS * tq

    @pl.when(first == 1)
    def _():
        for hd, piece in enumerate(_query_columns(q_ref[0], sb_ref[0], tq)):
            qa_ref[:, hd * tq:(hd + 1) * tq] = piece
        m_ref[...] = jnp.full(m_ref.shape, NEG_INF, F32)
        l_ref[...] = jnp.zeros(l_ref.shape, F32)
        acc_ref[...] = jnp.zeros(acc_ref.shape, F32)

    k_aug = jnp.concatenate([k_ref[...], oh_ref[...]], axis=1)
    sc = _dot(k_aug, qa_ref[...])
    vt = vt_ref[0]

    def update(sc):
        m_old = m_ref[...]
        m_new = jnp.maximum(m_old, jnp.max(sc, axis=0, keepdims=True))
        alpha = jnp.exp2(m_old - m_new)
        pr = jnp.exp2(sc - m_new)
        l_ref[...] = alpha * l_ref[...] + jnp.sum(pr, axis=0, keepdims=True)
        acc_ref[...] = alpha * acc_ref[...] + _dot(vt, pr.astype(BF16))
        m_ref[...] = m_new

    @pl.when(partial_tile == 1)
    def _():
        qpos = i * tq + (lax.broadcasted_iota(jnp.int32, (1, cols), 1) & (tq - 1))
        kpos = j * tk + lax.broadcasted_iota(jnp.int32, (tk, 1), 0)
        update(jnp.where(kpos <= qpos, sc, NEG_INF))

    @pl.when(partial_tile == 0)
    def _():
        update(sc)

    @pl.when(last == 1)
    def _():
        _store_heads(acc_ref[...] / l_ref[...], o_ref, tq)


def _query_columns(q, sel_bias, tq):
    q = q * (ATTN_SCALE * LOG2_E)
    zeros64 = jnp.zeros((HEAD_DIM, tq), F32)
    kv_head_rows = lambda x, kh: jnp.concatenate([x, zeros64] if kh == 0 else [zeros64, x], axis=0)
    pieces = []
    for m in range(NSA_HEADS // 2):
        q_t = q[:, m * LANES:(m + 1) * LANES].T
        for hd in (2 * m, 2 * m + 1):
            kh = hd // NSA_GROUP
            piece = kv_head_rows(q_t[(hd % 2) * HEAD_DIM:(hd % 2 + 1) * HEAD_DIM, :], kh)
            if sel_bias is not None:
                piece = jnp.concatenate([piece, kv_head_rows(sel_bias[kh], kh)], axis=0)
            pieces.append(piece.astype(BF16))
    return pieces


def _store_heads(o_t, o_ref, tq):
    for m in range(NSA_HEADS // 2):
        pair = jnp.concatenate(
            [o_t[(hd // NSA_GROUP) * HEAD_DIM:(hd // NSA_GROUP + 1) * HEAD_DIM, hd * tq:(hd + 1) * tq]
             for hd in (2 * m, 2 * m + 1)], axis=0)
        o_ref[0, :, m * LANES:(m + 1) * LANES] = pair.T


def _window_body(q_ref, *refs, tq, n_tiles):
    k_refs, v_refs, o_ref = refs[:n_tiles], refs[n_tiles:2 * n_tiles], refs[2 * n_tiles]
    i = pl.program_id(1)
    cols = NSA_HEADS * tq
    qa = jnp.concatenate(_query_columns(q_ref[0], None, tq), axis=1)
    qpos = i * tq + (lax.broadcasted_iota(jnp.int32, (1, cols), 1) & (tq - 1))
    scs = []
    for n, k_ref in enumerate(k_refs):
        kpos = (i - (n_tiles - 1) + n) * tq + lax.broadcasted_iota(jnp.int32, (tq, 1), 0)
        valid = (kpos <= qpos) & (kpos > qpos - WINDOW) & (kpos >= 0)
        scs.append(jnp.where(valid, _dot(k_ref[...], qa), NEG_INF))
    mx = scs[0].max(axis=0, keepdims=True)
    for sc in scs[1:]:
        mx = jnp.maximum(mx, sc.max(axis=0, keepdims=True))
    l_sum, acc = None, None
    for sc, v_ref in zip(scs, v_refs):
        pr = jnp.exp2(sc - mx)
        part_l, part_acc = jnp.sum(pr, axis=0, keepdims=True), _dot(v_ref[0], pr.astype(BF16))
        l_sum = part_l if l_sum is None else l_sum + part_l
        acc = part_acc if acc is None else acc + part_acc
    _store_heads(acc / l_sum, o_ref, tq)


def _attn_window_prompt(q3d, k_rows, v_t, *, tq):
    batch, seq, _ = q3d.shape
    assert WINDOW % tq == 0 and seq % tq == 0 and tq & (tq - 1) == 0
    n_tiles = WINDOW // tq + 1
    nq = seq // tq
    tile = lambda n: (lambda i: jnp.maximum(i - (n_tiles - 1) + n, 0))
    return pl.pallas_call(
        functools.partial(_window_body, tq=tq, n_tiles=n_tiles),
        grid=(batch, nq),
        in_specs=[pl.BlockSpec((1, tq, NSA_WIDTH), lambda b, i: (b, i, 0))]
        + [pl.BlockSpec((tq, KV_WIDTH), lambda b, i, t=tile(n): (b * nq + t(i), 0)) for n in range(n_tiles)]
        + [pl.BlockSpec((1, KV_WIDTH, tq), lambda b, i, t=tile(n): (b, 0, t(i))) for n in range(n_tiles)],
        out_specs=pl.BlockSpec((1, tq, NSA_WIDTH), lambda b, i: (b, i, 0)),
        out_shape=jax.ShapeDtypeStruct((batch, seq, NSA_WIDTH), F32),
        compiler_params=_cparams(("arbitrary", "arbitrary")),
        name="attn_win",
    )(q3d, *([k_rows] * n_tiles), *([v_t] * n_tiles))


def _block_onehot(seq):
    blk = np.arange(seq)[:, None] // SEL_BLOCK
    return jnp.asarray((np.arange(LANES)[None, :] % SEL_BLOCK) == blk, BF16)


def _attn_selected_prompt(q3d, k_rows, v_t, selb, *, tq, tk):
    batch, seq, _ = q3d.shape
    assert tq & (tq - 1) == 0 and tk % LANES == 0 and tq % LANES == 0 and selb.shape[2] == SEL_BLOCK
    tab = _attn_pairs(seq, tq, tk)
    cols = NSA_HEADS * tq
    C = ATTN_TAB_COLS
    nk = seq // tk
    return pl.pallas_call(
        functools.partial(_attn_body, tq=tq, tk=tk),
        grid_spec=pltpu.PrefetchScalarGridSpec(
            num_scalar_prefetch=1,
            grid=(batch, tab.shape[0]),
            in_specs=[pl.BlockSpec((1, tq, NSA_WIDTH), lambda b, p, t: (b, t[C * p], 0)),
                      pl.BlockSpec((tk, KV_WIDTH), lambda b, p, t: (b * nk + t[C * p + 1], 0)),
                      pl.BlockSpec((1, KV_WIDTH, tk), lambda b, p, t: (b, 0, t[C * p + 1])),
                      pl.BlockSpec((tk, LANES), lambda b, p, t: (t[C * p + 1], 0)),
                      pl.BlockSpec((1, NSA_KV_HEADS, SEL_BLOCK, tq),
                                   lambda b, p, t: (b, 0, 0, t[C * p]))],
            out_specs=pl.BlockSpec((1, tq, NSA_WIDTH), lambda b, p, t: (b, t[C * p], 0)),
            scratch_shapes=[pltpu.VMEM((2 * LANES, cols), BF16), pltpu.VMEM((1, cols), F32),
                            pltpu.VMEM((1, cols), F32), pltpu.VMEM((KV_WIDTH, cols), F32)]),
        out_shape=jax.ShapeDtypeStruct((batch, seq, NSA_WIDTH), F32),
        compiler_params=_cparams(("arbitrary", "arbitrary")),
        name="attn_sel",
    )(jnp.asarray(tab.reshape(-1)), q3d, k_rows, v_t, _block_onehot(seq), selb)


ATTN_PAGES_PER_STEP = 32
ATTN_PAGED_SPLIT = 2


def _attn_paged_body(pt_ref, qa_ref, bq_ref, bn_ref, kn_ref, pool_hbm, o_ref, m_ref, l_ref, acc_ref,
                     pages_ref, sem_ref, *, n_pages, n_new):
    slot = _gather_pages(pt_ref, pool_hbm, pages_ref, sem_ref, n_pages)
    page_refs = [pages_ref.at[slot, j] for j in range(n_pages)]
    c = pl.program_id(1)
    rows = qa_ref.shape[1]

    @pl.when(c == 0)
    def _():
        _softmax_init(m_ref, l_ref, acc_ref)

    n_split = m_ref.shape[0]
    per = n_pages // n_split
    keys = per * PAGE_SIZE
    qa = qa_ref[0]
    lhs = jnp.concatenate([qa, bq_ref[0, 0]], axis=1).astype(BF16)
    scs, vts = [], []
    for s in range(n_split):
        refs_s = page_refs[s * per:(s + 1) * per]
        kt = jnp.concatenate([r[0:KV_WIDTH, :] for r in refs_s], axis=1)
        rhs = jnp.concatenate([kt, _block_onehot_t(s * keys, keys)], axis=0).astype(BF16)
        scs.append(_dot(lhs, rhs))
        vts.append(jnp.concatenate([r[KV_WIDTH:, :] for r in refs_s], axis=1).astype(BF16))
    for s in range(n_split):
        _softmax_update(scs[s], vts[s], m_ref.at[s], l_ref.at[s], acc_ref.at[s])

    @pl.when(c == pl.num_programs(1) - 1)
    def _():
        kn = kn_ref[0]
        sc = _dot(qa.astype(BF16), kn[0:KV_WIDTH, :].astype(BF16)) + bn_ref[0]
        tq = lax.broadcasted_iota(jnp.int32, (rows, 1), 0) % n_new
        kk = lax.broadcasted_iota(jnp.int32, (1, kn.shape[1]), 1)
        sc = jnp.where((kk <= tq) & (kk < n_new), sc, NEG_INF)
        _softmax_update(sc, kn[KV_WIDTH:, :].astype(BF16), m_ref.at[0], l_ref.at[0], acc_ref.at[0])
        m_all = m_ref[0]
        for s in range(1, n_split):
            m_all = jnp.maximum(m_all, m_ref[s])
        l_all = jnp.zeros(m_all.shape, F32)
        acc_all = jnp.zeros(m_all.shape, F32)
        for s in range(n_split):
            scale = jnp.exp(m_ref[s] - m_all)
            l_all = l_all + scale * l_ref[s]
            acc_all = acc_all + scale * acc_ref[s]
        o_ref[0] = acc_all / l_all


def _attn_paged(qa, bias_q, bias_new, kv_new_t, pool, page_table, *, n_new):
    batch, n_pages = page_table.shape
    pps = ATTN_PAGES_PER_STEP
    assert n_pages % pps == 0 and pps * PAGE_SIZE // SEL_BLOCK <= SEL_BLOCK
    n_steps = n_pages // pps
    rows = qa.shape[1]

    per_b = lambda b, c, pt: (b, 0, 0)
    return pl.pallas_call(
        functools.partial(_attn_paged_body, n_pages=pps, n_new=n_new),
        grid_spec=pltpu.PrefetchScalarGridSpec(
            num_scalar_prefetch=1,
            grid=(batch, n_steps),
            in_specs=[pl.BlockSpec((1, rows, LANES), per_b),
                      pl.BlockSpec((1, 1, rows, LANES), lambda b, c, pt: (b, c, 0, 0)),
                      pl.BlockSpec((1, rows, LANES), per_b),
                      pl.BlockSpec((1,) + kv_new_t.shape[1:], per_b),
                      pl.BlockSpec(memory_space=pl.ANY)],
            out_specs=pl.BlockSpec((1, rows, LANES), per_b),
            scratch_shapes=[pltpu.VMEM((ATTN_PAGED_SPLIT, rows, LANES), F32)] * 3
            + [pltpu.VMEM((2, pps, 2 * KV_WIDTH, PAGE_SIZE), F32), pltpu.SemaphoreType.DMA((2,))]),
        out_shape=jax.ShapeDtypeStruct((batch, rows, LANES), F32),
        compiler_params=_cparams(("arbitrary", "arbitrary")),
        name="attn_sel_paged",
    )(page_table.reshape(-1), qa, bias_q, bias_new, kv_new_t, pool)


def _attn_window_body(qa_ref, wb_ref, kn_ref, o_ref, *, n_new, past):
    qa = qa_ref[0].astype(BF16)
    wb, kn = wb_ref[0], kn_ref[0]
    rows, n_buf = qa.shape[0], wb.shape[1]
    qpos = past + lax.broadcasted_iota(jnp.int32, (rows, 1), 0) % n_new

    def masked(sc, kpos, extra):
        diff = qpos - kpos
        return jnp.where((diff >= 0) & (diff < WINDOW) & (kpos >= 0) & extra, sc, NEG_INF)

    nb = lax.broadcasted_iota(jnp.int32, (1, n_buf), 1)
    nn = lax.broadcasted_iota(jnp.int32, (1, kn.shape[1]), 1)
    sb = masked(_dot(qa, wb[0:KV_WIDTH, :].astype(BF16)), past - n_buf + nb, nb >= 0)
    sn = masked(_dot(qa, kn[0:KV_WIDTH, :].astype(BF16)), past + nn, nn < n_new)
    mx = jnp.maximum(jnp.max(sb, axis=1, keepdims=True), jnp.max(sn, axis=1, keepdims=True))
    pb, pn = jnp.exp(sb - mx), jnp.exp(sn - mx)
    o = (_dot_nt(pb.astype(BF16), wb[KV_WIDTH:, :].astype(BF16))
         + _dot_nt(pn.astype(BF16), kn[KV_WIDTH:, :].astype(BF16)))
    o_ref[0] = o / (jnp.sum(pb, axis=1, keepdims=True) + jnp.sum(pn, axis=1, keepdims=True))


def _attn_window_small(qa, win_t, kv_new_t, *, n_new, past):
    batch, rows, _ = qa.shape
    per_b = lambda b: (b, 0, 0)
    return pl.pallas_call(
        functools.partial(_attn_window_body, n_new=n_new, past=past),
        grid=(batch,),
        in_specs=[pl.BlockSpec((1, rows, LANES), per_b),
                  pl.BlockSpec((1,) + win_t.shape[1:], per_b),
                  pl.BlockSpec((1,) + kv_new_t.shape[1:], per_b)],
        out_specs=pl.BlockSpec((1, rows, LANES), per_b),
        out_shape=jax.ShapeDtypeStruct((batch, rows, LANES), F32),
        compiler_params=_cparams(("arbitrary",)),
        name="attn_win_small",
    )(qa, win_t, kv_new_t)


FFN_TM = 512
FFN_VMEM_LIMIT = 58 * 1024 * 1024
MXU_DEPTH = 256
FFN_CHUNKS = ((0, 6 * MXU_DEPTH), (6 * MXU_DEPTH, D_FF))


def _ffn_body(x_ref, om_ref, oc_ref, os_ref, ow_ref, gt_ref, ge_ref, gn_ref, gf_ref, gl_ref, wc_ref,
              fb_ref, wo_hbm, wu_hbm, wd_hbm, y_ref, fn_ref, xx_ref, wo_ref, wu_ref, wd_ref, sem_ref,
              *, tm, stride, halo):
    s = pl.program_id(1)

    @pl.when((pl.program_id(0) == 0) & (s == 0))
    def _():
        copies = [pltpu.make_async_copy(src, dst, sem_ref.at[n])
                  for n, (src, dst) in enumerate(((wo_hbm, wo_ref), (wu_hbm, wu_ref), (wd_hbm, wd_ref)))]
        for cp in copies:
            cp.start()
        for cp in copies:
            cp.wait()

    sig = _sigmoid(gt_ref[...])
    hi = sig.astype(BF16)
    lo = (sig - hi.astype(F32)).astype(BF16)
    comb = None
    for br, ob_ref in enumerate((oc_ref, os_ref, ow_ref)):
        gate = _dot(hi, ge_ref[br]) + _dot(lo, ge_ref[br])
        term = gate * ob_ref[...]
        comb = term if comb is None else comb + term
    onsa = _rms(comb, gn_ref[...])
    h = (x_ref[...] + _dot(om_ref[...].astype(BF16), wo_ref[0:MLSTM_WIDTH, :])
         + _dot(onsa.astype(BF16), wo_ref[MLSTM_WIDTH:, :]))
    hn = _rms(h, gf_ref[...]).astype(BF16)

    base = halo - (FFN_CONV - 1) * stride

    @pl.when(s == 0)
    def _():
        xx_ref[base:halo, :] = fb_ref[0]

    y_ref[...] = h
    for lo_col, hi_col in FFN_CHUNKS:
        convs = []
        for half in range(2):
            cols = slice(half * D_FF + lo_col, half * D_FF + hi_col)
            xx_ref[halo:halo + tm, cols] = _dot(hn, wu_ref[:, cols])
            conv = xx_ref[base:base + tm, cols] * wc_ref[0:1, cols]
            for j in range(1, FFN_CONV):
                conv = conv + xx_ref[base + j * stride:base + j * stride + tm, cols] * wc_ref[j:j + 1, cols]
            convs.append(conv)
        act = _silu(convs[1]) * convs[0]
        y_ref[...] += _dot(act.astype(BF16), wd_ref[lo_col:hi_col, :])
    fn_ref[0, 0] = xx_ref[tm + base:tm + halo, :]
    xx_ref[0:halo, :] = xx_ref[tm:tm + halo, :]
    y_ref[...] = _rms(y_ref[...], gl_ref[...])


def _gate_expand():
    ge = np.zeros((N_BRANCH, LANES, NSA_WIDTH), np.float32)
    for hd in range(NSA_HEADS):
        for br in range(N_BRANCH):
            ge[br, GATE_COL_NSA + hd * N_BRANCH + br, hd * HEAD_DIM:(hd + 1) * HEAD_DIM] = 1.0
    return jnp.asarray(ge, BF16)


def _ffn(x2d, om, oc, osel, ow, gt, fbuf, w_out, g_nsa, g_ffn, g_final, w_up, w_fconv, w_down,
         *, nb, tm, stride):
    rows = x2d.shape[0]
    ns = rows // (nb * tm)
    halo = -(-(FFN_CONV - 1) * stride // SUBLANES) * SUBLANES
    assert tm >= halo and all((hi - lo) % MXU_DEPTH == 0 for lo, hi in FFN_CHUNKS)
    tok = lambda b, s: (b * ns + s, 0)
    nfb = (FFN_CONV - 1) * stride

    def const(shape):
        return pl.BlockSpec(shape, lambda b, s: (0,) * len(shape))

    hbm = pl.BlockSpec(memory_space=pl.ANY)
    y, fn = pl.pallas_call(
        functools.partial(_ffn_body, tm=tm, stride=stride, halo=halo),
        grid=(nb, ns),
        in_specs=[pl.BlockSpec((tm, D_MODEL), tok)] + [pl.BlockSpec((tm, NSA_WIDTH), tok)] * 4
        + [pl.BlockSpec((tm, LANES), tok),
           const((N_BRANCH, LANES, NSA_WIDTH)), const((1, NSA_WIDTH)), const((1, D_MODEL)),
           const((1, D_MODEL)), const((FFN_CONV, 2 * D_FF)),
           pl.BlockSpec((1, nfb, 2 * D_FF), lambda b, s: (b, 0, 0)), hbm, hbm, hbm],
        out_specs=[pl.BlockSpec((tm, D_MODEL), tok),
                   pl.BlockSpec((1, 1, nfb, 2 * D_FF), lambda b, s: (b, s, 0, 0))],
        out_shape=[jax.ShapeDtypeStruct((rows, D_MODEL), F32),
                   jax.ShapeDtypeStruct((nb, ns, nfb, 2 * D_FF), F32)],
        scratch_shapes=[pltpu.VMEM((halo + tm, 2 * D_FF), F32),
                        pltpu.VMEM((D_MODEL, D_MODEL), BF16), pltpu.VMEM((D_MODEL, 2 * D_FF), BF16),
                        pltpu.VMEM((D_FF, D_MODEL), BF16), pltpu.SemaphoreType.DMA((3,))],
        compiler_params=pltpu.CompilerParams(dimension_semantics=("arbitrary", "arbitrary"),
                                             vmem_limit_bytes=FFN_VMEM_LIMIT),
        name="outproj_ffn",
    )(x2d, om, oc, osel, ow, gt, _gate_expand(), g_nsa.reshape(1, -1), g_ffn.reshape(1, -1),
      g_final.reshape(1, -1), w_fconv, fbuf, w_out.astype(BF16), w_up.astype(BF16),
      w_down.astype(BF16))
    return y, fn[:, ns - 1]


PROMPT_TM = 512
PROMPT_TQ_CMP = 512
PROMPT_TT_TOPK = 1024
PROMPT_TQ_SEL = 512
PROMPT_TK_SEL = 512
PROMPT_TQ_WIN = 256


def _kv_rows(kv_t):
    batch, _, rows = kv_t.shape
    return kv_t.reshape(batch, 2, NSA_KV_HEADS, HEAD_DIM, rows).transpose(0, 4, 1, 2, 3)


def _kv_feature_major(kv5):
    batch, rows = kv5.shape[:2]
    return kv5.transpose(0, 2, 3, 4, 1).reshape(batch, 2 * KV_WIDTH, rows)


def _prompt_layer(x, wts):
    batch, seq, _ = x.shape
    x2d = x.reshape(batch * seq, D_MODEL)
    q, kc_rows, vc_rows, mu, mv, mo, gt, ks_rows, kw_rows, kvc_t, kvs_t, kvw_t, vs_t, vw_t = _in_proj(
        x2d, wts["g_mix"], wts["w_in_packed"], batch=batch, seq=seq, tm=min(PROMPT_TM, seq))
    H, DH, W = MLSTM_HEADS, MLSTM_DH, MLSTM_WIDTH
    o_m, mconv, c_new, n_new, m_new = _mlstm(
        mu, mv, mo, gt, jnp.zeros((batch, MLSTM_CONV - 1, W), F32), jnp.zeros((batch, H, DH, DH), F32),
        jnp.zeros((batch, H, DH), F32), jnp.zeros((batch, H), F32),
        wts["w_mconv"], wts["b_mconv"], wts["w_mq"], wts["w_mk"], wts["b_ig"], wts["b_fg"],
        wts["g_mhead"], wts["m_skip"], batch=batch, seq=seq)
    kce, kco = _compress_prompt(kc_rows, vc_rows, wts["cw"], batch=batch, seq=seq)
    n_sel = -(-seq // SEL_BLOCK)
    assert n_sel <= SEL_BLOCK
    o_cmp, scores_t = _cmp_attn(q, kce, kco, batch=batch, seq=seq, tq=min(PROMPT_TQ_CMP, seq), pos0=0)
    selb = _topk_blocks(scores_t.reshape(batch * NSA_KV_HEADS, -1, seq),
                        jnp.arange(seq, dtype=jnp.int32).reshape(1, seq),
                        n_sel=n_sel, nsw=SEL_BLOCK, tt=min(PROMPT_TT_TOPK, seq))
    selb = selb.reshape(batch, NSA_KV_HEADS, SEL_BLOCK, seq)
    q3d = q.reshape(batch, seq, NSA_WIDTH)
    o_sel = _attn_selected_prompt(q3d, ks_rows, vs_t, selb, tq=min(PROMPT_TQ_SEL, seq),
                                  tk=min(PROMPT_TK_SEL, seq))
    o_win = _attn_window_prompt(q3d, kw_rows, vw_t, tq=PROMPT_TQ_WIN)
    fbuf = jnp.zeros((batch, FFN_CONV - 1, 2 * D_FF), F32)
    y, f_new = _ffn(x2d, o_m, o_cmp, o_sel.reshape(-1, NSA_WIDTH), o_win.reshape(-1, NSA_WIDTH), gt,
                    fbuf, wts["w_out"], wts["g_nsa"], wts["g_ffn"], wts["g_final"], wts["w_up"],
                    wts["w_fconv"], wts["w_down"], nb=batch, tm=min(FFN_TM, seq), stride=1)
    n_win = min(WINDOW, seq)
    return (y.reshape(batch, seq, D_MODEL), _kv_rows(kvc_t), _kv_rows(kvs_t),
            _kv_rows(kvw_t[:, :, seq - n_win:]), mconv, c_new, n_new, m_new.reshape(batch, H), f_new)


def _decode_rows(q2d, batch, seq):
    q5 = (q2d * ATTN_SCALE).reshape(batch, seq, NSA_KV_HEADS, NSA_GROUP, HEAD_DIM).transpose(0, 2, 3, 1, 4)
    eye = jnp.eye(NSA_KV_HEADS, dtype=F32)
    qa = jnp.einsum('bkgtd,kK->bkgtKd', q5, eye)
    return qa.reshape(batch, NSA_KV_HEADS * NSA_GROUP * seq, KV_WIDTH)


def _decode_rows_out(o, batch, seq):
    o6 = o.reshape(batch, NSA_KV_HEADS, NSA_GROUP, seq, NSA_KV_HEADS, HEAD_DIM)
    o5 = jnp.stack([o6[:, kh, :, :, kh, :] for kh in range(NSA_KV_HEADS)], axis=1)
    return o5.transpose(0, 3, 1, 2, 4).reshape(batch * seq, NSA_WIDTH)


def _sample_layer(x, pool_cmp, pool_sel, win_buf, m_conv, m_c, m_n, m_m, f_buf, page_table, wts):
    batch, seq, _ = x.shape
    n_pages = page_table.shape[1]
    past = n_pages * PAGE_SIZE
    assert past % SEL_BLOCK == 0 and seq <= SEL_BLOCK and seq < CMP_BLOCK
    x2d = x.reshape(batch * seq, D_MODEL)
    q, _, _, mu, mv, mo, gt, _, _, kvc_t, kvs_t, kvw_t, _, _ = _in_proj(
        x2d, wts["g_mix"], wts["w_in_packed"], batch=1, seq=batch * seq, tm=batch * seq)
    per_batch = lambda a: a.reshape(2 * KV_WIDTH, batch, seq).transpose(1, 0, 2)
    kvc_t, kvs_t, kvw_t = per_batch(kvc_t), per_batch(kvs_t), per_batch(kvw_t)
    pad_keys = lambda a: jnp.pad(a, ((0, 0), (0, 0), (0, LANES - seq)))
    H = MLSTM_HEADS
    o_m, mconv, c_new, n_new, m_new = _mlstm(
        mu, mv, mo, gt, m_conv, m_c, m_n, m_m,
        wts["w_mconv"], wts["b_mconv"], wts["w_mq"], wts["w_mk"], wts["b_ig"], wts["b_fg"],
        wts["g_mhead"], wts["m_skip"], batch=batch, seq=seq)
    pool_cmp3, pool_sel3 = _kv_feature_major(pool_cmp), _kv_feature_major(pool_sel)
    kce, kco = _compress_paged(pool_cmp3, page_table, wts["cw"], wts["cw_pages"])
    n_past_blk = past // SEL_BLOCK
    n_sel = -(-(past + seq) // SEL_BLOCK)
    o_cmp, scores_t = _cmp_attn(q, kce, kco, batch=batch, seq=seq, tq=seq, pos0=past)
    ns = scores_t.shape[2]
    nsw = ns + LANES
    scores_all = scores_t.transpose(1, 2, 0, 3).reshape(NSA_KV_HEADS, ns, batch * seq)
    pos_all = (past + jnp.arange(batch * seq, dtype=jnp.int32) % seq).reshape(1, batch * seq)
    selb = _topk_blocks(scores_all, pos_all, n_sel=n_sel, nsw=nsw, tt=batch * seq)
    selb = selb.reshape(NSA_KV_HEADS, nsw, batch, seq).transpose(2, 0, 3, 1)
    qa = _decode_rows(q, batch, seq)
    rows = qa.shape[1]
    blk_per_step = ATTN_PAGES_PER_STEP * PAGE_SIZE // SEL_BLOCK
    n_steps = n_pages // ATTN_PAGES_PER_STEP
    sb_rows = jnp.broadcast_to(selb[:, :, None], (batch, NSA_KV_HEADS, NSA_GROUP, seq, selb.shape[-1]))
    sb_rows = sb_rows.reshape(batch, rows, selb.shape[-1])
    bias_q = sb_rows[:, :, :n_past_blk].reshape(batch, rows, n_steps, blk_per_step).transpose(0, 2, 1, 3)
    bias_q = jnp.pad(bias_q, ((0, 0), (0, 0), (0, 0), (0, LANES - blk_per_step)))
    bias_new = jnp.broadcast_to(sb_rows[:, :, n_past_blk:n_past_blk + 1], (batch, rows, LANES))
    o_sel = _attn_paged(qa, bias_q, bias_new, pad_keys(kvs_t), pool_sel3, page_table, n_new=seq)
    n_buf = win_buf.shape[1]
    assert past >= n_buf
    win_t = _kv_feature_major(win_buf)
    o_win = _attn_window_small(qa, win_t, pad_keys(kvw_t), n_new=seq, past=past)
    win_new = jnp.concatenate([win_t, kvw_t], axis=2)[:, :, seq:]
    tmaj = lambda a: a.reshape(batch, seq, -1).transpose(1, 0, 2).reshape(batch * seq, -1)
    fb_t = f_buf.transpose(1, 0, 2).reshape(1, (FFN_CONV - 1) * batch, 2 * D_FF)
    y, f_new = _ffn(tmaj(x2d), tmaj(o_m), tmaj(o_cmp), tmaj(_decode_rows_out(o_sel, batch, seq)),
                    tmaj(_decode_rows_out(o_win, batch, seq)), tmaj(gt), fb_t,
                    wts["w_out"], wts["g_nsa"], wts["g_ffn"], wts["g_final"], wts["w_up"],
                    wts["w_fconv"], wts["w_down"], nb=1, tm=batch * seq, stride=batch)
    y = y.reshape(seq, batch, D_MODEL).transpose(1, 0, 2)
    f_new = f_new.reshape(FFN_CONV - 1, batch, 2 * D_FF).transpose(1, 0, 2)
    return (y, _kv_rows(kvc_t), _kv_rows(kvs_t), _kv_rows(win_new), mconv, c_new, n_new,
            m_new.reshape(batch, H), f_new)


def kernel(x_prompt, x_sample, cache_cmp, cache_sel, state_win, state_mlstm_C, state_mlstm_n,
           state_mlstm_m, state_mlstm_conv, state_ffn_conv, page_table,
           g_mix, w_in, w_out, w_mconv, b_mconv, w_mq, w_mk, b_ig, b_fg, g_mhead, m_skip,
           pe_cmp, w_cmp1, w_cmp2, g_nsa, g_ffn, w_up, w_fconv, w_down, g_final):
    assert w_in.shape[0] == 1, "one layer: the final norm is fused into the layer's FFN kernel"
    l = 0
    wts = dict(g_mix=g_mix[l], w_in_packed=_pack_w_in(w_in[l]), w_out=w_out[l], w_mconv=w_mconv[l],
               b_mconv=b_mconv[l], w_mq=w_mq[l], w_mk=w_mk[l], b_ig=b_ig[l], b_fg=b_fg[l],
               g_mhead=g_mhead[l], m_skip=m_skip[l],
               cw=_pack_compress_weights(pe_cmp[l], w_cmp1[l], w_cmp2[l]),
               cw_pages=_page_pair_constants(pe_cmp[l]),
               g_nsa=g_nsa[l], g_ffn=g_ffn[l], g_final=g_final, w_up=w_up[l], w_fconv=w_fconv[l],
               w_down=w_down[l])
    p = _prompt_layer(x_prompt, wts)
    s = _sample_layer(x_sample, cache_cmp[l], cache_sel[l], state_win[l], state_mlstm_conv[l],
                      state_mlstm_C[l], state_mlstm_n[l], state_mlstm_m[l], state_ffn_conv[l],
                      page_table, wts)
    yp, cmp_p, sel_p, win_p, mconv_p, c_p, n_p, m_p, fconv_p = p
    ys, cmp_s, sel_s, win_s, mconv_s, c_s, n_s, m_s, fconv_s = s
    st = lambda a: a[None]
    return (yp, ys, st(cmp_p), st(cmp_s), st(sel_p), st(sel_s), st(win_p), st(win_s),
            st(c_p), st(c_s), st(n_p), st(n_s), st(m_p), st(m_s), st(mconv_p), st(mconv_s),
            st(fconv_p), st(fconv_s))
```

```python
import functools

import numpy as np
import jax
import jax.numpy as jnp
from jax import lax
from jax.experimental import pallas as pl
from jax.experimental.pallas import tpu as pltpu

F32 = jnp.float32
BF16 = jnp.bfloat16

D_MODEL = 1024
PAGE_SIZE = 128
HEAD_DIM = 64
NSA_HEADS = 8
NSA_KV_HEADS = 2
NSA_GROUP = NSA_HEADS // NSA_KV_HEADS
NSA_WIDTH = NSA_HEADS * HEAD_DIM
KV_WIDTH = NSA_KV_HEADS * HEAD_DIM
CMP_BLOCK = 32
CMP_HIDDEN = 2 * HEAD_DIM
SEL_BLOCK = 64
TOP_N = 16
WINDOW = 512
N_BRANCH = 3
ATTN_SCALE = HEAD_DIM ** -0.5
MLSTM_HEADS = 4
MLSTM_WIDTH = D_MODEL - NSA_WIDTH
MLSTM_DH = MLSTM_WIDTH // MLSTM_HEADS
MLSTM_CONV = 4
D_FF = ((8 * D_MODEL // 3 + 127) // 128) * 128
FFN_CONV = 3
EPS = 1e-6
NEG_INF = -1e30
SEL_PRIORITY = 1e4
LOG2_E = 1.4426950408889634

LANES = 128
SUBLANES = 8
VMEM_LIMIT = 48 * 1024 * 1024

GATE_COL_NSA = 0
GATE_COL_I = NSA_HEADS * N_BRANCH
GATE_COL_F = GATE_COL_I + MLSTM_HEADS

MLSTM_CHUNK = 128
MLSTM_SEQS_PER_STEP = 4


def _cparams(sem):
    return pltpu.CompilerParams(dimension_semantics=sem, vmem_limit_bytes=VMEM_LIMIT)


def _dot(a, b):
    return jnp.dot(a, b, preferred_element_type=F32)


def _dot_nt(a, b):
    return lax.dot_general(a, b, (((1,), (1,)), ((), ())), preferred_element_type=F32)


def _sigmoid(x):
    return 1.0 / (1.0 + jnp.exp(-x))


def _silu(x):
    return x * _sigmoid(x)


def _rms(x, g):
    return x * lax.rsqrt(jnp.mean(x * x, axis=-1, keepdims=True) + EPS) * g


IN_ROW_WIDTHS = (NSA_WIDTH, KV_WIDTH, KV_WIDTH, MLSTM_WIDTH, MLSTM_WIDTH, MLSTM_WIDTH, LANES,
                 KV_WIDTH, KV_WIDTH)
IN_ROW_DTYPES = (F32,) * 7 + (BF16,) * 2
N_KV_BRANCH = 3


def _inproj_body(x_ref, g_ref, w_ref, wt_ref, *out_refs):
    xb = _rms(x_ref[...], g_ref[...]).astype(BF16)
    off = 0
    n_rows = len(IN_ROW_WIDTHS)
    for ref in out_refs[:n_rows]:
        n = ref.shape[-1]
        ref[...] = _dot(xb, w_ref[:, off:off + n]).astype(ref.dtype)
        off += n
    kv_refs = out_refs[n_rows:n_rows + N_KV_BRANCH]
    vt_refs = out_refs[n_rows + N_KV_BRANCH:]
    for n, ref in enumerate(kv_refs):
        kv_t = _dot_nt(wt_ref[n * 2 * KV_WIDTH:(n + 1) * 2 * KV_WIDTH, :], xb)
        ref[0] = kv_t
        if n > 0:
            vt_refs[n - 1][0] = kv_t[KV_WIDTH:, :].astype(BF16)


def _pack_w_in(w_in):
    splits = np.cumsum([NSA_WIDTH, 2 * KV_WIDTH, 2 * KV_WIDTH, 2 * KV_WIDTH, NSA_HEADS * N_BRANCH,
                        MLSTM_WIDTH, MLSTM_WIDTH, MLSTM_WIDTH, MLSTM_HEADS]).tolist()
    q, kvc, kvs, kvw, gt, mu, mv, mo, mi, mf = jnp.split(w_in, splits, axis=1)
    gates = jnp.concatenate([gt, mi, mf], axis=1)
    gates = jnp.pad(gates, ((0, 0), (0, LANES - gates.shape[1])))
    w_rows = jnp.concatenate([q, kvc, mu, mv, mo, gates, kvs[:, :KV_WIDTH], kvw[:, :KV_WIDTH]],
                             axis=1).astype(BF16)
    w_kv_t = jnp.concatenate([kvc, kvs, kvw], axis=1).T.astype(BF16)
    return w_rows, w_kv_t


def _in_proj(x2d, g_mix, w_packed, *, batch, seq, tm):
    w_rows, w_kv_t = w_packed
    t = x2d.shape[0]
    ns = seq // tm
    kv_sd = jax.ShapeDtypeStruct((batch, 2 * KV_WIDTH, seq), F32)
    vt_sd = jax.ShapeDtypeStruct((batch, KV_WIDTH, seq), BF16)
    feat_major = lambda rows: pl.BlockSpec((1, rows, tm), lambda i: (i // ns, 0, i % ns))
    return pl.pallas_call(
        _inproj_body,
        grid=(t // tm,),
        in_specs=[pl.BlockSpec((tm, D_MODEL), lambda i: (i, 0)),
                  pl.BlockSpec((1, D_MODEL), lambda i: (0, 0)),
                  pl.BlockSpec(w_rows.shape, lambda i: (0, 0)),
                  pl.BlockSpec(w_kv_t.shape, lambda i: (0, 0))],
        out_specs=[pl.BlockSpec((tm, n), lambda i: (i, 0)) for n in IN_ROW_WIDTHS]
        + [feat_major(2 * KV_WIDTH)] * N_KV_BRANCH + [feat_major(KV_WIDTH)] * (N_KV_BRANCH - 1),
        out_shape=[jax.ShapeDtypeStruct((t, n), dt) for n, dt in zip(IN_ROW_WIDTHS, IN_ROW_DTYPES)]
        + [kv_sd] * N_KV_BRANCH + [vt_sd] * (N_KV_BRANCH - 1),
        compiler_params=_cparams(("arbitrary",)),
        name="in_proj",
    )(x2d, g_mix.reshape(1, D_MODEL), w_rows, w_kv_t)


def _mlstm_body(*refs, valid, bb):
    cb_ref, c0_ref, n0_ref, m0_ref = refs[4:8]
    cn_ref, c_ref, n_ref, m_ref, xx_ref = refs[16:21]
    halo = SUBLANES

    @pl.when(pl.program_id(1) == 0)
    def _():
        xx_ref[:, 0:halo, :] = jnp.zeros((bb, halo, MLSTM_WIDTH), F32)
        xx_ref[:, halo - (MLSTM_CONV - 1):halo, :] = cb_ref[...]
        c_ref[...] = c0_ref[...]
        n_ref[...] = n0_ref[...]
        m_ref[...] = m0_ref[...]

    _mlstm_chunk(*refs, valid=valid, bb=bb)


def _mlstm_chunk(mu_ref, mv_ref, mo_ref, g_ref, cb_ref, c0_ref, n0_ref, m0_ref,
                 wc_ref, bc_ref, wq_ref, wk_ref, gb_ref, gh_ref, sk_ref,
                 o_ref, cn_ref, c_ref, n_ref, m_ref,
                 xx_ref, vpad_ref, gpad_ref, *, valid, bb):
    L = MLSTM_CHUNK
    DH = MLSTM_DH
    halo = SUBLANES
    units = [(bi, h) for bi in range(bb) for h in range(MLSTM_HEADS)]
    head_lanes = lambda h: slice(h * DH, (h + 1) * DH)
    row = lax.broadcasted_iota(jnp.int32, (L, L), 0)
    col = lax.broadcasted_iota(jnp.int32, (L, L), 1)
    tril = row >= col
    triu = row <= col
    tok_col = lax.broadcasted_iota(jnp.int32, (L, 1), 0)
    tok_row = lax.broadcasted_iota(jnp.int32, (1, L), 1)

    def log_sigmoid(x):
        return jnp.minimum(x, 0.0) - jnp.log(1.0 + jnp.exp(-jnp.abs(x)))

    uc, gb, gbt = {}, {}, {}
    for bi in range(bb):
        if valid < L:
            xx_ref[bi, halo:, :] = jnp.zeros((L, MLSTM_WIDTH), F32)
            vpad_ref[bi] = jnp.zeros((L, MLSTM_WIDTH), F32)
            gpad_ref[bi] = jnp.zeros((L, LANES), F32)
        xx_ref[bi, halo:halo + valid, :] = mu_ref[bi]
        vpad_ref[bi, 0:valid, :] = mv_ref[bi]
        gpad_ref[bi, 0:valid, :] = g_ref[bi]
        conv = xx_ref[bi, halo - 3:halo - 3 + L, :] * wc_ref[0:1, :]
        for j in range(1, MLSTM_CONV):
            conv = conv + xx_ref[bi, halo - 3 + j:halo - 3 + j + L, :] * wc_ref[j:j + 1, :]
        uc[bi] = _silu(conv + bc_ref[...])
        tail = xx_ref[bi, valid + halo - 3:valid + halo, :]
        xx_ref[bi, halo - 3:halo, :] = tail
        cn_ref[bi] = tail
        gb[bi] = gpad_ref[bi] + gb_ref[...]
        gbt[bi] = gb[bi].T

    q, k, qb, kb = {}, {}, {}, {}
    for u in units:
        bi, h = u
        ub = uc[bi][:, head_lanes(h)].astype(BF16)
        q[u] = _dot(ub, wq_ref[h])
        k[u] = _dot(ub, wk_ref[h]) * (DH ** -0.5)
        qb[u], kb[u] = q[u].astype(BF16), k[u].astype(BF16)

    ic_col, ic_row, cum_col, cum_row = {}, {}, {}, {}
    for u in units:
        bi, h = u
        ic_c = gb[bi][:, GATE_COL_I + h:GATE_COL_I + h + 1]
        ic_r = gbt[bi][GATE_COL_I + h:GATE_COL_I + h + 1, :]
        lf_c = log_sigmoid(gb[bi][:, GATE_COL_F + h:GATE_COL_F + h + 1])
        lf_r = log_sigmoid(gbt[bi][GATE_COL_F + h:GATE_COL_F + h + 1, :])
        if valid < L:
            ic_c = jnp.where(tok_col < valid, ic_c, NEG_INF)
            ic_r = jnp.where(tok_row < valid, ic_r, NEG_INF)
            lf_c = jnp.where(tok_col < valid, lf_c, 0.0)
            lf_r = jnp.where(tok_row < valid, lf_r, 0.0)
        ic_col[u], ic_row[u] = ic_c, ic_r
        cum_col[u] = jnp.sum(jnp.where(tril, lf_r, 0.0), axis=1, keepdims=True)
        cum_row[u] = jnp.sum(jnp.where(triu, lf_c, 0.0), axis=0, keepdims=True)

    m_t, w, sc = {}, {}, {}
    for u in units:
        bi, h = u
        m0 = m_ref[bi, 0:1, h:h + 1]
        dmat = jnp.where(tril, cum_col[u] - cum_row[u] + ic_row[u], NEG_INF)
        inter = cum_col[u] + m0
        m_t[u] = jnp.maximum(inter, jnp.max(dmat, axis=1, keepdims=True))
        w[u] = jnp.exp(dmat - m_t[u])
        sc[u] = jnp.exp(inter - m_t[u])

    hc = {}
    for u in units:
        bi, h = u
        s = _dot_nt(qb[u], kb[u]) * w[u]
        v = vpad_ref[bi, :, head_lanes(h)]
        c_old = c_ref[bi, h]
        n_old = n_ref[bi, h:h + 1, :]
        num = _dot(s.astype(BF16), v.astype(BF16)) + sc[u] * _dot_nt(qb[u], c_old.astype(BF16))
        den = (jnp.sum(s, axis=1, keepdims=True)
               + sc[u] * jnp.sum(q[u] * n_old, axis=1, keepdims=True))
        hc[u] = num / jnp.maximum(jnp.abs(den), jnp.exp(-m_t[u]))

    for u in units:
        bi, h = u
        m0 = m_ref[bi, 0:1, h:h + 1]
        m_new = m_t[u][L - 1:L, :]
        cum_last = cum_col[u][L - 1:L, :]
        wl = jnp.exp(cum_last - cum_col[u] + ic_col[u] - m_new)
        sl = jnp.exp(cum_last + m0 - m_new)
        v = vpad_ref[bi, :, head_lanes(h)]
        vw_t = (v * wl).T.astype(BF16)
        c_ref[bi, h] = sl * c_ref[bi, h] + _dot(vw_t, kb[u])
        n_ref[bi, h:h + 1, :] = sl * n_ref[bi, h:h + 1, :] + jnp.sum(wl * k[u], axis=0, keepdims=True)
        m_ref[bi, 0:1, h:h + 1] = m_new

    for u in units:
        bi, h = u
        hn = _rms(hc[u], gh_ref[:, head_lanes(h)])
        u_h = uc[bi][:, head_lanes(h)]
        out = ((hn[0:valid, :] + sk_ref[:, head_lanes(h)] * u_h[0:valid, :])
               * _sigmoid(mo_ref[bi, :, head_lanes(h)]))
        o_ref[bi, :, head_lanes(h)] = out


def _mlstm(mu, mv, mo, gates, conv_buf, c0, n0, m0, w_mconv, b_mconv, w_mq, w_mk, b_ig, b_fg,
           g_mhead, m_skip, *, batch, seq):
    L = MLSTM_CHUNK
    valid = min(seq, L)
    assert seq % valid == 0 and (valid == L or seq == valid)
    nc = seq // valid
    gate_bias = jnp.zeros((1, LANES), F32)
    gate_bias = gate_bias.at[0, GATE_COL_I:GATE_COL_I + MLSTM_HEADS].set(b_ig)
    gate_bias = gate_bias.at[0, GATE_COL_F:GATE_COL_F + MLSTM_HEADS].set(b_fg)
    bb = MLSTM_SEQS_PER_STEP
    assert batch % bb == 0
    tok = lambda b, c: (b, c, 0)
    const2 = lambda b, c: (0, 0)
    const3 = lambda b, c: (0, 0, 0)
    per_b3 = lambda b, c: (b, 0, 0)
    per_b4 = lambda b, c: (b, 0, 0, 0)
    H, DH, W = MLSTM_HEADS, MLSTM_DH, MLSTM_WIDTH
    rows3 = lambda a: a.reshape(batch, seq, a.shape[-1])
    o_m, conv_new, c_new, n_new, m_new = pl.pallas_call(
        functools.partial(_mlstm_body, valid=valid, bb=bb),
        grid=(batch // bb, nc),
        in_specs=[pl.BlockSpec((bb, valid, W), tok), pl.BlockSpec((bb, valid, W), tok),
                  pl.BlockSpec((bb, valid, W), tok), pl.BlockSpec((bb, valid, LANES), tok),
                  pl.BlockSpec((bb, MLSTM_CONV - 1, W), per_b3),
                  pl.BlockSpec((bb, H, DH, DH), per_b4),
                  pl.BlockSpec((bb, H, DH), per_b3),
                  pl.BlockSpec((bb, 1, H), per_b3),
                  pl.BlockSpec((MLSTM_CONV, W), const2), pl.BlockSpec((1, W), const2),
                  pl.BlockSpec((H, DH, DH), const3), pl.BlockSpec((H, DH, DH), const3),
                  pl.BlockSpec((1, LANES), const2), pl.BlockSpec((1, W), const2),
                  pl.BlockSpec((1, W), const2)],
        out_specs=[pl.BlockSpec((bb, valid, W), tok),
                   pl.BlockSpec((bb, MLSTM_CONV - 1, W), per_b3),
                   pl.BlockSpec((bb, H, DH, DH), per_b4),
                   pl.BlockSpec((bb, H, DH), per_b3),
                   pl.BlockSpec((bb, 1, H), per_b3)],
        out_shape=[jax.ShapeDtypeStruct((batch, seq, W), F32),
                   jax.ShapeDtypeStruct((batch, MLSTM_CONV - 1, W), F32),
                   jax.ShapeDtypeStruct((batch, H, DH, DH), F32),
                   jax.ShapeDtypeStruct((batch, H, DH), F32),
                   jax.ShapeDtypeStruct((batch, 1, H), F32)],
        scratch_shapes=[pltpu.VMEM((bb, SUBLANES + L, W), F32), pltpu.VMEM((bb, L, W), F32),
                        pltpu.VMEM((bb, L, LANES), F32)],
        compiler_params=_cparams(("arbitrary", "arbitrary")),
        name="mlstm",
    )(rows3(mu), rows3(mv), rows3(mo), rows3(gates), conv_buf, c0, n0, m0.reshape(batch, 1, H),
      w_mconv, b_mconv.reshape(1, W), w_mq.astype(BF16), w_mk.astype(BF16), gate_bias,
      g_mhead.reshape(1, W), m_skip.reshape(1, W))
    return o_m.reshape(batch * seq, W), conv_new, c_new, n_new, m_new


def _compress_rows(xk_ref, xv_ref, pe_ref, w1_ref, w2_ref, n_pairs):
    pair_rows = 2 * CMP_BLOCK
    outs = []
    for kv, x_ref in enumerate((xk_ref, xv_ref)):
        acc = jnp.zeros((2 * n_pairs, NSA_KV_HEADS * CMP_HIDDEN), F32)
        for r in range(CMP_BLOCK):
            ev = x_ref[pl.ds(r, n_pairs, stride=pair_rows), :]
            od = x_ref[pl.ds(CMP_BLOCK + r, n_pairs, stride=pair_rows), :]
            xr = jnp.concatenate([ev, od], axis=0) + pe_ref[kv, r:r + 1, :]
            acc = acc + _dot(xr.astype(BF16), w1_ref[kv, r])
        outs.append(_dot(_silu(acc).astype(BF16), w2_ref[kv]))
    return jnp.concatenate(outs, axis=1)


def _compress_body(xk_ref, xv_ref, pe_ref, w1_ref, w2_ref, oe_ref, oo_ref, *, n_pairs):
    out = _compress_rows(xk_ref, xv_ref, pe_ref, w1_ref, w2_ref, n_pairs)
    oe_ref[0] = out[0:n_pairs, :]
    oo_ref[0] = out[n_pairs:, :]


BLOCKS_PER_PAGE = PAGE_SIZE // CMP_BLOCK


def _gather_pages(pt_ref, pool_hbm, pages_ref, sem_ref, n_pages):
    g = pl.program_id(0) * pl.num_programs(1) + pl.program_id(1)
    n_total = pl.num_programs(0) * pl.num_programs(1)

    def copies(step, slot):
        return [pltpu.make_async_copy(pool_hbm.at[pt_ref[step * n_pages + j]], pages_ref.at[slot, j],
                                      sem_ref.at[slot]) for j in range(n_pages)]

    @pl.when(g == 0)
    def _():
        for cp in copies(0, 0):
            cp.start()

    @pl.when(g + 1 < n_total)
    def _():
        for cp in copies(g + 1, (g + 1) % 2):
            cp.start()

    slot = g % 2
    for cp in copies(g, slot):
        cp.wait()
    return slot


def _compress_paged_body(pt_ref, pool_hbm, pet_ref, perm_ref, w1_ref, w2_ref, oe_ref, oo_ref,
                         buf_ref, os_ref, pages_ref, sem_ref, *, n_pages):
    slot = _gather_pages(pt_ref, pool_hbm, pages_ref, sem_ref, n_pages)
    grp = 2 * BLOCKS_PER_PAGE
    for jp in range(n_pages // 2):
        xt = jnp.concatenate([pages_ref[slot, 2 * jp], pages_ref[slot, 2 * jp + 1]], axis=1)
        xb = (xt + pet_ref[...]).astype(BF16)
        xp = _dot_nt(perm_ref[...], xb)
        for r in range(CMP_BLOCK):
            for kv in range(2):
                lane0 = (2 * kv + r % 2) * KV_WIDTH
                buf_ref[r // 2, grp * jp:grp * (jp + 1), lane0:lane0 + KV_WIDTH] = (
                    xp[grp * r:grp * (r + 1), kv * KV_WIDTH:(kv + 1) * KV_WIDTH])
    for kv in range(2):
        lanes = slice(2 * kv * KV_WIDTH, 2 * (kv + 1) * KV_WIDTH)
        acc = _dot(buf_ref[0, :, lanes].astype(BF16), w1_ref[kv, 0])
        for r2 in range(1, CMP_BLOCK // 2):
            acc = acc + _dot(buf_ref[r2, :, lanes].astype(BF16), w1_ref[kv, r2])
        os_ref[kv] = _dot(_silu(acc).astype(BF16), w2_ref[kv])
    half = os_ref.shape[1] // 2
    for parity, ref in enumerate((oe_ref, oo_ref)):
        ref[0] = jnp.concatenate([os_ref[kv, pl.ds(parity, half, stride=2), :] for kv in range(2)],
                                 axis=1)


def _page_pair_constants(pe):
    pe_t = jnp.broadcast_to(pe.transpose(0, 2, 1)[:, None, :, None, :],
                            (2, NSA_KV_HEADS, HEAD_DIM, 2 * BLOCKS_PER_PAGE, CMP_BLOCK))
    pe_t = pe_t.reshape(2 * KV_WIDTH, 2 * PAGE_SIZE)
    grp = 2 * BLOCKS_PER_PAGE
    perm = np.zeros((2 * PAGE_SIZE, 2 * PAGE_SIZE), np.float32)
    for r in range(CMP_BLOCK):
        for b in range(grp):
            perm[r * grp + b, b * CMP_BLOCK + r] = 1.0
    return pe_t, jnp.asarray(perm, BF16)


def _pack_compress_weights(pe, w1, w2):
    eye_h = jnp.eye(NSA_KV_HEADS, dtype=F32)
    pe_r = jnp.broadcast_to(pe[:, :, None, :], (2, CMP_BLOCK, NSA_KV_HEADS, HEAD_DIM))
    pe_r = pe_r.reshape(2, CMP_BLOCK, KV_WIDTH)
    w1r = w1.reshape(2, CMP_BLOCK, HEAD_DIM, CMP_HIDDEN)
    w1_big = jnp.einsum('krdc,hH->krhdHc', w1r, eye_h)
    w1_big = w1_big.reshape(2, CMP_BLOCK, KV_WIDTH, NSA_KV_HEADS * CMP_HIDDEN).astype(BF16)
    w2_big = jnp.einsum('kcd,hH->khcHd', w2, eye_h)
    w2_big = w2_big.reshape(2, NSA_KV_HEADS * CMP_HIDDEN, KV_WIDTH).astype(BF16)
    return pe_r, w1_big, w2_big


def _compress_prompt(k_rows, v_rows, cw, *, batch, seq):
    pe_r, w1_big, w2_big = cw
    n_pairs = seq // (2 * CMP_BLOCK)
    const3 = lambda b: (0, 0, 0)
    out_sd = jax.ShapeDtypeStruct((batch, n_pairs, 2 * KV_WIDTH), F32)
    return pl.pallas_call(
        functools.partial(_compress_body, n_pairs=n_pairs),
        grid=(batch,),
        in_specs=[pl.BlockSpec((seq, KV_WIDTH), lambda b: (b, 0)),
                  pl.BlockSpec((seq, KV_WIDTH), lambda b: (b, 0)),
                  pl.BlockSpec(pe_r.shape, const3),
                  pl.BlockSpec(w1_big.shape, lambda b: (0, 0, 0, 0)),
                  pl.BlockSpec(w2_big.shape, const3)],
        out_specs=[pl.BlockSpec((1, n_pairs, 2 * KV_WIDTH), lambda b: (b, 0, 0))] * 2,
        out_shape=[out_sd, out_sd],
        compiler_params=_cparams(("arbitrary",)),
        name="compress_prompt",
    )(k_rows, v_rows, pe_r, w1_big, w2_big)


COMPRESS_PAGES_PER_STEP = 64


def _compress_paged(pool, page_table, cw, cw_pages):
    _, w1_big, w2_big = cw
    w1_big = w1_big.reshape(2, CMP_BLOCK // 2, 2 * KV_WIDTH, NSA_KV_HEADS * CMP_HIDDEN)
    pe_t, perm = cw_pages
    batch, n_pages = page_table.shape
    pps = COMPRESS_PAGES_PER_STEP
    assert n_pages % pps == 0 and pps % 2 == 0
    n_steps = n_pages // pps
    n_blk = pps * BLOCKS_PER_PAGE
    const3 = lambda b, c, pt: (0, 0, 0)
    return pl.pallas_call(
        functools.partial(_compress_paged_body, n_pages=pps),
        grid_spec=pltpu.PrefetchScalarGridSpec(
            num_scalar_prefetch=1,
            grid=(batch, n_steps),
            in_specs=[pl.BlockSpec(memory_space=pl.ANY),
                      pl.BlockSpec(pe_t.shape, lambda b, c, pt: (0, 0)),
                      pl.BlockSpec(perm.shape, lambda b, c, pt: (0, 0)),
                      pl.BlockSpec(w1_big.shape, lambda b, c, pt: (0, 0, 0, 0)),
                      pl.BlockSpec(w2_big.shape, const3)],
            out_specs=[pl.BlockSpec((1, n_blk // 2, 2 * KV_WIDTH), lambda b, c, pt: (b, c, 0))] * 2,
            scratch_shapes=[pltpu.VMEM((CMP_BLOCK // 2, n_blk, 4 * KV_WIDTH), F32),
                            pltpu.VMEM((2, n_blk, KV_WIDTH), F32),
                            pltpu.VMEM((2, pps, 2 * KV_WIDTH, PAGE_SIZE), F32),
                            pltpu.SemaphoreType.DMA((2,))]),
        out_shape=[jax.ShapeDtypeStruct((batch, n_steps * n_blk // 2, 2 * KV_WIDTH), F32)] * 2,
        compiler_params=_cparams(("arbitrary", "arbitrary")),
        name="compress_paged",
    )(page_table.reshape(-1), pool, pe_t, perm, w1_big, w2_big)


def _cmp_attn_body(q_ref, ke_ref, ko_ref, o_ref, st_ref, *, tq, pos0):
    ns = ke_ref.shape[1]
    i = pl.program_id(1)
    rows = NSA_GROUP * tq
    tok0 = pos0 + i * tq
    pos_r = tok0 + lax.broadcasted_iota(jnp.int32, (1, rows), 1) % tq
    pair_c = lax.broadcasted_iota(jnp.int32, (ns, 1), 0)
    end_e = (2 * pair_c + 1) * CMP_BLOCK - 1
    end_o = (2 * pair_c + 2) * CMP_BLOCK - 1
    any_r = (CMP_BLOCK - 1 <= pos_r).astype(F32)
    contract_blocks = (((0,), (0,)), ((), ()))
    q = q_ref[...] * ATTN_SCALE
    for kh in range(NSA_KV_HEADS):
        qs = jnp.concatenate([q[:, (kh * NSA_GROUP + g) * HEAD_DIM:(kh * NSA_GROUP + g + 1) * HEAD_DIM]
                              for g in range(NSA_GROUP)], axis=0).astype(BF16)
        ks, vs = slice(kh * HEAD_DIM, (kh + 1) * HEAD_DIM), slice(KV_WIDTH + kh * HEAD_DIM,
                                                                   KV_WIDTH + (kh + 1) * HEAD_DIM)
        ke, ko = ke_ref[0, :, ks].astype(BF16), ko_ref[0, :, ks].astype(BF16)
        te = jnp.where(end_e <= pos_r, _dot_nt(ke, qs), NEG_INF)
        to = jnp.where(end_o <= pos_r, _dot_nt(ko, qs), NEG_INF)
        mt = jnp.maximum(jnp.max(te, axis=0, keepdims=True), jnp.max(to, axis=0, keepdims=True))
        pte, pto = jnp.exp(te - mt), jnp.exp(to - mt)
        invt = any_r / (jnp.sum(pte, axis=0, keepdims=True) + jnp.sum(pto, axis=0, keepdims=True))
        pte, pto = pte * invt, pto * invt
        oh = (lax.dot_general(pte.astype(BF16), ke_ref[0, :, vs].astype(BF16), contract_blocks,
                              preferred_element_type=F32)
              + lax.dot_general(pto.astype(BF16), ko_ref[0, :, vs].astype(BF16), contract_blocks,
                                preferred_element_type=F32))
        for g in range(NSA_GROUP):
            hd = kh * NSA_GROUP + g
            o_ref[:, hd * HEAD_DIM:(hd + 1) * HEAD_DIM] = oh[g * tq:(g + 1) * tq, :]
        ps = pte + pto
        score = ps[:, 0:tq]
        for g in range(1, NSA_GROUP):
            score = score + ps[:, g * tq:(g + 1) * tq]
        st_ref[0, kh] = score


def _cmp_attn(q2d, kce, kco, *, batch, seq, tq, pos0):
    ns = kce.shape[1]
    nq = seq // tq
    return pl.pallas_call(
        functools.partial(_cmp_attn_body, tq=tq, pos0=pos0),
        grid=(batch, nq),
        in_specs=[pl.BlockSpec((tq, NSA_WIDTH), lambda b, i: (b * nq + i, 0)),
                  pl.BlockSpec((1, ns, 2 * KV_WIDTH), lambda b, i: (b, 0, 0)),
                  pl.BlockSpec((1, ns, 2 * KV_WIDTH), lambda b, i: (b, 0, 0))],
        out_specs=[pl.BlockSpec((tq, NSA_WIDTH), lambda b, i: (b * nq + i, 0)),
                   pl.BlockSpec((1, NSA_KV_HEADS, ns, tq), lambda b, i: (b, 0, 0, i))],
        out_shape=[jax.ShapeDtypeStruct((batch * seq, NSA_WIDTH), F32),
                   jax.ShapeDtypeStruct((batch, NSA_KV_HEADS, ns, seq), F32)],
        compiler_params=_cparams(("arbitrary", "arbitrary")),
        name="cmp_attn",
    )(q2d, kce, kco)


def _topk_body(pos_ref, st_ref, b_ref, *, n_sel):
    score = st_ref[0]
    ns, tt = score.shape
    nsw = b_ref.shape[1]
    if nsw > ns:
        score = jnp.concatenate([score, jnp.zeros((nsw - ns, tt), F32)], axis=0)
    blk = lax.broadcasted_iota(jnp.int32, (nsw, 1), 0)
    blk_f = blk.astype(F32)
    cur = pos_ref[...] // SEL_BLOCK
    forced = (blk == 0) | (blk == cur) | (blk == cur - 1)
    pri = jnp.where(blk <= cur, jnp.where(forced, SEL_PRIORITY, score), -SEL_PRIORITY)
    pri = jnp.where(blk < n_sel, pri, -jnp.inf)
    bias = jnp.full((nsw, tt), NEG_INF, F32)
    for _ in range(min(TOP_N, n_sel)):
        top = jnp.max(pri, axis=0, keepdims=True)
        first = jnp.min(jnp.where(pri == top, blk_f, float(nsw)), axis=0, keepdims=True)
        hit = blk_f == first
        bias = jnp.where(hit, 0.0, bias)
        pri = jnp.where(hit, -jnp.inf, pri)
    b_ref[0] = bias


def _topk_blocks(scores_t, pos, *, n_sel, nsw, tt):
    groups, ns, tokens = scores_t.shape
    assert nsw >= max(ns, n_sel) and tokens % tt == 0
    return pl.pallas_call(
        functools.partial(_topk_body, n_sel=n_sel),
        grid=(groups, tokens // tt),
        in_specs=[pl.BlockSpec((1, tt), lambda g, i: (0, i)),
                  pl.BlockSpec((1, ns, tt), lambda g, i: (g, 0, i))],
        out_specs=pl.BlockSpec((1, nsw, tt), lambda g, i: (g, 0, i)),
        out_shape=jax.ShapeDtypeStruct((groups, nsw, tokens), F32),
        compiler_params=_cparams(("arbitrary", "arbitrary")),
        name="topk_blocks",
    )(pos, scores_t)


def _softmax_update(sc, vt_bf16, m_ref, l_ref, acc_ref):
    m_old = m_ref[...]
    m_new = jnp.maximum(m_old, jnp.max(sc, axis=1, keepdims=True))
    alpha = jnp.exp(m_old - m_new)
    pr = jnp.exp(sc - jnp.concatenate([m_new] * (sc.shape[1] // LANES), axis=1))
    l_ref[...] = alpha * l_ref[...] + jnp.sum(pr, axis=1, keepdims=True)
    acc_ref[...] = alpha * acc_ref[...] + _dot_nt(pr.astype(BF16), vt_bf16)
    m_ref[...] = m_new


def _softmax_init(m_ref, l_ref, acc_ref):
    m_ref[...] = jnp.full(m_ref.shape, NEG_INF, F32)
    l_ref[...] = jnp.zeros(l_ref.shape, F32)
    acc_ref[...] = jnp.zeros(acc_ref.shape, F32)


ATTN_TAB_COLS = 5


def _attn_pairs(seq, tq, tk):
    rows = []
    for i in range(seq // tq):
        t_lo, t_hi = i * tq, i * tq + tq - 1
        js = list(range(0, t_hi // tk + 1))
        for n, j in enumerate(js):
            rows.append((i, j, int(n == 0), int(n == len(js) - 1), int(j * tk + tk - 1 > t_lo)))
    return np.asarray(rows, np.int32)


def _attn_body(tab_ref, q_ref, k_ref, vt_ref, oh_ref, sb_ref, o_ref, qa_ref, m_ref, l_ref, acc_ref,
               *, tq, tk):
    p = pl.program_id(1)
    i, j, first, last, partial_tile = [tab_ref[ATTN_TAB_COLS * p + n] for n in range(ATTN_TAB_COLS)]
    cols = NSA_HEADS * tq

    @pl.when(first == 1)
    def _():
        for hd, piece in enumerate(_query_columns(q_ref[0], sb_ref[0], tq)):
            qa_ref[:, hd * tq:(hd + 1) * tq] = piece
        m_ref[...] = jnp.full(m_ref.shape, NEG_INF, F32)
        l_ref[...] = jnp.zeros(l_ref.shape, F32)
        acc_ref[...] = jnp.zeros(acc_ref.shape, F32)

    k_aug = jnp.concatenate([k_ref[...], oh_ref[...]], axis=1)
    sc = _dot(k_aug, qa_ref[...])
    vt = vt_ref[0]

    def update(sc):
        m_old = m_ref[...]
        m_new = jnp.maximum(m_old, jnp.max(sc, axis=0, keepdims=True))
        alpha = jnp.exp2(m_old - m_new)
        pr = jnp.exp2(sc - m_new)
        l_ref[...] = alpha * l_ref[...] + jnp.sum(pr, axis=0, keepdims=True)
        acc_ref[...] = alpha * acc_ref[...] + _dot(vt, pr.astype(BF16))
        m_ref[...] = m_new

    @pl.when(partial_tile == 1)
    def _():
        qpos = i * tq + (lax.broadcasted_iota(jnp.int32, (1, cols), 1) & (tq - 1))
        kpos = j * tk + lax.broadcasted_iota(jnp.int32, (tk, 1), 0)
        update(jnp.where(kpos <= qpos, sc, NEG_INF))

    @pl.when(partial_tile == 0)
    def _():
        update(sc)

    @pl.when(last == 1)
    def _():
        _store_heads(acc_ref[...] / l_ref[...], o_ref, tq)


def _query_columns(q, sel_bias, tq):
    q = q * (ATTN_SCALE * LOG2_E)
    zeros64 = jnp.zeros((HEAD_DIM, tq), F32)
    kv_head_rows = lambda x, kh: jnp.concatenate([x, zeros64] if kh == 0 else [zeros64, x], axis=0)
    pieces = []
    for m in range(NSA_HEADS // 2):
        q_t = q[:, m * LANES:(m + 1) * LANES].T
        for hd in (2 * m, 2 * m + 1):
            kh = hd // NSA_GROUP
            piece = kv_head_rows(q_t[(hd % 2) * HEAD_DIM:(hd % 2 + 1) * HEAD_DIM, :], kh)
            if sel_bias is not None:
                piece = jnp.concatenate([piece, kv_head_rows(sel_bias[kh], kh)], axis=0)
            pieces.append(piece.astype(BF16))
    return pieces


def _store_heads(o_t, o_ref, tq):
    for m in range(NSA_HEADS // 2):
        pair = jnp.concatenate(
            [o_t[(hd // NSA_GROUP) * HEAD_DIM:(hd // NSA_GROUP + 1) * HEAD_DIM, hd * tq:(hd + 1) * tq]
             for hd in (2 * m, 2 * m + 1)], axis=0)
        o_ref[0, :, m * LANES:(m + 1) * LANES] = pair.T


def _window_body(q_ref, *refs, tq, n_tiles):
    k_refs, v_refs, o_ref = refs[:n_tiles], refs[n_tiles:2 * n_tiles], refs[2 * n_tiles]
    i = pl.program_id(1)
    cols = NSA_HEADS * tq
    qa = jnp.concatenate(_query_columns(q_ref[0], None, tq), axis=1)
    qpos = i * tq + (lax.broadcasted_iota(jnp.int32, (1, cols), 1) & (tq - 1))
    scs = []
    for n, k_ref in enumerate(k_refs):
        kpos = (i - (n_tiles - 1) + n) * tq + lax.broadcasted_iota(jnp.int32, (tq, 1), 0)
        if n == n_tiles - 1:
            valid = kpos <= qpos
        elif n == 0:
            valid = (kpos > qpos - WINDOW) & (kpos >= 0)
        else:
            valid = kpos >= 0
        scs.append(jnp.where(valid, _dot(k_ref[...], qa), NEG_INF))
    mx = scs[0].max(axis=0, keepdims=True)
    for sc in scs[1:]:
        mx = jnp.maximum(mx, sc.max(axis=0, keepdims=True))
    l_sum, acc = None, None
    for sc, v_ref in zip(scs, v_refs):
        pr = jnp.exp2(sc - mx)
        part_l, part_acc = jnp.sum(pr, axis=0, keepdims=True), _dot(v_ref[0], pr.astype(BF16))
        l_sum = part_l if l_sum is None else l_sum + part_l
        acc = part_acc if acc is None else acc + part_acc
    _store_heads(acc / l_sum, o_ref, tq)


def _attn_window_prompt(q3d, k_rows, v_t, *, tq):
    batch, seq, _ = q3d.shape
    assert WINDOW % tq == 0 and seq % tq == 0 and tq & (tq - 1) == 0
    n_tiles = WINDOW // tq + 1
    nq = seq // tq
    tile = lambda n: (lambda i: jnp.maximum(i - (n_tiles - 1) + n, 0))
    return pl.pallas_call(
        functools.partial(_window_body, tq=tq, n_tiles=n_tiles),
        grid=(batch, nq),
        in_specs=[pl.BlockSpec((1, tq, NSA_WIDTH), lambda b, i: (b, i, 0))]
        + [pl.BlockSpec((tq, KV_WIDTH), lambda b, i, t=tile(n): (b * nq + t(i), 0)) for n in range(n_tiles)]
        + [pl.BlockSpec((1, KV_WIDTH, tq), lambda b, i, t=tile(n): (b, 0, t(i))) for n in range(n_tiles)],
        out_specs=pl.BlockSpec((1, tq, NSA_WIDTH), lambda b, i: (b, i, 0)),
        out_shape=jax.ShapeDtypeStruct((batch, seq, NSA_WIDTH), F32),
        compiler_params=_cparams(("arbitrary", "arbitrary")),
        name="attn_win",
    )(q3d, *([k_rows] * n_tiles), *([v_t] * n_tiles))


def _block_onehot(seq):
    blk = np.arange(seq)[:, None] // SEL_BLOCK
    return jnp.asarray((np.arange(LANES)[None, :] % SEL_BLOCK) == blk, BF16)


def _attn_selected_prompt(q3d, k_rows, v_t, selb, *, tq, tk):
    batch, seq, _ = q3d.shape
    assert tq & (tq - 1) == 0 and tk % LANES == 0 and tq % LANES == 0 and selb.shape[2] == SEL_BLOCK
    tab = _attn_pairs(seq, tq, tk)
    cols = NSA_HEADS * tq
    C = ATTN_TAB_COLS
    nk = seq // tk
    return pl.pallas_call(
        functools.partial(_attn_body, tq=tq, tk=tk),
        grid_spec=pltpu.PrefetchScalarGridSpec(
            num_scalar_prefetch=1,
            grid=(batch, tab.shape[0]),
            in_specs=[pl.BlockSpec((1, tq, NSA_WIDTH), lambda b, p, t: (b, t[C * p], 0)),
                      pl.BlockSpec((tk, KV_WIDTH), lambda b, p, t: (b * nk + t[C * p + 1], 0)),
                      pl.BlockSpec((1, KV_WIDTH, tk), lambda b, p, t: (b, 0, t[C * p + 1])),
                      pl.BlockSpec((tk, LANES), lambda b, p, t: (t[C * p + 1], 0)),
                      pl.BlockSpec((1, NSA_KV_HEADS, SEL_BLOCK, tq),
                                   lambda b, p, t: (b, 0, 0, t[C * p]))],
            out_specs=pl.BlockSpec((1, tq, NSA_WIDTH), lambda b, p, t: (b, t[C * p], 0)),
            scratch_shapes=[pltpu.VMEM((2 * LANES, cols), BF16), pltpu.VMEM((1, cols), F32),
                            pltpu.VMEM((1, cols), F32), pltpu.VMEM((KV_WIDTH, cols), F32)]),
        out_shape=jax.ShapeDtypeStruct((batch, seq, NSA_WIDTH), F32),
        compiler_params=_cparams(("arbitrary", "arbitrary")),
        name="attn_sel",
    )(jnp.asarray(tab.reshape(-1)), q3d, k_rows, v_t, _block_onehot(seq), selb)


ATTN_PAGES_PER_STEP = 32
ATTN_PAGED_SPLIT = 2


def _attn_paged_body(pt_ref, qa_ref, bq_ref, bn_ref, kn_ref, oh_ref, pool_hbm, o_ref, m_ref, l_ref,
                     acc_ref, pages_ref, sem_ref, *, n_pages, n_new):
    slot = _gather_pages(pt_ref, pool_hbm, pages_ref, sem_ref, n_pages)
    page_refs = [pages_ref.at[slot, j] for j in range(n_pages)]
    c = pl.program_id(1)
    rows = qa_ref.shape[1]

    @pl.when(c == 0)
    def _():
        _softmax_init(m_ref, l_ref, acc_ref)

    n_split = m_ref.shape[0]
    per = n_pages // n_split
    keys = per * PAGE_SIZE
    qa = qa_ref[0]
    lhs = jnp.concatenate([qa, bq_ref[0, 0]], axis=1).astype(BF16)
    scs, vts = [], []
    for s in range(n_split):
        refs_s = page_refs[s * per:(s + 1) * per]
        kt = jnp.concatenate([r[0:KV_WIDTH, :] for r in refs_s], axis=1)
        rhs = jnp.concatenate([kt.astype(BF16), oh_ref[:, s * keys:(s + 1) * keys]], axis=0)
        scs.append(_dot(lhs, rhs))
        vts.append(jnp.concatenate([r[KV_WIDTH:, :] for r in refs_s], axis=1).astype(BF16))
    for s in range(n_split):
        _softmax_update(scs[s], vts[s], m_ref.at[s], l_ref.at[s], acc_ref.at[s])

    @pl.when(c == pl.num_programs(1) - 1)
    def _():
        kn = kn_ref[0]
        sc = _dot(qa.astype(BF16), kn[0:KV_WIDTH, :].astype(BF16)) + bn_ref[0]
        tq = lax.broadcasted_iota(jnp.int32, (rows, 1), 0) % n_new
        kk = lax.broadcasted_iota(jnp.int32, (1, kn.shape[1]), 1)
        sc = jnp.where((kk <= tq) & (kk < n_new), sc, NEG_INF)
        _softmax_update(sc, kn[KV_WIDTH:, :].astype(BF16), m_ref.at[0], l_ref.at[0], acc_ref.at[0])
        m_all = m_ref[0]
        for s in range(1, n_split):
            m_all = jnp.maximum(m_all, m_ref[s])
        l_all = jnp.zeros(m_all.shape, F32)
        acc_all = jnp.zeros(m_all.shape, F32)
        for s in range(n_split):
            scale = jnp.exp(m_ref[s] - m_all)
            l_all = l_all + scale * l_ref[s]
            acc_all = acc_all + scale * acc_ref[s]
        o_ref[0] = acc_all / l_all


def _attn_paged(qa, bias_q, bias_new, kv_new_t, pool, page_table, *, n_new):
    batch, n_pages = page_table.shape
    pps = ATTN_PAGES_PER_STEP
    assert n_pages % pps == 0 and pps * PAGE_SIZE // SEL_BLOCK <= SEL_BLOCK
    n_steps = n_pages // pps
    rows = qa.shape[1]

    per_b = lambda b, c, pt: (b, 0, 0)
    return pl.pallas_call(
        functools.partial(_attn_paged_body, n_pages=pps, n_new=n_new),
        grid_spec=pltpu.PrefetchScalarGridSpec(
            num_scalar_prefetch=1,
            grid=(batch, n_steps),
            in_specs=[pl.BlockSpec((1, rows, LANES), per_b),
                      pl.BlockSpec((1, 1, rows, LANES), lambda b, c, pt: (b, c, 0, 0)),
                      pl.BlockSpec((1, rows, LANES), per_b),
                      pl.BlockSpec((1,) + kv_new_t.shape[1:], per_b),
                      pl.BlockSpec((LANES, pps * PAGE_SIZE), lambda b, c, pt: (0, 0)),
                      pl.BlockSpec(memory_space=pl.ANY)],
            out_specs=pl.BlockSpec((1, rows, LANES), per_b),
            scratch_shapes=[pltpu.VMEM((ATTN_PAGED_SPLIT, rows, LANES), F32)] * 3
            + [pltpu.VMEM((2, pps, 2 * KV_WIDTH, PAGE_SIZE), F32), pltpu.SemaphoreType.DMA((2,))]),
        out_shape=jax.ShapeDtypeStruct((batch, rows, LANES), F32),
        compiler_params=_cparams(("arbitrary", "arbitrary")),
        name="attn_sel_paged",
    )(page_table.reshape(-1), qa, bias_q, bias_new, kv_new_t, _block_onehot(pps * PAGE_SIZE).T, pool)


def _attn_window_body(qa_ref, wb_ref, kn_ref, o_ref, *, n_new, past):
    qa = qa_ref[0].astype(BF16)
    wb, kn = wb_ref[0], kn_ref[0]
    rows, n_buf = qa.shape[0], wb.shape[1]
    qpos = past + lax.broadcasted_iota(jnp.int32, (rows, 1), 0) % n_new

    def masked(sc, kpos, extra):
        diff = qpos - kpos
        return jnp.where((diff >= 0) & (diff < WINDOW) & (kpos >= 0) & extra, sc, NEG_INF)

    nb = lax.broadcasted_iota(jnp.int32, (1, n_buf), 1)
    nn = lax.broadcasted_iota(jnp.int32, (1, kn.shape[1]), 1)
    sb = masked(_dot(qa, wb[0:KV_WIDTH, :].astype(BF16)), past - n_buf + nb, nb >= 0)
    sn = masked(_dot(qa, kn[0:KV_WIDTH, :].astype(BF16)), past + nn, nn < n_new)
    mx = jnp.maximum(jnp.max(sb, axis=1, keepdims=True), jnp.max(sn, axis=1, keepdims=True))
    pb, pn = jnp.exp(sb - mx), jnp.exp(sn - mx)
    o = (_dot_nt(pb.astype(BF16), wb[KV_WIDTH:, :].astype(BF16))
         + _dot_nt(pn.astype(BF16), kn[KV_WIDTH:, :].astype(BF16)))
    o_ref[0] = o / (jnp.sum(pb, axis=1, keepdims=True) + jnp.sum(pn, axis=1, keepdims=True))


def _attn_window_small(qa, win_t, kv_new_t, *, n_new, past):
    batch, rows, _ = qa.shape
    per_b = lambda b: (b, 0, 0)
    return pl.pallas_call(
        functools.partial(_attn_window_body, n_new=n_new, past=past),
        grid=(batch,),
        in_specs=[pl.BlockSpec((1, rows, LANES), per_b),
                  pl.BlockSpec((1,) + win_t.shape[1:], per_b),
                  pl.BlockSpec((1,) + kv_new_t.shape[1:], per_b)],
        out_specs=pl.BlockSpec((1, rows, LANES), per_b),
        out_shape=jax.ShapeDtypeStruct((batch, rows, LANES), F32),
        compiler_params=_cparams(("arbitrary",)),
        name="attn_win_small",
    )(qa, win_t, kv_new_t)


FFN_TM = 512
FFN_VMEM_LIMIT = 58 * 1024 * 1024
MXU_DEPTH = 256
FFN_CHUNKS = ((0, 6 * MXU_DEPTH), (6 * MXU_DEPTH, D_FF))


def _ffn_body(x_ref, om_ref, oc_ref, os_ref, ow_ref, gt_ref, ge_ref, gn_ref, gf_ref, gl_ref, wc_ref,
              fb_ref, wo_hbm, wu_hbm, wd_hbm, y_ref, fn_ref, xx_ref, wo_ref, wu_ref, wd_ref, sem_ref,
              *, tm, stride, halo):
    s = pl.program_id(1)

    @pl.when((pl.program_id(0) == 0) & (s == 0))
    def _():
        copies = [pltpu.make_async_copy(src, dst, sem_ref.at[n])
                  for n, (src, dst) in enumerate(((wo_hbm, wo_ref), (wu_hbm, wu_ref), (wd_hbm, wd_ref)))]
        for cp in copies:
            cp.start()
        for cp in copies:
            cp.wait()

    sig = _sigmoid(gt_ref[...])
    hi = sig.astype(BF16)
    lo = (sig - hi.astype(F32)).astype(BF16)
    comb = None
    for br, ob_ref in enumerate((oc_ref, os_ref, ow_ref)):
        gate = _dot(hi, ge_ref[br]) + _dot(lo, ge_ref[br])
        term = gate * ob_ref[...]
        comb = term if comb is None else comb + term
    onsa = _rms(comb, gn_ref[...])
    h = (x_ref[...] + _dot(om_ref[...].astype(BF16), wo_ref[0:MLSTM_WIDTH, :])
         + _dot(onsa.astype(BF16), wo_ref[MLSTM_WIDTH:, :]))
    hn = _rms(h, gf_ref[...]).astype(BF16)

    base = halo - (FFN_CONV - 1) * stride

    @pl.when(s == 0)
    def _():
        xx_ref[base:halo, :] = fb_ref[0]

    y_ref[...] = h
    for lo_col, hi_col in FFN_CHUNKS:
        convs = []
        for half in range(2):
            cols = slice(half * D_FF + lo_col, half * D_FF + hi_col)
            xx_ref[halo:halo + tm, cols] = _dot(hn, wu_ref[:, cols])
            conv = xx_ref[base:base + tm, cols] * wc_ref[0:1, cols]
            for j in range(1, FFN_CONV):
                conv = conv + xx_ref[base + j * stride:base + j * stride + tm, cols] * wc_ref[j:j + 1, cols]
            convs.append(conv)
        act = _silu(convs[1]) * convs[0]
        y_ref[...] += _dot(act.astype(BF16), wd_ref[lo_col:hi_col, :])
    fn_ref[0, 0] = xx_ref[tm + base:tm + halo, :]
    xx_ref[0:halo, :] = xx_ref[tm:tm + halo, :]
    y_ref[...] = _rms(y_ref[...], gl_ref[...])


def _gate_expand():
    ge = np.zeros((N_BRANCH, LANES, NSA_WIDTH), np.float32)
    for hd in range(NSA_HEADS):
        for br in range(N_BRANCH):
            ge[br, GATE_COL_NSA + hd * N_BRANCH + br, hd * HEAD_DIM:(hd + 1) * HEAD_DIM] = 1.0
    return jnp.asarray(ge, BF16)


def _ffn(x2d, om, oc, osel, ow, gt, fbuf, w_out, g_nsa, g_ffn, g_final, w_up, w_fconv, w_down,
         *, nb, tm, stride):
    rows = x2d.shape[0]
    ns = rows // (nb * tm)
    halo = -(-(FFN_CONV - 1) * stride // SUBLANES) * SUBLANES
    assert tm >= halo and all((hi - lo) % MXU_DEPTH == 0 for lo, hi in FFN_CHUNKS)
    tok = lambda b, s: (b * ns + s, 0)
    nfb = (FFN_CONV - 1) * stride

    def const(shape):
        return pl.BlockSpec(shape, lambda b, s: (0,) * len(shape))

    hbm = pl.BlockSpec(memory_space=pl.ANY)
    y, fn = pl.pallas_call(
        functools.partial(_ffn_body, tm=tm, stride=stride, halo=halo),
        grid=(nb, ns),
        in_specs=[pl.BlockSpec((tm, D_MODEL), tok)] + [pl.BlockSpec((tm, NSA_WIDTH), tok)] * 4
        + [pl.BlockSpec((tm, LANES), tok),
           const((N_BRANCH, LANES, NSA_WIDTH)), const((1, NSA_WIDTH)), const((1, D_MODEL)),
           const((1, D_MODEL)), const((FFN_CONV, 2 * D_FF)),
           pl.BlockSpec((1, nfb, 2 * D_FF), lambda b, s: (b, 0, 0)), hbm, hbm, hbm],
        out_specs=[pl.BlockSpec((tm, D_MODEL), tok),
                   pl.BlockSpec((1, 1, nfb, 2 * D_FF), lambda b, s: (b, s, 0, 0))],
        out_shape=[jax.ShapeDtypeStruct((rows, D_MODEL), F32),
                   jax.ShapeDtypeStruct((nb, ns, nfb, 2 * D_FF), F32)],
        scratch_shapes=[pltpu.VMEM((halo + tm, 2 * D_FF), F32),
                        pltpu.VMEM((D_MODEL, D_MODEL), BF16), pltpu.VMEM((D_MODEL, 2 * D_FF), BF16),
                        pltpu.VMEM((D_FF, D_MODEL), BF16), pltpu.SemaphoreType.DMA((3,))],
        compiler_params=pltpu.CompilerParams(dimension_semantics=("arbitrary", "arbitrary"),
                                             vmem_limit_bytes=FFN_VMEM_LIMIT),
        name="outproj_ffn",
    )(x2d, om, oc, osel, ow, gt, _gate_expand(), g_nsa.reshape(1, -1), g_ffn.reshape(1, -1),
      g_final.reshape(1, -1), w_fconv, fbuf, w_out.astype(BF16), w_up.astype(BF16),
      w_down.astype(BF16))
    return y, fn[:, ns - 1]


PROMPT_TM = 512
PROMPT_TQ_CMP = 512
PROMPT_TT_TOPK = 1024
PROMPT_TQ_SEL = 512
PROMPT_TK_SEL = 512
PROMPT_TQ_WIN = 256


def _kv_rows(kv_t):
    batch, _, rows = kv_t.shape
    return kv_t.reshape(batch, 2, NSA_KV_HEADS, HEAD_DIM, rows).transpose(0, 4, 1, 2, 3)


def _kv_feature_major(kv5):
    batch, rows = kv5.shape[:2]
    return kv5.transpose(0, 2, 3, 4, 1).reshape(batch, 2 * KV_WIDTH, rows)


def _prompt_layer(x, wts):
    batch, seq, _ = x.shape
    x2d = x.reshape(batch * seq, D_MODEL)
    q, kc_rows, vc_rows, mu, mv, mo, gt, ks_rows, kw_rows, kvc_t, kvs_t, kvw_t, vs_t, vw_t = _in_proj(
        x2d, wts["g_mix"], wts["w_in_packed"], batch=batch, seq=seq, tm=min(PROMPT_TM, seq))
    H, DH, W = MLSTM_HEADS, MLSTM_DH, MLSTM_WIDTH
    o_m, mconv, c_new, n_new, m_new = _mlstm(
        mu, mv, mo, gt, jnp.zeros((batch, MLSTM_CONV - 1, W), F32), jnp.zeros((batch, H, DH, DH), F32),
        jnp.zeros((batch, H, DH), F32), jnp.zeros((batch, H), F32),
        wts["w_mconv"], wts["b_mconv"], wts["w_mq"], wts["w_mk"], wts["b_ig"], wts["b_fg"],
        wts["g_mhead"], wts["m_skip"], batch=batch, seq=seq)
    kce, kco = _compress_prompt(kc_rows, vc_rows, wts["cw"], batch=batch, seq=seq)
    n_sel = -(-seq // SEL_BLOCK)
    assert n_sel <= SEL_BLOCK
    o_cmp, scores_t = _cmp_attn(q, kce, kco, batch=batch, seq=seq, tq=min(PROMPT_TQ_CMP, seq), pos0=0)
    selb = _topk_blocks(scores_t.reshape(batch * NSA_KV_HEADS, -1, seq),
                        jnp.arange(seq, dtype=jnp.int32).reshape(1, seq),
                        n_sel=n_sel, nsw=SEL_BLOCK, tt=min(PROMPT_TT_TOPK, seq))
    selb = selb.reshape(batch, NSA_KV_HEADS, SEL_BLOCK, seq)
    q3d = q.reshape(batch, seq, NSA_WIDTH)
    o_sel = _attn_selected_prompt(q3d, ks_rows, vs_t, selb, tq=min(PROMPT_TQ_SEL, seq),
                                  tk=min(PROMPT_TK_SEL, seq))
    o_win = _attn_window_prompt(q3d, kw_rows, vw_t, tq=PROMPT_TQ_WIN)
    fbuf = jnp.zeros((batch, FFN_CONV - 1, 2 * D_FF), F32)
    y, f_new = _ffn(x2d, o_m, o_cmp, o_sel.reshape(-1, NSA_WIDTH), o_win.reshape(-1, NSA_WIDTH), gt,
                    fbuf, wts["w_out"], wts["g_nsa"], wts["g_ffn"], wts["g_final"], wts["w_up"],
                    wts["w_fconv"], wts["w_down"], nb=batch, tm=min(FFN_TM, seq), stride=1)
    n_win = min(WINDOW, seq)
    return (y.reshape(batch, seq, D_MODEL), _kv_rows(kvc_t), _kv_rows(kvs_t),
            _kv_rows(kvw_t[:, :, seq - n_win:]), mconv, c_new, n_new, m_new.reshape(batch, H), f_new)


def _decode_rows(q2d, batch, seq):
    q5 = (q2d * ATTN_SCALE).reshape(batch, seq, NSA_KV_HEADS, NSA_GROUP, HEAD_DIM).transpose(0, 2, 3, 1, 4)
    eye = jnp.eye(NSA_KV_HEADS, dtype=F32)
    qa = jnp.einsum('bkgtd,kK->bkgtKd', q5, eye)
    return qa.reshape(batch, NSA_KV_HEADS * NSA_GROUP * seq, KV_WIDTH)


def _decode_rows_out(o, batch, seq):
    o6 = o.reshape(batch, NSA_KV_HEADS, NSA_GROUP, seq, NSA_KV_HEADS, HEAD_DIM)
    o5 = jnp.stack([o6[:, kh, :, :, kh, :] for kh in range(NSA_KV_HEADS)], axis=1)
    return o5.transpose(0, 3, 1, 2, 4).reshape(batch * seq, NSA_WIDTH)


def _sample_layer(x, pool_cmp, pool_sel, win_buf, m_conv, m_c, m_n, m_m, f_buf, page_table, wts):
    batch, seq, _ = x.shape
    n_pages = page_table.shape[1]
    past = n_pages * PAGE_SIZE
    assert past % SEL_BLOCK == 0 and seq <= SEL_BLOCK and seq < CMP_BLOCK
    x2d = x.reshape(batch * seq, D_MODEL)
    q, _, _, mu, mv, mo, gt, _, _, kvc_t, kvs_t, kvw_t, _, _ = _in_proj(
        x2d, wts["g_mix"], wts["w_in_packed"], batch=1, seq=batch * seq, tm=batch * seq)
    per_batch = lambda a: a.reshape(2 * KV_WIDTH, batch, seq).transpose(1, 0, 2)
    kvc_t, kvs_t, kvw_t = per_batch(kvc_t), per_batch(kvs_t), per_batch(kvw_t)
    pad_keys = lambda a: jnp.pad(a, ((0, 0), (0, 0), (0, LANES - seq)))
    H = MLSTM_HEADS
    o_m, mconv, c_new, n_new, m_new = _mlstm(
        mu, mv, mo, gt, m_conv, m_c, m_n, m_m,
        wts["w_mconv"], wts["b_mconv"], wts["w_mq"], wts["w_mk"], wts["b_ig"], wts["b_fg"],
        wts["g_mhead"], wts["m_skip"], batch=batch, seq=seq)
    pool_cmp3, pool_sel3 = _kv_feature_major(pool_cmp), _kv_feature_major(pool_sel)
    kce, kco = _compress_paged(pool_cmp3, page_table, wts["cw"], wts["cw_pages"])
    n_past_blk = past // SEL_BLOCK
    n_sel = -(-(past + seq) // SEL_BLOCK)
    o_cmp, scores_t = _cmp_attn(q, kce, kco, batch=batch, seq=seq, tq=seq, pos0=past)
    ns = scores_t.shape[2]
    nsw = ns + LANES
    scores_all = scores_t.transpose(1, 2, 0, 3).reshape(NSA_KV_HEADS, ns, batch * seq)
    pos_all = (past + jnp.arange(batch * seq, dtype=jnp.int32) % seq).reshape(1, batch * seq)
    selb = _topk_blocks(scores_all, pos_all, n_sel=n_sel, nsw=nsw, tt=batch * seq)
    selb = selb.reshape(NSA_KV_HEADS, nsw, batch, seq).transpose(2, 0, 3, 1)
    qa = _decode_rows(q, batch, seq)
    rows = qa.shape[1]
    blk_per_step = ATTN_PAGES_PER_STEP * PAGE_SIZE // SEL_BLOCK
    n_steps = n_pages // ATTN_PAGES_PER_STEP
    sb_rows = jnp.broadcast_to(selb[:, :, None], (batch, NSA_KV_HEADS, NSA_GROUP, seq, selb.shape[-1]))
    sb_rows = sb_rows.reshape(batch, rows, selb.shape[-1])
    bias_q = sb_rows[:, :, :n_past_blk].reshape(batch, rows, n_steps, blk_per_step).transpose(0, 2, 1, 3)
    bias_q = jnp.pad(bias_q, ((0, 0), (0, 0), (0, 0), (0, LANES - blk_per_step)))
    bias_new = jnp.broadcast_to(sb_rows[:, :, n_past_blk:n_past_blk + 1], (batch, rows, LANES))
    o_sel = _attn_paged(qa, bias_q, bias_new, pad_keys(kvs_t), pool_sel3, page_table, n_new=seq)
    n_buf = win_buf.shape[1]
    assert past >= n_buf
    win_t = _kv_feature_major(win_buf)
    o_win = _attn_window_small(qa, win_t, pad_keys(kvw_t), n_new=seq, past=past)
    win_new = jnp.concatenate([win_t, kvw_t], axis=2)[:, :, seq:]
    tmaj = lambda a: a.reshape(batch, seq, -1).transpose(1, 0, 2).reshape(batch * seq, -1)
    fb_t = f_buf.transpose(1, 0, 2).reshape(1, (FFN_CONV - 1) * batch, 2 * D_FF)
    y, f_new = _ffn(tmaj(x2d), tmaj(o_m), tmaj(o_cmp), tmaj(_decode_rows_out(o_sel, batch, seq)),
                    tmaj(_decode_rows_out(o_win, batch, seq)), tmaj(gt), fb_t,
                    wts["w_out"], wts["g_nsa"], wts["g_ffn"], wts["g_final"], wts["w_up"],
                    wts["w_fconv"], wts["w_down"], nb=1, tm=batch * seq, stride=batch)
    y = y.reshape(seq, batch, D_MODEL).transpose(1, 0, 2)
    f_new = f_new.reshape(FFN_CONV - 1, batch, 2 * D_FF).transpose(1, 0, 2)
    return (y, _kv_rows(kvc_t), _kv_rows(kvs_t), _kv_rows(win_new), mconv, c_new, n_new,
            m_new.reshape(batch, H), f_new)


def kernel(x_prompt, x_sample, cache_cmp, cache_sel, state_win, state_mlstm_C, state_mlstm_n,
           state_mlstm_m, state_mlstm_conv, state_ffn_conv, page_table,
           g_mix, w_in, w_out, w_mconv, b_mconv, w_mq, w_mk, b_ig, b_fg, g_mhead, m_skip,
           pe_cmp, w_cmp1, w_cmp2, g_nsa, g_ffn, w_up, w_fconv, w_down, g_final):
    assert w_in.shape[0] == 1, "one layer: the final norm is fused into the layer's FFN kernel"
    l = 0
    wts = dict(g_mix=g_mix[l], w_in_packed=_pack_w_in(w_in[l]), w_out=w_out[l], w_mconv=w_mconv[l],
               b_mconv=b_mconv[l], w_mq=w_mq[l], w_mk=w_mk[l], b_ig=b_ig[l], b_fg=b_fg[l],
               g_mhead=g_mhead[l], m_skip=m_skip[l],
               cw=_pack_compress_weights(pe_cmp[l], w_cmp1[l], w_cmp2[l]),
               cw_pages=_page_pair_constants(pe_cmp[l]),
               g_nsa=g_nsa[l], g_ffn=g_ffn[l], g_final=g_final, w_up=w_up[l], w_fconv=w_fconv[l],
               w_down=w_down[l])
    p = _prompt_layer(x_prompt, wts)
    s = _sample_layer(x_sample, cache_cmp[l], cache_sel[l], state_win[l], state_mlstm_conv[l],
                      state_mlstm_C[l], state_mlstm_n[l], state_mlstm_m[l], state_ffn_conv[l],
                      page_table, wts)
    yp, cmp_p, sel_p, win_p, mconv_p, c_p, n_p, m_p, fconv_p = p
    ys, cmp_s, sel_s, win_s, mconv_s, c_s, n_s, m_s, fconv_s = s
    st = lambda a: a[None]
    return (yp, ys, st(cmp_p), st(cmp_s), st(sel_p), st(sel_s), st(win_p), st(win_s),
            st(c_p), st(c_s), st(n_p), st(n_s), st(m_p), st(m_s), st(mconv_p), st(mconv_s),
            st(fconv_p), st(fconv_s))
```

```python
import functools

import numpy as np
import jax
import jax.numpy as jnp
from jax import lax
from jax.experimental import pallas as pl
from jax.experimental.pallas import tpu as pltpu

F32 = jnp.float32
BF16 = jnp.bfloat16

D_MODEL = 1024
PAGE_SIZE = 128
HEAD_DIM = 64
NSA_HEADS = 8
NSA_KV_HEADS = 2
NSA_GROUP = NSA_HEADS // NSA_KV_HEADS
NSA_WIDTH = NSA_HEADS * HEAD_DIM
KV_WIDTH = NSA_KV_HEADS * HEAD_DIM
CMP_BLOCK = 32
CMP_HIDDEN = 2 * HEAD_DIM
SEL_BLOCK = 64
TOP_N = 16
WINDOW = 512
N_BRANCH = 3
ATTN_SCALE = HEAD_DIM ** -0.5
MLSTM_HEADS = 4
MLSTM_WIDTH = D_MODEL - NSA_WIDTH
MLSTM_DH = MLSTM_WIDTH // MLSTM_HEADS
MLSTM_CONV = 4
D_FF = ((8 * D_MODEL // 3 + 127) // 128) * 128
FFN_CONV = 3
EPS = 1e-6
NEG_INF = -1e30
SEL_PRIORITY = 1e4
LOG2_E = 1.4426950408889634

LANES = 128
SUBLANES = 8
VMEM_LIMIT = 48 * 1024 * 1024

GATE_COL_NSA = 0
GATE_COL_I = NSA_HEADS * N_BRANCH
GATE_COL_F = GATE_COL_I + MLSTM_HEADS

MLSTM_CHUNK = 128
MLSTM_SEQS_PER_STEP = 4


def _cparams(sem):
    return pltpu.CompilerParams(dimension_semantics=sem, vmem_limit_bytes=VMEM_LIMIT)


def _dot(a, b):
    return jnp.dot(a, b, preferred_element_type=F32)


def _dot_nt(a, b):
    return lax.dot_general(a, b, (((1,), (1,)), ((), ())), preferred_element_type=F32)


def _sigmoid(x):
    return 1.0 / (1.0 + jnp.exp(-x))


def _silu(x):
    return x * _sigmoid(x)


def _rms(x, g):
    return x * lax.rsqrt(jnp.mean(x * x, axis=-1, keepdims=True) + EPS) * g


IN_ROW_WIDTHS = (NSA_WIDTH, KV_WIDTH, KV_WIDTH, MLSTM_WIDTH, MLSTM_WIDTH, MLSTM_WIDTH, LANES,
                 KV_WIDTH, KV_WIDTH)
IN_ROW_DTYPES = (F32,) * 7 + (BF16,) * 2
N_KV_BRANCH = 3


def _inproj_body(x_ref, g_ref, w_ref, wt_ref, *out_refs):
    xb = _rms(x_ref[...], g_ref[...]).astype(BF16)
    off = 0
    n_rows = len(IN_ROW_WIDTHS)
    for ref in out_refs[:n_rows]:
        n = ref.shape[-1]
        ref[...] = _dot(xb, w_ref[:, off:off + n]).astype(ref.dtype)
        off += n
    kv_refs = out_refs[n_rows:n_rows + N_KV_BRANCH]
    vt_refs = out_refs[n_rows + N_KV_BRANCH:]
    for n, ref in enumerate(kv_refs):
        kv_t = _dot_nt(wt_ref[n * 2 * KV_WIDTH:(n + 1) * 2 * KV_WIDTH, :], xb)
        ref[0] = kv_t
        if n > 0:
            vt_refs[n - 1][0] = kv_t[KV_WIDTH:, :].astype(BF16)


def _pack_w_in(w_in):
    splits = np.cumsum([NSA_WIDTH, 2 * KV_WIDTH, 2 * KV_WIDTH, 2 * KV_WIDTH, NSA_HEADS * N_BRANCH,
                        MLSTM_WIDTH, MLSTM_WIDTH, MLSTM_WIDTH, MLSTM_HEADS]).tolist()
    q, kvc, kvs, kvw, gt, mu, mv, mo, mi, mf = jnp.split(w_in, splits, axis=1)
    gates = jnp.concatenate([gt, mi, mf], axis=1)
    gates = jnp.pad(gates, ((0, 0), (0, LANES - gates.shape[1])))
    w_rows = jnp.concatenate([q, kvc, mu, mv, mo, gates, kvs[:, :KV_WIDTH], kvw[:, :KV_WIDTH]],
                             axis=1).astype(BF16)
    w_kv_t = jnp.concatenate([kvc, kvs, kvw], axis=1).T.astype(BF16)
    return w_rows, w_kv_t


def _in_proj(x2d, g_mix, w_packed, *, batch, seq, tm):
    w_rows, w_kv_t = w_packed
    t = x2d.shape[0]
    ns = seq // tm
    kv_sd = jax.ShapeDtypeStruct((batch, 2 * KV_WIDTH, seq), F32)
    vt_sd = jax.ShapeDtypeStruct((batch, KV_WIDTH, seq), BF16)
    feat_major = lambda rows: pl.BlockSpec((1, rows, tm), lambda i: (i // ns, 0, i % ns))
    return pl.pallas_call(
        _inproj_body,
        grid=(t // tm,),
        in_specs=[pl.BlockSpec((tm, D_MODEL), lambda i: (i, 0)),
                  pl.BlockSpec((1, D_MODEL), lambda i: (0, 0)),
                  pl.BlockSpec(w_rows.shape, lambda i: (0, 0)),
                  pl.BlockSpec(w_kv_t.shape, lambda i: (0, 0))],
        out_specs=[pl.BlockSpec((tm, n), lambda i: (i, 0)) for n in IN_ROW_WIDTHS]
        + [feat_major(2 * KV_WIDTH)] * N_KV_BRANCH + [feat_major(KV_WIDTH)] * (N_KV_BRANCH - 1),
        out_shape=[jax.ShapeDtypeStruct((t, n), dt) for n, dt in zip(IN_ROW_WIDTHS, IN_ROW_DTYPES)]
        + [kv_sd] * N_KV_BRANCH + [vt_sd] * (N_KV_BRANCH - 1),
        compiler_params=_cparams(("arbitrary",)),
        name="in_proj",
    )(x2d, g_mix.reshape(1, D_MODEL), w_rows, w_kv_t)


def _mlstm_body(*refs, valid, bb):
    cb_ref, c0_ref, n0_ref, m0_ref = refs[4:8]
    cn_ref, c_ref, n_ref, m_ref, xx_ref = refs[16:21]
    halo = SUBLANES

    @pl.when(pl.program_id(1) == 0)
    def _():
        xx_ref[:, 0:halo, :] = jnp.zeros((bb, halo, MLSTM_WIDTH), F32)
        xx_ref[:, halo - (MLSTM_CONV - 1):halo, :] = cb_ref[...]
        c_ref[...] = c0_ref[...]
        n_ref[...] = n0_ref[...]
        m_ref[...] = m0_ref[...]

    _mlstm_chunk(*refs, valid=valid, bb=bb)


def _mlstm_chunk(mu_ref, mv_ref, mo_ref, g_ref, cb_ref, c0_ref, n0_ref, m0_ref,
                 wc_ref, bc_ref, wq_ref, wk_ref, gb_ref, gh_ref, sk_ref,
                 o_ref, cn_ref, c_ref, n_ref, m_ref,
                 xx_ref, vpad_ref, gpad_ref, *, valid, bb):
    L = MLSTM_CHUNK
    DH = MLSTM_DH
    halo = SUBLANES
    units = [(bi, h) for bi in range(bb) for h in range(MLSTM_HEADS)]
    head_lanes = lambda h: slice(h * DH, (h + 1) * DH)
    row = lax.broadcasted_iota(jnp.int32, (L, L), 0)
    col = lax.broadcasted_iota(jnp.int32, (L, L), 1)
    tril = row >= col
    triu = row <= col
    tok_col = lax.broadcasted_iota(jnp.int32, (L, 1), 0)
    tok_row = lax.broadcasted_iota(jnp.int32, (1, L), 1)

    def log_sigmoid(x):
        return jnp.minimum(x, 0.0) - jnp.log(1.0 + jnp.exp(-jnp.abs(x)))

    uc, gb, gbt = {}, {}, {}
    for bi in range(bb):
        if valid < L:
            xx_ref[bi, halo:, :] = jnp.zeros((L, MLSTM_WIDTH), F32)
            vpad_ref[bi] = jnp.zeros((L, MLSTM_WIDTH), F32)
            gpad_ref[bi] = jnp.zeros((L, LANES), F32)
        xx_ref[bi, halo:halo + valid, :] = mu_ref[bi]
        vpad_ref[bi, 0:valid, :] = mv_ref[bi]
        gpad_ref[bi, 0:valid, :] = g_ref[bi]
        conv = xx_ref[bi, halo - 3:halo - 3 + L, :] * wc_ref[0:1, :]
        for j in range(1, MLSTM_CONV):
            conv = conv + xx_ref[bi, halo - 3 + j:halo - 3 + j + L, :] * wc_ref[j:j + 1, :]
        uc[bi] = _silu(conv + bc_ref[...])
        tail = xx_ref[bi, valid + halo - 3:valid + halo, :]
        xx_ref[bi, halo - 3:halo, :] = tail
        cn_ref[bi] = tail
        gb[bi] = gpad_ref[bi] + gb_ref[...]
        gbt[bi] = gb[bi].T

    q, k, qb, kb = {}, {}, {}, {}
    for u in units:
        bi, h = u
        ub = uc[bi][:, head_lanes(h)].astype(BF16)
        q[u] = _dot(ub, wq_ref[h])
        k[u] = _dot(ub, wk_ref[h]) * (DH ** -0.5)
        qb[u], kb[u] = q[u].astype(BF16), k[u].astype(BF16)

    ic_col, ic_row, cum_col, cum_row = {}, {}, {}, {}
    for u in units:
        bi, h = u
        ic_c = gb[bi][:, GATE_COL_I + h:GATE_COL_I + h + 1]
        ic_r = gbt[bi][GATE_COL_I + h:GATE_COL_I + h + 1, :]
        lf_c = log_sigmoid(gb[bi][:, GATE_COL_F + h:GATE_COL_F + h + 1])
        lf_r = log_sigmoid(gbt[bi][GATE_COL_F + h:GATE_COL_F + h + 1, :])
        if valid < L:
            ic_c = jnp.where(tok_col < valid, ic_c, NEG_INF)
            ic_r = jnp.where(tok_row < valid, ic_r, NEG_INF)
            lf_c = jnp.where(tok_col < valid, lf_c, 0.0)
            lf_r = jnp.where(tok_row < valid, lf_r, 0.0)
        ic_col[u], ic_row[u] = ic_c, ic_r
        cum_col[u] = jnp.sum(jnp.where(tril, lf_r, 0.0), axis=1, keepdims=True)
        cum_row[u] = jnp.sum(jnp.where(triu, lf_c, 0.0), axis=0, keepdims=True)

    m_t, w, sc = {}, {}, {}
    for u in units:
        bi, h = u
        m0 = m_ref[bi, 0:1, h:h + 1]
        dmat = jnp.where(tril, cum_col[u] - cum_row[u] + ic_row[u], NEG_INF)
        inter = cum_col[u] + m0
        m_t[u] = jnp.maximum(inter, jnp.max(dmat, axis=1, keepdims=True))
        w[u] = jnp.exp(dmat - m_t[u])
        sc[u] = jnp.exp(inter - m_t[u])

    hc = {}
    for u in units:
        bi, h = u
        s = _dot_nt(qb[u], kb[u]) * w[u]
        v = vpad_ref[bi, :, head_lanes(h)]
        c_old = c_ref[bi, h]
        n_old = n_ref[bi, h:h + 1, :]
        num = _dot(s.astype(BF16), v.astype(BF16)) + sc[u] * _dot_nt(qb[u], c_old.astype(BF16))
        den = (jnp.sum(s, axis=1, keepdims=True)
               + sc[u] * jnp.sum(q[u] * n_old, axis=1, keepdims=True))
        hc[u] = num / jnp.maximum(jnp.abs(den), jnp.exp(-m_t[u]))

    for u in units:
        bi, h = u
        m0 = m_ref[bi, 0:1, h:h + 1]
        m_new = m_t[u][L - 1:L, :]
        cum_last = cum_col[u][L - 1:L, :]
        wl = jnp.exp(cum_last - cum_col[u] + ic_col[u] - m_new)
        sl = jnp.exp(cum_last + m0 - m_new)
        v = vpad_ref[bi, :, head_lanes(h)]
        vw_t = (v * wl).T.astype(BF16)
        c_ref[bi, h] = sl * c_ref[bi, h] + _dot(vw_t, kb[u])
        n_ref[bi, h:h + 1, :] = sl * n_ref[bi, h:h + 1, :] + jnp.sum(wl * k[u], axis=0, keepdims=True)
        m_ref[bi, 0:1, h:h + 1] = m_new

    for u in units:
        bi, h = u
        hn = _rms(hc[u], gh_ref[:, head_lanes(h)])
        u_h = uc[bi][:, head_lanes(h)]
        out = ((hn[0:valid, :] + sk_ref[:, head_lanes(h)] * u_h[0:valid, :])
               * _sigmoid(mo_ref[bi, :, head_lanes(h)]))
        o_ref[bi, :, head_lanes(h)] = out


def _mlstm(mu, mv, mo, gates, conv_buf, c0, n0, m0, w_mconv, b_mconv, w_mq, w_mk, b_ig, b_fg,
           g_mhead, m_skip, *, batch, seq):
    L = MLSTM_CHUNK
    valid = min(seq, L)
    assert seq % valid == 0 and (valid == L or seq == valid)
    nc = seq // valid
    gate_bias = jnp.zeros((1, LANES), F32)
    gate_bias = gate_bias.at[0, GATE_COL_I:GATE_COL_I + MLSTM_HEADS].set(b_ig)
    gate_bias = gate_bias.at[0, GATE_COL_F:GATE_COL_F + MLSTM_HEADS].set(b_fg)
    bb = MLSTM_SEQS_PER_STEP
    assert batch % bb == 0
    tok = lambda b, c: (b, c, 0)
    const2 = lambda b, c: (0, 0)
    const3 = lambda b, c: (0, 0, 0)
    per_b3 = lambda b, c: (b, 0, 0)
    per_b4 = lambda b, c: (b, 0, 0, 0)
    H, DH, W = MLSTM_HEADS, MLSTM_DH, MLSTM_WIDTH
    rows3 = lambda a: a.reshape(batch, seq, a.shape[-1])
    o_m, conv_new, c_new, n_new, m_new = pl.pallas_call(
        functools.partial(_mlstm_body, valid=valid, bb=bb),
        grid=(batch // bb, nc),
        in_specs=[pl.BlockSpec((bb, valid, W), tok), pl.BlockSpec((bb, valid, W), tok),
                  pl.BlockSpec((bb, valid, W), tok), pl.BlockSpec((bb, valid, LANES), tok),
                  pl.BlockSpec((bb, MLSTM_CONV - 1, W), per_b3),
                  pl.BlockSpec((bb, H, DH, DH), per_b4),
                  pl.BlockSpec((bb, H, DH), per_b3),
                  pl.BlockSpec((bb, 1, H), per_b3),
                  pl.BlockSpec((MLSTM_CONV, W), const2), pl.BlockSpec((1, W), const2),
                  pl.BlockSpec((H, DH, DH), const3), pl.BlockSpec((H, DH, DH), const3),
                  pl.BlockSpec((1, LANES), const2), pl.BlockSpec((1, W), const2),
                  pl.BlockSpec((1, W), const2)],
        out_specs=[pl.BlockSpec((bb, valid, W), tok),
                   pl.BlockSpec((bb, MLSTM_CONV - 1, W), per_b3),
                   pl.BlockSpec((bb, H, DH, DH), per_b4),
                   pl.BlockSpec((bb, H, DH), per_b3),
                   pl.BlockSpec((bb, 1, H), per_b3)],
        out_shape=[jax.ShapeDtypeStruct((batch, seq, W), F32),
                   jax.ShapeDtypeStruct((batch, MLSTM_CONV - 1, W), F32),
                   jax.ShapeDtypeStruct((batch, H, DH, DH), F32),
                   jax.ShapeDtypeStruct((batch, H, DH), F32),
                   jax.ShapeDtypeStruct((batch, 1, H), F32)],
        scratch_shapes=[pltpu.VMEM((bb, SUBLANES + L, W), F32), pltpu.VMEM((bb, L, W), F32),
                        pltpu.VMEM((bb, L, LANES), F32)],
        compiler_params=_cparams(("arbitrary", "arbitrary")),
        name="mlstm",
    )(rows3(mu), rows3(mv), rows3(mo), rows3(gates), conv_buf, c0, n0, m0.reshape(batch, 1, H),
      w_mconv, b_mconv.reshape(1, W), w_mq.astype(BF16), w_mk.astype(BF16), gate_bias,
      g_mhead.reshape(1, W), m_skip.reshape(1, W))
    return o_m.reshape(batch * seq, W), conv_new, c_new, n_new, m_new


def _compress_rows(xk_ref, xv_ref, pe_ref, w1_ref, w2_ref, n_pairs):
    pair_rows = 2 * CMP_BLOCK
    outs = []
    for kv, x_ref in enumerate((xk_ref, xv_ref)):
        acc = jnp.zeros((2 * n_pairs, NSA_KV_HEADS * CMP_HIDDEN), F32)
        for r in range(CMP_BLOCK):
            ev = x_ref[pl.ds(r, n_pairs, stride=pair_rows), :]
            od = x_ref[pl.ds(CMP_BLOCK + r, n_pairs, stride=pair_rows), :]
            xr = jnp.concatenate([ev, od], axis=0) + pe_ref[kv, r:r + 1, :]
            acc = acc + _dot(xr.astype(BF16), w1_ref[kv, r])
        outs.append(_dot(_silu(acc).astype(BF16), w2_ref[kv]))
    return jnp.concatenate(outs, axis=1)


def _compress_body(xk_ref, xv_ref, pe_ref, w1_ref, w2_ref, oe_ref, oo_ref, *, n_pairs):
    out = _compress_rows(xk_ref, xv_ref, pe_ref, w1_ref, w2_ref, n_pairs)
    oe_ref[0] = out[0:n_pairs, :]
    oo_ref[0] = out[n_pairs:, :]


BLOCKS_PER_PAGE = PAGE_SIZE // CMP_BLOCK


def _gather_pages(pt_ref, pool_hbm, pages_ref, sem_ref, n_pages):
    g = pl.program_id(0) * pl.num_programs(1) + pl.program_id(1)
    n_total = pl.num_programs(0) * pl.num_programs(1)

    def copies(step, slot):
        return [pltpu.make_async_copy(pool_hbm.at[pt_ref[step * n_pages + j]], pages_ref.at[slot, j],
                                      sem_ref.at[slot]) for j in range(n_pages)]

    @pl.when(g == 0)
    def _():
        for cp in copies(0, 0):
            cp.start()

    @pl.when(g + 1 < n_total)
    def _():
        for cp in copies(g + 1, (g + 1) % 2):
            cp.start()

    slot = g % 2
    for cp in copies(g, slot):
        cp.wait()
    return slot


def _compress_paged_body(pt_ref, pool_hbm, pet_ref, perm_ref, w1_ref, w2_ref, oe_ref, oo_ref,
                         buf_ref, os_ref, pages_ref, sem_ref, *, n_pages):
    slot = _gather_pages(pt_ref, pool_hbm, pages_ref, sem_ref, n_pages)
    grp = 2 * BLOCKS_PER_PAGE
    for jp in range(n_pages // 2):
        xt = jnp.concatenate([pages_ref[slot, 2 * jp], pages_ref[slot, 2 * jp + 1]], axis=1)
        xb = (xt + pet_ref[...]).astype(BF16)
        xp = _dot_nt(perm_ref[...], xb)
        for r in range(CMP_BLOCK):
            for kv in range(2):
                lane0 = (2 * kv + r % 2) * KV_WIDTH
                buf_ref[r // 2, grp * jp:grp * (jp + 1), lane0:lane0 + KV_WIDTH] = (
                    xp[grp * r:grp * (r + 1), kv * KV_WIDTH:(kv + 1) * KV_WIDTH])
    for kv in range(2):
        lanes = slice(2 * kv * KV_WIDTH, 2 * (kv + 1) * KV_WIDTH)
        acc = _dot(buf_ref[0, :, lanes].astype(BF16), w1_ref[kv, 0])
        for r2 in range(1, CMP_BLOCK // 2):
            acc = acc + _dot(buf_ref[r2, :, lanes].astype(BF16), w1_ref[kv, r2])
        os_ref[kv] = _dot(_silu(acc).astype(BF16), w2_ref[kv])
    half = os_ref.shape[1] // 2
    for parity, ref in enumerate((oe_ref, oo_ref)):
        ref[0] = jnp.concatenate([os_ref[kv, pl.ds(parity, half, stride=2), :] for kv in range(2)],
                                 axis=1)


def _page_pair_constants(pe):
    pe_t = jnp.broadcast_to(pe.transpose(0, 2, 1)[:, None, :, None, :],
                            (2, NSA_KV_HEADS, HEAD_DIM, 2 * BLOCKS_PER_PAGE, CMP_BLOCK))
    pe_t = pe_t.reshape(2 * KV_WIDTH, 2 * PAGE_SIZE)
    grp = 2 * BLOCKS_PER_PAGE
    perm = np.zeros((2 * PAGE_SIZE, 2 * PAGE_SIZE), np.float32)
    for r in range(CMP_BLOCK):
        for b in range(grp):
            perm[r * grp + b, b * CMP_BLOCK + r] = 1.0
    return pe_t, jnp.asarray(perm, BF16)


def _pack_compress_weights(pe, w1, w2):
    eye_h = jnp.eye(NSA_KV_HEADS, dtype=F32)
    pe_r = jnp.broadcast_to(pe[:, :, None, :], (2, CMP_BLOCK, NSA_KV_HEADS, HEAD_DIM))
    pe_r = pe_r.reshape(2, CMP_BLOCK, KV_WIDTH)
    w1r = w1.reshape(2, CMP_BLOCK, HEAD_DIM, CMP_HIDDEN)
    w1_big = jnp.einsum('krdc,hH->krhdHc', w1r, eye_h)
    w1_big = w1_big.reshape(2, CMP_BLOCK, KV_WIDTH, NSA_KV_HEADS * CMP_HIDDEN).astype(BF16)
    w2_big = jnp.einsum('kcd,hH->khcHd', w2, eye_h)
    w2_big = w2_big.reshape(2, NSA_KV_HEADS * CMP_HIDDEN, KV_WIDTH).astype(BF16)
    return pe_r, w1_big, w2_big


def _compress_prompt(k_rows, v_rows, cw, *, batch, seq):
    pe_r, w1_big, w2_big = cw
    n_pairs = seq // (2 * CMP_BLOCK)
    const3 = lambda b: (0, 0, 0)
    out_sd = jax.ShapeDtypeStruct((batch, n_pairs, 2 * KV_WIDTH), F32)
    return pl.pallas_call(
        functools.partial(_compress_body, n_pairs=n_pairs),
        grid=(batch,),
        in_specs=[pl.BlockSpec((seq, KV_WIDTH), lambda b: (b, 0)),
                  pl.BlockSpec((seq, KV_WIDTH), lambda b: (b, 0)),
                  pl.BlockSpec(pe_r.shape, const3),
                  pl.BlockSpec(w1_big.shape, lambda b: (0, 0, 0, 0)),
                  pl.BlockSpec(w2_big.shape, const3)],
        out_specs=[pl.BlockSpec((1, n_pairs, 2 * KV_WIDTH), lambda b: (b, 0, 0))] * 2,
        out_shape=[out_sd, out_sd],
        compiler_params=_cparams(("arbitrary",)),
        name="compress_prompt",
    )(k_rows, v_rows, pe_r, w1_big, w2_big)


COMPRESS_PAGES_PER_STEP = 64


def _compress_paged(pool, page_table, cw, cw_pages):
    _, w1_big, w2_big = cw
    w1_big = w1_big.reshape(2, CMP_BLOCK // 2, 2 * KV_WIDTH, NSA_KV_HEADS * CMP_HIDDEN)
    pe_t, perm = cw_pages
    batch, n_pages = page_table.shape
    pps = COMPRESS_PAGES_PER_STEP
    assert n_pages % pps == 0 and pps % 2 == 0
    n_steps = n_pages // pps
    n_blk = pps * BLOCKS_PER_PAGE
    const3 = lambda b, c, pt: (0, 0, 0)
    return pl.pallas_call(
        functools.partial(_compress_paged_body, n_pages=pps),
        grid_spec=pltpu.PrefetchScalarGridSpec(
            num_scalar_prefetch=1,
            grid=(batch, n_steps),
            in_specs=[pl.BlockSpec(memory_space=pl.ANY),
                      pl.BlockSpec(pe_t.shape, lambda b, c, pt: (0, 0)),
                      pl.BlockSpec(perm.shape, lambda b, c, pt: (0, 0)),
                      pl.BlockSpec(w1_big.shape, lambda b, c, pt: (0, 0, 0, 0)),
                      pl.BlockSpec(w2_big.shape, const3)],
            out_specs=[pl.BlockSpec((1, n_blk // 2, 2 * KV_WIDTH), lambda b, c, pt: (b, c, 0))] * 2,
            scratch_shapes=[pltpu.VMEM((CMP_BLOCK // 2, n_blk, 4 * KV_WIDTH), F32),
                            pltpu.VMEM((2, n_blk, KV_WIDTH), F32),
                            pltpu.VMEM((2, pps, 2 * KV_WIDTH, PAGE_SIZE), F32),
                            pltpu.SemaphoreType.DMA((2,))]),
        out_shape=[jax.ShapeDtypeStruct((batch, n_steps * n_blk // 2, 2 * KV_WIDTH), F32)] * 2,
        compiler_params=_cparams(("arbitrary", "arbitrary")),
        name="compress_paged",
    )(page_table.reshape(-1), pool, pe_t, perm, w1_big, w2_big)


def _cmp_attn_body(q_ref, ke_ref, ko_ref, o_ref, st_ref, *, tq, pos0):
    ns = ke_ref.shape[1]
    i = pl.program_id(1)
    rows = NSA_GROUP * tq
    tok0 = pos0 + i * tq
    pos_r = tok0 + lax.broadcasted_iota(jnp.int32, (1, rows), 1) % tq
    pair_c = lax.broadcasted_iota(jnp.int32, (ns, 1), 0)
    end_e = (2 * pair_c + 1) * CMP_BLOCK - 1
    end_o = (2 * pair_c + 2) * CMP_BLOCK - 1
    any_r = (CMP_BLOCK - 1 <= pos_r).astype(F32)
    contract_blocks = (((0,), (0,)), ((), ()))
    q = q_ref[...] * ATTN_SCALE
    for kh in range(NSA_KV_HEADS):
        qs = jnp.concatenate([q[:, (kh * NSA_GROUP + g) * HEAD_DIM:(kh * NSA_GROUP + g + 1) * HEAD_DIM]
                              for g in range(NSA_GROUP)], axis=0).astype(BF16)
        ks, vs = slice(kh * HEAD_DIM, (kh + 1) * HEAD_DIM), slice(KV_WIDTH + kh * HEAD_DIM,
                                                                   KV_WIDTH + (kh + 1) * HEAD_DIM)
        ke, ko = ke_ref[0, :, ks].astype(BF16), ko_ref[0, :, ks].astype(BF16)
        te = jnp.where(end_e <= pos_r, _dot_nt(ke, qs), NEG_INF)
        to = jnp.where(end_o <= pos_r, _dot_nt(ko, qs), NEG_INF)
        mt = jnp.maximum(jnp.max(te, axis=0, keepdims=True), jnp.max(to, axis=0, keepdims=True))
        pte, pto = jnp.exp(te - mt), jnp.exp(to - mt)
        invt = any_r / (jnp.sum(pte, axis=0, keepdims=True) + jnp.sum(pto, axis=0, keepdims=True))
        pte, pto = pte * invt, pto * invt
        oh = (lax.dot_general(pte.astype(BF16), ke_ref[0, :, vs].astype(BF16), contract_blocks,
                              preferred_element_type=F32)
              + lax.dot_general(pto.astype(BF16), ko_ref[0, :, vs].astype(BF16), contract_blocks,
                                preferred_element_type=F32))
        for g in range(NSA_GROUP):
            hd = kh * NSA_GROUP + g
            o_ref[:, hd * HEAD_DIM:(hd + 1) * HEAD_DIM] = oh[g * tq:(g + 1) * tq, :]
        ps = pte + pto
        score = ps[:, 0:tq]
        for g in range(1, NSA_GROUP):
            score = score + ps[:, g * tq:(g + 1) * tq]
        st_ref[0, kh] = score


def _cmp_attn(q2d, kce, kco, *, batch, seq, tq, pos0):
    ns = kce.shape[1]
    nq = seq // tq
    return pl.pallas_call(
        functools.partial(_cmp_attn_body, tq=tq, pos0=pos0),
        grid=(batch, nq),
        in_specs=[pl.BlockSpec((tq, NSA_WIDTH), lambda b, i: (b * nq + i, 0)),
                  pl.BlockSpec((1, ns, 2 * KV_WIDTH), lambda b, i: (b, 0, 0)),
                  pl.BlockSpec((1, ns, 2 * KV_WIDTH), lambda b, i: (b, 0, 0))],
        out_specs=[pl.BlockSpec((tq, NSA_WIDTH), lambda b, i: (b * nq + i, 0)),
                   pl.BlockSpec((1, NSA_KV_HEADS, ns, tq), lambda b, i: (b, 0, 0, i))],
        out_shape=[jax.ShapeDtypeStruct((batch * seq, NSA_WIDTH), F32),
                   jax.ShapeDtypeStruct((batch, NSA_KV_HEADS, ns, seq), F32)],
        compiler_params=_cparams(("arbitrary", "arbitrary")),
        name="cmp_attn",
    )(q2d, kce, kco)


def _topk_body(pos_ref, st_ref, b_ref, *, n_sel):
    score = st_ref[0]
    ns, tt = score.shape
    nsw = b_ref.shape[1]
    if nsw > ns:
        score = jnp.concatenate([score, jnp.zeros((nsw - ns, tt), F32)], axis=0)
    blk = lax.broadcasted_iota(jnp.int32, (nsw, 1), 0)
    blk_f = blk.astype(F32)
    cur = pos_ref[...] // SEL_BLOCK
    forced = (blk == 0) | (blk == cur) | (blk == cur - 1)
    pri = jnp.where(blk <= cur, jnp.where(forced, SEL_PRIORITY, score), -SEL_PRIORITY)
    pri = jnp.where(blk < n_sel, pri, -jnp.inf)
    bias = jnp.full((nsw, tt), NEG_INF, F32)
    for _ in range(min(TOP_N, n_sel)):
        top = jnp.max(pri, axis=0, keepdims=True)
        first = jnp.min(jnp.where(pri == top, blk_f, float(nsw)), axis=0, keepdims=True)
        hit = blk_f == first
        bias = jnp.where(hit, 0.0, bias)
        pri = jnp.where(hit, -jnp.inf, pri)
    b_ref[0] = bias


def _topk_blocks(scores_t, pos, *, n_sel, nsw, tt):
    groups, ns, tokens = scores_t.shape
    assert nsw >= max(ns, n_sel) and tokens % tt == 0
    return pl.pallas_call(
        functools.partial(_topk_body, n_sel=n_sel),
        grid=(groups, tokens // tt),
        in_specs=[pl.BlockSpec((1, tt), lambda g, i: (0, i)),
                  pl.BlockSpec((1, ns, tt), lambda g, i: (g, 0, i))],
        out_specs=pl.BlockSpec((1, nsw, tt), lambda g, i: (g, 0, i)),
        out_shape=jax.ShapeDtypeStruct((groups, nsw, tokens), F32),
        compiler_params=_cparams(("arbitrary", "arbitrary")),
        name="topk_blocks",
    )(pos, scores_t)


def _softmax_update(sc, vt_bf16, m_ref, l_ref, acc_ref):
    m_old = m_ref[...]
    m_new = jnp.maximum(m_old, jnp.max(sc, axis=1, keepdims=True))
    alpha = jnp.exp(m_old - m_new)
    pr = jnp.exp(sc - jnp.concatenate([m_new] * (sc.shape[1] // LANES), axis=1))
    l_ref[...] = alpha * l_ref[...] + jnp.sum(pr, axis=1, keepdims=True)
    acc_ref[...] = alpha * acc_ref[...] + _dot_nt(pr.astype(BF16), vt_bf16)
    m_ref[...] = m_new


def _softmax_init(m_ref, l_ref, acc_ref):
    m_ref[...] = jnp.full(m_ref.shape, NEG_INF, F32)
    l_ref[...] = jnp.zeros(l_ref.shape, F32)
    acc_ref[...] = jnp.zeros(acc_ref.shape, F32)


ATTN_TAB_COLS = 5


def _attn_pairs(seq, tq, tk):
    rows = []
    for i in range(seq // tq):
        t_lo, t_hi = i * tq, i * tq + tq - 1
        js = list(range(0, t_hi // tk + 1))
        for n, j in enumerate(js):
            rows.append((i, j, int(n == 0), int(n == len(js) - 1), int(j * tk + tk - 1 > t_lo)))
    return np.asarray(rows, np.int32)


def _attn_body(tab_ref, q_ref, k_ref, vt_ref, oh_ref, sb_ref, o_ref, qa_ref, m_ref, l_ref, acc_ref,
               *, tq, tk):
    p = pl.program_id(1)
    i, j, first, last, partial_tile = [tab_ref[ATTN_TAB_COLS * p + n] for n in range(ATTN_TAB_COLS)]
    cols = NSA_HEADS * tq

    @pl.when(first == 1)
    def _():
        for hd, piece in enumerate(_query_columns(q_ref[0], sb_ref[0], tq)):
            qa_ref[:, hd * tq:(hd + 1) * tq] = piece
        m_ref[...] = jnp.full(m_ref.shape, NEG_INF, F32)
        l_ref[...] = jnp.zeros(l_ref.shape, F32)
        acc_ref[...] = jnp.zeros(acc_ref.shape, F32)

    k_aug = jnp.concatenate([k_ref[...], oh_ref[...]], axis=1)
    sc = _dot(k_aug, qa_ref[...])
    vt = vt_ref[0]

    def update(sc):
        m_old = m_ref[...]
        m_new = jnp.maximum(m_old, jnp.max(sc, axis=0, keepdims=True))
        alpha = jnp.exp2(m_old - m_new)
        pr = jnp.exp2(sc - m_new)
        l_ref[...] = alpha * l_ref[...] + jnp.sum(pr, axis=0, keepdims=True)
        acc_ref[...] = alpha * acc_ref[...] + _dot(vt, pr.astype(BF16))
        m_ref[...] = m_new

    @pl.when(partial_tile == 1)
    def _():
        qpos = i * tq + (lax.broadcasted_iota(jnp.int32, (1, cols), 1) & (tq - 1))
        kpos = j * tk + lax.broadcasted_iota(jnp.int32, (tk, 1), 0)
        update(jnp.where(kpos <= qpos, sc, NEG_INF))

    @pl.when(partial_tile == 0)
    def _():
        update(sc)

    @pl.when(last == 1)
    def _():
        _store_heads(acc_ref[...] / l_ref[...], o_ref, tq)


def _query_columns(q, sel_bias, tq):
    q = q * (ATTN_SCALE * LOG2_E)
    zeros64 = jnp.zeros((HEAD_DIM, tq), F32)
    kv_head_rows = lambda x, kh: jnp.concatenate([x, zeros64] if kh == 0 else [zeros64, x], axis=0)
    pieces = []
    for m in range(NSA_HEADS // 2):
        q_t = q[:, m * LANES:(m + 1) * LANES].T
        for hd in (2 * m, 2 * m + 1):
            kh = hd // NSA_GROUP
            piece = kv_head_rows(q_t[(hd % 2) * HEAD_DIM:(hd % 2 + 1) * HEAD_DIM, :], kh)
            if sel_bias is not None:
                piece = jnp.concatenate([piece, kv_head_rows(sel_bias[kh], kh)], axis=0)
            pieces.append(piece.astype(BF16))
    return pieces


def _store_heads(o_t, o_ref, tq):
    for m in range(NSA_HEADS // 2):
        pair = jnp.concatenate(
            [o_t[(hd // NSA_GROUP) * HEAD_DIM:(hd // NSA_GROUP + 1) * HEAD_DIM, hd * tq:(hd + 1) * tq]
             for hd in (2 * m, 2 * m + 1)], axis=0)
        o_ref[0, :, m * LANES:(m + 1) * LANES] = pair.T


def _window_body(q_ref, *refs, tq, n_tiles):
    k_refs, v_refs, o_ref = refs[:n_tiles], refs[n_tiles:2 * n_tiles], refs[2 * n_tiles]
    i = pl.program_id(1)
    cols = NSA_HEADS * tq
    qa = jnp.concatenate(_query_columns(q_ref[0], None, tq), axis=1)
    qpos = i * tq + (lax.broadcasted_iota(jnp.int32, (1, cols), 1) & (tq - 1))
    scs = []
    for n, k_ref in enumerate(k_refs):
        kpos = (i - (n_tiles - 1) + n) * tq + lax.broadcasted_iota(jnp.int32, (tq, 1), 0)
        if n == n_tiles - 1:
            valid = kpos <= qpos
        elif n == 0:
            valid = (kpos > qpos - WINDOW) & (kpos >= 0)
        else:
            valid = kpos >= 0
        scs.append(jnp.where(valid, _dot(k_ref[...], qa), NEG_INF))
    mx = scs[0].max(axis=0, keepdims=True)
    for sc in scs[1:]:
        mx = jnp.maximum(mx, sc.max(axis=0, keepdims=True))
    l_sum, acc = None, None
    for sc, v_ref in zip(scs, v_refs):
        pr = jnp.exp2(sc - mx)
        part_l, part_acc = jnp.sum(pr, axis=0, keepdims=True), _dot(v_ref[0], pr.astype(BF16))
        l_sum = part_l if l_sum is None else l_sum + part_l
        acc = part_acc if acc is None else acc + part_acc
    _store_heads(acc / l_sum, o_ref, tq)


def _attn_window_prompt(q3d, k_rows, v_t, *, tq):
    batch, seq, _ = q3d.shape
    assert WINDOW % tq == 0 and seq % tq == 0 and tq & (tq - 1) == 0
    n_tiles = WINDOW // tq + 1
    nq = seq // tq
    tile = lambda n: (lambda i: jnp.maximum(i - (n_tiles - 1) + n, 0))
    return pl.pallas_call(
        functools.partial(_window_body, tq=tq, n_tiles=n_tiles),
        grid=(batch, nq),
        in_specs=[pl.BlockSpec((1, tq, NSA_WIDTH), lambda b, i: (b, i, 0))]
        + [pl.BlockSpec((tq, KV_WIDTH), lambda b, i, t=tile(n): (b * nq + t(i), 0)) for n in range(n_tiles)]
        + [pl.BlockSpec((1, KV_WIDTH, tq), lambda b, i, t=tile(n): (b, 0, t(i))) for n in range(n_tiles)],
        out_specs=pl.BlockSpec((1, tq, NSA_WIDTH), lambda b, i: (b, i, 0)),
        out_shape=jax.ShapeDtypeStruct((batch, seq, NSA_WIDTH), F32),
        compiler_params=_cparams(("arbitrary", "arbitrary")),
        name="attn_win",
    )(q3d, *([k_rows] * n_tiles), *([v_t] * n_tiles))


def _block_onehot(seq):
    blk = np.arange(seq)[:, None] // SEL_BLOCK
    return jnp.asarray((np.arange(LANES)[None, :] % SEL_BLOCK) == blk, BF16)


def _attn_selected_prompt(q3d, k_rows, v_t, selb, *, tq, tk):
    batch, seq, _ = q3d.shape
    assert tq & (tq - 1) == 0 and tk % LANES == 0 and tq % LANES == 0 and selb.shape[2] == SEL_BLOCK
    tab = _attn_pairs(seq, tq, tk)
    cols = NSA_HEADS * tq
    C = ATTN_TAB_COLS
    nk = seq // tk
    return pl.pallas_call(
        functools.partial(_attn_body, tq=tq, tk=tk),
        grid_spec=pltpu.PrefetchScalarGridSpec(
            num_scalar_prefetch=1,
            grid=(batch, tab.shape[0]),
            in_specs=[pl.BlockSpec((1, tq, NSA_WIDTH), lambda b, p, t: (b, t[C * p], 0)),
                      pl.BlockSpec((tk, KV_WIDTH), lambda b, p, t: (b * nk + t[C * p + 1], 0)),
                      pl.BlockSpec((1, KV_WIDTH, tk), lambda b, p, t: (b, 0, t[C * p + 1])),
                      pl.BlockSpec((tk, LANES), lambda b, p, t: (t[C * p + 1], 0)),
                      pl.BlockSpec((1, NSA_KV_HEADS, SEL_BLOCK, tq),
                                   lambda b, p, t: (b, 0, 0, t[C * p]))],
            out_specs=pl.BlockSpec((1, tq, NSA_WIDTH), lambda b, p, t: (b, t[C * p], 0)),
            scratch_shapes=[pltpu.VMEM((2 * LANES, cols), BF16), pltpu.VMEM((1, cols), F32),
                            pltpu.VMEM((1, cols), F32), pltpu.VMEM((KV_WIDTH, cols), F32)]),
        out_shape=jax.ShapeDtypeStruct((batch, seq, NSA_WIDTH), F32),
        compiler_params=_cparams(("arbitrary", "arbitrary")),
        name="attn_sel",
    )(jnp.asarray(tab.reshape(-1)), q3d, k_rows, v_t, _block_onehot(seq), selb)


ATTN_PAGES_PER_STEP = 64
ATTN_PAGED_SPLIT = 2


def _attn_paged_body(pt_ref, qa_ref, bq_ref, bn_ref, kn_ref, oh_ref, pool_hbm, o_ref, m_ref, l_ref,
                     acc_ref, pages_ref, sem_ref, *, n_pages, n_new):
    slot = _gather_pages(pt_ref, pool_hbm, pages_ref, sem_ref, n_pages)
    page_refs = [pages_ref.at[slot, j] for j in range(n_pages)]
    c = pl.program_id(1)
    rows = qa_ref.shape[1]

    @pl.when(c == 0)
    def _():
        _softmax_init(m_ref, l_ref, acc_ref)

    n_split = m_ref.shape[0]
    per = n_pages // n_split
    keys = per * PAGE_SIZE
    qa = qa_ref[0]
    bias = bq_ref[0, 0]
    blocks = keys // SEL_BLOCK
    lane = lax.broadcasted_iota(jnp.int32, (1, LANES), 1)
    scs, vts = [], []
    for s in range(n_split):
        refs_s = page_refs[s * per:(s + 1) * per]
        bias_s = bias if s == 0 else pltpu.roll(bias, LANES - s * blocks, axis=1)
        lhs = jnp.concatenate([qa, jnp.where(lane < blocks, bias_s, 0.0)], axis=1).astype(BF16)
        kt = jnp.concatenate([r[0:KV_WIDTH, :] for r in refs_s], axis=1)
        rhs = jnp.concatenate([kt.astype(BF16), oh_ref[...]], axis=0)
        scs.append(_dot(lhs, rhs))
        vts.append(jnp.concatenate([r[KV_WIDTH:, :] for r in refs_s], axis=1).astype(BF16))
    for s in range(n_split):
        _softmax_update(scs[s], vts[s], m_ref.at[s], l_ref.at[s], acc_ref.at[s])

    @pl.when(c == pl.num_programs(1) - 1)
    def _():
        kn = kn_ref[0]
        sc = _dot(qa.astype(BF16), kn[0:KV_WIDTH, :].astype(BF16)) + bn_ref[0]
        tq = lax.broadcasted_iota(jnp.int32, (rows, 1), 0) % n_new
        kk = lax.broadcasted_iota(jnp.int32, (1, kn.shape[1]), 1)
        sc = jnp.where((kk <= tq) & (kk < n_new), sc, NEG_INF)
        _softmax_update(sc, kn[KV_WIDTH:, :].astype(BF16), m_ref.at[0], l_ref.at[0], acc_ref.at[0])
        m_all = m_ref[0]
        for s in range(1, n_split):
            m_all = jnp.maximum(m_all, m_ref[s])
        l_all = jnp.zeros(m_all.shape, F32)
        acc_all = jnp.zeros(m_all.shape, F32)
        for s in range(n_split):
            scale = jnp.exp(m_ref[s] - m_all)
            l_all = l_all + scale * l_ref[s]
            acc_all = acc_all + scale * acc_ref[s]
        o_ref[0] = acc_all / l_all


def _attn_paged(qa, bias_q, bias_new, kv_new_t, pool, page_table, *, n_new):
    batch, n_pages = page_table.shape
    pps = ATTN_PAGES_PER_STEP
    keys_per_chain = pps // ATTN_PAGED_SPLIT * PAGE_SIZE
    assert n_pages % pps == 0 and pps * PAGE_SIZE // SEL_BLOCK <= LANES
    assert keys_per_chain // SEL_BLOCK <= SEL_BLOCK
    n_steps = n_pages // pps
    rows = qa.shape[1]

    per_b = lambda b, c, pt: (b, 0, 0)
    return pl.pallas_call(
        functools.partial(_attn_paged_body, n_pages=pps, n_new=n_new),
        grid_spec=pltpu.PrefetchScalarGridSpec(
            num_scalar_prefetch=1,
            grid=(batch, n_steps),
            in_specs=[pl.BlockSpec((1, rows, LANES), per_b),
                      pl.BlockSpec((1, 1, rows, LANES), lambda b, c, pt: (b, c, 0, 0)),
                      pl.BlockSpec((1, rows, LANES), per_b),
                      pl.BlockSpec((1,) + kv_new_t.shape[1:], per_b),
                      pl.BlockSpec((LANES, keys_per_chain), lambda b, c, pt: (0, 0)),
                      pl.BlockSpec(memory_space=pl.ANY)],
            out_specs=pl.BlockSpec((1, rows, LANES), per_b),
            scratch_shapes=[pltpu.VMEM((ATTN_PAGED_SPLIT, rows, LANES), F32)] * 3
            + [pltpu.VMEM((2, pps, 2 * KV_WIDTH, PAGE_SIZE), F32), pltpu.SemaphoreType.DMA((2,))]),
        out_shape=jax.ShapeDtypeStruct((batch, rows, LANES), F32),
        compiler_params=_cparams(("arbitrary", "arbitrary")),
        name="attn_sel_paged",
    )(page_table.reshape(-1), qa, bias_q, bias_new, kv_new_t, _block_onehot(keys_per_chain).T, pool)


def _attn_window_body(qa_ref, wb_ref, kn_ref, o_ref, *, n_new, past):
    qa = qa_ref[0].astype(BF16)
    wb, kn = wb_ref[0], kn_ref[0]
    rows, n_buf = qa.shape[0], wb.shape[1]
    qpos = past + lax.broadcasted_iota(jnp.int32, (rows, 1), 0) % n_new

    def masked(sc, kpos, extra):
        diff = qpos - kpos
        return jnp.where((diff >= 0) & (diff < WINDOW) & (kpos >= 0) & extra, sc, NEG_INF)

    nb = lax.broadcasted_iota(jnp.int32, (1, n_buf), 1)
    nn = lax.broadcasted_iota(jnp.int32, (1, kn.shape[1]), 1)
    sb = masked(_dot(qa, wb[0:KV_WIDTH, :].astype(BF16)), past - n_buf + nb, nb >= 0)
    sn = masked(_dot(qa, kn[0:KV_WIDTH, :].astype(BF16)), past + nn, nn < n_new)
    mx = jnp.maximum(jnp.max(sb, axis=1, keepdims=True), jnp.max(sn, axis=1, keepdims=True))
    pb, pn = jnp.exp(sb - mx), jnp.exp(sn - mx)
    o = (_dot_nt(pb.astype(BF16), wb[KV_WIDTH:, :].astype(BF16))
         + _dot_nt(pn.astype(BF16), kn[KV_WIDTH:, :].astype(BF16)))
    o_ref[0] = o / (jnp.sum(pb, axis=1, keepdims=True) + jnp.sum(pn, axis=1, keepdims=True))


def _attn_window_small(qa, win_t, kv_new_t, *, n_new, past):
    batch, rows, _ = qa.shape
    per_b = lambda b: (b, 0, 0)
    return pl.pallas_call(
        functools.partial(_attn_window_body, n_new=n_new, past=past),
        grid=(batch,),
        in_specs=[pl.BlockSpec((1, rows, LANES), per_b),
                  pl.BlockSpec((1,) + win_t.shape[1:], per_b),
                  pl.BlockSpec((1,) + kv_new_t.shape[1:], per_b)],
        out_specs=pl.BlockSpec((1, rows, LANES), per_b),
        out_shape=jax.ShapeDtypeStruct((batch, rows, LANES), F32),
        compiler_params=_cparams(("arbitrary",)),
        name="attn_win_small",
    )(qa, win_t, kv_new_t)


FFN_TM = 512
FFN_VMEM_LIMIT = 58 * 1024 * 1024
MXU_DEPTH = 256
FFN_CHUNKS = ((0, 6 * MXU_DEPTH), (6 * MXU_DEPTH, D_FF))


def _ffn_body(x_ref, om_ref, oc_ref, os_ref, ow_ref, gt_ref, ge_ref, gn_ref, gf_ref, gl_ref, wc_ref,
              fb_ref, wo_hbm, wu_hbm, wd_hbm, y_ref, fn_ref, xx_ref, wo_ref, wu_ref, wd_ref, sem_ref,
              *, tm, stride, halo):
    s = pl.program_id(1)

    @pl.when((pl.program_id(0) == 0) & (s == 0))
    def _():
        copies = [pltpu.make_async_copy(src, dst, sem_ref.at[n])
                  for n, (src, dst) in enumerate(((wo_hbm, wo_ref), (wu_hbm, wu_ref), (wd_hbm, wd_ref)))]
        for cp in copies:
            cp.start()
        for cp in copies:
            cp.wait()

    sig = _sigmoid(gt_ref[...])
    hi = sig.astype(BF16)
    lo = (sig - hi.astype(F32)).astype(BF16)
    comb = None
    for br, ob_ref in enumerate((oc_ref, os_ref, ow_ref)):
        gate = _dot(hi, ge_ref[br]) + _dot(lo, ge_ref[br])
        term = gate * ob_ref[...]
        comb = term if comb is None else comb + term
    onsa = _rms(comb, gn_ref[...])
    h = (x_ref[...] + _dot(om_ref[...].astype(BF16), wo_ref[0:MLSTM_WIDTH, :])
         + _dot(onsa.astype(BF16), wo_ref[MLSTM_WIDTH:, :]))
    hn = _rms(h, gf_ref[...]).astype(BF16)

    base = halo - (FFN_CONV - 1) * stride

    @pl.when(s == 0)
    def _():
        xx_ref[base:halo, :] = fb_ref[0]

    y_ref[...] = h
    for lo_col, hi_col in FFN_CHUNKS:
        convs = []
        for half in range(2):
            cols = slice(half * D_FF + lo_col, half * D_FF + hi_col)
            xx_ref[halo:halo + tm, cols] = _dot(hn, wu_ref[:, cols])
            conv = xx_ref[base:base + tm, cols] * wc_ref[0:1, cols]
            for j in range(1, FFN_CONV):
                conv = conv + xx_ref[base + j * stride:base + j * stride + tm, cols] * wc_ref[j:j + 1, cols]
            convs.append(conv)
        act = _silu(convs[1]) * convs[0]
        y_ref[...] += _dot(act.astype(BF16), wd_ref[lo_col:hi_col, :])
    fn_ref[0, 0] = xx_ref[tm + base:tm + halo, :]
    xx_ref[0:halo, :] = xx_ref[tm:tm + halo, :]
    y_ref[...] = _rms(y_ref[...], gl_ref[...])


def _gate_expand():
    ge = np.zeros((N_BRANCH, LANES, NSA_WIDTH), np.float32)
    for hd in range(NSA_HEADS):
        for br in range(N_BRANCH):
            ge[br, GATE_COL_NSA + hd * N_BRANCH + br, hd * HEAD_DIM:(hd + 1) * HEAD_DIM] = 1.0
    return jnp.asarray(ge, BF16)


def _ffn(x2d, om, oc, osel, ow, gt, fbuf, w_out, g_nsa, g_ffn, g_final, w_up, w_fconv, w_down,
         *, nb, tm, stride):
    rows = x2d.shape[0]
    ns = rows // (nb * tm)
    halo = -(-(FFN_CONV - 1) * stride // SUBLANES) * SUBLANES
    assert tm >= halo and all((hi - lo) % MXU_DEPTH == 0 for lo, hi in FFN_CHUNKS)
    tok = lambda b, s: (b * ns + s, 0)
    nfb = (FFN_CONV - 1) * stride

    def const(shape):
        return pl.BlockSpec(shape, lambda b, s: (0,) * len(shape))

    hbm = pl.BlockSpec(memory_space=pl.ANY)
    y, fn = pl.pallas_call(
        functools.partial(_ffn_body, tm=tm, stride=stride, halo=halo),
        grid=(nb, ns),
        in_specs=[pl.BlockSpec((tm, D_MODEL), tok)] + [pl.BlockSpec((tm, NSA_WIDTH), tok)] * 4
        + [pl.BlockSpec((tm, LANES), tok),
           const((N_BRANCH, LANES, NSA_WIDTH)), const((1, NSA_WIDTH)), const((1, D_MODEL)),
           const((1, D_MODEL)), const((FFN_CONV, 2 * D_FF)),
           pl.BlockSpec((1, nfb, 2 * D_FF), lambda b, s: (b, 0, 0)), hbm, hbm, hbm],
        out_specs=[pl.BlockSpec((tm, D_MODEL), tok),
                   pl.BlockSpec((1, 1, nfb, 2 * D_FF), lambda b, s: (b, s, 0, 0))],
        out_shape=[jax.ShapeDtypeStruct((rows, D_MODEL), F32),
                   jax.ShapeDtypeStruct((nb, ns, nfb, 2 * D_FF), F32)],
        scratch_shapes=[pltpu.VMEM((halo + tm, 2 * D_FF), F32),
                        pltpu.VMEM((D_MODEL, D_MODEL), BF16), pltpu.VMEM((D_MODEL, 2 * D_FF), BF16),
                        pltpu.VMEM((D_FF, D_MODEL), BF16), pltpu.SemaphoreType.DMA((3,))],
        compiler_params=pltpu.CompilerParams(dimension_semantics=("arbitrary", "arbitrary"),
                                             vmem_limit_bytes=FFN_VMEM_LIMIT),
        name="outproj_ffn",
    )(x2d, om, oc, osel, ow, gt, _gate_expand(), g_nsa.reshape(1, -1), g_ffn.reshape(1, -1),
      g_final.reshape(1, -1), w_fconv, fbuf, w_out.astype(BF16), w_up.astype(BF16),
      w_down.astype(BF16))
    return y, fn[:, ns - 1]


PROMPT_TM = 512
PROMPT_TQ_CMP = 512
PROMPT_TT_TOPK = 1024
PROMPT_TQ_SEL = 512
PROMPT_TK_SEL = 512
PROMPT_TQ_WIN = 256


def _kv_rows(kv_t):
    batch, _, rows = kv_t.shape
    return kv_t.reshape(batch, 2, NSA_KV_HEADS, HEAD_DIM, rows).transpose(0, 4, 1, 2, 3)


def _kv_feature_major(kv5):
    batch, rows = kv5.shape[:2]
    return kv5.transpose(0, 2, 3, 4, 1).reshape(batch, 2 * KV_WIDTH, rows)


def _prompt_layer(x, wts):
    batch, seq, _ = x.shape
    x2d = x.reshape(batch * seq, D_MODEL)
    q, kc_rows, vc_rows, mu, mv, mo, gt, ks_rows, kw_rows, kvc_t, kvs_t, kvw_t, vs_t, vw_t = _in_proj(
        x2d, wts["g_mix"], wts["w_in_packed"], batch=batch, seq=seq, tm=min(PROMPT_TM, seq))
    H, DH, W = MLSTM_HEADS, MLSTM_DH, MLSTM_WIDTH
    o_m, mconv, c_new, n_new, m_new = _mlstm(
        mu, mv, mo, gt, jnp.zeros((batch, MLSTM_CONV - 1, W), F32), jnp.zeros((batch, H, DH, DH), F32),
        jnp.zeros((batch, H, DH), F32), jnp.zeros((batch, H), F32),
        wts["w_mconv"], wts["b_mconv"], wts["w_mq"], wts["w_mk"], wts["b_ig"], wts["b_fg"],
        wts["g_mhead"], wts["m_skip"], batch=batch, seq=seq)
    kce, kco = _compress_prompt(kc_rows, vc_rows, wts["cw"], batch=batch, seq=seq)
    n_sel = -(-seq // SEL_BLOCK)
    assert n_sel <= SEL_BLOCK
    o_cmp, scores_t = _cmp_attn(q, kce, kco, batch=batch, seq=seq, tq=min(PROMPT_TQ_CMP, seq), pos0=0)
    selb = _topk_blocks(scores_t.reshape(batch * NSA_KV_HEADS, -1, seq),
                        jnp.arange(seq, dtype=jnp.int32).reshape(1, seq),
                        n_sel=n_sel, nsw=SEL_BLOCK, tt=min(PROMPT_TT_TOPK, seq))
    selb = selb.reshape(batch, NSA_KV_HEADS, SEL_BLOCK, seq)
    q3d = q.reshape(batch, seq, NSA_WIDTH)
    o_sel = _attn_selected_prompt(q3d, ks_rows, vs_t, selb, tq=min(PROMPT_TQ_SEL, seq),
                                  tk=min(PROMPT_TK_SEL, seq))
    o_win = _attn_window_prompt(q3d, kw_rows, vw_t, tq=PROMPT_TQ_WIN)
    fbuf = jnp.zeros((batch, FFN_CONV - 1, 2 * D_FF), F32)
    y, f_new = _ffn(x2d, o_m, o_cmp, o_sel.reshape(-1, NSA_WIDTH), o_win.reshape(-1, NSA_WIDTH), gt,
                    fbuf, wts["w_out"], wts["g_nsa"], wts["g_ffn"], wts["g_final"], wts["w_up"],
                    wts["w_fconv"], wts["w_down"], nb=batch, tm=min(FFN_TM, seq), stride=1)
    n_win = min(WINDOW, seq)
    return (y.reshape(batch, seq, D_MODEL), _kv_rows(kvc_t), _kv_rows(kvs_t),
            _kv_rows(kvw_t[:, :, seq - n_win:]), mconv, c_new, n_new, m_new.reshape(batch, H), f_new)


def _decode_rows(q2d, batch, seq):
    q5 = (q2d * ATTN_SCALE).reshape(batch, seq, NSA_KV_HEADS, NSA_GROUP, HEAD_DIM).transpose(0, 2, 3, 1, 4)
    eye = jnp.eye(NSA_KV_HEADS, dtype=F32)
    qa = jnp.einsum('bkgtd,kK->bkgtKd', q5, eye)
    return qa.reshape(batch, NSA_KV_HEADS * NSA_GROUP * seq, KV_WIDTH)


def _decode_rows_out(o, batch, seq):
    o6 = o.reshape(batch, NSA_KV_HEADS, NSA_GROUP, seq, NSA_KV_HEADS, HEAD_DIM)
    o5 = jnp.stack([o6[:, kh, :, :, kh, :] for kh in range(NSA_KV_HEADS)], axis=1)
    return o5.transpose(0, 3, 1, 2, 4).reshape(batch * seq, NSA_WIDTH)


def _sample_layer(x, pool_cmp, pool_sel, win_buf, m_conv, m_c, m_n, m_m, f_buf, page_table, wts):
    batch, seq, _ = x.shape
    n_pages = page_table.shape[1]
    past = n_pages * PAGE_SIZE
    assert past % SEL_BLOCK == 0 and seq <= SEL_BLOCK and seq < CMP_BLOCK
    x2d = x.reshape(batch * seq, D_MODEL)
    q, _, _, mu, mv, mo, gt, _, _, kvc_t, kvs_t, kvw_t, _, _ = _in_proj(
        x2d, wts["g_mix"], wts["w_in_packed"], batch=1, seq=batch * seq, tm=batch * seq)
    per_batch = lambda a: a.reshape(2 * KV_WIDTH, batch, seq).transpose(1, 0, 2)
    kvc_t, kvs_t, kvw_t = per_batch(kvc_t), per_batch(kvs_t), per_batch(kvw_t)
    pad_keys = lambda a: jnp.pad(a, ((0, 0), (0, 0), (0, LANES - seq)))
    H = MLSTM_HEADS
    o_m, mconv, c_new, n_new, m_new = _mlstm(
        mu, mv, mo, gt, m_conv, m_c, m_n, m_m,
        wts["w_mconv"], wts["b_mconv"], wts["w_mq"], wts["w_mk"], wts["b_ig"], wts["b_fg"],
        wts["g_mhead"], wts["m_skip"], batch=batch, seq=seq)
    pool_cmp3, pool_sel3 = _kv_feature_major(pool_cmp), _kv_feature_major(pool_sel)
    kce, kco = _compress_paged(pool_cmp3, page_table, wts["cw"], wts["cw_pages"])
    n_past_blk = past // SEL_BLOCK
    n_sel = -(-(past + seq) // SEL_BLOCK)
    o_cmp, scores_t = _cmp_attn(q, kce, kco, batch=batch, seq=seq, tq=seq, pos0=past)
    ns = scores_t.shape[2]
    nsw = ns + LANES
    scores_all = scores_t.transpose(1, 2, 0, 3).reshape(NSA_KV_HEADS, ns, batch * seq)
    pos_all = (past + jnp.arange(batch * seq, dtype=jnp.int32) % seq).reshape(1, batch * seq)
    selb = _topk_blocks(scores_all, pos_all, n_sel=n_sel, nsw=nsw, tt=batch * seq)
    selb = selb.reshape(NSA_KV_HEADS, nsw, batch, seq).transpose(2, 0, 3, 1)
    qa = _decode_rows(q, batch, seq)
    rows = qa.shape[1]
    blk_per_step = ATTN_PAGES_PER_STEP * PAGE_SIZE // SEL_BLOCK
    n_steps = n_pages // ATTN_PAGES_PER_STEP
    sb_rows = jnp.broadcast_to(selb[:, :, None], (batch, NSA_KV_HEADS, NSA_GROUP, seq, selb.shape[-1]))
    sb_rows = sb_rows.reshape(batch, rows, selb.shape[-1])
    bias_q = sb_rows[:, :, :n_past_blk].reshape(batch, rows, n_steps, blk_per_step).transpose(0, 2, 1, 3)
    bias_q = jnp.pad(bias_q, ((0, 0), (0, 0), (0, 0), (0, LANES - blk_per_step)))
    bias_new = jnp.broadcast_to(sb_rows[:, :, n_past_blk:n_past_blk + 1], (batch, rows, LANES))
    o_sel = _attn_paged(qa, bias_q, bias_new, pad_keys(kvs_t), pool_sel3, page_table, n_new=seq)
    n_buf = win_buf.shape[1]
    assert past >= n_buf
    win_t = _kv_feature_major(win_buf)
    o_win = _attn_window_small(qa, win_t, pad_keys(kvw_t), n_new=seq, past=past)
    win_new = jnp.concatenate([win_t, kvw_t], axis=2)[:, :, seq:]
    tmaj = lambda a: a.reshape(batch, seq, -1).transpose(1, 0, 2).reshape(batch * seq, -1)
    fb_t = f_buf.transpose(1, 0, 2).reshape(1, (FFN_CONV - 1) * batch, 2 * D_FF)
    y, f_new = _ffn(tmaj(x2d), tmaj(o_m), tmaj(o_cmp), tmaj(_decode_rows_out(o_sel, batch, seq)),
                    tmaj(_decode_rows_out(o_win, batch, seq)), tmaj(gt), fb_t,
                    wts["w_out"], wts["g_nsa"], wts["g_ffn"], wts["g_final"], wts["w_up"],
                    wts["w_fconv"], wts["w_down"], nb=1, tm=batch * seq, stride=batch)
    y = y.reshape(seq, batch, D_MODEL).transpose(1, 0, 2)
    f_new = f_new.reshape(FFN_CONV - 1, batch, 2 * D_FF).transpose(1, 0, 2)
    return (y, _kv_rows(kvc_t), _kv_rows(kvs_t), _kv_rows(win_new), mconv, c_new, n_new,
            m_new.reshape(batch, H), f_new)


def kernel(x_prompt, x_sample, cache_cmp, cache_sel, state_win, state_mlstm_C, state_mlstm_n,
           state_mlstm_m, state_mlstm_conv, state_ffn_conv, page_table,
           g_mix, w_in, w_out, w_mconv, b_mconv, w_mq, w_mk, b_ig, b_fg, g_mhead, m_skip,
           pe_cmp, w_cmp1, w_cmp2, g_nsa, g_ffn, w_up, w_fconv, w_down, g_final):
    assert w_in.shape[0] == 1, "one layer: the final norm is fused into the layer's FFN kernel"
    l = 0
    wts = dict(g_mix=g_mix[l], w_in_packed=_pack_w_in(w_in[l]), w_out=w_out[l], w_mconv=w_mconv[l],
               b_mconv=b_mconv[l], w_mq=w_mq[l], w_mk=w_mk[l], b_ig=b_ig[l], b_fg=b_fg[l],
               g_mhead=g_mhead[l], m_skip=m_skip[l],
               cw=_pack_compress_weights(pe_cmp[l], w_cmp1[l], w_cmp2[l]),
               cw_pages=_page_pair_constants(pe_cmp[l]),
               g_nsa=g_nsa[l], g_ffn=g_ffn[l], g_final=g_final, w_up=w_up[l], w_fconv=w_fconv[l],
               w_down=w_down[l])
    p = _prompt_layer(x_prompt, wts)
    s = _sample_layer(x_sample, cache_cmp[l], cache_sel[l], state_win[l], state_mlstm_conv[l],
                      state_mlstm_C[l], state_mlstm_n[l], state_mlstm_m[l], state_ffn_conv[l],
                      page_table, wts)
    yp, cmp_p, sel_p, win_p, mconv_p, c_p, n_p, m_p, fconv_p = p
    ys, cmp_s, sel_s, win_s, mconv_s, c_s, n_s, m_s, fconv_s = s
    st = lambda a: a[None]
    return (yp, ys, st(cmp_p), st(cmp_s), st(sel_p), st(sel_s), st(win_p), st(win_s),
            st(c_p), st(c_s), st(n_p), st(n_s), st(m_p), st(m_s), st(mconv_p), st(mconv_s),
            st(fconv_p), st(fconv_s))
```

```python
import functools

import numpy as np
import jax
import jax.numpy as jnp
from jax import lax
from jax.experimental import pallas as pl
from jax.experimental.pallas import tpu as pltpu

F32 = jnp.float32
BF16 = jnp.bfloat16

D_MODEL = 1024
PAGE_SIZE = 128
HEAD_DIM = 64
NSA_HEADS = 8
NSA_KV_HEADS = 2
NSA_GROUP = NSA_HEADS // NSA_KV_HEADS
NSA_WIDTH = NSA_HEADS * HEAD_DIM
KV_WIDTH = NSA_KV_HEADS * HEAD_DIM
CMP_BLOCK = 32
CMP_HIDDEN = 2 * HEAD_DIM
SEL_BLOCK = 64
TOP_N = 16
WINDOW = 512
N_BRANCH = 3
ATTN_SCALE = HEAD_DIM ** -0.5
MLSTM_HEADS = 4
MLSTM_WIDTH = D_MODEL - NSA_WIDTH
MLSTM_DH = MLSTM_WIDTH // MLSTM_HEADS
MLSTM_CONV = 4
D_FF = ((8 * D_MODEL // 3 + 127) // 128) * 128
FFN_CONV = 3
EPS = 1e-6
NEG_INF = -1e30
SEL_PRIORITY = 1e4
LOG2_E = 1.4426950408889634

LANES = 128
SUBLANES = 8
VMEM_BYTES = 64 * 1024 * 1024
VMEM_LIMIT = 3 * VMEM_BYTES // 4

GATE_COL_NSA = 0
GATE_COL_I = NSA_HEADS * N_BRANCH
GATE_COL_F = GATE_COL_I + MLSTM_HEADS

MLSTM_CHUNK = 128
MLSTM_SEQS_PER_STEP = 4


def _cparams(sem):
    return pltpu.CompilerParams(dimension_semantics=sem, vmem_limit_bytes=VMEM_LIMIT)


def _dot(a, b):
    return jnp.dot(a, b, preferred_element_type=F32)


def _dot_nt(a, b):
    return lax.dot_general(a, b, (((1,), (1,)), ((), ())), preferred_element_type=F32)


def _sigmoid(x):
    return 1.0 / (1.0 + jnp.exp(-x))


def _silu(x):
    return x * _sigmoid(x)


def _rms(x, g):
    return x * lax.rsqrt(jnp.mean(x * x, axis=-1, keepdims=True) + EPS) * g


IN_ROW_WIDTHS = (NSA_WIDTH, KV_WIDTH, KV_WIDTH, MLSTM_WIDTH, MLSTM_WIDTH, MLSTM_WIDTH, LANES,
                 KV_WIDTH, KV_WIDTH)
IN_ROW_DTYPES = (F32,) * 7 + (BF16,) * 2
N_KV_BRANCH = 3


def _inproj_body(x_ref, g_ref, w_ref, wt_ref, *out_refs):
    xb = _rms(x_ref[...], g_ref[...]).astype(BF16)
    off = 0
    n_rows = len(IN_ROW_WIDTHS)
    for ref in out_refs[:n_rows]:
        n = ref.shape[-1]
        ref[...] = _dot(xb, w_ref[:, off:off + n]).astype(ref.dtype)
        off += n
    kv_refs = out_refs[n_rows:n_rows + N_KV_BRANCH]
    vt_refs = out_refs[n_rows + N_KV_BRANCH:]
    for n, ref in enumerate(kv_refs):
        kv_t = _dot_nt(wt_ref[n * 2 * KV_WIDTH:(n + 1) * 2 * KV_WIDTH, :], xb)
        ref[0] = kv_t
        if n > 0:
            vt_refs[n - 1][0] = kv_t[KV_WIDTH:, :].astype(BF16)


def _pack_w_in(w_in):
    splits = np.cumsum([NSA_WIDTH, 2 * KV_WIDTH, 2 * KV_WIDTH, 2 * KV_WIDTH, NSA_HEADS * N_BRANCH,
                        MLSTM_WIDTH, MLSTM_WIDTH, MLSTM_WIDTH, MLSTM_HEADS]).tolist()
    q, kvc, kvs, kvw, gt, mu, mv, mo, mi, mf = jnp.split(w_in, splits, axis=1)
    gates = jnp.concatenate([gt, mi, mf], axis=1)
    gates = jnp.pad(gates, ((0, 0), (0, LANES - gates.shape[1])))
    w_rows = jnp.concatenate([q, kvc, mu, mv, mo, gates, kvs[:, :KV_WIDTH], kvw[:, :KV_WIDTH]],
                             axis=1).astype(BF16)
    w_kv_t = jnp.concatenate([kvc, kvs, kvw], axis=1).T.astype(BF16)
    return w_rows, w_kv_t


def _in_proj(x2d, g_mix, w_packed, *, batch, seq, tm):
    w_rows, w_kv_t = w_packed
    t = x2d.shape[0]
    ns = seq // tm
    kv_sd = jax.ShapeDtypeStruct((batch, 2 * KV_WIDTH, seq), F32)
    vt_sd = jax.ShapeDtypeStruct((batch, KV_WIDTH, seq), BF16)
    feat_major = lambda rows: pl.BlockSpec((1, rows, tm), lambda i: (i // ns, 0, i % ns))
    return pl.pallas_call(
        _inproj_body,
        grid=(t // tm,),
        in_specs=[pl.BlockSpec((tm, D_MODEL), lambda i: (i, 0)),
                  pl.BlockSpec((1, D_MODEL), lambda i: (0, 0)),
                  pl.BlockSpec(w_rows.shape, lambda i: (0, 0)),
                  pl.BlockSpec(w_kv_t.shape, lambda i: (0, 0))],
        out_specs=[pl.BlockSpec((tm, n), lambda i: (i, 0)) for n in IN_ROW_WIDTHS]
        + [feat_major(2 * KV_WIDTH)] * N_KV_BRANCH + [feat_major(KV_WIDTH)] * (N_KV_BRANCH - 1),
        out_shape=[jax.ShapeDtypeStruct((t, n), dt) for n, dt in zip(IN_ROW_WIDTHS, IN_ROW_DTYPES)]
        + [kv_sd] * N_KV_BRANCH + [vt_sd] * (N_KV_BRANCH - 1),
        compiler_params=_cparams(("arbitrary",)),
        name="in_proj",
    )(x2d, g_mix.reshape(1, D_MODEL), w_rows, w_kv_t)


def _mlstm_body(*refs, valid, bb):
    cb_ref, c0_ref, n0_ref, m0_ref = refs[4:8]
    cn_ref, c_ref, n_ref, m_ref, xx_ref = refs[16:21]
    halo = SUBLANES

    @pl.when(pl.program_id(1) == 0)
    def _():
        xx_ref[:, 0:halo, :] = jnp.zeros((bb, halo, MLSTM_WIDTH), F32)
        xx_ref[:, halo - (MLSTM_CONV - 1):halo, :] = cb_ref[...]
        c_ref[...] = c0_ref[...]
        n_ref[...] = n0_ref[...]
        m_ref[...] = m0_ref[...]

    _mlstm_chunk(*refs, valid=valid, bb=bb)


def _mlstm_chunk(mu_ref, mv_ref, mo_ref, g_ref, cb_ref, c0_ref, n0_ref, m0_ref,
                 wc_ref, bc_ref, wq_ref, wk_ref, gb_ref, gh_ref, sk_ref,
                 o_ref, cn_ref, c_ref, n_ref, m_ref,
                 xx_ref, vpad_ref, gpad_ref, *, valid, bb):
    L = MLSTM_CHUNK
    DH = MLSTM_DH
    halo = SUBLANES
    units = [(bi, h) for bi in range(bb) for h in range(MLSTM_HEADS)]
    head_lanes = lambda h: slice(h * DH, (h + 1) * DH)
    row = lax.broadcasted_iota(jnp.int32, (L, L), 0)
    col = lax.broadcasted_iota(jnp.int32, (L, L), 1)
    tril = row >= col
    triu = row <= col
    tok_col = lax.broadcasted_iota(jnp.int32, (L, 1), 0)
    tok_row = lax.broadcasted_iota(jnp.int32, (1, L), 1)

    def log_sigmoid(x):
        return jnp.minimum(x, 0.0) - jnp.log(1.0 + jnp.exp(-jnp.abs(x)))

    uc, gb, gbt = {}, {}, {}
    for bi in range(bb):
        if valid < L:
            xx_ref[bi, halo:, :] = jnp.zeros((L, MLSTM_WIDTH), F32)
            vpad_ref[bi] = jnp.zeros((L, MLSTM_WIDTH), F32)
            gpad_ref[bi] = jnp.zeros((L, LANES), F32)
        xx_ref[bi, halo:halo + valid, :] = mu_ref[bi]
        vpad_ref[bi, 0:valid, :] = mv_ref[bi]
        gpad_ref[bi, 0:valid, :] = g_ref[bi]
        conv = xx_ref[bi, halo - 3:halo - 3 + L, :] * wc_ref[0:1, :]
        for j in range(1, MLSTM_CONV):
            conv = conv + xx_ref[bi, halo - 3 + j:halo - 3 + j + L, :] * wc_ref[j:j + 1, :]
        uc[bi] = _silu(conv + bc_ref[...])
        tail = xx_ref[bi, valid + halo - 3:valid + halo, :]
        xx_ref[bi, halo - 3:halo, :] = tail
        cn_ref[bi] = tail
        gb[bi] = gpad_ref[bi] + gb_ref[...]
        gbt[bi] = gb[bi].T

    q, k, qb, kb = {}, {}, {}, {}
    for u in units:
        bi, h = u
        ub = uc[bi][:, head_lanes(h)].astype(BF16)
        q[u] = _dot(ub, wq_ref[h])
        k[u] = _dot(ub, wk_ref[h]) * (DH ** -0.5)
        qb[u], kb[u] = q[u].astype(BF16), k[u].astype(BF16)

    ic_col, ic_row, cum_col, cum_row = {}, {}, {}, {}
    for u in units:
        bi, h = u
        ic_c = gb[bi][:, GATE_COL_I + h:GATE_COL_I + h + 1]
        ic_r = gbt[bi][GATE_COL_I + h:GATE_COL_I + h + 1, :]
        lf_c = log_sigmoid(gb[bi][:, GATE_COL_F + h:GATE_COL_F + h + 1])
        lf_r = log_sigmoid(gbt[bi][GATE_COL_F + h:GATE_COL_F + h + 1, :])
        if valid < L:
            ic_c = jnp.where(tok_col < valid, ic_c, NEG_INF)
            ic_r = jnp.where(tok_row < valid, ic_r, NEG_INF)
            lf_c = jnp.where(tok_col < valid, lf_c, 0.0)
            lf_r = jnp.where(tok_row < valid, lf_r, 0.0)
        ic_col[u], ic_row[u] = ic_c, ic_r
        cum_col[u] = jnp.sum(jnp.where(tril, lf_r, 0.0), axis=1, keepdims=True)
        cum_row[u] = jnp.sum(jnp.where(triu, lf_c, 0.0), axis=0, keepdims=True)

    m_t, w, sc = {}, {}, {}
    for u in units:
        bi, h = u
        m0 = m_ref[bi, 0:1, h:h + 1]
        dmat = jnp.where(tril, cum_col[u] - cum_row[u] + ic_row[u], NEG_INF)
        inter = cum_col[u] + m0
        m_t[u] = jnp.maximum(inter, jnp.max(dmat, axis=1, keepdims=True))
        w[u] = jnp.exp(dmat - m_t[u])
        sc[u] = jnp.exp(inter - m_t[u])

    hc = {}
    for u in units:
        bi, h = u
        s = _dot_nt(qb[u], kb[u]) * w[u]
        v = vpad_ref[bi, :, head_lanes(h)]
        c_old = c_ref[bi, h]
        n_old = n_ref[bi, h:h + 1, :]
        num = _dot(s.astype(BF16), v.astype(BF16)) + sc[u] * _dot_nt(qb[u], c_old.astype(BF16))
        den = (jnp.sum(s, axis=1, keepdims=True)
               + sc[u] * jnp.sum(q[u] * n_old, axis=1, keepdims=True))
        hc[u] = num / jnp.maximum(jnp.abs(den), jnp.exp(-m_t[u]))

    for u in units:
        bi, h = u
        m0 = m_ref[bi, 0:1, h:h + 1]
        m_new = m_t[u][L - 1:L, :]
        cum_last = cum_col[u][L - 1:L, :]
        wl = jnp.exp(cum_last - cum_col[u] + ic_col[u] - m_new)
        sl = jnp.exp(cum_last + m0 - m_new)
        v = vpad_ref[bi, :, head_lanes(h)]
        vw_t = (v * wl).T.astype(BF16)
        c_ref[bi, h] = sl * c_ref[bi, h] + _dot(vw_t, kb[u])
        n_ref[bi, h:h + 1, :] = sl * n_ref[bi, h:h + 1, :] + jnp.sum(wl * k[u], axis=0, keepdims=True)
        m_ref[bi, 0:1, h:h + 1] = m_new

    for u in units:
        bi, h = u
        hn = _rms(hc[u], gh_ref[:, head_lanes(h)])
        u_h = uc[bi][:, head_lanes(h)]
        out = ((hn[0:valid, :] + sk_ref[:, head_lanes(h)] * u_h[0:valid, :])
               * _sigmoid(mo_ref[bi, :, head_lanes(h)]))
        o_ref[bi, :, head_lanes(h)] = out


def _mlstm(mu, mv, mo, gates, conv_buf, c0, n0, m0, w_mconv, b_mconv, w_mq, w_mk, b_ig, b_fg,
           g_mhead, m_skip, *, batch, seq):
    L = MLSTM_CHUNK
    valid = min(seq, L)
    assert seq % valid == 0 and (valid == L or seq == valid)
    nc = seq // valid
    gate_bias = jnp.zeros((1, LANES), F32)
    gate_bias = gate_bias.at[0, GATE_COL_I:GATE_COL_I + MLSTM_HEADS].set(b_ig)
    gate_bias = gate_bias.at[0, GATE_COL_F:GATE_COL_F + MLSTM_HEADS].set(b_fg)
    bb = MLSTM_SEQS_PER_STEP
    assert batch % bb == 0
    tok = lambda b, c: (b, c, 0)
    const2 = lambda b, c: (0, 0)
    const3 = lambda b, c: (0, 0, 0)
    per_b3 = lambda b, c: (b, 0, 0)
    per_b4 = lambda b, c: (b, 0, 0, 0)
    H, DH, W = MLSTM_HEADS, MLSTM_DH, MLSTM_WIDTH
    rows3 = lambda a: a.reshape(batch, seq, a.shape[-1])
    o_m, conv_new, c_new, n_new, m_new = pl.pallas_call(
        functools.partial(_mlstm_body, valid=valid, bb=bb),
        grid=(batch // bb, nc),
        in_specs=[pl.BlockSpec((bb, valid, W), tok), pl.BlockSpec((bb, valid, W), tok),
                  pl.BlockSpec((bb, valid, W), tok), pl.BlockSpec((bb, valid, LANES), tok),
                  pl.BlockSpec((bb, MLSTM_CONV - 1, W), per_b3),
                  pl.BlockSpec((bb, H, DH, DH), per_b4),
                  pl.BlockSpec((bb, H, DH), per_b3),
                  pl.BlockSpec((bb, 1, H), per_b3),
                  pl.BlockSpec((MLSTM_CONV, W), const2), pl.BlockSpec((1, W), const2),
                  pl.BlockSpec((H, DH, DH), const3), pl.BlockSpec((H, DH, DH), const3),
                  pl.BlockSpec((1, LANES), const2), pl.BlockSpec((1, W), const2),
                  pl.BlockSpec((1, W), const2)],
        out_specs=[pl.BlockSpec((bb, valid, W), tok),
                   pl.BlockSpec((bb, MLSTM_CONV - 1, W), per_b3),
                   pl.BlockSpec((bb, H, DH, DH), per_b4),
                   pl.BlockSpec((bb, H, DH), per_b3),
                   pl.BlockSpec((bb, 1, H), per_b3)],
        out_shape=[jax.ShapeDtypeStruct((batch, seq, W), F32),
                   jax.ShapeDtypeStruct((batch, MLSTM_CONV - 1, W), F32),
                   jax.ShapeDtypeStruct((batch, H, DH, DH), F32),
                   jax.ShapeDtypeStruct((batch, H, DH), F32),
                   jax.ShapeDtypeStruct((batch, 1, H), F32)],
        scratch_shapes=[pltpu.VMEM((bb, SUBLANES + L, W), F32), pltpu.VMEM((bb, L, W), F32),
                        pltpu.VMEM((bb, L, LANES), F32)],
        compiler_params=_cparams(("arbitrary", "arbitrary")),
        name="mlstm",
    )(rows3(mu), rows3(mv), rows3(mo), rows3(gates), conv_buf, c0, n0, m0.reshape(batch, 1, H),
      w_mconv, b_mconv.reshape(1, W), w_mq.astype(BF16), w_mk.astype(BF16), gate_bias,
      g_mhead.reshape(1, W), m_skip.reshape(1, W))
    return o_m.reshape(batch * seq, W), conv_new, c_new, n_new, m_new


def _compress_rows(xk_ref, xv_ref, pe_ref, w1_ref, w2_ref, n_pairs):
    pair_rows = 2 * CMP_BLOCK
    outs = []
    for kv, x_ref in enumerate((xk_ref, xv_ref)):
        acc = jnp.zeros((2 * n_pairs, NSA_KV_HEADS * CMP_HIDDEN), F32)
        for r in range(CMP_BLOCK):
            ev = x_ref[pl.ds(r, n_pairs, stride=pair_rows), :]
            od = x_ref[pl.ds(CMP_BLOCK + r, n_pairs, stride=pair_rows), :]
            xr = jnp.concatenate([ev, od], axis=0) + pe_ref[kv, r:r + 1, :]
            w1_r = w1_ref[kv, r // 2, (r % 2) * KV_WIDTH:(r % 2 + 1) * KV_WIDTH, :]
            acc = acc + _dot(xr.astype(BF16), w1_r)
        outs.append(_dot(_silu(acc).astype(BF16), w2_ref[kv]))
    return jnp.concatenate(outs, axis=1)


def _compress_body(xk_ref, xv_ref, pe_ref, w1_ref, w2_ref, oe_ref, oo_ref, *, n_pairs):
    out = _compress_rows(xk_ref, xv_ref, pe_ref, w1_ref, w2_ref, n_pairs)
    oe_ref[0] = out[0:n_pairs, :]
    oo_ref[0] = out[n_pairs:, :]


BLOCKS_PER_PAGE = PAGE_SIZE // CMP_BLOCK


def _gather_pages(pt_ref, pool_hbm, pages_ref, sem_ref, n_pages):
    g = pl.program_id(0) * pl.num_programs(1) + pl.program_id(1)
    n_total = pl.num_programs(0) * pl.num_programs(1)

    def copies(step, slot):
        return [pltpu.make_async_copy(pool_hbm.at[pt_ref[step * n_pages + j]], pages_ref.at[slot, j],
                                      sem_ref.at[slot]) for j in range(n_pages)]

    @pl.when(g == 0)
    def _():
        for cp in copies(0, 0):
            cp.start()

    @pl.when(g + 1 < n_total)
    def _():
        for cp in copies(g + 1, (g + 1) % 2):
            cp.start()

    slot = g % 2
    for cp in copies(g, slot):
        cp.wait()
    return slot


def _compress_paged_body(pt_ref, pool_hbm, pet_ref, perm_ref, w1_ref, w2_ref, oe_ref, oo_ref,
                         buf_ref, os_ref, pages_ref, sem_ref, *, n_pages):
    slot = _gather_pages(pt_ref, pool_hbm, pages_ref, sem_ref, n_pages)
    grp = 2 * BLOCKS_PER_PAGE
    for jp in range(n_pages // 2):
        xt = jnp.concatenate([pages_ref[slot, 2 * jp], pages_ref[slot, 2 * jp + 1]], axis=1)
        xb = (xt + pet_ref[...]).astype(BF16)
        xp = _dot_nt(perm_ref[...], xb)
        for r in range(CMP_BLOCK):
            for kv in range(2):
                lane0 = (2 * kv + r % 2) * KV_WIDTH
                buf_ref[r // 2, grp * jp:grp * (jp + 1), lane0:lane0 + KV_WIDTH] = (
                    xp[grp * r:grp * (r + 1), kv * KV_WIDTH:(kv + 1) * KV_WIDTH])
    for kv in range(2):
        lanes = slice(2 * kv * KV_WIDTH, 2 * (kv + 1) * KV_WIDTH)
        acc = _dot(buf_ref[0, :, lanes].astype(BF16), w1_ref[kv, 0])
        for r2 in range(1, CMP_BLOCK // 2):
            acc = acc + _dot(buf_ref[r2, :, lanes].astype(BF16), w1_ref[kv, r2])
        os_ref[kv] = _dot(_silu(acc).astype(BF16), w2_ref[kv])
    half = os_ref.shape[1] // 2
    for parity, ref in enumerate((oe_ref, oo_ref)):
        ref[0] = jnp.concatenate([os_ref[kv, pl.ds(parity, half, stride=2), :] for kv in range(2)],
                                 axis=1)


def _page_pair_constants(pe):
    pe_t = jnp.broadcast_to(pe.transpose(0, 2, 1)[:, None, :, None, :],
                            (2, NSA_KV_HEADS, HEAD_DIM, 2 * BLOCKS_PER_PAGE, CMP_BLOCK))
    pe_t = pe_t.reshape(2 * KV_WIDTH, 2 * PAGE_SIZE)
    grp = 2 * BLOCKS_PER_PAGE
    perm = np.zeros((2 * PAGE_SIZE, 2 * PAGE_SIZE), np.float32)
    for r in range(CMP_BLOCK):
        for b in range(grp):
            perm[r * grp + b, b * CMP_BLOCK + r] = 1.0
    return pe_t, jnp.asarray(perm, BF16)


def _pack_compress_weights(pe, w1, w2):
    eye_h = jnp.eye(NSA_KV_HEADS, dtype=F32)
    pe_r = jnp.broadcast_to(pe[:, :, None, :], (2, CMP_BLOCK, NSA_KV_HEADS, HEAD_DIM))
    pe_r = pe_r.reshape(2, CMP_BLOCK, KV_WIDTH)
    w1r = w1.reshape(2, CMP_BLOCK, HEAD_DIM, CMP_HIDDEN)
    w1_big = jnp.einsum('krdc,hH->krhdHc', w1r, eye_h)
    w1_big = w1_big.reshape(2, CMP_BLOCK // 2, 2 * KV_WIDTH, NSA_KV_HEADS * CMP_HIDDEN).astype(BF16)
    w2_big = jnp.einsum('kcd,hH->khcHd', w2, eye_h)
    w2_big = w2_big.reshape(2, NSA_KV_HEADS * CMP_HIDDEN, KV_WIDTH).astype(BF16)
    return pe_r, w1_big, w2_big


def _compress_prompt(k_rows, v_rows, cw, *, batch, seq):
    pe_r, w1_big, w2_big = cw
    n_pairs = seq // (2 * CMP_BLOCK)
    const3 = lambda b: (0, 0, 0)
    out_sd = jax.ShapeDtypeStruct((batch, n_pairs, 2 * KV_WIDTH), F32)
    return pl.pallas_call(
        functools.partial(_compress_body, n_pairs=n_pairs),
        grid=(batch,),
        in_specs=[pl.BlockSpec((seq, KV_WIDTH), lambda b: (b, 0)),
                  pl.BlockSpec((seq, KV_WIDTH), lambda b: (b, 0)),
                  pl.BlockSpec(pe_r.shape, const3),
                  pl.BlockSpec(w1_big.shape, lambda b: (0, 0, 0, 0)),
                  pl.BlockSpec(w2_big.shape, const3)],
        out_specs=[pl.BlockSpec((1, n_pairs, 2 * KV_WIDTH), lambda b: (b, 0, 0))] * 2,
        out_shape=[out_sd, out_sd],
        compiler_params=_cparams(("arbitrary",)),
        name="compress_prompt",
    )(k_rows, v_rows, pe_r, w1_big, w2_big)


COMPRESS_PAGES_PER_STEP = 64


def _compress_paged(pool, page_table, cw, cw_pages):
    _, w1_big, w2_big = cw
    pe_t, perm = cw_pages
    batch, n_pages = page_table.shape
    pps = COMPRESS_PAGES_PER_STEP
    assert n_pages % pps == 0 and pps % 2 == 0
    n_steps = n_pages // pps
    n_blk = pps * BLOCKS_PER_PAGE
    const3 = lambda b, c, pt: (0, 0, 0)
    return pl.pallas_call(
        functools.partial(_compress_paged_body, n_pages=pps),
        grid_spec=pltpu.PrefetchScalarGridSpec(
            num_scalar_prefetch=1,
            grid=(batch, n_steps),
            in_specs=[pl.BlockSpec(memory_space=pl.ANY),
                      pl.BlockSpec(pe_t.shape, lambda b, c, pt: (0, 0)),
                      pl.BlockSpec(perm.shape, lambda b, c, pt: (0, 0)),
                      pl.BlockSpec(w1_big.shape, lambda b, c, pt: (0, 0, 0, 0)),
                      pl.BlockSpec(w2_big.shape, const3)],
            out_specs=[pl.BlockSpec((1, n_blk // 2, 2 * KV_WIDTH), lambda b, c, pt: (b, c, 0))] * 2,
            scratch_shapes=[pltpu.VMEM((CMP_BLOCK // 2, n_blk, 4 * KV_WIDTH), F32),
                            pltpu.VMEM((2, n_blk, KV_WIDTH), F32),
                            pltpu.VMEM((2, pps, 2 * KV_WIDTH, PAGE_SIZE), F32),
                            pltpu.SemaphoreType.DMA((2,))]),
        out_shape=[jax.ShapeDtypeStruct((batch, n_steps * n_blk // 2, 2 * KV_WIDTH), F32)] * 2,
        compiler_params=_cparams(("arbitrary", "arbitrary")),
        name="compress_paged",
    )(page_table.reshape(-1), pool, pe_t, perm, w1_big, w2_big)


def _cmp_attn_body(q_ref, ke_ref, ko_ref, o_ref, st_ref, *, tq, pos0):
    ns = ke_ref.shape[1]
    i = pl.program_id(1)
    rows = NSA_GROUP * tq
    tok0 = pos0 + i * tq
    pos_r = tok0 + lax.broadcasted_iota(jnp.int32, (1, rows), 1) % tq
    pair_c = lax.broadcasted_iota(jnp.int32, (ns, 1), 0)
    end_e = (2 * pair_c + 1) * CMP_BLOCK - 1
    end_o = (2 * pair_c + 2) * CMP_BLOCK - 1
    any_r = (CMP_BLOCK - 1 <= pos_r).astype(F32)
    contract_blocks = (((0,), (0,)), ((), ()))
    q = q_ref[...] * ATTN_SCALE
    for kh in range(NSA_KV_HEADS):
        qs = jnp.concatenate([q[:, (kh * NSA_GROUP + g) * HEAD_DIM:(kh * NSA_GROUP + g + 1) * HEAD_DIM]
                              for g in range(NSA_GROUP)], axis=0).astype(BF16)
        ks, vs = slice(kh * HEAD_DIM, (kh + 1) * HEAD_DIM), slice(KV_WIDTH + kh * HEAD_DIM,
                                                                   KV_WIDTH + (kh + 1) * HEAD_DIM)
        ke, ko = ke_ref[0, :, ks].astype(BF16), ko_ref[0, :, ks].astype(BF16)
        te = jnp.where(end_e <= pos_r, _dot_nt(ke, qs), NEG_INF)
        to = jnp.where(end_o <= pos_r, _dot_nt(ko, qs), NEG_INF)
        mt = jnp.maximum(jnp.max(te, axis=0, keepdims=True), jnp.max(to, axis=0, keepdims=True))
        pte, pto = jnp.exp(te - mt), jnp.exp(to - mt)
        invt = any_r / (jnp.sum(pte, axis=0, keepdims=True) + jnp.sum(pto, axis=0, keepdims=True))
        pte, pto = pte * invt, pto * invt
        oh = (lax.dot_general(pte.astype(BF16), ke_ref[0, :, vs].astype(BF16), contract_blocks,
                              preferred_element_type=F32)
              + lax.dot_general(pto.astype(BF16), ko_ref[0, :, vs].astype(BF16), contract_blocks,
                                preferred_element_type=F32))
        for g in range(NSA_GROUP):
            hd = kh * NSA_GROUP + g
            o_ref[:, hd * HEAD_DIM:(hd + 1) * HEAD_DIM] = oh[g * tq:(g + 1) * tq, :]
        ps = pte + pto
        score = ps[:, 0:tq]
        for g in range(1, NSA_GROUP):
            score = score + ps[:, g * tq:(g + 1) * tq]
        st_ref[0, kh] = score


def _cmp_attn(q2d, kce, kco, *, batch, seq, tq, pos0):
    ns = kce.shape[1]
    nq = seq // tq
    return pl.pallas_call(
        functools.partial(_cmp_attn_body, tq=tq, pos0=pos0),
        grid=(batch, nq),
        in_specs=[pl.BlockSpec((tq, NSA_WIDTH), lambda b, i: (b * nq + i, 0)),
                  pl.BlockSpec((1, ns, 2 * KV_WIDTH), lambda b, i: (b, 0, 0)),
                  pl.BlockSpec((1, ns, 2 * KV_WIDTH), lambda b, i: (b, 0, 0))],
        out_specs=[pl.BlockSpec((tq, NSA_WIDTH), lambda b, i: (b * nq + i, 0)),
                   pl.BlockSpec((1, NSA_KV_HEADS, ns, tq), lambda b, i: (b, 0, 0, i))],
        out_shape=[jax.ShapeDtypeStruct((batch * seq, NSA_WIDTH), F32),
                   jax.ShapeDtypeStruct((batch, NSA_KV_HEADS, ns, seq), F32)],
        compiler_params=_cparams(("arbitrary", "arbitrary")),
        name="cmp_attn",
    )(q2d, kce, kco)


def _topk_body(pos_ref, st_ref, b_ref, *, n_sel):
    score = st_ref[0]
    ns, tt = score.shape
    nsw = b_ref.shape[1]
    if nsw > ns:
        score = jnp.concatenate([score, jnp.zeros((nsw - ns, tt), F32)], axis=0)
    blk = lax.broadcasted_iota(jnp.int32, (nsw, 1), 0)
    blk_f = blk.astype(F32)
    cur = pos_ref[...] // SEL_BLOCK
    forced = (blk == 0) | (blk == cur) | (blk == cur - 1)
    pri = jnp.where(blk <= cur, jnp.where(forced, SEL_PRIORITY, score), -SEL_PRIORITY)
    pri = jnp.where(blk < n_sel, pri, -jnp.inf)
    bias = jnp.full((nsw, tt), NEG_INF, F32)
    for _ in range(min(TOP_N, n_sel)):
        top = jnp.max(pri, axis=0, keepdims=True)
        first = jnp.min(jnp.where(pri == top, blk_f, float(nsw)), axis=0, keepdims=True)
        hit = blk_f == first
        bias = jnp.where(hit, 0.0, bias)
        pri = jnp.where(hit, -jnp.inf, pri)
    b_ref[0] = bias


def _topk_blocks(scores_t, pos, *, n_sel, nsw, tt):
    groups, ns, tokens = scores_t.shape
    assert nsw >= max(ns, n_sel) and tokens % tt == 0
    return pl.pallas_call(
        functools.partial(_topk_body, n_sel=n_sel),
        grid=(groups, tokens // tt),
        in_specs=[pl.BlockSpec((1, tt), lambda g, i: (0, i)),
                  pl.BlockSpec((1, ns, tt), lambda g, i: (g, 0, i))],
        out_specs=pl.BlockSpec((1, nsw, tt), lambda g, i: (g, 0, i)),
        out_shape=jax.ShapeDtypeStruct((groups, nsw, tokens), F32),
        compiler_params=_cparams(("arbitrary", "arbitrary")),
        name="topk_blocks",
    )(pos, scores_t)


def _softmax_update(sc, vt_bf16, m_ref, l_ref, acc_ref):
    m_old = m_ref[...]
    m_new = jnp.maximum(m_old, jnp.max(sc, axis=1, keepdims=True))
    alpha = jnp.exp(m_old - m_new)
    pr = jnp.exp(sc - jnp.concatenate([m_new] * (sc.shape[1] // LANES), axis=1))
    l_ref[...] = alpha * l_ref[...] + jnp.sum(pr, axis=1, keepdims=True)
    acc_ref[...] = alpha * acc_ref[...] + _dot_nt(pr.astype(BF16), vt_bf16)
    m_ref[...] = m_new


def _softmax_init(m_ref, l_ref, acc_ref):
    m_ref[...] = jnp.full(m_ref.shape, NEG_INF, F32)
    l_ref[...] = jnp.zeros(l_ref.shape, F32)
    acc_ref[...] = jnp.zeros(acc_ref.shape, F32)


ATTN_TAB_COLS = 5


def _attn_pairs(seq, tq, tk):
    rows = []
    for i in range(seq // tq):
        t_lo, t_hi = i * tq, i * tq + tq - 1
        js = list(range(0, t_hi // tk + 1))
        for n, j in enumerate(js):
            rows.append((i, j, int(n == 0), int(n == len(js) - 1), int(j * tk + tk - 1 > t_lo)))
    return np.asarray(rows, np.int32)


def _attn_body(tab_ref, q_ref, k_ref, vt_ref, oh_ref, sb_ref, o_ref, qa_ref, m_ref, l_ref, acc_ref,
               *, tq, tk):
    p = pl.program_id(1)
    i, j, first, last, partial_tile = [tab_ref[ATTN_TAB_COLS * p + n] for n in range(ATTN_TAB_COLS)]
    cols = NSA_HEADS * tq

    @pl.when(first == 1)
    def _():
        for hd, piece in enumerate(_query_columns(q_ref[0], sb_ref[0], tq)):
            qa_ref[:, hd * tq:(hd + 1) * tq] = piece
        m_ref[...] = jnp.full(m_ref.shape, NEG_INF, F32)
        l_ref[...] = jnp.zeros(l_ref.shape, F32)
        acc_ref[...] = jnp.zeros(acc_ref.shape, F32)

    k_aug = jnp.concatenate([k_ref[...], oh_ref[...]], axis=1)
    sc = _dot(k_aug, qa_ref[...])
    vt = vt_ref[0]

    def update(sc):
        m_old = m_ref[...]
        m_new = jnp.maximum(m_old, jnp.max(sc, axis=0, keepdims=True))
        alpha = jnp.exp2(m_old - m_new)
        pr = jnp.exp2(sc - m_new)
        l_ref[...] = alpha * l_ref[...] + jnp.sum(pr, axis=0, keepdims=True)
        acc_ref[...] = alpha * acc_ref[...] + _dot(vt, pr.astype(BF16))
        m_ref[...] = m_new

    @pl.when(partial_tile == 1)
    def _():
        qpos = i * tq + (lax.broadcasted_iota(jnp.int32, (1, cols), 1) & (tq - 1))
        kpos = j * tk + lax.broadcasted_iota(jnp.int32, (tk, 1), 0)
        update(jnp.where(kpos <= qpos, sc, NEG_INF))

    @pl.when(partial_tile == 0)
    def _():
        update(sc)

    @pl.when(last == 1)
    def _():
        _store_heads(acc_ref[...] / l_ref[...], o_ref, tq)


def _query_columns(q, sel_bias, tq):
    q = q * (ATTN_SCALE * LOG2_E)
    zeros64 = jnp.zeros((HEAD_DIM, tq), F32)
    kv_head_rows = lambda x, kh: jnp.concatenate([x, zeros64] if kh == 0 else [zeros64, x], axis=0)
    pieces = []
    for m in range(NSA_HEADS // 2):
        q_t = q[:, m * LANES:(m + 1) * LANES].T
        for hd in (2 * m, 2 * m + 1):
            kh = hd // NSA_GROUP
            piece = kv_head_rows(q_t[(hd % 2) * HEAD_DIM:(hd % 2 + 1) * HEAD_DIM, :], kh)
            if sel_bias is not None:
                piece = jnp.concatenate([piece, kv_head_rows(sel_bias[kh], kh)], axis=0)
            pieces.append(piece.astype(BF16))
    return pieces


def _store_heads(o_t, o_ref, tq):
    for m in range(NSA_HEADS // 2):
        pair = jnp.concatenate(
            [o_t[(hd // NSA_GROUP) * HEAD_DIM:(hd // NSA_GROUP + 1) * HEAD_DIM, hd * tq:(hd + 1) * tq]
             for hd in (2 * m, 2 * m + 1)], axis=0)
        o_ref[0, :, m * LANES:(m + 1) * LANES] = pair.T


def _window_body(q_ref, *refs, tq, n_tiles):
    k_refs, v_refs, o_ref = refs[:n_tiles], refs[n_tiles:2 * n_tiles], refs[2 * n_tiles]
    i = pl.program_id(1)
    cols = NSA_HEADS * tq
    qa = jnp.concatenate(_query_columns(q_ref[0], None, tq), axis=1)
    qpos = i * tq + (lax.broadcasted_iota(jnp.int32, (1, cols), 1) & (tq - 1))
    scs = []
    for n, k_ref in enumerate(k_refs):
        kpos = (i - (n_tiles - 1) + n) * tq + lax.broadcasted_iota(jnp.int32, (tq, 1), 0)
        if n == n_tiles - 1:
            valid = kpos <= qpos
        elif n == 0:
            valid = (kpos > qpos - WINDOW) & (kpos >= 0)
        else:
            valid = kpos >= 0
        scs.append(jnp.where(valid, _dot(k_ref[...], qa), NEG_INF))
    mx = scs[0].max(axis=0, keepdims=True)
    for sc in scs[1:]:
        mx = jnp.maximum(mx, sc.max(axis=0, keepdims=True))
    l_sum, acc = None, None
    for sc, v_ref in zip(scs, v_refs):
        pr = jnp.exp2(sc - mx)
        part_l, part_acc = jnp.sum(pr, axis=0, keepdims=True), _dot(v_ref[0], pr.astype(BF16))
        l_sum = part_l if l_sum is None else l_sum + part_l
        acc = part_acc if acc is None else acc + part_acc
    _store_heads(acc / l_sum, o_ref, tq)


def _attn_window_prompt(q3d, k_rows, v_t, *, tq):
    batch, seq, _ = q3d.shape
    assert WINDOW % tq == 0 and seq % tq == 0 and tq & (tq - 1) == 0
    n_tiles = WINDOW // tq + 1
    nq = seq // tq
    tile = lambda n: (lambda i: jnp.maximum(i - (n_tiles - 1) + n, 0))
    return pl.pallas_call(
        functools.partial(_window_body, tq=tq, n_tiles=n_tiles),
        grid=(batch, nq),
        in_specs=[pl.BlockSpec((1, tq, NSA_WIDTH), lambda b, i: (b, i, 0))]
        + [pl.BlockSpec((tq, KV_WIDTH), lambda b, i, t=tile(n): (b * nq + t(i), 0)) for n in range(n_tiles)]
        + [pl.BlockSpec((1, KV_WIDTH, tq), lambda b, i, t=tile(n): (b, 0, t(i))) for n in range(n_tiles)],
        out_specs=pl.BlockSpec((1, tq, NSA_WIDTH), lambda b, i: (b, i, 0)),
        out_shape=jax.ShapeDtypeStruct((batch, seq, NSA_WIDTH), F32),
        compiler_params=_cparams(("arbitrary", "arbitrary")),
        name="attn_win",
    )(q3d, *([k_rows] * n_tiles), *([v_t] * n_tiles))


def _block_onehot(seq):
    blk = np.arange(seq)[:, None] // SEL_BLOCK
    return jnp.asarray((np.arange(LANES)[None, :] % SEL_BLOCK) == blk, BF16)


def _attn_selected_prompt(q3d, k_rows, v_t, selb, *, tq, tk):
    batch, seq, _ = q3d.shape
    assert tq & (tq - 1) == 0 and tk % LANES == 0 and tq % LANES == 0 and selb.shape[2] == SEL_BLOCK
    tab = _attn_pairs(seq, tq, tk)
    cols = NSA_HEADS * tq
    C = ATTN_TAB_COLS
    nk = seq // tk
    return pl.pallas_call(
        functools.partial(_attn_body, tq=tq, tk=tk),
        grid_spec=pltpu.PrefetchScalarGridSpec(
            num_scalar_prefetch=1,
            grid=(batch, tab.shape[0]),
            in_specs=[pl.BlockSpec((1, tq, NSA_WIDTH), lambda b, p, t: (b, t[C * p], 0)),
                      pl.BlockSpec((tk, KV_WIDTH), lambda b, p, t: (b * nk + t[C * p + 1], 0)),
                      pl.BlockSpec((1, KV_WIDTH, tk), lambda b, p, t: (b, 0, t[C * p + 1])),
                      pl.BlockSpec((tk, LANES), lambda b, p, t: (t[C * p + 1], 0)),
                      pl.BlockSpec((1, NSA_KV_HEADS, SEL_BLOCK, tq),
                                   lambda b, p, t: (b, 0, 0, t[C * p]))],
            out_specs=pl.BlockSpec((1, tq, NSA_WIDTH), lambda b, p, t: (b, t[C * p], 0)),
            scratch_shapes=[pltpu.VMEM((2 * LANES, cols), BF16), pltpu.VMEM((1, cols), F32),
                            pltpu.VMEM((1, cols), F32), pltpu.VMEM((KV_WIDTH, cols), F32)]),
        out_shape=jax.ShapeDtypeStruct((batch, seq, NSA_WIDTH), F32),
        compiler_params=_cparams(("arbitrary", "arbitrary")),
        name="attn_sel",
    )(jnp.asarray(tab.reshape(-1)), q3d, k_rows, v_t, _block_onehot(seq), selb)


ATTN_PAGES_PER_STEP = 64
ATTN_PAGED_SPLIT = 2


def _attn_paged_body(pt_ref, qa_ref, bq_ref, bn_ref, kn_ref, oh_ref, pool_hbm, o_ref, m_ref, l_ref,
                     acc_ref, pages_ref, sem_ref, *, n_pages, n_new):
    slot = _gather_pages(pt_ref, pool_hbm, pages_ref, sem_ref, n_pages)
    page_refs = [pages_ref.at[slot, j] for j in range(n_pages)]
    c = pl.program_id(1)
    rows = qa_ref.shape[1]

    @pl.when(c == 0)
    def _():
        _softmax_init(m_ref, l_ref, acc_ref)

    n_split = m_ref.shape[0]
    per = n_pages // n_split
    keys = per * PAGE_SIZE
    qa = qa_ref[0]
    bias = bq_ref[0, 0]
    blocks = keys // SEL_BLOCK
    lane = lax.broadcasted_iota(jnp.int32, (1, LANES), 1)
    scs, vts = [], []
    for s in range(n_split):
        refs_s = page_refs[s * per:(s + 1) * per]
        bias_s = bias if s == 0 else pltpu.roll(bias, LANES - s * blocks, axis=1)
        lhs = jnp.concatenate([qa, jnp.where(lane < blocks, bias_s, 0.0)], axis=1).astype(BF16)
        kt = jnp.concatenate([r[0:KV_WIDTH, :] for r in refs_s], axis=1)
        rhs = jnp.concatenate([kt.astype(BF16), oh_ref[...]], axis=0)
        scs.append(_dot(lhs, rhs))
        vts.append(jnp.concatenate([r[KV_WIDTH:, :] for r in refs_s], axis=1).astype(BF16))
    for s in range(n_split):
        _softmax_update(scs[s], vts[s], m_ref.at[s], l_ref.at[s], acc_ref.at[s])

    @pl.when(c == pl.num_programs(1) - 1)
    def _():
        kn = kn_ref[0]
        sc = _dot(qa.astype(BF16), kn[0:KV_WIDTH, :].astype(BF16)) + bn_ref[0]
        tq = lax.broadcasted_iota(jnp.int32, (rows, 1), 0) % n_new
        kk = lax.broadcasted_iota(jnp.int32, (1, kn.shape[1]), 1)
        sc = jnp.where((kk <= tq) & (kk < n_new), sc, NEG_INF)
        _softmax_update(sc, kn[KV_WIDTH:, :].astype(BF16), m_ref.at[0], l_ref.at[0], acc_ref.at[0])
        m_all = m_ref[0]
        for s in range(1, n_split):
            m_all = jnp.maximum(m_all, m_ref[s])
        l_all = jnp.zeros(m_all.shape, F32)
        acc_all = jnp.zeros(m_all.shape, F32)
        for s in range(n_split):
            scale = jnp.exp(m_ref[s] - m_all)
            l_all = l_all + scale * l_ref[s]
            acc_all = acc_all + scale * acc_ref[s]
        o_ref[0] = acc_all / l_all


def _attn_paged(qa, bias_q, bias_new, kv_new_t, pool, page_table, *, n_new):
    batch, n_pages = page_table.shape
    pps = ATTN_PAGES_PER_STEP
    keys_per_chain = pps // ATTN_PAGED_SPLIT * PAGE_SIZE
    assert n_pages % pps == 0 and pps * PAGE_SIZE // SEL_BLOCK <= LANES
    assert keys_per_chain // SEL_BLOCK <= SEL_BLOCK
    n_steps = n_pages // pps
    rows = qa.shape[1]

    per_b = lambda b, c, pt: (b, 0, 0)
    return pl.pallas_call(
        functools.partial(_attn_paged_body, n_pages=pps, n_new=n_new),
        grid_spec=pltpu.PrefetchScalarGridSpec(
            num_scalar_prefetch=1,
            grid=(batch, n_steps),
            in_specs=[pl.BlockSpec((1, rows, LANES), per_b),
                      pl.BlockSpec((1, 1, rows, LANES), lambda b, c, pt: (b, c, 0, 0)),
                      pl.BlockSpec((1, rows, LANES), per_b),
                      pl.BlockSpec((1,) + kv_new_t.shape[1:], per_b),
                      pl.BlockSpec((LANES, keys_per_chain), lambda b, c, pt: (0, 0)),
                      pl.BlockSpec(memory_space=pl.ANY)],
            out_specs=pl.BlockSpec((1, rows, LANES), per_b),
            scratch_shapes=[pltpu.VMEM((ATTN_PAGED_SPLIT, rows, LANES), F32)] * 3
            + [pltpu.VMEM((2, pps, 2 * KV_WIDTH, PAGE_SIZE), F32), pltpu.SemaphoreType.DMA((2,))]),
        out_shape=jax.ShapeDtypeStruct((batch, rows, LANES), F32),
        compiler_params=_cparams(("arbitrary", "arbitrary")),
        name="attn_sel_paged",
    )(page_table.reshape(-1), qa, bias_q, bias_new, kv_new_t, _block_onehot(keys_per_chain).T, pool)


def _attn_window_body(qa_ref, wb_ref, kn_ref, o_ref, *, n_new, past):
    qa = qa_ref[0].astype(BF16)
    wb, kn = wb_ref[0], kn_ref[0]
    rows, n_buf = qa.shape[0], wb.shape[1]
    qpos = past + lax.broadcasted_iota(jnp.int32, (rows, 1), 0) % n_new

    def masked(sc, kpos, extra):
        diff = qpos - kpos
        return jnp.where((diff >= 0) & (diff < WINDOW) & (kpos >= 0) & extra, sc, NEG_INF)

    nb = lax.broadcasted_iota(jnp.int32, (1, n_buf), 1)
    nn = lax.broadcasted_iota(jnp.int32, (1, kn.shape[1]), 1)
    sb = masked(_dot(qa, wb[0:KV_WIDTH, :].astype(BF16)), past - n_buf + nb, nb >= 0)
    sn = masked(_dot(qa, kn[0:KV_WIDTH, :].astype(BF16)), past + nn, nn < n_new)
    mx = jnp.maximum(jnp.max(sb, axis=1, keepdims=True), jnp.max(sn, axis=1, keepdims=True))
    pb, pn = jnp.exp(sb - mx), jnp.exp(sn - mx)
    o = (_dot_nt(pb.astype(BF16), wb[KV_WIDTH:, :].astype(BF16))
         + _dot_nt(pn.astype(BF16), kn[KV_WIDTH:, :].astype(BF16)))
    o_ref[0] = o / (jnp.sum(pb, axis=1, keepdims=True) + jnp.sum(pn, axis=1, keepdims=True))


def _attn_window_small(qa, win_t, kv_new_t, *, n_new, past):
    batch, rows, _ = qa.shape
    per_b = lambda b: (b, 0, 0)
    return pl.pallas_call(
        functools.partial(_attn_window_body, n_new=n_new, past=past),
        grid=(batch,),
        in_specs=[pl.BlockSpec((1, rows, LANES), per_b),
                  pl.BlockSpec((1,) + win_t.shape[1:], per_b),
                  pl.BlockSpec((1,) + kv_new_t.shape[1:], per_b)],
        out_specs=pl.BlockSpec((1, rows, LANES), per_b),
        out_shape=jax.ShapeDtypeStruct((batch, rows, LANES), F32),
        compiler_params=_cparams(("arbitrary",)),
        name="attn_win_small",
    )(qa, win_t, kv_new_t)


FFN_TM = 512


def _ffn_vmem_bytes(tm, halo):
    weights = 2 * (D_MODEL * D_MODEL + D_MODEL * 2 * D_FF + D_FF * D_MODEL)
    conv_buffer = 4 * (halo + tm) * 2 * D_FF
    row_tiles = 2 * 4 * tm * (2 * D_MODEL + 4 * NSA_WIDTH + LANES)
    temporaries = 3 * 4 * tm * max(hi - lo for lo, hi in FFN_CHUNKS)
    return weights + conv_buffer + row_tiles + temporaries
MXU_DEPTH = 256
FFN_CHUNKS = ((0, 6 * MXU_DEPTH), (6 * MXU_DEPTH, D_FF))


def _ffn_body(x_ref, om_ref, oc_ref, os_ref, ow_ref, gt_ref, ge_ref, gn_ref, gf_ref, gl_ref, wc_ref,
              fb_ref, wo_hbm, wu_hbm, wd_hbm, y_ref, fn_ref, xx_ref, wo_ref, wu_ref, wd_ref, sem_ref,
              *, tm, stride, halo):
    s = pl.program_id(1)

    @pl.when((pl.program_id(0) == 0) & (s == 0))
    def _():
        copies = [pltpu.make_async_copy(src, dst, sem_ref.at[n])
                  for n, (src, dst) in enumerate(((wo_hbm, wo_ref), (wu_hbm, wu_ref), (wd_hbm, wd_ref)))]
        for cp in copies:
            cp.start()
        for cp in copies:
            cp.wait()

    sig = _sigmoid(gt_ref[...])
    hi = sig.astype(BF16)
    lo = (sig - hi.astype(F32)).astype(BF16)
    comb = None
    for br, ob_ref in enumerate((oc_ref, os_ref, ow_ref)):
        gate = _dot(hi, ge_ref[br]) + _dot(lo, ge_ref[br])
        term = gate * ob_ref[...]
        comb = term if comb is None else comb + term
    onsa = _rms(comb, gn_ref[...])
    h = (x_ref[...] + _dot(om_ref[...].astype(BF16), wo_ref[0:MLSTM_WIDTH, :])
         + _dot(onsa.astype(BF16), wo_ref[MLSTM_WIDTH:, :]))
    hn = _rms(h, gf_ref[...]).astype(BF16)

    base = halo - (FFN_CONV - 1) * stride

    @pl.when(s == 0)
    def _():
        xx_ref[base:halo, :] = fb_ref[0]

    y_ref[...] = h
    for lo_col, hi_col in FFN_CHUNKS:
        convs = []
        for half in range(2):
            cols = slice(half * D_FF + lo_col, half * D_FF + hi_col)
            xx_ref[halo:halo + tm, cols] = _dot(hn, wu_ref[:, cols])
            conv = xx_ref[base:base + tm, cols] * wc_ref[0:1, cols]
            for j in range(1, FFN_CONV):
                conv = conv + xx_ref[base + j * stride:base + j * stride + tm, cols] * wc_ref[j:j + 1, cols]
            convs.append(conv)
        act = _silu(convs[1]) * convs[0]
        y_ref[...] += _dot(act.astype(BF16), wd_ref[lo_col:hi_col, :])
    fn_ref[0, 0] = xx_ref[tm + base:tm + halo, :]
    xx_ref[0:halo, :] = xx_ref[tm:tm + halo, :]
    y_ref[...] = _rms(y_ref[...], gl_ref[...])


def _gate_expand():
    ge = np.zeros((N_BRANCH, LANES, NSA_WIDTH), np.float32)
    for hd in range(NSA_HEADS):
        for br in range(N_BRANCH):
            ge[br, GATE_COL_NSA + hd * N_BRANCH + br, hd * HEAD_DIM:(hd + 1) * HEAD_DIM] = 1.0
    return jnp.asarray(ge, BF16)


def _ffn(x2d, om, oc, osel, ow, gt, fbuf, w_out, g_nsa, g_ffn, g_final, w_up, w_fconv, w_down,
         *, nb, tm, stride):
    rows = x2d.shape[0]
    ns = rows // (nb * tm)
    halo = -(-(FFN_CONV - 1) * stride // SUBLANES) * SUBLANES
    assert tm >= halo and all((hi - lo) % MXU_DEPTH == 0 for lo, hi in FFN_CHUNKS)
    vmem_limit = _ffn_vmem_bytes(tm, halo)
    assert vmem_limit <= VMEM_BYTES
    tok = lambda b, s: (b * ns + s, 0)
    nfb = (FFN_CONV - 1) * stride

    def const(shape):
        return pl.BlockSpec(shape, lambda b, s: (0,) * len(shape))

    hbm = pl.BlockSpec(memory_space=pl.ANY)
    y, fn = pl.pallas_call(
        functools.partial(_ffn_body, tm=tm, stride=stride, halo=halo),
        grid=(nb, ns),
        in_specs=[pl.BlockSpec((tm, D_MODEL), tok)] + [pl.BlockSpec((tm, NSA_WIDTH), tok)] * 4
        + [pl.BlockSpec((tm, LANES), tok),
           const((N_BRANCH, LANES, NSA_WIDTH)), const((1, NSA_WIDTH)), const((1, D_MODEL)),
           const((1, D_MODEL)), const((FFN_CONV, 2 * D_FF)),
           pl.BlockSpec((1, nfb, 2 * D_FF), lambda b, s: (b, 0, 0)), hbm, hbm, hbm],
        out_specs=[pl.BlockSpec((tm, D_MODEL), tok),
                   pl.BlockSpec((1, 1, nfb, 2 * D_FF), lambda b, s: (b, s, 0, 0))],
        out_shape=[jax.ShapeDtypeStruct((rows, D_MODEL), F32),
                   jax.ShapeDtypeStruct((nb, ns, nfb, 2 * D_FF), F32)],
        scratch_shapes=[pltpu.VMEM((halo + tm, 2 * D_FF), F32),
                        pltpu.VMEM((D_MODEL, D_MODEL), BF16), pltpu.VMEM((D_MODEL, 2 * D_FF), BF16),
                        pltpu.VMEM((D_FF, D_MODEL), BF16), pltpu.SemaphoreType.DMA((3,))],
        compiler_params=pltpu.CompilerParams(dimension_semantics=("arbitrary", "arbitrary"),
                                             vmem_limit_bytes=vmem_limit),
        name="outproj_ffn",
    )(x2d, om, oc, osel, ow, gt, _gate_expand(), g_nsa.reshape(1, -1), g_ffn.reshape(1, -1),
      g_final.reshape(1, -1), w_fconv, fbuf, w_out.astype(BF16), w_up.astype(BF16),
      w_down.astype(BF16))
    return y, fn[:, ns - 1]


PROMPT_TM = 512
PROMPT_TQ_CMP = 512
PROMPT_TT_TOPK = 1024
PROMPT_TQ_SEL = 512
PROMPT_TK_SEL = 512
PROMPT_TQ_WIN = 256


def _kv_rows(kv_t):
    batch, _, rows = kv_t.shape
    return kv_t.reshape(batch, 2, NSA_KV_HEADS, HEAD_DIM, rows).transpose(0, 4, 1, 2, 3)


def _kv_feature_major(kv5):
    batch, rows = kv5.shape[:2]
    return kv5.transpose(0, 2, 3, 4, 1).reshape(batch, 2 * KV_WIDTH, rows)


def _prompt_layer(x, wts):
    batch, seq, _ = x.shape
    x2d = x.reshape(batch * seq, D_MODEL)
    q, kc_rows, vc_rows, mu, mv, mo, gt, ks_rows, kw_rows, kvc_t, kvs_t, kvw_t, vs_t, vw_t = _in_proj(
        x2d, wts["g_mix"], wts["w_in_packed"], batch=batch, seq=seq, tm=min(PROMPT_TM, seq))
    H, DH, W = MLSTM_HEADS, MLSTM_DH, MLSTM_WIDTH
    o_m, mconv, c_new, n_new, m_new = _mlstm(
        mu, mv, mo, gt, jnp.zeros((batch, MLSTM_CONV - 1, W), F32), jnp.zeros((batch, H, DH, DH), F32),
        jnp.zeros((batch, H, DH), F32), jnp.zeros((batch, H), F32),
        wts["w_mconv"], wts["b_mconv"], wts["w_mq"], wts["w_mk"], wts["b_ig"], wts["b_fg"],
        wts["g_mhead"], wts["m_skip"], batch=batch, seq=seq)
    kce, kco = _compress_prompt(kc_rows, vc_rows, wts["cw"], batch=batch, seq=seq)
    n_sel = -(-seq // SEL_BLOCK)
    assert n_sel <= SEL_BLOCK
    o_cmp, scores_t = _cmp_attn(q, kce, kco, batch=batch, seq=seq, tq=min(PROMPT_TQ_CMP, seq), pos0=0)
    selb = _topk_blocks(scores_t.reshape(batch * NSA_KV_HEADS, -1, seq),
                        jnp.arange(seq, dtype=jnp.int32).reshape(1, seq),
                        n_sel=n_sel, nsw=SEL_BLOCK, tt=min(PROMPT_TT_TOPK, seq))
    selb = selb.reshape(batch, NSA_KV_HEADS, SEL_BLOCK, seq)
    q3d = q.reshape(batch, seq, NSA_WIDTH)
    o_sel = _attn_selected_prompt(q3d, ks_rows, vs_t, selb, tq=min(PROMPT_TQ_SEL, seq),
                                  tk=min(PROMPT_TK_SEL, seq))
    o_win = _attn_window_prompt(q3d, kw_rows, vw_t, tq=PROMPT_TQ_WIN)
    fbuf = jnp.zeros((batch, FFN_CONV - 1, 2 * D_FF), F32)
    y, f_new = _ffn(x2d, o_m, o_cmp, o_sel.reshape(-1, NSA_WIDTH), o_win.reshape(-1, NSA_WIDTH), gt,
                    fbuf, wts["w_out"], wts["g_nsa"], wts["g_ffn"], wts["g_final"], wts["w_up"],
                    wts["w_fconv"], wts["w_down"], nb=batch, tm=min(FFN_TM, seq), stride=1)
    n_win = min(WINDOW, seq)
    return (y.reshape(batch, seq, D_MODEL), _kv_rows(kvc_t), _kv_rows(kvs_t),
            _kv_rows(kvw_t[:, :, seq - n_win:]), mconv, c_new, n_new, m_new.reshape(batch, H), f_new)


def _decode_rows(q2d, batch, seq):
    q5 = (q2d * ATTN_SCALE).reshape(batch, seq, NSA_KV_HEADS, NSA_GROUP, HEAD_DIM).transpose(0, 2, 3, 1, 4)
    eye = jnp.eye(NSA_KV_HEADS, dtype=F32)
    qa = jnp.einsum('bkgtd,kK->bkgtKd', q5, eye)
    return qa.reshape(batch, NSA_KV_HEADS * NSA_GROUP * seq, KV_WIDTH)


def _decode_rows_out(o, batch, seq):
    o6 = o.reshape(batch, NSA_KV_HEADS, NSA_GROUP, seq, NSA_KV_HEADS, HEAD_DIM)
    o5 = jnp.stack([o6[:, kh, :, :, kh, :] for kh in range(NSA_KV_HEADS)], axis=1)
    return o5.transpose(0, 3, 1, 2, 4).reshape(batch * seq, NSA_WIDTH)


def _sample_layer(x, pool_cmp, pool_sel, win_buf, m_conv, m_c, m_n, m_m, f_buf, page_table, wts):
    batch, seq, _ = x.shape
    n_pages = page_table.shape[1]
    past = n_pages * PAGE_SIZE
    assert past % SEL_BLOCK == 0 and seq <= SEL_BLOCK and seq < CMP_BLOCK
    x2d = x.reshape(batch * seq, D_MODEL)
    q, _, _, mu, mv, mo, gt, _, _, kvc_t, kvs_t, kvw_t, _, _ = _in_proj(
        x2d, wts["g_mix"], wts["w_in_packed"], batch=1, seq=batch * seq, tm=batch * seq)
    per_batch = lambda a: a.reshape(2 * KV_WIDTH, batch, seq).transpose(1, 0, 2)
    kvc_t, kvs_t, kvw_t = per_batch(kvc_t), per_batch(kvs_t), per_batch(kvw_t)
    pad_keys = lambda a: jnp.pad(a, ((0, 0), (0, 0), (0, LANES - seq)))
    H = MLSTM_HEADS
    o_m, mconv, c_new, n_new, m_new = _mlstm(
        mu, mv, mo, gt, m_conv, m_c, m_n, m_m,
        wts["w_mconv"], wts["b_mconv"], wts["w_mq"], wts["w_mk"], wts["b_ig"], wts["b_fg"],
        wts["g_mhead"], wts["m_skip"], batch=batch, seq=seq)
    pool_cmp3, pool_sel3 = _kv_feature_major(pool_cmp), _kv_feature_major(pool_sel)
    kce, kco = _compress_paged(pool_cmp3, page_table, wts["cw"], wts["cw_pages"])
    n_past_blk = past // SEL_BLOCK
    n_sel = -(-(past + seq) // SEL_BLOCK)
    o_cmp, scores_t = _cmp_attn(q, kce, kco, batch=batch, seq=seq, tq=seq, pos0=past)
    ns = scores_t.shape[2]
    nsw = ns + LANES
    scores_all = scores_t.transpose(1, 2, 0, 3).reshape(NSA_KV_HEADS, ns, batch * seq)
    pos_all = (past + jnp.arange(batch * seq, dtype=jnp.int32) % seq).reshape(1, batch * seq)
    selb = _topk_blocks(scores_all, pos_all, n_sel=n_sel, nsw=nsw, tt=batch * seq)
    selb = selb.reshape(NSA_KV_HEADS, nsw, batch, seq).transpose(2, 0, 3, 1)
    qa = _decode_rows(q, batch, seq)
    rows = qa.shape[1]
    blk_per_step = ATTN_PAGES_PER_STEP * PAGE_SIZE // SEL_BLOCK
    n_steps = n_pages // ATTN_PAGES_PER_STEP
    sb_rows = jnp.broadcast_to(selb[:, :, None], (batch, NSA_KV_HEADS, NSA_GROUP, seq, selb.shape[-1]))
    sb_rows = sb_rows.reshape(batch, rows, selb.shape[-1])
    bias_q = sb_rows[:, :, :n_past_blk].reshape(batch, rows, n_steps, blk_per_step).transpose(0, 2, 1, 3)
    bias_q = jnp.pad(bias_q, ((0, 0), (0, 0), (0, 0), (0, LANES - blk_per_step)))
    bias_new = jnp.broadcast_to(sb_rows[:, :, n_past_blk:n_past_blk + 1], (batch, rows, LANES))
    o_sel = _attn_paged(qa, bias_q, bias_new, pad_keys(kvs_t), pool_sel3, page_table, n_new=seq)
    n_buf = win_buf.shape[1]
    assert past >= n_buf
    win_t = _kv_feature_major(win_buf)
    o_win = _attn_window_small(qa, win_t, pad_keys(kvw_t), n_new=seq, past=past)
    win_new = jnp.concatenate([win_t, kvw_t], axis=2)[:, :, seq:]
    tmaj = lambda a: a.reshape(batch, seq, -1).transpose(1, 0, 2).reshape(batch * seq, -1)
    fb_t = f_buf.transpose(1, 0, 2).reshape(1, (FFN_CONV - 1) * batch, 2 * D_FF)
    y, f_new = _ffn(tmaj(x2d), tmaj(o_m), tmaj(o_cmp), tmaj(_decode_rows_out(o_sel, batch, seq)),
                    tmaj(_decode_rows_out(o_win, batch, seq)), tmaj(gt), fb_t,
                    wts["w_out"], wts["g_nsa"], wts["g_ffn"], wts["g_final"], wts["w_up"],
                    wts["w_fconv"], wts["w_down"], nb=1, tm=batch * seq, stride=batch)
    y = y.reshape(seq, batch, D_MODEL).transpose(1, 0, 2)
    f_new = f_new.reshape(FFN_CONV - 1, batch, 2 * D_FF).transpose(1, 0, 2)
    return (y, _kv_rows(kvc_t), _kv_rows(kvs_t), _kv_rows(win_new), mconv, c_new, n_new,
            m_new.reshape(batch, H), f_new)


def kernel(x_prompt, x_sample, cache_cmp, cache_sel, state_win, state_mlstm_C, state_mlstm_n,
           state_mlstm_m, state_mlstm_conv, state_ffn_conv, page_table,
           g_mix, w_in, w_out, w_mconv, b_mconv, w_mq, w_mk, b_ig, b_fg, g_mhead, m_skip,
           pe_cmp, w_cmp1, w_cmp2, g_nsa, g_ffn, w_up, w_fconv, w_down, g_final):
    assert w_in.shape[0] == 1, "one layer: the final norm is fused into the layer's FFN kernel"
    l = 0
    wts = dict(g_mix=g_mix[l], w_in_packed=_pack_w_in(w_in[l]), w_out=w_out[l], w_mconv=w_mconv[l],
               b_mconv=b_mconv[l], w_mq=w_mq[l], w_mk=w_mk[l], b_ig=b_ig[l], b_fg=b_fg[l],
               g_mhead=g_mhead[l], m_skip=m_skip[l],
               cw=_pack_compress_weights(pe_cmp[l], w_cmp1[l], w_cmp2[l]),
               cw_pages=_page_pair_constants(pe_cmp[l]),
               g_nsa=g_nsa[l], g_ffn=g_ffn[l], g_final=g_final, w_up=w_up[l], w_fconv=w_fconv[l],
               w_down=w_down[l])
    p = _prompt_layer(x_prompt, wts)
    s = _sample_layer(x_sample, cache_cmp[l], cache_sel[l], state_win[l], state_mlstm_conv[l],
                      state_mlstm_C[l], state_mlstm_n[l], state_mlstm_m[l], state_ffn_conv[l],
                      page_table, wts)
    yp, cmp_p, sel_p, win_p, mconv_p, c_p, n_p, m_p, fconv_p = p
    ys, cmp_s, sel_s, win_s, mconv_s, c_s, n_s, m_s, fconv_s = s
    st = lambda a: a[None]
    return (yp, ys, st(cmp_p), st(cmp_s), st(sel_p), st(sel_s), st(win_p), st(win_s),
            st(c_p), st(c_s), st(n_p), st(n_s), st(m_p), st(m_s), st(mconv_p), st(mconv_s),
            st(fconv_p), st(fconv_s))
```

```python
import functools

import numpy as np
import jax
import jax.numpy as jnp
from jax import lax
from jax.experimental import pallas as pl
from jax.experimental.pallas import tpu as pltpu

F32 = jnp.float32
BF16 = jnp.bfloat16

D_MODEL = 1024
PAGE_SIZE = 128
HEAD_DIM = 64
NSA_HEADS = 8
NSA_KV_HEADS = 2
NSA_GROUP = NSA_HEADS // NSA_KV_HEADS
NSA_WIDTH = NSA_HEADS * HEAD_DIM
KV_WIDTH = NSA_KV_HEADS * HEAD_DIM
CMP_BLOCK = 32
CMP_HIDDEN = 2 * HEAD_DIM
SEL_BLOCK = 64
TOP_N = 16
WINDOW = 512
N_BRANCH = 3
ATTN_SCALE = HEAD_DIM ** -0.5
MLSTM_HEADS = 4
MLSTM_WIDTH = D_MODEL - NSA_WIDTH
MLSTM_DH = MLSTM_WIDTH // MLSTM_HEADS
MLSTM_CONV = 4
D_FF = ((8 * D_MODEL // 3 + 127) // 128) * 128
FFN_CONV = 3
EPS = 1e-6
NEG_INF = -1e30
SEL_PRIORITY = 1e4
LOG2_E = 1.4426950408889634

LANES = 128
SUBLANES = 8
VMEM_BYTES = 64 * 1024 * 1024
VMEM_LIMIT = 3 * VMEM_BYTES // 4

GATE_COL_NSA = 0
GATE_COL_I = NSA_HEADS * N_BRANCH
GATE_COL_F = GATE_COL_I + MLSTM_HEADS

MLSTM_CHUNK = 128
MLSTM_SEQS_PER_STEP = 4


def _cparams(sem):
    return pltpu.CompilerParams(dimension_semantics=sem, vmem_limit_bytes=VMEM_LIMIT)


def _dot(a, b):
    return jnp.dot(a, b, preferred_element_type=F32)


def _dot_nt(a, b):
    return lax.dot_general(a, b, (((1,), (1,)), ((), ())), preferred_element_type=F32)


def _sigmoid(x):
    return 1.0 / (1.0 + jnp.exp(-x))


def _silu(x):
    return x * _sigmoid(x)


def _rms(x, g):
    return x * lax.rsqrt(jnp.mean(x * x, axis=-1, keepdims=True) + EPS) * g


IN_ROW_WIDTHS = (NSA_WIDTH, KV_WIDTH, KV_WIDTH, MLSTM_WIDTH, MLSTM_WIDTH, MLSTM_WIDTH, LANES,
                 KV_WIDTH, KV_WIDTH)
IN_ROW_DTYPES = (F32,) * 7 + (BF16,) * 2
N_KV_BRANCH = 3


def _inproj_body(x_ref, g_ref, w_ref, wt_ref, *out_refs):
    xb = _rms(x_ref[...], g_ref[...]).astype(BF16)
    off = 0
    n_rows = len(IN_ROW_WIDTHS)
    for ref in out_refs[:n_rows]:
        n = ref.shape[-1]
        ref[...] = _dot(xb, w_ref[:, off:off + n]).astype(ref.dtype)
        off += n
    kv_refs = out_refs[n_rows:n_rows + N_KV_BRANCH]
    vt_refs = out_refs[n_rows + N_KV_BRANCH:]
    for n, ref in enumerate(kv_refs):
        kv_t = _dot_nt(wt_ref[n * 2 * KV_WIDTH:(n + 1) * 2 * KV_WIDTH, :], xb)
        ref[0] = kv_t
        if n > 0:
            vt_refs[n - 1][0] = kv_t[KV_WIDTH:, :].astype(BF16)


def _pack_w_in(w_in):
    splits = np.cumsum([NSA_WIDTH, 2 * KV_WIDTH, 2 * KV_WIDTH, 2 * KV_WIDTH, NSA_HEADS * N_BRANCH,
                        MLSTM_WIDTH, MLSTM_WIDTH, MLSTM_WIDTH, MLSTM_HEADS]).tolist()
    q, kvc, kvs, kvw, gt, mu, mv, mo, mi, mf = jnp.split(w_in, splits, axis=1)
    gates = jnp.concatenate([gt, mi, mf], axis=1)
    gates = jnp.pad(gates, ((0, 0), (0, LANES - gates.shape[1])))
    w_rows = jnp.concatenate([q, kvc, mu, mv, mo, gates, kvs[:, :KV_WIDTH], kvw[:, :KV_WIDTH]],
                             axis=1).astype(BF16)
    w_kv_t = jnp.concatenate([kvc, kvs, kvw], axis=1).T.astype(BF16)
    return w_rows, w_kv_t


def _in_proj(x2d, g_mix, w_packed, *, batch, seq, tm):
    w_rows, w_kv_t = w_packed
    t = x2d.shape[0]
    ns = seq // tm
    kv_sd = jax.ShapeDtypeStruct((batch, 2 * KV_WIDTH, seq), F32)
    vt_sd = jax.ShapeDtypeStruct((batch, KV_WIDTH, seq), BF16)
    feat_major = lambda rows: pl.BlockSpec((1, rows, tm), lambda i: (i // ns, 0, i % ns))
    return pl.pallas_call(
        _inproj_body,
        grid=(t // tm,),
        in_specs=[pl.BlockSpec((tm, D_MODEL), lambda i: (i, 0)),
                  pl.BlockSpec((1, D_MODEL), lambda i: (0, 0)),
                  pl.BlockSpec(w_rows.shape, lambda i: (0, 0)),
                  pl.BlockSpec(w_kv_t.shape, lambda i: (0, 0))],
        out_specs=[pl.BlockSpec((tm, n), lambda i: (i, 0)) for n in IN_ROW_WIDTHS]
        + [feat_major(2 * KV_WIDTH)] * N_KV_BRANCH + [feat_major(KV_WIDTH)] * (N_KV_BRANCH - 1),
        out_shape=[jax.ShapeDtypeStruct((t, n), dt) for n, dt in zip(IN_ROW_WIDTHS, IN_ROW_DTYPES)]
        + [kv_sd] * N_KV_BRANCH + [vt_sd] * (N_KV_BRANCH - 1),
        compiler_params=_cparams(("arbitrary",)),
        name="in_proj",
    )(x2d, g_mix.reshape(1, D_MODEL), w_rows, w_kv_t)


def _mlstm_body(*refs, valid, bb):
    cb_ref, c0_ref, n0_ref, m0_ref = refs[4:8]
    cn_ref, c_ref, n_ref, m_ref, xx_ref = refs[16:21]
    halo = SUBLANES

    @pl.when(pl.program_id(1) == 0)
    def _():
        xx_ref[:, 0:halo, :] = jnp.zeros((bb, halo, MLSTM_WIDTH), F32)
        xx_ref[:, halo - (MLSTM_CONV - 1):halo, :] = cb_ref[...]
        c_ref[...] = c0_ref[...]
        n_ref[...] = n0_ref[...]
        m_ref[...] = m0_ref[...]

    _mlstm_chunk(*refs, valid=valid, bb=bb)


def _mlstm_chunk(mu_ref, mv_ref, mo_ref, g_ref, cb_ref, c0_ref, n0_ref, m0_ref,
                 wc_ref, bc_ref, wq_ref, wk_ref, gb_ref, gh_ref, sk_ref,
                 o_ref, cn_ref, c_ref, n_ref, m_ref,
                 xx_ref, vpad_ref, gpad_ref, *, valid, bb):
    L = MLSTM_CHUNK
    DH = MLSTM_DH
    halo = SUBLANES
    units = [(bi, h) for bi in range(bb) for h in range(MLSTM_HEADS)]
    head_lanes = lambda h: slice(h * DH, (h + 1) * DH)
    row = lax.broadcasted_iota(jnp.int32, (L, L), 0)
    col = lax.broadcasted_iota(jnp.int32, (L, L), 1)
    tril = row >= col
    triu = row <= col
    tok_col = lax.broadcasted_iota(jnp.int32, (L, 1), 0)
    tok_row = lax.broadcasted_iota(jnp.int32, (1, L), 1)

    def log_sigmoid(x):
        return jnp.minimum(x, 0.0) - jnp.log(1.0 + jnp.exp(-jnp.abs(x)))

    uc, gb, gbt = {}, {}, {}
    for bi in range(bb):
        if valid < L:
            xx_ref[bi, halo:, :] = jnp.zeros((L, MLSTM_WIDTH), F32)
            vpad_ref[bi] = jnp.zeros((L, MLSTM_WIDTH), F32)
            gpad_ref[bi] = jnp.zeros((L, LANES), F32)
        xx_ref[bi, halo:halo + valid, :] = mu_ref[bi]
        vpad_ref[bi, 0:valid, :] = mv_ref[bi]
        gpad_ref[bi, 0:valid, :] = g_ref[bi]
        conv = xx_ref[bi, halo - 3:halo - 3 + L, :] * wc_ref[0:1, :]
        for j in range(1, MLSTM_CONV):
            conv = conv + xx_ref[bi, halo - 3 + j:halo - 3 + j + L, :] * wc_ref[j:j + 1, :]
        uc[bi] = _silu(conv + bc_ref[...])
        tail = xx_ref[bi, valid + halo - 3:valid + halo, :]
        xx_ref[bi, halo - 3:halo, :] = tail
        cn_ref[bi] = tail
        gb[bi] = gpad_ref[bi] + gb_ref[...]
        gbt[bi] = gb[bi].T

    q, k, qb, kb = {}, {}, {}, {}
    for u in units:
        bi, h = u
        ub = uc[bi][:, head_lanes(h)].astype(BF16)
        q[u] = _dot(ub, wq_ref[h])
        k[u] = _dot(ub, wk_ref[h]) * (DH ** -0.5)
        qb[u], kb[u] = q[u].astype(BF16), k[u].astype(BF16)

    ic_col, ic_row, cum_col, cum_row = {}, {}, {}, {}
    for u in units:
        bi, h = u
        ic_c = gb[bi][:, GATE_COL_I + h:GATE_COL_I + h + 1]
        ic_r = gbt[bi][GATE_COL_I + h:GATE_COL_I + h + 1, :]
        lf_c = log_sigmoid(gb[bi][:, GATE_COL_F + h:GATE_COL_F + h + 1])
        lf_r = log_sigmoid(gbt[bi][GATE_COL_F + h:GATE_COL_F + h + 1, :])
        if valid < L:
            ic_c = jnp.where(tok_col < valid, ic_c, NEG_INF)
            ic_r = jnp.where(tok_row < valid, ic_r, NEG_INF)
            lf_c = jnp.where(tok_col < valid, lf_c, 0.0)
            lf_r = jnp.where(tok_row < valid, lf_r, 0.0)
        ic_col[u], ic_row[u] = ic_c, ic_r
        cum_col[u] = jnp.sum(jnp.where(tril, lf_r, 0.0), axis=1, keepdims=True)
        cum_row[u] = jnp.sum(jnp.where(triu, lf_c, 0.0), axis=0, keepdims=True)

    m_t, w, sc = {}, {}, {}
    for u in units:
        bi, h = u
        m0 = m_ref[bi, 0:1, h:h + 1]
        dmat = jnp.where(tril, cum_col[u] - cum_row[u] + ic_row[u], NEG_INF)
        inter = cum_col[u] + m0
        m_t[u] = jnp.maximum(inter, jnp.max(dmat, axis=1, keepdims=True))
        w[u] = jnp.exp(dmat - m_t[u])
        sc[u] = jnp.exp(inter - m_t[u])

    hc = {}
    for u in units:
        bi, h = u
        s = _dot_nt(qb[u], kb[u]) * w[u]
        v = vpad_ref[bi, :, head_lanes(h)]
        c_old = c_ref[bi, h]
        n_old = n_ref[bi, h:h + 1, :]
        num = _dot(s.astype(BF16), v.astype(BF16)) + sc[u] * _dot_nt(qb[u], c_old.astype(BF16))
        den = (jnp.sum(s, axis=1, keepdims=True)
               + sc[u] * jnp.sum(q[u] * n_old, axis=1, keepdims=True))
        hc[u] = num / jnp.maximum(jnp.abs(den), jnp.exp(-m_t[u]))

    for u in units:
        bi, h = u
        m0 = m_ref[bi, 0:1, h:h + 1]
        m_new = m_t[u][L - 1:L, :]
        cum_last = cum_col[u][L - 1:L, :]
        wl = jnp.exp(cum_last - cum_col[u] + ic_col[u] - m_new)
        sl = jnp.exp(cum_last + m0 - m_new)
        v = vpad_ref[bi, :, head_lanes(h)]
        vw_t = (v * wl).T.astype(BF16)
        c_ref[bi, h] = sl * c_ref[bi, h] + _dot(vw_t, kb[u])
        n_ref[bi, h:h + 1, :] = sl * n_ref[bi, h:h + 1, :] + jnp.sum(wl * k[u], axis=0, keepdims=True)
        m_ref[bi, 0:1, h:h + 1] = m_new

    for u in units:
        bi, h = u
        hn = _rms(hc[u], gh_ref[:, head_lanes(h)])
        u_h = uc[bi][:, head_lanes(h)]
        out = ((hn[0:valid, :] + sk_ref[:, head_lanes(h)] * u_h[0:valid, :])
               * _sigmoid(mo_ref[bi, :, head_lanes(h)]))
        o_ref[bi, :, head_lanes(h)] = out


def _mlstm(mu, mv, mo, gates, conv_buf, c0, n0, m0, w_mconv, b_mconv, w_mq, w_mk, b_ig, b_fg,
           g_mhead, m_skip, *, batch, seq):
    L = MLSTM_CHUNK
    valid = min(seq, L)
    assert seq % valid == 0 and (valid == L or seq == valid)
    nc = seq // valid
    gate_bias = jnp.zeros((1, LANES), F32)
    gate_bias = gate_bias.at[0, GATE_COL_I:GATE_COL_I + MLSTM_HEADS].set(b_ig)
    gate_bias = gate_bias.at[0, GATE_COL_F:GATE_COL_F + MLSTM_HEADS].set(b_fg)
    bb = MLSTM_SEQS_PER_STEP
    assert batch % bb == 0
    tok = lambda b, c: (b, c, 0)
    const2 = lambda b, c: (0, 0)
    const3 = lambda b, c: (0, 0, 0)
    per_b3 = lambda b, c: (b, 0, 0)
    per_b4 = lambda b, c: (b, 0, 0, 0)
    H, DH, W = MLSTM_HEADS, MLSTM_DH, MLSTM_WIDTH
    rows3 = lambda a: a.reshape(batch, seq, a.shape[-1])
    o_m, conv_new, c_new, n_new, m_new = pl.pallas_call(
        functools.partial(_mlstm_body, valid=valid, bb=bb),
        grid=(batch // bb, nc),
        in_specs=[pl.BlockSpec((bb, valid, W), tok), pl.BlockSpec((bb, valid, W), tok),
                  pl.BlockSpec((bb, valid, W), tok), pl.BlockSpec((bb, valid, LANES), tok),
                  pl.BlockSpec((bb, MLSTM_CONV - 1, W), per_b3),
                  pl.BlockSpec((bb, H, DH, DH), per_b4),
                  pl.BlockSpec((bb, H, DH), per_b3),
                  pl.BlockSpec((bb, 1, H), per_b3),
                  pl.BlockSpec((MLSTM_CONV, W), const2), pl.BlockSpec((1, W), const2),
                  pl.BlockSpec((H, DH, DH), const3), pl.BlockSpec((H, DH, DH), const3),
                  pl.BlockSpec((1, LANES), const2), pl.BlockSpec((1, W), const2),
                  pl.BlockSpec((1, W), const2)],
        out_specs=[pl.BlockSpec((bb, valid, W), tok),
                   pl.BlockSpec((bb, MLSTM_CONV - 1, W), per_b3),
                   pl.BlockSpec((bb, H, DH, DH), per_b4),
                   pl.BlockSpec((bb, H, DH), per_b3),
                   pl.BlockSpec((bb, 1, H), per_b3)],
        out_shape=[jax.ShapeDtypeStruct((batch, seq, W), F32),
                   jax.ShapeDtypeStruct((batch, MLSTM_CONV - 1, W), F32),
                   jax.ShapeDtypeStruct((batch, H, DH, DH), F32),
                   jax.ShapeDtypeStruct((batch, H, DH), F32),
                   jax.ShapeDtypeStruct((batch, 1, H), F32)],
        scratch_shapes=[pltpu.VMEM((bb, SUBLANES + L, W), F32), pltpu.VMEM((bb, L, W), F32),
                        pltpu.VMEM((bb, L, LANES), F32)],
        compiler_params=_cparams(("arbitrary", "arbitrary")),
        name="mlstm",
    )(rows3(mu), rows3(mv), rows3(mo), rows3(gates), conv_buf, c0, n0, m0.reshape(batch, 1, H),
      w_mconv, b_mconv.reshape(1, W), w_mq.astype(BF16), w_mk.astype(BF16), gate_bias,
      g_mhead.reshape(1, W), m_skip.reshape(1, W))
    return o_m.reshape(batch * seq, W), conv_new, c_new, n_new, m_new


def _compress_rows(xk_ref, xv_ref, pe_ref, w1_ref, w2_ref, n_pairs):
    pair_rows = 2 * CMP_BLOCK
    accs = [None, None]
    for r in range(CMP_BLOCK):
        for kv, x_ref in enumerate((xk_ref, xv_ref)):
            ev = x_ref[pl.ds(r, n_pairs, stride=pair_rows), :]
            od = x_ref[pl.ds(CMP_BLOCK + r, n_pairs, stride=pair_rows), :]
            xr = jnp.concatenate([ev, od], axis=0) + pe_ref[kv, r:r + 1, :]
            w1_r = w1_ref[kv, r // 2, (r % 2) * KV_WIDTH:(r % 2 + 1) * KV_WIDTH, :]
            part = _dot(xr.astype(BF16), w1_r)
            accs[kv] = part if accs[kv] is None else accs[kv] + part
    return jnp.concatenate([_dot(_silu(accs[kv]).astype(BF16), w2_ref[kv]) for kv in range(2)], axis=1)


def _compress_body(xk_ref, xv_ref, pe_ref, w1_ref, w2_ref, oe_ref, oo_ref, *, n_pairs):
    out = _compress_rows(xk_ref, xv_ref, pe_ref, w1_ref, w2_ref, n_pairs)
    oe_ref[0] = out[0:n_pairs, :]
    oo_ref[0] = out[n_pairs:, :]


BLOCKS_PER_PAGE = PAGE_SIZE // CMP_BLOCK


def _gather_pages(pt_ref, pool_hbm, pages_ref, sem_ref, n_pages):
    g = pl.program_id(0) * pl.num_programs(1) + pl.program_id(1)
    n_total = pl.num_programs(0) * pl.num_programs(1)

    def copies(step, slot):
        return [pltpu.make_async_copy(pool_hbm.at[pt_ref[step * n_pages + j]], pages_ref.at[slot, j],
                                      sem_ref.at[slot]) for j in range(n_pages)]

    @pl.when(g == 0)
    def _():
        for cp in copies(0, 0):
            cp.start()

    @pl.when(g + 1 < n_total)
    def _():
        for cp in copies(g + 1, (g + 1) % 2):
            cp.start()

    slot = g % 2
    for cp in copies(g, slot):
        cp.wait()
    return slot


def _compress_paged_body(pt_ref, pool_hbm, pet_ref, perm_ref, w1_ref, w2_ref, oe_ref, oo_ref,
                         buf_ref, os_ref, pages_ref, sem_ref, *, n_pages):
    slot = _gather_pages(pt_ref, pool_hbm, pages_ref, sem_ref, n_pages)
    grp = 2 * BLOCKS_PER_PAGE
    for jp in range(n_pages // 2):
        xt = jnp.concatenate([pages_ref[slot, 2 * jp], pages_ref[slot, 2 * jp + 1]], axis=1)
        xb = (xt + pet_ref[...]).astype(BF16)
        xp = _dot_nt(perm_ref[...], xb)
        for r in range(CMP_BLOCK):
            for kv in range(2):
                lane0 = (2 * kv + r % 2) * KV_WIDTH
                buf_ref[r // 2, grp * jp:grp * (jp + 1), lane0:lane0 + KV_WIDTH] = (
                    xp[grp * r:grp * (r + 1), kv * KV_WIDTH:(kv + 1) * KV_WIDTH])
    accs = [None, None]
    for r2 in range(CMP_BLOCK // 2):
        for kv in range(2):
            lanes = slice(2 * kv * KV_WIDTH, 2 * (kv + 1) * KV_WIDTH)
            part = _dot(buf_ref[r2, :, lanes].astype(BF16), w1_ref[kv, r2])
            accs[kv] = part if accs[kv] is None else accs[kv] + part
    for kv in range(2):
        os_ref[kv] = _dot(_silu(accs[kv]).astype(BF16), w2_ref[kv])
    half = os_ref.shape[1] // 2
    for parity, ref in enumerate((oe_ref, oo_ref)):
        ref[0] = jnp.concatenate([os_ref[kv, pl.ds(parity, half, stride=2), :] for kv in range(2)],
                                 axis=1)


def _page_pair_constants(pe):
    pe_t = jnp.broadcast_to(pe.transpose(0, 2, 1)[:, None, :, None, :],
                            (2, NSA_KV_HEADS, HEAD_DIM, 2 * BLOCKS_PER_PAGE, CMP_BLOCK))
    pe_t = pe_t.reshape(2 * KV_WIDTH, 2 * PAGE_SIZE)
    grp = 2 * BLOCKS_PER_PAGE
    perm = np.zeros((2 * PAGE_SIZE, 2 * PAGE_SIZE), np.float32)
    for r in range(CMP_BLOCK):
        for b in range(grp):
            perm[r * grp + b, b * CMP_BLOCK + r] = 1.0
    return pe_t, jnp.asarray(perm, BF16)


def _pack_compress_weights(pe, w1, w2):
    eye_h = jnp.eye(NSA_KV_HEADS, dtype=F32)
    pe_r = jnp.broadcast_to(pe[:, :, None, :], (2, CMP_BLOCK, NSA_KV_HEADS, HEAD_DIM))
    pe_r = pe_r.reshape(2, CMP_BLOCK, KV_WIDTH)
    w1r = w1.reshape(2, CMP_BLOCK, HEAD_DIM, CMP_HIDDEN)
    w1_big = jnp.einsum('krdc,hH->krhdHc', w1r, eye_h)
    w1_big = w1_big.reshape(2, CMP_BLOCK // 2, 2 * KV_WIDTH, NSA_KV_HEADS * CMP_HIDDEN).astype(BF16)
    w2_big = jnp.einsum('kcd,hH->khcHd', w2, eye_h)
    w2_big = w2_big.reshape(2, NSA_KV_HEADS * CMP_HIDDEN, KV_WIDTH).astype(BF16)
    return pe_r, w1_big, w2_big


def _compress_prompt(k_rows, v_rows, cw, *, batch, seq):
    pe_r, w1_big, w2_big = cw
    n_pairs = seq // (2 * CMP_BLOCK)
    const3 = lambda b: (0, 0, 0)
    out_sd = jax.ShapeDtypeStruct((batch, n_pairs, 2 * KV_WIDTH), F32)
    return pl.pallas_call(
        functools.partial(_compress_body, n_pairs=n_pairs),
        grid=(batch,),
        in_specs=[pl.BlockSpec((seq, KV_WIDTH), lambda b: (b, 0)),
                  pl.BlockSpec((seq, KV_WIDTH), lambda b: (b, 0)),
                  pl.BlockSpec(pe_r.shape, const3),
                  pl.BlockSpec(w1_big.shape, lambda b: (0, 0, 0, 0)),
                  pl.BlockSpec(w2_big.shape, const3)],
        out_specs=[pl.BlockSpec((1, n_pairs, 2 * KV_WIDTH), lambda b: (b, 0, 0))] * 2,
        out_shape=[out_sd, out_sd],
        compiler_params=_cparams(("arbitrary",)),
        name="compress_prompt",
    )(k_rows, v_rows, pe_r, w1_big, w2_big)


COMPRESS_PAGES_PER_STEP = 64


def _compress_paged(pool, page_table, cw, cw_pages):
    _, w1_big, w2_big = cw
    pe_t, perm = cw_pages
    batch, n_pages = page_table.shape
    pps = COMPRESS_PAGES_PER_STEP
    assert n_pages % pps == 0 and pps % 2 == 0
    n_steps = n_pages // pps
    n_blk = pps * BLOCKS_PER_PAGE
    const3 = lambda b, c, pt: (0, 0, 0)
    return pl.pallas_call(
        functools.partial(_compress_paged_body, n_pages=pps),
        grid_spec=pltpu.PrefetchScalarGridSpec(
            num_scalar_prefetch=1,
            grid=(batch, n_steps),
            in_specs=[pl.BlockSpec(memory_space=pl.ANY),
                      pl.BlockSpec(pe_t.shape, lambda b, c, pt: (0, 0)),
                      pl.BlockSpec(perm.shape, lambda b, c, pt: (0, 0)),
                      pl.BlockSpec(w1_big.shape, lambda b, c, pt: (0, 0, 0, 0)),
                      pl.BlockSpec(w2_big.shape, const3)],
            out_specs=[pl.BlockSpec((1, n_blk // 2, 2 * KV_WIDTH), lambda b, c, pt: (b, c, 0))] * 2,
            scratch_shapes=[pltpu.VMEM((CMP_BLOCK // 2, n_blk, 4 * KV_WIDTH), F32),
                            pltpu.VMEM((2, n_blk, KV_WIDTH), F32),
                            pltpu.VMEM((2, pps, 2 * KV_WIDTH, PAGE_SIZE), F32),
                            pltpu.SemaphoreType.DMA((2,))]),
        out_shape=[jax.ShapeDtypeStruct((batch, n_steps * n_blk // 2, 2 * KV_WIDTH), F32)] * 2,
        compiler_params=_cparams(("arbitrary", "arbitrary")),
        name="compress_paged",
    )(page_table.reshape(-1), pool, pe_t, perm, w1_big, w2_big)


def _cmp_attn_body(q_ref, ke_ref, ko_ref, o_ref, st_ref, *, tq, pos0):
    ns = ke_ref.shape[1]
    i = pl.program_id(1)
    rows = NSA_GROUP * tq
    tok0 = pos0 + i * tq
    pos_r = tok0 + lax.broadcasted_iota(jnp.int32, (1, rows), 1) % tq
    pair_c = lax.broadcasted_iota(jnp.int32, (ns, 1), 0)
    end_e = (2 * pair_c + 1) * CMP_BLOCK - 1
    end_o = (2 * pair_c + 2) * CMP_BLOCK - 1
    any_r = (CMP_BLOCK - 1 <= pos_r).astype(F32)
    contract_blocks = (((0,), (0,)), ((), ()))
    q = q_ref[...] * ATTN_SCALE
    heads = range(NSA_KV_HEADS)
    te, to = {}, {}
    for kh in heads:
        qs = jnp.concatenate([q[:, (kh * NSA_GROUP + g) * HEAD_DIM:(kh * NSA_GROUP + g + 1) * HEAD_DIM]
                              for g in range(NSA_GROUP)], axis=0).astype(BF16)
        ks = slice(kh * HEAD_DIM, (kh + 1) * HEAD_DIM)
        te[kh] = jnp.where(end_e <= pos_r, _dot_nt(ke_ref[0, :, ks].astype(BF16), qs), NEG_INF)
        to[kh] = jnp.where(end_o <= pos_r, _dot_nt(ko_ref[0, :, ks].astype(BF16), qs), NEG_INF)
    pte, pto = {}, {}
    for kh in heads:
        mt = jnp.maximum(jnp.max(te[kh], axis=0, keepdims=True), jnp.max(to[kh], axis=0, keepdims=True))
        pe, po = jnp.exp(te[kh] - mt), jnp.exp(to[kh] - mt)
        invt = any_r / (jnp.sum(pe, axis=0, keepdims=True) + jnp.sum(po, axis=0, keepdims=True))
        pte[kh], pto[kh] = pe * invt, po * invt
    for kh in heads:
        vs = slice(KV_WIDTH + kh * HEAD_DIM, KV_WIDTH + (kh + 1) * HEAD_DIM)
        oh = (lax.dot_general(pte[kh].astype(BF16), ke_ref[0, :, vs].astype(BF16), contract_blocks,
                              preferred_element_type=F32)
              + lax.dot_general(pto[kh].astype(BF16), ko_ref[0, :, vs].astype(BF16), contract_blocks,
                                preferred_element_type=F32))
        for g in range(NSA_GROUP):
            hd = kh * NSA_GROUP + g
            o_ref[:, hd * HEAD_DIM:(hd + 1) * HEAD_DIM] = oh[g * tq:(g + 1) * tq, :]
        ps = pte[kh] + pto[kh]
        score = ps[:, 0:tq]
        for g in range(1, NSA_GROUP):
            score = score + ps[:, g * tq:(g + 1) * tq]
        st_ref[0, kh] = score


def _cmp_attn(q2d, kce, kco, *, batch, seq, tq, pos0):
    ns = kce.shape[1]
    nq = seq // tq
    return pl.pallas_call(
        functools.partial(_cmp_attn_body, tq=tq, pos0=pos0),
        grid=(batch, nq),
        in_specs=[pl.BlockSpec((tq, NSA_WIDTH), lambda b, i: (b * nq + i, 0)),
                  pl.BlockSpec((1, ns, 2 * KV_WIDTH), lambda b, i: (b, 0, 0)),
                  pl.BlockSpec((1, ns, 2 * KV_WIDTH), lambda b, i: (b, 0, 0))],
        out_specs=[pl.BlockSpec((tq, NSA_WIDTH), lambda b, i: (b * nq + i, 0)),
                   pl.BlockSpec((1, NSA_KV_HEADS, ns, tq), lambda b, i: (b, 0, 0, i))],
        out_shape=[jax.ShapeDtypeStruct((batch * seq, NSA_WIDTH), F32),
                   jax.ShapeDtypeStruct((batch, NSA_KV_HEADS, ns, seq), F32)],
        compiler_params=_cparams(("arbitrary", "arbitrary")),
        name="cmp_attn",
    )(q2d, kce, kco)


def _topk_body(pos_ref, st_ref, b_ref, *, n_sel):
    score = st_ref[0]
    ns, tt = score.shape
    nsw = b_ref.shape[1]
    if nsw > ns:
        score = jnp.concatenate([score, jnp.zeros((nsw - ns, tt), F32)], axis=0)
    blk = lax.broadcasted_iota(jnp.int32, (nsw, 1), 0)
    blk_f = blk.astype(F32)
    cur = pos_ref[...] // SEL_BLOCK
    forced = (blk == 0) | (blk == cur) | (blk == cur - 1)
    pri = jnp.where(blk <= cur, jnp.where(forced, SEL_PRIORITY, score), -SEL_PRIORITY)
    pri = jnp.where(blk < n_sel, pri, -jnp.inf)
    bias = jnp.full((nsw, tt), NEG_INF, F32)
    for _ in range(min(TOP_N, n_sel)):
        top = jnp.max(pri, axis=0, keepdims=True)
        first = jnp.min(jnp.where(pri == top, blk_f, float(nsw)), axis=0, keepdims=True)
        hit = blk_f == first
        bias = jnp.where(hit, 0.0, bias)
        pri = jnp.where(hit, -jnp.inf, pri)
    b_ref[0] = bias


def _topk_blocks(scores_t, pos, *, n_sel, nsw, tt):
    groups, ns, tokens = scores_t.shape
    assert nsw >= max(ns, n_sel) and tokens % tt == 0
    return pl.pallas_call(
        functools.partial(_topk_body, n_sel=n_sel),
        grid=(groups, tokens // tt),
        in_specs=[pl.BlockSpec((1, tt), lambda g, i: (0, i)),
                  pl.BlockSpec((1, ns, tt), lambda g, i: (g, 0, i))],
        out_specs=pl.BlockSpec((1, nsw, tt), lambda g, i: (g, 0, i)),
        out_shape=jax.ShapeDtypeStruct((groups, nsw, tokens), F32),
        compiler_params=_cparams(("arbitrary", "arbitrary")),
        name="topk_blocks",
    )(pos, scores_t)


def _softmax_update(sc, vt_bf16, m_ref, l_ref, acc_ref):
    m_old = m_ref[...]
    m_new = jnp.maximum(m_old, jnp.max(sc, axis=1, keepdims=True))
    alpha = jnp.exp(m_old - m_new)
    pr = jnp.exp(sc - jnp.concatenate([m_new] * (sc.shape[1] // LANES), axis=1))
    l_ref[...] = alpha * l_ref[...] + jnp.sum(pr, axis=1, keepdims=True)
    acc_ref[...] = alpha * acc_ref[...] + _dot_nt(pr.astype(BF16), vt_bf16)
    m_ref[...] = m_new


def _softmax_init(m_ref, l_ref, acc_ref):
    m_ref[...] = jnp.full(m_ref.shape, NEG_INF, F32)
    l_ref[...] = jnp.zeros(l_ref.shape, F32)
    acc_ref[...] = jnp.zeros(acc_ref.shape, F32)


ATTN_TAB_COLS = 5


def _attn_pairs(seq, tq, tk):
    rows = []
    for i in range(seq // tq):
        t_lo, t_hi = i * tq, i * tq + tq - 1
        js = list(range(0, t_hi // tk + 1))
        for n, j in enumerate(js):
            rows.append((i, j, int(n == 0), int(n == len(js) - 1), int(j * tk + tk - 1 > t_lo)))
    return np.asarray(rows, np.int32)


def _attn_body(tab_ref, q_ref, k_ref, vt_ref, oh_ref, sb_ref, o_ref, qa_ref, m_ref, l_ref, acc_ref,
               *, tq, tk):
    p = pl.program_id(1)
    i, j, first, last, partial_tile = [tab_ref[ATTN_TAB_COLS * p + n] for n in range(ATTN_TAB_COLS)]
    cols = NSA_HEADS * tq

    @pl.when(first == 1)
    def _():
        for hd, piece in enumerate(_query_columns(q_ref[0], sb_ref[0], tq)):
            qa_ref[:, hd * tq:(hd + 1) * tq] = piece
        m_ref[...] = jnp.full(m_ref.shape, NEG_INF, F32)
        l_ref[...] = jnp.zeros(l_ref.shape, F32)
        acc_ref[...] = jnp.zeros(acc_ref.shape, F32)

    k_aug = jnp.concatenate([k_ref[...], oh_ref[...]], axis=1)
    sc = _dot(k_aug, qa_ref[...])
    vt = vt_ref[0]

    def update(sc):
        m_old = m_ref[...]
        m_new = jnp.maximum(m_old, jnp.max(sc, axis=0, keepdims=True))
        alpha = jnp.exp2(m_old - m_new)
        pr = jnp.exp2(sc - m_new)
        l_ref[...] = alpha * l_ref[...] + jnp.sum(pr, axis=0, keepdims=True)
        acc_ref[...] = alpha * acc_ref[...] + _dot(vt, pr.astype(BF16))
        m_ref[...] = m_new

    @pl.when(partial_tile == 1)
    def _():
        qpos = i * tq + (lax.broadcasted_iota(jnp.int32, (1, cols), 1) & (tq - 1))
        kpos = j * tk + lax.broadcasted_iota(jnp.int32, (tk, 1), 0)
        update(jnp.where(kpos <= qpos, sc, NEG_INF))

    @pl.when(partial_tile == 0)
    def _():
        update(sc)

    @pl.when(last == 1)
    def _():
        _store_heads(acc_ref[...] / l_ref[...], o_ref, tq)


def _query_columns(q, sel_bias, tq):
    q = q * (ATTN_SCALE * LOG2_E)
    zeros64 = jnp.zeros((HEAD_DIM, tq), F32)
    kv_head_rows = lambda x, kh: jnp.concatenate([x, zeros64] if kh == 0 else [zeros64, x], axis=0)
    pieces = []
    for m in range(NSA_HEADS // 2):
        q_t = q[:, m * LANES:(m + 1) * LANES].T
        for hd in (2 * m, 2 * m + 1):
            kh = hd // NSA_GROUP
            piece = kv_head_rows(q_t[(hd % 2) * HEAD_DIM:(hd % 2 + 1) * HEAD_DIM, :], kh)
            if sel_bias is not None:
                piece = jnp.concatenate([piece, kv_head_rows(sel_bias[kh], kh)], axis=0)
            pieces.append(piece.astype(BF16))
    return pieces


def _store_heads(o_t, o_ref, tq):
    for m in range(NSA_HEADS // 2):
        pair = jnp.concatenate(
            [o_t[(hd // NSA_GROUP) * HEAD_DIM:(hd // NSA_GROUP + 1) * HEAD_DIM, hd * tq:(hd + 1) * tq]
             for hd in (2 * m, 2 * m + 1)], axis=0)
        o_ref[0, :, m * LANES:(m + 1) * LANES] = pair.T


def _window_body(q_ref, *refs, tq, n_tiles):
    k_refs, v_refs, o_ref = refs[:n_tiles], refs[n_tiles:2 * n_tiles], refs[2 * n_tiles]
    i = pl.program_id(1)
    cols = NSA_HEADS * tq
    qa = jnp.concatenate(_query_columns(q_ref[0], None, tq), axis=1)
    qpos = i * tq + (lax.broadcasted_iota(jnp.int32, (1, cols), 1) & (tq - 1))
    scs = []
    for n, k_ref in enumerate(k_refs):
        kpos = (i - (n_tiles - 1) + n) * tq + lax.broadcasted_iota(jnp.int32, (tq, 1), 0)
        if n == n_tiles - 1:
            valid = kpos <= qpos
        elif n == 0:
            valid = (kpos > qpos - WINDOW) & (kpos >= 0)
        else:
            valid = kpos >= 0
        scs.append(jnp.where(valid, _dot(k_ref[...], qa), NEG_INF))
    mx = scs[0].max(axis=0, keepdims=True)
    for sc in scs[1:]:
        mx = jnp.maximum(mx, sc.max(axis=0, keepdims=True))
    l_sum, acc = None, None
    for sc, v_ref in zip(scs, v_refs):
        pr = jnp.exp2(sc - mx)
        part_l, part_acc = jnp.sum(pr, axis=0, keepdims=True), _dot(v_ref[0], pr.astype(BF16))
        l_sum = part_l if l_sum is None else l_sum + part_l
        acc = part_acc if acc is None else acc + part_acc
    _store_heads(acc / l_sum, o_ref, tq)


def _attn_window_prompt(q3d, k_rows, v_t, *, tq):
    batch, seq, _ = q3d.shape
    assert WINDOW % tq == 0 and seq % tq == 0 and tq & (tq - 1) == 0
    n_tiles = WINDOW // tq + 1
    nq = seq // tq
    tile = lambda n: (lambda i: jnp.maximum(i - (n_tiles - 1) + n, 0))
    return pl.pallas_call(
        functools.partial(_window_body, tq=tq, n_tiles=n_tiles),
        grid=(batch, nq),
        in_specs=[pl.BlockSpec((1, tq, NSA_WIDTH), lambda b, i: (b, i, 0))]
        + [pl.BlockSpec((tq, KV_WIDTH), lambda b, i, t=tile(n): (b * nq + t(i), 0)) for n in range(n_tiles)]
        + [pl.BlockSpec((1, KV_WIDTH, tq), lambda b, i, t=tile(n): (b, 0, t(i))) for n in range(n_tiles)],
        out_specs=pl.BlockSpec((1, tq, NSA_WIDTH), lambda b, i: (b, i, 0)),
        out_shape=jax.ShapeDtypeStruct((batch, seq, NSA_WIDTH), F32),
        compiler_params=_cparams(("arbitrary", "arbitrary")),
        name="attn_win",
    )(q3d, *([k_rows] * n_tiles), *([v_t] * n_tiles))


def _block_onehot(seq):
    blk = np.arange(seq)[:, None] // SEL_BLOCK
    return jnp.asarray((np.arange(LANES)[None, :] % SEL_BLOCK) == blk, BF16)


def _attn_selected_prompt(q3d, k_rows, v_t, selb, *, tq, tk):
    batch, seq, _ = q3d.shape
    assert tq & (tq - 1) == 0 and tk % LANES == 0 and tq % LANES == 0 and selb.shape[2] == SEL_BLOCK
    tab = _attn_pairs(seq, tq, tk)
    cols = NSA_HEADS * tq
    C = ATTN_TAB_COLS
    nk = seq // tk
    return pl.pallas_call(
        functools.partial(_attn_body, tq=tq, tk=tk),
        grid_spec=pltpu.PrefetchScalarGridSpec(
            num_scalar_prefetch=1,
            grid=(batch, tab.shape[0]),
            in_specs=[pl.BlockSpec((1, tq, NSA_WIDTH), lambda b, p, t: (b, t[C * p], 0)),
                      pl.BlockSpec((tk, KV_WIDTH), lambda b, p, t: (b * nk + t[C * p + 1], 0)),
                      pl.BlockSpec((1, KV_WIDTH, tk), lambda b, p, t: (b, 0, t[C * p + 1])),
                      pl.BlockSpec((tk, LANES), lambda b, p, t: (t[C * p + 1], 0)),
                      pl.BlockSpec((1, NSA_KV_HEADS, SEL_BLOCK, tq),
                                   lambda b, p, t: (b, 0, 0, t[C * p]))],
            out_specs=pl.BlockSpec((1, tq, NSA_WIDTH), lambda b, p, t: (b, t[C * p], 0)),
            scratch_shapes=[pltpu.VMEM((2 * LANES, cols), BF16), pltpu.VMEM((1, cols), F32),
                            pltpu.VMEM((1, cols), F32), pltpu.VMEM((KV_WIDTH, cols), F32)]),
        out_shape=jax.ShapeDtypeStruct((batch, seq, NSA_WIDTH), F32),
        compiler_params=_cparams(("arbitrary", "arbitrary")),
        name="attn_sel",
    )(jnp.asarray(tab.reshape(-1)), q3d, k_rows, v_t, _block_onehot(seq), selb)


ATTN_PAGES_PER_STEP = 64
ATTN_PAGED_SPLIT = 2


def _attn_paged_body(pt_ref, qa_ref, bq_ref, bn_ref, kn_ref, oh_ref, pool_hbm, o_ref, m_ref, l_ref,
                     acc_ref, pages_ref, sem_ref, *, n_pages, n_new):
    slot = _gather_pages(pt_ref, pool_hbm, pages_ref, sem_ref, n_pages)
    page_refs = [pages_ref.at[slot, j] for j in range(n_pages)]
    c = pl.program_id(1)
    rows = qa_ref.shape[1]

    @pl.when(c == 0)
    def _():
        _softmax_init(m_ref, l_ref, acc_ref)

    n_split = m_ref.shape[0]
    per = n_pages // n_split
    keys = per * PAGE_SIZE
    qa = qa_ref[0]
    bias = bq_ref[0, 0]
    blocks = keys // SEL_BLOCK
    lane = lax.broadcasted_iota(jnp.int32, (1, LANES), 1)
    scs, vts = [], []
    for s in range(n_split):
        refs_s = page_refs[s * per:(s + 1) * per]
        bias_s = bias if s == 0 else pltpu.roll(bias, LANES - s * blocks, axis=1)
        lhs = jnp.concatenate([qa, jnp.where(lane < blocks, bias_s, 0.0)], axis=1).astype(BF16)
        kt = jnp.concatenate([r[0:KV_WIDTH, :] for r in refs_s], axis=1)
        rhs = jnp.concatenate([kt.astype(BF16), oh_ref[...]], axis=0)
        scs.append(_dot(lhs, rhs))
        vts.append(jnp.concatenate([r[KV_WIDTH:, :] for r in refs_s], axis=1).astype(BF16))
    for s in range(n_split):
        _softmax_update(scs[s], vts[s], m_ref.at[s], l_ref.at[s], acc_ref.at[s])

    @pl.when(c == pl.num_programs(1) - 1)
    def _():
        kn = kn_ref[0]
        sc = _dot(qa.astype(BF16), kn[0:KV_WIDTH, :].astype(BF16)) + bn_ref[0]
        tq = lax.broadcasted_iota(jnp.int32, (rows, 1), 0) % n_new
        kk = lax.broadcasted_iota(jnp.int32, (1, kn.shape[1]), 1)
        sc = jnp.where((kk <= tq) & (kk < n_new), sc, NEG_INF)
        _softmax_update(sc, kn[KV_WIDTH:, :].astype(BF16), m_ref.at[0], l_ref.at[0], acc_ref.at[0])
        m_all = m_ref[0]
        for s in range(1, n_split):
            m_all = jnp.maximum(m_all, m_ref[s])
        l_all = jnp.zeros(m_all.shape, F32)
        acc_all = jnp.zeros(m_all.shape, F32)
        for s in range(n_split):
            scale = jnp.exp(m_ref[s] - m_all)
            l_all = l_all + scale * l_ref[s]
            acc_all = acc_all + scale * acc_ref[s]
        o_ref[0] = acc_all / l_all


def _attn_paged(qa, bias_q, bias_new, kv_new_t, pool, page_table, *, n_new):
    batch, n_pages = page_table.shape
    pps = ATTN_PAGES_PER_STEP
    keys_per_chain = pps // ATTN_PAGED_SPLIT * PAGE_SIZE
    assert n_pages % pps == 0 and pps * PAGE_SIZE // SEL_BLOCK <= LANES
    assert keys_per_chain // SEL_BLOCK <= SEL_BLOCK
    n_steps = n_pages // pps
    rows = qa.shape[1]

    per_b = lambda b, c, pt: (b, 0, 0)
    return pl.pallas_call(
        functools.partial(_attn_paged_body, n_pages=pps, n_new=n_new),
        grid_spec=pltpu.PrefetchScalarGridSpec(
            num_scalar_prefetch=1,
            grid=(batch, n_steps),
            in_specs=[pl.BlockSpec((1, rows, LANES), per_b),
                      pl.BlockSpec((1, 1, rows, LANES), lambda b, c, pt: (b, c, 0, 0)),
                      pl.BlockSpec((1, rows, LANES), per_b),
                      pl.BlockSpec((1,) + kv_new_t.shape[1:], per_b),
                      pl.BlockSpec((LANES, keys_per_chain), lambda b, c, pt: (0, 0)),
                      pl.BlockSpec(memory_space=pl.ANY)],
            out_specs=pl.BlockSpec((1, rows, LANES), per_b),
            scratch_shapes=[pltpu.VMEM((ATTN_PAGED_SPLIT, rows, LANES), F32)] * 3
            + [pltpu.VMEM((2, pps, 2 * KV_WIDTH, PAGE_SIZE), F32), pltpu.SemaphoreType.DMA((2,))]),
        out_shape=jax.ShapeDtypeStruct((batch, rows, LANES), F32),
        compiler_params=_cparams(("arbitrary", "arbitrary")),
        name="attn_sel_paged",
    )(page_table.reshape(-1), qa, bias_q, bias_new, kv_new_t, _block_onehot(keys_per_chain).T, pool)


def _attn_window_body(qa_ref, wb_ref, kn_ref, o_ref, *, n_new, past):
    qa = qa_ref[0].astype(BF16)
    wb, kn = wb_ref[0], kn_ref[0]
    rows, n_buf = qa.shape[0], wb.shape[1]
    qpos = past + lax.broadcasted_iota(jnp.int32, (rows, 1), 0) % n_new

    def masked(sc, kpos, extra):
        diff = qpos - kpos
        return jnp.where((diff >= 0) & (diff < WINDOW) & (kpos >= 0) & extra, sc, NEG_INF)

    nb = lax.broadcasted_iota(jnp.int32, (1, n_buf), 1)
    nn = lax.broadcasted_iota(jnp.int32, (1, kn.shape[1]), 1)
    sb = masked(_dot(qa, wb[0:KV_WIDTH, :].astype(BF16)), past - n_buf + nb, nb >= 0)
    sn = masked(_dot(qa, kn[0:KV_WIDTH, :].astype(BF16)), past + nn, nn < n_new)
    mx = jnp.maximum(jnp.max(sb, axis=1, keepdims=True), jnp.max(sn, axis=1, keepdims=True))
    pb, pn = jnp.exp(sb - mx), jnp.exp(sn - mx)
    o = (_dot_nt(pb.astype(BF16), wb[KV_WIDTH:, :].astype(BF16))
         + _dot_nt(pn.astype(BF16), kn[KV_WIDTH:, :].astype(BF16)))
    o_ref[0] = o / (jnp.sum(pb, axis=1, keepdims=True) + jnp.sum(pn, axis=1, keepdims=True))


def _attn_window_small(qa, win_t, kv_new_t, *, n_new, past):
    batch, rows, _ = qa.shape
    per_b = lambda b: (b, 0, 0)
    return pl.pallas_call(
        functools.partial(_attn_window_body, n_new=n_new, past=past),
        grid=(batch,),
        in_specs=[pl.BlockSpec((1, rows, LANES), per_b),
                  pl.BlockSpec((1,) + win_t.shape[1:], per_b),
                  pl.BlockSpec((1,) + kv_new_t.shape[1:], per_b)],
        out_specs=pl.BlockSpec((1, rows, LANES), per_b),
        out_shape=jax.ShapeDtypeStruct((batch, rows, LANES), F32),
        compiler_params=_cparams(("arbitrary",)),
        name="attn_win_small",
    )(qa, win_t, kv_new_t)


FFN_TM = 512


def _ffn_vmem_bytes(tm, halo):
    weights = 2 * (D_MODEL * D_MODEL + D_MODEL * 2 * D_FF + D_FF * D_MODEL)
    conv_buffer = 4 * (halo + tm) * 2 * D_FF
    row_tiles = 2 * 4 * tm * (2 * D_MODEL + 4 * NSA_WIDTH + LANES)
    temporaries = 3 * 4 * tm * max(hi - lo for lo, hi in FFN_CHUNKS)
    return weights + conv_buffer + row_tiles + temporaries
MXU_DEPTH = 256
FFN_CHUNKS = ((0, 6 * MXU_DEPTH), (6 * MXU_DEPTH, D_FF))


def _ffn_body(x_ref, om_ref, oc_ref, os_ref, ow_ref, gt_ref, ge_ref, gn_ref, gf_ref, gl_ref, wc_ref,
              fb_ref, wo_hbm, wu_hbm, wd_hbm, y_ref, fn_ref, xx_ref, wo_ref, wu_ref, wd_ref, sem_ref,
              *, tm, stride, halo):
    s = pl.program_id(1)

    @pl.when((pl.program_id(0) == 0) & (s == 0))
    def _():
        copies = [pltpu.make_async_copy(src, dst, sem_ref.at[n])
                  for n, (src, dst) in enumerate(((wo_hbm, wo_ref), (wu_hbm, wu_ref), (wd_hbm, wd_ref)))]
        for cp in copies:
            cp.start()
        for cp in copies:
            cp.wait()

    sig = _sigmoid(gt_ref[...])
    hi = sig.astype(BF16)
    lo = (sig - hi.astype(F32)).astype(BF16)
    comb = None
    for br, ob_ref in enumerate((oc_ref, os_ref, ow_ref)):
        gate = _dot(hi, ge_ref[br]) + _dot(lo, ge_ref[br])
        term = gate * ob_ref[...]
        comb = term if comb is None else comb + term
    onsa = _rms(comb, gn_ref[...])
    h = (x_ref[...] + _dot(om_ref[...].astype(BF16), wo_ref[0:MLSTM_WIDTH, :])
         + _dot(onsa.astype(BF16), wo_ref[MLSTM_WIDTH:, :]))
    hn = _rms(h, gf_ref[...]).astype(BF16)

    base = halo - (FFN_CONV - 1) * stride

    @pl.when(s == 0)
    def _():
        xx_ref[base:halo, :] = fb_ref[0]

    y_ref[...] = h

    def up_project(chunk):
        for half in range(2):
            cols = slice(half * D_FF + chunk[0], half * D_FF + chunk[1])
            xx_ref[halo:halo + tm, cols] = _dot(hn, wu_ref[:, cols])

    up_project(FFN_CHUNKS[0])
    for n, (lo_col, hi_col) in enumerate(FFN_CHUNKS):
        if n + 1 < len(FFN_CHUNKS):
            up_project(FFN_CHUNKS[n + 1])
        convs = []
        for half in range(2):
            cols = slice(half * D_FF + lo_col, half * D_FF + hi_col)
            conv = xx_ref[base:base + tm, cols] * wc_ref[0:1, cols]
            for j in range(1, FFN_CONV):
                conv = conv + xx_ref[base + j * stride:base + j * stride + tm, cols] * wc_ref[j:j + 1, cols]
            convs.append(conv)
        act = _silu(convs[1]) * convs[0]
        y_ref[...] += _dot(act.astype(BF16), wd_ref[lo_col:hi_col, :])
    fn_ref[0, 0] = xx_ref[tm + base:tm + halo, :]
    xx_ref[0:halo, :] = xx_ref[tm:tm + halo, :]
    y_ref[...] = _rms(y_ref[...], gl_ref[...])


def _gate_expand():
    ge = np.zeros((N_BRANCH, LANES, NSA_WIDTH), np.float32)
    for hd in range(NSA_HEADS):
        for br in range(N_BRANCH):
            ge[br, GATE_COL_NSA + hd * N_BRANCH + br, hd * HEAD_DIM:(hd + 1) * HEAD_DIM] = 1.0
    return jnp.asarray(ge, BF16)


def _ffn(x2d, om, oc, osel, ow, gt, fbuf, w_out, g_nsa, g_ffn, g_final, w_up, w_fconv, w_down,
         *, nb, tm, stride):
    rows = x2d.shape[0]
    ns = rows // (nb * tm)
    halo = -(-(FFN_CONV - 1) * stride // SUBLANES) * SUBLANES
    assert tm >= halo and all((hi - lo) % MXU_DEPTH == 0 for lo, hi in FFN_CHUNKS)
    vmem_limit = _ffn_vmem_bytes(tm, halo)
    assert vmem_limit <= VMEM_BYTES
    tok = lambda b, s: (b * ns + s, 0)
    nfb = (FFN_CONV - 1) * stride

    def const(shape):
        return pl.BlockSpec(shape, lambda b, s: (0,) * len(shape))

    hbm = pl.BlockSpec(memory_space=pl.ANY)
    y, fn = pl.pallas_call(
        functools.partial(_ffn_body, tm=tm, stride=stride, halo=halo),
        grid=(nb, ns),
        in_specs=[pl.BlockSpec((tm, D_MODEL), tok)] + [pl.BlockSpec((tm, NSA_WIDTH), tok)] * 4
        + [pl.BlockSpec((tm, LANES), tok),
           const((N_BRANCH, LANES, NSA_WIDTH)), const((1, NSA_WIDTH)), const((1, D_MODEL)),
           const((1, D_MODEL)), const((FFN_CONV, 2 * D_FF)),
           pl.BlockSpec((1, nfb, 2 * D_FF), lambda b, s: (b, 0, 0)), hbm, hbm, hbm],
        out_specs=[pl.BlockSpec((tm, D_MODEL), tok),
                   pl.BlockSpec((1, 1, nfb, 2 * D_FF), lambda b, s: (b, s, 0, 0))],
        out_shape=[jax.ShapeDtypeStruct((rows, D_MODEL), F32),
                   jax.ShapeDtypeStruct((nb, ns, nfb, 2 * D_FF), F32)],
        scratch_shapes=[pltpu.VMEM((halo + tm, 2 * D_FF), F32),
                        pltpu.VMEM((D_MODEL, D_MODEL), BF16), pltpu.VMEM((D_MODEL, 2 * D_FF), BF16),
                        pltpu.VMEM((D_FF, D_MODEL), BF16), pltpu.SemaphoreType.DMA((3,))],
        compiler_params=pltpu.CompilerParams(dimension_semantics=("arbitrary", "arbitrary"),
                                             vmem_limit_bytes=vmem_limit),
        name="outproj_ffn",
    )(x2d, om, oc, osel, ow, gt, _gate_expand(), g_nsa.reshape(1, -1), g_ffn.reshape(1, -1),
      g_final.reshape(1, -1), w_fconv, fbuf, w_out.astype(BF16), w_up.astype(BF16),
      w_down.astype(BF16))
    return y, fn[:, ns - 1]


PROMPT_TM = 512
PROMPT_TQ_CMP = 512
PROMPT_TT_TOPK = 1024
PROMPT_TQ_SEL = 512
PROMPT_TK_SEL = 512
PROMPT_TQ_WIN = 256


def _kv_rows(kv_t):
    batch, _, rows = kv_t.shape
    return kv_t.reshape(batch, 2, NSA_KV_HEADS, HEAD_DIM, rows).transpose(0, 4, 1, 2, 3)


def _kv_feature_major(kv5):
    batch, rows = kv5.shape[:2]
    return kv5.transpose(0, 2, 3, 4, 1).reshape(batch, 2 * KV_WIDTH, rows)


def _prompt_layer(x, wts):
    batch, seq, _ = x.shape
    x2d = x.reshape(batch * seq, D_MODEL)
    q, kc_rows, vc_rows, mu, mv, mo, gt, ks_rows, kw_rows, kvc_t, kvs_t, kvw_t, vs_t, vw_t = _in_proj(
        x2d, wts["g_mix"], wts["w_in_packed"], batch=batch, seq=seq, tm=min(PROMPT_TM, seq))
    H, DH, W = MLSTM_HEADS, MLSTM_DH, MLSTM_WIDTH
    o_m, mconv, c_new, n_new, m_new = _mlstm(
        mu, mv, mo, gt, jnp.zeros((batch, MLSTM_CONV - 1, W), F32), jnp.zeros((batch, H, DH, DH), F32),
        jnp.zeros((batch, H, DH), F32), jnp.zeros((batch, H), F32),
        wts["w_mconv"], wts["b_mconv"], wts["w_mq"], wts["w_mk"], wts["b_ig"], wts["b_fg"],
        wts["g_mhead"], wts["m_skip"], batch=batch, seq=seq)
    kce, kco = _compress_prompt(kc_rows, vc_rows, wts["cw"], batch=batch, seq=seq)
    n_sel = -(-seq // SEL_BLOCK)
    assert n_sel <= SEL_BLOCK
    o_cmp, scores_t = _cmp_attn(q, kce, kco, batch=batch, seq=seq, tq=min(PROMPT_TQ_CMP, seq), pos0=0)
    selb = _topk_blocks(scores_t.reshape(batch * NSA_KV_HEADS, -1, seq),
                        jnp.arange(seq, dtype=jnp.int32).reshape(1, seq),
                        n_sel=n_sel, nsw=SEL_BLOCK, tt=min(PROMPT_TT_TOPK, seq))
    selb = selb.reshape(batch, NSA_KV_HEADS, SEL_BLOCK, seq)
    q3d = q.reshape(batch, seq, NSA_WIDTH)
    o_sel = _attn_selected_prompt(q3d, ks_rows, vs_t, selb, tq=min(PROMPT_TQ_SEL, seq),
                                  tk=min(PROMPT_TK_SEL, seq))
    o_win = _attn_window_prompt(q3d, kw_rows, vw_t, tq=PROMPT_TQ_WIN)
    fbuf = jnp.zeros((batch, FFN_CONV - 1, 2 * D_FF), F32)
    y, f_new = _ffn(x2d, o_m, o_cmp, o_sel.reshape(-1, NSA_WIDTH), o_win.reshape(-1, NSA_WIDTH), gt,
                    fbuf, wts["w_out"], wts["g_nsa"], wts["g_ffn"], wts["g_final"], wts["w_up"],
                    wts["w_fconv"], wts["w_down"], nb=batch, tm=min(FFN_TM, seq), stride=1)
    n_win = min(WINDOW, seq)
    return (y.reshape(batch, seq, D_MODEL), _kv_rows(kvc_t), _kv_rows(kvs_t),
            _kv_rows(kvw_t[:, :, seq - n_win:]), mconv, c_new, n_new, m_new.reshape(batch, H), f_new)


def _decode_rows(q2d, batch, seq):
    q5 = (q2d * ATTN_SCALE).reshape(batch, seq, NSA_KV_HEADS, NSA_GROUP, HEAD_DIM).transpose(0, 2, 3, 1, 4)
    eye = jnp.eye(NSA_KV_HEADS, dtype=F32)
    qa = jnp.einsum('bkgtd,kK->bkgtKd', q5, eye)
    return qa.reshape(batch, NSA_KV_HEADS * NSA_GROUP * seq, KV_WIDTH)


def _decode_rows_out(o, batch, seq):
    o6 = o.reshape(batch, NSA_KV_HEADS, NSA_GROUP, seq, NSA_KV_HEADS, HEAD_DIM)
    o5 = jnp.stack([o6[:, kh, :, :, kh, :] for kh in range(NSA_KV_HEADS)], axis=1)
    return o5.transpose(0, 3, 1, 2, 4).reshape(batch * seq, NSA_WIDTH)


def _sample_layer(x, pool_cmp, pool_sel, win_buf, m_conv, m_c, m_n, m_m, f_buf, page_table, wts):
    batch, seq, _ = x.shape
    n_pages = page_table.shape[1]
    past = n_pages * PAGE_SIZE
    assert past % SEL_BLOCK == 0 and seq <= SEL_BLOCK and seq < CMP_BLOCK
    x2d = x.reshape(batch * seq, D_MODEL)
    q, _, _, mu, mv, mo, gt, _, _, kvc_t, kvs_t, kvw_t, _, _ = _in_proj(
        x2d, wts["g_mix"], wts["w_in_packed"], batch=1, seq=batch * seq, tm=batch * seq)
    per_batch = lambda a: a.reshape(2 * KV_WIDTH, batch, seq).transpose(1, 0, 2)
    kvc_t, kvs_t, kvw_t = per_batch(kvc_t), per_batch(kvs_t), per_batch(kvw_t)
    pad_keys = lambda a: jnp.pad(a, ((0, 0), (0, 0), (0, LANES - seq)))
    H = MLSTM_HEADS
    o_m, mconv, c_new, n_new, m_new = _mlstm(
        mu, mv, mo, gt, m_conv, m_c, m_n, m_m,
        wts["w_mconv"], wts["b_mconv"], wts["w_mq"], wts["w_mk"], wts["b_ig"], wts["b_fg"],
        wts["g_mhead"], wts["m_skip"], batch=batch, seq=seq)
    pool_cmp3, pool_sel3 = _kv_feature_major(pool_cmp), _kv_feature_major(pool_sel)
    kce, kco = _compress_paged(pool_cmp3, page_table, wts["cw"], wts["cw_pages"])
    n_past_blk = past // SEL_BLOCK
    n_sel = -(-(past + seq) // SEL_BLOCK)
    o_cmp, scores_t = _cmp_attn(q, kce, kco, batch=batch, seq=seq, tq=seq, pos0=past)
    ns = scores_t.shape[2]
    nsw = ns + LANES
    scores_all = scores_t.transpose(1, 2, 0, 3).reshape(NSA_KV_HEADS, ns, batch * seq)
    pos_all = (past + jnp.arange(batch * seq, dtype=jnp.int32) % seq).reshape(1, batch * seq)
    selb = _topk_blocks(scores_all, pos_all, n_sel=n_sel, nsw=nsw, tt=batch * seq)
    selb = selb.reshape(NSA_KV_HEADS, nsw, batch, seq).transpose(2, 0, 3, 1)
    qa = _decode_rows(q, batch, seq)
    rows = qa.shape[1]
    blk_per_step = ATTN_PAGES_PER_STEP * PAGE_SIZE // SEL_BLOCK
    n_steps = n_pages // ATTN_PAGES_PER_STEP
    sb_rows = jnp.broadcast_to(selb[:, :, None], (batch, NSA_KV_HEADS, NSA_GROUP, seq, selb.shape[-1]))
    sb_rows = sb_rows.reshape(batch, rows, selb.shape[-1])
    bias_q = sb_rows[:, :, :n_past_blk].reshape(batch, rows, n_steps, blk_per_step).transpose(0, 2, 1, 3)
    bias_q = jnp.pad(bias_q, ((0, 0), (0, 0), (0, 0), (0, LANES - blk_per_step)))
    bias_new = jnp.broadcast_to(sb_rows[:, :, n_past_blk:n_past_blk + 1], (batch, rows, LANES))
    o_sel = _attn_paged(qa, bias_q, bias_new, pad_keys(kvs_t), pool_sel3, page_table, n_new=seq)
    n_buf = win_buf.shape[1]
    assert past >= n_buf
    win_t = _kv_feature_major(win_buf)
    o_win = _attn_window_small(qa, win_t, pad_keys(kvw_t), n_new=seq, past=past)
    win_new = jnp.concatenate([win_t, kvw_t], axis=2)[:, :, seq:]
    tmaj = lambda a: a.reshape(batch, seq, -1).transpose(1, 0, 2).reshape(batch * seq, -1)
    fb_t = f_buf.transpose(1, 0, 2).reshape(1, (FFN_CONV - 1) * batch, 2 * D_FF)
    y, f_new = _ffn(tmaj(x2d), tmaj(o_m), tmaj(o_cmp), tmaj(_decode_rows_out(o_sel, batch, seq)),
                    tmaj(_decode_rows_out(o_win, batch, seq)), tmaj(gt), fb_t,
                    wts["w_out"], wts["g_nsa"], wts["g_ffn"], wts["g_final"], wts["w_up"],
                    wts["w_fconv"], wts["w_down"], nb=1, tm=batch * seq, stride=batch)
    y = y.reshape(seq, batch, D_MODEL).transpose(1, 0, 2)
    f_new = f_new.reshape(FFN_CONV - 1, batch, 2 * D_FF).transpose(1, 0, 2)
    return (y, _kv_rows(kvc_t), _kv_rows(kvs_t), _kv_rows(win_new), mconv, c_new, n_new,
            m_new.reshape(batch, H), f_new)


def kernel(x_prompt, x_sample, cache_cmp, cache_sel, state_win, state_mlstm_C, state_mlstm_n,
           state_mlstm_m, state_mlstm_conv, state_ffn_conv, page_table,
           g_mix, w_in, w_out, w_mconv, b_mconv, w_mq, w_mk, b_ig, b_fg, g_mhead, m_skip,
           pe_cmp, w_cmp1, w_cmp2, g_nsa, g_ffn, w_up, w_fconv, w_down, g_final):
    assert w_in.shape[0] == 1, "one layer: the final norm is fused into the layer's FFN kernel"
    l = 0
    wts = dict(g_mix=g_mix[l], w_in_packed=_pack_w_in(w_in[l]), w_out=w_out[l], w_mconv=w_mconv[l],
               b_mconv=b_mconv[l], w_mq=w_mq[l], w_mk=w_mk[l], b_ig=b_ig[l], b_fg=b_fg[l],
               g_mhead=g_mhead[l], m_skip=m_skip[l],
               cw=_pack_compress_weights(pe_cmp[l], w_cmp1[l], w_cmp2[l]),
               cw_pages=_page_pair_constants(pe_cmp[l]),
               g_nsa=g_nsa[l], g_ffn=g_ffn[l], g_final=g_final, w_up=w_up[l], w_fconv=w_fconv[l],
               w_down=w_down[l])
    p = _prompt_layer(x_prompt, wts)
    s = _sample_layer(x_sample, cache_cmp[l], cache_sel[l], state_win[l], state_mlstm_conv[l],
                      state_mlstm_C[l], state_mlstm_n[l], state_mlstm_m[l], state_ffn_conv[l],
                      page_table, wts)
    yp, cmp_p, sel_p, win_p, mconv_p, c_p, n_p, m_p, fconv_p = p
    ys, cmp_s, sel_s, win_s, mconv_s, c_s, n_s, m_s, fconv_s = s
    st = lambda a: a[None]
    return (yp, ys, st(cmp_p), st(cmp_s), st(sel_p), st(sel_s), st(win_p), st(win_s),
            st(c_p), st(c_s), st(n_p), st(n_s), st(m_p), st(m_s), st(mconv_p), st(mconv_s),
            st(fconv_p), st(fconv_s))
```

```python
import functools

import numpy as np
import jax
import jax.numpy as jnp
from jax import lax
from jax.experimental import pallas as pl
from jax.experimental.pallas import tpu as pltpu

F32 = jnp.float32
BF16 = jnp.bfloat16

D_MODEL = 1024
PAGE_SIZE = 128
HEAD_DIM = 64
NSA_HEADS = 8
NSA_KV_HEADS = 2
NSA_GROUP = NSA_HEADS // NSA_KV_HEADS
NSA_WIDTH = NSA_HEADS * HEAD_DIM
KV_WIDTH = NSA_KV_HEADS * HEAD_DIM
CMP_BLOCK = 32
CMP_HIDDEN = 2 * HEAD_DIM
SEL_BLOCK = 64
TOP_N = 16
WINDOW = 512
N_BRANCH = 3
ATTN_SCALE = HEAD_DIM ** -0.5
MLSTM_HEADS = 4
MLSTM_WIDTH = D_MODEL - NSA_WIDTH
MLSTM_DH = MLSTM_WIDTH // MLSTM_HEADS
MLSTM_CONV = 4
D_FF = ((8 * D_MODEL // 3 + 127) // 128) * 128
FFN_CONV = 3
EPS = 1e-6
NEG_INF = -1e30
SEL_PRIORITY = 1e4
LOG2_E = 1.4426950408889634

LANES = 128
SUBLANES = 8
VMEM_BYTES = 64 * 1024 * 1024
VMEM_LIMIT = 3 * VMEM_BYTES // 4

GATE_COL_NSA = 0
GATE_COL_I = NSA_HEADS * N_BRANCH
GATE_COL_F = GATE_COL_I + MLSTM_HEADS

MLSTM_CHUNK = 128
MLSTM_SEQS_PER_STEP = 4


def _cparams(sem):
    return pltpu.CompilerParams(dimension_semantics=sem, vmem_limit_bytes=VMEM_LIMIT)


def _dot(a, b):
    return jnp.dot(a, b, preferred_element_type=F32)


def _dot_nt(a, b):
    return lax.dot_general(a, b, (((1,), (1,)), ((), ())), preferred_element_type=F32)


def _sigmoid(x):
    return 1.0 / (1.0 + jnp.exp(-x))


def _silu(x):
    return x * _sigmoid(x)


def _rms(x, g):
    return x * lax.rsqrt(jnp.mean(x * x, axis=-1, keepdims=True) + EPS) * g


IN_ROW_WIDTHS = (NSA_WIDTH, KV_WIDTH, KV_WIDTH, MLSTM_WIDTH, MLSTM_WIDTH, MLSTM_WIDTH, LANES,
                 KV_WIDTH, KV_WIDTH)
IN_ROW_DTYPES = (F32,) * 7 + (BF16,) * 2
N_KV_BRANCH = 3


def _inproj_body(x_ref, g_ref, w_ref, wt_ref, *out_refs):
    xb = _rms(x_ref[...], g_ref[...]).astype(BF16)
    off = 0
    n_rows = len(IN_ROW_WIDTHS)
    for ref in out_refs[:n_rows]:
        n = ref.shape[-1]
        ref[...] = _dot(xb, w_ref[:, off:off + n]).astype(ref.dtype)
        off += n
    kv_refs = out_refs[n_rows:n_rows + N_KV_BRANCH]
    vt_refs = out_refs[n_rows + N_KV_BRANCH:]
    for n, ref in enumerate(kv_refs):
        kv_t = _dot_nt(wt_ref[n * 2 * KV_WIDTH:(n + 1) * 2 * KV_WIDTH, :], xb)
        ref[0] = kv_t
        if n > 0:
            vt_refs[n - 1][0] = kv_t[KV_WIDTH:, :].astype(BF16)


def _pack_w_in(w_in):
    splits = np.cumsum([NSA_WIDTH, 2 * KV_WIDTH, 2 * KV_WIDTH, 2 * KV_WIDTH, NSA_HEADS * N_BRANCH,
                        MLSTM_WIDTH, MLSTM_WIDTH, MLSTM_WIDTH, MLSTM_HEADS]).tolist()
    q, kvc, kvs, kvw, gt, mu, mv, mo, mi, mf = jnp.split(w_in, splits, axis=1)
    gates = jnp.concatenate([gt, mi, mf], axis=1)
    gates = jnp.pad(gates, ((0, 0), (0, LANES - gates.shape[1])))
    w_rows = jnp.concatenate([q, kvc, mu, mv, mo, gates, kvs[:, :KV_WIDTH], kvw[:, :KV_WIDTH]],
                             axis=1).astype(BF16)
    w_kv_t = jnp.concatenate([kvc, kvs, kvw], axis=1).T.astype(BF16)
    return w_rows, w_kv_t


def _in_proj(x2d, g_mix, w_packed, *, batch, seq, tm):
    w_rows, w_kv_t = w_packed
    t = x2d.shape[0]
    ns = seq // tm
    kv_sd = jax.ShapeDtypeStruct((batch, 2 * KV_WIDTH, seq), F32)
    vt_sd = jax.ShapeDtypeStruct((batch, KV_WIDTH, seq), BF16)
    feat_major = lambda rows: pl.BlockSpec((1, rows, tm), lambda i: (i // ns, 0, i % ns))
    return pl.pallas_call(
        _inproj_body,
        grid=(t // tm,),
        in_specs=[pl.BlockSpec((tm, D_MODEL), lambda i: (i, 0)),
                  pl.BlockSpec((1, D_MODEL), lambda i: (0, 0)),
                  pl.BlockSpec(w_rows.shape, lambda i: (0, 0)),
                  pl.BlockSpec(w_kv_t.shape, lambda i: (0, 0))],
        out_specs=[pl.BlockSpec((tm, n), lambda i: (i, 0)) for n in IN_ROW_WIDTHS]
        + [feat_major(2 * KV_WIDTH)] * N_KV_BRANCH + [feat_major(KV_WIDTH)] * (N_KV_BRANCH - 1),
        out_shape=[jax.ShapeDtypeStruct((t, n), dt) for n, dt in zip(IN_ROW_WIDTHS, IN_ROW_DTYPES)]
        + [kv_sd] * N_KV_BRANCH + [vt_sd] * (N_KV_BRANCH - 1),
        compiler_params=_cparams(("arbitrary",)),
        name="in_proj",
    )(x2d, g_mix.reshape(1, D_MODEL), w_rows, w_kv_t)


def _mlstm_body(*refs, valid, bb):
    cb_ref, c0_ref, n0_ref, m0_ref = refs[4:8]
    cn_ref, c_ref, n_ref, m_ref, xx_ref = refs[16:21]
    halo = SUBLANES

    @pl.when(pl.program_id(1) == 0)
    def _():
        xx_ref[:, 0:halo, :] = jnp.zeros((bb, halo, MLSTM_WIDTH), F32)
        xx_ref[:, halo - (MLSTM_CONV - 1):halo, :] = cb_ref[...]
        c_ref[...] = c0_ref[...]
        n_ref[...] = n0_ref[...]
        m_ref[...] = m0_ref[...]

    _mlstm_chunk(*refs, valid=valid, bb=bb)


def _mlstm_chunk(mu_ref, mv_ref, mo_ref, g_ref, cb_ref, c0_ref, n0_ref, m0_ref,
                 wc_ref, bc_ref, wq_ref, wk_ref, gb_ref, gh_ref, sk_ref,
                 o_ref, cn_ref, c_ref, n_ref, m_ref,
                 xx_ref, vpad_ref, gpad_ref, *, valid, bb):
    L = MLSTM_CHUNK
    DH = MLSTM_DH
    halo = SUBLANES
    units = [(bi, h) for bi in range(bb) for h in range(MLSTM_HEADS)]
    head_lanes = lambda h: slice(h * DH, (h + 1) * DH)
    row = lax.broadcasted_iota(jnp.int32, (L, L), 0)
    col = lax.broadcasted_iota(jnp.int32, (L, L), 1)
    tril = row >= col
    triu = row <= col
    tok_col = lax.broadcasted_iota(jnp.int32, (L, 1), 0)
    tok_row = lax.broadcasted_iota(jnp.int32, (1, L), 1)

    def log_sigmoid(x):
        return jnp.minimum(x, 0.0) - jnp.log(1.0 + jnp.exp(-jnp.abs(x)))

    uc, gb, gbt = {}, {}, {}
    for bi in range(bb):
        if valid < L:
            xx_ref[bi, halo:, :] = jnp.zeros((L, MLSTM_WIDTH), F32)
            vpad_ref[bi] = jnp.zeros((L, MLSTM_WIDTH), F32)
            gpad_ref[bi] = jnp.zeros((L, LANES), F32)
        xx_ref[bi, halo:halo + valid, :] = mu_ref[bi]
        vpad_ref[bi, 0:valid, :] = mv_ref[bi]
        gpad_ref[bi, 0:valid, :] = g_ref[bi]
        conv = xx_ref[bi, halo - 3:halo - 3 + L, :] * wc_ref[0:1, :]
        for j in range(1, MLSTM_CONV):
            conv = conv + xx_ref[bi, halo - 3 + j:halo - 3 + j + L, :] * wc_ref[j:j + 1, :]
        uc[bi] = _silu(conv + bc_ref[...])
        tail = xx_ref[bi, valid + halo - 3:valid + halo, :]
        xx_ref[bi, halo - 3:halo, :] = tail
        cn_ref[bi] = tail
        gb[bi] = gpad_ref[bi] + gb_ref[...]
        gbt[bi] = gb[bi].T

    q, k, qb, kb = {}, {}, {}, {}
    for u in units:
        bi, h = u
        ub = uc[bi][:, head_lanes(h)].astype(BF16)
        q[u] = _dot(ub, wq_ref[h])
        k[u] = _dot(ub, wk_ref[h]) * (DH ** -0.5)
        qb[u], kb[u] = q[u].astype(BF16), k[u].astype(BF16)

    ic_col, ic_row, cum_col, cum_row = {}, {}, {}, {}
    for u in units:
        bi, h = u
        ic_c = gb[bi][:, GATE_COL_I + h:GATE_COL_I + h + 1]
        ic_r = gbt[bi][GATE_COL_I + h:GATE_COL_I + h + 1, :]
        lf_c = log_sigmoid(gb[bi][:, GATE_COL_F + h:GATE_COL_F + h + 1])
        lf_r = log_sigmoid(gbt[bi][GATE_COL_F + h:GATE_COL_F + h + 1, :])
        if valid < L:
            ic_c = jnp.where(tok_col < valid, ic_c, NEG_INF)
            ic_r = jnp.where(tok_row < valid, ic_r, NEG_INF)
            lf_c = jnp.where(tok_col < valid, lf_c, 0.0)
            lf_r = jnp.where(tok_row < valid, lf_r, 0.0)
        ic_col[u], ic_row[u] = ic_c, ic_r
        cum_col[u] = jnp.sum(jnp.where(tril, lf_r, 0.0), axis=1, keepdims=True)
        cum_row[u] = jnp.sum(jnp.where(triu, lf_c, 0.0), axis=0, keepdims=True)

    m_t, w, sc = {}, {}, {}
    for u in units:
        bi, h = u
        m0 = m_ref[bi, 0:1, h:h + 1]
        dmat = jnp.where(tril, cum_col[u] - cum_row[u] + ic_row[u], NEG_INF)
        inter = cum_col[u] + m0
        m_t[u] = jnp.maximum(inter, jnp.max(dmat, axis=1, keepdims=True))
        w[u] = jnp.exp(dmat - m_t[u])
        sc[u] = jnp.exp(inter - m_t[u])

    s, inter_num = {}, {}
    for u in units:
        bi, h = u
        s[u] = _dot_nt(qb[u], kb[u]) * w[u]
        inter_num[u] = _dot_nt(qb[u], c_ref[bi, h].astype(BF16))
    hc = {}
    for u in units:
        bi, h = u
        v = vpad_ref[bi, :, head_lanes(h)]
        n_old = n_ref[bi, h:h + 1, :]
        num = _dot(s[u].astype(BF16), v.astype(BF16)) + sc[u] * inter_num[u]
        den = (jnp.sum(s[u], axis=1, keepdims=True)
               + sc[u] * jnp.sum(q[u] * n_old, axis=1, keepdims=True))
        hc[u] = num / jnp.maximum(jnp.abs(den), jnp.exp(-m_t[u]))

    wl, sl, vw_t = {}, {}, {}
    for u in units:
        bi, h = u
        m_new = m_t[u][L - 1:L, :]
        cum_last = cum_col[u][L - 1:L, :]
        wl[u] = jnp.exp(cum_last - cum_col[u] + ic_col[u] - m_new)
        sl[u] = jnp.exp(cum_last + m_ref[bi, 0:1, h:h + 1] - m_new)
        vw_t[u] = (vpad_ref[bi, :, head_lanes(h)] * wl[u]).T.astype(BF16)
    for u in units:
        bi, h = u
        c_ref[bi, h] = sl[u] * c_ref[bi, h] + _dot(vw_t[u], kb[u])
        n_ref[bi, h:h + 1, :] = (sl[u] * n_ref[bi, h:h + 1, :]
                                 + jnp.sum(wl[u] * k[u], axis=0, keepdims=True))
        m_ref[bi, 0:1, h:h + 1] = m_t[u][L - 1:L, :]

    for u in units:
        bi, h = u
        hn = _rms(hc[u], gh_ref[:, head_lanes(h)])
        u_h = uc[bi][:, head_lanes(h)]
        out = ((hn[0:valid, :] + sk_ref[:, head_lanes(h)] * u_h[0:valid, :])
               * _sigmoid(mo_ref[bi, :, head_lanes(h)]))
        o_ref[bi, :, head_lanes(h)] = out


def _mlstm(mu, mv, mo, gates, conv_buf, c0, n0, m0, w_mconv, b_mconv, w_mq, w_mk, b_ig, b_fg,
           g_mhead, m_skip, *, batch, seq):
    L = MLSTM_CHUNK
    valid = min(seq, L)
    assert seq % valid == 0 and (valid == L or seq == valid)
    nc = seq // valid
    gate_bias = jnp.zeros((1, LANES), F32)
    gate_bias = gate_bias.at[0, GATE_COL_I:GATE_COL_I + MLSTM_HEADS].set(b_ig)
    gate_bias = gate_bias.at[0, GATE_COL_F:GATE_COL_F + MLSTM_HEADS].set(b_fg)
    bb = MLSTM_SEQS_PER_STEP
    assert batch % bb == 0
    tok = lambda b, c: (b, c, 0)
    const2 = lambda b, c: (0, 0)
    const3 = lambda b, c: (0, 0, 0)
    per_b3 = lambda b, c: (b, 0, 0)
    per_b4 = lambda b, c: (b, 0, 0, 0)
    H, DH, W = MLSTM_HEADS, MLSTM_DH, MLSTM_WIDTH
    rows3 = lambda a: a.reshape(batch, seq, a.shape[-1])
    o_m, conv_new, c_new, n_new, m_new = pl.pallas_call(
        functools.partial(_mlstm_body, valid=valid, bb=bb),
        grid=(batch // bb, nc),
        in_specs=[pl.BlockSpec((bb, valid, W), tok), pl.BlockSpec((bb, valid, W), tok),
                  pl.BlockSpec((bb, valid, W), tok), pl.BlockSpec((bb, valid, LANES), tok),
                  pl.BlockSpec((bb, MLSTM_CONV - 1, W), per_b3),
                  pl.BlockSpec((bb, H, DH, DH), per_b4),
                  pl.BlockSpec((bb, H, DH), per_b3),
                  pl.BlockSpec((bb, 1, H), per_b3),
                  pl.BlockSpec((MLSTM_CONV, W), const2), pl.BlockSpec((1, W), const2),
                  pl.BlockSpec((H, DH, DH), const3), pl.BlockSpec((H, DH, DH), const3),
                  pl.BlockSpec((1, LANES), const2), pl.BlockSpec((1, W), const2),
                  pl.BlockSpec((1, W), const2)],
        out_specs=[pl.BlockSpec((bb, valid, W), tok),
                   pl.BlockSpec((bb, MLSTM_CONV - 1, W), per_b3),
                   pl.BlockSpec((bb, H, DH, DH), per_b4),
                   pl.BlockSpec((bb, H, DH), per_b3),
                   pl.BlockSpec((bb, 1, H), per_b3)],
        out_shape=[jax.ShapeDtypeStruct((batch, seq, W), F32),
                   jax.ShapeDtypeStruct((batch, MLSTM_CONV - 1, W), F32),
                   jax.ShapeDtypeStruct((batch, H, DH, DH), F32),
                   jax.ShapeDtypeStruct((batch, H, DH), F32),
                   jax.ShapeDtypeStruct((batch, 1, H), F32)],
        scratch_shapes=[pltpu.VMEM((bb, SUBLANES + L, W), F32), pltpu.VMEM((bb, L, W), F32),
                        pltpu.VMEM((bb, L, LANES), F32)],
        compiler_params=_cparams(("arbitrary", "arbitrary")),
        name="mlstm",
    )(rows3(mu), rows3(mv), rows3(mo), rows3(gates), conv_buf, c0, n0, m0.reshape(batch, 1, H),
      w_mconv, b_mconv.reshape(1, W), w_mq.astype(BF16), w_mk.astype(BF16), gate_bias,
      g_mhead.reshape(1, W), m_skip.reshape(1, W))
    return o_m.reshape(batch * seq, W), conv_new, c_new, n_new, m_new


def _compress_rows(xk_ref, xv_ref, pe_ref, w1_ref, w2_ref, n_pairs):
    pair_rows = 2 * CMP_BLOCK
    accs = [None, None]
    for r in range(CMP_BLOCK):
        for kv, x_ref in enumerate((xk_ref, xv_ref)):
            ev = x_ref[pl.ds(r, n_pairs, stride=pair_rows), :]
            od = x_ref[pl.ds(CMP_BLOCK + r, n_pairs, stride=pair_rows), :]
            xr = jnp.concatenate([ev, od], axis=0) + pe_ref[kv, r:r + 1, :]
            w1_r = w1_ref[kv, r // 2, (r % 2) * KV_WIDTH:(r % 2 + 1) * KV_WIDTH, :]
            part = _dot(xr.astype(BF16), w1_r)
            accs[kv] = part if accs[kv] is None else accs[kv] + part
    return jnp.concatenate([_dot(_silu(accs[kv]).astype(BF16), w2_ref[kv]) for kv in range(2)], axis=1)


def _compress_body(xk_ref, xv_ref, pe_ref, w1_ref, w2_ref, oe_ref, oo_ref, *, n_pairs):
    out = _compress_rows(xk_ref, xv_ref, pe_ref, w1_ref, w2_ref, n_pairs)
    oe_ref[0] = out[0:n_pairs, :]
    oo_ref[0] = out[n_pairs:, :]


BLOCKS_PER_PAGE = PAGE_SIZE // CMP_BLOCK


def _gather_pages(pt_ref, pool_hbm, pages_ref, sem_ref, n_pages):
    g = pl.program_id(0) * pl.num_programs(1) + pl.program_id(1)
    n_total = pl.num_programs(0) * pl.num_programs(1)

    def copies(step, slot):
        return [pltpu.make_async_copy(pool_hbm.at[pt_ref[step * n_pages + j]], pages_ref.at[slot, j],
                                      sem_ref.at[slot]) for j in range(n_pages)]

    @pl.when(g == 0)
    def _():
        for cp in copies(0, 0):
            cp.start()

    @pl.when(g + 1 < n_total)
    def _():
        for cp in copies(g + 1, (g + 1) % 2):
            cp.start()

    slot = g % 2
    for cp in copies(g, slot):
        cp.wait()
    return slot


def _compress_paged_body(pt_ref, pool_hbm, pet_ref, perm_ref, w1_ref, w2_ref, oe_ref, oo_ref,
                         buf_ref, os_ref, pages_ref, sem_ref, *, n_pages):
    slot = _gather_pages(pt_ref, pool_hbm, pages_ref, sem_ref, n_pages)
    grp = 2 * BLOCKS_PER_PAGE
    for jp in range(n_pages // 2):
        xt = jnp.concatenate([pages_ref[slot, 2 * jp], pages_ref[slot, 2 * jp + 1]], axis=1)
        xb = (xt + pet_ref[...]).astype(BF16)
        xp = _dot_nt(perm_ref[...], xb)
        for r in range(CMP_BLOCK):
            for kv in range(2):
                lane0 = (2 * kv + r % 2) * KV_WIDTH
                buf_ref[r // 2, grp * jp:grp * (jp + 1), lane0:lane0 + KV_WIDTH] = (
                    xp[grp * r:grp * (r + 1), kv * KV_WIDTH:(kv + 1) * KV_WIDTH])
    accs = [None, None]
    for r2 in range(CMP_BLOCK // 2):
        for kv in range(2):
            lanes = slice(2 * kv * KV_WIDTH, 2 * (kv + 1) * KV_WIDTH)
            part = _dot(buf_ref[r2, :, lanes].astype(BF16), w1_ref[kv, r2])
            accs[kv] = part if accs[kv] is None else accs[kv] + part
    for kv in range(2):
        os_ref[kv] = _dot(_silu(accs[kv]).astype(BF16), w2_ref[kv])
    half = os_ref.shape[1] // 2
    for parity, ref in enumerate((oe_ref, oo_ref)):
        ref[0] = jnp.concatenate([os_ref[kv, pl.ds(parity, half, stride=2), :] for kv in range(2)],
                                 axis=1)


def _page_pair_constants(pe):
    pe_t = jnp.broadcast_to(pe.transpose(0, 2, 1)[:, None, :, None, :],
                            (2, NSA_KV_HEADS, HEAD_DIM, 2 * BLOCKS_PER_PAGE, CMP_BLOCK))
    pe_t = pe_t.reshape(2 * KV_WIDTH, 2 * PAGE_SIZE)
    grp = 2 * BLOCKS_PER_PAGE
    perm = np.zeros((2 * PAGE_SIZE, 2 * PAGE_SIZE), np.float32)
    for r in range(CMP_BLOCK):
        for b in range(grp):
            perm[r * grp + b, b * CMP_BLOCK + r] = 1.0
    return pe_t, jnp.asarray(perm, BF16)


def _pack_compress_weights(pe, w1, w2):
    eye_h = jnp.eye(NSA_KV_HEADS, dtype=F32)
    pe_r = jnp.broadcast_to(pe[:, :, None, :], (2, CMP_BLOCK, NSA_KV_HEADS, HEAD_DIM))
    pe_r = pe_r.reshape(2, CMP_BLOCK, KV_WIDTH)
    w1r = w1.reshape(2, CMP_BLOCK, HEAD_DIM, CMP_HIDDEN)
    w1_big = jnp.einsum('krdc,hH->krhdHc', w1r, eye_h)
    w1_big = w1_big.reshape(2, CMP_BLOCK // 2, 2 * KV_WIDTH, NSA_KV_HEADS * CMP_HIDDEN).astype(BF16)
    w2_big = jnp.einsum('kcd,hH->khcHd', w2, eye_h)
    w2_big = w2_big.reshape(2, NSA_KV_HEADS * CMP_HIDDEN, KV_WIDTH).astype(BF16)
    return pe_r, w1_big, w2_big


def _compress_prompt(k_rows, v_rows, cw, *, batch, seq):
    pe_r, w1_big, w2_big = cw
    n_pairs = seq // (2 * CMP_BLOCK)
    const3 = lambda b: (0, 0, 0)
    out_sd = jax.ShapeDtypeStruct((batch, n_pairs, 2 * KV_WIDTH), F32)
    return pl.pallas_call(
        functools.partial(_compress_body, n_pairs=n_pairs),
        grid=(batch,),
        in_specs=[pl.BlockSpec((seq, KV_WIDTH), lambda b: (b, 0)),
                  pl.BlockSpec((seq, KV_WIDTH), lambda b: (b, 0)),
                  pl.BlockSpec(pe_r.shape, const3),
                  pl.BlockSpec(w1_big.shape, lambda b: (0, 0, 0, 0)),
                  pl.BlockSpec(w2_big.shape, const3)],
        out_specs=[pl.BlockSpec((1, n_pairs, 2 * KV_WIDTH), lambda b: (b, 0, 0))] * 2,
        out_shape=[out_sd, out_sd],
        compiler_params=_cparams(("arbitrary",)),
        name="compress_prompt",
    )(k_rows, v_rows, pe_r, w1_big, w2_big)


COMPRESS_PAGES_PER_STEP = 64


def _compress_paged(pool, page_table, cw, cw_pages):
    _, w1_big, w2_big = cw
    pe_t, perm = cw_pages
    batch, n_pages = page_table.shape
    pps = COMPRESS_PAGES_PER_STEP
    assert n_pages % pps == 0 and pps % 2 == 0
    n_steps = n_pages // pps
    n_blk = pps * BLOCKS_PER_PAGE
    const3 = lambda b, c, pt: (0, 0, 0)
    return pl.pallas_call(
        functools.partial(_compress_paged_body, n_pages=pps),
        grid_spec=pltpu.PrefetchScalarGridSpec(
            num_scalar_prefetch=1,
            grid=(batch, n_steps),
            in_specs=[pl.BlockSpec(memory_space=pl.ANY),
                      pl.BlockSpec(pe_t.shape, lambda b, c, pt: (0, 0)),
                      pl.BlockSpec(perm.shape, lambda b, c, pt: (0, 0)),
                      pl.BlockSpec(w1_big.shape, lambda b, c, pt: (0, 0, 0, 0)),
                      pl.BlockSpec(w2_big.shape, const3)],
            out_specs=[pl.BlockSpec((1, n_blk // 2, 2 * KV_WIDTH), lambda b, c, pt: (b, c, 0))] * 2,
            scratch_shapes=[pltpu.VMEM((CMP_BLOCK // 2, n_blk, 4 * KV_WIDTH), F32),
                            pltpu.VMEM((2, n_blk, KV_WIDTH), F32),
                            pltpu.VMEM((2, pps, 2 * KV_WIDTH, PAGE_SIZE), F32),
                            pltpu.SemaphoreType.DMA((2,))]),
        out_shape=[jax.ShapeDtypeStruct((batch, n_steps * n_blk // 2, 2 * KV_WIDTH), F32)] * 2,
        compiler_params=_cparams(("arbitrary", "arbitrary")),
        name="compress_paged",
    )(page_table.reshape(-1), pool, pe_t, perm, w1_big, w2_big)


def _cmp_attn_body(q_ref, ke_ref, ko_ref, o_ref, st_ref, *, tq, pos0):
    ns = ke_ref.shape[1]
    i = pl.program_id(1)
    rows = NSA_GROUP * tq
    tok0 = pos0 + i * tq
    pos_r = tok0 + lax.broadcasted_iota(jnp.int32, (1, rows), 1) % tq
    pair_c = lax.broadcasted_iota(jnp.int32, (ns, 1), 0)
    end_e = (2 * pair_c + 1) * CMP_BLOCK - 1
    end_o = (2 * pair_c + 2) * CMP_BLOCK - 1
    any_r = (CMP_BLOCK - 1 <= pos_r).astype(F32)
    contract_blocks = (((0,), (0,)), ((), ()))
    q = q_ref[...] * ATTN_SCALE
    heads = range(NSA_KV_HEADS)
    te, to = {}, {}
    for kh in heads:
        qs = jnp.concatenate([q[:, (kh * NSA_GROUP + g) * HEAD_DIM:(kh * NSA_GROUP + g + 1) * HEAD_DIM]
                              for g in range(NSA_GROUP)], axis=0).astype(BF16)
        ks = slice(kh * HEAD_DIM, (kh + 1) * HEAD_DIM)
        te[kh] = jnp.where(end_e <= pos_r, _dot_nt(ke_ref[0, :, ks].astype(BF16), qs), NEG_INF)
        to[kh] = jnp.where(end_o <= pos_r, _dot_nt(ko_ref[0, :, ks].astype(BF16), qs), NEG_INF)
    pte, pto = {}, {}
    for kh in heads:
        mt = jnp.maximum(jnp.max(te[kh], axis=0, keepdims=True), jnp.max(to[kh], axis=0, keepdims=True))
        pe, po = jnp.exp(te[kh] - mt), jnp.exp(to[kh] - mt)
        invt = any_r / (jnp.sum(pe, axis=0, keepdims=True) + jnp.sum(po, axis=0, keepdims=True))
        pte[kh], pto[kh] = pe * invt, po * invt
    for kh in heads:
        vs = slice(KV_WIDTH + kh * HEAD_DIM, KV_WIDTH + (kh + 1) * HEAD_DIM)
        oh = (lax.dot_general(pte[kh].astype(BF16), ke_ref[0, :, vs].astype(BF16), contract_blocks,
                              preferred_element_type=F32)
              + lax.dot_general(pto[kh].astype(BF16), ko_ref[0, :, vs].astype(BF16), contract_blocks,
                                preferred_element_type=F32))
        for g in range(NSA_GROUP):
            hd = kh * NSA_GROUP + g
            o_ref[:, hd * HEAD_DIM:(hd + 1) * HEAD_DIM] = oh[g * tq:(g + 1) * tq, :]
        ps = pte[kh] + pto[kh]
        score = ps[:, 0:tq]
        for g in range(1, NSA_GROUP):
            score = score + ps[:, g * tq:(g + 1) * tq]
        st_ref[0, kh] = score


def _cmp_attn(q2d, kce, kco, *, batch, seq, tq, pos0):
    ns = kce.shape[1]
    nq = seq // tq
    return pl.pallas_call(
        functools.partial(_cmp_attn_body, tq=tq, pos0=pos0),
        grid=(batch, nq),
        in_specs=[pl.BlockSpec((tq, NSA_WIDTH), lambda b, i: (b * nq + i, 0)),
                  pl.BlockSpec((1, ns, 2 * KV_WIDTH), lambda b, i: (b, 0, 0)),
                  pl.BlockSpec((1, ns, 2 * KV_WIDTH), lambda b, i: (b, 0, 0))],
        out_specs=[pl.BlockSpec((tq, NSA_WIDTH), lambda b, i: (b * nq + i, 0)),
                   pl.BlockSpec((1, NSA_KV_HEADS, ns, tq), lambda b, i: (b, 0, 0, i))],
        out_shape=[jax.ShapeDtypeStruct((batch * seq, NSA_WIDTH), F32),
                   jax.ShapeDtypeStruct((batch, NSA_KV_HEADS, ns, seq), F32)],
        compiler_params=_cparams(("arbitrary", "arbitrary")),
        name="cmp_attn",
    )(q2d, kce, kco)


def _topk_body(pos_ref, st_ref, b_ref, *, n_sel):
    score = st_ref[0]
    ns, tt = score.shape
    nsw = b_ref.shape[1]
    if nsw > ns:
        score = jnp.concatenate([score, jnp.zeros((nsw - ns, tt), F32)], axis=0)
    blk = lax.broadcasted_iota(jnp.int32, (nsw, 1), 0)
    blk_f = blk.astype(F32)
    cur = pos_ref[...] // SEL_BLOCK
    forced = (blk == 0) | (blk == cur) | (blk == cur - 1)
    pri = jnp.where(blk <= cur, jnp.where(forced, SEL_PRIORITY, score), -SEL_PRIORITY)
    pri = jnp.where(blk < n_sel, pri, -jnp.inf)
    bias = jnp.full((nsw, tt), NEG_INF, F32)
    for _ in range(min(TOP_N, n_sel)):
        top = jnp.max(pri, axis=0, keepdims=True)
        first = jnp.min(jnp.where(pri == top, blk_f, float(nsw)), axis=0, keepdims=True)
        hit = blk_f == first
        bias = jnp.where(hit, 0.0, bias)
        pri = jnp.where(hit, -jnp.inf, pri)
    b_ref[0] = bias


def _topk_blocks(scores_t, pos, *, n_sel, nsw, tt):
    groups, ns, tokens = scores_t.shape
    assert nsw >= max(ns, n_sel) and tokens % tt == 0
    return pl.pallas_call(
        functools.partial(_topk_body, n_sel=n_sel),
        grid=(groups, tokens // tt),
        in_specs=[pl.BlockSpec((1, tt), lambda g, i: (0, i)),
                  pl.BlockSpec((1, ns, tt), lambda g, i: (g, 0, i))],
        out_specs=pl.BlockSpec((1, nsw, tt), lambda g, i: (g, 0, i)),
        out_shape=jax.ShapeDtypeStruct((groups, nsw, tokens), F32),
        compiler_params=_cparams(("arbitrary", "arbitrary")),
        name="topk_blocks",
    )(pos, scores_t)


def _softmax_update(sc, vt_bf16, m_ref, l_ref, acc_ref):
    m_old = m_ref[...]
    m_new = jnp.maximum(m_old, jnp.max(sc, axis=1, keepdims=True))
    alpha = jnp.exp(m_old - m_new)
    pr = jnp.exp(sc - jnp.concatenate([m_new] * (sc.shape[1] // LANES), axis=1))
    l_ref[...] = alpha * l_ref[...] + jnp.sum(pr, axis=1, keepdims=True)
    acc_ref[...] = alpha * acc_ref[...] + _dot_nt(pr.astype(BF16), vt_bf16)
    m_ref[...] = m_new


def _softmax_init(m_ref, l_ref, acc_ref):
    m_ref[...] = jnp.full(m_ref.shape, NEG_INF, F32)
    l_ref[...] = jnp.zeros(l_ref.shape, F32)
    acc_ref[...] = jnp.zeros(acc_ref.shape, F32)


ATTN_TAB_COLS = 5


def _attn_pairs(seq, tq, tk):
    rows = []
    for i in range(seq // tq):
        t_lo, t_hi = i * tq, i * tq + tq - 1
        js = list(range(0, t_hi // tk + 1))
        for n, j in enumerate(js):
            rows.append((i, j, int(n == 0), int(n == len(js) - 1), int(j * tk + tk - 1 > t_lo)))
    return np.asarray(rows, np.int32)


def _attn_body(tab_ref, q_ref, k_ref, vt_ref, oh_ref, sb_ref, o_ref, qa_ref, m_ref, l_ref, acc_ref,
               *, tq, tk):
    p = pl.program_id(1)
    i, j, first, last, partial_tile = [tab_ref[ATTN_TAB_COLS * p + n] for n in range(ATTN_TAB_COLS)]
    cols = NSA_HEADS * tq

    @pl.when(first == 1)
    def _():
        for hd, piece in enumerate(_query_columns(q_ref[0], sb_ref[0], tq)):
            qa_ref[:, hd * tq:(hd + 1) * tq] = piece
        m_ref[...] = jnp.full(m_ref.shape, NEG_INF, F32)
        l_ref[...] = jnp.zeros(l_ref.shape, F32)
        acc_ref[...] = jnp.zeros(acc_ref.shape, F32)

    k_aug = jnp.concatenate([k_ref[...], oh_ref[...]], axis=1)
    sc = _dot(k_aug, qa_ref[...])
    vt = vt_ref[0]

    def update(sc):
        m_old = m_ref[...]
        m_new = jnp.maximum(m_old, jnp.max(sc, axis=0, keepdims=True))
        alpha = jnp.exp2(m_old - m_new)
        pr = jnp.exp2(sc - m_new)
        l_ref[...] = alpha * l_ref[...] + jnp.sum(pr, axis=0, keepdims=True)
        acc_ref[...] = alpha * acc_ref[...] + _dot(vt, pr.astype(BF16))
        m_ref[...] = m_new

    @pl.when(partial_tile == 1)
    def _():
        qpos = i * tq + (lax.broadcasted_iota(jnp.int32, (1, cols), 1) & (tq - 1))
        kpos = j * tk + lax.broadcasted_iota(jnp.int32, (tk, 1), 0)
        update(jnp.where(kpos <= qpos, sc, NEG_INF))

    @pl.when(partial_tile == 0)
    def _():
        update(sc)

    @pl.when(last == 1)
    def _():
        _store_heads(acc_ref[...] / l_ref[...], o_ref, tq)


def _query_columns(q, sel_bias, tq):
    q = q * (ATTN_SCALE * LOG2_E)
    zeros64 = jnp.zeros((HEAD_DIM, tq), F32)
    kv_head_rows = lambda x, kh: jnp.concatenate([x, zeros64] if kh == 0 else [zeros64, x], axis=0)
    pieces = []
    for m in range(NSA_HEADS // 2):
        q_t = q[:, m * LANES:(m + 1) * LANES].T
        for hd in (2 * m, 2 * m + 1):
            kh = hd // NSA_GROUP
            piece = kv_head_rows(q_t[(hd % 2) * HEAD_DIM:(hd % 2 + 1) * HEAD_DIM, :], kh)
            if sel_bias is not None:
                piece = jnp.concatenate([piece, kv_head_rows(sel_bias[kh], kh)], axis=0)
            pieces.append(piece.astype(BF16))
    return pieces


def _store_heads(o_t, o_ref, tq):
    for m in range(NSA_HEADS // 2):
        pair = jnp.concatenate(
            [o_t[(hd // NSA_GROUP) * HEAD_DIM:(hd // NSA_GROUP + 1) * HEAD_DIM, hd * tq:(hd + 1) * tq]
             for hd in (2 * m, 2 * m + 1)], axis=0)
        o_ref[0, :, m * LANES:(m + 1) * LANES] = pair.T


def _window_body(q_ref, *refs, tq, n_tiles):
    k_refs, v_refs, o_ref = refs[:n_tiles], refs[n_tiles:2 * n_tiles], refs[2 * n_tiles]
    i = pl.program_id(1)
    cols = NSA_HEADS * tq
    qa = jnp.concatenate(_query_columns(q_ref[0], None, tq), axis=1)
    qpos = i * tq + (lax.broadcasted_iota(jnp.int32, (1, cols), 1) & (tq - 1))
    scs = []
    for n, k_ref in enumerate(k_refs):
        kpos = (i - (n_tiles - 1) + n) * tq + lax.broadcasted_iota(jnp.int32, (tq, 1), 0)
        if n == n_tiles - 1:
            valid = kpos <= qpos
        elif n == 0:
            valid = (kpos > qpos - WINDOW) & (kpos >= 0)
        else:
            valid = kpos >= 0
        scs.append(jnp.where(valid, _dot(k_ref[...], qa), NEG_INF))
    mx = scs[0].max(axis=0, keepdims=True)
    for sc in scs[1:]:
        mx = jnp.maximum(mx, sc.max(axis=0, keepdims=True))
    l_sum, acc = None, None
    for sc, v_ref in zip(scs, v_refs):
        pr = jnp.exp2(sc - mx)
        part_l, part_acc = jnp.sum(pr, axis=0, keepdims=True), _dot(v_ref[0], pr.astype(BF16))
        l_sum = part_l if l_sum is None else l_sum + part_l
        acc = part_acc if acc is None else acc + part_acc
    _store_heads(acc / l_sum, o_ref, tq)


def _attn_window_prompt(q3d, k_rows, v_t, *, tq):
    batch, seq, _ = q3d.shape
    assert WINDOW % tq == 0 and seq % tq == 0 and tq & (tq - 1) == 0
    n_tiles = WINDOW // tq + 1
    nq = seq // tq
    tile = lambda n: (lambda i: jnp.maximum(i - (n_tiles - 1) + n, 0))
    return pl.pallas_call(
        functools.partial(_window_body, tq=tq, n_tiles=n_tiles),
        grid=(batch, nq),
        in_specs=[pl.BlockSpec((1, tq, NSA_WIDTH), lambda b, i: (b, i, 0))]
        + [pl.BlockSpec((tq, KV_WIDTH), lambda b, i, t=tile(n): (b * nq + t(i), 0)) for n in range(n_tiles)]
        + [pl.BlockSpec((1, KV_WIDTH, tq), lambda b, i, t=tile(n): (b, 0, t(i))) for n in range(n_tiles)],
        out_specs=pl.BlockSpec((1, tq, NSA_WIDTH), lambda b, i: (b, i, 0)),
        out_shape=jax.ShapeDtypeStruct((batch, seq, NSA_WIDTH), F32),
        compiler_params=_cparams(("arbitrary", "arbitrary")),
        name="attn_win",
    )(q3d, *([k_rows] * n_tiles), *([v_t] * n_tiles))


def _block_onehot(seq):
    blk = np.arange(seq)[:, None] // SEL_BLOCK
    return jnp.asarray((np.arange(LANES)[None, :] % SEL_BLOCK) == blk, BF16)


def _attn_selected_prompt(q3d, k_rows, v_t, selb, *, tq, tk):
    batch, seq, _ = q3d.shape
    assert tq & (tq - 1) == 0 and tk % LANES == 0 and tq % LANES == 0 and selb.shape[2] == SEL_BLOCK
    tab = _attn_pairs(seq, tq, tk)
    cols = NSA_HEADS * tq
    C = ATTN_TAB_COLS
    nk = seq // tk
    return pl.pallas_call(
        functools.partial(_attn_body, tq=tq, tk=tk),
        grid_spec=pltpu.PrefetchScalarGridSpec(
            num_scalar_prefetch=1,
            grid=(batch, tab.shape[0]),
            in_specs=[pl.BlockSpec((1, tq, NSA_WIDTH), lambda b, p, t: (b, t[C * p], 0)),
                      pl.BlockSpec((tk, KV_WIDTH), lambda b, p, t: (b * nk + t[C * p + 1], 0)),
                      pl.BlockSpec((1, KV_WIDTH, tk), lambda b, p, t: (b, 0, t[C * p + 1])),
                      pl.BlockSpec((tk, LANES), lambda b, p, t: (t[C * p + 1], 0)),
                      pl.BlockSpec((1, NSA_KV_HEADS, SEL_BLOCK, tq),
                                   lambda b, p, t: (b, 0, 0, t[C * p]))],
            out_specs=pl.BlockSpec((1, tq, NSA_WIDTH), lambda b, p, t: (b, t[C * p], 0)),
            scratch_shapes=[pltpu.VMEM((2 * LANES, cols), BF16), pltpu.VMEM((1, cols), F32),
                            pltpu.VMEM((1, cols), F32), pltpu.VMEM((KV_WIDTH, cols), F32)]),
        out_shape=jax.ShapeDtypeStruct((batch, seq, NSA_WIDTH), F32),
        compiler_params=_cparams(("arbitrary", "arbitrary")),
        name="attn_sel",
    )(jnp.asarray(tab.reshape(-1)), q3d, k_rows, v_t, _block_onehot(seq), selb)


ATTN_PAGES_PER_STEP = 64
ATTN_PAGED_SPLIT = 2


def _attn_paged_body(pt_ref, qa_ref, bq_ref, bn_ref, kn_ref, oh_ref, pool_hbm, o_ref, m_ref, l_ref,
                     acc_ref, pages_ref, sem_ref, *, n_pages, n_new):
    slot = _gather_pages(pt_ref, pool_hbm, pages_ref, sem_ref, n_pages)
    page_refs = [pages_ref.at[slot, j] for j in range(n_pages)]
    c = pl.program_id(1)
    rows = qa_ref.shape[1]

    @pl.when(c == 0)
    def _():
        _softmax_init(m_ref, l_ref, acc_ref)

    n_split = m_ref.shape[0]
    per = n_pages // n_split
    keys = per * PAGE_SIZE
    qa = qa_ref[0]
    bias = bq_ref[0, 0]
    blocks = keys // SEL_BLOCK
    lane = lax.broadcasted_iota(jnp.int32, (1, LANES), 1)
    scs, vts = [], []
    for s in range(n_split):
        refs_s = page_refs[s * per:(s + 1) * per]
        bias_s = bias if s == 0 else pltpu.roll(bias, LANES - s * blocks, axis=1)
        lhs = jnp.concatenate([qa, jnp.where(lane < blocks, bias_s, 0.0)], axis=1).astype(BF16)
        kt = jnp.concatenate([r[0:KV_WIDTH, :] for r in refs_s], axis=1)
        rhs = jnp.concatenate([kt.astype(BF16), oh_ref[...]], axis=0)
        scs.append(_dot(lhs, rhs))
        vts.append(jnp.concatenate([r[KV_WIDTH:, :] for r in refs_s], axis=1).astype(BF16))
    for s in range(n_split):
        _softmax_update(scs[s], vts[s], m_ref.at[s], l_ref.at[s], acc_ref.at[s])

    @pl.when(c == pl.num_programs(1) - 1)
    def _():
        kn = kn_ref[0]
        sc = _dot(qa.astype(BF16), kn[0:KV_WIDTH, :].astype(BF16)) + bn_ref[0]
        tq = lax.broadcasted_iota(jnp.int32, (rows, 1), 0) % n_new
        kk = lax.broadcasted_iota(jnp.int32, (1, kn.shape[1]), 1)
        sc = jnp.where((kk <= tq) & (kk < n_new), sc, NEG_INF)
        _softmax_update(sc, kn[KV_WIDTH:, :].astype(BF16), m_ref.at[0], l_ref.at[0], acc_ref.at[0])
        m_all = m_ref[0]
        for s in range(1, n_split):
            m_all = jnp.maximum(m_all, m_ref[s])
        l_all = jnp.zeros(m_all.shape, F32)
        acc_all = jnp.zeros(m_all.shape, F32)
        for s in range(n_split):
            scale = jnp.exp(m_ref[s] - m_all)
            l_all = l_all + scale * l_ref[s]
            acc_all = acc_all + scale * acc_ref[s]
        o_ref[0] = acc_all / l_all


def _attn_paged(qa, bias_q, bias_new, kv_new_t, pool, page_table, *, n_new):
    batch, n_pages = page_table.shape
    pps = ATTN_PAGES_PER_STEP
    keys_per_chain = pps // ATTN_PAGED_SPLIT * PAGE_SIZE
    assert n_pages % pps == 0 and pps * PAGE_SIZE // SEL_BLOCK <= LANES
    assert keys_per_chain // SEL_BLOCK <= SEL_BLOCK
    n_steps = n_pages // pps
    rows = qa.shape[1]

    per_b = lambda b, c, pt: (b, 0, 0)
    return pl.pallas_call(
        functools.partial(_attn_paged_body, n_pages=pps, n_new=n_new),
        grid_spec=pltpu.PrefetchScalarGridSpec(
            num_scalar_prefetch=1,
            grid=(batch, n_steps),
            in_specs=[pl.BlockSpec((1, rows, LANES), per_b),
                      pl.BlockSpec((1, 1, rows, LANES), lambda b, c, pt: (b, c, 0, 0)),
                      pl.BlockSpec((1, rows, LANES), per_b),
                      pl.BlockSpec((1,) + kv_new_t.shape[1:], per_b),
                      pl.BlockSpec((LANES, keys_per_chain), lambda b, c, pt: (0, 0)),
                      pl.BlockSpec(memory_space=pl.ANY)],
            out_specs=pl.BlockSpec((1, rows, LANES), per_b),
            scratch_shapes=[pltpu.VMEM((ATTN_PAGED_SPLIT, rows, LANES), F32)] * 3
            + [pltpu.VMEM((2, pps, 2 * KV_WIDTH, PAGE_SIZE), F32), pltpu.SemaphoreType.DMA((2,))]),
        out_shape=jax.ShapeDtypeStruct((batch, rows, LANES), F32),
        compiler_params=_cparams(("arbitrary", "arbitrary")),
        name="attn_sel_paged",
    )(page_table.reshape(-1), qa, bias_q, bias_new, kv_new_t, _block_onehot(keys_per_chain).T, pool)


def _attn_window_body(qa_ref, wb_ref, kn_ref, o_ref, *, n_new, past):
    qa = qa_ref[0].astype(BF16)
    wb, kn = wb_ref[0], kn_ref[0]
    rows, n_buf = qa.shape[0], wb.shape[1]
    qpos = past + lax.broadcasted_iota(jnp.int32, (rows, 1), 0) % n_new

    def masked(sc, kpos, extra):
        diff = qpos - kpos
        return jnp.where((diff >= 0) & (diff < WINDOW) & (kpos >= 0) & extra, sc, NEG_INF)

    nb = lax.broadcasted_iota(jnp.int32, (1, n_buf), 1)
    nn = lax.broadcasted_iota(jnp.int32, (1, kn.shape[1]), 1)
    sb = masked(_dot(qa, wb[0:KV_WIDTH, :].astype(BF16)), past - n_buf + nb, nb >= 0)
    sn = masked(_dot(qa, kn[0:KV_WIDTH, :].astype(BF16)), past + nn, nn < n_new)
    mx = jnp.maximum(jnp.max(sb, axis=1, keepdims=True), jnp.max(sn, axis=1, keepdims=True))
    pb, pn = jnp.exp(sb - mx), jnp.exp(sn - mx)
    o = (_dot_nt(pb.astype(BF16), wb[KV_WIDTH:, :].astype(BF16))
         + _dot_nt(pn.astype(BF16), kn[KV_WIDTH:, :].astype(BF16)))
    o_ref[0] = o / (jnp.sum(pb, axis=1, keepdims=True) + jnp.sum(pn, axis=1, keepdims=True))


def _attn_window_small(qa, win_t, kv_new_t, *, n_new, past):
    batch, rows, _ = qa.shape
    per_b = lambda b: (b, 0, 0)
    return pl.pallas_call(
        functools.partial(_attn_window_body, n_new=n_new, past=past),
        grid=(batch,),
        in_specs=[pl.BlockSpec((1, rows, LANES), per_b),
                  pl.BlockSpec((1,) + win_t.shape[1:], per_b),
                  pl.BlockSpec((1,) + kv_new_t.shape[1:], per_b)],
        out_specs=pl.BlockSpec((1, rows, LANES), per_b),
        out_shape=jax.ShapeDtypeStruct((batch, rows, LANES), F32),
        compiler_params=_cparams(("arbitrary",)),
        name="attn_win_small",
    )(qa, win_t, kv_new_t)


FFN_TM = 512


def _ffn_vmem_bytes(tm, halo):
    weights = 2 * (D_MODEL * D_MODEL + D_MODEL * 2 * D_FF + D_FF * D_MODEL)
    conv_buffer = 4 * (halo + tm) * 2 * D_FF
    row_tiles = 2 * 4 * tm * (2 * D_MODEL + 4 * NSA_WIDTH + LANES)
    temporaries = 3 * 4 * tm * max(hi - lo for lo, hi in FFN_CHUNKS)
    return weights + conv_buffer + row_tiles + temporaries
MXU_DEPTH = 256
FFN_CHUNKS = ((0, 6 * MXU_DEPTH), (6 * MXU_DEPTH, D_FF))


def _ffn_body(x_ref, om_ref, oc_ref, os_ref, ow_ref, gt_ref, ge_ref, gn_ref, gf_ref, gl_ref, wc_ref,
              fb_ref, wo_hbm, wu_hbm, wd_hbm, y_ref, fn_ref, xx_ref, wo_ref, wu_ref, wd_ref, sem_ref,
              *, tm, stride, halo):
    s = pl.program_id(1)

    @pl.when((pl.program_id(0) == 0) & (s == 0))
    def _():
        copies = [pltpu.make_async_copy(src, dst, sem_ref.at[n])
                  for n, (src, dst) in enumerate(((wo_hbm, wo_ref), (wu_hbm, wu_ref), (wd_hbm, wd_ref)))]
        for cp in copies:
            cp.start()
        for cp in copies:
            cp.wait()

    sig = _sigmoid(gt_ref[...])
    hi = sig.astype(BF16)
    lo = (sig - hi.astype(F32)).astype(BF16)
    comb = None
    for br, ob_ref in enumerate((oc_ref, os_ref, ow_ref)):
        gate = _dot(hi, ge_ref[br]) + _dot(lo, ge_ref[br])
        term = gate * ob_ref[...]
        comb = term if comb is None else comb + term
    onsa = _rms(comb, gn_ref[...])
    h = (x_ref[...] + _dot(om_ref[...].astype(BF16), wo_ref[0:MLSTM_WIDTH, :])
         + _dot(onsa.astype(BF16), wo_ref[MLSTM_WIDTH:, :]))
    hn = _rms(h, gf_ref[...]).astype(BF16)

    base = halo - (FFN_CONV - 1) * stride

    @pl.when(s == 0)
    def _():
        xx_ref[base:halo, :] = fb_ref[0]

    y_ref[...] = h

    def up_project(chunk):
        for half in range(2):
            cols = slice(half * D_FF + chunk[0], half * D_FF + chunk[1])
            xx_ref[halo:halo + tm, cols] = _dot(hn, wu_ref[:, cols])

    up_project(FFN_CHUNKS[0])
    for n, (lo_col, hi_col) in enumerate(FFN_CHUNKS):
        if n + 1 < len(FFN_CHUNKS):
            up_project(FFN_CHUNKS[n + 1])
        convs = []
        for half in range(2):
            cols = slice(half * D_FF + lo_col, half * D_FF + hi_col)
            conv = xx_ref[base:base + tm, cols] * wc_ref[0:1, cols]
            for j in range(1, FFN_CONV):
                conv = conv + xx_ref[base + j * stride:base + j * stride + tm, cols] * wc_ref[j:j + 1, cols]
            convs.append(conv)
        act = _silu(convs[1]) * convs[0]
        y_ref[...] += _dot(act.astype(BF16), wd_ref[lo_col:hi_col, :])
    fn_ref[0, 0] = xx_ref[tm + base:tm + halo, :]
    xx_ref[0:halo, :] = xx_ref[tm:tm + halo, :]
    y_ref[...] = _rms(y_ref[...], gl_ref[...])


def _gate_expand():
    ge = np.zeros((N_BRANCH, LANES, NSA_WIDTH), np.float32)
    for hd in range(NSA_HEADS):
        for br in range(N_BRANCH):
            ge[br, GATE_COL_NSA + hd * N_BRANCH + br, hd * HEAD_DIM:(hd + 1) * HEAD_DIM] = 1.0
    return jnp.asarray(ge, BF16)


def _ffn(x2d, om, oc, osel, ow, gt, fbuf, w_out, g_nsa, g_ffn, g_final, w_up, w_fconv, w_down,
         *, nb, tm, stride):
    rows = x2d.shape[0]
    ns = rows // (nb * tm)
    halo = -(-(FFN_CONV - 1) * stride // SUBLANES) * SUBLANES
    assert tm >= halo and all((hi - lo) % MXU_DEPTH == 0 for lo, hi in FFN_CHUNKS)
    vmem_limit = _ffn_vmem_bytes(tm, halo)
    assert vmem_limit <= VMEM_BYTES
    tok = lambda b, s: (b * ns + s, 0)
    nfb = (FFN_CONV - 1) * stride

    def const(shape):
        return pl.BlockSpec(shape, lambda b, s: (0,) * len(shape))

    hbm = pl.BlockSpec(memory_space=pl.ANY)
    y, fn = pl.pallas_call(
        functools.partial(_ffn_body, tm=tm, stride=stride, halo=halo),
        grid=(nb, ns),
        in_specs=[pl.BlockSpec((tm, D_MODEL), tok)] + [pl.BlockSpec((tm, NSA_WIDTH), tok)] * 4
        + [pl.BlockSpec((tm, LANES), tok),
           const((N_BRANCH, LANES, NSA_WIDTH)), const((1, NSA_WIDTH)), const((1, D_MODEL)),
           const((1, D_MODEL)), const((FFN_CONV, 2 * D_FF)),
           pl.BlockSpec((1, nfb, 2 * D_FF), lambda b, s: (b, 0, 0)), hbm, hbm, hbm],
        out_specs=[pl.BlockSpec((tm, D_MODEL), tok),
                   pl.BlockSpec((1, 1, nfb, 2 * D_FF), lambda b, s: (b, s, 0, 0))],
        out_shape=[jax.ShapeDtypeStruct((rows, D_MODEL), F32),
                   jax.ShapeDtypeStruct((nb, ns, nfb, 2 * D_FF), F32)],
        scratch_shapes=[pltpu.VMEM((halo + tm, 2 * D_FF), F32),
                        pltpu.VMEM((D_MODEL, D_MODEL), BF16), pltpu.VMEM((D_MODEL, 2 * D_FF), BF16),
                        pltpu.VMEM((D_FF, D_MODEL), BF16), pltpu.SemaphoreType.DMA((3,))],
        compiler_params=pltpu.CompilerParams(dimension_semantics=("arbitrary", "arbitrary"),
                                             vmem_limit_bytes=vmem_limit),
        name="outproj_ffn",
    )(x2d, om, oc, osel, ow, gt, _gate_expand(), g_nsa.reshape(1, -1), g_ffn.reshape(1, -1),
      g_final.reshape(1, -1), w_fconv, fbuf, w_out.astype(BF16), w_up.astype(BF16),
      w_down.astype(BF16))
    return y, fn[:, ns - 1]


PROMPT_TM = 512
PROMPT_TQ_CMP = 512
PROMPT_TT_TOPK = 1024
PROMPT_TQ_SEL = 512
PROMPT_TK_SEL = 512
PROMPT_TQ_WIN = 256


def _kv_rows(kv_t):
    batch, _, rows = kv_t.shape
    return kv_t.reshape(batch, 2, NSA_KV_HEADS, HEAD_DIM, rows).transpose(0, 4, 1, 2, 3)


def _kv_feature_major(kv5):
    batch, rows = kv5.shape[:2]
    return kv5.transpose(0, 2, 3, 4, 1).reshape(batch, 2 * KV_WIDTH, rows)


def _prompt_layer(x, wts):
    batch, seq, _ = x.shape
    x2d = x.reshape(batch * seq, D_MODEL)
    q, kc_rows, vc_rows, mu, mv, mo, gt, ks_rows, kw_rows, kvc_t, kvs_t, kvw_t, vs_t, vw_t = _in_proj(
        x2d, wts["g_mix"], wts["w_in_packed"], batch=batch, seq=seq, tm=min(PROMPT_TM, seq))
    H, DH, W = MLSTM_HEADS, MLSTM_DH, MLSTM_WIDTH
    o_m, mconv, c_new, n_new, m_new = _mlstm(
        mu, mv, mo, gt, jnp.zeros((batch, MLSTM_CONV - 1, W), F32), jnp.zeros((batch, H, DH, DH), F32),
        jnp.zeros((batch, H, DH), F32), jnp.zeros((batch, H), F32),
        wts["w_mconv"], wts["b_mconv"], wts["w_mq"], wts["w_mk"], wts["b_ig"], wts["b_fg"],
        wts["g_mhead"], wts["m_skip"], batch=batch, seq=seq)
    kce, kco = _compress_prompt(kc_rows, vc_rows, wts["cw"], batch=batch, seq=seq)
    n_sel = -(-seq // SEL_BLOCK)
    assert n_sel <= SEL_BLOCK
    o_cmp, scores_t = _cmp_attn(q, kce, kco, batch=batch, seq=seq, tq=min(PROMPT_TQ_CMP, seq), pos0=0)
    selb = _topk_blocks(scores_t.reshape(batch * NSA_KV_HEADS, -1, seq),
                        jnp.arange(seq, dtype=jnp.int32).reshape(1, seq),
                        n_sel=n_sel, nsw=SEL_BLOCK, tt=min(PROMPT_TT_TOPK, seq))
    selb = selb.reshape(batch, NSA_KV_HEADS, SEL_BLOCK, seq)
    q3d = q.reshape(batch, seq, NSA_WIDTH)
    o_sel = _attn_selected_prompt(q3d, ks_rows, vs_t, selb, tq=min(PROMPT_TQ_SEL, seq),
                                  tk=min(PROMPT_TK_SEL, seq))
    o_win = _attn_window_prompt(q3d, kw_rows, vw_t, tq=PROMPT_TQ_WIN)
    fbuf = jnp.zeros((batch, FFN_CONV - 1, 2 * D_FF), F32)
    y, f_new = _ffn(x2d, o_m, o_cmp, o_sel.reshape(-1, NSA_WIDTH), o_win.reshape(-1, NSA_WIDTH), gt,
                    fbuf, wts["w_out"], wts["g_nsa"], wts["g_ffn"], wts["g_final"], wts["w_up"],
                    wts["w_fconv"], wts["w_down"], nb=batch, tm=min(FFN_TM, seq), stride=1)
    n_win = min(WINDOW, seq)
    return (y.reshape(batch, seq, D_MODEL), _kv_rows(kvc_t), _kv_rows(kvs_t),
            _kv_rows(kvw_t[:, :, seq - n_win:]), mconv, c_new, n_new, m_new.reshape(batch, H), f_new)


def _decode_rows(q2d, batch, seq):
    q5 = (q2d * ATTN_SCALE).reshape(batch, seq, NSA_KV_HEADS, NSA_GROUP, HEAD_DIM).transpose(0, 2, 3, 1, 4)
    eye = jnp.eye(NSA_KV_HEADS, dtype=F32)
    qa = jnp.einsum('bkgtd,kK->bkgtKd', q5, eye)
    return qa.reshape(batch, NSA_KV_HEADS * NSA_GROUP * seq, KV_WIDTH)


def _decode_rows_out(o, batch, seq):
    o6 = o.reshape(batch, NSA_KV_HEADS, NSA_GROUP, seq, NSA_KV_HEADS, HEAD_DIM)
    o5 = jnp.stack([o6[:, kh, :, :, kh, :] for kh in range(NSA_KV_HEADS)], axis=1)
    return o5.transpose(0, 3, 1, 2, 4).reshape(batch * seq, NSA_WIDTH)


def _sample_layer(x, pool_cmp, pool_sel, win_buf, m_conv, m_c, m_n, m_m, f_buf, page_table, wts):
    batch, seq, _ = x.shape
    n_pages = page_table.shape[1]
    past = n_pages * PAGE_SIZE
    assert past % SEL_BLOCK == 0 and seq <= SEL_BLOCK and seq < CMP_BLOCK
    x2d = x.reshape(batch * seq, D_MODEL)
    q, _, _, mu, mv, mo, gt, _, _, kvc_t, kvs_t, kvw_t, _, _ = _in_proj(
        x2d, wts["g_mix"], wts["w_in_packed"], batch=1, seq=batch * seq, tm=batch * seq)
    per_batch = lambda a: a.reshape(2 * KV_WIDTH, batch, seq).transpose(1, 0, 2)
    kvc_t, kvs_t, kvw_t = per_batch(kvc_t), per_batch(kvs_t), per_batch(kvw_t)
    pad_keys = lambda a: jnp.pad(a, ((0, 0), (0, 0), (0, LANES - seq)))
    H = MLSTM_HEADS
    o_m, mconv, c_new, n_new, m_new = _mlstm(
        mu, mv, mo, gt, m_conv, m_c, m_n, m_m,
        wts["w_mconv"], wts["b_mconv"], wts["w_mq"], wts["w_mk"], wts["b_ig"], wts["b_fg"],
        wts["g_mhead"], wts["m_skip"], batch=batch, seq=seq)
    pool_cmp3, pool_sel3 = _kv_feature_major(pool_cmp), _kv_feature_major(pool_sel)
    kce, kco = _compress_paged(pool_cmp3, page_table, wts["cw"], wts["cw_pages"])
    n_past_blk = past // SEL_BLOCK
    n_sel = -(-(past + seq) // SEL_BLOCK)
    o_cmp, scores_t = _cmp_attn(q, kce, kco, batch=batch, seq=seq, tq=seq, pos0=past)
    ns = scores_t.shape[2]
    nsw = ns + LANES
    scores_all = scores_t.transpose(1, 2, 0, 3).reshape(NSA_KV_HEADS, ns, batch * seq)
    pos_all = (past + jnp.arange(batch * seq, dtype=jnp.int32) % seq).reshape(1, batch * seq)
    selb = _topk_blocks(scores_all, pos_all, n_sel=n_sel, nsw=nsw, tt=batch * seq)
    selb = selb.reshape(NSA_KV_HEADS, nsw, batch, seq).transpose(2, 0, 3, 1)
    qa = _decode_rows(q, batch, seq)
    rows = qa.shape[1]
    blk_per_step = ATTN_PAGES_PER_STEP * PAGE_SIZE // SEL_BLOCK
    n_steps = n_pages // ATTN_PAGES_PER_STEP
    sb_rows = jnp.broadcast_to(selb[:, :, None], (batch, NSA_KV_HEADS, NSA_GROUP, seq, selb.shape[-1]))
    sb_rows = sb_rows.reshape(batch, rows, selb.shape[-1])
    bias_q = sb_rows[:, :, :n_past_blk].reshape(batch, rows, n_steps, blk_per_step).transpose(0, 2, 1, 3)
    bias_q = jnp.pad(bias_q, ((0, 0), (0, 0), (0, 0), (0, LANES - blk_per_step)))
    bias_new = jnp.broadcast_to(sb_rows[:, :, n_past_blk:n_past_blk + 1], (batch, rows, LANES))
    o_sel = _attn_paged(qa, bias_q, bias_new, pad_keys(kvs_t), pool_sel3, page_table, n_new=seq)
    n_buf = win_buf.shape[1]
    assert past >= n_buf
    win_t = _kv_feature_major(win_buf)
    o_win = _attn_window_small(qa, win_t, pad_keys(kvw_t), n_new=seq, past=past)
    win_new = jnp.concatenate([win_t, kvw_t], axis=2)[:, :, seq:]
    tmaj = lambda a: a.reshape(batch, seq, -1).transpose(1, 0, 2).reshape(batch * seq, -1)
    fb_t = f_buf.transpose(1, 0, 2).reshape(1, (FFN_CONV - 1) * batch, 2 * D_FF)
    y, f_new = _ffn(tmaj(x2d), tmaj(o_m), tmaj(o_cmp), tmaj(_decode_rows_out(o_sel, batch, seq)),
                    tmaj(_decode_rows_out(o_win, batch, seq)), tmaj(gt), fb_t,
                    wts["w_out"], wts["g_nsa"], wts["g_ffn"], wts["g_final"], wts["w_up"],
                    wts["w_fconv"], wts["w_down"], nb=1, tm=batch * seq, stride=batch)
    y = y.reshape(seq, batch, D_MODEL).transpose(1, 0, 2)
    f_new = f_new.reshape(FFN_CONV - 1, batch, 2 * D_FF).transpose(1, 0, 2)
    return (y, _kv_rows(kvc_t), _kv_rows(kvs_t), _kv_rows(win_new), mconv, c_new, n_new,
            m_new.reshape(batch, H), f_new)


def kernel(x_prompt, x_sample, cache_cmp, cache_sel, state_win, state_mlstm_C, state_mlstm_n,
           state_mlstm_m, state_mlstm_conv, state_ffn_conv, page_table,
           g_mix, w_in, w_out, w_mconv, b_mconv, w_mq, w_mk, b_ig, b_fg, g_mhead, m_skip,
           pe_cmp, w_cmp1, w_cmp2, g_nsa, g_ffn, w_up, w_fconv, w_down, g_final):
    assert w_in.shape[0] == 1, "one layer: the final norm is fused into the layer's FFN kernel"
    l = 0
    wts = dict(g_mix=g_mix[l], w_in_packed=_pack_w_in(w_in[l]), w_out=w_out[l], w_mconv=w_mconv[l],
               b_mconv=b_mconv[l], w_mq=w_mq[l], w_mk=w_mk[l], b_ig=b_ig[l], b_fg=b_fg[l],
               g_mhead=g_mhead[l], m_skip=m_skip[l],
               cw=_pack_compress_weights(pe_cmp[l], w_cmp1[l], w_cmp2[l]),
               cw_pages=_page_pair_constants(pe_cmp[l]),
               g_nsa=g_nsa[l], g_ffn=g_ffn[l], g_final=g_final, w_up=w_up[l], w_fconv=w_fconv[l],
               w_down=w_down[l])
    p = _prompt_layer(x_prompt, wts)
    s = _sample_layer(x_sample, cache_cmp[l], cache_sel[l], state_win[l], state_mlstm_conv[l],
                      state_mlstm_C[l], state_mlstm_n[l], state_mlstm_m[l], state_ffn_conv[l],
                      page_table, wts)
    yp, cmp_p, sel_p, win_p, mconv_p, c_p, n_p, m_p, fconv_p = p
    ys, cmp_s, sel_s, win_s, mconv_s, c_s, n_s, m_s, fconv_s = s
    st = lambda a: a[None]
    return (yp, ys, st(cmp_p), st(cmp_s), st(sel_p), st(sel_s), st(win_p), st(win_s),
            st(c_p), st(c_s), st(n_p), st(n_s), st(m_p), st(m_s), st(mconv_p), st(mconv_s),
            st(fconv_p), st(fconv_s))
```

```python
import functools

import numpy as np
import jax
import jax.numpy as jnp
from jax import lax
from jax.experimental import pallas as pl
from jax.experimental.pallas import tpu as pltpu

F32 = jnp.float32
BF16 = jnp.bfloat16

D_MODEL = 1024
PAGE_SIZE = 128
HEAD_DIM = 64
NSA_HEADS = 8
NSA_KV_HEADS = 2
NSA_GROUP = NSA_HEADS // NSA_KV_HEADS
NSA_WIDTH = NSA_HEADS * HEAD_DIM
KV_WIDTH = NSA_KV_HEADS * HEAD_DIM
CMP_BLOCK = 32
CMP_HIDDEN = 2 * HEAD_DIM
SEL_BLOCK = 64
TOP_N = 16
WINDOW = 512
N_BRANCH = 3
ATTN_SCALE = HEAD_DIM ** -0.5
MLSTM_HEADS = 4
MLSTM_WIDTH = D_MODEL - NSA_WIDTH
MLSTM_DH = MLSTM_WIDTH // MLSTM_HEADS
MLSTM_CONV = 4
D_FF = ((8 * D_MODEL // 3 + 127) // 128) * 128
FFN_CONV = 3
EPS = 1e-6
NEG_INF = -1e30
SEL_PRIORITY = 1e4
LOG2_E = 1.4426950408889634

LANES = 128
SUBLANES = 8
VMEM_BYTES = 64 * 1024 * 1024
VMEM_LIMIT = 3 * VMEM_BYTES // 4

GATE_COL_NSA = 0
GATE_COL_I = NSA_HEADS * N_BRANCH
GATE_COL_F = GATE_COL_I + MLSTM_HEADS

MLSTM_CHUNK = 128
MLSTM_SEQS_PER_STEP = 4


def _cparams(sem):
    return pltpu.CompilerParams(dimension_semantics=sem, vmem_limit_bytes=VMEM_LIMIT)


def _dot(a, b):
    return jnp.dot(a, b, preferred_element_type=F32)


def _dot_nt(a, b):
    return lax.dot_general(a, b, (((1,), (1,)), ((), ())), preferred_element_type=F32)


def _sigmoid(x):
    return 1.0 / (1.0 + jnp.exp(-x))


def _silu(x):
    return x * _sigmoid(x)


def _rms(x, g):
    return x * lax.rsqrt(jnp.mean(x * x, axis=-1, keepdims=True) + EPS) * g


IN_ROW_WIDTHS = (NSA_WIDTH, KV_WIDTH, KV_WIDTH, MLSTM_WIDTH, MLSTM_WIDTH, MLSTM_WIDTH, LANES,
                 KV_WIDTH, KV_WIDTH)
IN_ROW_DTYPES = (F32,) * 7 + (BF16,) * 2
N_KV_BRANCH = 3


def _inproj_body(x_ref, g_ref, w_ref, wt_ref, *out_refs):
    xb = _rms(x_ref[...], g_ref[...]).astype(BF16)
    off = 0
    n_rows = len(IN_ROW_WIDTHS)
    for ref in out_refs[:n_rows]:
        n = ref.shape[-1]
        ref[...] = _dot(xb, w_ref[:, off:off + n]).astype(ref.dtype)
        off += n
    kv_refs = out_refs[n_rows:n_rows + N_KV_BRANCH]
    vt_refs = out_refs[n_rows + N_KV_BRANCH:]
    for n, ref in enumerate(kv_refs):
        kv_t = _dot_nt(wt_ref[n * 2 * KV_WIDTH:(n + 1) * 2 * KV_WIDTH, :], xb)
        ref[0] = kv_t
        if n > 0:
            vt_refs[n - 1][0] = kv_t[KV_WIDTH:, :].astype(BF16)


def _pack_w_in(w_in):
    splits = np.cumsum([NSA_WIDTH, 2 * KV_WIDTH, 2 * KV_WIDTH, 2 * KV_WIDTH, NSA_HEADS * N_BRANCH,
                        MLSTM_WIDTH, MLSTM_WIDTH, MLSTM_WIDTH, MLSTM_HEADS]).tolist()
    q, kvc, kvs, kvw, gt, mu, mv, mo, mi, mf = jnp.split(w_in, splits, axis=1)
    gates = jnp.concatenate([gt, mi, mf], axis=1)
    gates = jnp.pad(gates, ((0, 0), (0, LANES - gates.shape[1])))
    w_rows = jnp.concatenate([q, kvc, mu, mv, mo, gates, kvs[:, :KV_WIDTH], kvw[:, :KV_WIDTH]],
                             axis=1).astype(BF16)
    w_kv_t = jnp.concatenate([kvc, kvs, kvw], axis=1).T.astype(BF16)
    return w_rows, w_kv_t


def _in_proj(x2d, g_mix, w_packed, *, batch, seq, tm):
    w_rows, w_kv_t = w_packed
    t = x2d.shape[0]
    ns = seq // tm
    kv_sd = jax.ShapeDtypeStruct((batch, 2 * KV_WIDTH, seq), F32)
    vt_sd = jax.ShapeDtypeStruct((batch, KV_WIDTH, seq), BF16)
    feat_major = lambda rows: pl.BlockSpec((1, rows, tm), lambda i: (i // ns, 0, i % ns))
    return pl.pallas_call(
        _inproj_body,
        grid=(t // tm,),
        in_specs=[pl.BlockSpec((tm, D_MODEL), lambda i: (i, 0)),
                  pl.BlockSpec((1, D_MODEL), lambda i: (0, 0)),
                  pl.BlockSpec(w_rows.shape, lambda i: (0, 0)),
                  pl.BlockSpec(w_kv_t.shape, lambda i: (0, 0))],
        out_specs=[pl.BlockSpec((tm, n), lambda i: (i, 0)) for n in IN_ROW_WIDTHS]
        + [feat_major(2 * KV_WIDTH)] * N_KV_BRANCH + [feat_major(KV_WIDTH)] * (N_KV_BRANCH - 1),
        out_shape=[jax.ShapeDtypeStruct((t, n), dt) for n, dt in zip(IN_ROW_WIDTHS, IN_ROW_DTYPES)]
        + [kv_sd] * N_KV_BRANCH + [vt_sd] * (N_KV_BRANCH - 1),
        compiler_params=_cparams(("arbitrary",)),
        name="in_proj",
    )(x2d, g_mix.reshape(1, D_MODEL), w_rows, w_kv_t)


def _mlstm_body(*refs, valid, bb):
    cb_ref, c0_ref, n0_ref, m0_ref = refs[4:8]
    cn_ref, c_ref, n_ref, m_ref, xx_ref = refs[16:21]
    halo = SUBLANES

    @pl.when(pl.program_id(1) == 0)
    def _():
        xx_ref[:, 0:halo, :] = jnp.zeros((bb, halo, MLSTM_WIDTH), F32)
        xx_ref[:, halo - (MLSTM_CONV - 1):halo, :] = cb_ref[...]
        c_ref[...] = c0_ref[...]
        n_ref[...] = n0_ref[...]
        m_ref[...] = m0_ref[...]

    _mlstm_chunk(*refs, valid=valid, bb=bb)


def _mlstm_chunk(mu_ref, mv_ref, mo_ref, g_ref, cb_ref, c0_ref, n0_ref, m0_ref,
                 wc_ref, bc_ref, wq_ref, wk_ref, gb_ref, gh_ref, sk_ref,
                 o_ref, cn_ref, c_ref, n_ref, m_ref,
                 xx_ref, vpad_ref, gpad_ref, *, valid, bb):
    L = MLSTM_CHUNK
    DH = MLSTM_DH
    halo = SUBLANES
    units = [(bi, h) for bi in range(bb) for h in range(MLSTM_HEADS)]
    head_lanes = lambda h: slice(h * DH, (h + 1) * DH)
    row = lax.broadcasted_iota(jnp.int32, (L, L), 0)
    col = lax.broadcasted_iota(jnp.int32, (L, L), 1)
    tril = row >= col
    triu = row <= col
    tok_col = lax.broadcasted_iota(jnp.int32, (L, 1), 0)
    tok_row = lax.broadcasted_iota(jnp.int32, (1, L), 1)

    def log_sigmoid(x):
        return jnp.minimum(x, 0.0) - jnp.log(1.0 + jnp.exp(-jnp.abs(x)))

    uc, gb, gbt = {}, {}, {}
    for bi in range(bb):
        if valid < L:
            xx_ref[bi, halo:, :] = jnp.zeros((L, MLSTM_WIDTH), F32)
            vpad_ref[bi] = jnp.zeros((L, MLSTM_WIDTH), F32)
            gpad_ref[bi] = jnp.zeros((L, LANES), F32)
        xx_ref[bi, halo:halo + valid, :] = mu_ref[bi]
        vpad_ref[bi, 0:valid, :] = mv_ref[bi]
        gpad_ref[bi, 0:valid, :] = g_ref[bi]
        conv = xx_ref[bi, halo - 3:halo - 3 + L, :] * wc_ref[0:1, :]
        for j in range(1, MLSTM_CONV):
            conv = conv + xx_ref[bi, halo - 3 + j:halo - 3 + j + L, :] * wc_ref[j:j + 1, :]
        uc[bi] = _silu(conv + bc_ref[...])
        tail = xx_ref[bi, valid + halo - 3:valid + halo, :]
        xx_ref[bi, halo - 3:halo, :] = tail
        cn_ref[bi] = tail
        gb[bi] = gpad_ref[bi] + gb_ref[...]
        gbt[bi] = gb[bi].T

    q, k, qb, kb = {}, {}, {}, {}
    for u in units:
        bi, h = u
        ub = uc[bi][:, head_lanes(h)].astype(BF16)
        q[u] = _dot(ub, wq_ref[h])
        k[u] = _dot(ub, wk_ref[h]) * (DH ** -0.5)
        qb[u], kb[u] = q[u].astype(BF16), k[u].astype(BF16)

    ic_col, ic_row, cum_col, cum_row = {}, {}, {}, {}
    for u in units:
        bi, h = u
        ic_c = gb[bi][:, GATE_COL_I + h:GATE_COL_I + h + 1]
        ic_r = gbt[bi][GATE_COL_I + h:GATE_COL_I + h + 1, :]
        lf_c = log_sigmoid(gb[bi][:, GATE_COL_F + h:GATE_COL_F + h + 1])
        lf_r = log_sigmoid(gbt[bi][GATE_COL_F + h:GATE_COL_F + h + 1, :])
        if valid < L:
            ic_c = jnp.where(tok_col < valid, ic_c, NEG_INF)
            ic_r = jnp.where(tok_row < valid, ic_r, NEG_INF)
            lf_c = jnp.where(tok_col < valid, lf_c, 0.0)
            lf_r = jnp.where(tok_row < valid, lf_r, 0.0)
        ic_col[u], ic_row[u] = ic_c, ic_r
        cum_col[u] = jnp.sum(jnp.where(tril, lf_r, 0.0), axis=1, keepdims=True)
        cum_row[u] = jnp.sum(jnp.where(triu, lf_c, 0.0), axis=0, keepdims=True)

    m_t, w, sc = {}, {}, {}
    for u in units:
        bi, h = u
        m0 = m_ref[bi, 0:1, h:h + 1]
        dmat = jnp.where(tril, cum_col[u] - cum_row[u] + ic_row[u], NEG_INF)
        inter = cum_col[u] + m0
        m_t[u] = jnp.maximum(inter, jnp.max(dmat, axis=1, keepdims=True))
        w[u] = jnp.exp(dmat - m_t[u])
        sc[u] = jnp.exp(inter - m_t[u])

    s, inter_num = {}, {}
    for u in units:
        bi, h = u
        s[u] = _dot_nt(qb[u], kb[u]) * w[u]
        inter_num[u] = _dot_nt(qb[u], c_ref[bi, h].astype(BF16))
    hc = {}
    for u in units:
        bi, h = u
        v = vpad_ref[bi, :, head_lanes(h)]
        n_old = n_ref[bi, h:h + 1, :]
        num = _dot(s[u].astype(BF16), v.astype(BF16)) + sc[u] * inter_num[u]
        den = (jnp.sum(s[u], axis=1, keepdims=True)
               + sc[u] * jnp.sum(q[u] * n_old, axis=1, keepdims=True))
        hc[u] = num / jnp.maximum(jnp.abs(den), jnp.exp(-m_t[u]))

    wl, sl, vw_t = {}, {}, {}
    for u in units:
        bi, h = u
        m_new = m_t[u][L - 1:L, :]
        cum_last = cum_col[u][L - 1:L, :]
        wl[u] = jnp.exp(cum_last - cum_col[u] + ic_col[u] - m_new)
        sl[u] = jnp.exp(cum_last + m_ref[bi, 0:1, h:h + 1] - m_new)
        vw_t[u] = (vpad_ref[bi, :, head_lanes(h)] * wl[u]).T.astype(BF16)
    for u in units:
        bi, h = u
        c_ref[bi, h] = sl[u] * c_ref[bi, h] + _dot(vw_t[u], kb[u])
        n_ref[bi, h:h + 1, :] = (sl[u] * n_ref[bi, h:h + 1, :]
                                 + jnp.sum(wl[u] * k[u], axis=0, keepdims=True))
        m_ref[bi, 0:1, h:h + 1] = m_t[u][L - 1:L, :]

    for u in units:
        bi, h = u
        hn = _rms(hc[u], gh_ref[:, head_lanes(h)])
        u_h = uc[bi][:, head_lanes(h)]
        out = ((hn[0:valid, :] + sk_ref[:, head_lanes(h)] * u_h[0:valid, :])
               * _sigmoid(mo_ref[bi, :, head_lanes(h)]))
        o_ref[bi, :, head_lanes(h)] = out


def _mlstm(mu, mv, mo, gates, conv_buf, c0, n0, m0, w_mconv, b_mconv, w_mq, w_mk, b_ig, b_fg,
           g_mhead, m_skip, *, batch, seq):
    L = MLSTM_CHUNK
    valid = min(seq, L)
    assert seq % valid == 0 and (valid == L or seq == valid)
    nc = seq // valid
    gate_bias = jnp.zeros((1, LANES), F32)
    gate_bias = gate_bias.at[0, GATE_COL_I:GATE_COL_I + MLSTM_HEADS].set(b_ig)
    gate_bias = gate_bias.at[0, GATE_COL_F:GATE_COL_F + MLSTM_HEADS].set(b_fg)
    bb = MLSTM_SEQS_PER_STEP
    assert batch % bb == 0
    tok = lambda b, c: (b, c, 0)
    const2 = lambda b, c: (0, 0)
    const3 = lambda b, c: (0, 0, 0)
    per_b3 = lambda b, c: (b, 0, 0)
    per_b4 = lambda b, c: (b, 0, 0, 0)
    H, DH, W = MLSTM_HEADS, MLSTM_DH, MLSTM_WIDTH
    rows3 = lambda a: a.reshape(batch, seq, a.shape[-1])
    o_m, conv_new, c_new, n_new, m_new = pl.pallas_call(
        functools.partial(_mlstm_body, valid=valid, bb=bb),
        grid=(batch // bb, nc),
        in_specs=[pl.BlockSpec((bb, valid, W), tok), pl.BlockSpec((bb, valid, W), tok),
                  pl.BlockSpec((bb, valid, W), tok), pl.BlockSpec((bb, valid, LANES), tok),
                  pl.BlockSpec((bb, MLSTM_CONV - 1, W), per_b3),
                  pl.BlockSpec((bb, H, DH, DH), per_b4),
                  pl.BlockSpec((bb, H, DH), per_b3),
                  pl.BlockSpec((bb, 1, H), per_b3),
                  pl.BlockSpec((MLSTM_CONV, W), const2), pl.BlockSpec((1, W), const2),
                  pl.BlockSpec((H, DH, DH), const3), pl.BlockSpec((H, DH, DH), const3),
                  pl.BlockSpec((1, LANES), const2), pl.BlockSpec((1, W), const2),
                  pl.BlockSpec((1, W), const2)],
        out_specs=[pl.BlockSpec((bb, valid, W), tok),
                   pl.BlockSpec((bb, MLSTM_CONV - 1, W), per_b3),
                   pl.BlockSpec((bb, H, DH, DH), per_b4),
                   pl.BlockSpec((bb, H, DH), per_b3),
                   pl.BlockSpec((bb, 1, H), per_b3)],
        out_shape=[jax.ShapeDtypeStruct((batch, seq, W), F32),
                   jax.ShapeDtypeStruct((batch, MLSTM_CONV - 1, W), F32),
                   jax.ShapeDtypeStruct((batch, H, DH, DH), F32),
                   jax.ShapeDtypeStruct((batch, H, DH), F32),
                   jax.ShapeDtypeStruct((batch, 1, H), F32)],
        scratch_shapes=[pltpu.VMEM((bb, SUBLANES + L, W), F32), pltpu.VMEM((bb, L, W), F32),
                        pltpu.VMEM((bb, L, LANES), F32)],
        compiler_params=_cparams(("arbitrary", "arbitrary")),
        name="mlstm",
    )(rows3(mu), rows3(mv), rows3(mo), rows3(gates), conv_buf, c0, n0, m0.reshape(batch, 1, H),
      w_mconv, b_mconv.reshape(1, W), w_mq.astype(BF16), w_mk.astype(BF16), gate_bias,
      g_mhead.reshape(1, W), m_skip.reshape(1, W))
    return o_m.reshape(batch * seq, W), conv_new, c_new, n_new, m_new


def _compress_rows(xk_ref, xv_ref, pe_ref, w1_ref, w2_ref, n_pairs):
    pair_rows = 2 * CMP_BLOCK
    accs = [None, None]
    for r in range(CMP_BLOCK):
        for kv, x_ref in enumerate((xk_ref, xv_ref)):
            ev = x_ref[pl.ds(r, n_pairs, stride=pair_rows), :]
            od = x_ref[pl.ds(CMP_BLOCK + r, n_pairs, stride=pair_rows), :]
            xr = jnp.concatenate([ev, od], axis=0) + pe_ref[kv, r:r + 1, :]
            w1_r = w1_ref[kv, r // 2, (r % 2) * KV_WIDTH:(r % 2 + 1) * KV_WIDTH, :]
            part = _dot(xr.astype(BF16), w1_r)
            accs[kv] = part if accs[kv] is None else accs[kv] + part
    return jnp.concatenate([_dot(_silu(accs[kv]).astype(BF16), w2_ref[kv]) for kv in range(2)], axis=1)


def _compress_body(xk_ref, xv_ref, pe_ref, w1_ref, w2_ref, oe_ref, oo_ref, *, n_pairs):
    out = _compress_rows(xk_ref, xv_ref, pe_ref, w1_ref, w2_ref, n_pairs)
    oe_ref[0] = out[0:n_pairs, :]
    oo_ref[0] = out[n_pairs:, :]


BLOCKS_PER_PAGE = PAGE_SIZE // CMP_BLOCK


def _gather_pages(pt_ref, pool_hbm, pages_ref, sem_ref, n_pages):
    g = pl.program_id(0) * pl.num_programs(1) + pl.program_id(1)
    n_total = pl.num_programs(0) * pl.num_programs(1)

    def copies(step, slot):
        return [pltpu.make_async_copy(pool_hbm.at[pt_ref[step * n_pages + j]], pages_ref.at[slot, j],
                                      sem_ref.at[slot]) for j in range(n_pages)]

    @pl.when(g == 0)
    def _():
        for cp in copies(0, 0):
            cp.start()

    @pl.when(g + 1 < n_total)
    def _():
        for cp in copies(g + 1, (g + 1) % 2):
            cp.start()

    slot = g % 2
    for cp in copies(g, slot):
        cp.wait()
    return slot


def _compress_paged_body(pt_ref, pool_hbm, pet_ref, perm_ref, w1_ref, w2_ref, oe_ref, oo_ref,
                         buf_ref, os_ref, pages_ref, sem_ref, *, n_pages):
    slot = _gather_pages(pt_ref, pool_hbm, pages_ref, sem_ref, n_pages)
    grp = 2 * BLOCKS_PER_PAGE
    for jp in range(n_pages // 2):
        xt = jnp.concatenate([pages_ref[slot, 2 * jp], pages_ref[slot, 2 * jp + 1]], axis=1)
        xb = (xt + pet_ref[...]).astype(BF16)
        xp = _dot_nt(perm_ref[...], xb)
        for r in range(CMP_BLOCK):
            for kv in range(2):
                lane0 = (2 * kv + r % 2) * KV_WIDTH
                buf_ref[r // 2, grp * jp:grp * (jp + 1), lane0:lane0 + KV_WIDTH] = (
                    xp[grp * r:grp * (r + 1), kv * KV_WIDTH:(kv + 1) * KV_WIDTH])
    accs = [None, None]
    for r2 in range(CMP_BLOCK // 2):
        for kv in range(2):
            lanes = slice(2 * kv * KV_WIDTH, 2 * (kv + 1) * KV_WIDTH)
            part = _dot(buf_ref[r2, :, lanes].astype(BF16), w1_ref[kv, r2])
            accs[kv] = part if accs[kv] is None else accs[kv] + part
    for kv in range(2):
        os_ref[kv] = _dot(_silu(accs[kv]).astype(BF16), w2_ref[kv])
    half = os_ref.shape[1] // 2
    for parity, ref in enumerate((oe_ref, oo_ref)):
        ref[0] = jnp.concatenate([os_ref[kv, pl.ds(parity, half, stride=2), :] for kv in range(2)],
                                 axis=1)


def _page_pair_constants(pe):
    pe_t = jnp.broadcast_to(pe.transpose(0, 2, 1)[:, None, :, None, :],
                            (2, NSA_KV_HEADS, HEAD_DIM, 2 * BLOCKS_PER_PAGE, CMP_BLOCK))
    pe_t = pe_t.reshape(2 * KV_WIDTH, 2 * PAGE_SIZE)
    grp = 2 * BLOCKS_PER_PAGE
    perm = np.zeros((2 * PAGE_SIZE, 2 * PAGE_SIZE), np.float32)
    for r in range(CMP_BLOCK):
        for b in range(grp):
            perm[r * grp + b, b * CMP_BLOCK + r] = 1.0
    return pe_t, jnp.asarray(perm, BF16)


def _pack_compress_weights(pe, w1, w2):
    eye_h = jnp.eye(NSA_KV_HEADS, dtype=F32)
    pe_r = jnp.broadcast_to(pe[:, :, None, :], (2, CMP_BLOCK, NSA_KV_HEADS, HEAD_DIM))
    pe_r = pe_r.reshape(2, CMP_BLOCK, KV_WIDTH)
    w1r = w1.reshape(2, CMP_BLOCK, HEAD_DIM, CMP_HIDDEN)
    w1_big = jnp.einsum('krdc,hH->krhdHc', w1r, eye_h)
    w1_big = w1_big.reshape(2, CMP_BLOCK // 2, 2 * KV_WIDTH, NSA_KV_HEADS * CMP_HIDDEN).astype(BF16)
    w2_big = jnp.einsum('kcd,hH->khcHd', w2, eye_h)
    w2_big = w2_big.reshape(2, NSA_KV_HEADS * CMP_HIDDEN, KV_WIDTH).astype(BF16)
    return pe_r, w1_big, w2_big


def _compress_prompt(k_rows, v_rows, cw, *, batch, seq):
    pe_r, w1_big, w2_big = cw
    n_pairs = seq // (2 * CMP_BLOCK)
    const3 = lambda b: (0, 0, 0)
    out_sd = jax.ShapeDtypeStruct((batch, n_pairs, 2 * KV_WIDTH), F32)
    return pl.pallas_call(
        functools.partial(_compress_body, n_pairs=n_pairs),
        grid=(batch,),
        in_specs=[pl.BlockSpec((seq, KV_WIDTH), lambda b: (b, 0)),
                  pl.BlockSpec((seq, KV_WIDTH), lambda b: (b, 0)),
                  pl.BlockSpec(pe_r.shape, const3),
                  pl.BlockSpec(w1_big.shape, lambda b: (0, 0, 0, 0)),
                  pl.BlockSpec(w2_big.shape, const3)],
        out_specs=[pl.BlockSpec((1, n_pairs, 2 * KV_WIDTH), lambda b: (b, 0, 0))] * 2,
        out_shape=[out_sd, out_sd],
        compiler_params=_cparams(("arbitrary",)),
        name="compress_prompt",
    )(k_rows, v_rows, pe_r, w1_big, w2_big)


COMPRESS_PAGES_PER_STEP = 64


def _compress_paged(pool, page_table, cw, cw_pages):
    _, w1_big, w2_big = cw
    pe_t, perm = cw_pages
    batch, n_pages = page_table.shape
    pps = COMPRESS_PAGES_PER_STEP
    assert n_pages % pps == 0 and pps % 2 == 0
    n_steps = n_pages // pps
    n_blk = pps * BLOCKS_PER_PAGE
    const3 = lambda b, c, pt: (0, 0, 0)
    return pl.pallas_call(
        functools.partial(_compress_paged_body, n_pages=pps),
        grid_spec=pltpu.PrefetchScalarGridSpec(
            num_scalar_prefetch=1,
            grid=(batch, n_steps),
            in_specs=[pl.BlockSpec(memory_space=pl.ANY),
                      pl.BlockSpec(pe_t.shape, lambda b, c, pt: (0, 0)),
                      pl.BlockSpec(perm.shape, lambda b, c, pt: (0, 0)),
                      pl.BlockSpec(w1_big.shape, lambda b, c, pt: (0, 0, 0, 0)),
                      pl.BlockSpec(w2_big.shape, const3)],
            out_specs=[pl.BlockSpec((1, n_blk // 2, 2 * KV_WIDTH), lambda b, c, pt: (b, c, 0))] * 2,
            scratch_shapes=[pltpu.VMEM((CMP_BLOCK // 2, n_blk, 4 * KV_WIDTH), F32),
                            pltpu.VMEM((2, n_blk, KV_WIDTH), F32),
                            pltpu.VMEM((2, pps, 2 * KV_WIDTH, PAGE_SIZE), F32),
                            pltpu.SemaphoreType.DMA((2,))]),
        out_shape=[jax.ShapeDtypeStruct((batch, n_steps * n_blk // 2, 2 * KV_WIDTH), F32)] * 2,
        compiler_params=_cparams(("arbitrary", "arbitrary")),
        name="compress_paged",
    )(page_table.reshape(-1), pool, pe_t, perm, w1_big, w2_big)


def _cmp_attn_body(q_ref, ke_ref, ko_ref, o_ref, st_ref, *, tq, pos0):
    ns = ke_ref.shape[1]
    i = pl.program_id(1)
    rows = NSA_GROUP * tq
    tok0 = pos0 + i * tq
    pos_r = tok0 + lax.broadcasted_iota(jnp.int32, (1, rows), 1) % tq
    pair_c = lax.broadcasted_iota(jnp.int32, (ns, 1), 0)
    end_e = (2 * pair_c + 1) * CMP_BLOCK - 1
    end_o = (2 * pair_c + 2) * CMP_BLOCK - 1
    any_r = (CMP_BLOCK - 1 <= pos_r).astype(F32)
    contract_blocks = (((0,), (0,)), ((), ()))
    q = q_ref[...] * ATTN_SCALE
    heads = range(NSA_KV_HEADS)
    te, to = {}, {}
    for kh in heads:
        qs = jnp.concatenate([q[:, (kh * NSA_GROUP + g) * HEAD_DIM:(kh * NSA_GROUP + g + 1) * HEAD_DIM]
                              for g in range(NSA_GROUP)], axis=0).astype(BF16)
        ks = slice(kh * HEAD_DIM, (kh + 1) * HEAD_DIM)
        te[kh] = jnp.where(end_e <= pos_r, _dot_nt(ke_ref[0, :, ks].astype(BF16), qs), NEG_INF)
        to[kh] = jnp.where(end_o <= pos_r, _dot_nt(ko_ref[0, :, ks].astype(BF16), qs), NEG_INF)
    pte, pto = {}, {}
    for kh in heads:
        mt = jnp.maximum(jnp.max(te[kh], axis=0, keepdims=True), jnp.max(to[kh], axis=0, keepdims=True))
        pe, po = jnp.exp(te[kh] - mt), jnp.exp(to[kh] - mt)
        invt = any_r / (jnp.sum(pe, axis=0, keepdims=True) + jnp.sum(po, axis=0, keepdims=True))
        pte[kh], pto[kh] = pe * invt, po * invt
    for kh in heads:
        vs = slice(KV_WIDTH + kh * HEAD_DIM, KV_WIDTH + (kh + 1) * HEAD_DIM)
        oh = (lax.dot_general(pte[kh].astype(BF16), ke_ref[0, :, vs].astype(BF16), contract_blocks,
                              preferred_element_type=F32)
              + lax.dot_general(pto[kh].astype(BF16), ko_ref[0, :, vs].astype(BF16), contract_blocks,
                                preferred_element_type=F32))
        for g in range(NSA_GROUP):
            hd = kh * NSA_GROUP + g
            o_ref[:, hd * HEAD_DIM:(hd + 1) * HEAD_DIM] = oh[g * tq:(g + 1) * tq, :]
        ps = pte[kh] + pto[kh]
        score = ps[:, 0:tq]
        for g in range(1, NSA_GROUP):
            score = score + ps[:, g * tq:(g + 1) * tq]
        st_ref[0, kh] = score


def _cmp_attn(q2d, kce, kco, *, batch, seq, tq, pos0):
    ns = kce.shape[1]
    nq = seq // tq
    return pl.pallas_call(
        functools.partial(_cmp_attn_body, tq=tq, pos0=pos0),
        grid=(batch, nq),
        in_specs=[pl.BlockSpec((tq, NSA_WIDTH), lambda b, i: (b * nq + i, 0)),
                  pl.BlockSpec((1, ns, 2 * KV_WIDTH), lambda b, i: (b, 0, 0)),
                  pl.BlockSpec((1, ns, 2 * KV_WIDTH), lambda b, i: (b, 0, 0))],
        out_specs=[pl.BlockSpec((tq, NSA_WIDTH), lambda b, i: (b * nq + i, 0)),
                   pl.BlockSpec((1, NSA_KV_HEADS, ns, tq), lambda b, i: (b, 0, 0, i))],
        out_shape=[jax.ShapeDtypeStruct((batch * seq, NSA_WIDTH), F32),
                   jax.ShapeDtypeStruct((batch, NSA_KV_HEADS, ns, seq), F32)],
        compiler_params=_cparams(("arbitrary", "arbitrary")),
        name="cmp_attn",
    )(q2d, kce, kco)


def _topk_body(pos_ref, st_ref, b_ref, *, n_sel):
    score = st_ref[0]
    ns, tt = score.shape
    nsw = b_ref.shape[1]
    if nsw > ns:
        score = jnp.concatenate([score, jnp.zeros((nsw - ns, tt), F32)], axis=0)
    blk = lax.broadcasted_iota(jnp.int32, (nsw, 1), 0)
    blk_f = blk.astype(F32)
    cur = pos_ref[...] // SEL_BLOCK
    forced = (blk == 0) | (blk == cur) | (blk == cur - 1)
    pri = jnp.where(blk <= cur, jnp.where(forced, SEL_PRIORITY, score), -SEL_PRIORITY)
    pri = jnp.where(blk < n_sel, pri, -jnp.inf)
    bias = jnp.full((nsw, tt), NEG_INF, F32)
    for _ in range(min(TOP_N, n_sel)):
        top = jnp.max(pri, axis=0, keepdims=True)
        first = jnp.min(jnp.where(pri == top, blk_f, float(nsw)), axis=0, keepdims=True)
        hit = blk_f == first
        bias = jnp.where(hit, 0.0, bias)
        pri = jnp.where(hit, -jnp.inf, pri)
    b_ref[0] = bias


def _topk_blocks(scores_t, pos, *, n_sel, nsw, tt):
    groups, ns, tokens = scores_t.shape
    assert nsw >= max(ns, n_sel) and tokens % tt == 0
    return pl.pallas_call(
        functools.partial(_topk_body, n_sel=n_sel),
        grid=(groups, tokens // tt),
        in_specs=[pl.BlockSpec((1, tt), lambda g, i: (0, i)),
                  pl.BlockSpec((1, ns, tt), lambda g, i: (g, 0, i))],
        out_specs=pl.BlockSpec((1, nsw, tt), lambda g, i: (g, 0, i)),
        out_shape=jax.ShapeDtypeStruct((groups, nsw, tokens), F32),
        compiler_params=_cparams(("arbitrary", "arbitrary")),
        name="topk_blocks",
    )(pos, scores_t)


def _softmax_update(sc, vt_bf16, m_ref, l_ref, acc_ref):
    m_old = m_ref[...]
    m_new = jnp.maximum(m_old, jnp.max(sc, axis=1, keepdims=True))
    alpha = jnp.exp(m_old - m_new)
    pr = jnp.exp(sc - jnp.concatenate([m_new] * (sc.shape[1] // LANES), axis=1))
    l_ref[...] = alpha * l_ref[...] + jnp.sum(pr, axis=1, keepdims=True)
    acc_ref[...] = alpha * acc_ref[...] + _dot_nt(pr.astype(BF16), vt_bf16)
    m_ref[...] = m_new


def _softmax_init(m_ref, l_ref, acc_ref):
    m_ref[...] = jnp.full(m_ref.shape, NEG_INF, F32)
    l_ref[...] = jnp.zeros(l_ref.shape, F32)
    acc_ref[...] = jnp.zeros(acc_ref.shape, F32)


ATTN_TAB_COLS = 5


def _attn_pairs(seq, tq, tk):
    rows = []
    for i in range(seq // tq):
        t_lo, t_hi = i * tq, i * tq + tq - 1
        js = list(range(0, t_hi // tk + 1))
        for n, j in enumerate(js):
            rows.append((i, j, int(n == 0), int(n == len(js) - 1), int(j * tk + tk - 1 > t_lo)))
    return np.asarray(rows, np.int32)


def _attn_body(tab_ref, q_ref, k_ref, vt_ref, oh_ref, sb_ref, o_ref, qa_ref, m_ref, l_ref, acc_ref,
               *, tq, tk):
    p = pl.program_id(1)
    i, j, first, last, partial_tile = [tab_ref[ATTN_TAB_COLS * p + n] for n in range(ATTN_TAB_COLS)]
    cols = NSA_HEADS * tq

    @pl.when(first == 1)
    def _():
        for hd, piece in enumerate(_query_columns(q_ref[0], sb_ref[0], tq)):
            qa_ref[:, hd * tq:(hd + 1) * tq] = piece
        m_ref[...] = jnp.full(m_ref.shape, NEG_INF, F32)
        l_ref[...] = jnp.zeros(l_ref.shape, F32)
        acc_ref[...] = jnp.zeros(acc_ref.shape, F32)

    k_aug = jnp.concatenate([k_ref[...], oh_ref[...]], axis=1)
    vt = vt_ref[0]

    def update(mask):
        sc = _dot(k_aug, qa_ref[...])
        if mask is not None:
            sc = jnp.where(mask, sc, NEG_INF)
        m_old = m_ref[...]
        m_new = jnp.maximum(m_old, jnp.max(sc, axis=0, keepdims=True))
        alpha = jnp.exp2(m_old - m_new)
        pr = jnp.exp2(sc - m_new)
        l_ref[...] = alpha * l_ref[...] + jnp.sum(pr, axis=0, keepdims=True)
        acc_ref[...] = alpha * acc_ref[...] + _dot(vt, pr.astype(BF16))
        m_ref[...] = m_new

    @pl.when(partial_tile == 1)
    def _():
        qpos = i * tq + (lax.broadcasted_iota(jnp.int32, (1, cols), 1) & (tq - 1))
        kpos = j * tk + lax.broadcasted_iota(jnp.int32, (tk, 1), 0)
        update(kpos <= qpos)

    @pl.when(partial_tile == 0)
    def _():
        update(None)

    @pl.when(last == 1)
    def _():
        _store_heads(acc_ref[...] / l_ref[...], o_ref, tq)


def _query_columns(q, sel_bias, tq):
    q = q * (ATTN_SCALE * LOG2_E)
    zeros64 = jnp.zeros((HEAD_DIM, tq), F32)
    kv_head_rows = lambda x, kh: jnp.concatenate([x, zeros64] if kh == 0 else [zeros64, x], axis=0)
    pieces = []
    for m in range(NSA_HEADS // 2):
        q_t = q[:, m * LANES:(m + 1) * LANES].T
        for hd in (2 * m, 2 * m + 1):
            kh = hd // NSA_GROUP
            piece = kv_head_rows(q_t[(hd % 2) * HEAD_DIM:(hd % 2 + 1) * HEAD_DIM, :], kh)
            if sel_bias is not None:
                piece = jnp.concatenate([piece, kv_head_rows(sel_bias[kh], kh)], axis=0)
            pieces.append(piece.astype(BF16))
    return pieces


def _store_heads(o_t, o_ref, tq):
    for m in range(NSA_HEADS // 2):
        pair = jnp.concatenate(
            [o_t[(hd // NSA_GROUP) * HEAD_DIM:(hd // NSA_GROUP + 1) * HEAD_DIM, hd * tq:(hd + 1) * tq]
             for hd in (2 * m, 2 * m + 1)], axis=0)
        o_ref[0, :, m * LANES:(m + 1) * LANES] = pair.T


def _window_body(q_ref, *refs, tq, n_tiles):
    k_refs, v_refs, o_ref = refs[:n_tiles], refs[n_tiles:2 * n_tiles], refs[2 * n_tiles]
    i = pl.program_id(1)
    cols = NSA_HEADS * tq
    qa = jnp.concatenate(_query_columns(q_ref[0], None, tq), axis=1)
    qpos = i * tq + (lax.broadcasted_iota(jnp.int32, (1, cols), 1) & (tq - 1))
    scs = []
    for n, k_ref in enumerate(k_refs):
        kpos = (i - (n_tiles - 1) + n) * tq + lax.broadcasted_iota(jnp.int32, (tq, 1), 0)
        if n == n_tiles - 1:
            valid = kpos <= qpos
        elif n == 0:
            valid = (kpos > qpos - WINDOW) & (kpos >= 0)
        else:
            valid = kpos >= 0
        scs.append(jnp.where(valid, _dot(k_ref[...], qa), NEG_INF))
    mx = scs[0].max(axis=0, keepdims=True)
    for sc in scs[1:]:
        mx = jnp.maximum(mx, sc.max(axis=0, keepdims=True))
    l_sum, acc = None, None
    for sc, v_ref in zip(scs, v_refs):
        pr = jnp.exp2(sc - mx)
        part_l, part_acc = jnp.sum(pr, axis=0, keepdims=True), _dot(v_ref[0], pr.astype(BF16))
        l_sum = part_l if l_sum is None else l_sum + part_l
        acc = part_acc if acc is None else acc + part_acc
    _store_heads(acc / l_sum, o_ref, tq)


def _attn_window_prompt(q3d, k_rows, v_t, *, tq):
    batch, seq, _ = q3d.shape
    assert WINDOW % tq == 0 and seq % tq == 0 and tq & (tq - 1) == 0
    n_tiles = WINDOW // tq + 1
    nq = seq // tq
    tile = lambda n: (lambda i: jnp.maximum(i - (n_tiles - 1) + n, 0))
    return pl.pallas_call(
        functools.partial(_window_body, tq=tq, n_tiles=n_tiles),
        grid=(batch, nq),
        in_specs=[pl.BlockSpec((1, tq, NSA_WIDTH), lambda b, i: (b, i, 0))]
        + [pl.BlockSpec((tq, KV_WIDTH), lambda b, i, t=tile(n): (b * nq + t(i), 0)) for n in range(n_tiles)]
        + [pl.BlockSpec((1, KV_WIDTH, tq), lambda b, i, t=tile(n): (b, 0, t(i))) for n in range(n_tiles)],
        out_specs=pl.BlockSpec((1, tq, NSA_WIDTH), lambda b, i: (b, i, 0)),
        out_shape=jax.ShapeDtypeStruct((batch, seq, NSA_WIDTH), F32),
        compiler_params=_cparams(("arbitrary", "arbitrary")),
        name="attn_win",
    )(q3d, *([k_rows] * n_tiles), *([v_t] * n_tiles))


def _block_onehot(seq):
    blk = np.arange(seq)[:, None] // SEL_BLOCK
    return jnp.asarray((np.arange(LANES)[None, :] % SEL_BLOCK) == blk, BF16)


def _attn_selected_prompt(q3d, k_rows, v_t, selb, *, tq, tk):
    batch, seq, _ = q3d.shape
    assert tq & (tq - 1) == 0 and tk % LANES == 0 and tq % LANES == 0 and selb.shape[2] == SEL_BLOCK
    tab = _attn_pairs(seq, tq, tk)
    cols = NSA_HEADS * tq
    C = ATTN_TAB_COLS
    nk = seq // tk
    return pl.pallas_call(
        functools.partial(_attn_body, tq=tq, tk=tk),
        grid_spec=pltpu.PrefetchScalarGridSpec(
            num_scalar_prefetch=1,
            grid=(batch, tab.shape[0]),
            in_specs=[pl.BlockSpec((1, tq, NSA_WIDTH), lambda b, p, t: (b, t[C * p], 0)),
                      pl.BlockSpec((tk, KV_WIDTH), lambda b, p, t: (b * nk + t[C * p + 1], 0)),
                      pl.BlockSpec((1, KV_WIDTH, tk), lambda b, p, t: (b, 0, t[C * p + 1])),
                      pl.BlockSpec((tk, LANES), lambda b, p, t: (t[C * p + 1], 0)),
                      pl.BlockSpec((1, NSA_KV_HEADS, SEL_BLOCK, tq),
                                   lambda b, p, t: (b, 0, 0, t[C * p]))],
            out_specs=pl.BlockSpec((1, tq, NSA_WIDTH), lambda b, p, t: (b, t[C * p], 0)),
            scratch_shapes=[pltpu.VMEM((2 * LANES, cols), BF16), pltpu.VMEM((1, cols), F32),
                            pltpu.VMEM((1, cols), F32), pltpu.VMEM((KV_WIDTH, cols), F32)]),
        out_shape=jax.ShapeDtypeStruct((batch, seq, NSA_WIDTH), F32),
        compiler_params=_cparams(("arbitrary", "arbitrary")),
        name="attn_sel",
    )(jnp.asarray(tab.reshape(-1)), q3d, k_rows, v_t, _block_onehot(seq), selb)


ATTN_PAGES_PER_STEP = 64
ATTN_PAGED_SPLIT = 2


def _attn_paged_body(pt_ref, qa_ref, bq_ref, bn_ref, kn_ref, oh_ref, pool_hbm, o_ref, m_ref, l_ref,
                     acc_ref, pages_ref, sem_ref, *, n_pages, n_new):
    slot = _gather_pages(pt_ref, pool_hbm, pages_ref, sem_ref, n_pages)
    page_refs = [pages_ref.at[slot, j] for j in range(n_pages)]
    c = pl.program_id(1)
    rows = qa_ref.shape[1]

    @pl.when(c == 0)
    def _():
        _softmax_init(m_ref, l_ref, acc_ref)

    n_split = m_ref.shape[0]
    per = n_pages // n_split
    keys = per * PAGE_SIZE
    qa = qa_ref[0]
    bias = bq_ref[0, 0]
    blocks = keys // SEL_BLOCK
    lane = lax.broadcasted_iota(jnp.int32, (1, LANES), 1)
    scs, vts = [], []
    for s in range(n_split):
        refs_s = page_refs[s * per:(s + 1) * per]
        bias_s = bias if s == 0 else pltpu.roll(bias, LANES - s * blocks, axis=1)
        lhs = jnp.concatenate([qa, jnp.where(lane < blocks, bias_s, 0.0)], axis=1).astype(BF16)
        kt = jnp.concatenate([r[0:KV_WIDTH, :] for r in refs_s], axis=1)
        rhs = jnp.concatenate([kt.astype(BF16), oh_ref[...]], axis=0)
        scs.append(_dot(lhs, rhs))
        vts.append(jnp.concatenate([r[KV_WIDTH:, :] for r in refs_s], axis=1).astype(BF16))
    for s in range(n_split):
        _softmax_update(scs[s], vts[s], m_ref.at[s], l_ref.at[s], acc_ref.at[s])

    @pl.when(c == pl.num_programs(1) - 1)
    def _():
        kn = kn_ref[0]
        sc = _dot(qa.astype(BF16), kn[0:KV_WIDTH, :].astype(BF16)) + bn_ref[0]
        tq = lax.broadcasted_iota(jnp.int32, (rows, 1), 0) % n_new
        kk = lax.broadcasted_iota(jnp.int32, (1, kn.shape[1]), 1)
        sc = jnp.where((kk <= tq) & (kk < n_new), sc, NEG_INF)
        _softmax_update(sc, kn[KV_WIDTH:, :].astype(BF16), m_ref.at[0], l_ref.at[0], acc_ref.at[0])
        m_all = m_ref[0]
        for s in range(1, n_split):
            m_all = jnp.maximum(m_all, m_ref[s])
        l_all = jnp.zeros(m_all.shape, F32)
        acc_all = jnp.zeros(m_all.shape, F32)
        for s in range(n_split):
            scale = jnp.exp(m_ref[s] - m_all)
            l_all = l_all + scale * l_ref[s]
            acc_all = acc_all + scale * acc_ref[s]
        o_ref[0] = acc_all / l_all


def _attn_paged(qa, bias_q, bias_new, kv_new_t, pool, page_table, *, n_new):
    batch, n_pages = page_table.shape
    pps = ATTN_PAGES_PER_STEP
    keys_per_chain = pps // ATTN_PAGED_SPLIT * PAGE_SIZE
    assert n_pages % pps == 0 and pps * PAGE_SIZE // SEL_BLOCK <= LANES
    assert keys_per_chain // SEL_BLOCK <= SEL_BLOCK
    n_steps = n_pages // pps
    rows = qa.shape[1]

    per_b = lambda b, c, pt: (b, 0, 0)
    return pl.pallas_call(
        functools.partial(_attn_paged_body, n_pages=pps, n_new=n_new),
        grid_spec=pltpu.PrefetchScalarGridSpec(
            num_scalar_prefetch=1,
            grid=(batch, n_steps),
            in_specs=[pl.BlockSpec((1, rows, LANES), per_b),
                      pl.BlockSpec((1, 1, rows, LANES), lambda b, c, pt: (b, c, 0, 0)),
                      pl.BlockSpec((1, rows, LANES), per_b),
                      pl.BlockSpec((1,) + kv_new_t.shape[1:], per_b),
                      pl.BlockSpec((LANES, keys_per_chain), lambda b, c, pt: (0, 0)),
                      pl.BlockSpec(memory_space=pl.ANY)],
            out_specs=pl.BlockSpec((1, rows, LANES), per_b),
            scratch_shapes=[pltpu.VMEM((ATTN_PAGED_SPLIT, rows, LANES), F32)] * 3
            + [pltpu.VMEM((2, pps, 2 * KV_WIDTH, PAGE_SIZE), F32), pltpu.SemaphoreType.DMA((2,))]),
        out_shape=jax.ShapeDtypeStruct((batch, rows, LANES), F32),
        compiler_params=_cparams(("arbitrary", "arbitrary")),
        name="attn_sel_paged",
    )(page_table.reshape(-1), qa, bias_q, bias_new, kv_new_t, _block_onehot(keys_per_chain).T, pool)


def _attn_window_body(qa_ref, wb_ref, kn_ref, o_ref, *, n_new, past):
    qa = qa_ref[0].astype(BF16)
    wb, kn = wb_ref[0], kn_ref[0]
    rows, n_buf = qa.shape[0], wb.shape[1]
    qpos = past + lax.broadcasted_iota(jnp.int32, (rows, 1), 0) % n_new

    def masked(sc, kpos, extra):
        diff = qpos - kpos
        return jnp.where((diff >= 0) & (diff < WINDOW) & (kpos >= 0) & extra, sc, NEG_INF)

    nb = lax.broadcasted_iota(jnp.int32, (1, n_buf), 1)
    nn = lax.broadcasted_iota(jnp.int32, (1, kn.shape[1]), 1)
    sb = masked(_dot(qa, wb[0:KV_WIDTH, :].astype(BF16)), past - n_buf + nb, nb >= 0)
    sn = masked(_dot(qa, kn[0:KV_WIDTH, :].astype(BF16)), past + nn, nn < n_new)
    mx = jnp.maximum(jnp.max(sb, axis=1, keepdims=True), jnp.max(sn, axis=1, keepdims=True))
    pb, pn = jnp.exp(sb - mx), jnp.exp(sn - mx)
    o = (_dot_nt(pb.astype(BF16), wb[KV_WIDTH:, :].astype(BF16))
         + _dot_nt(pn.astype(BF16), kn[KV_WIDTH:, :].astype(BF16)))
    o_ref[0] = o / (jnp.sum(pb, axis=1, keepdims=True) + jnp.sum(pn, axis=1, keepdims=True))


def _attn_window_small(qa, win_t, kv_new_t, *, n_new, past):
    batch, rows, _ = qa.shape
    per_b = lambda b: (b, 0, 0)
    return pl.pallas_call(
        functools.partial(_attn_window_body, n_new=n_new, past=past),
        grid=(batch,),
        in_specs=[pl.BlockSpec((1, rows, LANES), per_b),
                  pl.BlockSpec((1,) + win_t.shape[1:], per_b),
                  pl.BlockSpec((1,) + kv_new_t.shape[1:], per_b)],
        out_specs=pl.BlockSpec((1, rows, LANES), per_b),
        out_shape=jax.ShapeDtypeStruct((batch, rows, LANES), F32),
        compiler_params=_cparams(("arbitrary",)),
        name="attn_win_small",
    )(qa, win_t, kv_new_t)


FFN_TM = 512


def _ffn_vmem_bytes(tm, halo):
    weights = 2 * (D_MODEL * D_MODEL + D_MODEL * 2 * D_FF + D_FF * D_MODEL)
    conv_buffer = 4 * (halo + tm) * 2 * D_FF
    row_tiles = 2 * 4 * tm * (2 * D_MODEL + 4 * NSA_WIDTH + LANES)
    temporaries = 3 * 4 * tm * max(hi - lo for lo, hi in FFN_CHUNKS)
    return weights + conv_buffer + row_tiles + temporaries
MXU_DEPTH = 256
FFN_CHUNKS = ((0, 6 * MXU_DEPTH), (6 * MXU_DEPTH, D_FF))


def _ffn_body(x_ref, om_ref, oc_ref, os_ref, ow_ref, gt_ref, ge_ref, gn_ref, gf_ref, gl_ref, wc_ref,
              fb_ref, wo_hbm, wu_hbm, wd_hbm, y_ref, fn_ref, xx_ref, wo_ref, wu_ref, wd_ref, sem_ref,
              *, tm, stride, halo):
    s = pl.program_id(1)

    @pl.when((pl.program_id(0) == 0) & (s == 0))
    def _():
        copies = [pltpu.make_async_copy(src, dst, sem_ref.at[n])
                  for n, (src, dst) in enumerate(((wo_hbm, wo_ref), (wu_hbm, wu_ref), (wd_hbm, wd_ref)))]
        for cp in copies:
            cp.start()
        for cp in copies:
            cp.wait()

    sig = _sigmoid(gt_ref[...])
    hi = sig.astype(BF16)
    lo = (sig - hi.astype(F32)).astype(BF16)
    comb = None
    for br, ob_ref in enumerate((oc_ref, os_ref, ow_ref)):
        gate = _dot(hi, ge_ref[br]) + _dot(lo, ge_ref[br])
        term = gate * ob_ref[...]
        comb = term if comb is None else comb + term
    onsa = _rms(comb, gn_ref[...])
    h = (x_ref[...] + _dot(om_ref[...].astype(BF16), wo_ref[0:MLSTM_WIDTH, :])
         + _dot(onsa.astype(BF16), wo_ref[MLSTM_WIDTH:, :]))
    hn = _rms(h, gf_ref[...]).astype(BF16)

    base = halo - (FFN_CONV - 1) * stride

    @pl.when(s == 0)
    def _():
        xx_ref[base:halo, :] = fb_ref[0]

    y_ref[...] = h

    def up_project(chunk):
        for half in range(2):
            cols = slice(half * D_FF + chunk[0], half * D_FF + chunk[1])
            xx_ref[halo:halo + tm, cols] = _dot(hn, wu_ref[:, cols])

    up_project(FFN_CHUNKS[0])
    for n, (lo_col, hi_col) in enumerate(FFN_CHUNKS):
        if n + 1 < len(FFN_CHUNKS):
            up_project(FFN_CHUNKS[n + 1])
        convs = []
        for half in range(2):
            cols = slice(half * D_FF + lo_col, half * D_FF + hi_col)
            conv = xx_ref[base:base + tm, cols] * wc_ref[0:1, cols]
            for j in range(1, FFN_CONV):
                conv = conv + xx_ref[base + j * stride:base + j * stride + tm, cols] * wc_ref[j:j + 1, cols]
            convs.append(conv)
        act = _silu(convs[1]) * convs[0]
        y_ref[...] += _dot(act.astype(BF16), wd_ref[lo_col:hi_col, :])
    fn_ref[0, 0] = xx_ref[tm + base:tm + halo, :]
    xx_ref[0:halo, :] = xx_ref[tm:tm + halo, :]
    y_ref[...] = _rms(y_ref[...], gl_ref[...])


def _gate_expand():
    ge = np.zeros((N_BRANCH, LANES, NSA_WIDTH), np.float32)
    for hd in range(NSA_HEADS):
        for br in range(N_BRANCH):
            ge[br, GATE_COL_NSA + hd * N_BRANCH + br, hd * HEAD_DIM:(hd + 1) * HEAD_DIM] = 1.0
    return jnp.asarray(ge, BF16)


def _ffn(x2d, om, oc, osel, ow, gt, fbuf, w_out, g_nsa, g_ffn, g_final, w_up, w_fconv, w_down,
         *, nb, tm, stride):
    rows = x2d.shape[0]
    ns = rows // (nb * tm)
    halo = -(-(FFN_CONV - 1) * stride // SUBLANES) * SUBLANES
    assert tm >= halo and all((hi - lo) % MXU_DEPTH == 0 for lo, hi in FFN_CHUNKS)
    vmem_limit = _ffn_vmem_bytes(tm, halo)
    assert vmem_limit <= VMEM_BYTES
    tok = lambda b, s: (b * ns + s, 0)
    nfb = (FFN_CONV - 1) * stride

    def const(shape):
        return pl.BlockSpec(shape, lambda b, s: (0,) * len(shape))

    hbm = pl.BlockSpec(memory_space=pl.ANY)
    y, fn = pl.pallas_call(
        functools.partial(_ffn_body, tm=tm, stride=stride, halo=halo),
        grid=(nb, ns),
        in_specs=[pl.BlockSpec((tm, D_MODEL), tok)] + [pl.BlockSpec((tm, NSA_WIDTH), tok)] * 4
        + [pl.BlockSpec((tm, LANES), tok),
           const((N_BRANCH, LANES, NSA_WIDTH)), const((1, NSA_WIDTH)), const((1, D_MODEL)),
           const((1, D_MODEL)), const((FFN_CONV, 2 * D_FF)),
           pl.BlockSpec((1, nfb, 2 * D_FF), lambda b, s: (b, 0, 0)), hbm, hbm, hbm],
        out_specs=[pl.BlockSpec((tm, D_MODEL), tok),
                   pl.BlockSpec((1, 1, nfb, 2 * D_FF), lambda b, s: (b, s, 0, 0))],
        out_shape=[jax.ShapeDtypeStruct((rows, D_MODEL), F32),
                   jax.ShapeDtypeStruct((nb, ns, nfb, 2 * D_FF), F32)],
        scratch_shapes=[pltpu.VMEM((halo + tm, 2 * D_FF), F32),
                        pltpu.VMEM((D_MODEL, D_MODEL), BF16), pltpu.VMEM((D_MODEL, 2 * D_FF), BF16),
                        pltpu.VMEM((D_FF, D_MODEL), BF16), pltpu.SemaphoreType.DMA((3,))],
        compiler_params=pltpu.CompilerParams(dimension_semantics=("arbitrary", "arbitrary"),
                                             vmem_limit_bytes=vmem_limit),
        name="outproj_ffn",
    )(x2d, om, oc, osel, ow, gt, _gate_expand(), g_nsa.reshape(1, -1), g_ffn.reshape(1, -1),
      g_final.reshape(1, -1), w_fconv, fbuf, w_out.astype(BF16), w_up.astype(BF16),
      w_down.astype(BF16))
    return y, fn[:, ns - 1]


PROMPT_TM = 512
PROMPT_TQ_CMP = 512
PROMPT_TT_TOPK = 1024
PROMPT_TQ_SEL = 512
PROMPT_TK_SEL = 512
PROMPT_TQ_WIN = 256


def _kv_rows(kv_t):
    batch, _, rows = kv_t.shape
    return kv_t.reshape(batch, 2, NSA_KV_HEADS, HEAD_DIM, rows).transpose(0, 4, 1, 2, 3)


def _kv_feature_major(kv5):
    batch, rows = kv5.shape[:2]
    return kv5.transpose(0, 2, 3, 4, 1).reshape(batch, 2 * KV_WIDTH, rows)


def _prompt_layer(x, wts):
    batch, seq, _ = x.shape
    x2d = x.reshape(batch * seq, D_MODEL)
    q, kc_rows, vc_rows, mu, mv, mo, gt, ks_rows, kw_rows, kvc_t, kvs_t, kvw_t, vs_t, vw_t = _in_proj(
        x2d, wts["g_mix"], wts["w_in_packed"], batch=batch, seq=seq, tm=min(PROMPT_TM, seq))
    H, DH, W = MLSTM_HEADS, MLSTM_DH, MLSTM_WIDTH
    o_m, mconv, c_new, n_new, m_new = _mlstm(
        mu, mv, mo, gt, jnp.zeros((batch, MLSTM_CONV - 1, W), F32), jnp.zeros((batch, H, DH, DH), F32),
        jnp.zeros((batch, H, DH), F32), jnp.zeros((batch, H), F32),
        wts["w_mconv"], wts["b_mconv"], wts["w_mq"], wts["w_mk"], wts["b_ig"], wts["b_fg"],
        wts["g_mhead"], wts["m_skip"], batch=batch, seq=seq)
    kce, kco = _compress_prompt(kc_rows, vc_rows, wts["cw"], batch=batch, seq=seq)
    n_sel = -(-seq // SEL_BLOCK)
    assert n_sel <= SEL_BLOCK
    o_cmp, scores_t = _cmp_attn(q, kce, kco, batch=batch, seq=seq, tq=min(PROMPT_TQ_CMP, seq), pos0=0)
    selb = _topk_blocks(scores_t.reshape(batch * NSA_KV_HEADS, -1, seq),
                        jnp.arange(seq, dtype=jnp.int32).reshape(1, seq),
                        n_sel=n_sel, nsw=SEL_BLOCK, tt=min(PROMPT_TT_TOPK, seq))
    selb = selb.reshape(batch, NSA_KV_HEADS, SEL_BLOCK, seq)
    q3d = q.reshape(batch, seq, NSA_WIDTH)
    o_sel = _attn_selected_prompt(q3d, ks_rows, vs_t, selb, tq=min(PROMPT_TQ_SEL, seq),
                                  tk=min(PROMPT_TK_SEL, seq))
    o_win = _attn_window_prompt(q3d, kw_rows, vw_t, tq=PROMPT_TQ_WIN)
    fbuf = jnp.zeros((batch, FFN_CONV - 1, 2 * D_FF), F32)
    y, f_new = _ffn(x2d, o_m, o_cmp, o_sel.reshape(-1, NSA_WIDTH), o_win.reshape(-1, NSA_WIDTH), gt,
                    fbuf, wts["w_out"], wts["g_nsa"], wts["g_ffn"], wts["g_final"], wts["w_up"],
                    wts["w_fconv"], wts["w_down"], nb=batch, tm=min(FFN_TM, seq), stride=1)
    n_win = min(WINDOW, seq)
    return (y.reshape(batch, seq, D_MODEL), _kv_rows(kvc_t), _kv_rows(kvs_t),
            _kv_rows(kvw_t[:, :, seq - n_win:]), mconv, c_new, n_new, m_new.reshape(batch, H), f_new)


def _decode_rows(q2d, batch, seq):
    q5 = (q2d * ATTN_SCALE).reshape(batch, seq, NSA_KV_HEADS, NSA_GROUP, HEAD_DIM).transpose(0, 2, 3, 1, 4)
    eye = jnp.eye(NSA_KV_HEADS, dtype=F32)
    qa = jnp.einsum('bkgtd,kK->bkgtKd', q5, eye)
    return qa.reshape(batch, NSA_KV_HEADS * NSA_GROUP * seq, KV_WIDTH)


def _decode_rows_out(o, batch, seq):
    o6 = o.reshape(batch, NSA_KV_HEADS, NSA_GROUP, seq, NSA_KV_HEADS, HEAD_DIM)
    o5 = jnp.stack([o6[:, kh, :, :, kh, :] for kh in range(NSA_KV_HEADS)], axis=1)
    return o5.transpose(0, 3, 1, 2, 4).reshape(batch * seq, NSA_WIDTH)


def _sample_layer(x, pool_cmp, pool_sel, win_buf, m_conv, m_c, m_n, m_m, f_buf, page_table, wts):
    batch, seq, _ = x.shape
    n_pages = page_table.shape[1]
    past = n_pages * PAGE_SIZE
    assert past % SEL_BLOCK == 0 and seq <= SEL_BLOCK and seq < CMP_BLOCK
    x2d = x.reshape(batch * seq, D_MODEL)
    q, _, _, mu, mv, mo, gt, _, _, kvc_t, kvs_t, kvw_t, _, _ = _in_proj(
        x2d, wts["g_mix"], wts["w_in_packed"], batch=1, seq=batch * seq, tm=batch * seq)
    per_batch = lambda a: a.reshape(2 * KV_WIDTH, batch, seq).transpose(1, 0, 2)
    kvc_t, kvs_t, kvw_t = per_batch(kvc_t), per_batch(kvs_t), per_batch(kvw_t)
    pad_keys = lambda a: jnp.pad(a, ((0, 0), (0, 0), (0, LANES - seq)))
    H = MLSTM_HEADS
    o_m, mconv, c_new, n_new, m_new = _mlstm(
        mu, mv, mo, gt, m_conv, m_c, m_n, m_m,
        wts["w_mconv"], wts["b_mconv"], wts["w_mq"], wts["w_mk"], wts["b_ig"], wts["b_fg"],
        wts["g_mhead"], wts["m_skip"], batch=batch, seq=seq)
    pool_cmp3, pool_sel3 = _kv_feature_major(pool_cmp), _kv_feature_major(pool_sel)
    kce, kco = _compress_paged(pool_cmp3, page_table, wts["cw"], wts["cw_pages"])
    n_past_blk = past // SEL_BLOCK
    n_sel = -(-(past + seq) // SEL_BLOCK)
    o_cmp, scores_t = _cmp_attn(q, kce, kco, batch=batch, seq=seq, tq=seq, pos0=past)
    ns = scores_t.shape[2]
    nsw = ns + LANES
    scores_all = scores_t.transpose(1, 2, 0, 3).reshape(NSA_KV_HEADS, ns, batch * seq)
    pos_all = (past + jnp.arange(batch * seq, dtype=jnp.int32) % seq).reshape(1, batch * seq)
    selb = _topk_blocks(scores_all, pos_all, n_sel=n_sel, nsw=nsw, tt=batch * seq)
    selb = selb.reshape(NSA_KV_HEADS, nsw, batch, seq).transpose(2, 0, 3, 1)
    qa = _decode_rows(q, batch, seq)
    rows = qa.shape[1]
    blk_per_step = ATTN_PAGES_PER_STEP * PAGE_SIZE // SEL_BLOCK
    n_steps = n_pages // ATTN_PAGES_PER_STEP
    sb_rows = jnp.broadcast_to(selb[:, :, None], (batch, NSA_KV_HEADS, NSA_GROUP, seq, selb.shape[-1]))
    sb_rows = sb_rows.reshape(batch, rows, selb.shape[-1])
    bias_q = sb_rows[:, :, :n_past_blk].reshape(batch, rows, n_steps, blk_per_step).transpose(0, 2, 1, 3)
    bias_q = jnp.pad(bias_q, ((0, 0), (0, 0), (0, 0), (0, LANES - blk_per_step)))
    bias_new = jnp.broadcast_to(sb_rows[:, :, n_past_blk:n_past_blk + 1], (batch, rows, LANES))
    o_sel = _attn_paged(qa, bias_q, bias_new, pad_keys(kvs_t), pool_sel3, page_table, n_new=seq)
    n_buf = win_buf.shape[1]
    assert past >= n_buf
    win_t = _kv_feature_major(win_buf)
    o_win = _attn_window_small(qa, win_t, pad_keys(kvw_t), n_new=seq, past=past)
    win_new = jnp.concatenate([win_t, kvw_t], axis=2)[:, :, seq:]
    tmaj = lambda a: a.reshape(batch, seq, -1).transpose(1, 0, 2).reshape(batch * seq, -1)
    fb_t = f_buf.transpose(1, 0, 2).reshape(1, (FFN_CONV - 1) * batch, 2 * D_FF)
    y, f_new = _ffn(tmaj(x2d), tmaj(o_m), tmaj(o_cmp), tmaj(_decode_rows_out(o_sel, batch, seq)),
                    tmaj(_decode_rows_out(o_win, batch, seq)), tmaj(gt), fb_t,
                    wts["w_out"], wts["g_nsa"], wts["g_ffn"], wts["g_final"], wts["w_up"],
                    wts["w_fconv"], wts["w_down"], nb=1, tm=batch * seq, stride=batch)
    y = y.reshape(seq, batch, D_MODEL).transpose(1, 0, 2)
    f_new = f_new.reshape(FFN_CONV - 1, batch, 2 * D_FF).transpose(1, 0, 2)
    return (y, _kv_rows(kvc_t), _kv_rows(kvs_t), _kv_rows(win_new), mconv, c_new, n_new,
            m_new.reshape(batch, H), f_new)


def kernel(x_prompt, x_sample, cache_cmp, cache_sel, state_win, state_mlstm_C, state_mlstm_n,
           state_mlstm_m, state_mlstm_conv, state_ffn_conv, page_table,
           g_mix, w_in, w_out, w_mconv, b_mconv, w_mq, w_mk, b_ig, b_fg, g_mhead, m_skip,
           pe_cmp, w_cmp1, w_cmp2, g_nsa, g_ffn, w_up, w_fconv, w_down, g_final):
    assert w_in.shape[0] == 1, "one layer: the final norm is fused into the layer's FFN kernel"
    l = 0
    wts = dict(g_mix=g_mix[l], w_in_packed=_pack_w_in(w_in[l]), w_out=w_out[l], w_mconv=w_mconv[l],
               b_mconv=b_mconv[l], w_mq=w_mq[l], w_mk=w_mk[l], b_ig=b_ig[l], b_fg=b_fg[l],
               g_mhead=g_mhead[l], m_skip=m_skip[l],
               cw=_pack_compress_weights(pe_cmp[l], w_cmp1[l], w_cmp2[l]),
               cw_pages=_page_pair_constants(pe_cmp[l]),
               g_nsa=g_nsa[l], g_ffn=g_ffn[l], g_final=g_final, w_up=w_up[l], w_fconv=w_fconv[l],
               w_down=w_down[l])
    p = _prompt_layer(x_prompt, wts)
    s = _sample_layer(x_sample, cache_cmp[l], cache_sel[l], state_win[l], state_mlstm_conv[l],
                      state_mlstm_C[l], state_mlstm_n[l], state_mlstm_m[l], state_ffn_conv[l],
                      page_table, wts)
    yp, cmp_p, sel_p, win_p, mconv_p, c_p, n_p, m_p, fconv_p = p
    ys, cmp_s, sel_s, win_s, mconv_s, c_s, n_s, m_s, fconv_s = s
    st = lambda a: a[None]
    return (yp, ys, st(cmp_p), st(cmp_s), st(sel_p), st(sel_s), st(win_p), st(win_s),
            st(c_p), st(c_s), st(n_p), st(n_s), st(m_p), st(m_s), st(mconv_p), st(mconv_s),
            st(fconv_p), st(fconv_s))
```
